```python
import math
import jax, jax.numpy as jnp
from jax import lax
import numpy as np

D_MODEL = 1024
BATCH = 16
SEQ = 2048
DEPTH = 2

GRID_W = 64
LRU_WIDTH = D_MODEL
LRU_BLOCKS = 8
LRU_BW = LRU_WIDTH // LRU_BLOCKS
CONV_W = 4
CONV_PAD = (2, 1)
RG_C = 8.0
HEAD_DIM = 128
N_HEADS = D_MODEL // HEAD_DIM
N_KV_HEADS = 2
GROUP = N_HEADS // N_KV_HEADS
Q_BLOCK = 128
ROPE_THETA = 10000.0
AXIS_FREQS = HEAD_DIM // 4
D_FF = 4 * D_MODEL
EPS = 1e-6
N_RG = (DEPTH + 1) // 2
N_AT = DEPTH // 2

kernel_name = "hybrid_rglru_axial_gqa_encoder"


def rms_norm(x, g):
    xf = x.astype(jnp.float32)
    y = xf * lax.rsqrt(jnp.mean(xf * xf, axis=-1, keepdims=True) + EPS)
    return (y * g.astype(jnp.float32)).astype(x.dtype)


def rglru_direction(x, w_a, b_a, w_x, b_x, lam, reverse):
    B, L, C = x.shape
    xb = x.reshape(B, L, LRU_BLOCKS, LRU_BW)
    r = jax.nn.sigmoid((jnp.einsum('blhi,hij->blhj', xb, w_a).reshape(B, L, C) + b_a).astype(jnp.float32))
    i = jax.nn.sigmoid((jnp.einsum('blhi,hij->blhj', xb, w_x).reshape(B, L, C) + b_x).astype(jnp.float32))
    log_a = -RG_C * r * jax.nn.softplus(-lam.astype(jnp.float32))
    a = jnp.exp(log_a)
    mult = jnp.sqrt(-jnp.expm1(2.0 * log_a))
    u = mult * (i * x.astype(jnp.float32))

    def combine(p, q):
        a1, b1 = p
        a2, b2 = q
        return a1 * a2, a2 * b1 + b2

    _, h = lax.associative_scan(combine, (a, u), axis=1, reverse=reverse)
    return h.astype(x.dtype)


def rglru_block(h, w_in, conv_w, conv_b, w_a, b_a, w_x, b_x, lam, w_out):
    z = h @ w_in
    gate, rec = jnp.split(z, 2, axis=-1)
    gate = jax.nn.gelu(gate)
    rec = lax.conv_general_dilated(rec, conv_w, window_strides=(1,), padding=[CONV_PAD],
                                   dimension_numbers=('NWC', 'WIO', 'NWC'),
                                   feature_group_count=LRU_WIDTH) + conv_b
    y = (rglru_direction(rec, w_a[0], b_a[0], w_x[0], b_x[0], lam[0], False)
         + rglru_direction(rec, w_a[1], b_a[1], w_x[1], b_x[1], lam[1], True))
    return (y * gate) @ w_out


def axial_rope_tables(L):
    rows = L // GRID_W
    row = jnp.repeat(jnp.arange(rows, dtype=jnp.float32), GRID_W)
    col = jnp.tile(jnp.arange(GRID_W, dtype=jnp.float32), rows)
    inv = ROPE_THETA ** (-jnp.arange(AXIS_FREQS, dtype=jnp.float32) / AXIS_FREQS)
    ang_r = row[:, None] * inv
    ang_c = col[:, None] * inv
    return (jnp.cos(ang_r)[:, None, :], jnp.sin(ang_r)[:, None, :],
            jnp.cos(ang_c)[:, None, :], jnp.sin(ang_c)[:, None, :])


def rope_half(x, cos, sin):
    x1, x2 = jnp.split(x, 2, axis=-1)
    return jnp.concatenate([x1 * cos - x2 * sin, x2 * cos + x1 * sin], axis=-1)


def apply_axial_rope(x, tabs):
    cr, sr, cc, sc = tabs
    xf = x.astype(jnp.float32)
    xr, xc = jnp.split(xf, 2, axis=-1)
    return jnp.concatenate([rope_half(xr, cr, sr), rope_half(xc, cc, sc)], axis=-1).astype(x.dtype)


def attention_block(h, w_qkv, q_g, k_g, w_o):
    B, L, _ = h.shape
    qkv = h @ w_qkv
    q, k, v = jnp.split(qkv, [N_HEADS * HEAD_DIM, (N_HEADS + N_KV_HEADS) * HEAD_DIM], axis=-1)
    q = rms_norm(q.reshape(B, L, N_HEADS, HEAD_DIM), q_g)
    k = rms_norm(k.reshape(B, L, N_KV_HEADS, HEAD_DIM), k_g)
    v = v.reshape(B, L, N_KV_HEADS, HEAD_DIM)
    tabs = axial_rope_tables(L)
    q = apply_axial_rope(q, tabs)
    k = apply_axial_rope(k, tabs)
    nb = L // Q_BLOCK
    qb = q.reshape(B, nb, Q_BLOCK, N_KV_HEADS, GROUP, HEAD_DIM).transpose(1, 0, 2, 3, 4, 5)
    scale = 1.0 / math.sqrt(HEAD_DIM)

    def attend(qblk):
        s = jnp.einsum('bqkgd,bskd->bkgqs', qblk, k).astype(jnp.float32) * scale
        p = jax.nn.softmax(s, axis=-1).astype(v.dtype)
        return jnp.einsum('bkgqs,bskd->bqkgd', p, v)

    o = lax.map(attend, qb)
    o = o.transpose(1, 0, 2, 3, 4, 5).reshape(B, L, N_HEADS * HEAD_DIM)
    return o @ w_o


def sq_relu_mlp(h, w_up, w_down):
    u = jax.nn.relu(h @ w_up)
    return (u * u) @ w_down


def _fwd_setup_inputs(seed: int = 0) -> dict:
    key = jax.random.key(seed)
    ks = jax.random.split(key, 24)
    f32 = jnp.float32
    nrm = lambda k, shape, fan_in: jax.random.normal(k, shape, f32) * (fan_in ** -0.5)
    gain = lambda k, shape: 1.0 + 0.02 * jax.random.normal(k, shape, f32)
    small = lambda k, shape: 0.01 * jax.random.normal(k, shape, f32)
    u = jax.random.uniform(ks[10], (N_RG, 2, LRU_WIDTH), f32, 0.9, 0.999)
    s = u ** (1.0 / RG_C)
    lam = jnp.log(s) - jnp.log1p(-s)
    return {
        "x": jax.random.normal(ks[0], (BATCH, SEQ, D_MODEL), f32),
        "norm_mix_g": gain(ks[1], (DEPTH, D_MODEL)),
        "norm_mlp_g": gain(ks[2], (DEPTH, D_MODEL)),
        "rg_w_in": nrm(ks[3], (N_RG, D_MODEL, 2 * LRU_WIDTH), D_MODEL),
        "rg_conv_w": nrm(ks[4], (N_RG, CONV_W, 1, LRU_WIDTH), CONV_W),
        "rg_conv_b": small(ks[5], (N_RG, LRU_WIDTH)),
        "rg_w_a": nrm(ks[6], (N_RG, 2, LRU_BLOCKS, LRU_BW, LRU_BW), LRU_BW),
        "rg_b_a": small(ks[7], (N_RG, 2, LRU_WIDTH)),
        "rg_w_x": nrm(ks[8], (N_RG, 2, LRU_BLOCKS, LRU_BW, LRU_BW), LRU_BW),
        "rg_b_x": small(ks[9], (N_RG, 2, LRU_WIDTH)),
        "rg_lam": lam,
        "rg_w_out": nrm(ks[11], (N_RG, LRU_WIDTH, D_MODEL), LRU_WIDTH),
        "at_w_qkv": nrm(ks[12], (N_AT, D_MODEL, (N_HEADS + 2 * N_KV_HEADS) * HEAD_DIM), D_MODEL),
        "at_q_g": gain(ks[13], (N_AT, HEAD_DIM)),
        "at_k_g": gain(ks[14], (N_AT, HEAD_DIM)),
        "at_w_o": nrm(ks[15], (N_AT, N_HEADS * HEAD_DIM, D_MODEL), N_HEADS * HEAD_DIM),
        "mlp_w_up": nrm(ks[16], (DEPTH, D_MODEL, D_FF), D_MODEL),
        "mlp_w_down": nrm(ks[17], (DEPTH, D_FF, D_MODEL), D_FF),
        "final_g": gain(ks[18], (D_MODEL,)),
    }


def _fwd_reference(x, norm_mix_g, norm_mlp_g, rg_w_in, rg_conv_w, rg_conv_b, rg_w_a, rg_b_a,
              rg_w_x, rg_b_x, rg_lam, rg_w_out, at_w_qkv, at_q_g, at_k_g, at_w_o,
              mlp_w_up, mlp_w_down, final_g):
    for i in range(DEPTH):
        h = rms_norm(x, norm_mix_g[i])
        j = i // 2
        if i % 2 == 0:
            mix = rglru_block(h, rg_w_in[j], rg_conv_w[j], rg_conv_b[j], rg_w_a[j], rg_b_a[j],
                              rg_w_x[j], rg_b_x[j], rg_lam[j], rg_w_out[j])
        else:
            mix = attention_block(h, at_w_qkv[j], at_q_g[j], at_k_g[j], at_w_o[j])
        x = x + mix
        x = x + sq_relu_mlp(rms_norm(x, norm_mlp_g[i]), mlp_w_up[i], mlp_w_down[i])
    return rms_norm(x, final_g)


import jax as _jax
import jax.numpy as _jnp

TWIN_FORMAT = 'train_step'
FWD_PARAMS = ['x', 'norm_mix_g', 'norm_mlp_g', 'rg_w_in', 'rg_conv_w', 'rg_conv_b', 'rg_w_a', 'rg_b_a', 'rg_w_x', 'rg_b_x', 'rg_lam', 'rg_w_out', 'at_w_qkv', 'at_q_g', 'at_k_g', 'at_w_o', 'mlp_w_up', 'mlp_w_down', 'final_g']
TWIN_WEIGHTS = ['norm_mix_g', 'norm_mlp_g', 'rg_w_in', 'rg_conv_w', 'rg_conv_b', 'rg_w_a', 'rg_b_a', 'rg_w_x', 'rg_b_x', 'rg_lam', 'rg_w_out', 'at_w_qkv', 'at_q_g', 'at_k_g', 'at_w_o', 'mlp_w_up', 'mlp_w_down', 'final_g']
TWIN_DIFF_INPUT = 'x'
TWIN_INPUTS = ['x', 'norm_mix_g', 'norm_mlp_g', 'rg_w_in', 'rg_conv_w', 'rg_conv_b', 'rg_w_a', 'rg_b_a', 'rg_w_x', 'rg_b_x', 'rg_lam', 'rg_w_out', 'at_w_qkv', 'at_q_g', 'at_k_g', 'at_w_o', 'mlp_w_up', 'mlp_w_down', 'final_g', 'loss_target', 'm_norm_mix_g', 'm_norm_mlp_g', 'm_rg_w_in', 'm_rg_conv_w', 'm_rg_conv_b', 'm_rg_w_a', 'm_rg_b_a', 'm_rg_w_x', 'm_rg_b_x', 'm_rg_lam', 'm_rg_w_out', 'm_at_w_qkv', 'm_at_q_g', 'm_at_k_g', 'm_at_w_o', 'm_mlp_w_up', 'm_mlp_w_down', 'm_final_g', 'v_norm_mix_g', 'v_norm_mlp_g', 'v_rg_w_in', 'v_rg_conv_w', 'v_rg_conv_b', 'v_rg_w_a', 'v_rg_b_a', 'v_rg_w_x', 'v_rg_b_x', 'v_rg_lam', 'v_rg_w_out', 'v_at_w_qkv', 'v_at_q_g', 'v_at_k_g', 'v_at_w_o', 'v_mlp_w_up', 'v_mlp_w_down', 'v_final_g']
TWIN_OUTPUTS = ['loss', 'grad_x', 'grad_norm_mix_g', 'grad_norm_mlp_g', 'grad_rg_w_in', 'grad_rg_conv_w', 'grad_rg_conv_b', 'grad_rg_w_a', 'grad_rg_b_a', 'grad_rg_w_x', 'grad_rg_b_x', 'grad_rg_lam', 'grad_rg_w_out', 'grad_at_w_qkv', 'grad_at_q_g', 'grad_at_k_g', 'grad_at_w_o', 'grad_mlp_w_up', 'grad_mlp_w_down', 'grad_final_g', 'delta_norm_mix_g', 'delta_norm_mlp_g', 'delta_rg_w_in', 'delta_rg_conv_w', 'delta_rg_conv_b', 'delta_rg_w_a', 'delta_rg_b_a', 'delta_rg_w_x', 'delta_rg_b_x', 'delta_rg_lam', 'delta_rg_w_out', 'delta_at_w_qkv', 'delta_at_q_g', 'delta_at_k_g', 'delta_at_w_o', 'delta_mlp_w_up', 'delta_mlp_w_down', 'delta_final_g', 'new_m_norm_mix_g', 'new_m_norm_mlp_g', 'new_m_rg_w_in', 'new_m_rg_conv_w', 'new_m_rg_conv_b', 'new_m_rg_w_a', 'new_m_rg_b_a', 'new_m_rg_w_x', 'new_m_rg_b_x', 'new_m_rg_lam', 'new_m_rg_w_out', 'new_m_at_w_qkv', 'new_m_at_q_g', 'new_m_at_k_g', 'new_m_at_w_o', 'new_m_mlp_w_up', 'new_m_mlp_w_down', 'new_m_final_g', 'new_v_norm_mix_g', 'new_v_norm_mlp_g', 'new_v_rg_w_in', 'new_v_rg_conv_w', 'new_v_rg_conv_b', 'new_v_rg_w_a', 'new_v_rg_b_a', 'new_v_rg_w_x', 'new_v_rg_b_x', 'new_v_rg_lam', 'new_v_rg_w_out', 'new_v_at_w_qkv', 'new_v_at_q_g', 'new_v_at_k_g', 'new_v_at_w_o', 'new_v_mlp_w_up', 'new_v_mlp_w_down', 'new_v_final_g']
TWIN_LEAF_KINDS = {'loss': 'loss', 'grad_x': 'grad_x', 'grad_norm_mix_g': 'grad_w', 'grad_norm_mlp_g': 'grad_w', 'grad_rg_w_in': 'grad_w', 'grad_rg_conv_w': 'grad_w', 'grad_rg_conv_b': 'grad_w', 'grad_rg_w_a': 'grad_w', 'grad_rg_b_a': 'grad_w', 'grad_rg_w_x': 'grad_w', 'grad_rg_b_x': 'grad_w', 'grad_rg_lam': 'grad_w', 'grad_rg_w_out': 'grad_w', 'grad_at_w_qkv': 'grad_w', 'grad_at_q_g': 'grad_w', 'grad_at_k_g': 'grad_w', 'grad_at_w_o': 'grad_w', 'grad_mlp_w_up': 'grad_w', 'grad_mlp_w_down': 'grad_w', 'grad_final_g': 'grad_w', 'delta_norm_mix_g': 'delta_w', 'delta_norm_mlp_g': 'delta_w', 'delta_rg_w_in': 'delta_w', 'delta_rg_conv_w': 'delta_w', 'delta_rg_conv_b': 'delta_w', 'delta_rg_w_a': 'delta_w', 'delta_rg_b_a': 'delta_w', 'delta_rg_w_x': 'delta_w', 'delta_rg_b_x': 'delta_w', 'delta_rg_lam': 'delta_w', 'delta_rg_w_out': 'delta_w', 'delta_at_w_qkv': 'delta_w', 'delta_at_q_g': 'delta_w', 'delta_at_k_g': 'delta_w', 'delta_at_w_o': 'delta_w', 'delta_mlp_w_up': 'delta_w', 'delta_mlp_w_down': 'delta_w', 'delta_final_g': 'delta_w', 'new_m_norm_mix_g': 'new_m', 'new_m_norm_mlp_g': 'new_m', 'new_m_rg_w_in': 'new_m', 'new_m_rg_conv_w': 'new_m', 'new_m_rg_conv_b': 'new_m', 'new_m_rg_w_a': 'new_m', 'new_m_rg_b_a': 'new_m', 'new_m_rg_w_x': 'new_m', 'new_m_rg_b_x': 'new_m', 'new_m_rg_lam': 'new_m', 'new_m_rg_w_out': 'new_m', 'new_m_at_w_qkv': 'new_m', 'new_m_at_q_g': 'new_m', 'new_m_at_k_g': 'new_m', 'new_m_at_w_o': 'new_m', 'new_m_mlp_w_up': 'new_m', 'new_m_mlp_w_down': 'new_m', 'new_m_final_g': 'new_m', 'new_v_norm_mix_g': 'new_v', 'new_v_norm_mlp_g': 'new_v', 'new_v_rg_w_in': 'new_v', 'new_v_rg_conv_w': 'new_v', 'new_v_rg_conv_b': 'new_v', 'new_v_rg_w_a': 'new_v', 'new_v_rg_b_a': 'new_v', 'new_v_rg_w_x': 'new_v', 'new_v_rg_b_x': 'new_v', 'new_v_rg_lam': 'new_v', 'new_v_rg_w_out': 'new_v', 'new_v_at_w_qkv': 'new_v', 'new_v_at_q_g': 'new_v', 'new_v_at_k_g': 'new_v', 'new_v_at_w_o': 'new_v', 'new_v_mlp_w_up': 'new_v', 'new_v_mlp_w_down': 'new_v', 'new_v_final_g': 'new_v'}


def _forward(args):
    return _fwd_reference(*[args[k] for k in FWD_PARAMS])


def _output_shape():
    out = _jax.eval_shape(lambda: _forward(_fwd_setup_inputs(0)))
    return out.shape, out.dtype

N_MICROBATCH = 1
ADAM_LR = 0.001
ADAM_B1 = 0.9
ADAM_B2 = 0.999
ADAM_EPS = 1e-08
ADAM_WD = 0.01
ADAM_STEP = 10
PER_EXAMPLE_BATCH_AXIS = {'x': 0, 'loss_target': 0}
SHARED_INPUTS = []
_WEIGHT_DTYPES = {'norm_mix_g': _jnp.float32, 'norm_mlp_g': _jnp.float32, 'rg_w_in': _jnp.float32, 'rg_conv_w': _jnp.float32, 'rg_conv_b': _jnp.float32, 'rg_w_a': _jnp.float32, 'rg_b_a': _jnp.float32, 'rg_w_x': _jnp.float32, 'rg_b_x': _jnp.float32, 'rg_lam': _jnp.float32, 'rg_w_out': _jnp.float32, 'at_w_qkv': _jnp.float32, 'at_q_g': _jnp.float32, 'at_k_g': _jnp.float32, 'at_w_o': _jnp.float32, 'mlp_w_up': _jnp.float32, 'mlp_w_down': _jnp.float32, 'final_g': _jnp.float32}
MOMENT_SCALE = {'norm_mix_g': 1.176897e-01, 'norm_mlp_g': 1.380896e-01, 'rg_w_in': 1.079059e-01, 'rg_conv_w': 1.091893e-01, 'rg_conv_b': 1.862660e+00, 'rg_w_a': 2.518099e-02, 'rg_b_a': 1.957429e-02, 'rg_w_x': 4.568427e-02, 'rg_b_x': 2.625747e-02, 'rg_lam': 3.709754e-02, 'rg_w_out': 1.135341e-01, 'at_w_qkv': 3.838714e-02, 'at_q_g': 2.653380e-02, 'at_k_g': 2.701262e-02, 'at_w_o': 4.289593e-02, 'mlp_w_up': 7.028382e-02, 'mlp_w_down': 1.404269e-01, 'final_g': 3.268880e+01}


def _to_microbatches(a, axis):
    t = _jnp.moveaxis(a, axis, 0)
    t = t.reshape((N_MICROBATCH, t.shape[0] // N_MICROBATCH) + t.shape[1:])
    return _jnp.moveaxis(t, 1, axis + 1)


def setup_inputs(seed: int = 0) -> dict:
    inp = _fwd_setup_inputs(seed)
    key = _jax.random.fold_in(_jax.random.key(seed), 7919)
    shape, _ = _output_shape()
    out = dict(inp)
    out["loss_target"] = _jax.random.normal(_jax.random.fold_in(key, 0), shape, _jnp.float32)
    for i, name in enumerate(TWIN_WEIGHTS):
        w = inp[name].astype(_jnp.float32)
        if MOMENT_SCALE is None:
            s = _jnp.sqrt(_jnp.mean(_jnp.square(w)) + 1e-30)
        else:
            s = MOMENT_SCALE[name]
        km, kv = _jax.random.split(_jax.random.fold_in(key, i + 1))
        out[name] = w
        out["m_" + name] = s * _jax.random.normal(km, w.shape, _jnp.float32)
        out["v_" + name] = (s * s) * _jax.random.uniform(kv, w.shape, _jnp.float32, 0.5, 1.5)
    if N_MICROBATCH > 1:
        for name, axis in PER_EXAMPLE_BATCH_AXIS.items():
            out[name] = _to_microbatches(out[name], axis)
    return {'x': out['x'], 'norm_mix_g': out['norm_mix_g'], 'norm_mlp_g': out['norm_mlp_g'], 'rg_w_in': out['rg_w_in'], 'rg_conv_w': out['rg_conv_w'], 'rg_conv_b': out['rg_conv_b'], 'rg_w_a': out['rg_w_a'], 'rg_b_a': out['rg_b_a'], 'rg_w_x': out['rg_w_x'], 'rg_b_x': out['rg_b_x'], 'rg_lam': out['rg_lam'], 'rg_w_out': out['rg_w_out'], 'at_w_qkv': out['at_w_qkv'], 'at_q_g': out['at_q_g'], 'at_k_g': out['at_k_g'], 'at_w_o': out['at_w_o'], 'mlp_w_up': out['mlp_w_up'], 'mlp_w_down': out['mlp_w_down'], 'final_g': out['final_g'], 'loss_target': out['loss_target'], 'm_norm_mix_g': out['m_norm_mix_g'], 'm_norm_mlp_g': out['m_norm_mlp_g'], 'm_rg_w_in': out['m_rg_w_in'], 'm_rg_conv_w': out['m_rg_conv_w'], 'm_rg_conv_b': out['m_rg_conv_b'], 'm_rg_w_a': out['m_rg_w_a'], 'm_rg_b_a': out['m_rg_b_a'], 'm_rg_w_x': out['m_rg_w_x'], 'm_rg_b_x': out['m_rg_b_x'], 'm_rg_lam': out['m_rg_lam'], 'm_rg_w_out': out['m_rg_w_out'], 'm_at_w_qkv': out['m_at_w_qkv'], 'm_at_q_g': out['m_at_q_g'], 'm_at_k_g': out['m_at_k_g'], 'm_at_w_o': out['m_at_w_o'], 'm_mlp_w_up': out['m_mlp_w_up'], 'm_mlp_w_down': out['m_mlp_w_down'], 'm_final_g': out['m_final_g'], 'v_norm_mix_g': out['v_norm_mix_g'], 'v_norm_mlp_g': out['v_norm_mlp_g'], 'v_rg_w_in': out['v_rg_w_in'], 'v_rg_conv_w': out['v_rg_conv_w'], 'v_rg_conv_b': out['v_rg_conv_b'], 'v_rg_w_a': out['v_rg_w_a'], 'v_rg_b_a': out['v_rg_b_a'], 'v_rg_w_x': out['v_rg_w_x'], 'v_rg_b_x': out['v_rg_b_x'], 'v_rg_lam': out['v_rg_lam'], 'v_rg_w_out': out['v_rg_w_out'], 'v_at_w_qkv': out['v_at_w_qkv'], 'v_at_q_g': out['v_at_q_g'], 'v_at_k_g': out['v_at_k_g'], 'v_at_w_o': out['v_at_w_o'], 'v_mlp_w_up': out['v_mlp_w_up'], 'v_mlp_w_down': out['v_mlp_w_down'], 'v_final_g': out['v_final_g']}


def _loss(weights, diff, rest, loss_target):
    with _jax.named_scope("forward"):
        args = {**rest, TWIN_DIFF_INPUT: diff, **{k: w.astype(_WEIGHT_DTYPES[k]) for k, w in weights.items()}}
        y = _forward(args)
    with _jax.named_scope("loss_head"):
        err = _jnp.square(y.astype(_jnp.float32) - loss_target)
        return 0.5 * _jnp.sum(_jnp.mean(err, axis=-1)) if err.ndim else 0.5 * err


def _adamw(w, g, m, v):
    m = ADAM_B1 * m + (1.0 - ADAM_B1) * g
    v = ADAM_B2 * v + (1.0 - ADAM_B2) * _jnp.square(g)
    m_hat = m / (1.0 - ADAM_B1 ** ADAM_STEP)
    v_hat = v / (1.0 - ADAM_B2 ** ADAM_STEP)
    delta = -ADAM_LR * (m_hat / (_jnp.sqrt(v_hat) + ADAM_EPS) + ADAM_WD * w)
    return delta, m, v


def reference(x, norm_mix_g, norm_mlp_g, rg_w_in, rg_conv_w, rg_conv_b, rg_w_a, rg_b_a, rg_w_x, rg_b_x, rg_lam, rg_w_out, at_w_qkv, at_q_g, at_k_g, at_w_o, mlp_w_up, mlp_w_down, final_g, loss_target, m_norm_mix_g, m_norm_mlp_g, m_rg_w_in, m_rg_conv_w, m_rg_conv_b, m_rg_w_a, m_rg_b_a, m_rg_w_x, m_rg_b_x, m_rg_lam, m_rg_w_out, m_at_w_qkv, m_at_q_g, m_at_k_g, m_at_w_o, m_mlp_w_up, m_mlp_w_down, m_final_g, v_norm_mix_g, v_norm_mlp_g, v_rg_w_in, v_rg_conv_w, v_rg_conv_b, v_rg_w_a, v_rg_b_a, v_rg_w_x, v_rg_b_x, v_rg_lam, v_rg_w_out, v_at_w_qkv, v_at_q_g, v_at_k_g, v_at_w_o, v_mlp_w_up, v_mlp_w_down, v_final_g):
    given = dict(x=x, norm_mix_g=norm_mix_g, norm_mlp_g=norm_mlp_g, rg_w_in=rg_w_in, rg_conv_w=rg_conv_w, rg_conv_b=rg_conv_b, rg_w_a=rg_w_a, rg_b_a=rg_b_a, rg_w_x=rg_w_x, rg_b_x=rg_b_x, rg_lam=rg_lam, rg_w_out=rg_w_out, at_w_qkv=at_w_qkv, at_q_g=at_q_g, at_k_g=at_k_g, at_w_o=at_w_o, mlp_w_up=mlp_w_up, mlp_w_down=mlp_w_down, final_g=final_g, loss_target=loss_target, m_norm_mix_g=m_norm_mix_g, m_norm_mlp_g=m_norm_mlp_g, m_rg_w_in=m_rg_w_in, m_rg_conv_w=m_rg_conv_w, m_rg_conv_b=m_rg_conv_b, m_rg_w_a=m_rg_w_a, m_rg_b_a=m_rg_b_a, m_rg_w_x=m_rg_w_x, m_rg_b_x=m_rg_b_x, m_rg_lam=m_rg_lam, m_rg_w_out=m_rg_w_out, m_at_w_qkv=m_at_w_qkv, m_at_q_g=m_at_q_g, m_at_k_g=m_at_k_g, m_at_w_o=m_at_w_o, m_mlp_w_up=m_mlp_w_up, m_mlp_w_down=m_mlp_w_down, m_final_g=m_final_g, v_norm_mix_g=v_norm_mix_g, v_norm_mlp_g=v_norm_mlp_g, v_rg_w_in=v_rg_w_in, v_rg_conv_w=v_rg_conv_w, v_rg_conv_b=v_rg_conv_b, v_rg_w_a=v_rg_w_a, v_rg_b_a=v_rg_b_a, v_rg_w_x=v_rg_w_x, v_rg_b_x=v_rg_b_x, v_rg_lam=v_rg_lam, v_rg_w_out=v_rg_w_out, v_at_w_qkv=v_at_w_qkv, v_at_q_g=v_at_q_g, v_at_k_g=v_at_k_g, v_at_w_o=v_at_w_o, v_mlp_w_up=v_mlp_w_up, v_mlp_w_down=v_mlp_w_down, v_final_g=v_final_g)
    weights = {n: given[n] for n in TWIN_WEIGHTS}
    shared = {n: given[n] for n in SHARED_INPUTS}
    per_example = {n: given[n] for n in ['x']}
    grad_fn = _jax.value_and_grad(_loss, argnums=(0, 1))

    def one_microbatch(ex, loss_target):
        ex = dict(ex)
        diff = ex.pop(TWIN_DIFF_INPUT)
        return grad_fn(weights, diff, {**shared, **ex}, loss_target)

    if N_MICROBATCH == 1:
        loss, (grad_w, grad_x) = one_microbatch(per_example, given["loss_target"])
    else:
        def body(carry, xs):
            loss_sum, grad_sum = carry
            l_k, (gw_k, gx_k) = one_microbatch(xs[0], xs[1])
            with _jax.named_scope("update"):
                return (loss_sum + l_k, _jax.tree.map(_jnp.add, grad_sum, gw_k)), gx_k

        init = (_jnp.zeros((), _jnp.float32), _jax.tree.map(_jnp.zeros_like, weights))
        (loss, grad_w), grad_x = _jax.lax.scan(body, init, (per_example, given["loss_target"]))
    with _jax.named_scope("update"):
        delta_w, new_m, new_v = {}, {}, {}
        for n in TWIN_WEIGHTS:
            delta_w[n], new_m[n], new_v[n] = _adamw(weights[n], grad_w[n], given["m_" + n], given["v_" + n])
    return (loss, grad_x, *[grad_w[n] for n in TWIN_WEIGHTS], *[delta_w[n] for n in TWIN_WEIGHTS],
            *[new_m[n] for n in TWIN_WEIGHTS], *[new_v[n] for n in TWIN_WEIGHTS])
```

```python
import functools
import math

import jax
import jax.numpy as jnp
from jax import lax
from jax.experimental import pallas as pl
from jax.experimental.pallas import tpu as pltpu

F32 = jnp.float32
BF16 = jnp.bfloat16
S = jax.ShapeDtypeStruct

EPS = 1e-6
HEAD_DIM = 128
N_KV = 2
GRID_W = 64
ROPE_THETA = 10000.0
LRU_BW = 128
RG_C = 8.0
CONV_W = 4
N_DEV = 8
N_SEG = 8
VMEM_LIMIT_V7X = 56 * 1024 * 1024
SOFTMAX_SCALE = 1.0 / math.sqrt(HEAD_DIM)
GELU_K = math.sqrt(2.0 / math.pi)
GELU_C = 0.044715

ADAM_LR = 0.001
ADAM_B1 = 0.9
ADAM_B2 = 0.999
ADAM_EPS = 1e-08
ADAM_WD = 0.01
ADAM_STEP = 10

NT = (((1,), (1,)), ((), ()))
TN = (((0,), (0,)), ((), ()))


def _cp(*sem):
    return pltpu.CompilerParams(dimension_semantics=sem, vmem_limit_bytes=VMEM_LIMIT_V7X)


def _rms_r(xv):
    return lax.rsqrt(jnp.mean(xv * xv, axis=-1, keepdims=True) + EPS)


def _rms_bwd(dh, xv, g):
    r = _rms_r(xv)
    xh = xv * r
    dg = jnp.sum(dh * xh, axis=0, keepdims=True)
    dxh = dh * g
    dx = r * (dxh - xh * jnp.mean(dxh * xh, axis=-1, keepdims=True))
    return dx, dg


def _dot(a, b):
    return jnp.dot(a, b, preferred_element_type=F32)


def _dot_nt(a, b):
    return lax.dot_general(a, b, NT, preferred_element_type=F32)


def _dot_tn(a, b):
    return lax.dot_general(a, b, TN, preferred_element_type=F32)


def _norm_matmul(x, g, wblk, name, out_dtype=F32):
    T, D = x.shape
    NB, _, nb = wblk.shape
    tm = min(T, 512)

    def body(x_ref, g_ref, w_ref, o_ref, h_ref):
        @pl.when(pl.program_id(1) == 0)
        def _():
            xv = x_ref[...]
            h_ref[...] = (xv * _rms_r(xv) * g_ref[...]).astype(BF16)

        o_ref[...] = _dot(h_ref[...], w_ref[...]).astype(o_ref.dtype)

    return pl.pallas_call(
        body, name=name, grid=(T // tm, NB),
        in_specs=[pl.BlockSpec((tm, D), lambda i, j: (i, 0)),
                  pl.BlockSpec((1, D), lambda i, j: (0, 0)),
                  pl.BlockSpec((None, D, nb), lambda i, j: (j, 0, 0))],
        out_specs=[pl.BlockSpec((tm, nb), lambda i, j: (i, j)),
                   pl.BlockSpec((tm, D), lambda i, j: (i, 0))],
        out_shape=[S((T, NB * nb), out_dtype), S((T, D), BF16)],
        compiler_params=_cp("parallel", "arbitrary"),
    )(x, g, wblk)


def _matmul_res(a, w, res, name):
    T, K = a.shape
    N = w.shape[1]
    tm = min(T, 512)

    def body(a_ref, w_ref, r_ref, o_ref):
        o_ref[...] = r_ref[...] + _dot(a_ref[...], w_ref[...])

    return pl.pallas_call(
        body, name=name, grid=(T // tm,),
        in_specs=[pl.BlockSpec((tm, K), lambda i: (i, 0)),
                  pl.BlockSpec((K, N), lambda i: (0, 0)),
                  pl.BlockSpec((tm, N), lambda i: (i, 0))],
        out_specs=pl.BlockSpec((tm, N), lambda i: (i, 0)),
        out_shape=S((T, N), F32),
        compiler_params=_cp("parallel"),
    )(a, w, res)


def _matmul_nt(a, w, name, out_dtype):
    T, N = a.shape
    K = w.shape[0]
    tm = min(T, 512)

    def body(a_ref, w_ref, o_ref, ab_ref):
        ab = a_ref[...].astype(BF16)
        ab_ref[...] = ab
        o_ref[...] = _dot_nt(ab, w_ref[...]).astype(o_ref.dtype)

    return pl.pallas_call(
        body, name=name, grid=(T // tm,),
        in_specs=[pl.BlockSpec((tm, N), lambda i: (i, 0)),
                  pl.BlockSpec((K, N), lambda i: (0, 0))],
        out_specs=[pl.BlockSpec((tm, K), lambda i: (i, 0)),
                   pl.BlockSpec((tm, N), lambda i: (i, 0))],
        out_shape=[S((T, K), out_dtype), S((T, N), BF16)],
        compiler_params=_cp("parallel"),
    )(a, w)


def _matmul_tn(a, b3, nb, name, blocked):
    T, M = a.shape
    SB, _, N = b3.shape
    per = N // nb
    NB = SB * per
    tk = min(T, 1024)
    nk = T // tk
    if blocked:
        out_spec, out_shape = pl.BlockSpec((None, M, nb), lambda j, k: (j, 0, 0)), S((NB, M, nb), BF16)
    else:
        assert SB == 1
        out_spec, out_shape = pl.BlockSpec((M, nb), lambda j, k: (0, j)), S((M, N), BF16)

    def body(a_ref, b_ref, o_ref, acc_ref):
        k = pl.program_id(1)

        @pl.when(k == 0)
        def _():
            acc_ref[...] = jnp.zeros_like(acc_ref)

        acc_ref[...] += _dot_tn(a_ref[...], b_ref[...])

        @pl.when(k == nk - 1)
        def _():
            o_ref[...] = acc_ref[...].astype(BF16)

    return pl.pallas_call(
        body, name=name, grid=(NB, nk),
        in_specs=[pl.BlockSpec((tk, M), lambda j, k: (k, 0)),
                  pl.BlockSpec((None, tk, nb), lambda j, k: (j // per, k, j % per))],
        out_specs=out_spec,
        out_shape=out_shape,
        scratch_shapes=[pltpu.VMEM((M, nb), F32)],
        compiler_params=_cp("parallel", "arbitrary"),
    )(a, b3)


def _nt_normbwd(dz3, wblk, x, g, dres, name):
    T, D = x.shape
    NB, _, nb = wblk.shape
    SB, _, N = dz3.shape
    per = N // nb
    tm = min(T, 512)

    def body(dz_ref, w_ref, x_ref, g_ref, dr_ref, dx_ref, dg_ref, acc_ref):
        i, j = pl.program_id(0), pl.program_id(1)

        @pl.when(j == 0)
        def _():
            acc_ref[...] = jnp.zeros_like(acc_ref)

        @pl.when((i == 0) & (j == 0))
        def _():
            dg_ref[...] = jnp.zeros_like(dg_ref)

        acc_ref[...] += _dot_nt(dz_ref[...], w_ref[...])

        @pl.when(j == NB - 1)
        def _():
            dx, dg = _rms_bwd(acc_ref[...], x_ref[...], g_ref[...])
            dx_ref[...] = dr_ref[...] + dx
            dg_ref[...] += dg

    return pl.pallas_call(
        body, name=name, grid=(T // tm, NB),
        in_specs=[pl.BlockSpec((None, tm, nb), lambda i, j: (j // per, i, j % per)),
                  pl.BlockSpec((None, D, nb), lambda i, j: (j, 0, 0)),
                  pl.BlockSpec((tm, D), lambda i, j: (i, 0)),
                  pl.BlockSpec((1, D), lambda i, j: (0, 0)),
                  pl.BlockSpec((tm, D), lambda i, j: (i, 0))],
        out_specs=[pl.BlockSpec((tm, D), lambda i, j: (i, 0)),
                   pl.BlockSpec((1, D), lambda i, j: (0, 0))],
        out_shape=[S((T, D), F32), S((1, D), F32)],
        scratch_shapes=[pltpu.VMEM((tm, D), F32)],
        compiler_params=_cp("arbitrary", "arbitrary"),
    )(dz3, wblk, x, g, dres)


def _mlp_fwd(x, g, wup, wdown, name):
    T, D = x.shape
    NB, _, fb = wup.shape
    tm = min(T, 1024)

    def body(x_ref, g_ref, wu_ref, wd_ref, xo_ref, a_ref, h_ref, acc_ref):
        j = pl.program_id(1)

        @pl.when(j == 0)
        def _():
            xv = x_ref[...]
            h_ref[...] = (xv * _rms_r(xv) * g_ref[...]).astype(BF16)
            acc_ref[...] = xv

        a = _dot(h_ref[...], wu_ref[...])
        a_ref[...] = a.astype(BF16)
        u = jnp.maximum(a, 0.0)
        acc_ref[...] += _dot((u * u).astype(BF16), wd_ref[...])

        @pl.when(j == NB - 1)
        def _():
            xo_ref[...] = acc_ref[...]

    return pl.pallas_call(
        body, name=name, grid=(T // tm, NB),
        in_specs=[pl.BlockSpec((tm, D), lambda i, j: (i, 0)),
                  pl.BlockSpec((1, D), lambda i, j: (0, 0)),
                  pl.BlockSpec((None, D, fb), lambda i, j: (j, 0, 0)),
                  pl.BlockSpec((fb, D), lambda i, j: (j, 0))],
        out_specs=[pl.BlockSpec((tm, D), lambda i, j: (i, 0)),
                   pl.BlockSpec((tm, fb), lambda i, j: (i, j)),
                   pl.BlockSpec((tm, D), lambda i, j: (i, 0))],
        out_shape=[S((T, D), F32), S((T, NB * fb), BF16), S((T, D), BF16)],
        scratch_shapes=[pltpu.VMEM((tm, D), F32)],
        compiler_params=_cp("parallel", "arbitrary"),
    )(x, g, wup, wdown)


def _mlp_bwd_dx(x, dout, a, g, wup, wdown, name):
    T, D = x.shape
    NB, _, fb = wup.shape
    tm = min(T, 512)

    def body(x_ref, do_ref, a_ref, g_ref, wu_ref, wd_ref, dx_ref, da_ref, dob_ref, dg_ref, acc_ref):
        i, j = pl.program_id(0), pl.program_id(1)

        @pl.when(j == 0)
        def _():
            dob_ref[...] = do_ref[...].astype(BF16)
            acc_ref[...] = jnp.zeros_like(acc_ref)

        @pl.when((i == 0) & (j == 0))
        def _():
            dg_ref[...] = jnp.zeros_like(dg_ref)

        du2 = _dot_nt(dob_ref[...], wd_ref[...])
        u = jnp.maximum(a_ref[...].astype(F32), 0.0)
        da = (du2 * (2.0 * u)).astype(BF16)
        da_ref[...] = da
        acc_ref[...] += _dot_nt(da, wu_ref[...])

        @pl.when(j == NB - 1)
        def _():
            dx, dg = _rms_bwd(acc_ref[...], x_ref[...], g_ref[...])
            dx_ref[...] = do_ref[...] + dx
            dg_ref[...] += dg

    return pl.pallas_call(
        body, name=name, grid=(T // tm, NB),
        in_specs=[pl.BlockSpec((tm, D), lambda i, j: (i, 0)),
                  pl.BlockSpec((tm, D), lambda i, j: (i, 0)),
                  pl.BlockSpec((tm, fb), lambda i, j: (i, j)),
                  pl.BlockSpec((1, D), lambda i, j: (0, 0)),
                  pl.BlockSpec((None, D, fb), lambda i, j: (j, 0, 0)),
                  pl.BlockSpec((fb, D), lambda i, j: (j, 0))],
        out_specs=[pl.BlockSpec((tm, D), lambda i, j: (i, 0)),
                   pl.BlockSpec((tm, fb), lambda i, j: (i, j)),
                   pl.BlockSpec((tm, D), lambda i, j: (i, 0)),
                   pl.BlockSpec((1, D), lambda i, j: (0, 0))],
        out_shape=[S((T, D), F32), S((T, NB * fb), BF16), S((T, D), BF16), S((1, D), F32)],
        scratch_shapes=[pltpu.VMEM((tm, D), F32)],
        compiler_params=_cp("arbitrary", "arbitrary"),
    )(x, dout, a, g, wup, wdown)


def _mlp_bwd_dw(h, da, a, dob, fb, name):
    T, D = h.shape
    F = a.shape[1]
    NB = F // fb
    tk = min(T, 1024)
    nk = T // tk

    def body(h_ref, da_ref, a_ref, dob_ref, dwu_ref, dwd_ref, au_ref, ad_ref):
        k = pl.program_id(1)

        @pl.when(k == 0)
        def _():
            au_ref[...] = jnp.zeros_like(au_ref)
            ad_ref[...] = jnp.zeros_like(ad_ref)

        au_ref[...] += _dot_tn(h_ref[...], da_ref[...])
        u = jnp.maximum(a_ref[...].astype(F32), 0.0)
        ad_ref[...] += _dot_tn((u * u).astype(BF16), dob_ref[...])

        @pl.when(k == nk - 1)
        def _():
            dwu_ref[...] = au_ref[...].astype(BF16)
            dwd_ref[...] = ad_ref[...].astype(BF16)

    return pl.pallas_call(
        body, name=name, grid=(NB, nk),
        in_specs=[pl.BlockSpec((tk, D), lambda j, k: (k, 0)),
                  pl.BlockSpec((tk, fb), lambda j, k: (k, j)),
                  pl.BlockSpec((tk, fb), lambda j, k: (k, j)),
                  pl.BlockSpec((tk, D), lambda j, k: (k, 0))],
        out_specs=[pl.BlockSpec((None, D, fb), lambda j, k: (j, 0, 0)),
                   pl.BlockSpec((fb, D), lambda j, k: (j, 0))],
        out_shape=[S((NB, D, fb), BF16), S((F, D), BF16)],
        scratch_shapes=[pltpu.VMEM((D, fb), F32), pltpu.VMEM((fb, D), F32)],
        compiler_params=_cp("parallel", "arbitrary"),
    )(h, da, a, dob)


def _final_loss(x, tgt, g, name):
    T, D = x.shape
    tm = min(T, 512)

    def body(x_ref, t_ref, g_ref, dx_ref, loss_ref, dg_ref):
        @pl.when(pl.program_id(0) == 0)
        def _():
            loss_ref[...] = jnp.zeros_like(loss_ref)
            dg_ref[...] = jnp.zeros_like(dg_ref)

        xv = x_ref[...]
        gv = g_ref[...]
        err = xv * _rms_r(xv) * gv - t_ref[...]
        e2 = jnp.sum(jnp.sum(err * err, axis=-1, keepdims=True), axis=0, keepdims=True)
        loss_ref[...] += (0.5 / D) * e2
        dx, dg = _rms_bwd(err * (1.0 / D), xv, gv)
        dx_ref[...] = dx
        dg_ref[...] += dg

    return pl.pallas_call(
        body, name=name, grid=(T // tm,),
        in_specs=[pl.BlockSpec((tm, D), lambda i: (i, 0)),
                  pl.BlockSpec((tm, D), lambda i: (i, 0)),
                  pl.BlockSpec((1, D), lambda i: (0, 0))],
        out_specs=[pl.BlockSpec((tm, D), lambda i: (i, 0)),
                   pl.BlockSpec((1, 128), lambda i: (0, 0)),
                   pl.BlockSpec((1, D), lambda i: (0, 0))],
        out_shape=[S((T, D), F32), S((1, 128), F32), S((1, D), F32)],
        compiler_params=_cp("arbitrary"),
    )(x, tgt, g)


def _rope_tables(L):
    nf = HEAD_DIM // 4
    t = jnp.arange(L, dtype=jnp.int32)
    row = (t // GRID_W).astype(F32)
    col = (t % GRID_W).astype(F32)
    inv = ROPE_THETA ** (-jnp.arange(nf, dtype=F32) / nf)
    ar = row[:, None] * inv
    ac = col[:, None] * inv
    cos = jnp.concatenate([jnp.cos(ar), jnp.cos(ar), jnp.cos(ac), jnp.cos(ac)], axis=-1)
    sin = jnp.concatenate([-jnp.sin(ar), jnp.sin(ar), -jnp.sin(ac), jnp.sin(ac)], axis=-1)
    return cos, sin


def _swap32(x):
    lane = lax.broadcasted_iota(jnp.int32, x.shape, 1)
    up = pltpu.roll(x, HEAD_DIM - 32, 1)
    down = pltpu.roll(x, 32, 1)
    return jnp.where((lane % 64) < 32, up, down)


def _qk_prep(qkv, qg, kg, cos, sin, L, name):
    T, W = qkv.shape
    nh = W // HEAD_DIM - 2 * N_KV
    tm = min(L, 512)
    lb = L // tm

    def body(qkv_ref, qg_ref, kg_ref, cos_ref, sin_ref, q_ref, k_ref, v_ref):
        c = cos_ref[...]
        s = sin_ref[...]
        for h in range(nh + N_KV):
            xh = qkv_ref[:, h * HEAD_DIM:(h + 1) * HEAD_DIM]
            gv = qg_ref[...] if h < nh else kg_ref[...]
            y = xh * _rms_r(xh) * gv
            y = (y * c + _swap32(y) * s).astype(BF16)
            if h < nh:
                q_ref[:, h * HEAD_DIM:(h + 1) * HEAD_DIM] = y
            else:
                k_ref[:, (h - nh) * HEAD_DIM:(h - nh + 1) * HEAD_DIM] = y
        v_ref[...] = qkv_ref[:, (nh + N_KV) * HEAD_DIM:].astype(BF16)

    return pl.pallas_call(
        body, name=name, grid=(T // tm,),
        in_specs=[pl.BlockSpec((tm, W), lambda i: (i, 0)),
                  pl.BlockSpec((1, HEAD_DIM), lambda i: (0, 0)),
                  pl.BlockSpec((1, HEAD_DIM), lambda i: (0, 0)),
                  pl.BlockSpec((tm, HEAD_DIM), lambda i: (i % lb, 0)),
                  pl.BlockSpec((tm, HEAD_DIM), lambda i: (i % lb, 0))],
        out_specs=[pl.BlockSpec((tm, nh * HEAD_DIM), lambda i: (i, 0)),
                   pl.BlockSpec((tm, N_KV * HEAD_DIM), lambda i: (i, 0)),
                   pl.BlockSpec((tm, N_KV * HEAD_DIM), lambda i: (i, 0))],
        out_shape=[S((T, nh * HEAD_DIM), BF16), S((T, N_KV * HEAD_DIM), BF16), S((T, N_KV * HEAD_DIM), BF16)],
        compiler_params=_cp("parallel"),
    )(qkv, qg, kg, cos, sin)


def _qk_prep_bwd(qkv, dq, dk, dv, qg, kg, cos, sin, L, name):
    T, W = qkv.shape
    nh = W // HEAD_DIM - 2 * N_KV
    tm = min(L, 512)
    lb = L // tm

    def body(qkv_ref, dq_ref, dk_ref, dv_ref, qg_ref, kg_ref, cos_ref, sin_ref, dz_ref, dqg_ref, dkg_ref):
        @pl.when(pl.program_id(0) == 0)
        def _():
            dqg_ref[...] = jnp.zeros_like(dqg_ref)
            dkg_ref[...] = jnp.zeros_like(dkg_ref)

        c = cos_ref[...]
        s = sin_ref[...]
        for h in range(nh + N_KV):
            cols = slice(h * HEAD_DIM, (h + 1) * HEAD_DIM)
            if h < nh:
                dout, gv, dg_ref = dq_ref[:, cols], qg_ref[...], dqg_ref
            else:
                kc = slice((h - nh) * HEAD_DIM, (h - nh + 1) * HEAD_DIM)
                dout, gv, dg_ref = dk_ref[:, kc], kg_ref[...], dkg_ref
            dy = dout * c - _swap32(dout) * s
            dx, dg = _rms_bwd(dy, qkv_ref[:, cols], gv)
            dg_ref[...] += dg
            dz_ref[:, cols] = dx.astype(BF16)
        dz_ref[:, (nh + N_KV) * HEAD_DIM:] = dv_ref[...].astype(BF16)

    return pl.pallas_call(
        body, name=name, grid=(T // tm,),
        in_specs=[pl.BlockSpec((tm, W), lambda i: (i, 0)),
                  pl.BlockSpec((tm, nh * HEAD_DIM), lambda i: (i, 0)),
                  pl.BlockSpec((tm, N_KV * HEAD_DIM), lambda i: (i, 0)),
                  pl.BlockSpec((tm, N_KV * HEAD_DIM), lambda i: (i, 0)),
                  pl.BlockSpec((1, HEAD_DIM), lambda i: (0, 0)),
                  pl.BlockSpec((1, HEAD_DIM), lambda i: (0, 0)),
                  pl.BlockSpec((tm, HEAD_DIM), lambda i: (i % lb, 0)),
                  pl.BlockSpec((tm, HEAD_DIM), lambda i: (i % lb, 0))],
        out_specs=[pl.BlockSpec((tm, W), lambda i: (i, 0)),
                   pl.BlockSpec((1, HEAD_DIM), lambda i: (0, 0)),
                   pl.BlockSpec((1, HEAD_DIM), lambda i: (0, 0))],
        out_shape=[S((T, W), BF16), S((1, HEAD_DIM), F32), S((1, HEAD_DIM), F32)],
        compiler_params=_cp("arbitrary"),
    )(qkv, dq, dk, dv, qg, kg, cos, sin)


def _softmax_rows(q, k):
    s = _dot_nt(q, k) * SOFTMAX_SCALE
    e = jnp.exp(s - jnp.max(s, axis=-1, keepdims=True))
    return e, jnp.sum(e, axis=-1, keepdims=True)


def _attn_fwd(q, k, v, L, name):
    T = q.shape[0]
    nh = q.shape[1] // HEAD_DIM
    G = nh // N_KV
    B = T // L
    tq = min(L, 512)
    nq = L // tq

    def body(q_ref, k_ref, v_ref, o_ref):
        e, l = _softmax_rows(q_ref[...], k_ref[...])
        o = _dot(e.astype(BF16), v_ref[...]) / l
        o_ref[...] = o.astype(BF16)

    qspec = pl.BlockSpec((tq, HEAD_DIM), lambda b, kv, g, qi: (b * nq + qi, kv * G + g))
    kspec = pl.BlockSpec((L, HEAD_DIM), lambda b, kv, g, qi: (b, kv))
    return pl.pallas_call(
        body, name=name, grid=(B, N_KV, G, nq),
        in_specs=[qspec, kspec, kspec],
        out_specs=qspec,
        out_shape=S((T, nh * HEAD_DIM), BF16),
        compiler_params=_cp("parallel", "parallel", "parallel", "parallel"),
    )(q, k, v)


def _attn_bwd(q, k, v, do, o, L, name):
    T = q.shape[0]
    nh = q.shape[1] // HEAD_DIM
    G = nh // N_KV
    B = T // L
    tq = min(L, 256)
    nq = L // tq

    def body(q_ref, k_ref, v_ref, do_ref, o_ref, dq_ref, dk_ref, dv_ref):
        @pl.when((pl.program_id(2) == 0) & (pl.program_id(3) == 0))
        def _():
            dk_ref[...] = jnp.zeros_like(dk_ref)
            dv_ref[...] = jnp.zeros_like(dv_ref)

        qv, kv_, vv, dov = q_ref[...], k_ref[...], v_ref[...], do_ref[...]
        e, l = _softmax_rows(qv, kv_)
        p = e / l
        dsum = jnp.sum(dov.astype(F32) * o_ref[...].astype(F32), axis=-1, keepdims=True)
        dp = _dot_nt(dov, vv)
        ds = (p * (dp - dsum) * SOFTMAX_SCALE).astype(BF16)
        dq_ref[...] = _dot(ds, kv_)
        dk_ref[...] += _dot_tn(ds, qv)
        dv_ref[...] += _dot_tn(p.astype(BF16), dov)

    qspec = pl.BlockSpec((tq, HEAD_DIM), lambda b, kv, g, qi: (b * nq + qi, kv * G + g))
    kspec = pl.BlockSpec((L, HEAD_DIM), lambda b, kv, g, qi: (b, kv))
    return pl.pallas_call(
        body, name=name, grid=(B, N_KV, G, nq),
        in_specs=[qspec, kspec, kspec, qspec, qspec],
        out_specs=[qspec, kspec, kspec],
        out_shape=[S((T, nh * HEAD_DIM), F32), S((T, N_KV * HEAD_DIM), F32), S((T, N_KV * HEAD_DIM), F32)],
        compiler_params=_cp("parallel", "parallel", "arbitrary", "arbitrary"),
    )(q, k, v, do, o)


PV_CONV_W = 0
PV_B_A = 4
PV_B_X = 6
PV_LAM = 8
PV_CONV_B = 10
PV_ROWS = 16


def _shift_rows(x, k):
    L = x.shape[0]
    if k == 0:
        return x
    t = lax.broadcasted_iota(jnp.int32, x.shape, 0)
    rolled = pltpu.roll(x, k % L, 0)
    keep = (t >= k) if k > 0 else (t < L + k)
    return jnp.where(keep, rolled, 0.0)


def _conv_taps(rec, pv):
    c = pv[PV_CONV_B:PV_CONV_B + 1]
    for j in range(CONV_W):
        c = c + pv[PV_CONV_W + j:PV_CONV_W + j + 1] * _shift_rows(rec, 2 - j)
    return c


def _sigmoid(x):
    return 1.0 / (1.0 + jnp.exp(-x))


def _neg_expm1(x):
    poly = -x * (1.0 + 0.5 * x * (1.0 + x * (1.0 / 3.0) * (1.0 + 0.25 * x * (1.0 + 0.2 * x))))
    return jnp.where(x > -0.1, poly, 1.0 - jnp.exp(x))


def _rg_gates(c, cbf, wa, wx, ba, bx, lam):
    r = _sigmoid(_dot(cbf, wa) + ba)
    i = _sigmoid(_dot(cbf, wx) + bx)
    sp = jnp.maximum(-lam, 0.0) + jnp.log1p(jnp.exp(-jnp.abs(lam)))
    la = (-RG_C) * r * sp
    a = jnp.exp(la)
    m = jnp.sqrt(_neg_expm1(2.0 * la))
    return r, i, a, m, sp


def _gelu(x):
    t = jnp.tanh(GELU_K * (x + GELU_C * x * x * x))
    return 0.5 * x * (1.0 + t), t


def _scan_pair(af_ref, uf_ref, ab_ref, ub_ref, hf_ref, hb_ref, pf_ref, pb_ref, L):
    ls = L // N_SEG
    zero = jnp.zeros((N_SEG, LRU_BW), F32)
    one = jnp.ones((N_SEG, LRU_BW), F32)

    def step(t, carry):
        hf, pf, hb, pb = carry
        rf = pl.ds(t, N_SEG, stride=ls)
        rb = pl.ds(ls - 1 - t, N_SEG, stride=ls)
        af = af_ref[rf, :]
        hf = af * hf + uf_ref[rf, :]
        pf = pf * af
        hf_ref[rf, :] = hf
        pf_ref[rf, :] = pf
        ab = ab_ref[rb, :]
        hb = ab * hb + ub_ref[rb, :]
        pb = pb * ab
        hb_ref[rb, :] = hb
        pb_ref[rb, :] = pb
        return hf, pf, hb, pb

    hf_e, pf_e, hb_e, pb_e = lax.fori_loop(0, ls, step, (zero, one, zero, one), unroll=8)

    cin = jnp.zeros((1, LRU_BW), F32)
    for s in range(N_SEG):
        rows = slice(s * ls, (s + 1) * ls)
        if s > 0:
            hf_ref[rows, :] = hf_ref[rows, :] + pf_ref[rows, :] * cin
        cin = hf_e[s:s + 1] + pf_e[s:s + 1] * cin
    cin = jnp.zeros((1, LRU_BW), F32)
    for s in reversed(range(N_SEG)):
        rows = slice(s * ls, (s + 1) * ls)
        if s < N_SEG - 1:
            hb_ref[rows, :] = hb_ref[rows, :] + pb_ref[rows, :] * cin
        cin = hb_e[s:s + 1] + pb_e[s:s + 1] * cin


def _rg_specs(L, D, nblk):
    slab = lambda off: pl.BlockSpec((L, LRU_BW), lambda cb, b: (b, off + cb))
    wspec = pl.BlockSpec((2, None, LRU_BW, LRU_BW), lambda cb, b: (0, cb, 0, 0))
    pvspec = pl.BlockSpec((PV_ROWS, LRU_BW), lambda cb, b: (0, cb))
    return slab, wspec, pvspec


def _rg_fwd(z, pvec, wa, wx, L, name):
    T, C2 = z.shape
    C = C2 // 2
    nblk = C // LRU_BW
    B = T // L
    slab, wspec, pvspec = _rg_specs(L, C, nblk)

    def body(gp_ref, rec_ref, pv_ref, wa_ref, wx_ref, yg_ref, hf_ref, hb_ref, a_scr, u_scr, p_scr):
        pv = pv_ref[...]
        c = _conv_taps(rec_ref[...], pv)
        cbf = c.astype(BF16)
        for d in range(2):
            _, i, a, m, _ = _rg_gates(c, cbf, wa_ref[d], wx_ref[d], pv[PV_B_A + d:PV_B_A + d + 1],
                                      pv[PV_B_X + d:PV_B_X + d + 1], pv[PV_LAM + d:PV_LAM + d + 1])
            a_scr[d] = a
            u_scr[d] = m * (i * c)
        _scan_pair(a_scr.at[0], u_scr.at[0], a_scr.at[1], u_scr.at[1], hf_ref, hb_ref, p_scr.at[0], p_scr.at[1], L)
        gate, _ = _gelu(gp_ref[...])
        yg_ref[...] = ((hf_ref[...] + hb_ref[...]) * gate).astype(BF16)

    return pl.pallas_call(
        body, name=name, grid=(nblk, B),
        in_specs=[slab(0), slab(nblk), pvspec, wspec, wspec],
        out_specs=[slab(0), slab(0), slab(0)],
        out_shape=[S((T, C), BF16), S((T, C), F32), S((T, C), F32)],
        scratch_shapes=[pltpu.VMEM((2, L, LRU_BW), F32)] * 3,
        compiler_params=_cp("parallel", "parallel"),
    )(z, z, pvec, wa, wx)


def _rg_bwd(z, hf, hb, dyg, pvec, wa, wx, L, name):
    T, C2 = z.shape
    C = C2 // 2
    nblk = C // LRU_BW
    B = T // L
    slab, wspec, pvspec = _rg_specs(L, C, nblk)

    def body(gp_ref, rec_ref, hf_ref, hb_ref, dyg_ref, pv_ref, wa_ref, wx_ref,
             dz_ref, dwa_ref, dwx_ref, dpv_ref, a_scr, u_scr, d_scr, p_scr):
        @pl.when(pl.program_id(1) == 0)
        def _():
            dwa_ref[...] = jnp.zeros_like(dwa_ref)
            dwx_ref[...] = jnp.zeros_like(dwx_ref)
            dpv_ref[...] = jnp.zeros_like(dpv_ref)

        pv = pv_ref[...]
        rec = rec_ref[...]
        c = _conv_taps(rec, pv)
        cbf = c.astype(BF16)
        gp = gp_ref[...]
        gate, th = _gelu(gp)
        dgelu = 0.5 * (1.0 + th) + 0.5 * gp * (1.0 - th * th) * GELU_K * (1.0 + 3.0 * GELU_C * gp * gp)
        dyg = dyg_ref[...]
        dz_ref[0] = (dyg * (hf_ref[...] + hb_ref[...]) * dgelu).astype(BF16)
        dy = dyg * gate

        gates = []
        for d in range(2):
            gates.append(_rg_gates(c, cbf, wa_ref[d], wx_ref[d], pv[PV_B_A + d:PV_B_A + d + 1],
                                   pv[PV_B_X + d:PV_B_X + d + 1], pv[PV_LAM + d:PV_LAM + d + 1]))
        a_scr[0] = _shift_rows(gates[1][2], 1)
        a_scr[1] = _shift_rows(gates[0][2], -1)
        u_scr[...] = dy
        _scan_pair(a_scr.at[0], u_scr, a_scr.at[1], u_scr, d_scr.at[1], d_scr.at[0], p_scr.at[0], p_scr.at[1], L)

        dc = jnp.zeros_like(c)
        rows = []
        for d in range(2):
            r, i, a, m, sp = gates[d]
            delta = d_scr[d]
            hnb = _shift_rows(hf_ref[...], 1) if d == 0 else _shift_rows(hb_ref[...], -1)
            da = delta * hnb
            dm = delta * (i * c)
            di = delta * (m * c)
            dc = dc + delta * (m * i)
            dla = da * a - dm * (a * a) / m
            dpa = (dla * ((-RG_C) * sp)) * (r * (1.0 - r))
            dpx = di * (i * (1.0 - i))
            dsp = jnp.sum(dla * ((-RG_C) * r), axis=0, keepdims=True)
            lam = pv[PV_LAM + d:PV_LAM + d + 1]
            rows.append((jnp.sum(dpa, axis=0, keepdims=True), jnp.sum(dpx, axis=0, keepdims=True),
                         -dsp * _sigmoid(-lam)))
            dpab = dpa.astype(BF16)
            dpxb = dpx.astype(BF16)
            dwa_ref[d] += _dot_tn(cbf, dpab)
            dwx_ref[d] += _dot_tn(cbf, dpxb)
            dc = dc + _dot_nt(dpab, wa_ref[d]) + _dot_nt(dpxb, wx_ref[d])

        drec = jnp.zeros_like(c)
        dcw = []
        for j in range(CONV_W):
            drec = drec + pv[PV_CONV_W + j:PV_CONV_W + j + 1] * _shift_rows(dc, j - 2)
            dcw.append(jnp.sum(dc * _shift_rows(rec, 2 - j), axis=0, keepdims=True))
        dz_ref[1] = drec.astype(BF16)
        zrow = jnp.zeros((1, LRU_BW), F32)
        dpv_ref[...] += jnp.concatenate(
            dcw + [rows[0][0], rows[1][0], rows[0][1], rows[1][1], rows[0][2], rows[1][2],
                   jnp.sum(dc, axis=0, keepdims=True)] + [zrow] * (PV_ROWS - PV_CONV_B - 1), axis=0)

    return pl.pallas_call(
        body, name=name, grid=(nblk, B),
        in_specs=[slab(0), slab(nblk), slab(0), slab(0), slab(0), pvspec, wspec, wspec],
        out_specs=[pl.BlockSpec((2, L, LRU_BW), lambda cb, b: (0, b, cb)), wspec, wspec, pvspec],
        out_shape=[S((2, T, C), BF16), S((2, nblk, LRU_BW, LRU_BW), F32), S((2, nblk, LRU_BW, LRU_BW), F32),
                   S((PV_ROWS, C), F32)],
        scratch_shapes=[pltpu.VMEM((2, L, LRU_BW), F32), pltpu.VMEM((L, LRU_BW), F32),
                        pltpu.VMEM((2, L, LRU_BW), F32), pltpu.VMEM((2, L, LRU_BW), F32)],
        compiler_params=_cp("parallel", "arbitrary"),
    )(z, z, hf, hb, dyg, pvec, wa, wx)


QKV_NB = 512


def _local_step(x3, tgt3, w):
    Bl, L, D = x3.shape
    T = Bl * L
    x = x3.reshape(T, D)
    tgt = tgt3.reshape(T, D)
    gm = [w["g_mix"][i:i + 1] for i in range(2)]
    gl = [w["g_mlp"][i:i + 1] for i in range(2)]
    fb = w["w_up"][0].shape[-1]
    nb_in = w["w_in"].shape[-1]

    z, h0 = _norm_matmul(x, gm[0], w["w_in"], "rg_in")
    yg, hf, hb = _rg_fwd(z, w["pvec"], w["wa"], w["wx"], L, "rg_fwd")
    x1 = _matmul_res(yg, w["w_out"], x, "rg_out")
    x2, a0, hm0 = _mlp_fwd(x1, gl[0], w["w_up"][0], w["w_down"][0], "mlp0_fwd")
    qkv, h1 = _norm_matmul(x2, gm[1], w["w_qkv"], "at_qkv")
    cos, sin = _rope_tables(L)
    qn, kn, vb = _qk_prep(qkv, w["qg"], w["kg"], cos, sin, L, "at_prep")
    o = _attn_fwd(qn, kn, vb, L, "at_fwd")
    x3_ = _matmul_res(o, w["w_o"], x2, "at_out")
    x4, a1, hm1 = _mlp_fwd(x3_, gl[1], w["w_up"][1], w["w_down"][1], "mlp1_fwd")
    dx4, loss, dgf = _final_loss(x4, tgt, w["g_fin"], "loss_head")

    dx3, da1, dob1, dgl1 = _mlp_bwd_dx(x3_, dx4, a1, gl[1], w["w_up"][1], w["w_down"][1], "mlp1_bwd_dx")
    dwu1, dwd1 = _mlp_bwd_dw(hm1, da1, a1, dob1, fb, "mlp1_bwd_dw")
    do, dx3b = _matmul_nt(dx3, w["w_o"], "at_out_bwd", BF16)
    dwo = _matmul_tn(o, dx3b[None], QKV_NB, "at_out_dw", blocked=False)
    dq, dk, dv = _attn_bwd(qn, kn, vb, do, o, L, "at_bwd")
    dqkv, dqg, dkg = _qk_prep_bwd(qkv, dq, dk, dv, w["qg"], w["kg"], cos, sin, L, "at_prep_bwd")
    dwqkv = _matmul_tn(h1, dqkv[None], QKV_NB, "at_qkv_dw", blocked=False)
    dx2, dgm1 = _nt_normbwd(dqkv[None], w["w_qkv"], x2, gm[1], dx3, "at_qkv_bwd")
    dx1, da0, dob0, dgl0 = _mlp_bwd_dx(x1, dx2, a0, gl[0], w["w_up"][0], w["w_down"][0], "mlp0_bwd_dx")
    dwu0, dwd0 = _mlp_bwd_dw(hm0, da0, a0, dob0, fb, "mlp0_bwd_dw")
    dyg, dx1b = _matmul_nt(dx1, w["w_out"], "rg_out_bwd", F32)
    dwout = _matmul_tn(yg, dx1b[None], QKV_NB, "rg_out_dw", blocked=False)
    dz, dwa, dwx, dpv = _rg_bwd(z, hf, hb, dyg, w["pvec"], w["wa"], w["wx"], L, "rg_bwd")
    dwin = _matmul_tn(h0, dz, nb_in, "rg_in_dw", blocked=True)
    dx0, dgm0 = _nt_normbwd(dz, w["w_in"], x, gm[0], dx1, "rg_in_bwd")

    grads = dict(
        g_mix=jnp.concatenate([dgm0, dgm1], axis=0), g_mlp=jnp.concatenate([dgl0, dgl1], axis=0), g_fin=dgf,
        qg=dqg, kg=dkg, pvec=dpv, w_in=dwin, wa=dwa, wx=dwx, w_out=dwout, w_qkv=dwqkv, w_o=dwo,
        w_up=[dwu0, dwu1], w_down=[dwd0, dwd1])
    return loss[0, 0], dx0.reshape(Bl, L, D), grads


MESH = pl.DeviceIdType.MESH
ANY = pl.BlockSpec(memory_space=pl.ANY)
N_PEERS = N_DEV - 1


def _my_place():
    return lax.axis_index("x"), lax.axis_index("y"), lax.axis_index("c")


def _flat(px, py, pc):
    return 4 * px + 2 * py + pc


def _all_gather(shards, name):
    n = len(shards)

    def body(*refs):
        ins, outs = refs[:n], refs[n:2 * n]
        send_sems, recv_sems, local_sems = refs[2 * n:]
        x, y, c = _my_place()
        me, sibling = (x, y, c), (x, y, 1 - c)
        chips = [(1 - x, y), (x, 1 - y), (1 - x, 1 - y)]

        def copy(a, k, block, to, src=None):
            dst = outs[a].at[_flat(*block)]
            return pltpu.make_async_remote_copy(
                src_ref=dst if src is None else src, dst_ref=dst,
                send_sem=send_sems.at[a, k], recv_sem=recv_sems.at[a, k],
                device_id=to, device_id_type=MESH)

        mine = [pltpu.make_async_copy(ins[a], outs[a].at[_flat(*me)], local_sems.at[a]) for a in range(n)]
        for cp in mine:
            cp.start()
        first = []
        for a in range(n):
            first.append(copy(a, 0, me, sibling, src=ins[a]))
            first += [copy(a, 1 + j, me, (*chip, c), src=ins[a]) for j, chip in enumerate(chips)]
        for cp in first:
            cp.start()
        passed = []
        for j, chip in enumerate(chips):
            for a in range(n):
                copy(a, 1 + j, (*chip, c), me).wait_recv()
                fwd = copy(a, 4 + j, (*chip, c), sibling)
                fwd.start()
                passed.append(fwd)
        for a in range(n):
            copy(a, 0, sibling, me).wait_recv()
            for j, chip in enumerate(chips):
                copy(a, 4 + j, (*chip, 1 - c), me).wait_recv()
        for cp in first + passed:
            cp.wait_send()
        for cp in mine:
            cp.wait()

    return pl.pallas_call(
        body, name=name,
        in_specs=[ANY] * n, out_specs=[ANY] * n,
        out_shape=[S((N_DEV,) + s.shape, s.dtype) for s in shards],
        scratch_shapes=[pltpu.SemaphoreType.DMA((n, N_PEERS)), pltpu.SemaphoreType.DMA((n, N_PEERS)),
                        pltpu.SemaphoreType.DMA((n,))],
    )(*shards)


def _all_to_all(xs, name):
    n = len(xs)

    def body(*refs):
        ins, outs = refs[:n], refs[n:2 * n]
        send_sems, recv_sems, local_sems = refs[2 * n:]
        x, y, c = _my_place()
        me = _flat(x, y, c)
        mine = [pltpu.make_async_copy(ins[a].at[me], outs[a].at[me], local_sems.at[a]) for a in range(n)]
        for cp in mine:
            cp.start()
        sent = []
        for r in range(1, N_DEV):
            peer = (1 - x if r & 4 else x, 1 - y if r & 2 else y, 1 - c if r & 1 else c)
            for a in range(n):
                cp = pltpu.make_async_remote_copy(
                    src_ref=ins[a].at[_flat(*peer)], dst_ref=outs[a].at[me],
                    send_sem=send_sems.at[a, r - 1], recv_sem=recv_sems.at[a, r - 1],
                    device_id=peer, device_id_type=MESH)
                cp.start()
                sent.append((cp, a, r, peer))
        for cp, a, r, peer in sent:
            pltpu.make_async_remote_copy(
                src_ref=ins[a].at[_flat(*peer)], dst_ref=outs[a].at[_flat(*peer)],
                send_sem=send_sems.at[a, r - 1], recv_sem=recv_sems.at[a, r - 1],
                device_id=peer, device_id_type=MESH).wait_recv()
        for cp, a, r, peer in sent:
            cp.wait_send()
        for cp in mine:
            cp.wait()

    return pl.pallas_call(
        body, name=name,
        in_specs=[ANY] * n, out_specs=[ANY] * n,
        out_shape=[S(v.shape, v.dtype) for v in xs],
        scratch_shapes=[pltpu.SemaphoreType.DMA((n, N_PEERS)), pltpu.SemaphoreType.DMA((n, N_PEERS)),
                        pltpu.SemaphoreType.DMA((n,))],
    )(*xs)


def _row_tile(rows, cols):
    want = max(16, (128 * 1024) // cols)
    if rows <= want:
        return rows
    t = want - want % 16
    while rows % t:
        t -= 16
    return t


def _sum_parts(parts, name):
    P, R, C = parts.shape
    tr = _row_tile(R, C)

    def body(p_ref, o_ref):
        g = p_ref[0].astype(F32)
        for i in range(1, P):
            g = g + p_ref[i].astype(F32)
        o_ref[...] = g

    return pl.pallas_call(
        body, name=name, grid=(R // tr,),
        in_specs=[pl.BlockSpec((P, tr, C), lambda i: (0, i, 0))],
        out_specs=pl.BlockSpec((tr, C), lambda i: (i, 0)),
        out_shape=S((R, C), F32),
        compiler_params=_cp("parallel"),
    )(parts)


def _adamw(parts, w, m, v, name):
    P, R, C = parts.shape
    tr = _row_tile(R, C)
    c1 = 1.0 - ADAM_B1 ** ADAM_STEP
    c2 = 1.0 - ADAM_B2 ** ADAM_STEP

    def body(p_ref, w_ref, m_ref, v_ref, g_ref, d_ref, mo_ref, vo_ref):
        g = p_ref[0].astype(F32)
        for i in range(1, P):
            g = g + p_ref[i].astype(F32)
        mn = ADAM_B1 * m_ref[...] + (1.0 - ADAM_B1) * g
        vn = ADAM_B2 * v_ref[...] + (1.0 - ADAM_B2) * (g * g)
        g_ref[...] = g
        mo_ref[...] = mn
        vo_ref[...] = vn
        d_ref[...] = (-ADAM_LR) * ((mn / c1) / (jnp.sqrt(vn / c2) + ADAM_EPS) + ADAM_WD * w_ref[...])

    blk = pl.BlockSpec((tr, C), lambda i: (i, 0))
    return pl.pallas_call(
        body, name=name, grid=(R // tr,),
        in_specs=[pl.BlockSpec((P, tr, C), lambda i: (0, i, 0)), blk, blk, blk],
        out_specs=[blk, blk, blk, blk],
        out_shape=[S((R, C), F32)] * 4,
        compiler_params=_cp("parallel"),
    )(parts, w, m, v)


def _adamw_nd(parts, w, m, v, name):
    shp = w.shape
    C = shp[-1]
    outs = _adamw(parts.reshape(parts.shape[0], -1, C), w.reshape(-1, C), m.reshape(-1, C), v.reshape(-1, C), name)
    return [o.reshape(shp) for o in outs]


RP_PAD_ROWS = 8 * N_DEV


def _pack_rows(arrs):
    rows = jnp.concatenate([a.reshape(-1, 128) for a in arrs], axis=0)
    pad = (-rows.shape[0]) % RP_PAD_ROWS
    return jnp.concatenate([rows, jnp.zeros((pad, 128), rows.dtype)], axis=0)


def _unpack_rows(rows, like):
    out, r = [], 0
    for a in like:
        n = a.size // 128
        out.append(rows[r:r + n].reshape(a.shape))
        r += n
    return out


def kernel(x, norm_mix_g, norm_mlp_g, rg_w_in, rg_conv_w, rg_conv_b, rg_w_a, rg_b_a, rg_w_x, rg_b_x, rg_lam, rg_w_out, at_w_qkv, at_q_g, at_k_g, at_w_o, mlp_w_up, mlp_w_down, final_g, loss_target, m_norm_mix_g, m_norm_mlp_g, m_rg_w_in, m_rg_conv_w, m_rg_conv_b, m_rg_w_a, m_rg_b_a, m_rg_w_x, m_rg_b_x, m_rg_lam, m_rg_w_out, m_at_w_qkv, m_at_q_g, m_at_k_g, m_at_w_o, m_mlp_w_up, m_mlp_w_down, m_final_g, v_norm_mix_g, v_norm_mlp_g, v_rg_w_in, v_rg_conv_w, v_rg_conv_b, v_rg_w_a, v_rg_b_a, v_rg_w_x, v_rg_b_x, v_rg_lam, v_rg_w_out, v_at_w_qkv, v_at_q_g, v_at_k_g, v_at_w_o, v_mlp_w_up, v_mlp_w_down, v_final_g):
    D = x.shape[-1]
    bf = lambda a: a.astype(BF16)

    def small_pack(cw, ba, bx, lam):
        return jnp.concatenate([cw[0, :, 0, :], ba[0], bx[0], lam[0], jnp.zeros((PV_ROWS - PV_CONV_B, LRU_BW), F32)], axis=0)

    sp_w = small_pack(rg_conv_w, rg_b_a, rg_b_x, rg_lam)
    g_in, g_out, g_qkv, g_o, g_up0, g_up1, g_dn0, g_dn1, g_sp = _all_gather(
        [bf(rg_w_in[0]), bf(rg_w_out[0]), bf(at_w_qkv[0]), bf(at_w_o[0]), bf(mlp_w_up[0]), bf(mlp_w_up[1]),
         bf(mlp_w_down[0]), bf(mlp_w_down[1]), sp_w], "gather_weights")
    qkv_w = g_qkv.shape[0] * g_qkv.shape[2]
    pvec = g_sp.transpose(1, 0, 2).reshape(PV_ROWS, D)
    pvec = pvec.at[PV_CONV_B].set(rg_conv_b[0])
    w = dict(
        g_mix=norm_mix_g, g_mlp=norm_mlp_g, g_fin=final_g[None], qg=at_q_g, kg=at_k_g, pvec=pvec,
        w_in=g_in, wa=bf(rg_w_a[0]), wx=bf(rg_w_x[0]), w_out=g_out.reshape(D, D),
        w_qkv=g_qkv.transpose(1, 0, 2).reshape(D, qkv_w // QKV_NB, QKV_NB).transpose(1, 0, 2),
        w_o=g_o.reshape(D, D), w_up=[g_up0, g_up1],
        w_down=[g_dn0.reshape(-1, D), g_dn1.reshape(-1, D)])

    loss, grad_x, g = _local_step(x, loss_target, w)
    loss = lax.psum(loss, ("x", "y", "c"))

    rep_like = [norm_mix_g, norm_mlp_g, final_g, rg_conv_b, at_q_g, at_k_g, rg_w_a, rg_w_x]
    rep_g = _pack_rows([g["g_mix"], g["g_mlp"], g["g_fin"], g["pvec"][PV_CONV_B], g["qg"], g["kg"], g["wa"], g["wx"]])
    rp_rows = rep_g.shape[0] // N_DEV
    r_in, r_out, r_qkv, r_o, r_up0, r_up1, r_dn0, r_dn1, r_sp, r_rep = _all_to_all(
        [g["w_in"], g["w_out"].reshape(N_DEV, -1, D),
         g["w_qkv"].reshape(D, N_DEV, -1).transpose(1, 0, 2), g["w_o"].reshape(N_DEV, -1, D),
         g["w_up"][0], g["w_up"][1], g["w_down"][0].reshape(N_DEV, -1, D), g["w_down"][1].reshape(N_DEV, -1, D),
         g["pvec"].reshape(PV_ROWS, N_DEV, LRU_BW).transpose(1, 0, 2), rep_g.reshape(N_DEV, rp_rows, 128)],
        "scatter_grads")

    res = {}
    res["rg_w_in"] = _adamw_nd(r_in[:, None], rg_w_in, m_rg_w_in, v_rg_w_in, "adam_rg_w_in")
    res["rg_w_out"] = _adamw_nd(r_out[:, None], rg_w_out, m_rg_w_out, v_rg_w_out, "adam_rg_w_out")
    res["at_w_qkv"] = _adamw_nd(r_qkv[:, None], at_w_qkv, m_at_w_qkv, v_at_w_qkv, "adam_at_w_qkv")
    res["at_w_o"] = _adamw_nd(r_o[:, None], at_w_o, m_at_w_o, v_at_w_o, "adam_at_w_o")
    res["mlp_w_up"] = _adamw_nd(jnp.stack([r_up0, r_up1], axis=1), mlp_w_up, m_mlp_w_up, v_mlp_w_up, "adam_mlp_w_up")
    res["mlp_w_down"] = _adamw_nd(jnp.stack([r_dn0, r_dn1], axis=1), mlp_w_down, m_mlp_w_down, v_mlp_w_down, "adam_mlp_w_down")
    sp_res = _adamw(r_sp, sp_w, small_pack(m_rg_conv_w, m_rg_b_a, m_rg_b_x, m_rg_lam),
                    small_pack(v_rg_conv_w, v_rg_b_a, v_rg_b_x, v_rg_lam), "adam_small")
    for k, o in enumerate(sp_res):
        res.setdefault("rg_conv_w", []).append(o[PV_CONV_W:PV_CONV_W + CONV_W][None, :, None, :])
        res.setdefault("rg_b_a", []).append(o[PV_B_A:PV_B_A + 2][None])
        res.setdefault("rg_b_x", []).append(o[PV_B_X:PV_B_X + 2][None])
        res.setdefault("rg_lam", []).append(o[PV_LAM:PV_LAM + 2][None])

    rep_sum, = _all_gather([_sum_parts(r_rep, "reduce_replicated")], "gather_replicated")
    m_like = [m_norm_mix_g, m_norm_mlp_g, m_final_g, m_rg_conv_b, m_at_q_g, m_at_k_g, m_rg_w_a, m_rg_w_x]
    v_like = [v_norm_mix_g, v_norm_mlp_g, v_final_g, v_rg_conv_b, v_at_q_g, v_at_k_g, v_rg_w_a, v_rg_w_x]
    rep_res = _adamw(rep_sum.reshape(1, -1, 128), _pack_rows(rep_like), _pack_rows(m_like), _pack_rows(v_like), "adam_replicated")
    rep_names = ["norm_mix_g", "norm_mlp_g", "final_g", "rg_conv_b", "at_q_g", "at_k_g", "rg_w_a", "rg_w_x"]
    for o in rep_res:
        for nm, val in zip(rep_names, _unpack_rows(o, rep_like)):
            res.setdefault(nm, []).append(val)

    order = ["norm_mix_g", "norm_mlp_g", "rg_w_in", "rg_conv_w", "rg_conv_b", "rg_w_a", "rg_b_a", "rg_w_x", "rg_b_x",
             "rg_lam", "rg_w_out", "at_w_qkv", "at_q_g", "at_k_g", "at_w_o", "mlp_w_up", "mlp_w_down", "final_g"]
    return (loss, grad_x, *[res[nm][k] for k in range(4) for nm in order])
```

```python
import functools
import math

import jax
import jax.numpy as jnp
from jax import lax
from jax.experimental import pallas as pl
from jax.experimental.pallas import tpu as pltpu

F32 = jnp.float32
BF16 = jnp.bfloat16
S = jax.ShapeDtypeStruct

EPS = 1e-6
HEAD_DIM = 128
N_KV = 2
GRID_W = 64
ROPE_THETA = 10000.0
LRU_BW = 128
RG_C = 8.0
CONV_W = 4
N_DEV = 8
N_SEG = 8
SCAN_UNROLL = 8
TN_STEP_COLS = 512
VMEM_LIMIT_V7X = 56 * 1024 * 1024
SOFTMAX_SCALE = 1.0 / math.sqrt(HEAD_DIM)
GELU_K = math.sqrt(2.0 / math.pi)
GELU_C = 0.044715

ADAM_LR = 0.001
ADAM_B1 = 0.9
ADAM_B2 = 0.999
ADAM_EPS = 1e-08
ADAM_WD = 0.01
ADAM_STEP = 10

NT = (((1,), (1,)), ((), ()))
TN = (((0,), (0,)), ((), ()))


def _cp(*sem):
    return pltpu.CompilerParams(dimension_semantics=sem, vmem_limit_bytes=VMEM_LIMIT_V7X)


def _rms_r(xv):
    return lax.rsqrt(jnp.mean(xv * xv, axis=-1, keepdims=True) + EPS)


def _rms_bwd(dh, xv, g):
    r = _rms_r(xv)
    xh = xv * r
    dg = jnp.sum(dh * xh, axis=0, keepdims=True)
    dxh = dh * g
    dx = r * (dxh - xh * jnp.mean(dxh * xh, axis=-1, keepdims=True))
    return dx, dg


def _dot(a, b):
    return jnp.dot(a, b, preferred_element_type=F32)


def _dot_nt(a, b):
    return lax.dot_general(a, b, NT, preferred_element_type=F32)


def _dot_tn(a, b):
    return lax.dot_general(a, b, TN, preferred_element_type=F32)


def _norm_matmul(x, g, wblk, name, out_dtype=F32):
    T, D = x.shape
    NB, _, nb = wblk.shape
    tm = min(T, 512)

    def body(x_ref, g_ref, w_ref, o_ref, h_ref):
        xv = x_ref[...]
        hb = (xv * _rms_r(xv) * g_ref[...]).astype(BF16)
        h_ref[...] = hb
        for q in range(NB):
            o_ref[:, q * nb:(q + 1) * nb] = _dot(hb, w_ref[q]).astype(o_ref.dtype)

    return pl.pallas_call(
        body, name=name, grid=(T // tm,),
        in_specs=[pl.BlockSpec((tm, D), lambda i: (i, 0)),
                  pl.BlockSpec((1, D), lambda i: (0, 0)),
                  pl.BlockSpec((NB, D, nb), lambda i: (0, 0, 0))],
        out_specs=[pl.BlockSpec((tm, NB * nb), lambda i: (i, 0)),
                   pl.BlockSpec((tm, D), lambda i: (i, 0))],
        out_shape=[S((T, NB * nb), out_dtype), S((T, D), BF16)],
        compiler_params=_cp("parallel"),
    )(x, g, wblk)


def _matmul_res(a, w, res, name):
    T, K = a.shape
    N = w.shape[1]
    tm = min(T, 512)

    def body(a_ref, w_ref, r_ref, o_ref):
        o_ref[...] = r_ref[...] + _dot(a_ref[...], w_ref[...])

    return pl.pallas_call(
        body, name=name, grid=(T // tm,),
        in_specs=[pl.BlockSpec((tm, K), lambda i: (i, 0)),
                  pl.BlockSpec((K, N), lambda i: (0, 0)),
                  pl.BlockSpec((tm, N), lambda i: (i, 0))],
        out_specs=pl.BlockSpec((tm, N), lambda i: (i, 0)),
        out_shape=S((T, N), F32),
        compiler_params=_cp("parallel"),
    )(a, w, res)


def _matmul_nt(a, w, name, out_dtype, after=()):
    T, N = a.shape
    K = w.shape[0]
    tm = min(T, 512)

    def body(a_ref, w_ref, *rest):
        o_ref, ab_ref = rest[len(after):]
        ab = a_ref[...].astype(BF16)
        ab_ref[...] = ab
        o_ref[...] = _dot_nt(ab, w_ref[...]).astype(o_ref.dtype)

    return pl.pallas_call(
        body, name=name, grid=(T // tm,),
        in_specs=[pl.BlockSpec((tm, N), lambda i: (i, 0)),
                  pl.BlockSpec((K, N), lambda i: (0, 0))] + [pl.BlockSpec(memory_space=pl.ANY)] * len(after),
        out_specs=[pl.BlockSpec((tm, K), lambda i: (i, 0)),
                   pl.BlockSpec((tm, N), lambda i: (i, 0))],
        out_shape=[S((T, K), out_dtype), S((T, N), BF16)],
        compiler_params=_cp("parallel"),
    )(a, w, *after)


def _matmul_tn(a, b3, nb, name, blocked):
    T, M = a.shape
    SB, _, N = b3.shape
    per = N // nb
    NB = SB * per
    tk = min(T, 1024)
    nk = T // tk
    jb = max(1, TN_STEP_COLS // nb) if blocked else 1
    assert per % jb == 0
    if blocked:
        out_spec, out_shape = pl.BlockSpec((jb, M, nb), lambda j, k: (j, 0, 0)), S((NB, M, nb), BF16)
    else:
        assert SB == 1
        out_spec, out_shape = pl.BlockSpec((M, nb), lambda j, k: (0, j)), S((M, N), BF16)

    def body(a_ref, b_ref, o_ref, acc_ref):
        k = pl.program_id(1)

        @pl.when(k == 0)
        def _():
            acc_ref[...] = jnp.zeros_like(acc_ref)

        av = a_ref[...]
        for q in range(jb):
            acc_ref[q] += _dot_tn(av, b_ref[:, q * nb:(q + 1) * nb])

        @pl.when(k == nk - 1)
        def _():
            if blocked:
                o_ref[...] = acc_ref[...].astype(BF16)
            else:
                o_ref[...] = acc_ref[0].astype(BF16)

    return pl.pallas_call(
        body, name=name, grid=(NB // jb, nk),
        in_specs=[pl.BlockSpec((tk, M), lambda j, k: (k, 0)),
                  pl.BlockSpec((None, tk, jb * nb), lambda j, k: ((j * jb) // per, k, ((j * jb) % per) // jb))],
        out_specs=out_spec,
        out_shape=out_shape,
        scratch_shapes=[pltpu.VMEM((jb, M, nb), F32)],
        compiler_params=_cp("parallel", "arbitrary"),
    )(a, b3)


def _nt_normbwd(dz3, wblk, x, g, dres, name, after=()):
    T, D = x.shape
    NB, _, nb = wblk.shape
    SB, _, N = dz3.shape
    per = N // nb
    tm = min(T, 512)

    def body(dz_ref, w_ref, x_ref, g_ref, dr_ref, *rest):
        dx_ref, dg_ref = rest[len(after):]

        @pl.when(pl.program_id(0) == 0)
        def _():
            dg_ref[...] = jnp.zeros_like(dg_ref)

        dh = None
        for q in range(NB):
            cols = slice((q % per) * nb, (q % per + 1) * nb)
            part = _dot_nt(dz_ref[q // per, :, cols], w_ref[q])
            dh = part if dh is None else dh + part
        dx, dg = _rms_bwd(dh, x_ref[...], g_ref[...])
        dx_ref[...] = dr_ref[...] + dx
        dg_ref[...] += dg

    return pl.pallas_call(
        body, name=name, grid=(T // tm,),
        in_specs=[pl.BlockSpec((SB, tm, N), lambda i: (0, i, 0)),
                  pl.BlockSpec((NB, D, nb), lambda i: (0, 0, 0)),
                  pl.BlockSpec((tm, D), lambda i: (i, 0)),
                  pl.BlockSpec((1, D), lambda i: (0, 0)),
                  pl.BlockSpec((tm, D), lambda i: (i, 0))] + [pl.BlockSpec(memory_space=pl.ANY)] * len(after),
        out_specs=[pl.BlockSpec((tm, D), lambda i: (i, 0)),
                   pl.BlockSpec((1, D), lambda i: (0, 0))],
        out_shape=[S((T, D), F32), S((1, D), F32)],
        compiler_params=_cp("arbitrary"),
    )(dz3, wblk, x, g, dres, *after)


def _mlp_fwd(x, g, wup, wdown, name):
    T, D = x.shape
    NB, _, fb = wup.shape
    tm = min(T, 1024)

    def body(x_ref, g_ref, wu_ref, wd_ref, xo_ref, a_ref, h_ref, acc_ref):
        j = pl.program_id(1)

        @pl.when(j == 0)
        def _():
            xv = x_ref[...]
            h_ref[...] = (xv * _rms_r(xv) * g_ref[...]).astype(BF16)
            acc_ref[...] = xv

        a = _dot(h_ref[...], wu_ref[...])
        a_ref[...] = a.astype(BF16)
        u = jnp.maximum(a, 0.0)
        acc_ref[...] += _dot((u * u).astype(BF16), wd_ref[...])

        @pl.when(j == NB - 1)
        def _():
            xo_ref[...] = acc_ref[...]

    return pl.pallas_call(
        body, name=name, grid=(T // tm, NB),
        in_specs=[pl.BlockSpec((tm, D), lambda i, j: (i, 0)),
                  pl.BlockSpec((1, D), lambda i, j: (0, 0)),
                  pl.BlockSpec((None, D, fb), lambda i, j: (j, 0, 0)),
                  pl.BlockSpec((fb, D), lambda i, j: (j, 0))],
        out_specs=[pl.BlockSpec((tm, D), lambda i, j: (i, 0)),
                   pl.BlockSpec((tm, fb), lambda i, j: (i, j)),
                   pl.BlockSpec((tm, D), lambda i, j: (i, 0))],
        out_shape=[S((T, D), F32), S((T, NB * fb), BF16), S((T, D), BF16)],
        scratch_shapes=[pltpu.VMEM((tm, D), F32)],
        compiler_params=_cp("parallel", "arbitrary"),
    )(x, g, wup, wdown)


def _mlp_bwd_dx(x, dout, a, g, wup, wdown, name):
    T, D = x.shape
    NB, _, fb = wup.shape
    tm = min(T, 512)

    def body(x_ref, do_ref, a_ref, g_ref, wu_ref, wd_ref, dx_ref, da_ref, dob_ref, dg_ref, acc_ref):
        i, j = pl.program_id(0), pl.program_id(1)

        @pl.when(j == 0)
        def _():
            dob_ref[...] = do_ref[...].astype(BF16)
            acc_ref[...] = jnp.zeros_like(acc_ref)

        @pl.when((i == 0) & (j == 0))
        def _():
            dg_ref[...] = jnp.zeros_like(dg_ref)

        du2 = _dot_nt(dob_ref[...], wd_ref[...])
        u = jnp.maximum(a_ref[...].astype(F32), 0.0)
        da = (du2 * (2.0 * u)).astype(BF16)
        da_ref[...] = da
        acc_ref[...] += _dot_nt(da, wu_ref[...])

        @pl.when(j == NB - 1)
        def _():
            dx, dg = _rms_bwd(acc_ref[...], x_ref[...], g_ref[...])
            dx_ref[...] = do_ref[...] + dx
            dg_ref[...] += dg

    return pl.pallas_call(
        body, name=name, grid=(T // tm, NB),
        in_specs=[pl.BlockSpec((tm, D), lambda i, j: (i, 0)),
                  pl.BlockSpec((tm, D), lambda i, j: (i, 0)),
                  pl.BlockSpec((tm, fb), lambda i, j: (i, j)),
                  pl.BlockSpec((1, D), lambda i, j: (0, 0)),
                  pl.BlockSpec((None, D, fb), lambda i, j: (j, 0, 0)),
                  pl.BlockSpec((fb, D), lambda i, j: (j, 0))],
        out_specs=[pl.BlockSpec((tm, D), lambda i, j: (i, 0)),
                   pl.BlockSpec((tm, fb), lambda i, j: (i, j)),
                   pl.BlockSpec((tm, D), lambda i, j: (i, 0)),
                   pl.BlockSpec((1, D), lambda i, j: (0, 0))],
        out_shape=[S((T, D), F32), S((T, NB * fb), BF16), S((T, D), BF16), S((1, D), F32)],
        scratch_shapes=[pltpu.VMEM((tm, D), F32)],
        compiler_params=_cp("arbitrary", "arbitrary"),
    )(x, dout, a, g, wup, wdown)


def _mlp_bwd_dw(h, da, a, dob, fb, name):
    T, D = h.shape
    F = a.shape[1]
    NB = F // fb
    tk = min(T, 1024)
    nk = T // tk

    def body(h_ref, da_ref, a_ref, dob_ref, dwu_ref, dwd_ref, au_ref, ad_ref):
        k = pl.program_id(1)

        @pl.when(k == 0)
        def _():
            au_ref[...] = jnp.zeros_like(au_ref)
            ad_ref[...] = jnp.zeros_like(ad_ref)

        au_ref[...] += _dot_tn(h_ref[...], da_ref[...])
        u = jnp.maximum(a_ref[...].astype(F32), 0.0)
        ad_ref[...] += _dot_tn((u * u).astype(BF16), dob_ref[...])

        @pl.when(k == nk - 1)
        def _():
            dwu_ref[...] = au_ref[...].astype(BF16)
            dwd_ref[...] = ad_ref[...].astype(BF16)

    return pl.pallas_call(
        body, name=name, grid=(NB, nk),
        in_specs=[pl.BlockSpec((tk, D), lambda j, k: (k, 0)),
                  pl.BlockSpec((tk, fb), lambda j, k: (k, j)),
                  pl.BlockSpec((tk, fb), lambda j, k: (k, j)),
                  pl.BlockSpec((tk, D), lambda j, k: (k, 0))],
        out_specs=[pl.BlockSpec((None, D, fb), lambda j, k: (j, 0, 0)),
                   pl.BlockSpec((fb, D), lambda j, k: (j, 0))],
        out_shape=[S((NB, D, fb), BF16), S((F, D), BF16)],
        scratch_shapes=[pltpu.VMEM((D, fb), F32), pltpu.VMEM((fb, D), F32)],
        compiler_params=_cp("parallel", "arbitrary"),
    )(h, da, a, dob)


def _final_loss(x, tgt, g, name):
    T, D = x.shape
    tm = min(T, 512)

    def body(x_ref, t_ref, g_ref, dx_ref, loss_ref, dg_ref):
        @pl.when(pl.program_id(0) == 0)
        def _():
            loss_ref[...] = jnp.zeros_like(loss_ref)
            dg_ref[...] = jnp.zeros_like(dg_ref)

        xv = x_ref[...]
        gv = g_ref[...]
        err = xv * _rms_r(xv) * gv - t_ref[...]
        e2 = jnp.sum(jnp.sum(err * err, axis=-1, keepdims=True), axis=0, keepdims=True)
        loss_ref[...] += (0.5 / D) * e2
        dx, dg = _rms_bwd(err * (1.0 / D), xv, gv)
        dx_ref[...] = dx
        dg_ref[...] += dg

    return pl.pallas_call(
        body, name=name, grid=(T // tm,),
        in_specs=[pl.BlockSpec((tm, D), lambda i: (i, 0)),
                  pl.BlockSpec((tm, D), lambda i: (i, 0)),
                  pl.BlockSpec((1, D), lambda i: (0, 0))],
        out_specs=[pl.BlockSpec((tm, D), lambda i: (i, 0)),
                   pl.BlockSpec((1, 128), lambda i: (0, 0)),
                   pl.BlockSpec((1, D), lambda i: (0, 0))],
        out_shape=[S((T, D), F32), S((1, 128), F32), S((1, D), F32)],
        compiler_params=_cp("arbitrary"),
    )(x, tgt, g)


def _rope_tables(L):
    nf = HEAD_DIM // 4
    t = jnp.arange(L, dtype=jnp.int32)
    row = (t // GRID_W).astype(F32)
    col = (t % GRID_W).astype(F32)
    inv = ROPE_THETA ** (-jnp.arange(nf, dtype=F32) / nf)
    ar = row[:, None] * inv
    ac = col[:, None] * inv
    cos = jnp.concatenate([jnp.cos(ar), jnp.cos(ar), jnp.cos(ac), jnp.cos(ac)], axis=-1)
    sin = jnp.concatenate([-jnp.sin(ar), jnp.sin(ar), -jnp.sin(ac), jnp.sin(ac)], axis=-1)
    return cos, sin


def _swap32(x):
    lane = lax.broadcasted_iota(jnp.int32, x.shape, 1)
    up = pltpu.roll(x, HEAD_DIM - 32, 1)
    down = pltpu.roll(x, 32, 1)
    return jnp.where((lane % 64) < 32, up, down)


def _qk_prep(qkv, qg, kg, cos, sin, L, name):
    T, W = qkv.shape
    nh = W // HEAD_DIM - 2 * N_KV
    tm = min(L, 512)
    lb = L // tm

    def body(qkv_ref, qg_ref, kg_ref, cos_ref, sin_ref, q_ref, k_ref, v_ref):
        c = cos_ref[...]
        s = sin_ref[...]
        for h in range(nh + N_KV):
            xh = qkv_ref[:, h * HEAD_DIM:(h + 1) * HEAD_DIM]
            gv = qg_ref[...] if h < nh else kg_ref[...]
            y = xh * _rms_r(xh) * gv
            y = (y * c + _swap32(y) * s).astype(BF16)
            if h < nh:
                q_ref[:, h * HEAD_DIM:(h + 1) * HEAD_DIM] = y
            else:
                k_ref[:, (h - nh) * HEAD_DIM:(h - nh + 1) * HEAD_DIM] = y
        v_ref[...] = qkv_ref[:, (nh + N_KV) * HEAD_DIM:].astype(BF16)

    return pl.pallas_call(
        body, name=name, grid=(T // tm,),
        in_specs=[pl.BlockSpec((tm, W), lambda i: (i, 0)),
                  pl.BlockSpec((1, HEAD_DIM), lambda i: (0, 0)),
                  pl.BlockSpec((1, HEAD_DIM), lambda i: (0, 0)),
                  pl.BlockSpec((tm, HEAD_DIM), lambda i: (i % lb, 0)),
                  pl.BlockSpec((tm, HEAD_DIM), lambda i: (i % lb, 0))],
        out_specs=[pl.BlockSpec((tm, nh * HEAD_DIM), lambda i: (i, 0)),
                   pl.BlockSpec((tm, N_KV * HEAD_DIM), lambda i: (i, 0)),
                   pl.BlockSpec((tm, N_KV * HEAD_DIM), lambda i: (i, 0))],
        out_shape=[S((T, nh * HEAD_DIM), BF16), S((T, N_KV * HEAD_DIM), BF16), S((T, N_KV * HEAD_DIM), BF16)],
        compiler_params=_cp("parallel"),
    )(qkv, qg, kg, cos, sin)


def _qk_prep_bwd(qkv, dq, dk, dv, qg, kg, cos, sin, L, name):
    T, W = qkv.shape
    nh = W // HEAD_DIM - 2 * N_KV
    tm = min(L, 512)
    lb = L // tm

    def body(qkv_ref, dq_ref, dk_ref, dv_ref, qg_ref, kg_ref, cos_ref, sin_ref, dz_ref, dqg_ref, dkg_ref):
        @pl.when(pl.program_id(0) == 0)
        def _():
            dqg_ref[...] = jnp.zeros_like(dqg_ref)
            dkg_ref[...] = jnp.zeros_like(dkg_ref)

        c = cos_ref[...]
        s = sin_ref[...]
        for h in range(nh + N_KV):
            cols = slice(h * HEAD_DIM, (h + 1) * HEAD_DIM)
            if h < nh:
                dout, gv, dg_ref = dq_ref[:, cols], qg_ref[...], dqg_ref
            else:
                kc = slice((h - nh) * HEAD_DIM, (h - nh + 1) * HEAD_DIM)
                dout, gv, dg_ref = dk_ref[:, kc], kg_ref[...], dkg_ref
            dy = dout * c - _swap32(dout) * s
            dx, dg = _rms_bwd(dy, qkv_ref[:, cols], gv)
            dg_ref[...] += dg
            dz_ref[:, cols] = dx.astype(BF16)
        dz_ref[:, (nh + N_KV) * HEAD_DIM:] = dv_ref[...].astype(BF16)

    return pl.pallas_call(
        body, name=name, grid=(T // tm,),
        in_specs=[pl.BlockSpec((tm, W), lambda i: (i, 0)),
                  pl.BlockSpec((tm, nh * HEAD_DIM), lambda i: (i, 0)),
                  pl.BlockSpec((tm, N_KV * HEAD_DIM), lambda i: (i, 0)),
                  pl.BlockSpec((tm, N_KV * HEAD_DIM), lambda i: (i, 0)),
                  pl.BlockSpec((1, HEAD_DIM), lambda i: (0, 0)),
                  pl.BlockSpec((1, HEAD_DIM), lambda i: (0, 0)),
                  pl.BlockSpec((tm, HEAD_DIM), lambda i: (i % lb, 0)),
                  pl.BlockSpec((tm, HEAD_DIM), lambda i: (i % lb, 0))],
        out_specs=[pl.BlockSpec((tm, W), lambda i: (i, 0)),
                   pl.BlockSpec((1, HEAD_DIM), lambda i: (0, 0)),
                   pl.BlockSpec((1, HEAD_DIM), lambda i: (0, 0))],
        out_shape=[S((T, W), BF16), S((1, HEAD_DIM), F32), S((1, HEAD_DIM), F32)],
        compiler_params=_cp("arbitrary"),
    )(qkv, dq, dk, dv, qg, kg, cos, sin)


def _softmax_rows(q, k):
    s = _dot_nt(q, k) * SOFTMAX_SCALE
    e = jnp.exp(s - jnp.max(s, axis=-1, keepdims=True))
    return e, jnp.sum(e, axis=-1, keepdims=True)


def _attn_fwd(q, k, v, L, name):
    T = q.shape[0]
    nh = q.shape[1] // HEAD_DIM
    G = nh // N_KV
    B = T // L
    tq = min(L, 512)
    nq = L // tq

    def body(q_ref, k_ref, v_ref, o_ref):
        e, l = _softmax_rows(q_ref[...], k_ref[...])
        o = _dot(e.astype(BF16), v_ref[...]) / l
        o_ref[...] = o.astype(BF16)

    qspec = pl.BlockSpec((tq, HEAD_DIM), lambda b, kv, g, qi: (b * nq + qi, kv * G + g))
    kspec = pl.BlockSpec((L, HEAD_DIM), lambda b, kv, g, qi: (b, kv))
    return pl.pallas_call(
        body, name=name, grid=(B, N_KV, G, nq),
        in_specs=[qspec, kspec, kspec],
        out_specs=qspec,
        out_shape=S((T, nh * HEAD_DIM), BF16),
        compiler_params=_cp("parallel", "parallel", "parallel", "parallel"),
    )(q, k, v)


def _attn_bwd(q, k, v, do, o, L, name):
    T = q.shape[0]
    nh = q.shape[1] // HEAD_DIM
    G = nh // N_KV
    B = T // L
    tq = min(L, 256)
    nq = L // tq

    def body(q_ref, k_ref, v_ref, do_ref, o_ref, dq_ref, dk_ref, dv_ref):
        @pl.when((pl.program_id(2) == 0) & (pl.program_id(3) == 0))
        def _():
            dk_ref[...] = jnp.zeros_like(dk_ref)
            dv_ref[...] = jnp.zeros_like(dv_ref)

        qv, kv_, vv, dov = q_ref[...], k_ref[...], v_ref[...], do_ref[...]
        e, l = _softmax_rows(qv, kv_)
        p = e / l
        dsum = jnp.sum(dov.astype(F32) * o_ref[...].astype(F32), axis=-1, keepdims=True)
        dp = _dot_nt(dov, vv)
        ds = (p * (dp - dsum) * SOFTMAX_SCALE).astype(BF16)
        dq_ref[...] = _dot(ds, kv_)
        dk_ref[...] += _dot_tn(ds, qv)
        dv_ref[...] += _dot_tn(p.astype(BF16), dov)

    qspec = pl.BlockSpec((tq, HEAD_DIM), lambda b, kv, g, qi: (b * nq + qi, kv * G + g))
    kspec = pl.BlockSpec((L, HEAD_DIM), lambda b, kv, g, qi: (b, kv))
    return pl.pallas_call(
        body, name=name, grid=(B, N_KV, G, nq),
        in_specs=[qspec, kspec, kspec, qspec, qspec],
        out_specs=[qspec, kspec, kspec],
        out_shape=[S((T, nh * HEAD_DIM), F32), S((T, N_KV * HEAD_DIM), F32), S((T, N_KV * HEAD_DIM), F32)],
        compiler_params=_cp("parallel", "parallel", "arbitrary", "arbitrary"),
    )(q, k, v, do, o)


PV_CONV_W = 0
PV_B_A = 8
PV_B_X = 16
PV_LAM = 24
PV_CONV_B = 32
PV_ROWS = 40


def _shift_rows(x, k):
    L = x.shape[0]
    if k == 0:
        return x
    t = lax.broadcasted_iota(jnp.int32, x.shape, 0)
    rolled = pltpu.roll(x, k % L, 0)
    keep = (t >= k) if k > 0 else (t < L + k)
    return jnp.where(keep, rolled, 0.0)


def _conv_taps(rec, pv):
    c = pv[PV_CONV_B:PV_CONV_B + 1]
    for j in range(CONV_W):
        c = c + pv[PV_CONV_W + j:PV_CONV_W + j + 1] * _shift_rows(rec, 2 - j)
    return c


def _sigmoid(x):
    return 0.5 * jnp.tanh(0.5 * x) + 0.5


EXPM1_SERIES_BELOW = 0.03


def _rg_gates(c, cbf, wa, wx, ba, bx, lam):
    r = _sigmoid(_dot(cbf, wa) + ba)
    i = _sigmoid(_dot(cbf, wx) + bx)
    sp = jnp.maximum(-lam, 0.0) + jnp.log1p(jnp.exp(-jnp.abs(lam)))
    la = r * ((-RG_C) * sp)
    a = jnp.exp(la)
    a2 = a * a
    x = la + la
    series = -(x * ((x * (1.0 / 6.0) + 0.5) * x + 1.0))
    om = jnp.where(x > -EXPM1_SERIES_BELOW, series, 1.0 - a2)
    rm = lax.rsqrt(om)
    return r, i, a, om * rm, rm, a2, sp


def _gelu(x):
    t = jnp.tanh(GELU_K * (x + GELU_C * x * x * x))
    return 0.5 * x * (1.0 + t), t


def _scan_pair(af_ref, uf_ref, ab_ref, ub_ref, hf_ref, hb_ref, pf_ref, pb_ref, L):
    ls = L // N_SEG
    zero = jnp.zeros((N_SEG, LRU_BW), F32)
    one = jnp.ones((N_SEG, LRU_BW), F32)

    def steps(tc, carry):
        hf, pf, hb, pb = carry
        for q in range(SCAN_UNROLL):
            t = tc * SCAN_UNROLL + q
            rf = pl.ds(t, N_SEG, stride=ls)
            rb = pl.ds(ls - 1 - t, N_SEG, stride=ls)
            af = af_ref[rf, :]
            hf = af * hf + uf_ref[rf, :]
            pf = pf * af
            hf_ref[rf, :] = hf
            pf_ref[rf, :] = pf
            ab = ab_ref[rb, :]
            hb = ab * hb + ub_ref[rb, :]
            pb = pb * ab
            hb_ref[rb, :] = hb
            pb_ref[rb, :] = pb
        return hf, pf, hb, pb

    hf_e, pf_e, hb_e, pb_e = lax.fori_loop(0, ls // SCAN_UNROLL, steps, (zero, one, zero, one))

    cin = jnp.zeros((1, LRU_BW), F32)
    for s in range(N_SEG):
        rows = slice(s * ls, (s + 1) * ls)
        if s > 0:
            hf_ref[rows, :] = hf_ref[rows, :] + pf_ref[rows, :] * cin
        cin = hf_e[s:s + 1] + pf_e[s:s + 1] * cin
    cin = jnp.zeros((1, LRU_BW), F32)
    for s in reversed(range(N_SEG)):
        rows = slice(s * ls, (s + 1) * ls)
        if s < N_SEG - 1:
            hb_ref[rows, :] = hb_ref[rows, :] + pb_ref[rows, :] * cin
        cin = hb_e[s:s + 1] + pb_e[s:s + 1] * cin


def _rg_specs(L, D, nblk):
    slab = lambda off: pl.BlockSpec((L, LRU_BW), lambda cb, b: (b, off + cb))
    wspec = pl.BlockSpec((2, None, LRU_BW, LRU_BW), lambda cb, b: (0, cb, 0, 0))
    pvspec = pl.BlockSpec((PV_ROWS, LRU_BW), lambda cb, b: (0, cb))
    return slab, wspec, pvspec


def _rg_fwd(z, pvec, wa, wx, L, name):
    T, C2 = z.shape
    C = C2 // 2
    nblk = C // LRU_BW
    B = T // L
    slab, wspec, pvspec = _rg_specs(L, C, nblk)

    def body(gp_ref, rec_ref, pv_ref, wa_ref, wx_ref, yg_ref, hf_ref, hb_ref, a_scr, u_scr, p_scr):
        pv = pv_ref[...]
        c = _conv_taps(rec_ref[...], pv)
        cbf = c.astype(BF16)
        for d in range(2):
            _, i, a, m, _, _, _ = _rg_gates(c, cbf, wa_ref[d], wx_ref[d], pv[PV_B_A + d:PV_B_A + d + 1],
                                      pv[PV_B_X + d:PV_B_X + d + 1], pv[PV_LAM + d:PV_LAM + d + 1])
            a_scr[d] = a
            u_scr[d] = m * (i * c)
        _scan_pair(a_scr.at[0], u_scr.at[0], a_scr.at[1], u_scr.at[1], hf_ref, hb_ref, p_scr.at[0], p_scr.at[1], L)
        gate, _ = _gelu(gp_ref[...])
        yg_ref[...] = ((hf_ref[...] + hb_ref[...]) * gate).astype(BF16)

    return pl.pallas_call(
        body, name=name, grid=(nblk, B),
        in_specs=[slab(0), slab(nblk), pvspec, wspec, wspec],
        out_specs=[slab(0), slab(0), slab(0)],
        out_shape=[S((T, C), BF16), S((T, C), F32), S((T, C), F32)],
        scratch_shapes=[pltpu.VMEM((2, L, LRU_BW), F32)] * 3,
        compiler_params=_cp("parallel", "parallel"),
    )(z, z, pvec, wa, wx)


def _rg_bwd(z, hf, hb, dyg, pvec, wa, wx, L, name):
    T, C2 = z.shape
    C = C2 // 2
    nblk = C // LRU_BW
    B = T // L
    slab, wspec, pvspec = _rg_specs(L, C, nblk)

    def body(gp_ref, rec_ref, hf_ref, hb_ref, dyg_ref, pv_ref, wa_ref, wx_ref,
             dz_ref, dwa_ref, dwx_ref, dpv_ref, a_scr, u_scr, d_scr, p_scr):
        @pl.when(pl.program_id(1) == 0)
        def _():
            dwa_ref[...] = jnp.zeros_like(dwa_ref)
            dwx_ref[...] = jnp.zeros_like(dwx_ref)
            dpv_ref[...] = jnp.zeros_like(dpv_ref)

        pv = pv_ref[...]
        rec = rec_ref[...]
        c = _conv_taps(rec, pv)
        cbf = c.astype(BF16)
        gp = gp_ref[...]
        gate, th = _gelu(gp)
        dgelu = 0.5 * (1.0 + th) + 0.5 * gp * (1.0 - th * th) * GELU_K * (1.0 + 3.0 * GELU_C * gp * gp)
        dyg = dyg_ref[...]
        dz_ref[0] = (dyg * (hf_ref[...] + hb_ref[...]) * dgelu).astype(BF16)
        dy = dyg * gate

        gates = []
        for d in range(2):
            gates.append(_rg_gates(c, cbf, wa_ref[d], wx_ref[d], pv[PV_B_A + d:PV_B_A + d + 1],
                                   pv[PV_B_X + d:PV_B_X + d + 1], pv[PV_LAM + d:PV_LAM + d + 1]))
        a_scr[0] = _shift_rows(gates[1][2], 1)
        a_scr[1] = _shift_rows(gates[0][2], -1)
        u_scr[...] = dy
        _scan_pair(a_scr.at[0], u_scr, a_scr.at[1], u_scr, d_scr.at[1], d_scr.at[0], p_scr.at[0], p_scr.at[1], L)

        dc = jnp.zeros_like(c)
        rows = []
        for d in range(2):
            r, i, a, m, rm, a2, sp = gates[d]
            delta = d_scr[d]
            hnb = _shift_rows(hf_ref[...], 1) if d == 0 else _shift_rows(hb_ref[...], -1)
            da = delta * hnb
            dm = delta * (i * c)
            di = delta * (m * c)
            dc = dc + delta * (m * i)
            dla = da * a - dm * (a2 * rm)
            dpa = (dla * ((-RG_C) * sp)) * (r * (1.0 - r))
            dpx = di * (i * (1.0 - i))
            dsp = (-RG_C) * jnp.sum(dla * r, axis=0, keepdims=True)
            lam = pv[PV_LAM + d:PV_LAM + d + 1]
            rows.append((jnp.sum(dpa, axis=0, keepdims=True), jnp.sum(dpx, axis=0, keepdims=True),
                         -dsp * _sigmoid(-lam)))
            dpab = dpa.astype(BF16)
            dpxb = dpx.astype(BF16)
            dwa_ref[d] += _dot_tn(cbf, dpab)
            dwx_ref[d] += _dot_tn(cbf, dpxb)
            dc = dc + _dot_nt(dpab, wa_ref[d]) + _dot_nt(dpxb, wx_ref[d])

        drec = jnp.zeros_like(c)
        dcw = []
        for j in range(CONV_W):
            drec = drec + pv[PV_CONV_W + j:PV_CONV_W + j + 1] * _shift_rows(dc, j - 2)
            dcw.append(jnp.sum(dc * _shift_rows(rec, 2 - j), axis=0, keepdims=True))
        dz_ref[1] = drec.astype(BF16)
        for j in range(CONV_W):
            dpv_ref[PV_CONV_W + j:PV_CONV_W + j + 1, :] += dcw[j]
        for d in range(2):
            dpv_ref[PV_B_A + d:PV_B_A + d + 1, :] += rows[d][0]
            dpv_ref[PV_B_X + d:PV_B_X + d + 1, :] += rows[d][1]
            dpv_ref[PV_LAM + d:PV_LAM + d + 1, :] += rows[d][2]
        dpv_ref[PV_CONV_B:PV_CONV_B + 1, :] += jnp.sum(dc, axis=0, keepdims=True)

    return pl.pallas_call(
        body, name=name, grid=(nblk, B),
        in_specs=[slab(0), slab(nblk), slab(0), slab(0), slab(0), pvspec, wspec, wspec],
        out_specs=[pl.BlockSpec((2, L, LRU_BW), lambda cb, b: (0, b, cb)), wspec, wspec, pvspec],
        out_shape=[S((2, T, C), BF16), S((2, nblk, LRU_BW, LRU_BW), F32), S((2, nblk, LRU_BW, LRU_BW), F32),
                   S((PV_ROWS, C), F32)],
        scratch_shapes=[pltpu.VMEM((2, L, LRU_BW), F32), pltpu.VMEM((L, LRU_BW), F32),
                        pltpu.VMEM((2, L, LRU_BW), F32), pltpu.VMEM((2, L, LRU_BW), F32)],
        compiler_params=_cp("parallel", "arbitrary"),
    )(z, z, hf, hb, dyg, pvec, wa, wx)


QKV_NB = 512


def _local_step(x3, tgt3, w, fetch, send):
    Bl, L, D = x3.shape
    T = Bl * L
    x = x3.reshape(T, D)
    tgt = tgt3.reshape(T, D)
    gm = [w["g_mix"][i:i + 1] for i in range(2)]
    gl = [w["g_mlp"][i:i + 1] for i in range(2)]

    w0 = fetch(0, ())
    nb_in = w0["w_in"].shape[-1]
    z, h0 = _norm_matmul(x, gm[0], w0["w_in"], "rg_in")
    yg, hf, hb = _rg_fwd(z, w0["pvec"], w["wa"], w["wx"], L, "rg_fwd")
    w1 = fetch(1, (yg,))
    fb = w1["w_up0"].shape[-1]
    x1 = _matmul_res(yg, w1["w_out"], x, "rg_out")
    x2, a0, hm0 = _mlp_fwd(x1, gl[0], w1["w_up0"], w1["w_down0"], "mlp0_fwd")
    w2 = fetch(2, (x2,))
    qkv, h1 = _norm_matmul(x2, gm[1], w2["w_qkv"], "at_qkv")
    cos, sin = _rope_tables(L)
    qn, kn, vb = _qk_prep(qkv, w["qg"], w["kg"], cos, sin, L, "at_prep")
    o = _attn_fwd(qn, kn, vb, L, "at_fwd")
    x3_ = _matmul_res(o, w2["w_o"], x2, "at_out")
    w3 = fetch(3, (x3_,))
    x4, a1, hm1 = _mlp_fwd(x3_, gl[1], w3["w_up1"], w3["w_down1"], "mlp1_fwd")
    dx4, loss, dgf = _final_loss(x4, tgt, w["g_fin"], "loss_head")

    dx3, da1, dob1, dgl1 = _mlp_bwd_dx(x3_, dx4, a1, gl[1], w3["w_up1"], w3["w_down1"], "mlp1_bwd_dx")
    dwu1, dwd1 = _mlp_bwd_dw(hm1, da1, a1, dob1, fb, "mlp1_bwd_dw")
    sent = send(3, dict(w_up1=dwu1, w_down1=dwd1))
    do, dx3b = _matmul_nt(dx3, w2["w_o"], "at_out_bwd", BF16, after=sent)
    dwo = _matmul_tn(o, dx3b[None], QKV_NB, "at_out_dw", blocked=False)
    dq, dk, dv = _attn_bwd(qn, kn, vb, do, o, L, "at_bwd")
    dqkv, dqg, dkg = _qk_prep_bwd(qkv, dq, dk, dv, w["qg"], w["kg"], cos, sin, L, "at_prep_bwd")
    dwqkv = _matmul_tn(h1, dqkv[None], QKV_NB, "at_qkv_dw", blocked=False)
    sent = send(2, dict(w_qkv=dwqkv, w_o=dwo))
    dx2, dgm1 = _nt_normbwd(dqkv[None], w2["w_qkv"], x2, gm[1], dx3, "at_qkv_bwd", after=sent)
    dx1, da0, dob0, dgl0 = _mlp_bwd_dx(x1, dx2, a0, gl[0], w1["w_up0"], w1["w_down0"], "mlp0_bwd_dx")
    dwu0, dwd0 = _mlp_bwd_dw(hm0, da0, a0, dob0, fb, "mlp0_bwd_dw")
    sent = send(1, dict(w_up0=dwu0, w_down0=dwd0))
    dyg, dx1b = _matmul_nt(dx1, w1["w_out"], "rg_out_bwd", F32, after=sent)
    dwout = _matmul_tn(yg, dx1b[None], QKV_NB, "rg_out_dw", blocked=False)
    dz, dwa, dwx, dpv = _rg_bwd(z, hf, hb, dyg, w0["pvec"], w["wa"], w["wx"], L, "rg_bwd")
    dwin = _matmul_tn(h0, dz, nb_in, "rg_in_dw", blocked=True)
    dx0, dgm0 = _nt_normbwd(dz, w0["w_in"], x, gm[0], dx1, "rg_in_bwd")
    send(0, dict(w_in=dwin, w_out=dwout, pvec=dpv, g_mix=[dgm0, dgm1], g_mlp=[dgl0, dgl1], g_fin=dgf,
                 qg=dqg, kg=dkg, wa=dwa, wx=dwx))
    return loss[0, 0], dx0.reshape(Bl, L, D)


MESH = pl.DeviceIdType.MESH
ANY = pl.BlockSpec(memory_space=pl.ANY)
N_PEERS = N_DEV - 1


def _my_place():
    return lax.axis_index("x"), lax.axis_index("y"), lax.axis_index("c")


def _flat(px, py, pc):
    return 4 * px + 2 * py + pc


def _all_gather(shards, name):
    n = len(shards)

    def body(*refs):
        ins, outs = refs[:n], refs[n:2 * n]
        send_sems, recv_sems, local_sems = refs[2 * n:]
        x, y, c = _my_place()
        me, sibling = (x, y, c), (x, y, 1 - c)
        chips = [(1 - x, y), (x, 1 - y), (1 - x, 1 - y)]

        def copy(a, k, block, to, src=None):
            dst = outs[a].at[_flat(*block)]
            return pltpu.make_async_remote_copy(
                src_ref=dst if src is None else src, dst_ref=dst,
                send_sem=send_sems.at[a, k], recv_sem=recv_sems.at[a, k],
                device_id=to, device_id_type=MESH)

        mine = [pltpu.make_async_copy(ins[a], outs[a].at[_flat(*me)], local_sems.at[a]) for a in range(n)]
        for cp in mine:
            cp.start()
        first = []
        for a in range(n):
            first.append(copy(a, 0, me, sibling, src=ins[a]))
            first += [copy(a, 1 + j, me, (*chip, c), src=ins[a]) for j, chip in enumerate(chips)]
        for cp in first:
            cp.start()
        passed = []
        for j, chip in enumerate(chips):
            for a in range(n):
                copy(a, 1 + j, (*chip, c), me).wait_recv()
                fwd = copy(a, 4 + j, (*chip, c), sibling)
                fwd.start()
                passed.append(fwd)
        for a in range(n):
            copy(a, 0, sibling, me).wait_recv()
            for j, chip in enumerate(chips):
                copy(a, 4 + j, (*chip, 1 - c), me).wait_recv()
        for cp in first + passed:
            cp.wait_send()
        for cp in mine:
            cp.wait()

    return pl.pallas_call(
        body, name=name,
        in_specs=[ANY] * n, out_specs=[ANY] * n,
        out_shape=[S((N_DEV,) + s.shape, s.dtype) for s in shards],
        scratch_shapes=[pltpu.SemaphoreType.DMA((n, N_PEERS)), pltpu.SemaphoreType.DMA((n, N_PEERS)),
                        pltpu.SemaphoreType.DMA((n,))],
    )(*shards)


HBM = pl.BlockSpec(memory_space=pltpu.HBM)
SEM = pl.BlockSpec(memory_space=pltpu.SEMAPHORE)
SIDE_EFFECT = pltpu.SideEffectType.DATAFLOW_SIDE_EFFECTING
SEMS_PER_GROUP = 3


def _exchange_copies(srcs, lands, sems, scatter):
    send_sems, recv_sems, local_sems = sems
    x, y, c = _my_place()
    me = _flat(x, y, c)
    remote, local = [], []
    for a in range(len(srcs)):
        for r in range(1, N_DEV):
            peer = (1 - x if r & 4 else x, 1 - y if r & 2 else y, 1 - c if r & 1 else c)
            remote.append(pltpu.make_async_remote_copy(
                src_ref=srcs[a].at[_flat(*peer)] if scatter else srcs[a], dst_ref=lands[a].at[me],
                send_sem=send_sems.at[a * N_PEERS + r - 1], recv_sem=recv_sems.at[a * N_PEERS + r - 1],
                device_id=peer, device_id_type=MESH))
        local.append(pltpu.make_async_copy(srcs[a].at[me] if scatter else srcs[a], lands[a].at[me], local_sems.at[a]))
    return remote, local


def _exchange_start(groups, scatter, name):
    sizes = [len(g) for g in groups]
    srcs = [pltpu.with_memory_space_constraint(a, pltpu.HBM) for g in groups for a in g]
    n = len(srcs)
    lands = [pltpu.with_memory_space_constraint(lax.empty(a.shape if scatter else (N_DEV,) + a.shape, a.dtype), pltpu.HBM)
             for a in srcs]
    n_sem = SEMS_PER_GROUP * len(groups)

    def body(*refs):
        src_refs, land_refs, sem_refs, token = refs[:n], refs[n:2 * n], refs[2 * n:2 * n + n_sem], refs[-1]
        off = 0
        for gi, k in enumerate(sizes):
            remote, local = _exchange_copies(src_refs[off:off + k], land_refs[off:off + k],
                                             sem_refs[SEMS_PER_GROUP * gi:SEMS_PER_GROUP * (gi + 1)], scatter)
            for cp in local + remote:
                cp.start()
            off += k
        token[...] = jnp.zeros_like(token)

    sem_shapes = []
    for k in sizes:
        sem_shapes += [pltpu.SemaphoreType.DMA((k * N_PEERS,)), pltpu.SemaphoreType.DMA((k * N_PEERS,)),
                       pltpu.SemaphoreType.DMA((k,))]
    outs = pl.pallas_call(
        body, name=name,
        out_shape=sem_shapes + [pltpu.HBM(a.shape, a.dtype) for a in srcs + lands] + [S((8, 128), F32)],
        in_specs=[HBM] * (2 * n),
        out_specs=[SEM] * n_sem + [HBM] * (2 * n) + [pl.BlockSpec(memory_space=pltpu.VMEM)],
        input_output_aliases={i: n_sem + i for i in range(2 * n)},
        compiler_params=pltpu.CompilerParams(has_side_effects=SIDE_EFFECT),
    )(*srcs, *lands)
    sems, thru, token = outs[:n_sem], outs[n_sem:n_sem + 2 * n], outs[-1]
    per_group, off = [], 0
    for gi, k in enumerate(sizes):
        per_group.append((sems[SEMS_PER_GROUP * gi:SEMS_PER_GROUP * (gi + 1)], thru[off:off + k], thru[n + off:n + off + k]))
        off += k
    return per_group, token


def _exchange_wait(group, after, scatter, name):
    sems, srcs, lands = group
    k = len(srcs)

    def body(*refs):
        remote, local = _exchange_copies(refs[:k], refs[k:2 * k], refs[2 * k:2 * k + SEMS_PER_GROUP], scatter)
        for cp in remote:
            cp.wait_send()
            cp.wait_recv()
        for cp in local:
            cp.wait()

    outs = pl.pallas_call(
        body, name=name,
        out_shape=[pltpu.HBM(a.shape, a.dtype) for a in list(srcs) + list(lands)],
        in_specs=[HBM] * (2 * k) + [SEM] * SEMS_PER_GROUP + [ANY] * len(after),
        out_specs=[HBM] * (2 * k),
        input_output_aliases={i: i for i in range(2 * k)},
        compiler_params=pltpu.CompilerParams(has_side_effects=SIDE_EFFECT),
    )(*srcs, *lands, *sems, *after)
    return outs[k:]


def _row_tile(rows, cols):
    want = max(16, (128 * 1024) // cols)
    if rows <= want:
        return rows
    t = want - want % 16
    while rows % t:
        t -= 16
    return t


def _sum_parts(parts, name):
    P, R, C = parts.shape
    tr = _row_tile(R, C)

    def body(p_ref, o_ref):
        g = p_ref[0].astype(F32)
        for i in range(1, P):
            g = g + p_ref[i].astype(F32)
        o_ref[...] = g

    return pl.pallas_call(
        body, name=name, grid=(R // tr,),
        in_specs=[pl.BlockSpec((P, tr, C), lambda i: (0, i, 0))],
        out_specs=pl.BlockSpec((tr, C), lambda i: (i, 0)),
        out_shape=S((R, C), F32),
        compiler_params=_cp("parallel"),
    )(parts)


def _adamw(parts, w, m, v, name):
    P, R, C = parts.shape
    tr = _row_tile(R, C)
    c1 = 1.0 - ADAM_B1 ** ADAM_STEP
    c2 = 1.0 - ADAM_B2 ** ADAM_STEP

    def body(p_ref, w_ref, m_ref, v_ref, g_ref, d_ref, mo_ref, vo_ref):
        g = p_ref[0].astype(F32)
        for i in range(1, P):
            g = g + p_ref[i].astype(F32)
        mn = ADAM_B1 * m_ref[...] + (1.0 - ADAM_B1) * g
        vn = ADAM_B2 * v_ref[...] + (1.0 - ADAM_B2) * (g * g)
        g_ref[...] = g
        mo_ref[...] = mn
        vo_ref[...] = vn
        d_ref[...] = (-ADAM_LR) * ((mn / c1) / (jnp.sqrt(vn / c2) + ADAM_EPS) + ADAM_WD * w_ref[...])

    blk = pl.BlockSpec((tr, C), lambda i: (i, 0))
    return pl.pallas_call(
        body, name=name, grid=(R // tr,),
        in_specs=[pl.BlockSpec((P, tr, C), lambda i: (0, i, 0)), blk, blk, blk],
        out_specs=[blk, blk, blk, blk],
        out_shape=[S((R, C), F32)] * 4,
        compiler_params=_cp("parallel"),
    )(parts, w, m, v)


def _adamw_nd(parts, w, m, v, name):
    shp = w.shape
    C = shp[-1]
    outs = _adamw(parts.reshape(parts.shape[0], -1, C), w.reshape(-1, C), m.reshape(-1, C), v.reshape(-1, C), name)
    return [o.reshape(shp) for o in outs]


TILE_ROWS = 8


def _rows8(a):
    r = a.reshape(-1, 128)
    return jnp.pad(r, ((0, (-r.shape[0]) % TILE_ROWS), (0, 0)))


def _pack_rows(arrs):
    return jnp.concatenate([_rows8(a) for a in arrs], axis=0)


def _unpack_rows(rows, like):
    out, r = [], 0
    for a in like:
        n = a.size // 128
        out.append(rows[r:r + n].reshape(a.shape))
        r += n + (-n) % TILE_ROWS
    return out


def _small_pack(cw, ba, bx, lam):
    pad8 = lambda a: jnp.pad(a, ((0, TILE_ROWS - a.shape[0]), (0, 0)))
    return jnp.concatenate([pad8(cw[0, :, 0, :]), pad8(ba[0]), pad8(bx[0]), pad8(lam[0]),
                            jnp.zeros((PV_ROWS - PV_CONV_B, LRU_BW), F32)], axis=0)


def kernel(x, norm_mix_g, norm_mlp_g, rg_w_in, rg_conv_w, rg_conv_b, rg_w_a, rg_b_a, rg_w_x, rg_b_x, rg_lam, rg_w_out, at_w_qkv, at_q_g, at_k_g, at_w_o, mlp_w_up, mlp_w_down, final_g, loss_target, m_norm_mix_g, m_norm_mlp_g, m_rg_w_in, m_rg_conv_w, m_rg_conv_b, m_rg_w_a, m_rg_b_a, m_rg_w_x, m_rg_b_x, m_rg_lam, m_rg_w_out, m_at_w_qkv, m_at_q_g, m_at_k_g, m_at_w_o, m_mlp_w_up, m_mlp_w_down, m_final_g, v_norm_mix_g, v_norm_mlp_g, v_rg_w_in, v_rg_conv_w, v_rg_conv_b, v_rg_w_a, v_rg_b_a, v_rg_w_x, v_rg_b_x, v_rg_lam, v_rg_w_out, v_at_w_qkv, v_at_q_g, v_at_k_g, v_at_w_o, v_mlp_w_up, v_mlp_w_down, v_final_g):
    D = x.shape[-1]
    bf = lambda a: a.astype(BF16)

    sp_w = _small_pack(rg_conv_w, rg_b_a, rg_b_x, rg_lam)
    gathers, _ = _exchange_start(
        [[bf(rg_w_in[0]), sp_w], [bf(rg_w_out[0]), bf(mlp_w_up[0]), bf(mlp_w_down[0])],
         [bf(at_w_qkv[0]), bf(at_w_o[0])], [bf(mlp_w_up[1]), bf(mlp_w_down[1])]], False, "gather_start")

    def fetch(stage, after):
        got = _exchange_wait(gathers[stage], tuple(after), False, "gather_wait%d" % stage)
        if stage == 0:
            g_in, g_sp = got
            pvec = g_sp.transpose(1, 0, 2).reshape(PV_ROWS, D)
            pvec = jnp.concatenate([pvec[:PV_CONV_B], jnp.broadcast_to(rg_conv_b, (PV_ROWS - PV_CONV_B, D))], axis=0)
            return dict(w_in=g_in, pvec=pvec)
        if stage == 1:
            return dict(w_out=got[0].reshape(D, D), w_up0=got[1], w_down0=got[2].reshape(-1, D))
        if stage == 2:
            g_qkv = got[0]
            cols = g_qkv.shape[0] * g_qkv.shape[2]
            return dict(w_qkv=g_qkv.transpose(1, 0, 2).reshape(D, cols // QKV_NB, QKV_NB).transpose(1, 0, 2),
                        w_o=got[1].reshape(D, D))
        return dict(w_up1=got[0], w_down1=got[1].reshape(-1, D))

    scatters = {}

    def send(stage, g):
        if stage == 3:
            arrs = [g["w_up1"], g["w_down1"].reshape(N_DEV, -1, D)]
        elif stage == 2:
            arrs = [g["w_qkv"].reshape(D, N_DEV, -1).transpose(1, 0, 2), g["w_o"].reshape(N_DEV, -1, D)]
        elif stage == 1:
            arrs = [g["w_up0"], g["w_down0"].reshape(N_DEV, -1, D)]
        else:
            rep = _pack_rows(g["g_mix"] + g["g_mlp"] + [g["g_fin"], g["pvec"][PV_CONV_B], g["qg"], g["kg"], g["wa"], g["wx"]])
            assert rep.shape[0] % (TILE_ROWS * N_DEV) == 0
            arrs = [g["w_in"], g["w_out"].reshape(N_DEV, -1, D),
                    g["pvec"].reshape(PV_ROWS, N_DEV, LRU_BW).transpose(1, 0, 2), rep.reshape(N_DEV, -1, 128)]
        (group,), token = _exchange_start([arrs], True, "scatter_start%d" % stage)
        scatters[stage] = (group, token)
        return (token,)

    w = dict(g_mix=norm_mix_g, g_mlp=norm_mlp_g, g_fin=final_g[None], qg=at_q_g, kg=at_k_g,
             wa=bf(rg_w_a[0]), wx=bf(rg_w_x[0]))
    loss, grad_x = _local_step(x, loss_target, w, fetch, send)
    loss = lax.psum(loss, ("x", "y", "c"))

    res = {}
    after = (scatters[0][1],)
    r_up1, r_dn1 = _exchange_wait(scatters[3][0], after, True, "scatter_wait3")
    up1 = _adamw_nd(r_up1, mlp_w_up[1], m_mlp_w_up[1], v_mlp_w_up[1], "adam_mlp_w_up1")
    dn1 = _adamw_nd(r_dn1, mlp_w_down[1], m_mlp_w_down[1], v_mlp_w_down[1], "adam_mlp_w_down1")
    r_qkv, r_o = _exchange_wait(scatters[2][0], (dn1[0],), True, "scatter_wait2")
    res["at_w_qkv"] = _adamw_nd(r_qkv[:, None], at_w_qkv, m_at_w_qkv, v_at_w_qkv, "adam_at_w_qkv")
    res["at_w_o"] = _adamw_nd(r_o[:, None], at_w_o, m_at_w_o, v_at_w_o, "adam_at_w_o")
    r_up0, r_dn0 = _exchange_wait(scatters[1][0], (res["at_w_o"][0],), True, "scatter_wait1")
    up0 = _adamw_nd(r_up0, mlp_w_up[0], m_mlp_w_up[0], v_mlp_w_up[0], "adam_mlp_w_up0")
    dn0 = _adamw_nd(r_dn0, mlp_w_down[0], m_mlp_w_down[0], v_mlp_w_down[0], "adam_mlp_w_down0")
    res["mlp_w_up"] = [jnp.stack([a0, a1]) for a0, a1 in zip(up0, up1)]
    res["mlp_w_down"] = [jnp.stack([a0, a1]) for a0, a1 in zip(dn0, dn1)]
    r_in, r_out, r_sp, r_rep = _exchange_wait(scatters[0][0], (dn0[0],), True, "scatter_wait0")
    res["rg_w_in"] = _adamw_nd(r_in[:, None], rg_w_in, m_rg_w_in, v_rg_w_in, "adam_rg_w_in")
    res["rg_w_out"] = _adamw_nd(r_out[:, None], rg_w_out, m_rg_w_out, v_rg_w_out, "adam_rg_w_out")
    sp_res = _adamw(r_sp, sp_w, _small_pack(m_rg_conv_w, m_rg_b_a, m_rg_b_x, m_rg_lam),
                    _small_pack(v_rg_conv_w, v_rg_b_a, v_rg_b_x, v_rg_lam), "adam_small")
    for o in sp_res:
        res.setdefault("rg_conv_w", []).append(o[PV_CONV_W:PV_CONV_W + CONV_W][None, :, None, :])
        res.setdefault("rg_b_a", []).append(o[PV_B_A:PV_B_A + 2][None])
        res.setdefault("rg_b_x", []).append(o[PV_B_X:PV_B_X + 2][None])
        res.setdefault("rg_lam", []).append(o[PV_LAM:PV_LAM + 2][None])

    rep_like = [norm_mix_g, norm_mlp_g, final_g, rg_conv_b, at_q_g, at_k_g, rg_w_a, rg_w_x]
    rep_sum, = _all_gather([_sum_parts(r_rep, "reduce_replicated")], "gather_replicated")
    m_like = [m_norm_mix_g, m_norm_mlp_g, m_final_g, m_rg_conv_b, m_at_q_g, m_at_k_g, m_rg_w_a, m_rg_w_x]
    v_like = [v_norm_mix_g, v_norm_mlp_g, v_final_g, v_rg_conv_b, v_at_q_g, v_at_k_g, v_rg_w_a, v_rg_w_x]
    rep_res = _adamw(rep_sum.reshape(1, -1, 128), _pack_rows(rep_like), _pack_rows(m_like), _pack_rows(v_like), "adam_replicated")
    rep_names = ["norm_mix_g", "norm_mlp_g", "final_g", "rg_conv_b", "at_q_g", "at_k_g", "rg_w_a", "rg_w_x"]
    for o in rep_res:
        for nm, val in zip(rep_names, _unpack_rows(o, rep_like)):
            res.setdefault(nm, []).append(val)

    order = ["norm_mix_g", "norm_mlp_g", "rg_w_in", "rg_conv_w", "rg_conv_b", "rg_w_a", "rg_b_a", "rg_w_x", "rg_b_x",
             "rg_lam", "rg_w_out", "at_w_qkv", "at_q_g", "at_k_g", "at_w_o", "mlp_w_up", "mlp_w_down", "final_g"]
    return (loss, grad_x, *[res[nm][k] for k in range(4) for nm in order])
```

```python
import functools
import math

import jax
import jax.numpy as jnp
from jax import lax
from jax.experimental import pallas as pl
from jax.experimental.pallas import tpu as pltpu

F32 = jnp.float32
BF16 = jnp.bfloat16
S = jax.ShapeDtypeStruct

EPS = 1e-6
HEAD_DIM = 128
N_KV = 2
GRID_W = 64
ROPE_THETA = 10000.0
LRU_BW = 128
RG_C = 8.0
CONV_W = 4
N_DEV = 8
N_SEG = 8
SCAN_UNROLL = 8
TN_STEP_COLS = 512
VMEM_LIMIT_V7X = 56 * 1024 * 1024
SOFTMAX_SCALE = 1.0 / math.sqrt(HEAD_DIM)
GELU_K = math.sqrt(2.0 / math.pi)
GELU_C = 0.044715

ADAM_LR = 0.001
ADAM_B1 = 0.9
ADAM_B2 = 0.999
ADAM_EPS = 1e-08
ADAM_WD = 0.01
ADAM_STEP = 10

NT = (((1,), (1,)), ((), ()))
TN = (((0,), (0,)), ((), ()))


def _cp(*sem):
    return pltpu.CompilerParams(dimension_semantics=sem, vmem_limit_bytes=VMEM_LIMIT_V7X)


def _rms_r(xv):
    return lax.rsqrt(jnp.mean(xv * xv, axis=-1, keepdims=True) + EPS)


def _rms_bwd(dh, xv, g):
    r = _rms_r(xv)
    xh = xv * r
    dg = jnp.sum(dh * xh, axis=0, keepdims=True)
    dxh = dh * g
    dx = r * (dxh - xh * jnp.mean(dxh * xh, axis=-1, keepdims=True))
    return dx, dg


def _dot(a, b):
    return jnp.dot(a, b, preferred_element_type=F32)


def _dot_nt(a, b):
    return lax.dot_general(a, b, NT, preferred_element_type=F32)


def _dot_tn(a, b):
    return lax.dot_general(a, b, TN, preferred_element_type=F32)


def _norm_matmul(x, g, wblk, name, out_dtype=F32):
    T, D = x.shape
    NB, _, nb = wblk.shape
    tm = min(T, 512)

    def body(x_ref, g_ref, w_ref, o_ref, h_ref):
        xv = x_ref[...]
        hb = (xv * _rms_r(xv) * g_ref[...]).astype(BF16)
        h_ref[...] = hb
        for q in range(NB):
            o_ref[:, q * nb:(q + 1) * nb] = _dot(hb, w_ref[q]).astype(o_ref.dtype)

    return pl.pallas_call(
        body, name=name, grid=(T // tm,),
        in_specs=[pl.BlockSpec((tm, D), lambda i: (i, 0)),
                  pl.BlockSpec((1, D), lambda i: (0, 0)),
                  pl.BlockSpec((NB, D, nb), lambda i: (0, 0, 0))],
        out_specs=[pl.BlockSpec((tm, NB * nb), lambda i: (i, 0)),
                   pl.BlockSpec((tm, D), lambda i: (i, 0))],
        out_shape=[S((T, NB * nb), out_dtype), S((T, D), BF16)],
        compiler_params=_cp("parallel"),
    )(x, g, wblk)


def _matmul_res(a, w, res, name):
    T, K = a.shape
    N = w.shape[1]
    tm = min(T, 512)

    def body(a_ref, w_ref, r_ref, o_ref):
        o_ref[...] = r_ref[...] + _dot(a_ref[...], w_ref[...])

    return pl.pallas_call(
        body, name=name, grid=(T // tm,),
        in_specs=[pl.BlockSpec((tm, K), lambda i: (i, 0)),
                  pl.BlockSpec((K, N), lambda i: (0, 0)),
                  pl.BlockSpec((tm, N), lambda i: (i, 0))],
        out_specs=pl.BlockSpec((tm, N), lambda i: (i, 0)),
        out_shape=S((T, N), F32),
        compiler_params=_cp("parallel"),
    )(a, w, res)


def _matmul_nt(a, w, name, out_dtype, after=()):
    T, N = a.shape
    K = w.shape[0]
    tm = min(T, 512)

    def body(a_ref, w_ref, *rest):
        o_ref, ab_ref = rest[len(after):]
        ab = a_ref[...].astype(BF16)
        ab_ref[...] = ab
        o_ref[...] = _dot_nt(ab, w_ref[...]).astype(o_ref.dtype)

    return pl.pallas_call(
        body, name=name, grid=(T // tm,),
        in_specs=[pl.BlockSpec((tm, N), lambda i: (i, 0)),
                  pl.BlockSpec((K, N), lambda i: (0, 0))] + [pl.BlockSpec(memory_space=pl.ANY)] * len(after),
        out_specs=[pl.BlockSpec((tm, K), lambda i: (i, 0)),
                   pl.BlockSpec((tm, N), lambda i: (i, 0))],
        out_shape=[S((T, K), out_dtype), S((T, N), BF16)],
        compiler_params=_cp("parallel"),
    )(a, w, *after)


def _matmul_tn(a, b3, nb, name, blocked):
    T, M = a.shape
    SB, _, N = b3.shape
    per = N // nb
    NB = SB * per
    tk = min(T, 1024)
    nk = T // tk
    jb = max(1, TN_STEP_COLS // nb) if blocked else 1
    assert per % jb == 0
    if blocked:
        out_spec, out_shape = pl.BlockSpec((jb, M, nb), lambda j, k: (j, 0, 0)), S((NB, M, nb), BF16)
    else:
        assert SB == 1
        out_spec, out_shape = pl.BlockSpec((M, nb), lambda j, k: (0, j)), S((M, N), BF16)

    def body(a_ref, b_ref, o_ref, acc_ref):
        k = pl.program_id(1)

        @pl.when(k == 0)
        def _():
            acc_ref[...] = jnp.zeros_like(acc_ref)

        av = a_ref[...]
        for q in range(jb):
            acc_ref[q] += _dot_tn(av, b_ref[:, q * nb:(q + 1) * nb])

        @pl.when(k == nk - 1)
        def _():
            if blocked:
                o_ref[...] = acc_ref[...].astype(BF16)
            else:
                o_ref[...] = acc_ref[0].astype(BF16)

    return pl.pallas_call(
        body, name=name, grid=(NB // jb, nk),
        in_specs=[pl.BlockSpec((tk, M), lambda j, k: (k, 0)),
                  pl.BlockSpec((None, tk, jb * nb), lambda j, k: ((j * jb) // per, k, ((j * jb) % per) // jb))],
        out_specs=out_spec,
        out_shape=out_shape,
        scratch_shapes=[pltpu.VMEM((jb, M, nb), F32)],
        compiler_params=_cp("parallel", "arbitrary"),
    )(a, b3)


def _nt_normbwd(dz3, wblk, x, g, dres, name, after=()):
    T, D = x.shape
    NB, _, nb = wblk.shape
    SB, _, N = dz3.shape
    per = N // nb
    tm = min(T, 512)

    def body(dz_ref, w_ref, x_ref, g_ref, dr_ref, *rest):
        dx_ref, dg_ref = rest[len(after):]

        @pl.when(pl.program_id(0) == 0)
        def _():
            dg_ref[...] = jnp.zeros_like(dg_ref)

        dh = None
        for q in range(NB):
            cols = slice((q % per) * nb, (q % per + 1) * nb)
            part = _dot_nt(dz_ref[q // per, :, cols], w_ref[q])
            dh = part if dh is None else dh + part
        dx, dg = _rms_bwd(dh, x_ref[...], g_ref[...])
        dx_ref[...] = dr_ref[...] + dx
        dg_ref[...] += dg

    return pl.pallas_call(
        body, name=name, grid=(T // tm,),
        in_specs=[pl.BlockSpec((SB, tm, N), lambda i: (0, i, 0)),
                  pl.BlockSpec((NB, D, nb), lambda i: (0, 0, 0)),
                  pl.BlockSpec((tm, D), lambda i: (i, 0)),
                  pl.BlockSpec((1, D), lambda i: (0, 0)),
                  pl.BlockSpec((tm, D), lambda i: (i, 0))] + [pl.BlockSpec(memory_space=pl.ANY)] * len(after),
        out_specs=[pl.BlockSpec((tm, D), lambda i: (i, 0)),
                   pl.BlockSpec((1, D), lambda i: (0, 0))],
        out_shape=[S((T, D), F32), S((1, D), F32)],
        compiler_params=_cp("arbitrary"),
    )(dz3, wblk, x, g, dres, *after)


def _mlp_fwd(x, g, wup, wdown, name):
    T, D = x.shape
    NB, _, fb = wup.shape
    tm = min(T, 1024)

    def body(x_ref, g_ref, wu_ref, wd_ref, xo_ref, a_ref, h_ref, acc_ref):
        j = pl.program_id(1)

        @pl.when(j == 0)
        def _():
            xv = x_ref[...]
            h_ref[...] = (xv * _rms_r(xv) * g_ref[...]).astype(BF16)
            acc_ref[...] = xv

        a = _dot(h_ref[...], wu_ref[...])
        a_ref[...] = a.astype(BF16)
        u = jnp.maximum(a, 0.0)
        acc_ref[...] += _dot((u * u).astype(BF16), wd_ref[...])

        @pl.when(j == NB - 1)
        def _():
            xo_ref[...] = acc_ref[...]

    return pl.pallas_call(
        body, name=name, grid=(T // tm, NB),
        in_specs=[pl.BlockSpec((tm, D), lambda i, j: (i, 0)),
                  pl.BlockSpec((1, D), lambda i, j: (0, 0)),
                  pl.BlockSpec((None, D, fb), lambda i, j: (j, 0, 0)),
                  pl.BlockSpec((fb, D), lambda i, j: (j, 0))],
        out_specs=[pl.BlockSpec((tm, D), lambda i, j: (i, 0)),
                   pl.BlockSpec((tm, fb), lambda i, j: (i, j)),
                   pl.BlockSpec((tm, D), lambda i, j: (i, 0))],
        out_shape=[S((T, D), F32), S((T, NB * fb), BF16), S((T, D), BF16)],
        scratch_shapes=[pltpu.VMEM((tm, D), F32)],
        compiler_params=_cp("parallel", "arbitrary"),
    )(x, g, wup, wdown)


def _mlp_bwd_dx(x, dout, a, g, wup, wdown, name):
    T, D = x.shape
    NB, _, fb = wup.shape
    tm = min(T, 512)

    def body(x_ref, do_ref, a_ref, g_ref, wu_ref, wd_ref, dx_ref, da_ref, dob_ref, dg_ref, acc_ref):
        i, j = pl.program_id(0), pl.program_id(1)

        @pl.when(j == 0)
        def _():
            dob_ref[...] = do_ref[...].astype(BF16)
            acc_ref[...] = jnp.zeros_like(acc_ref)

        @pl.when((i == 0) & (j == 0))
        def _():
            dg_ref[...] = jnp.zeros_like(dg_ref)

        du2 = _dot_nt(dob_ref[...], wd_ref[...])
        u = jnp.maximum(a_ref[...].astype(F32), 0.0)
        da = (du2 * (2.0 * u)).astype(BF16)
        da_ref[...] = da
        acc_ref[...] += _dot_nt(da, wu_ref[...])

        @pl.when(j == NB - 1)
        def _():
            dx, dg = _rms_bwd(acc_ref[...], x_ref[...], g_ref[...])
            dx_ref[...] = do_ref[...] + dx
            dg_ref[...] += dg

    return pl.pallas_call(
        body, name=name, grid=(T // tm, NB),
        in_specs=[pl.BlockSpec((tm, D), lambda i, j: (i, 0)),
                  pl.BlockSpec((tm, D), lambda i, j: (i, 0)),
                  pl.BlockSpec((tm, fb), lambda i, j: (i, j)),
                  pl.BlockSpec((1, D), lambda i, j: (0, 0)),
                  pl.BlockSpec((None, D, fb), lambda i, j: (j, 0, 0)),
                  pl.BlockSpec((fb, D), lambda i, j: (j, 0))],
        out_specs=[pl.BlockSpec((tm, D), lambda i, j: (i, 0)),
                   pl.BlockSpec((tm, fb), lambda i, j: (i, j)),
                   pl.BlockSpec((tm, D), lambda i, j: (i, 0)),
                   pl.BlockSpec((1, D), lambda i, j: (0, 0))],
        out_shape=[S((T, D), F32), S((T, NB * fb), BF16), S((T, D), BF16), S((1, D), F32)],
        scratch_shapes=[pltpu.VMEM((tm, D), F32)],
        compiler_params=_cp("arbitrary", "arbitrary"),
    )(x, dout, a, g, wup, wdown)


def _mlp_bwd_dw(h, da, a, dob, fb, name):
    T, D = h.shape
    F = a.shape[1]
    NB = F // fb
    tk = min(T, 1024)
    nk = T // tk

    def body(h_ref, da_ref, a_ref, dob_ref, dwu_ref, dwd_ref, au_ref, ad_ref):
        k = pl.program_id(1)

        @pl.when(k == 0)
        def _():
            au_ref[...] = jnp.zeros_like(au_ref)
            ad_ref[...] = jnp.zeros_like(ad_ref)

        au_ref[...] += _dot_tn(h_ref[...], da_ref[...])
        u = jnp.maximum(a_ref[...].astype(F32), 0.0)
        ad_ref[...] += _dot_tn((u * u).astype(BF16), dob_ref[...])

        @pl.when(k == nk - 1)
        def _():
            dwu_ref[...] = au_ref[...].astype(BF16)
            dwd_ref[...] = ad_ref[...].astype(BF16)

    return pl.pallas_call(
        body, name=name, grid=(NB, nk),
        in_specs=[pl.BlockSpec((tk, D), lambda j, k: (k, 0)),
                  pl.BlockSpec((tk, fb), lambda j, k: (k, j)),
                  pl.BlockSpec((tk, fb), lambda j, k: (k, j)),
                  pl.BlockSpec((tk, D), lambda j, k: (k, 0))],
        out_specs=[pl.BlockSpec((None, D, fb), lambda j, k: (j, 0, 0)),
                   pl.BlockSpec((fb, D), lambda j, k: (j, 0))],
        out_shape=[S((NB, D, fb), BF16), S((F, D), BF16)],
        scratch_shapes=[pltpu.VMEM((D, fb), F32), pltpu.VMEM((fb, D), F32)],
        compiler_params=_cp("parallel", "arbitrary"),
    )(h, da, a, dob)


def _final_loss(x, tgt, g, name):
    T, D = x.shape
    tm = min(T, 512)

    def body(x_ref, t_ref, g_ref, dx_ref, loss_ref, dg_ref):
        @pl.when(pl.program_id(0) == 0)
        def _():
            loss_ref[...] = jnp.zeros_like(loss_ref)
            dg_ref[...] = jnp.zeros_like(dg_ref)

        xv = x_ref[...]
        gv = g_ref[...]
        err = xv * _rms_r(xv) * gv - t_ref[...]
        e2 = jnp.sum(jnp.sum(err * err, axis=-1, keepdims=True), axis=0, keepdims=True)
        loss_ref[...] += (0.5 / D) * e2
        dx, dg = _rms_bwd(err * (1.0 / D), xv, gv)
        dx_ref[...] = dx
        dg_ref[...] += dg

    return pl.pallas_call(
        body, name=name, grid=(T // tm,),
        in_specs=[pl.BlockSpec((tm, D), lambda i: (i, 0)),
                  pl.BlockSpec((tm, D), lambda i: (i, 0)),
                  pl.BlockSpec((1, D), lambda i: (0, 0))],
        out_specs=[pl.BlockSpec((tm, D), lambda i: (i, 0)),
                   pl.BlockSpec((1, 128), lambda i: (0, 0)),
                   pl.BlockSpec((1, D), lambda i: (0, 0))],
        out_shape=[S((T, D), F32), S((1, 128), F32), S((1, D), F32)],
        compiler_params=_cp("arbitrary"),
    )(x, tgt, g)


def _rope_tables(L):
    nf = HEAD_DIM // 4
    t = jnp.arange(L, dtype=jnp.int32)
    row = (t // GRID_W).astype(F32)
    col = (t % GRID_W).astype(F32)
    inv = ROPE_THETA ** (-jnp.arange(nf, dtype=F32) / nf)
    ar = row[:, None] * inv
    ac = col[:, None] * inv
    cos = jnp.concatenate([jnp.cos(ar), jnp.cos(ar), jnp.cos(ac), jnp.cos(ac)], axis=-1)
    sin = jnp.concatenate([-jnp.sin(ar), jnp.sin(ar), -jnp.sin(ac), jnp.sin(ac)], axis=-1)
    return cos, sin


def _swap32(x):
    lane = lax.broadcasted_iota(jnp.int32, x.shape, 1)
    up = pltpu.roll(x, HEAD_DIM - 32, 1)
    down = pltpu.roll(x, 32, 1)
    return jnp.where((lane % 64) < 32, up, down)


def _qk_prep(qkv, qg, kg, cos, sin, L, name):
    T, W = qkv.shape
    nh = W // HEAD_DIM - 2 * N_KV
    tm = min(L, 512)
    lb = L // tm

    def body(qkv_ref, qg_ref, kg_ref, cos_ref, sin_ref, q_ref, k_ref, v_ref):
        c = cos_ref[...]
        s = sin_ref[...]
        for h in range(nh + N_KV):
            xh = qkv_ref[:, h * HEAD_DIM:(h + 1) * HEAD_DIM]
            gv = qg_ref[...] if h < nh else kg_ref[...]
            y = xh * _rms_r(xh) * gv
            y = (y * c + _swap32(y) * s).astype(BF16)
            if h < nh:
                q_ref[:, h * HEAD_DIM:(h + 1) * HEAD_DIM] = y
            else:
                k_ref[:, (h - nh) * HEAD_DIM:(h - nh + 1) * HEAD_DIM] = y
        v_ref[...] = qkv_ref[:, (nh + N_KV) * HEAD_DIM:].astype(BF16)

    return pl.pallas_call(
        body, name=name, grid=(T // tm,),
        in_specs=[pl.BlockSpec((tm, W), lambda i: (i, 0)),
                  pl.BlockSpec((1, HEAD_DIM), lambda i: (0, 0)),
                  pl.BlockSpec((1, HEAD_DIM), lambda i: (0, 0)),
                  pl.BlockSpec((tm, HEAD_DIM), lambda i: (i % lb, 0)),
                  pl.BlockSpec((tm, HEAD_DIM), lambda i: (i % lb, 0))],
        out_specs=[pl.BlockSpec((tm, nh * HEAD_DIM), lambda i: (i, 0)),
                   pl.BlockSpec((tm, N_KV * HEAD_DIM), lambda i: (i, 0)),
                   pl.BlockSpec((tm, N_KV * HEAD_DIM), lambda i: (i, 0))],
        out_shape=[S((T, nh * HEAD_DIM), BF16), S((T, N_KV * HEAD_DIM), BF16), S((T, N_KV * HEAD_DIM), BF16)],
        compiler_params=_cp("parallel"),
    )(qkv, qg, kg, cos, sin)


def _qk_prep_bwd(qkv, dq, dk, dv, qg, kg, cos, sin, L, name):
    T, W = qkv.shape
    nh = W // HEAD_DIM - 2 * N_KV
    tm = min(L, 512)
    lb = L // tm

    def body(qkv_ref, dq_ref, dk_ref, dv_ref, qg_ref, kg_ref, cos_ref, sin_ref, dz_ref, dqg_ref, dkg_ref):
        @pl.when(pl.program_id(0) == 0)
        def _():
            dqg_ref[...] = jnp.zeros_like(dqg_ref)
            dkg_ref[...] = jnp.zeros_like(dkg_ref)

        c = cos_ref[...]
        s = sin_ref[...]
        for h in range(nh + N_KV):
            cols = slice(h * HEAD_DIM, (h + 1) * HEAD_DIM)
            if h < nh:
                dout, gv, dg_ref = dq_ref[:, cols], qg_ref[...], dqg_ref
            else:
                kc = slice((h - nh) * HEAD_DIM, (h - nh + 1) * HEAD_DIM)
                dout, gv, dg_ref = dk_ref[:, kc], kg_ref[...], dkg_ref
            dy = dout * c - _swap32(dout) * s
            dx, dg = _rms_bwd(dy, qkv_ref[:, cols], gv)
            dg_ref[...] += dg
            dz_ref[:, cols] = dx.astype(BF16)
        dz_ref[:, (nh + N_KV) * HEAD_DIM:] = dv_ref[...].astype(BF16)

    return pl.pallas_call(
        body, name=name, grid=(T // tm,),
        in_specs=[pl.BlockSpec((tm, W), lambda i: (i, 0)),
                  pl.BlockSpec((tm, nh * HEAD_DIM), lambda i: (i, 0)),
                  pl.BlockSpec((tm, N_KV * HEAD_DIM), lambda i: (i, 0)),
                  pl.BlockSpec((tm, N_KV * HEAD_DIM), lambda i: (i, 0)),
                  pl.BlockSpec((1, HEAD_DIM), lambda i: (0, 0)),
                  pl.BlockSpec((1, HEAD_DIM), lambda i: (0, 0)),
                  pl.BlockSpec((tm, HEAD_DIM), lambda i: (i % lb, 0)),
                  pl.BlockSpec((tm, HEAD_DIM), lambda i: (i % lb, 0))],
        out_specs=[pl.BlockSpec((tm, W), lambda i: (i, 0)),
                   pl.BlockSpec((1, HEAD_DIM), lambda i: (0, 0)),
                   pl.BlockSpec((1, HEAD_DIM), lambda i: (0, 0))],
        out_shape=[S((T, W), BF16), S((1, HEAD_DIM), F32), S((1, HEAD_DIM), F32)],
        compiler_params=_cp("arbitrary"),
    )(qkv, dq, dk, dv, qg, kg, cos, sin)


EXP2_SCALE = SOFTMAX_SCALE * math.log2(math.e)
ATTN_SUB = 128


def _softmax_rows(q, k):
    s = _dot_nt(q, k)
    e = jnp.exp2((s - jnp.max(s, axis=-1, keepdims=True)) * EXP2_SCALE)
    return e, jnp.sum(e, axis=-1, keepdims=True)


def _attn_fwd(q, k, v, L, name):
    T = q.shape[0]
    nh = q.shape[1] // HEAD_DIM
    G = nh // N_KV
    B = T // L
    tq = min(L, 512)
    nq = L // tq
    sub = min(tq, ATTN_SUB)

    def body(q_ref, k_ref, v_ref, o_ref):
        for h in range(tq // sub):
            rows = slice(h * sub, (h + 1) * sub)
            e, l = _softmax_rows(q_ref[rows, :], k_ref[...])
            o_ref[rows, :] = (_dot(e.astype(BF16), v_ref[...]) / l).astype(BF16)

    qspec = pl.BlockSpec((tq, HEAD_DIM), lambda b, kv, g, qi: (b * nq + qi, kv * G + g))
    kspec = pl.BlockSpec((L, HEAD_DIM), lambda b, kv, g, qi: (b, kv))
    return pl.pallas_call(
        body, name=name, grid=(B, N_KV, G, nq),
        in_specs=[qspec, kspec, kspec],
        out_specs=qspec,
        out_shape=S((T, nh * HEAD_DIM), BF16),
        compiler_params=_cp("parallel", "parallel", "parallel", "parallel"),
    )(q, k, v)


def _attn_bwd(q, k, v, do, o, L, name):
    T = q.shape[0]
    nh = q.shape[1] // HEAD_DIM
    G = nh // N_KV
    B = T // L
    tq = min(L, 512)
    nq = L // tq

    sub = min(tq, ATTN_SUB)

    def body(q_ref, k_ref, v_ref, do_ref, o_ref, dq_ref, dk_ref, dv_ref, ds_scr, p_scr):
        first = (pl.program_id(2) == 0) & (pl.program_id(3) == 0)
        last = (pl.program_id(2) == G - 1) & (pl.program_id(3) == nq - 1)

        @pl.when(first)
        def _():
            dk_ref[...] = jnp.zeros_like(dk_ref)
            dv_ref[...] = jnp.zeros_like(dv_ref)

        for h in range(tq // sub):
            rows = slice(h * sub, (h + 1) * sub)
            dov = do_ref[rows, :]
            e, l = _softmax_rows(q_ref[rows, :], k_ref[...])
            p = e * (1.0 / l)
            dsum = jnp.sum(dov.astype(F32) * o_ref[rows, :].astype(F32), axis=-1, keepdims=True)
            ds_scr[rows, :] = (p * (_dot_nt(dov, v_ref[...]) - dsum)).astype(BF16)
            p_scr[rows, :] = p.astype(BF16)
        ds = ds_scr[...]
        dq_ref[...] = _dot(ds, k_ref[...]) * SOFTMAX_SCALE
        dk_ref[...] += _dot_tn(ds, q_ref[...])
        dv_ref[...] += _dot_tn(p_scr[...], do_ref[...])

        @pl.when(last)
        def _():
            dk_ref[...] = dk_ref[...] * SOFTMAX_SCALE

    qspec = pl.BlockSpec((tq, HEAD_DIM), lambda b, kv, g, qi: (b * nq + qi, kv * G + g))
    kspec = pl.BlockSpec((L, HEAD_DIM), lambda b, kv, g, qi: (b, kv))
    return pl.pallas_call(
        body, name=name, grid=(B, N_KV, G, nq),
        in_specs=[qspec, kspec, kspec, qspec, qspec],
        out_specs=[qspec, kspec, kspec],
        out_shape=[S((T, nh * HEAD_DIM), F32), S((T, N_KV * HEAD_DIM), F32), S((T, N_KV * HEAD_DIM), F32)],
        scratch_shapes=[pltpu.VMEM((tq, L), BF16), pltpu.VMEM((tq, L), BF16)],
        compiler_params=_cp("parallel", "parallel", "arbitrary", "arbitrary"),
    )(q, k, v, do, o)


PV_CONV_W = 0
PV_B_A = 8
PV_B_X = 16
PV_LAM = 24
PV_CONV_B = 32
PV_ROWS = 40


def _shift_rows(x, k):
    L = x.shape[0]
    if k == 0:
        return x
    t = lax.broadcasted_iota(jnp.int32, x.shape, 0)
    rolled = pltpu.roll(x, k % L, 0)
    keep = (t >= k) if k > 0 else (t < L + k)
    return jnp.where(keep, rolled, 0.0)


def _conv_taps(rec, pv):
    c = pv[PV_CONV_B:PV_CONV_B + 1]
    for j in range(CONV_W):
        c = c + pv[PV_CONV_W + j:PV_CONV_W + j + 1] * _shift_rows(rec, 2 - j)
    return c


def _sigmoid(x):
    return 0.5 * jnp.tanh(0.5 * x) + 0.5


EXPM1_SERIES_BELOW = 0.03


def _rg_gates(c, cbf, wa, wx, ba, bx, lam):
    r = _sigmoid(_dot(cbf, wa) + ba)
    i = _sigmoid(_dot(cbf, wx) + bx)
    sp = jnp.maximum(-lam, 0.0) + jnp.log1p(jnp.exp(-jnp.abs(lam)))
    la = r * ((-RG_C) * sp)
    a = jnp.exp(la)
    a2 = a * a
    x = la + la
    series = -(x * ((x * (1.0 / 6.0) + 0.5) * x + 1.0))
    om = jnp.where(x > -EXPM1_SERIES_BELOW, series, 1.0 - a2)
    rm = lax.rsqrt(om)
    return r, i, a, om * rm, rm, a2, sp


def _gelu(x):
    t = jnp.tanh(GELU_K * (x + GELU_C * x * x * x))
    return 0.5 * x * (1.0 + t), t


def _scan_pair(af_ref, uf_ref, ab_ref, ub_ref, hf_ref, hb_ref, pf_ref, pb_ref, L):
    ls = L // N_SEG
    zero = jnp.zeros((N_SEG, LRU_BW), F32)
    one = jnp.ones((N_SEG, LRU_BW), F32)

    def steps(tc, carry):
        hf, pf, hb, pb = carry
        for q in range(SCAN_UNROLL):
            t = tc * SCAN_UNROLL + q
            rf = pl.ds(t, N_SEG, stride=ls)
            rb = pl.ds(ls - 1 - t, N_SEG, stride=ls)
            af = af_ref[rf, :]
            hf = af * hf + uf_ref[rf, :]
            pf = pf * af
            hf_ref[rf, :] = hf
            pf_ref[rf, :] = pf
            ab = ab_ref[rb, :]
            hb = ab * hb + ub_ref[rb, :]
            pb = pb * ab
            hb_ref[rb, :] = hb
            pb_ref[rb, :] = pb
        return hf, pf, hb, pb

    hf_e, pf_e, hb_e, pb_e = lax.fori_loop(0, ls // SCAN_UNROLL, steps, (zero, one, zero, one))

    cin = jnp.zeros((1, LRU_BW), F32)
    for s in range(N_SEG):
        rows = slice(s * ls, (s + 1) * ls)
        if s > 0:
            hf_ref[rows, :] = hf_ref[rows, :] + pf_ref[rows, :] * cin
        cin = hf_e[s:s + 1] + pf_e[s:s + 1] * cin
    cin = jnp.zeros((1, LRU_BW), F32)
    for s in reversed(range(N_SEG)):
        rows = slice(s * ls, (s + 1) * ls)
        if s < N_SEG - 1:
            hb_ref[rows, :] = hb_ref[rows, :] + pb_ref[rows, :] * cin
        cin = hb_e[s:s + 1] + pb_e[s:s + 1] * cin


def _rg_specs(L, D, nblk):
    slab = lambda off: pl.BlockSpec((L, LRU_BW), lambda cb, b: (b, off + cb))
    wspec = pl.BlockSpec((2, None, LRU_BW, LRU_BW), lambda cb, b: (0, cb, 0, 0))
    pvspec = pl.BlockSpec((PV_ROWS, LRU_BW), lambda cb, b: (0, cb))
    return slab, wspec, pvspec


def _rg_fwd(z, pvec, wa, wx, L, name):
    T, C2 = z.shape
    C = C2 // 2
    nblk = C // LRU_BW
    B = T // L
    slab, wspec, pvspec = _rg_specs(L, C, nblk)

    def body(gp_ref, rec_ref, pv_ref, wa_ref, wx_ref, yg_ref, hf_ref, hb_ref, a_scr, u_scr, p_scr):
        pv = pv_ref[...]
        c = _conv_taps(rec_ref[...], pv)
        cbf = c.astype(BF16)
        for d in range(2):
            _, i, a, m, _, _, _ = _rg_gates(c, cbf, wa_ref[d], wx_ref[d], pv[PV_B_A + d:PV_B_A + d + 1],
                                      pv[PV_B_X + d:PV_B_X + d + 1], pv[PV_LAM + d:PV_LAM + d + 1])
            a_scr[d] = a
            u_scr[d] = m * (i * c)
        _scan_pair(a_scr.at[0], u_scr.at[0], a_scr.at[1], u_scr.at[1], hf_ref, hb_ref, p_scr.at[0], p_scr.at[1], L)
        gate, _ = _gelu(gp_ref[...])
        yg_ref[...] = ((hf_ref[...] + hb_ref[...]) * gate).astype(BF16)

    return pl.pallas_call(
        body, name=name, grid=(nblk, B),
        in_specs=[slab(0), slab(nblk), pvspec, wspec, wspec],
        out_specs=[slab(0), slab(0), slab(0)],
        out_shape=[S((T, C), BF16), S((T, C), F32), S((T, C), F32)],
        scratch_shapes=[pltpu.VMEM((2, L, LRU_BW), F32)] * 3,
        compiler_params=_cp("parallel", "parallel"),
    )(z, z, pvec, wa, wx)


def _rg_bwd(z, hf, hb, dyg, pvec, wa, wx, L, name):
    T, C2 = z.shape
    C = C2 // 2
    nblk = C // LRU_BW
    B = T // L
    slab, wspec, pvspec = _rg_specs(L, C, nblk)

    def body(gp_ref, rec_ref, hf_ref, hb_ref, dyg_ref, pv_ref, wa_ref, wx_ref,
             dz_ref, dwa_ref, dwx_ref, dpv_ref, a_scr, u_scr, d_scr, p_scr):
        @pl.when(pl.program_id(1) == 0)
        def _():
            dwa_ref[...] = jnp.zeros_like(dwa_ref)
            dwx_ref[...] = jnp.zeros_like(dwx_ref)
            dpv_ref[...] = jnp.zeros_like(dpv_ref)

        pv = pv_ref[...]
        rec = rec_ref[...]
        c = _conv_taps(rec, pv)
        cbf = c.astype(BF16)
        gp = gp_ref[...]
        gate, th = _gelu(gp)
        dgelu = 0.5 * (1.0 + th) + 0.5 * gp * (1.0 - th * th) * GELU_K * (1.0 + 3.0 * GELU_C * gp * gp)
        dyg = dyg_ref[...]
        dz_ref[0] = (dyg * (hf_ref[...] + hb_ref[...]) * dgelu).astype(BF16)
        dy = dyg * gate

        gates = []
        for d in range(2):
            gates.append(_rg_gates(c, cbf, wa_ref[d], wx_ref[d], pv[PV_B_A + d:PV_B_A + d + 1],
                                   pv[PV_B_X + d:PV_B_X + d + 1], pv[PV_LAM + d:PV_LAM + d + 1]))
        a_scr[0] = _shift_rows(gates[1][2], 1)
        a_scr[1] = _shift_rows(gates[0][2], -1)
        u_scr[...] = dy
        _scan_pair(a_scr.at[0], u_scr, a_scr.at[1], u_scr, d_scr.at[1], d_scr.at[0], p_scr.at[0], p_scr.at[1], L)

        dc = jnp.zeros_like(c)
        rows = []
        for d in range(2):
            r, i, a, m, rm, a2, sp = gates[d]
            delta = d_scr[d]
            hnb = _shift_rows(hf_ref[...], 1) if d == 0 else _shift_rows(hb_ref[...], -1)
            da = delta * hnb
            dm = delta * (i * c)
            di = delta * (m * c)
            dc = dc + delta * (m * i)
            dla = da * a - dm * (a2 * rm)
            dpa = (dla * ((-RG_C) * sp)) * (r * (1.0 - r))
            dpx = di * (i * (1.0 - i))
            dsp = (-RG_C) * jnp.sum(dla * r, axis=0, keepdims=True)
            lam = pv[PV_LAM + d:PV_LAM + d + 1]
            rows.append((jnp.sum(dpa, axis=0, keepdims=True), jnp.sum(dpx, axis=0, keepdims=True),
                         -dsp * _sigmoid(-lam)))
            dpab = dpa.astype(BF16)
            dpxb = dpx.astype(BF16)
            dwa_ref[d] += _dot_tn(cbf, dpab)
            dwx_ref[d] += _dot_tn(cbf, dpxb)
            dc = dc + _dot_nt(dpab, wa_ref[d]) + _dot_nt(dpxb, wx_ref[d])

        drec = jnp.zeros_like(c)
        dcw = []
        for j in range(CONV_W):
            drec = drec + pv[PV_CONV_W + j:PV_CONV_W + j + 1] * _shift_rows(dc, j - 2)
            dcw.append(jnp.sum(dc * _shift_rows(rec, 2 - j), axis=0, keepdims=True))
        dz_ref[1] = drec.astype(BF16)
        for j in range(CONV_W):
            dpv_ref[PV_CONV_W + j:PV_CONV_W + j + 1, :] += dcw[j]
        for d in range(2):
            dpv_ref[PV_B_A + d:PV_B_A + d + 1, :] += rows[d][0]
            dpv_ref[PV_B_X + d:PV_B_X + d + 1, :] += rows[d][1]
            dpv_ref[PV_LAM + d:PV_LAM + d + 1, :] += rows[d][2]
        dpv_ref[PV_CONV_B:PV_CONV_B + 1, :] += jnp.sum(dc, axis=0, keepdims=True)

    return pl.pallas_call(
        body, name=name, grid=(nblk, B),
        in_specs=[slab(0), slab(nblk), slab(0), slab(0), slab(0), pvspec, wspec, wspec],
        out_specs=[pl.BlockSpec((2, L, LRU_BW), lambda cb, b: (0, b, cb)), wspec, wspec, pvspec],
        out_shape=[S((2, T, C), BF16), S((2, nblk, LRU_BW, LRU_BW), F32), S((2, nblk, LRU_BW, LRU_BW), F32),
                   S((PV_ROWS, C), F32)],
        scratch_shapes=[pltpu.VMEM((2, L, LRU_BW), F32), pltpu.VMEM((L, LRU_BW), F32),
                        pltpu.VMEM((2, L, LRU_BW), F32), pltpu.VMEM((2, L, LRU_BW), F32)],
        compiler_params=_cp("parallel", "arbitrary"),
    )(z, z, hf, hb, dyg, pvec, wa, wx)


QKV_NB = 512


def _local_step(x3, tgt3, w, fetch, send):
    Bl, L, D = x3.shape
    T = Bl * L
    x = x3.reshape(T, D)
    tgt = tgt3.reshape(T, D)
    gm = [w["g_mix"][i:i + 1] for i in range(2)]
    gl = [w["g_mlp"][i:i + 1] for i in range(2)]

    w0 = fetch(0, ())
    nb_in = w0["w_in"].shape[-1]
    z, h0 = _norm_matmul(x, gm[0], w0["w_in"], "rg_in")
    yg, hf, hb = _rg_fwd(z, w0["pvec"], w["wa"], w["wx"], L, "rg_fwd")
    w1 = fetch(1, (yg,))
    fb = w1["w_up0"].shape[-1]
    x1 = _matmul_res(yg, w1["w_out"], x, "rg_out")
    x2, a0, hm0 = _mlp_fwd(x1, gl[0], w1["w_up0"], w1["w_down0"], "mlp0_fwd")
    w2 = fetch(2, (x2,))
    qkv, h1 = _norm_matmul(x2, gm[1], w2["w_qkv"], "at_qkv")
    cos, sin = _rope_tables(L)
    qn, kn, vb = _qk_prep(qkv, w["qg"], w["kg"], cos, sin, L, "at_prep")
    o = _attn_fwd(qn, kn, vb, L, "at_fwd")
    x3_ = _matmul_res(o, w2["w_o"], x2, "at_out")
    w3 = fetch(3, (x3_,))
    x4, a1, hm1 = _mlp_fwd(x3_, gl[1], w3["w_up1"], w3["w_down1"], "mlp1_fwd")
    dx4, loss, dgf = _final_loss(x4, tgt, w["g_fin"], "loss_head")

    dx3, da1, dob1, dgl1 = _mlp_bwd_dx(x3_, dx4, a1, gl[1], w3["w_up1"], w3["w_down1"], "mlp1_bwd_dx")
    dwu1, dwd1 = _mlp_bwd_dw(hm1, da1, a1, dob1, fb, "mlp1_bwd_dw")
    sent = send(3, dict(w_up1=dwu1, w_down1=dwd1))
    do, dx3b = _matmul_nt(dx3, w2["w_o"], "at_out_bwd", BF16, after=sent)
    dwo = _matmul_tn(o, dx3b[None], QKV_NB, "at_out_dw", blocked=False)
    dq, dk, dv = _attn_bwd(qn, kn, vb, do, o, L, "at_bwd")
    dqkv, dqg, dkg = _qk_prep_bwd(qkv, dq, dk, dv, w["qg"], w["kg"], cos, sin, L, "at_prep_bwd")
    dwqkv = _matmul_tn(h1, dqkv[None], QKV_NB, "at_qkv_dw", blocked=False)
    sent = send(2, dict(w_qkv=dwqkv, w_o=dwo))
    dx2, dgm1 = _nt_normbwd(dqkv[None], w2["w_qkv"], x2, gm[1], dx3, "at_qkv_bwd", after=sent)
    dx1, da0, dob0, dgl0 = _mlp_bwd_dx(x1, dx2, a0, gl[0], w1["w_up0"], w1["w_down0"], "mlp0_bwd_dx")
    dwu0, dwd0 = _mlp_bwd_dw(hm0, da0, a0, dob0, fb, "mlp0_bwd_dw")
    sent = send(1, dict(w_up0=dwu0, w_down0=dwd0))
    dyg, dx1b = _matmul_nt(dx1, w1["w_out"], "rg_out_bwd", F32, after=sent)
    dwout = _matmul_tn(yg, dx1b[None], QKV_NB, "rg_out_dw", blocked=False)
    dz, dwa, dwx, dpv = _rg_bwd(z, hf, hb, dyg, w0["pvec"], w["wa"], w["wx"], L, "rg_bwd")
    dwin = _matmul_tn(h0, dz, nb_in, "rg_in_dw", blocked=True)
    sent = send(0, dict(w_in=dwin, w_out=dwout, pvec=dpv, wa=dwa, wx=dwx))
    dx0, dgm0 = _nt_normbwd(dz, w0["w_in"], x, gm[0], dx1, "rg_in_bwd", after=sent)
    send(-1, dict(g_mix=[dgm0, dgm1], g_mlp=[dgl0, dgl1], g_fin=dgf, conv_b=dpv[PV_CONV_B], qg=dqg, kg=dkg, loss=loss))
    return dx0.reshape(Bl, L, D)


MESH = pl.DeviceIdType.MESH
ANY = pl.BlockSpec(memory_space=pl.ANY)
N_PEERS = N_DEV - 1


def _my_place():
    return lax.axis_index("x"), lax.axis_index("y"), lax.axis_index("c")


def _flat(px, py, pc):
    return 4 * px + 2 * py + pc


def _all_gather(shards, name):
    n = len(shards)

    def body(*refs):
        ins, outs = refs[:n], refs[n:2 * n]
        send_sems, recv_sems, local_sems = refs[2 * n:]
        x, y, c = _my_place()
        me, sibling = (x, y, c), (x, y, 1 - c)
        chips = [(1 - x, y), (x, 1 - y), (1 - x, 1 - y)]

        def copy(a, k, block, to, src=None):
            dst = outs[a].at[_flat(*block)]
            return pltpu.make_async_remote_copy(
                src_ref=dst if src is None else src, dst_ref=dst,
                send_sem=send_sems.at[a, k], recv_sem=recv_sems.at[a, k],
                device_id=to, device_id_type=MESH)

        mine = [pltpu.make_async_copy(ins[a], outs[a].at[_flat(*me)], local_sems.at[a]) for a in range(n)]
        for cp in mine:
            cp.start()
        first = []
        for a in range(n):
            first.append(copy(a, 0, me, sibling, src=ins[a]))
            first += [copy(a, 1 + j, me, (*chip, c), src=ins[a]) for j, chip in enumerate(chips)]
        for cp in first:
            cp.start()
        passed = []
        for j, chip in enumerate(chips):
            for a in range(n):
                copy(a, 1 + j, (*chip, c), me).wait_recv()
                fwd = copy(a, 4 + j, (*chip, c), sibling)
                fwd.start()
                passed.append(fwd)
        for a in range(n):
            copy(a, 0, sibling, me).wait_recv()
            for j, chip in enumerate(chips):
                copy(a, 4 + j, (*chip, 1 - c), me).wait_recv()
        for cp in first + passed:
            cp.wait_send()
        for cp in mine:
            cp.wait()

    return pl.pallas_call(
        body, name=name,
        in_specs=[ANY] * n, out_specs=[ANY] * n,
        out_shape=[S((N_DEV,) + s.shape, s.dtype) for s in shards],
        scratch_shapes=[pltpu.SemaphoreType.DMA((n, N_PEERS)), pltpu.SemaphoreType.DMA((n, N_PEERS)),
                        pltpu.SemaphoreType.DMA((n,))],
    )(*shards)


HBM = pl.BlockSpec(memory_space=pltpu.HBM)
SEM = pl.BlockSpec(memory_space=pltpu.SEMAPHORE)
SIDE_EFFECT = pltpu.SideEffectType.DATAFLOW_SIDE_EFFECTING
SEMS_PER_GROUP = 3


def _exchange_copies(srcs, lands, sems, scatter):
    send_sems, recv_sems, local_sems = sems
    x, y, c = _my_place()
    me = _flat(x, y, c)
    remote, local = [], []
    for a in range(len(srcs)):
        for r in range(1, N_DEV):
            peer = (1 - x if r & 4 else x, 1 - y if r & 2 else y, 1 - c if r & 1 else c)
            remote.append(pltpu.make_async_remote_copy(
                src_ref=srcs[a].at[_flat(*peer)] if scatter else srcs[a], dst_ref=lands[a].at[me],
                send_sem=send_sems.at[a * N_PEERS + r - 1], recv_sem=recv_sems.at[a * N_PEERS + r - 1],
                device_id=peer, device_id_type=MESH))
        local.append(pltpu.make_async_copy(srcs[a].at[me] if scatter else srcs[a], lands[a].at[me], local_sems.at[a]))
    return remote, local


def _exchange_start(groups, scatter, name):
    sizes = [len(g) for g in groups]
    srcs = [pltpu.with_memory_space_constraint(a, pltpu.HBM) for g in groups for a in g]
    n = len(srcs)
    lands = [pltpu.with_memory_space_constraint(lax.empty(a.shape if scatter else (N_DEV,) + a.shape, a.dtype), pltpu.HBM)
             for a in srcs]
    n_sem = SEMS_PER_GROUP * len(groups)

    def body(*refs):
        src_refs, land_refs, sem_refs, token = refs[:n], refs[n:2 * n], refs[2 * n:2 * n + n_sem], refs[-1]
        off = 0
        for gi, k in enumerate(sizes):
            remote, local = _exchange_copies(src_refs[off:off + k], land_refs[off:off + k],
                                             sem_refs[SEMS_PER_GROUP * gi:SEMS_PER_GROUP * (gi + 1)], scatter)
            for cp in local + remote:
                cp.start()
            off += k
        token[...] = jnp.zeros_like(token)

    sem_shapes = []
    for k in sizes:
        sem_shapes += [pltpu.SemaphoreType.DMA((k * N_PEERS,)), pltpu.SemaphoreType.DMA((k * N_PEERS,)),
                       pltpu.SemaphoreType.DMA((k,))]
    outs = pl.pallas_call(
        body, name=name,
        out_shape=sem_shapes + [pltpu.HBM(a.shape, a.dtype) for a in srcs + lands] + [S((8, 128), F32)],
        in_specs=[HBM] * (2 * n),
        out_specs=[SEM] * n_sem + [HBM] * (2 * n) + [pl.BlockSpec(memory_space=pltpu.VMEM)],
        input_output_aliases={i: n_sem + i for i in range(2 * n)},
        compiler_params=pltpu.CompilerParams(has_side_effects=SIDE_EFFECT),
    )(*srcs, *lands)
    sems, thru, token = outs[:n_sem], outs[n_sem:n_sem + 2 * n], outs[-1]
    per_group, off = [], 0
    for gi, k in enumerate(sizes):
        per_group.append((sems[SEMS_PER_GROUP * gi:SEMS_PER_GROUP * (gi + 1)], thru[off:off + k], thru[n + off:n + off + k]))
        off += k
    return per_group, token


def _exchange_wait(group, after, scatter, name):
    sems, srcs, lands = group
    k = len(srcs)

    def body(*refs):
        remote, local = _exchange_copies(refs[:k], refs[k:2 * k], refs[2 * k:2 * k + SEMS_PER_GROUP], scatter)
        for cp in remote:
            cp.wait_send()
            cp.wait_recv()
        for cp in local:
            cp.wait()

    outs = pl.pallas_call(
        body, name=name,
        out_shape=[pltpu.HBM(a.shape, a.dtype) for a in list(srcs) + list(lands)],
        in_specs=[HBM] * (2 * k) + [SEM] * SEMS_PER_GROUP + [ANY] * len(after),
        out_specs=[HBM] * (2 * k),
        input_output_aliases={i: i for i in range(2 * k)},
        compiler_params=pltpu.CompilerParams(has_side_effects=SIDE_EFFECT),
    )(*srcs, *lands, *sems, *after)
    return outs[k:]


def _row_tile(rows, cols):
    want = max(16, (128 * 1024) // cols)
    if rows <= want:
        return rows
    t = want - want % 16
    while rows % t:
        t -= 16
    return t


def _sum_parts(parts, name):
    P, R, C = parts.shape
    tr = _row_tile(R, C)

    def body(p_ref, o_ref):
        g = p_ref[0].astype(F32)
        for i in range(1, P):
            g = g + p_ref[i].astype(F32)
        o_ref[...] = g

    return pl.pallas_call(
        body, name=name, grid=(R // tr,),
        in_specs=[pl.BlockSpec((P, tr, C), lambda i: (0, i, 0))],
        out_specs=pl.BlockSpec((tr, C), lambda i: (i, 0)),
        out_shape=S((R, C), F32),
        compiler_params=_cp("parallel"),
    )(parts)


def _adamw(parts, w, m, v, name):
    P, R, C = parts.shape
    tr = _row_tile(R, C)
    c1 = 1.0 - ADAM_B1 ** ADAM_STEP
    c2 = 1.0 - ADAM_B2 ** ADAM_STEP

    def body(p_ref, w_ref, m_ref, v_ref, g_ref, d_ref, mo_ref, vo_ref):
        g = p_ref[0].astype(F32)
        for i in range(1, P):
            g = g + p_ref[i].astype(F32)
        mn = ADAM_B1 * m_ref[...] + (1.0 - ADAM_B1) * g
        vn = ADAM_B2 * v_ref[...] + (1.0 - ADAM_B2) * (g * g)
        g_ref[...] = g
        mo_ref[...] = mn
        vo_ref[...] = vn
        d_ref[...] = (-ADAM_LR) * ((mn / c1) / (jnp.sqrt(vn / c2) + ADAM_EPS) + ADAM_WD * w_ref[...])

    blk = pl.BlockSpec((tr, C), lambda i: (i, 0))
    return pl.pallas_call(
        body, name=name, grid=(R // tr,),
        in_specs=[pl.BlockSpec((P, tr, C), lambda i: (0, i, 0)), blk, blk, blk],
        out_specs=[blk, blk, blk, blk],
        out_shape=[S((R, C), F32)] * 4,
        compiler_params=_cp("parallel"),
    )(parts, w, m, v)


def _adamw_layer(parts, w3, m3, v3, layer, prev, name):
    P, R, C = parts.shape
    NL = w3.shape[0]
    tr = _row_tile(R, C)
    c1 = 1.0 - ADAM_B1 ** ADAM_STEP
    c2 = 1.0 - ADAM_B2 ** ADAM_STEP
    n_prev = 0 if prev is None else len(prev)

    def body(p_ref, w_ref, m_ref, v_ref, *rest):
        g_ref, d_ref, mo_ref, vo_ref = rest[n_prev:]
        g = p_ref[0].astype(F32)
        for i in range(1, P):
            g = g + p_ref[i].astype(F32)
        mn = ADAM_B1 * m_ref[...] + (1.0 - ADAM_B1) * g
        vn = ADAM_B2 * v_ref[...] + (1.0 - ADAM_B2) * (g * g)
        g_ref[...] = g
        mo_ref[...] = mn
        vo_ref[...] = vn
        d_ref[...] = (-ADAM_LR) * ((mn / c1) / (jnp.sqrt(vn / c2) + ADAM_EPS) + ADAM_WD * w_ref[...])

    blk = pl.BlockSpec((None, tr, C), lambda i: (layer, i, 0))
    return pl.pallas_call(
        body, name=name, grid=(R // tr,),
        in_specs=[pl.BlockSpec((P, tr, C), lambda i: (0, i, 0)), blk, blk, blk] + [ANY] * n_prev,
        out_specs=[blk, blk, blk, blk],
        out_shape=[S((NL, R, C), F32)] * 4,
        input_output_aliases={4 + k: k for k in range(n_prev)},
        compiler_params=_cp("parallel"),
    )(parts, w3, m3, v3, *(prev or ()))


def _adamw_nd(parts, w, m, v, name):
    shp = w.shape
    C = shp[-1]
    outs = _adamw(parts.reshape(parts.shape[0], -1, C), w.reshape(-1, C), m.reshape(-1, C), v.reshape(-1, C), name)
    return [o.reshape(shp) for o in outs]


TILE_ROWS = 8


def _rows8(a):
    r = a.reshape(-1, 128)
    return jnp.pad(r, ((0, (-r.shape[0]) % TILE_ROWS), (0, 0)))


REP_SMALL_ROWS = 128


def _pack_rows(arrs, total=None):
    rows = jnp.concatenate([_rows8(a) for a in arrs], axis=0)
    if total is not None:
        rows = jnp.pad(rows, ((0, total - rows.shape[0]), (0, 0)))
    assert rows.shape[0] % (TILE_ROWS * N_DEV) == 0
    return rows


def _unpack_rows(rows, like):
    out, r = [], 0
    for a in like:
        n = a.size // 128
        out.append(rows[r:r + n].reshape(a.shape))
        r += n + (-n) % TILE_ROWS
    return out


def _small_pack(cw, ba, bx, lam):
    pad8 = lambda a: jnp.pad(a, ((0, TILE_ROWS - a.shape[0]), (0, 0)))
    return jnp.concatenate([pad8(cw[0, :, 0, :]), pad8(ba[0]), pad8(bx[0]), pad8(lam[0]),
                            jnp.zeros((PV_ROWS - PV_CONV_B, LRU_BW), F32)], axis=0)


def kernel(x, norm_mix_g, norm_mlp_g, rg_w_in, rg_conv_w, rg_conv_b, rg_w_a, rg_b_a, rg_w_x, rg_b_x, rg_lam, rg_w_out, at_w_qkv, at_q_g, at_k_g, at_w_o, mlp_w_up, mlp_w_down, final_g, loss_target, m_norm_mix_g, m_norm_mlp_g, m_rg_w_in, m_rg_conv_w, m_rg_conv_b, m_rg_w_a, m_rg_b_a, m_rg_w_x, m_rg_b_x, m_rg_lam, m_rg_w_out, m_at_w_qkv, m_at_q_g, m_at_k_g, m_at_w_o, m_mlp_w_up, m_mlp_w_down, m_final_g, v_norm_mix_g, v_norm_mlp_g, v_rg_w_in, v_rg_conv_w, v_rg_conv_b, v_rg_w_a, v_rg_b_a, v_rg_w_x, v_rg_b_x, v_rg_lam, v_rg_w_out, v_at_w_qkv, v_at_q_g, v_at_k_g, v_at_w_o, v_mlp_w_up, v_mlp_w_down, v_final_g):
    D = x.shape[-1]
    bf = lambda a: a.astype(BF16)

    sp_w = _small_pack(rg_conv_w, rg_b_a, rg_b_x, rg_lam)
    gathers, _ = _exchange_start(
        [[bf(rg_w_in[0]), sp_w], [bf(rg_w_out[0]), bf(mlp_w_up[0]), bf(mlp_w_down[0])],
         [bf(at_w_qkv[0]), bf(at_w_o[0])], [bf(mlp_w_up[1]), bf(mlp_w_down[1])]], False, "gather_start")

    def fetch(stage, after):
        got = _exchange_wait(gathers[stage], tuple(after), False, "gather_wait%d" % stage)
        if stage == 0:
            g_in, g_sp = got
            pvec = g_sp.transpose(1, 0, 2).reshape(PV_ROWS, D)
            pvec = jnp.concatenate([pvec[:PV_CONV_B], jnp.broadcast_to(rg_conv_b, (PV_ROWS - PV_CONV_B, D))], axis=0)
            return dict(w_in=g_in, pvec=pvec)
        if stage == 1:
            return dict(w_out=got[0].reshape(D, D), w_up0=got[1], w_down0=got[2].reshape(-1, D))
        if stage == 2:
            g_qkv = got[0]
            cols = g_qkv.shape[0] * g_qkv.shape[2]
            return dict(w_qkv=g_qkv.transpose(1, 0, 2).reshape(D, cols // QKV_NB, QKV_NB).transpose(1, 0, 2),
                        w_o=got[1].reshape(D, D))
        return dict(w_up1=got[0], w_down1=got[1].reshape(-1, D))

    scatters = {}

    def send(stage, g):
        if stage == 3:
            arrs = [g["w_up1"], g["w_down1"].reshape(N_DEV, -1, D)]
        elif stage == 2:
            arrs = [g["w_qkv"].reshape(D, N_DEV, -1).transpose(1, 0, 2), g["w_o"].reshape(N_DEV, -1, D)]
        elif stage == 1:
            arrs = [g["w_up0"], g["w_down0"].reshape(N_DEV, -1, D)]
        elif stage == 0:
            big = jnp.concatenate([g["wa"].reshape(-1, 128), g["wx"].reshape(-1, 128)], axis=0).astype(BF16)
            arrs = [g["w_in"], g["w_out"].reshape(N_DEV, -1, D),
                    g["pvec"].reshape(PV_ROWS, N_DEV, LRU_BW).transpose(1, 0, 2), big.reshape(N_DEV, -1, 128)]
        else:
            small = _pack_rows(g["g_mix"] + g["g_mlp"] + [g["g_fin"], g["conv_b"], g["qg"], g["kg"], g["loss"]], REP_SMALL_ROWS)
            arrs = [small.reshape(N_DEV, -1, 128)]
        (group,), token = _exchange_start([arrs], True, "scatter_start%d" % (stage % 5))
        scatters[stage] = (group, token)
        return (token,)

    w = dict(g_mix=norm_mix_g, g_mlp=norm_mlp_g, g_fin=final_g[None], qg=at_q_g, kg=at_k_g,
             wa=bf(rg_w_a[0]), wx=bf(rg_w_x[0]))
    grad_x = _local_step(x, loss_target, w, fetch, send)

    res = {}
    r_up1, r_dn1 = _exchange_wait(scatters[3][0], (scatters[-1][1],), True, "scatter_wait3")
    up = _adamw_layer(r_up1, mlp_w_up, m_mlp_w_up, v_mlp_w_up, 1, None, "adam_mlp_w_up1")
    dn = _adamw_layer(r_dn1, mlp_w_down, m_mlp_w_down, v_mlp_w_down, 1, None, "adam_mlp_w_down1")
    r_qkv, r_o = _exchange_wait(scatters[2][0], (dn[0],), True, "scatter_wait2")
    res["at_w_qkv"] = _adamw_nd(r_qkv[:, None], at_w_qkv, m_at_w_qkv, v_at_w_qkv, "adam_at_w_qkv")
    res["at_w_o"] = _adamw_nd(r_o[:, None], at_w_o, m_at_w_o, v_at_w_o, "adam_at_w_o")
    r_up0, r_dn0 = _exchange_wait(scatters[1][0], (res["at_w_o"][0],), True, "scatter_wait1")
    res["mlp_w_up"] = _adamw_layer(r_up0, mlp_w_up, m_mlp_w_up, v_mlp_w_up, 0, up, "adam_mlp_w_up0")
    res["mlp_w_down"] = _adamw_layer(r_dn0, mlp_w_down, m_mlp_w_down, v_mlp_w_down, 0, dn, "adam_mlp_w_down0")
    r_in, r_out, r_sp, r_big = _exchange_wait(scatters[0][0], (res["mlp_w_down"][0],), True, "scatter_wait0")
    res["rg_w_in"] = _adamw_nd(r_in[:, None], rg_w_in, m_rg_w_in, v_rg_w_in, "adam_rg_w_in")
    res["rg_w_out"] = _adamw_nd(r_out[:, None], rg_w_out, m_rg_w_out, v_rg_w_out, "adam_rg_w_out")
    sp_res = _adamw(r_sp, sp_w, _small_pack(m_rg_conv_w, m_rg_b_a, m_rg_b_x, m_rg_lam),
                    _small_pack(v_rg_conv_w, v_rg_b_a, v_rg_b_x, v_rg_lam), "adam_small")
    for o in sp_res:
        res.setdefault("rg_conv_w", []).append(o[PV_CONV_W:PV_CONV_W + CONV_W][None, :, None, :])
        res.setdefault("rg_b_a", []).append(o[PV_B_A:PV_B_A + 2][None])
        res.setdefault("rg_b_x", []).append(o[PV_B_X:PV_B_X + 2][None])
        res.setdefault("rg_lam", []).append(o[PV_LAM:PV_LAM + 2][None])

    r_small, = _exchange_wait(scatters[-1][0], (sp_res[0],), True, "scatter_wait4")
    big_sum, small_sum = _all_gather([_sum_parts(r_big, "reduce_rep_big"), _sum_parts(r_small, "reduce_rep_small")],
                                     "gather_replicated")
    nil = jnp.zeros((1, 128), F32)
    packs = [([rg_w_a, rg_w_x], [m_rg_w_a, m_rg_w_x], [v_rg_w_a, v_rg_w_x], ["rg_w_a", "rg_w_x"], big_sum, None),
             ([norm_mix_g, norm_mlp_g, final_g, rg_conv_b, at_q_g, at_k_g, nil],
              [m_norm_mix_g, m_norm_mlp_g, m_final_g, m_rg_conv_b, m_at_q_g, m_at_k_g, nil],
              [v_norm_mix_g, v_norm_mlp_g, v_final_g, v_rg_conv_b, v_at_q_g, v_at_k_g, nil],
              ["norm_mix_g", "norm_mlp_g", "final_g", "rg_conv_b", "at_q_g", "at_k_g", "loss"], small_sum, REP_SMALL_ROWS)]
    for w_like, m_like, v_like, names, gsum, total in packs:
        outs = _adamw(gsum.reshape(1, -1, 128), _pack_rows(w_like, total), _pack_rows(m_like, total),
                      _pack_rows(v_like, total), "adam_" + names[0])
        for o in outs:
            for nm, val in zip(names, _unpack_rows(o, w_like)):
                res.setdefault(nm, []).append(val)
    loss = res["loss"][0][0, 0]

    order = ["norm_mix_g", "norm_mlp_g", "rg_w_in", "rg_conv_w", "rg_conv_b", "rg_w_a", "rg_b_a", "rg_w_x", "rg_b_x",
             "rg_lam", "rg_w_out", "at_w_qkv", "at_q_g", "at_k_g", "at_w_o", "mlp_w_up", "mlp_w_down", "final_g"]
    return (loss, grad_x, *[res[nm][k] for k in range(4) for nm in order])
```

```python
import functools
import math

import jax
import jax.numpy as jnp
from jax import lax
from jax.experimental import pallas as pl
from jax.experimental.pallas import tpu as pltpu

F32 = jnp.float32
BF16 = jnp.bfloat16
S = jax.ShapeDtypeStruct

EPS = 1e-6
HEAD_DIM = 128
N_KV = 2
GRID_W = 64
ROPE_THETA = 10000.0
LRU_BW = 128
RG_C = 8.0
CONV_W = 4
N_DEV = 8
N_SEG = 8
SCAN_UNROLL = 8
TN_STEP_COLS = 512
VMEM_LIMIT_V7X = 56 * 1024 * 1024
SOFTMAX_SCALE = 1.0 / math.sqrt(HEAD_DIM)
GELU_K = math.sqrt(2.0 / math.pi)
GELU_C = 0.044715

ADAM_LR = 0.001
ADAM_B1 = 0.9
ADAM_B2 = 0.999
ADAM_EPS = 1e-08
ADAM_WD = 0.01
ADAM_STEP = 10

NT = (((1,), (1,)), ((), ()))
TN = (((0,), (0,)), ((), ()))


def _cp(*sem):
    return pltpu.CompilerParams(dimension_semantics=sem, vmem_limit_bytes=VMEM_LIMIT_V7X)


def _rms_r(xv):
    return lax.rsqrt(jnp.mean(xv * xv, axis=-1, keepdims=True) + EPS)


def _rms_bwd(dh, xv, g):
    r = _rms_r(xv)
    xh = xv * r
    dg = jnp.sum(dh * xh, axis=0, keepdims=True)
    dxh = dh * g
    dx = r * (dxh - xh * jnp.mean(dxh * xh, axis=-1, keepdims=True))
    return dx, dg


def _dot(a, b):
    return jnp.dot(a, b, preferred_element_type=F32)


def _dot_nt(a, b):
    return lax.dot_general(a, b, NT, preferred_element_type=F32)


def _dot_tn(a, b):
    return lax.dot_general(a, b, TN, preferred_element_type=F32)


def _norm_matmul(x, g, wblk, name, out_dtype=F32):
    T, D = x.shape
    NB, _, nb = wblk.shape
    tm = min(T, 512)

    def body(x_ref, g_ref, w_ref, o_ref, h_ref):
        xv = x_ref[...]
        hb = (xv * _rms_r(xv) * g_ref[...]).astype(BF16)
        h_ref[...] = hb
        for q in range(NB):
            o_ref[:, q * nb:(q + 1) * nb] = _dot(hb, w_ref[q]).astype(o_ref.dtype)

    return pl.pallas_call(
        body, name=name, grid=(T // tm,),
        in_specs=[pl.BlockSpec((tm, D), lambda i: (i, 0)),
                  pl.BlockSpec((1, D), lambda i: (0, 0)),
                  pl.BlockSpec((NB, D, nb), lambda i: (0, 0, 0))],
        out_specs=[pl.BlockSpec((tm, NB * nb), lambda i: (i, 0)),
                   pl.BlockSpec((tm, D), lambda i: (i, 0))],
        out_shape=[S((T, NB * nb), out_dtype), S((T, D), BF16)],
        compiler_params=_cp("parallel"),
    )(x, g, wblk)


def _matmul_res(a, w, res, name):
    T, K = a.shape
    N = w.shape[1]
    tm = min(T, 512)

    def body(a_ref, w_ref, r_ref, o_ref):
        o_ref[...] = r_ref[...] + _dot(a_ref[...], w_ref[...])

    return pl.pallas_call(
        body, name=name, grid=(T // tm,),
        in_specs=[pl.BlockSpec((tm, K), lambda i: (i, 0)),
                  pl.BlockSpec((K, N), lambda i: (0, 0)),
                  pl.BlockSpec((tm, N), lambda i: (i, 0))],
        out_specs=pl.BlockSpec((tm, N), lambda i: (i, 0)),
        out_shape=S((T, N), F32),
        compiler_params=_cp("parallel"),
    )(a, w, res)


def _matmul_nt(a, w, name, out_dtype, after=()):
    T, N = a.shape
    K = w.shape[0]
    tm = min(T, 512)

    def body(a_ref, w_ref, *rest):
        o_ref, ab_ref = rest[len(after):]
        ab = a_ref[...].astype(BF16)
        ab_ref[...] = ab
        o_ref[...] = _dot_nt(ab, w_ref[...]).astype(o_ref.dtype)

    return pl.pallas_call(
        body, name=name, grid=(T // tm,),
        in_specs=[pl.BlockSpec((tm, N), lambda i: (i, 0)),
                  pl.BlockSpec((K, N), lambda i: (0, 0))] + [pl.BlockSpec(memory_space=pl.ANY)] * len(after),
        out_specs=[pl.BlockSpec((tm, K), lambda i: (i, 0)),
                   pl.BlockSpec((tm, N), lambda i: (i, 0))],
        out_shape=[S((T, K), out_dtype), S((T, N), BF16)],
        compiler_params=_cp("parallel"),
    )(a, w, *after)


def _matmul_tn(a, b3, nb, name, blocked):
    T, M = a.shape
    SB, _, N = b3.shape
    per = N // nb
    NB = SB * per
    tk = min(T, 1024)
    nk = T // tk
    jb = max(1, TN_STEP_COLS // nb) if blocked else 1
    assert per % jb == 0
    if blocked:
        out_spec, out_shape = pl.BlockSpec((jb, M, nb), lambda j, k: (j, 0, 0)), S((NB, M, nb), BF16)
    else:
        assert SB == 1
        out_spec, out_shape = pl.BlockSpec((M, nb), lambda j, k: (0, j)), S((M, N), BF16)

    def body(a_ref, b_ref, o_ref, acc_ref):
        k = pl.program_id(1)

        @pl.when(k == 0)
        def _():
            acc_ref[...] = jnp.zeros_like(acc_ref)

        av = a_ref[...]
        for q in range(jb):
            acc_ref[q] += _dot_tn(av, b_ref[:, q * nb:(q + 1) * nb])

        @pl.when(k == nk - 1)
        def _():
            if blocked:
                o_ref[...] = acc_ref[...].astype(BF16)
            else:
                o_ref[...] = acc_ref[0].astype(BF16)

    return pl.pallas_call(
        body, name=name, grid=(NB // jb, nk),
        in_specs=[pl.BlockSpec((tk, M), lambda j, k: (k, 0)),
                  pl.BlockSpec((None, tk, jb * nb), lambda j, k: ((j * jb) // per, k, ((j * jb) % per) // jb))],
        out_specs=out_spec,
        out_shape=out_shape,
        scratch_shapes=[pltpu.VMEM((jb, M, nb), F32)],
        compiler_params=_cp("parallel", "arbitrary"),
    )(a, b3)


def _nt_normbwd(dz3, wblk, x, g, dres, name, after=()):
    T, D = x.shape
    NB, _, nb = wblk.shape
    SB, _, N = dz3.shape
    per = N // nb
    tm = min(T, 512)

    def body(dz_ref, w_ref, x_ref, g_ref, dr_ref, *rest):
        dx_ref, dg_ref = rest[len(after):]

        @pl.when(pl.program_id(0) == 0)
        def _():
            dg_ref[...] = jnp.zeros_like(dg_ref)

        dh = None
        for q in range(NB):
            cols = slice((q % per) * nb, (q % per + 1) * nb)
            part = _dot_nt(dz_ref[q // per, :, cols], w_ref[q])
            dh = part if dh is None else dh + part
        dx, dg = _rms_bwd(dh, x_ref[...], g_ref[...])
        dx_ref[...] = dr_ref[...] + dx
        dg_ref[...] += dg

    return pl.pallas_call(
        body, name=name, grid=(T // tm,),
        in_specs=[pl.BlockSpec((SB, tm, N), lambda i: (0, i, 0)),
                  pl.BlockSpec((NB, D, nb), lambda i: (0, 0, 0)),
                  pl.BlockSpec((tm, D), lambda i: (i, 0)),
                  pl.BlockSpec((1, D), lambda i: (0, 0)),
                  pl.BlockSpec((tm, D), lambda i: (i, 0))] + [pl.BlockSpec(memory_space=pl.ANY)] * len(after),
        out_specs=[pl.BlockSpec((tm, D), lambda i: (i, 0)),
                   pl.BlockSpec((1, D), lambda i: (0, 0))],
        out_shape=[S((T, D), F32), S((1, D), F32)],
        compiler_params=_cp("arbitrary"),
    )(dz3, wblk, x, g, dres, *after)


def _mlp_fwd(x, g, wup, wdown, name):
    T, D = x.shape
    NB, _, fb = wup.shape
    tm = min(T, 1024)

    def body(x_ref, g_ref, wu_ref, wd_ref, xo_ref, a_ref, h_ref, acc_ref):
        j = pl.program_id(1)

        @pl.when(j == 0)
        def _():
            xv = x_ref[...]
            h_ref[...] = (xv * _rms_r(xv) * g_ref[...]).astype(BF16)
            acc_ref[...] = xv

        a = _dot(h_ref[...], wu_ref[...])
        a_ref[...] = a.astype(BF16)
        u = jnp.maximum(a, 0.0)
        acc_ref[...] += _dot((u * u).astype(BF16), wd_ref[...])

        @pl.when(j == NB - 1)
        def _():
            xo_ref[...] = acc_ref[...]

    return pl.pallas_call(
        body, name=name, grid=(T // tm, NB),
        in_specs=[pl.BlockSpec((tm, D), lambda i, j: (i, 0)),
                  pl.BlockSpec((1, D), lambda i, j: (0, 0)),
                  pl.BlockSpec((None, D, fb), lambda i, j: (j, 0, 0)),
                  pl.BlockSpec((fb, D), lambda i, j: (j, 0))],
        out_specs=[pl.BlockSpec((tm, D), lambda i, j: (i, 0)),
                   pl.BlockSpec((tm, fb), lambda i, j: (i, j)),
                   pl.BlockSpec((tm, D), lambda i, j: (i, 0))],
        out_shape=[S((T, D), F32), S((T, NB * fb), BF16), S((T, D), BF16)],
        scratch_shapes=[pltpu.VMEM((tm, D), F32)],
        compiler_params=_cp("parallel", "arbitrary"),
    )(x, g, wup, wdown)


def _mlp_bwd_dx(x, dout, a, g, wup, wdown, name):
    T, D = x.shape
    NB, _, fb = wup.shape
    tm = min(T, 1024)

    def body(x_ref, do_ref, a_ref, g_ref, wu_ref, wd_ref, dx_ref, da_ref, dob_ref, dg_ref, acc_ref):
        i, j = pl.program_id(0), pl.program_id(1)

        @pl.when(j == 0)
        def _():
            dob_ref[...] = do_ref[...].astype(BF16)
            acc_ref[...] = jnp.zeros_like(acc_ref)

        @pl.when((i == 0) & (j == 0))
        def _():
            dg_ref[...] = jnp.zeros_like(dg_ref)

        du2 = _dot_nt(dob_ref[...], wd_ref[...])
        u = jnp.maximum(a_ref[...].astype(F32), 0.0)
        da = (du2 * (2.0 * u)).astype(BF16)
        da_ref[...] = da
        acc_ref[...] += _dot_nt(da, wu_ref[...])

        @pl.when(j == NB - 1)
        def _():
            dx, dg = _rms_bwd(acc_ref[...], x_ref[...], g_ref[...])
            dx_ref[...] = do_ref[...] + dx
            dg_ref[...] += dg

    return pl.pallas_call(
        body, name=name, grid=(T // tm, NB),
        in_specs=[pl.BlockSpec((tm, D), lambda i, j: (i, 0)),
                  pl.BlockSpec((tm, D), lambda i, j: (i, 0)),
                  pl.BlockSpec((tm, fb), lambda i, j: (i, j)),
                  pl.BlockSpec((1, D), lambda i, j: (0, 0)),
                  pl.BlockSpec((None, D, fb), lambda i, j: (j, 0, 0)),
                  pl.BlockSpec((fb, D), lambda i, j: (j, 0))],
        out_specs=[pl.BlockSpec((tm, D), lambda i, j: (i, 0)),
                   pl.BlockSpec((tm, fb), lambda i, j: (i, j)),
                   pl.BlockSpec((tm, D), lambda i, j: (i, 0)),
                   pl.BlockSpec((1, D), lambda i, j: (0, 0))],
        out_shape=[S((T, D), F32), S((T, NB * fb), BF16), S((T, D), BF16), S((1, D), F32)],
        scratch_shapes=[pltpu.VMEM((tm, D), F32)],
        compiler_params=_cp("arbitrary", "arbitrary"),
    )(x, dout, a, g, wup, wdown)


def _mlp_bwd_dw(h, da, a, dob, fb, name):
    T, D = h.shape
    F = a.shape[1]
    NB = F // fb
    tk = min(T, 1024)
    nk = T // tk

    def body(h_ref, da_ref, a_ref, dob_ref, dwu_ref, dwd_ref, au_ref, ad_ref):
        k = pl.program_id(1)

        @pl.when(k == 0)
        def _():
            au_ref[...] = jnp.zeros_like(au_ref)
            ad_ref[...] = jnp.zeros_like(ad_ref)

        au_ref[...] += _dot_tn(h_ref[...], da_ref[...])
        u = jnp.maximum(a_ref[...].astype(F32), 0.0)
        ad_ref[...] += _dot_tn((u * u).astype(BF16), dob_ref[...])

        @pl.when(k == nk - 1)
        def _():
            dwu_ref[...] = au_ref[...].astype(BF16)
            dwd_ref[...] = ad_ref[...].astype(BF16)

    return pl.pallas_call(
        body, name=name, grid=(NB, nk),
        in_specs=[pl.BlockSpec((tk, D), lambda j, k: (k, 0)),
                  pl.BlockSpec((tk, fb), lambda j, k: (k, j)),
                  pl.BlockSpec((tk, fb), lambda j, k: (k, j)),
                  pl.BlockSpec((tk, D), lambda j, k: (k, 0))],
        out_specs=[pl.BlockSpec((None, D, fb), lambda j, k: (j, 0, 0)),
                   pl.BlockSpec((fb, D), lambda j, k: (j, 0))],
        out_shape=[S((NB, D, fb), BF16), S((F, D), BF16)],
        scratch_shapes=[pltpu.VMEM((D, fb), F32), pltpu.VMEM((fb, D), F32)],
        compiler_params=_cp("parallel", "arbitrary"),
    )(h, da, a, dob)


def _final_loss(x, tgt, g, name):
    T, D = x.shape
    tm = min(T, 512)

    def body(x_ref, t_ref, g_ref, dx_ref, loss_ref, dg_ref):
        @pl.when(pl.program_id(0) == 0)
        def _():
            loss_ref[...] = jnp.zeros_like(loss_ref)
            dg_ref[...] = jnp.zeros_like(dg_ref)

        xv = x_ref[...]
        gv = g_ref[...]
        err = xv * _rms_r(xv) * gv - t_ref[...]
        e2 = jnp.sum(jnp.sum(err * err, axis=-1, keepdims=True), axis=0, keepdims=True)
        loss_ref[...] += (0.5 / D) * e2
        dx, dg = _rms_bwd(err * (1.0 / D), xv, gv)
        dx_ref[...] = dx
        dg_ref[...] += dg

    return pl.pallas_call(
        body, name=name, grid=(T // tm,),
        in_specs=[pl.BlockSpec((tm, D), lambda i: (i, 0)),
                  pl.BlockSpec((tm, D), lambda i: (i, 0)),
                  pl.BlockSpec((1, D), lambda i: (0, 0))],
        out_specs=[pl.BlockSpec((tm, D), lambda i: (i, 0)),
                   pl.BlockSpec((1, 128), lambda i: (0, 0)),
                   pl.BlockSpec((1, D), lambda i: (0, 0))],
        out_shape=[S((T, D), F32), S((1, 128), F32), S((1, D), F32)],
        compiler_params=_cp("arbitrary"),
    )(x, tgt, g)


def _rope_tables(L):
    nf = HEAD_DIM // 4
    t = jnp.arange(L, dtype=jnp.int32)
    row = (t // GRID_W).astype(F32)
    col = (t % GRID_W).astype(F32)
    inv = ROPE_THETA ** (-jnp.arange(nf, dtype=F32) / nf)
    ar = row[:, None] * inv
    ac = col[:, None] * inv
    cos = jnp.concatenate([jnp.cos(ar), jnp.cos(ar), jnp.cos(ac), jnp.cos(ac)], axis=-1)
    sin = jnp.concatenate([-jnp.sin(ar), jnp.sin(ar), -jnp.sin(ac), jnp.sin(ac)], axis=-1)
    return cos, sin


def _swap32(x):
    lane = lax.broadcasted_iota(jnp.int32, x.shape, 1)
    up = pltpu.roll(x, HEAD_DIM - 32, 1)
    down = pltpu.roll(x, 32, 1)
    return jnp.where((lane % 64) < 32, up, down)


def _qk_prep(qkv, qg, kg, cos, sin, L, name):
    T, W = qkv.shape
    nh = W // HEAD_DIM - 2 * N_KV
    tm = min(L, 512)
    lb = L // tm

    def body(qkv_ref, qg_ref, kg_ref, cos_ref, sin_ref, q_ref, k_ref, v_ref):
        c = cos_ref[...]
        s = sin_ref[...]
        for h in range(nh + N_KV):
            xh = qkv_ref[:, h * HEAD_DIM:(h + 1) * HEAD_DIM]
            gv = qg_ref[...] if h < nh else kg_ref[...]
            y = xh * _rms_r(xh) * gv
            y = (y * c + _swap32(y) * s).astype(BF16)
            if h < nh:
                q_ref[:, h * HEAD_DIM:(h + 1) * HEAD_DIM] = y
            else:
                k_ref[:, (h - nh) * HEAD_DIM:(h - nh + 1) * HEAD_DIM] = y
        v_ref[...] = qkv_ref[:, (nh + N_KV) * HEAD_DIM:].astype(BF16)

    return pl.pallas_call(
        body, name=name, grid=(T // tm,),
        in_specs=[pl.BlockSpec((tm, W), lambda i: (i, 0)),
                  pl.BlockSpec((1, HEAD_DIM), lambda i: (0, 0)),
                  pl.BlockSpec((1, HEAD_DIM), lambda i: (0, 0)),
                  pl.BlockSpec((tm, HEAD_DIM), lambda i: (i % lb, 0)),
                  pl.BlockSpec((tm, HEAD_DIM), lambda i: (i % lb, 0))],
        out_specs=[pl.BlockSpec((tm, nh * HEAD_DIM), lambda i: (i, 0)),
                   pl.BlockSpec((tm, N_KV * HEAD_DIM), lambda i: (i, 0)),
                   pl.BlockSpec((tm, N_KV * HEAD_DIM), lambda i: (i, 0))],
        out_shape=[S((T, nh * HEAD_DIM), BF16), S((T, N_KV * HEAD_DIM), BF16), S((T, N_KV * HEAD_DIM), BF16)],
        compiler_params=_cp("parallel"),
    )(qkv, qg, kg, cos, sin)


def _qk_prep_bwd(qkv, dq, dk, dv, qg, kg, cos, sin, L, name):
    T, W = qkv.shape
    nh = W // HEAD_DIM - 2 * N_KV
    tm = min(L, 512)
    lb = L // tm

    def body(qkv_ref, dq_ref, dk_ref, dv_ref, qg_ref, kg_ref, cos_ref, sin_ref, dz_ref, dqg_ref, dkg_ref):
        @pl.when(pl.program_id(0) == 0)
        def _():
            dqg_ref[...] = jnp.zeros_like(dqg_ref)
            dkg_ref[...] = jnp.zeros_like(dkg_ref)

        c = cos_ref[...]
        s = sin_ref[...]
        for h in range(nh + N_KV):
            cols = slice(h * HEAD_DIM, (h + 1) * HEAD_DIM)
            if h < nh:
                dout, gv, dg_ref = dq_ref[:, cols], qg_ref[...], dqg_ref
            else:
                kc = slice((h - nh) * HEAD_DIM, (h - nh + 1) * HEAD_DIM)
                dout, gv, dg_ref = dk_ref[:, kc], kg_ref[...], dkg_ref
            dy = dout * c - _swap32(dout) * s
            dx, dg = _rms_bwd(dy, qkv_ref[:, cols], gv)
            dg_ref[...] += dg
            dz_ref[:, cols] = dx.astype(BF16)
        dz_ref[:, (nh + N_KV) * HEAD_DIM:] = dv_ref[...].astype(BF16)

    return pl.pallas_call(
        body, name=name, grid=(T // tm,),
        in_specs=[pl.BlockSpec((tm, W), lambda i: (i, 0)),
                  pl.BlockSpec((tm, nh * HEAD_DIM), lambda i: (i, 0)),
                  pl.BlockSpec((tm, N_KV * HEAD_DIM), lambda i: (i, 0)),
                  pl.BlockSpec((tm, N_KV * HEAD_DIM), lambda i: (i, 0)),
                  pl.BlockSpec((1, HEAD_DIM), lambda i: (0, 0)),
                  pl.BlockSpec((1, HEAD_DIM), lambda i: (0, 0)),
                  pl.BlockSpec((tm, HEAD_DIM), lambda i: (i % lb, 0)),
                  pl.BlockSpec((tm, HEAD_DIM), lambda i: (i % lb, 0))],
        out_specs=[pl.BlockSpec((tm, W), lambda i: (i, 0)),
                   pl.BlockSpec((1, HEAD_DIM), lambda i: (0, 0)),
                   pl.BlockSpec((1, HEAD_DIM), lambda i: (0, 0))],
        out_shape=[S((T, W), BF16), S((1, HEAD_DIM), F32), S((1, HEAD_DIM), F32)],
        compiler_params=_cp("arbitrary"),
    )(qkv, dq, dk, dv, qg, kg, cos, sin)


EXP2_SCALE = SOFTMAX_SCALE * math.log2(math.e)
ATTN_SUB = 128


def _softmax_rows(q, k):
    s = _dot_nt(q, k)
    e = jnp.exp2((s - jnp.max(s, axis=-1, keepdims=True)) * EXP2_SCALE)
    return e, jnp.sum(e, axis=-1, keepdims=True)


def _attn_fwd(q, k, v, L, name):
    T = q.shape[0]
    nh = q.shape[1] // HEAD_DIM
    G = nh // N_KV
    B = T // L
    tq = min(L, 512)
    nq = L // tq
    sub = min(tq, ATTN_SUB)

    def body(q_ref, k_ref, v_ref, o_ref):
        for h in range(tq // sub):
            rows = slice(h * sub, (h + 1) * sub)
            e, l = _softmax_rows(q_ref[rows, :], k_ref[...])
            o_ref[rows, :] = (_dot(e.astype(BF16), v_ref[...]) / l).astype(BF16)

    qspec = pl.BlockSpec((tq, HEAD_DIM), lambda b, kv, g, qi: (b * nq + qi, kv * G + g))
    kspec = pl.BlockSpec((L, HEAD_DIM), lambda b, kv, g, qi: (b, kv))
    return pl.pallas_call(
        body, name=name, grid=(B, N_KV, G, nq),
        in_specs=[qspec, kspec, kspec],
        out_specs=qspec,
        out_shape=S((T, nh * HEAD_DIM), BF16),
        compiler_params=_cp("parallel", "parallel", "parallel", "parallel"),
    )(q, k, v)


def _attn_bwd(q, k, v, do, o, L, name):
    T = q.shape[0]
    nh = q.shape[1] // HEAD_DIM
    G = nh // N_KV
    B = T // L
    tq = min(L, 512)
    nq = L // tq

    sub = min(tq, ATTN_SUB)

    def body(q_ref, k_ref, v_ref, do_ref, o_ref, dq_ref, dk_ref, dv_ref, ds_scr, p_scr):
        first = (pl.program_id(2) == 0) & (pl.program_id(3) == 0)
        last = (pl.program_id(2) == G - 1) & (pl.program_id(3) == nq - 1)

        @pl.when(first)
        def _():
            dk_ref[...] = jnp.zeros_like(dk_ref)
            dv_ref[...] = jnp.zeros_like(dv_ref)

        for h in range(tq // sub):
            rows = slice(h * sub, (h + 1) * sub)
            dov = do_ref[rows, :]
            e, l = _softmax_rows(q_ref[rows, :], k_ref[...])
            p = e * (1.0 / l)
            dsum = jnp.sum(dov.astype(F32) * o_ref[rows, :].astype(F32), axis=-1, keepdims=True)
            ds_scr[rows, :] = (p * (_dot_nt(dov, v_ref[...]) - dsum)).astype(BF16)
            p_scr[rows, :] = p.astype(BF16)
        ds = ds_scr[...]
        dq_ref[...] = _dot(ds, k_ref[...]) * SOFTMAX_SCALE
        dk_ref[...] += _dot_tn(ds, q_ref[...])
        dv_ref[...] += _dot_tn(p_scr[...], do_ref[...])

        @pl.when(last)
        def _():
            dk_ref[...] = dk_ref[...] * SOFTMAX_SCALE

    qspec = pl.BlockSpec((tq, HEAD_DIM), lambda b, kv, g, qi: (b * nq + qi, kv * G + g))
    kspec = pl.BlockSpec((L, HEAD_DIM), lambda b, kv, g, qi: (b, kv))
    return pl.pallas_call(
        body, name=name, grid=(B, N_KV, G, nq),
        in_specs=[qspec, kspec, kspec, qspec, qspec],
        out_specs=[qspec, kspec, kspec],
        out_shape=[S((T, nh * HEAD_DIM), F32), S((T, N_KV * HEAD_DIM), F32), S((T, N_KV * HEAD_DIM), F32)],
        scratch_shapes=[pltpu.VMEM((tq, L), BF16), pltpu.VMEM((tq, L), BF16)],
        compiler_params=_cp("parallel", "parallel", "arbitrary", "arbitrary"),
    )(q, k, v, do, o)


PV_CONV_W = 0
PV_B_A = 8
PV_B_X = 16
PV_LAM = 24
PV_CONV_B = 32
PV_ROWS = 40


def _shift_rows(x, k):
    if k == 0:
        return x
    L = x.shape[0]
    n = N_SEG * abs(k)
    seg = lax.broadcasted_iota(jnp.int32, (n, x.shape[1]), 0) % N_SEG
    if k > 0:
        edge = jnp.where(seg == 0, 0.0, pltpu.roll(x[L - n:], 1, 0))
        return jnp.concatenate([edge, x[:L - n]], axis=0)
    edge = jnp.where(seg == N_SEG - 1, 0.0, pltpu.roll(x[:n], n - 1, 0))
    return jnp.concatenate([x[n:], edge], axis=0)


def _conv_taps(rec, pv):
    c = pv[PV_CONV_B:PV_CONV_B + 1]
    for j in range(CONV_W):
        c = c + pv[PV_CONV_W + j:PV_CONV_W + j + 1] * _shift_rows(rec, 2 - j)
    return c


def _sigmoid(x):
    return 0.5 * jnp.tanh(0.5 * x) + 0.5


EXPM1_SERIES_BELOW = 0.03


def _rg_gates(c, cbf, wa, wx, ba, bx, lam):
    r = _sigmoid(_dot(cbf, wa) + ba)
    i = _sigmoid(_dot(cbf, wx) + bx)
    sp = jnp.maximum(-lam, 0.0) + jnp.log1p(jnp.exp(-jnp.abs(lam)))
    la = r * ((-RG_C) * sp)
    a = jnp.exp(la)
    a2 = a * a
    x = la + la
    series = -(x * ((x * (1.0 / 6.0) + 0.5) * x + 1.0))
    om = jnp.where(x > -EXPM1_SERIES_BELOW, series, 1.0 - a2)
    rm = lax.rsqrt(om)
    return r, i, a, om * rm, rm, a2, sp


def _gelu(x):
    t = jnp.tanh(GELU_K * (x + GELU_C * x * x * x))
    return 0.5 * x * (1.0 + t), t


def _scan_pair(af_ref, uf_ref, ab_ref, ub_ref, hf_ref, hb_ref, pf_ref, pb_ref, L):
    ls = L // N_SEG
    zero = jnp.zeros((N_SEG, LRU_BW), F32)
    one = jnp.ones((N_SEG, LRU_BW), F32)
    tile = lambda t: pl.ds(pl.multiple_of(t * N_SEG, N_SEG), N_SEG)

    def steps(tc, carry):
        hf, pf, hb, pb = carry
        for q in range(SCAN_UNROLL):
            t = tc * SCAN_UNROLL + q
            rf, rb = tile(t), tile(ls - 1 - t)
            af = af_ref[rf, :]
            hf = af * hf + uf_ref[rf, :]
            pf = pf * af
            hf_ref[rf, :] = hf
            pf_ref[rf, :] = pf
            ab = ab_ref[rb, :]
            hb = ab * hb + ub_ref[rb, :]
            pb = pb * ab
            hb_ref[rb, :] = hb
            pb_ref[rb, :] = pb
        return hf, pf, hb, pb

    hf_e, pf_e, hb_e, pb_e = lax.fori_loop(0, ls // SCAN_UNROLL, steps, (zero, one, zero, one))

    rows, cin = [], jnp.zeros((1, LRU_BW), F32)
    for s in range(N_SEG):
        rows.append(cin)
        cin = hf_e[s:s + 1] + pf_e[s:s + 1] * cin
    cf = jnp.concatenate(rows, axis=0)
    rows, cin = [], jnp.zeros((1, LRU_BW), F32)
    for s in reversed(range(N_SEG)):
        rows.append(cin)
        cin = hb_e[s:s + 1] + pb_e[s:s + 1] * cin
    cb = jnp.concatenate(rows[::-1], axis=0)

    def fix(tc, _):
        for q in range(SCAN_UNROLL):
            r = tile(tc * SCAN_UNROLL + q)
            hf_ref[r, :] = hf_ref[r, :] + pf_ref[r, :] * cf
            hb_ref[r, :] = hb_ref[r, :] + pb_ref[r, :] * cb
        return 0

    lax.fori_loop(0, ls // SCAN_UNROLL, fix, 0)


def _rg_specs(L, D, nblk):
    slab = lambda off: pl.BlockSpec((L, LRU_BW), lambda cb, b: (b, off + cb))
    wspec = pl.BlockSpec((2, None, LRU_BW, LRU_BW), lambda cb, b: (0, cb, 0, 0))
    pvspec = pl.BlockSpec((PV_ROWS, LRU_BW), lambda cb, b: (0, cb))
    return slab, wspec, pvspec


def _rg_fwd(z, pvec, wa, wx, L, name):
    T, C2 = z.shape
    C = C2 // 2
    nblk = C // LRU_BW
    B = T // L
    slab, wspec, pvspec = _rg_specs(L, C, nblk)

    def body(gp_ref, rec_ref, pv_ref, wa_ref, wx_ref, yg_ref, hf_ref, hb_ref, a_scr, u_scr, p_scr):
        pv = pv_ref[...]
        c = _conv_taps(rec_ref[...], pv)
        cbf = c.astype(BF16)
        for d in range(2):
            _, i, a, m, _, _, _ = _rg_gates(c, cbf, wa_ref[d], wx_ref[d], pv[PV_B_A + d:PV_B_A + d + 1],
                                      pv[PV_B_X + d:PV_B_X + d + 1], pv[PV_LAM + d:PV_LAM + d + 1])
            a_scr[d] = a
            u_scr[d] = m * (i * c)
        _scan_pair(a_scr.at[0], u_scr.at[0], a_scr.at[1], u_scr.at[1], hf_ref, hb_ref, p_scr.at[0], p_scr.at[1], L)
        gate, _ = _gelu(gp_ref[...])
        yg_ref[...] = ((hf_ref[...] + hb_ref[...]) * gate).astype(BF16)

    return pl.pallas_call(
        body, name=name, grid=(nblk, B),
        in_specs=[slab(0), slab(nblk), pvspec, wspec, wspec],
        out_specs=[slab(0), slab(0), slab(0)],
        out_shape=[S((T, C), BF16), S((T, C), F32), S((T, C), F32)],
        scratch_shapes=[pltpu.VMEM((2, L, LRU_BW), F32)] * 3,
        compiler_params=_cp("parallel", "parallel"),
    )(z, z, pvec, wa, wx)


def _rg_bwd(z, hf, hb, dyg, pvec, wa, wx, L, name):
    T, C2 = z.shape
    C = C2 // 2
    nblk = C // LRU_BW
    B = T // L
    slab, wspec, pvspec = _rg_specs(L, C, nblk)

    def body(gp_ref, rec_ref, hf_ref, hb_ref, dyg_ref, pv_ref, wa_ref, wx_ref,
             dz_ref, dwa_ref, dwx_ref, dpv_ref, a_scr, u_scr, d_scr, p_scr):
        @pl.when(pl.program_id(1) == 0)
        def _():
            dwa_ref[...] = jnp.zeros_like(dwa_ref)
            dwx_ref[...] = jnp.zeros_like(dwx_ref)
            dpv_ref[...] = jnp.zeros_like(dpv_ref)

        pv = pv_ref[...]
        rec = rec_ref[...]
        c = _conv_taps(rec, pv)
        cbf = c.astype(BF16)
        gp = gp_ref[...]
        gate, th = _gelu(gp)
        dgelu = 0.5 * (1.0 + th) + 0.5 * gp * (1.0 - th * th) * GELU_K * (1.0 + 3.0 * GELU_C * gp * gp)
        dyg = dyg_ref[...]
        dz_ref[0] = (dyg * (hf_ref[...] + hb_ref[...]) * dgelu).astype(BF16)
        dy = dyg * gate

        gates = []
        for d in range(2):
            gates.append(_rg_gates(c, cbf, wa_ref[d], wx_ref[d], pv[PV_B_A + d:PV_B_A + d + 1],
                                   pv[PV_B_X + d:PV_B_X + d + 1], pv[PV_LAM + d:PV_LAM + d + 1]))
        a_scr[0] = _shift_rows(gates[1][2], 1)
        a_scr[1] = _shift_rows(gates[0][2], -1)
        u_scr[...] = dy
        _scan_pair(a_scr.at[0], u_scr, a_scr.at[1], u_scr, d_scr.at[1], d_scr.at[0], p_scr.at[0], p_scr.at[1], L)

        dc = jnp.zeros_like(c)
        rows = []
        for d in range(2):
            r, i, a, m, rm, a2, sp = gates[d]
            delta = d_scr[d]
            hnb = _shift_rows(hf_ref[...], 1) if d == 0 else _shift_rows(hb_ref[...], -1)
            da = delta * hnb
            dm = delta * (i * c)
            di = delta * (m * c)
            dc = dc + delta * (m * i)
            dla = da * a - dm * (a2 * rm)
            dpa = (dla * ((-RG_C) * sp)) * (r * (1.0 - r))
            dpx = di * (i * (1.0 - i))
            dsp = (-RG_C) * jnp.sum(dla * r, axis=0, keepdims=True)
            lam = pv[PV_LAM + d:PV_LAM + d + 1]
            rows.append((jnp.sum(dpa, axis=0, keepdims=True), jnp.sum(dpx, axis=0, keepdims=True),
                         -dsp * _sigmoid(-lam)))
            dpab = dpa.astype(BF16)
            dpxb = dpx.astype(BF16)
            dwa_ref[d] += _dot_tn(cbf, dpab)
            dwx_ref[d] += _dot_tn(cbf, dpxb)
            dc = dc + _dot_nt(dpab, wa_ref[d]) + _dot_nt(dpxb, wx_ref[d])

        drec = jnp.zeros_like(c)
        dcw = []
        for j in range(CONV_W):
            drec = drec + pv[PV_CONV_W + j:PV_CONV_W + j + 1] * _shift_rows(dc, j - 2)
            dcw.append(jnp.sum(dc * _shift_rows(rec, 2 - j), axis=0, keepdims=True))
        dz_ref[1] = drec.astype(BF16)
        for j in range(CONV_W):
            dpv_ref[PV_CONV_W + j:PV_CONV_W + j + 1, :] += dcw[j]
        for d in range(2):
            dpv_ref[PV_B_A + d:PV_B_A + d + 1, :] += rows[d][0]
            dpv_ref[PV_B_X + d:PV_B_X + d + 1, :] += rows[d][1]
            dpv_ref[PV_LAM + d:PV_LAM + d + 1, :] += rows[d][2]
        dpv_ref[PV_CONV_B:PV_CONV_B + 1, :] += jnp.sum(dc, axis=0, keepdims=True)

    return pl.pallas_call(
        body, name=name, grid=(nblk, B),
        in_specs=[slab(0), slab(nblk), slab(0), slab(0), slab(0), pvspec, wspec, wspec],
        out_specs=[pl.BlockSpec((2, L, LRU_BW), lambda cb, b: (0, b, cb)), wspec, wspec, pvspec],
        out_shape=[S((2, T, C), BF16), S((2, nblk, LRU_BW, LRU_BW), F32), S((2, nblk, LRU_BW, LRU_BW), F32),
                   S((PV_ROWS, C), F32)],
        scratch_shapes=[pltpu.VMEM((2, L, LRU_BW), F32), pltpu.VMEM((L, LRU_BW), F32),
                        pltpu.VMEM((2, L, LRU_BW), F32), pltpu.VMEM((2, L, LRU_BW), F32)],
        compiler_params=_cp("parallel", "arbitrary"),
    )(z, z, hf, hb, dyg, pvec, wa, wx)


QKV_NB = 512


def _interleave(a):
    *lead, L, D = a.shape
    return a.reshape(*lead, N_SEG, L // N_SEG, D).swapaxes(-3, -2).reshape(*lead, L, D)


def _deinterleave(a):
    *lead, L, D = a.shape
    return a.reshape(*lead, L // N_SEG, N_SEG, D).swapaxes(-3, -2).reshape(*lead, L, D)


def _local_step(x3, tgt3, w, fetch, send):
    Bl, L, D = x3.shape
    T = Bl * L
    x = _interleave(x3).reshape(T, D)
    tgt = _interleave(tgt3).reshape(T, D)
    gm = [w["g_mix"][i:i + 1] for i in range(2)]
    gl = [w["g_mlp"][i:i + 1] for i in range(2)]

    w0 = fetch(0, ())
    nb_in = w0["w_in"].shape[-1]
    z, h0 = _norm_matmul(x, gm[0], w0["w_in"], "rg_in")
    yg, hf, hb = _rg_fwd(z, w0["pvec"], w["wa"], w["wx"], L, "rg_fwd")
    w1 = fetch(1, (yg,))
    fb = w1["w_up0"].shape[-1]
    x1 = _matmul_res(yg, w1["w_out"], x, "rg_out")
    x2, a0, hm0 = _mlp_fwd(x1, gl[0], w1["w_up0"], w1["w_down0"], "mlp0_fwd")
    w2 = fetch(2, (x2,))
    qkv, h1 = _norm_matmul(x2, gm[1], w2["w_qkv"], "at_qkv")
    cos, sin = [_interleave(t) for t in _rope_tables(L)]
    qn, kn, vb = _qk_prep(qkv, w["qg"], w["kg"], cos, sin, L, "at_prep")
    o = _attn_fwd(qn, kn, vb, L, "at_fwd")
    x3_ = _matmul_res(o, w2["w_o"], x2, "at_out")
    w3 = fetch(3, (x3_,))
    x4, a1, hm1 = _mlp_fwd(x3_, gl[1], w3["w_up1"], w3["w_down1"], "mlp1_fwd")
    dx4, loss, dgf = _final_loss(x4, tgt, w["g_fin"], "loss_head")

    dx3, da1, dob1, dgl1 = _mlp_bwd_dx(x3_, dx4, a1, gl[1], w3["w_up1"], w3["w_down1"], "mlp1_bwd_dx")
    dwu1, dwd1 = _mlp_bwd_dw(hm1, da1, a1, dob1, fb, "mlp1_bwd_dw")
    sent = send(3, dict(w_up1=dwu1, w_down1=dwd1))
    do, dx3b = _matmul_nt(dx3, w2["w_o"], "at_out_bwd", BF16, after=sent)
    dwo = _matmul_tn(o, dx3b[None], QKV_NB, "at_out_dw", blocked=False)
    dq, dk, dv = _attn_bwd(qn, kn, vb, do, o, L, "at_bwd")
    dqkv, dqg, dkg = _qk_prep_bwd(qkv, dq, dk, dv, w["qg"], w["kg"], cos, sin, L, "at_prep_bwd")
    dwqkv = _matmul_tn(h1, dqkv[None], QKV_NB, "at_qkv_dw", blocked=False)
    sent = send(2, dict(w_qkv=dwqkv, w_o=dwo))
    dx2, dgm1 = _nt_normbwd(dqkv[None], w2["w_qkv"], x2, gm[1], dx3, "at_qkv_bwd", after=sent)
    dx1, da0, dob0, dgl0 = _mlp_bwd_dx(x1, dx2, a0, gl[0], w1["w_up0"], w1["w_down0"], "mlp0_bwd_dx")
    dwu0, dwd0 = _mlp_bwd_dw(hm0, da0, a0, dob0, fb, "mlp0_bwd_dw")
    sent = send(1, dict(w_up0=dwu0, w_down0=dwd0))
    dyg, dx1b = _matmul_nt(dx1, w1["w_out"], "rg_out_bwd", F32, after=sent)
    dwout = _matmul_tn(yg, dx1b[None], QKV_NB, "rg_out_dw", blocked=False)
    dz, dwa, dwx, dpv = _rg_bwd(z, hf, hb, dyg, w0["pvec"], w["wa"], w["wx"], L, "rg_bwd")
    dwin = _matmul_tn(h0, dz, nb_in, "rg_in_dw", blocked=True)
    sent = send(0, dict(w_in=dwin, w_out=dwout, pvec=dpv, wa=dwa, wx=dwx))
    dx0, dgm0 = _nt_normbwd(dz, w0["w_in"], x, gm[0], dx1, "rg_in_bwd", after=sent)
    send(-1, dict(g_mix=[dgm0, dgm1], g_mlp=[dgl0, dgl1], g_fin=dgf, conv_b=dpv[PV_CONV_B], qg=dqg, kg=dkg, loss=loss))
    return _deinterleave(dx0.reshape(Bl, L, D))


MESH = pl.DeviceIdType.MESH
ANY = pl.BlockSpec(memory_space=pl.ANY)
N_PEERS = N_DEV - 1


def _my_place():
    return lax.axis_index("x"), lax.axis_index("y"), lax.axis_index("c")


def _flat(px, py, pc):
    return 4 * px + 2 * py + pc


def _all_gather(shards, name):
    n = len(shards)

    def body(*refs):
        ins, outs = refs[:n], refs[n:2 * n]
        send_sems, recv_sems, local_sems = refs[2 * n:]
        x, y, c = _my_place()
        me, sibling = (x, y, c), (x, y, 1 - c)
        chips = [(1 - x, y), (x, 1 - y), (1 - x, 1 - y)]

        def copy(a, k, block, to, src=None):
            dst = outs[a].at[_flat(*block)]
            return pltpu.make_async_remote_copy(
                src_ref=dst if src is None else src, dst_ref=dst,
                send_sem=send_sems.at[a, k], recv_sem=recv_sems.at[a, k],
                device_id=to, device_id_type=MESH)

        mine = [pltpu.make_async_copy(ins[a], outs[a].at[_flat(*me)], local_sems.at[a]) for a in range(n)]
        for cp in mine:
            cp.start()
        first = []
        for a in range(n):
            first.append(copy(a, 0, me, sibling, src=ins[a]))
            first += [copy(a, 1 + j, me, (*chip, c), src=ins[a]) for j, chip in enumerate(chips)]
        for cp in first:
            cp.start()
        passed = []
        for j, chip in enumerate(chips):
            for a in range(n):
                copy(a, 1 + j, (*chip, c), me).wait_recv()
                fwd = copy(a, 4 + j, (*chip, c), sibling)
                fwd.start()
                passed.append(fwd)
        for a in range(n):
            copy(a, 0, sibling, me).wait_recv()
            for j, chip in enumerate(chips):
                copy(a, 4 + j, (*chip, 1 - c), me).wait_recv()
        for cp in first + passed:
            cp.wait_send()
        for cp in mine:
            cp.wait()

    return pl.pallas_call(
        body, name=name,
        in_specs=[ANY] * n, out_specs=[ANY] * n,
        out_shape=[S((N_DEV,) + s.shape, s.dtype) for s in shards],
        scratch_shapes=[pltpu.SemaphoreType.DMA((n, N_PEERS)), pltpu.SemaphoreType.DMA((n, N_PEERS)),
                        pltpu.SemaphoreType.DMA((n,))],
    )(*shards)


HBM = pl.BlockSpec(memory_space=pltpu.HBM)
SEM = pl.BlockSpec(memory_space=pltpu.SEMAPHORE)
SIDE_EFFECT = pltpu.SideEffectType.DATAFLOW_SIDE_EFFECTING
SEMS_PER_GROUP = 3


def _exchange_copies(srcs, lands, sems, scatter):
    send_sems, recv_sems, local_sems = sems
    x, y, c = _my_place()
    me = _flat(x, y, c)
    remote, local = [], []
    for a in range(len(srcs)):
        for r in range(1, N_DEV):
            peer = (1 - x if r & 4 else x, 1 - y if r & 2 else y, 1 - c if r & 1 else c)
            remote.append(pltpu.make_async_remote_copy(
                src_ref=srcs[a].at[_flat(*peer)] if scatter else srcs[a], dst_ref=lands[a].at[me],
                send_sem=send_sems.at[a * N_PEERS + r - 1], recv_sem=recv_sems.at[a * N_PEERS + r - 1],
                device_id=peer, device_id_type=MESH))
        local.append(pltpu.make_async_copy(srcs[a].at[me] if scatter else srcs[a], lands[a].at[me], local_sems.at[a]))
    return remote, local


def _exchange_start(groups, scatter, name):
    sizes = [len(g) for g in groups]
    srcs = [pltpu.with_memory_space_constraint(a, pltpu.HBM) for g in groups for a in g]
    n = len(srcs)
    lands = [pltpu.with_memory_space_constraint(lax.empty(a.shape if scatter else (N_DEV,) + a.shape, a.dtype), pltpu.HBM)
             for a in srcs]
    n_sem = SEMS_PER_GROUP * len(groups)

    def body(*refs):
        src_refs, land_refs, sem_refs, token = refs[:n], refs[n:2 * n], refs[2 * n:2 * n + n_sem], refs[-1]
        off = 0
        for gi, k in enumerate(sizes):
            remote, local = _exchange_copies(src_refs[off:off + k], land_refs[off:off + k],
                                             sem_refs[SEMS_PER_GROUP * gi:SEMS_PER_GROUP * (gi + 1)], scatter)
            for cp in local + remote:
                cp.start()
            off += k
        token[...] = jnp.zeros_like(token)

    sem_shapes = []
    for k in sizes:
        sem_shapes += [pltpu.SemaphoreType.DMA((k * N_PEERS,)), pltpu.SemaphoreType.DMA((k * N_PEERS,)),
                       pltpu.SemaphoreType.DMA((k,))]
    outs = pl.pallas_call(
        body, name=name,
        out_shape=sem_shapes + [pltpu.HBM(a.shape, a.dtype) for a in srcs + lands] + [S((8, 128), F32)],
        in_specs=[HBM] * (2 * n),
        out_specs=[SEM] * n_sem + [HBM] * (2 * n) + [pl.BlockSpec(memory_space=pltpu.VMEM)],
        input_output_aliases={i: n_sem + i for i in range(2 * n)},
        compiler_params=pltpu.CompilerParams(has_side_effects=SIDE_EFFECT),
    )(*srcs, *lands)
    sems, thru, token = outs[:n_sem], outs[n_sem:n_sem + 2 * n], outs[-1]
    per_group, off = [], 0
    for gi, k in enumerate(sizes):
        per_group.append((sems[SEMS_PER_GROUP * gi:SEMS_PER_GROUP * (gi + 1)], thru[off:off + k], thru[n + off:n + off + k]))
        off += k
    return per_group, token


def _exchange_wait(group, after, scatter, name):
    sems, srcs, lands = group
    k = len(srcs)

    def body(*refs):
        remote, local = _exchange_copies(refs[:k], refs[k:2 * k], refs[2 * k:2 * k + SEMS_PER_GROUP], scatter)
        for cp in remote:
            cp.wait_send()
            cp.wait_recv()
        for cp in local:
            cp.wait()

    outs = pl.pallas_call(
        body, name=name,
        out_shape=[pltpu.HBM(a.shape, a.dtype) for a in list(srcs) + list(lands)],
        in_specs=[HBM] * (2 * k) + [SEM] * SEMS_PER_GROUP + [ANY] * len(after),
        out_specs=[HBM] * (2 * k),
        input_output_aliases={i: i for i in range(2 * k)},
        compiler_params=pltpu.CompilerParams(has_side_effects=SIDE_EFFECT),
    )(*srcs, *lands, *sems, *after)
    return outs[k:]


def _row_tile(rows, cols):
    want = max(16, (128 * 1024) // cols)
    if rows <= want:
        return rows
    t = want - want % 16
    while rows % t:
        t -= 16
    return t


def _sum_parts(parts, name):
    P, R, C = parts.shape
    tr = _row_tile(R, C)

    def body(p_ref, o_ref):
        g = p_ref[0].astype(F32)
        for i in range(1, P):
            g = g + p_ref[i].astype(F32)
        o_ref[...] = g

    return pl.pallas_call(
        body, name=name, grid=(R // tr,),
        in_specs=[pl.BlockSpec((P, tr, C), lambda i: (0, i, 0))],
        out_specs=pl.BlockSpec((tr, C), lambda i: (i, 0)),
        out_shape=S((R, C), F32),
        compiler_params=_cp("parallel"),
    )(parts)


def _adamw(parts, w, m, v, name):
    P, R, C = parts.shape
    tr = _row_tile(R, C)
    c1 = 1.0 - ADAM_B1 ** ADAM_STEP
    c2 = 1.0 - ADAM_B2 ** ADAM_STEP

    def body(p_ref, w_ref, m_ref, v_ref, g_ref, d_ref, mo_ref, vo_ref):
        g = p_ref[0].astype(F32)
        for i in range(1, P):
            g = g + p_ref[i].astype(F32)
        mn = ADAM_B1 * m_ref[...] + (1.0 - ADAM_B1) * g
        vn = ADAM_B2 * v_ref[...] + (1.0 - ADAM_B2) * (g * g)
        g_ref[...] = g
        mo_ref[...] = mn
        vo_ref[...] = vn
        d_ref[...] = (-ADAM_LR) * ((mn / c1) / (jnp.sqrt(vn / c2) + ADAM_EPS) + ADAM_WD * w_ref[...])

    blk = pl.BlockSpec((tr, C), lambda i: (i, 0))
    return pl.pallas_call(
        body, name=name, grid=(R // tr,),
        in_specs=[pl.BlockSpec((P, tr, C), lambda i: (0, i, 0)), blk, blk, blk],
        out_specs=[blk, blk, blk, blk],
        out_shape=[S((R, C), F32)] * 4,
        compiler_params=_cp("parallel"),
    )(parts, w, m, v)


def _adamw_layer(parts, w3, m3, v3, layer, prev, name):
    P, R, C = parts.shape
    NL = w3.shape[0]
    tr = _row_tile(R, C)
    c1 = 1.0 - ADAM_B1 ** ADAM_STEP
    c2 = 1.0 - ADAM_B2 ** ADAM_STEP
    n_prev = 0 if prev is None else len(prev)

    def body(p_ref, w_ref, m_ref, v_ref, *rest):
        g_ref, d_ref, mo_ref, vo_ref = rest[n_prev:]
        g = p_ref[0].astype(F32)
        for i in range(1, P):
            g = g + p_ref[i].astype(F32)
        mn = ADAM_B1 * m_ref[...] + (1.0 - ADAM_B1) * g
        vn = ADAM_B2 * v_ref[...] + (1.0 - ADAM_B2) * (g * g)
        g_ref[...] = g
        mo_ref[...] = mn
        vo_ref[...] = vn
        d_ref[...] = (-ADAM_LR) * ((mn / c1) / (jnp.sqrt(vn / c2) + ADAM_EPS) + ADAM_WD * w_ref[...])

    blk = pl.BlockSpec((None, tr, C), lambda i: (layer, i, 0))
    return pl.pallas_call(
        body, name=name, grid=(R // tr,),
        in_specs=[pl.BlockSpec((P, tr, C), lambda i: (0, i, 0)), blk, blk, blk] + [ANY] * n_prev,
        out_specs=[blk, blk, blk, blk],
        out_shape=[S((NL, R, C), F32)] * 4,
        input_output_aliases={4 + k: k for k in range(n_prev)},
        compiler_params=_cp("parallel"),
    )(parts, w3, m3, v3, *(prev or ()))


def _adamw_nd(parts, w, m, v, name):
    shp = w.shape
    C = shp[-1]
    outs = _adamw(parts.reshape(parts.shape[0], -1, C), w.reshape(-1, C), m.reshape(-1, C), v.reshape(-1, C), name)
    return [o.reshape(shp) for o in outs]


TILE_ROWS = 8


def _rows8(a):
    r = a.reshape(-1, 128)
    return jnp.pad(r, ((0, (-r.shape[0]) % TILE_ROWS), (0, 0)))


REP_SMALL_ROWS = 128


def _pack_rows(arrs, total=None):
    rows = jnp.concatenate([_rows8(a) for a in arrs], axis=0)
    if total is not None:
        rows = jnp.pad(rows, ((0, total - rows.shape[0]), (0, 0)))
    assert rows.shape[0] % (TILE_ROWS * N_DEV) == 0
    return rows


def _unpack_rows(rows, like):
    out, r = [], 0
    for a in like:
        n = a.size // 128
        out.append(rows[r:r + n].reshape(a.shape))
        r += n + (-n) % TILE_ROWS
    return out


def _small_pack(cw, ba, bx, lam):
    pad8 = lambda a: jnp.pad(a, ((0, TILE_ROWS - a.shape[0]), (0, 0)))
    return jnp.concatenate([pad8(cw[0, :, 0, :]), pad8(ba[0]), pad8(bx[0]), pad8(lam[0]),
                            jnp.zeros((PV_ROWS - PV_CONV_B, LRU_BW), F32)], axis=0)


def kernel(x, norm_mix_g, norm_mlp_g, rg_w_in, rg_conv_w, rg_conv_b, rg_w_a, rg_b_a, rg_w_x, rg_b_x, rg_lam, rg_w_out, at_w_qkv, at_q_g, at_k_g, at_w_o, mlp_w_up, mlp_w_down, final_g, loss_target, m_norm_mix_g, m_norm_mlp_g, m_rg_w_in, m_rg_conv_w, m_rg_conv_b, m_rg_w_a, m_rg_b_a, m_rg_w_x, m_rg_b_x, m_rg_lam, m_rg_w_out, m_at_w_qkv, m_at_q_g, m_at_k_g, m_at_w_o, m_mlp_w_up, m_mlp_w_down, m_final_g, v_norm_mix_g, v_norm_mlp_g, v_rg_w_in, v_rg_conv_w, v_rg_conv_b, v_rg_w_a, v_rg_b_a, v_rg_w_x, v_rg_b_x, v_rg_lam, v_rg_w_out, v_at_w_qkv, v_at_q_g, v_at_k_g, v_at_w_o, v_mlp_w_up, v_mlp_w_down, v_final_g):
    D = x.shape[-1]
    bf = lambda a: a.astype(BF16)

    sp_w = _small_pack(rg_conv_w, rg_b_a, rg_b_x, rg_lam)
    gathers, _ = _exchange_start(
        [[bf(rg_w_in[0]), sp_w], [bf(rg_w_out[0]), bf(mlp_w_up[0]), bf(mlp_w_down[0])],
         [bf(at_w_qkv[0]), bf(at_w_o[0])], [bf(mlp_w_up[1]), bf(mlp_w_down[1])]], False, "gather_start")

    def fetch(stage, after):
        got = _exchange_wait(gathers[stage], tuple(after), False, "gather_wait%d" % stage)
        if stage == 0:
            g_in, g_sp = got
            pvec = g_sp.transpose(1, 0, 2).reshape(PV_ROWS, D)
            pvec = jnp.concatenate([pvec[:PV_CONV_B], jnp.broadcast_to(rg_conv_b, (PV_ROWS - PV_CONV_B, D))], axis=0)
            return dict(w_in=g_in, pvec=pvec)
        if stage == 1:
            return dict(w_out=got[0].reshape(D, D), w_up0=got[1], w_down0=got[2].reshape(-1, D))
        if stage == 2:
            g_qkv = got[0]
            cols = g_qkv.shape[0] * g_qkv.shape[2]
            return dict(w_qkv=g_qkv.transpose(1, 0, 2).reshape(D, cols // QKV_NB, QKV_NB).transpose(1, 0, 2),
                        w_o=got[1].reshape(D, D))
        return dict(w_up1=got[0], w_down1=got[1].reshape(-1, D))

    scatters = {}

    def send(stage, g):
        if stage == 3:
            arrs = [g["w_up1"], g["w_down1"].reshape(N_DEV, -1, D)]
        elif stage == 2:
            arrs = [g["w_qkv"].reshape(D, N_DEV, -1).transpose(1, 0, 2), g["w_o"].reshape(N_DEV, -1, D)]
        elif stage == 1:
            arrs = [g["w_up0"], g["w_down0"].reshape(N_DEV, -1, D)]
        elif stage == 0:
            arrs = [g["w_in"], g["w_out"].reshape(N_DEV, -1, D),
                    g["pvec"].reshape(PV_ROWS, N_DEV, LRU_BW).transpose(1, 0, 2),
                    bf(g["wa"]).reshape(N_DEV, -1, 128), bf(g["wx"]).reshape(N_DEV, -1, 128)]
        else:
            small = _pack_rows(g["g_mix"] + g["g_mlp"] + [g["g_fin"], g["conv_b"], g["qg"], g["kg"], g["loss"]], REP_SMALL_ROWS)
            arrs = [small.reshape(N_DEV, -1, 128)]
        (group,), token = _exchange_start([arrs], True, "scatter_start%d" % (stage % 5))
        scatters[stage] = (group, token)
        return (token,)

    w = dict(g_mix=norm_mix_g, g_mlp=norm_mlp_g, g_fin=final_g[None], qg=at_q_g, kg=at_k_g,
             wa=bf(rg_w_a[0]), wx=bf(rg_w_x[0]))
    grad_x = _local_step(x, loss_target, w, fetch, send)

    res = {}
    r_up1, r_dn1 = _exchange_wait(scatters[3][0], (scatters[-1][1],), True, "scatter_wait3")
    up = _adamw_layer(r_up1, mlp_w_up, m_mlp_w_up, v_mlp_w_up, 1, None, "adam_mlp_w_up1")
    dn = _adamw_layer(r_dn1, mlp_w_down, m_mlp_w_down, v_mlp_w_down, 1, None, "adam_mlp_w_down1")
    r_qkv, r_o = _exchange_wait(scatters[2][0], (dn[0],), True, "scatter_wait2")
    res["at_w_qkv"] = _adamw_nd(r_qkv[:, None], at_w_qkv, m_at_w_qkv, v_at_w_qkv, "adam_at_w_qkv")
    res["at_w_o"] = _adamw_nd(r_o[:, None], at_w_o, m_at_w_o, v_at_w_o, "adam_at_w_o")
    r_up0, r_dn0 = _exchange_wait(scatters[1][0], (res["at_w_o"][0],), True, "scatter_wait1")
    res["mlp_w_up"] = _adamw_layer(r_up0, mlp_w_up, m_mlp_w_up, v_mlp_w_up, 0, up, "adam_mlp_w_up0")
    res["mlp_w_down"] = _adamw_layer(r_dn0, mlp_w_down, m_mlp_w_down, v_mlp_w_down, 0, dn, "adam_mlp_w_down0")
    r_in, r_out, r_sp, r_wa, r_wx = _exchange_wait(scatters[0][0], (res["mlp_w_down"][0],), True, "scatter_wait0")
    res["rg_w_in"] = _adamw_nd(r_in[:, None], rg_w_in, m_rg_w_in, v_rg_w_in, "adam_rg_w_in")
    res["rg_w_out"] = _adamw_nd(r_out[:, None], rg_w_out, m_rg_w_out, v_rg_w_out, "adam_rg_w_out")
    sp_res = _adamw(r_sp, sp_w, _small_pack(m_rg_conv_w, m_rg_b_a, m_rg_b_x, m_rg_lam),
                    _small_pack(v_rg_conv_w, v_rg_b_a, v_rg_b_x, v_rg_lam), "adam_small")
    for o in sp_res:
        res.setdefault("rg_conv_w", []).append(o[PV_CONV_W:PV_CONV_W + CONV_W][None, :, None, :])
        res.setdefault("rg_b_a", []).append(o[PV_B_A:PV_B_A + 2][None])
        res.setdefault("rg_b_x", []).append(o[PV_B_X:PV_B_X + 2][None])
        res.setdefault("rg_lam", []).append(o[PV_LAM:PV_LAM + 2][None])

    r_small, = _exchange_wait(scatters[-1][0], (sp_res[0],), True, "scatter_wait4")
    wa_sum, wx_sum, small_sum = _all_gather(
        [_sum_parts(r_wa, "reduce_w_a"), _sum_parts(r_wx, "reduce_w_x"), _sum_parts(r_small, "reduce_rep_small")],
        "gather_replicated")
    rows = lambda a: a.reshape(-1, 128)
    res["rg_w_a"] = [o.reshape(rg_w_a.shape) for o in _adamw(
        wa_sum.reshape(1, -1, 128), rows(rg_w_a), rows(m_rg_w_a), rows(v_rg_w_a), "adam_rg_w_a")]
    res["rg_w_x"] = [o.reshape(rg_w_x.shape) for o in _adamw(
        wx_sum.reshape(1, -1, 128), rows(rg_w_x), rows(m_rg_w_x), rows(v_rg_w_x), "adam_rg_w_x")]
    nil = jnp.zeros((1, 128), F32)
    w_like = [norm_mix_g, norm_mlp_g, final_g, rg_conv_b, at_q_g, at_k_g, nil]
    m_like = [m_norm_mix_g, m_norm_mlp_g, m_final_g, m_rg_conv_b, m_at_q_g, m_at_k_g, nil]
    v_like = [v_norm_mix_g, v_norm_mlp_g, v_final_g, v_rg_conv_b, v_at_q_g, v_at_k_g, nil]
    names = ["norm_mix_g", "norm_mlp_g", "final_g", "rg_conv_b", "at_q_g", "at_k_g", "loss"]
    outs = _adamw(small_sum.reshape(1, -1, 128), _pack_rows(w_like, REP_SMALL_ROWS), _pack_rows(m_like, REP_SMALL_ROWS),
                  _pack_rows(v_like, REP_SMALL_ROWS), "adam_rep_small")
    for o in outs:
        for nm, val in zip(names, _unpack_rows(o, w_like)):
            res.setdefault(nm, []).append(val)
    loss = res["loss"][0][0, 0]

    order = ["norm_mix_g", "norm_mlp_g", "rg_w_in", "rg_conv_w", "rg_conv_b", "rg_w_a", "rg_b_a", "rg_w_x", "rg_b_x",
             "rg_lam", "rg_w_out", "at_w_qkv", "at_q_g", "at_k_g", "at_w_o", "mlp_w_up", "mlp_w_down", "final_g"]
    return (loss, grad_x, *[res[nm][k] for k in range(4) for nm in order])
```

```python
import functools
import math

import jax
import jax.numpy as jnp
from jax import lax
from jax.experimental import pallas as pl
from jax.experimental.pallas import tpu as pltpu

F32 = jnp.float32
BF16 = jnp.bfloat16
S = jax.ShapeDtypeStruct

EPS = 1e-6
HEAD_DIM = 128
N_KV = 2
GRID_W = 64
ROPE_THETA = 10000.0
LRU_BW = 128
RG_C = 8.0
CONV_W = 4
N_DEV = 8
N_SEG = 8
SCAN_UNROLL = 8
TN_STEP_COLS = 512
VMEM_LIMIT_V7X = 56 * 1024 * 1024
SOFTMAX_SCALE = 1.0 / math.sqrt(HEAD_DIM)
GELU_K = math.sqrt(2.0 / math.pi)
GELU_C = 0.044715

ADAM_LR = 0.001
ADAM_B1 = 0.9
ADAM_B2 = 0.999
ADAM_EPS = 1e-08
ADAM_WD = 0.01
ADAM_STEP = 10

NT = (((1,), (1,)), ((), ()))
TN = (((0,), (0,)), ((), ()))


def _cp(*sem):
    return pltpu.CompilerParams(dimension_semantics=sem, vmem_limit_bytes=VMEM_LIMIT_V7X)


def _rms_r(xv):
    return lax.rsqrt(jnp.mean(xv * xv, axis=-1, keepdims=True) + EPS)


def _rms_bwd(dh, xv, g):
    r = _rms_r(xv)
    xh = xv * r
    dg = jnp.sum(dh * xh, axis=0, keepdims=True)
    dxh = dh * g
    dx = r * (dxh - xh * jnp.mean(dxh * xh, axis=-1, keepdims=True))
    return dx, dg


def _dot(a, b):
    return jnp.dot(a, b, preferred_element_type=F32)


def _dot_nt(a, b):
    return lax.dot_general(a, b, NT, preferred_element_type=F32)


def _dot_tn(a, b):
    return lax.dot_general(a, b, TN, preferred_element_type=F32)


def _norm_matmul(x, g, wblk, name, out_dtype=F32):
    T, D = x.shape
    NB, _, nb = wblk.shape
    tm = min(T, 512)

    def body(x_ref, g_ref, w_ref, o_ref, h_ref):
        xv = x_ref[...]
        hb = (xv * _rms_r(xv) * g_ref[...]).astype(BF16)
        h_ref[...] = hb
        for q in range(NB):
            o_ref[:, q * nb:(q + 1) * nb] = _dot(hb, w_ref[q]).astype(o_ref.dtype)

    return pl.pallas_call(
        body, name=name, grid=(T // tm,),
        in_specs=[pl.BlockSpec((tm, D), lambda i: (i, 0)),
                  pl.BlockSpec((1, D), lambda i: (0, 0)),
                  pl.BlockSpec((NB, D, nb), lambda i: (0, 0, 0))],
        out_specs=[pl.BlockSpec((tm, NB * nb), lambda i: (i, 0)),
                   pl.BlockSpec((tm, D), lambda i: (i, 0))],
        out_shape=[S((T, NB * nb), out_dtype), S((T, D), BF16)],
        compiler_params=_cp("parallel"),
    )(x, g, wblk)


def _matmul_res(a, w, res, name):
    T, K = a.shape
    N = w.shape[1]
    tm = min(T, 512)

    def body(a_ref, w_ref, r_ref, o_ref):
        o_ref[...] = r_ref[...] + _dot(a_ref[...], w_ref[...])

    return pl.pallas_call(
        body, name=name, grid=(T // tm,),
        in_specs=[pl.BlockSpec((tm, K), lambda i: (i, 0)),
                  pl.BlockSpec((K, N), lambda i: (0, 0)),
                  pl.BlockSpec((tm, N), lambda i: (i, 0))],
        out_specs=pl.BlockSpec((tm, N), lambda i: (i, 0)),
        out_shape=S((T, N), F32),
        compiler_params=_cp("parallel"),
    )(a, w, res)


def _matmul_nt(a, w, name, out_dtype, after=()):
    T, N = a.shape
    K = w.shape[0]
    tm = min(T, 512)

    def body(a_ref, w_ref, *rest):
        o_ref, ab_ref = rest[len(after):]
        ab = a_ref[...].astype(BF16)
        ab_ref[...] = ab
        o_ref[...] = _dot_nt(ab, w_ref[...]).astype(o_ref.dtype)

    return pl.pallas_call(
        body, name=name, grid=(T // tm,),
        in_specs=[pl.BlockSpec((tm, N), lambda i: (i, 0)),
                  pl.BlockSpec((K, N), lambda i: (0, 0))] + [pl.BlockSpec(memory_space=pl.ANY)] * len(after),
        out_specs=[pl.BlockSpec((tm, K), lambda i: (i, 0)),
                   pl.BlockSpec((tm, N), lambda i: (i, 0))],
        out_shape=[S((T, K), out_dtype), S((T, N), BF16)],
        compiler_params=_cp("parallel"),
    )(a, w, *after)


def _matmul_tn(a, b3, nb, name, blocked):
    T, M = a.shape
    SB, _, N = b3.shape
    per = N // nb
    NB = SB * per
    tk = min(T, 1024)
    nk = T // tk
    jb = max(1, TN_STEP_COLS // nb) if blocked else 1
    assert per % jb == 0
    if blocked:
        out_spec, out_shape = pl.BlockSpec((jb, M, nb), lambda j, k: (j, 0, 0)), S((NB, M, nb), BF16)
    else:
        assert SB == 1
        out_spec, out_shape = pl.BlockSpec((M, nb), lambda j, k: (0, j)), S((M, N), BF16)

    def body(a_ref, b_ref, o_ref, acc_ref):
        k = pl.program_id(1)

        @pl.when(k == 0)
        def _():
            acc_ref[...] = jnp.zeros_like(acc_ref)

        av = a_ref[...]
        for q in range(jb):
            acc_ref[q] += _dot_tn(av, b_ref[:, q * nb:(q + 1) * nb])

        @pl.when(k == nk - 1)
        def _():
            if blocked:
                o_ref[...] = acc_ref[...].astype(BF16)
            else:
                o_ref[...] = acc_ref[0].astype(BF16)

    return pl.pallas_call(
        body, name=name, grid=(NB // jb, nk),
        in_specs=[pl.BlockSpec((tk, M), lambda j, k: (k, 0)),
                  pl.BlockSpec((None, tk, jb * nb), lambda j, k: ((j * jb) // per, k, ((j * jb) % per) // jb))],
        out_specs=out_spec,
        out_shape=out_shape,
        scratch_shapes=[pltpu.VMEM((jb, M, nb), F32)],
        compiler_params=_cp("parallel", "arbitrary"),
    )(a, b3)


def _nt_normbwd(dz3, wblk, x, g, dres, name, after=()):
    T, D = x.shape
    NB, _, nb = wblk.shape
    SB, _, N = dz3.shape
    per = N // nb
    tm = min(T, 512)

    def body(dz_ref, w_ref, x_ref, g_ref, dr_ref, *rest):
        dx_ref, dg_ref = rest[len(after):]

        @pl.when(pl.program_id(0) == 0)
        def _():
            dg_ref[...] = jnp.zeros_like(dg_ref)

        dh = None
        for q in range(NB):
            cols = slice((q % per) * nb, (q % per + 1) * nb)
            part = _dot_nt(dz_ref[q // per, :, cols], w_ref[q])
            dh = part if dh is None else dh + part
        dx, dg = _rms_bwd(dh, x_ref[...], g_ref[...])
        dx_ref[...] = dr_ref[...] + dx
        dg_ref[...] += dg

    return pl.pallas_call(
        body, name=name, grid=(T // tm,),
        in_specs=[pl.BlockSpec((SB, tm, N), lambda i: (0, i, 0)),
                  pl.BlockSpec((NB, D, nb), lambda i: (0, 0, 0)),
                  pl.BlockSpec((tm, D), lambda i: (i, 0)),
                  pl.BlockSpec((1, D), lambda i: (0, 0)),
                  pl.BlockSpec((tm, D), lambda i: (i, 0))] + [pl.BlockSpec(memory_space=pl.ANY)] * len(after),
        out_specs=[pl.BlockSpec((tm, D), lambda i: (i, 0)),
                   pl.BlockSpec((1, D), lambda i: (0, 0))],
        out_shape=[S((T, D), F32), S((1, D), F32)],
        compiler_params=_cp("arbitrary"),
    )(dz3, wblk, x, g, dres, *after)


def _mlp_fwd(x, g, wup, wdown, name):
    T, D = x.shape
    NB, _, fb = wup.shape
    tm = min(T, 1024)

    def body(x_ref, g_ref, wu_ref, wd_ref, xo_ref, a_ref, h_ref, acc_ref):
        j = pl.program_id(1)

        @pl.when(j == 0)
        def _():
            xv = x_ref[...]
            h_ref[...] = (xv * _rms_r(xv) * g_ref[...]).astype(BF16)
            acc_ref[...] = xv

        a = _dot(h_ref[...], wu_ref[...])
        a_ref[...] = a.astype(BF16)
        u = jnp.maximum(a, 0.0)
        acc_ref[...] += _dot((u * u).astype(BF16), wd_ref[...])

        @pl.when(j == NB - 1)
        def _():
            xo_ref[...] = acc_ref[...]

    return pl.pallas_call(
        body, name=name, grid=(T // tm, NB),
        in_specs=[pl.BlockSpec((tm, D), lambda i, j: (i, 0)),
                  pl.BlockSpec((1, D), lambda i, j: (0, 0)),
                  pl.BlockSpec((None, D, fb), lambda i, j: (j, 0, 0)),
                  pl.BlockSpec((fb, D), lambda i, j: (j, 0))],
        out_specs=[pl.BlockSpec((tm, D), lambda i, j: (i, 0)),
                   pl.BlockSpec((tm, fb), lambda i, j: (i, j)),
                   pl.BlockSpec((tm, D), lambda i, j: (i, 0))],
        out_shape=[S((T, D), F32), S((T, NB * fb), BF16), S((T, D), BF16)],
        scratch_shapes=[pltpu.VMEM((tm, D), F32)],
        compiler_params=_cp("parallel", "arbitrary"),
    )(x, g, wup, wdown)


def _mlp_bwd_dx(x, dout, a, g, wup, wdown, name):
    T, D = x.shape
    NB, _, fb = wup.shape
    tm = min(T, 1024)

    def body(x_ref, do_ref, a_ref, g_ref, wu_ref, wd_ref, dx_ref, da_ref, dob_ref, dg_ref, acc_ref):
        i, j = pl.program_id(0), pl.program_id(1)

        @pl.when(j == 0)
        def _():
            dob_ref[...] = do_ref[...].astype(BF16)
            acc_ref[...] = jnp.zeros_like(acc_ref)

        @pl.when((i == 0) & (j == 0))
        def _():
            dg_ref[...] = jnp.zeros_like(dg_ref)

        du2 = _dot_nt(dob_ref[...], wd_ref[...])
        u = jnp.maximum(a_ref[...].astype(F32), 0.0)
        da = (du2 * (2.0 * u)).astype(BF16)
        da_ref[...] = da
        acc_ref[...] += _dot_nt(da, wu_ref[...])

        @pl.when(j == NB - 1)
        def _():
            dx, dg = _rms_bwd(acc_ref[...], x_ref[...], g_ref[...])
            dx_ref[...] = do_ref[...] + dx
            dg_ref[...] += dg

    return pl.pallas_call(
        body, name=name, grid=(T // tm, NB),
        in_specs=[pl.BlockSpec((tm, D), lambda i, j: (i, 0)),
                  pl.BlockSpec((tm, D), lambda i, j: (i, 0)),
                  pl.BlockSpec((tm, fb), lambda i, j: (i, j)),
                  pl.BlockSpec((1, D), lambda i, j: (0, 0)),
                  pl.BlockSpec((None, D, fb), lambda i, j: (j, 0, 0)),
                  pl.BlockSpec((fb, D), lambda i, j: (j, 0))],
        out_specs=[pl.BlockSpec((tm, D), lambda i, j: (i, 0)),
                   pl.BlockSpec((tm, fb), lambda i, j: (i, j)),
                   pl.BlockSpec((tm, D), lambda i, j: (i, 0)),
                   pl.BlockSpec((1, D), lambda i, j: (0, 0))],
        out_shape=[S((T, D), F32), S((T, NB * fb), BF16), S((T, D), BF16), S((1, D), F32)],
        scratch_shapes=[pltpu.VMEM((tm, D), F32)],
        compiler_params=_cp("arbitrary", "arbitrary"),
    )(x, dout, a, g, wup, wdown)


def _mlp_bwd_dw(h, da, a, dob, fb, name):
    T, D = h.shape
    F = a.shape[1]
    NB = F // fb
    tk = min(T, 1024)
    nk = T // tk

    def body(h_ref, da_ref, a_ref, dob_ref, dwu_ref, dwd_ref, au_ref, ad_ref):
        k = pl.program_id(1)

        @pl.when(k == 0)
        def _():
            au_ref[...] = jnp.zeros_like(au_ref)
            ad_ref[...] = jnp.zeros_like(ad_ref)

        au_ref[...] += _dot_tn(h_ref[...], da_ref[...])
        u = jnp.maximum(a_ref[...].astype(F32), 0.0)
        ad_ref[...] += _dot_tn((u * u).astype(BF16), dob_ref[...])

        @pl.when(k == nk - 1)
        def _():
            dwu_ref[...] = au_ref[...].astype(BF16)
            dwd_ref[...] = ad_ref[...].astype(BF16)

    return pl.pallas_call(
        body, name=name, grid=(NB, nk),
        in_specs=[pl.BlockSpec((tk, D), lambda j, k: (k, 0)),
                  pl.BlockSpec((tk, fb), lambda j, k: (k, j)),
                  pl.BlockSpec((tk, fb), lambda j, k: (k, j)),
                  pl.BlockSpec((tk, D), lambda j, k: (k, 0))],
        out_specs=[pl.BlockSpec((None, D, fb), lambda j, k: (j, 0, 0)),
                   pl.BlockSpec((fb, D), lambda j, k: (j, 0))],
        out_shape=[S((NB, D, fb), BF16), S((F, D), BF16)],
        scratch_shapes=[pltpu.VMEM((D, fb), F32), pltpu.VMEM((fb, D), F32)],
        compiler_params=_cp("parallel", "arbitrary"),
    )(h, da, a, dob)


def _final_loss(x, tgt, g, name):
    T, D = x.shape
    tm = min(T, 512)

    def body(x_ref, t_ref, g_ref, dx_ref, loss_ref, dg_ref):
        @pl.when(pl.program_id(0) == 0)
        def _():
            loss_ref[...] = jnp.zeros_like(loss_ref)
            dg_ref[...] = jnp.zeros_like(dg_ref)

        xv = x_ref[...]
        gv = g_ref[...]
        err = xv * _rms_r(xv) * gv - t_ref[...]
        e2 = jnp.sum(jnp.sum(err * err, axis=-1, keepdims=True), axis=0, keepdims=True)
        loss_ref[...] += (0.5 / D) * e2
        dx, dg = _rms_bwd(err * (1.0 / D), xv, gv)
        dx_ref[...] = dx
        dg_ref[...] += dg

    return pl.pallas_call(
        body, name=name, grid=(T // tm,),
        in_specs=[pl.BlockSpec((tm, D), lambda i: (i, 0)),
                  pl.BlockSpec((tm, D), lambda i: (i, 0)),
                  pl.BlockSpec((1, D), lambda i: (0, 0))],
        out_specs=[pl.BlockSpec((tm, D), lambda i: (i, 0)),
                   pl.BlockSpec((1, 128), lambda i: (0, 0)),
                   pl.BlockSpec((1, D), lambda i: (0, 0))],
        out_shape=[S((T, D), F32), S((1, 128), F32), S((1, D), F32)],
        compiler_params=_cp("arbitrary"),
    )(x, tgt, g)


def _rope_tables(L):
    nf = HEAD_DIM // 4
    t = jnp.arange(L, dtype=jnp.int32)
    row = (t // GRID_W).astype(F32)
    col = (t % GRID_W).astype(F32)
    inv = ROPE_THETA ** (-jnp.arange(nf, dtype=F32) / nf)
    ar = row[:, None] * inv
    ac = col[:, None] * inv
    cos = jnp.concatenate([jnp.cos(ar), jnp.cos(ar), jnp.cos(ac), jnp.cos(ac)], axis=-1)
    sin = jnp.concatenate([-jnp.sin(ar), jnp.sin(ar), -jnp.sin(ac), jnp.sin(ac)], axis=-1)
    return cos, sin


def _swap32(x):
    lane = lax.broadcasted_iota(jnp.int32, x.shape, 1)
    up = pltpu.roll(x, HEAD_DIM - 32, 1)
    down = pltpu.roll(x, 32, 1)
    return jnp.where((lane % 64) < 32, up, down)


def _qk_prep(qkv, qg, kg, cos, sin, L, name):
    T, W = qkv.shape
    nh = W // HEAD_DIM - 2 * N_KV
    tm = min(L, 512)
    lb = L // tm

    def body(qkv_ref, qg_ref, kg_ref, cos_ref, sin_ref, q_ref, k_ref, v_ref):
        c = cos_ref[...]
        s = sin_ref[...]
        for h in range(nh + N_KV):
            xh = qkv_ref[:, h * HEAD_DIM:(h + 1) * HEAD_DIM]
            gv = qg_ref[...] if h < nh else kg_ref[...]
            y = xh * _rms_r(xh) * gv
            y = (y * c + _swap32(y) * s).astype(BF16)
            if h < nh:
                q_ref[:, h * HEAD_DIM:(h + 1) * HEAD_DIM] = y
            else:
                k_ref[:, (h - nh) * HEAD_DIM:(h - nh + 1) * HEAD_DIM] = y
        v_ref[...] = qkv_ref[:, (nh + N_KV) * HEAD_DIM:].astype(BF16)

    return pl.pallas_call(
        body, name=name, grid=(T // tm,),
        in_specs=[pl.BlockSpec((tm, W), lambda i: (i, 0)),
                  pl.BlockSpec((1, HEAD_DIM), lambda i: (0, 0)),
                  pl.BlockSpec((1, HEAD_DIM), lambda i: (0, 0)),
                  pl.BlockSpec((tm, HEAD_DIM), lambda i: (i % lb, 0)),
                  pl.BlockSpec((tm, HEAD_DIM), lambda i: (i % lb, 0))],
        out_specs=[pl.BlockSpec((tm, nh * HEAD_DIM), lambda i: (i, 0)),
                   pl.BlockSpec((tm, N_KV * HEAD_DIM), lambda i: (i, 0)),
                   pl.BlockSpec((tm, N_KV * HEAD_DIM), lambda i: (i, 0))],
        out_shape=[S((T, nh * HEAD_DIM), BF16), S((T, N_KV * HEAD_DIM), BF16), S((T, N_KV * HEAD_DIM), BF16)],
        compiler_params=_cp("parallel"),
    )(qkv, qg, kg, cos, sin)


def _qk_prep_bwd(qkv, dq, dk, dv, qg, kg, cos, sin, L, name):
    T, W = qkv.shape
    nh = W // HEAD_DIM - 2 * N_KV
    tm = min(L, 512)
    lb = L // tm

    def body(qkv_ref, dq_ref, dk_ref, dv_ref, qg_ref, kg_ref, cos_ref, sin_ref, dz_ref, dqg_ref, dkg_ref):
        @pl.when(pl.program_id(0) == 0)
        def _():
            dqg_ref[...] = jnp.zeros_like(dqg_ref)
            dkg_ref[...] = jnp.zeros_like(dkg_ref)

        c = cos_ref[...]
        s = sin_ref[...]
        for h in range(nh + N_KV):
            cols = slice(h * HEAD_DIM, (h + 1) * HEAD_DIM)
            if h < nh:
                dout, gv, dg_ref = dq_ref[:, cols], qg_ref[...], dqg_ref
            else:
                kc = slice((h - nh) * HEAD_DIM, (h - nh + 1) * HEAD_DIM)
                dout, gv, dg_ref = dk_ref[:, kc], kg_ref[...], dkg_ref
            dy = dout * c - _swap32(dout) * s
            dx, dg = _rms_bwd(dy, qkv_ref[:, cols], gv)
            dg_ref[...] += dg
            dz_ref[:, cols] = dx.astype(BF16)
        dz_ref[:, (nh + N_KV) * HEAD_DIM:] = dv_ref[...].astype(BF16)

    return pl.pallas_call(
        body, name=name, grid=(T // tm,),
        in_specs=[pl.BlockSpec((tm, W), lambda i: (i, 0)),
                  pl.BlockSpec((tm, nh * HEAD_DIM), lambda i: (i, 0)),
                  pl.BlockSpec((tm, N_KV * HEAD_DIM), lambda i: (i, 0)),
                  pl.BlockSpec((tm, N_KV * HEAD_DIM), lambda i: (i, 0)),
                  pl.BlockSpec((1, HEAD_DIM), lambda i: (0, 0)),
                  pl.BlockSpec((1, HEAD_DIM), lambda i: (0, 0)),
                  pl.BlockSpec((tm, HEAD_DIM), lambda i: (i % lb, 0)),
                  pl.BlockSpec((tm, HEAD_DIM), lambda i: (i % lb, 0))],
        out_specs=[pl.BlockSpec((tm, W), lambda i: (i, 0)),
                   pl.BlockSpec((1, HEAD_DIM), lambda i: (0, 0)),
                   pl.BlockSpec((1, HEAD_DIM), lambda i: (0, 0))],
        out_shape=[S((T, W), BF16), S((1, HEAD_DIM), F32), S((1, HEAD_DIM), F32)],
        compiler_params=_cp("arbitrary"),
    )(qkv, dq, dk, dv, qg, kg, cos, sin)


EXP2_SCALE = SOFTMAX_SCALE * math.log2(math.e)
ATTN_SUB = 128


def _softmax_rows(q, k):
    s = _dot_nt(q, k)
    e = jnp.exp2((s - jnp.max(s, axis=-1, keepdims=True)) * EXP2_SCALE)
    return e, jnp.sum(e, axis=-1, keepdims=True)


def _attn_fwd(q, k, v, L, name):
    T = q.shape[0]
    nh = q.shape[1] // HEAD_DIM
    G = nh // N_KV
    B = T // L
    tq = min(L, 512)
    nq = L // tq
    sub = min(tq, ATTN_SUB)

    def body(q_ref, k_ref, v_ref, o_ref):
        for h in range(tq // sub):
            rows = slice(h * sub, (h + 1) * sub)
            e, l = _softmax_rows(q_ref[rows, :], k_ref[...])
            o_ref[rows, :] = (_dot(e.astype(BF16), v_ref[...]) / l).astype(BF16)

    qspec = pl.BlockSpec((tq, HEAD_DIM), lambda b, kv, g, qi: (b * nq + qi, kv * G + g))
    kspec = pl.BlockSpec((L, HEAD_DIM), lambda b, kv, g, qi: (b, kv))
    return pl.pallas_call(
        body, name=name, grid=(B, N_KV, G, nq),
        in_specs=[qspec, kspec, kspec],
        out_specs=qspec,
        out_shape=S((T, nh * HEAD_DIM), BF16),
        compiler_params=_cp("parallel", "parallel", "parallel", "parallel"),
    )(q, k, v)


def _attn_bwd(q, k, v, do, o, L, name):
    T = q.shape[0]
    nh = q.shape[1] // HEAD_DIM
    G = nh // N_KV
    B = T // L
    tq = min(L, 512)
    nq = L // tq

    sub = min(tq, ATTN_SUB)

    def body(q_ref, k_ref, v_ref, do_ref, o_ref, dq_ref, dk_ref, dv_ref, ds_scr, p_scr):
        first = (pl.program_id(2) == 0) & (pl.program_id(3) == 0)
        last = (pl.program_id(2) == G - 1) & (pl.program_id(3) == nq - 1)

        @pl.when(first)
        def _():
            dk_ref[...] = jnp.zeros_like(dk_ref)
            dv_ref[...] = jnp.zeros_like(dv_ref)

        for h in range(tq // sub):
            rows = slice(h * sub, (h + 1) * sub)
            dov = do_ref[rows, :]
            e, l = _softmax_rows(q_ref[rows, :], k_ref[...])
            p = e * (1.0 / l)
            dsum = jnp.sum(dov.astype(F32) * o_ref[rows, :].astype(F32), axis=-1, keepdims=True)
            ds_scr[rows, :] = (p * (_dot_nt(dov, v_ref[...]) - dsum)).astype(BF16)
            p_scr[rows, :] = p.astype(BF16)
        ds = ds_scr[...]
        dq_ref[...] = _dot(ds, k_ref[...]) * SOFTMAX_SCALE
        dk_ref[...] += _dot_tn(ds, q_ref[...])
        dv_ref[...] += _dot_tn(p_scr[...], do_ref[...])

        @pl.when(last)
        def _():
            dk_ref[...] = dk_ref[...] * SOFTMAX_SCALE

    qspec = pl.BlockSpec((tq, HEAD_DIM), lambda b, kv, g, qi: (b * nq + qi, kv * G + g))
    kspec = pl.BlockSpec((L, HEAD_DIM), lambda b, kv, g, qi: (b, kv))
    return pl.pallas_call(
        body, name=name, grid=(B, N_KV, G, nq),
        in_specs=[qspec, kspec, kspec, qspec, qspec],
        out_specs=[qspec, kspec, kspec],
        out_shape=[S((T, nh * HEAD_DIM), F32), S((T, N_KV * HEAD_DIM), F32), S((T, N_KV * HEAD_DIM), F32)],
        scratch_shapes=[pltpu.VMEM((tq, L), BF16), pltpu.VMEM((tq, L), BF16)],
        compiler_params=_cp("parallel", "parallel", "arbitrary", "arbitrary"),
    )(q, k, v, do, o)


PV_CONV_W = 0
PV_B_A = 8
PV_B_X = 16
PV_LAM = 24
PV_CONV_B = 32
PV_ROWS = 40


def _shift_rows(x, k):
    if k == 0:
        return x
    L = x.shape[0]
    n = N_SEG * abs(k)
    seg = lax.broadcasted_iota(jnp.int32, (n, x.shape[1]), 0) % N_SEG
    if k > 0:
        edge = jnp.where(seg == 0, 0.0, pltpu.roll(x[L - n:], 1, 0))
        return jnp.concatenate([edge, x[:L - n]], axis=0)
    edge = jnp.where(seg == N_SEG - 1, 0.0, pltpu.roll(x[:n], n - 1, 0))
    return jnp.concatenate([x[n:], edge], axis=0)


def _conv_taps(rec, pv):
    c = pv[PV_CONV_B:PV_CONV_B + 1]
    for j in range(CONV_W):
        c = c + pv[PV_CONV_W + j:PV_CONV_W + j + 1] * _shift_rows(rec, 2 - j)
    return c


def _sigmoid(x):
    return 0.5 * jnp.tanh(0.5 * x) + 0.5


EXPM1_SERIES_BELOW = 0.03


def _rg_gates(c, cbf, wa, wx, ba, bx, lam):
    r = _sigmoid(_dot(cbf, wa) + ba)
    i = _sigmoid(_dot(cbf, wx) + bx)
    sp = jnp.maximum(-lam, 0.0) + jnp.log1p(jnp.exp(-jnp.abs(lam)))
    la = r * ((-RG_C) * sp)
    a = jnp.exp(la)
    a2 = a * a
    x = la + la
    series = -(x * ((x * (1.0 / 6.0) + 0.5) * x + 1.0))
    om = jnp.where(x > -EXPM1_SERIES_BELOW, series, 1.0 - a2)
    rm = lax.rsqrt(om)
    return r, i, a, om * rm, rm, a2, sp


def _gelu(x):
    t = jnp.tanh(GELU_K * (x + GELU_C * x * x * x))
    return 0.5 * x * (1.0 + t), t


def _scan_pair(af_ref, uf_ref, ab_ref, ub_ref, hf_ref, hb_ref, pf_ref, pb_ref, L):
    ls = L // N_SEG
    zero = jnp.zeros((N_SEG, LRU_BW), F32)
    one = jnp.ones((N_SEG, LRU_BW), F32)
    tile = lambda t: pl.ds(pl.multiple_of(t * N_SEG, N_SEG), N_SEG)

    def steps(tc, carry):
        hf, pf, hb, pb = carry
        for q in range(SCAN_UNROLL):
            t = tc * SCAN_UNROLL + q
            rf, rb = tile(t), tile(ls - 1 - t)
            af = af_ref[rf, :]
            hf = af * hf + uf_ref[rf, :]
            pf = pf * af
            hf_ref[rf, :] = hf
            pf_ref[rf, :] = pf
            ab = ab_ref[rb, :]
            hb = ab * hb + ub_ref[rb, :]
            pb = pb * ab
            hb_ref[rb, :] = hb
            pb_ref[rb, :] = pb
        return hf, pf, hb, pb

    hf_e, pf_e, hb_e, pb_e = lax.fori_loop(0, ls // SCAN_UNROLL, steps, (zero, one, zero, one))

    rows, cin = [], jnp.zeros((1, LRU_BW), F32)
    for s in range(N_SEG):
        rows.append(cin)
        cin = hf_e[s:s + 1] + pf_e[s:s + 1] * cin
    cf = jnp.concatenate(rows, axis=0)
    rows, cin = [], jnp.zeros((1, LRU_BW), F32)
    for s in reversed(range(N_SEG)):
        rows.append(cin)
        cin = hb_e[s:s + 1] + pb_e[s:s + 1] * cin
    cb = jnp.concatenate(rows[::-1], axis=0)

    def fix(tc, _):
        for q in range(SCAN_UNROLL):
            r = tile(tc * SCAN_UNROLL + q)
            hf_ref[r, :] = hf_ref[r, :] + pf_ref[r, :] * cf
            hb_ref[r, :] = hb_ref[r, :] + pb_ref[r, :] * cb
        return 0

    lax.fori_loop(0, ls // SCAN_UNROLL, fix, 0)


def _rg_specs(L, D, nblk):
    slab = lambda off: pl.BlockSpec((L, LRU_BW), lambda cb, b: (b, off + cb))
    wspec = pl.BlockSpec((2, None, LRU_BW, LRU_BW), lambda cb, b: (0, cb, 0, 0))
    pvspec = pl.BlockSpec((PV_ROWS, LRU_BW), lambda cb, b: (0, cb))
    return slab, wspec, pvspec


def _rg_fwd(z, pvec, wa, wx, L, name):
    T, C2 = z.shape
    C = C2 // 2
    nblk = C // LRU_BW
    B = T // L
    slab, wspec, pvspec = _rg_specs(L, C, nblk)

    def body(gp_ref, rec_ref, pv_ref, wa_ref, wx_ref, yg_ref, hf_ref, hb_ref, a_scr, u_scr, p_scr):
        pv = pv_ref[...]
        c = _conv_taps(rec_ref[...], pv)
        cbf = c.astype(BF16)
        for d in range(2):
            _, i, a, m, _, _, _ = _rg_gates(c, cbf, wa_ref[d], wx_ref[d], pv[PV_B_A + d:PV_B_A + d + 1],
                                      pv[PV_B_X + d:PV_B_X + d + 1], pv[PV_LAM + d:PV_LAM + d + 1])
            a_scr[d] = a
            u_scr[d] = m * (i * c)
        _scan_pair(a_scr.at[0], u_scr.at[0], a_scr.at[1], u_scr.at[1], hf_ref, hb_ref, p_scr.at[0], p_scr.at[1], L)
        gate, _ = _gelu(gp_ref[...])
        yg_ref[...] = ((hf_ref[...] + hb_ref[...]) * gate).astype(BF16)

    return pl.pallas_call(
        body, name=name, grid=(nblk, B),
        in_specs=[slab(0), slab(nblk), pvspec, wspec, wspec],
        out_specs=[slab(0), slab(0), slab(0)],
        out_shape=[S((T, C), BF16), S((T, C), F32), S((T, C), F32)],
        scratch_shapes=[pltpu.VMEM((2, L, LRU_BW), F32)] * 3,
        compiler_params=_cp("parallel", "parallel"),
    )(z, z, pvec, wa, wx)


def _rg_bwd(z, hf, hb, dyg, pvec, wa, wx, L, name):
    T, C2 = z.shape
    C = C2 // 2
    nblk = C // LRU_BW
    B = T // L
    slab, wspec, pvspec = _rg_specs(L, C, nblk)

    def body(gp_ref, rec_ref, hf_ref, hb_ref, dyg_ref, pv_ref, wa_ref, wx_ref,
             dz_ref, dwa_ref, dwx_ref, dpv_ref, a_scr, u_scr, d_scr, p_scr):
        @pl.when(pl.program_id(1) == 0)
        def _():
            dwa_ref[...] = jnp.zeros_like(dwa_ref)
            dwx_ref[...] = jnp.zeros_like(dwx_ref)
            dpv_ref[...] = jnp.zeros_like(dpv_ref)

        pv = pv_ref[...]
        rec = rec_ref[...]
        c = _conv_taps(rec, pv)
        cbf = c.astype(BF16)
        gp = gp_ref[...]
        gate, th = _gelu(gp)
        dgelu = 0.5 * (1.0 + th) + 0.5 * gp * (1.0 - th * th) * GELU_K * (1.0 + 3.0 * GELU_C * gp * gp)
        dyg = dyg_ref[...]
        dz_ref[0] = (dyg * (hf_ref[...] + hb_ref[...]) * dgelu).astype(BF16)
        dy = dyg * gate

        gates = []
        for d in range(2):
            gates.append(_rg_gates(c, cbf, wa_ref[d], wx_ref[d], pv[PV_B_A + d:PV_B_A + d + 1],
                                   pv[PV_B_X + d:PV_B_X + d + 1], pv[PV_LAM + d:PV_LAM + d + 1]))
        a_scr[0] = _shift_rows(gates[1][2], 1)
        a_scr[1] = _shift_rows(gates[0][2], -1)
        u_scr[...] = dy
        _scan_pair(a_scr.at[0], u_scr, a_scr.at[1], u_scr, d_scr.at[1], d_scr.at[0], p_scr.at[0], p_scr.at[1], L)

        dc = jnp.zeros_like(c)
        rows = []
        for d in range(2):
            r, i, a, m, rm, a2, sp = gates[d]
            delta = d_scr[d]
            hnb = _shift_rows(hf_ref[...], 1) if d == 0 else _shift_rows(hb_ref[...], -1)
            da = delta * hnb
            dm = delta * (i * c)
            di = delta * (m * c)
            dc = dc + delta * (m * i)
            dla = da * a - dm * (a2 * rm)
            dpa = (dla * ((-RG_C) * sp)) * (r * (1.0 - r))
            dpx = di * (i * (1.0 - i))
            dsp = (-RG_C) * jnp.sum(dla * r, axis=0, keepdims=True)
            lam = pv[PV_LAM + d:PV_LAM + d + 1]
            rows.append((jnp.sum(dpa, axis=0, keepdims=True), jnp.sum(dpx, axis=0, keepdims=True),
                         -dsp * _sigmoid(-lam)))
            dpab = dpa.astype(BF16)
            dpxb = dpx.astype(BF16)
            dwa_ref[d] += _dot_tn(cbf, dpab)
            dwx_ref[d] += _dot_tn(cbf, dpxb)
            dc = dc + _dot_nt(dpab, wa_ref[d]) + _dot_nt(dpxb, wx_ref[d])

        drec = jnp.zeros_like(c)
        dcw = []
        for j in range(CONV_W):
            drec = drec + pv[PV_CONV_W + j:PV_CONV_W + j + 1] * _shift_rows(dc, j - 2)
            dcw.append(jnp.sum(dc * _shift_rows(rec, 2 - j), axis=0, keepdims=True))
        dz_ref[1] = drec.astype(BF16)
        for j in range(CONV_W):
            dpv_ref[PV_CONV_W + j:PV_CONV_W + j + 1, :] += dcw[j]
        for d in range(2):
            dpv_ref[PV_B_A + d:PV_B_A + d + 1, :] += rows[d][0]
            dpv_ref[PV_B_X + d:PV_B_X + d + 1, :] += rows[d][1]
            dpv_ref[PV_LAM + d:PV_LAM + d + 1, :] += rows[d][2]
        dpv_ref[PV_CONV_B:PV_CONV_B + 1, :] += jnp.sum(dc, axis=0, keepdims=True)

    return pl.pallas_call(
        body, name=name, grid=(nblk, B),
        in_specs=[slab(0), slab(nblk), slab(0), slab(0), slab(0), pvspec, wspec, wspec],
        out_specs=[pl.BlockSpec((2, L, LRU_BW), lambda cb, b: (0, b, cb)), wspec, wspec, pvspec],
        out_shape=[S((2, T, C), BF16), S((2, nblk, LRU_BW, LRU_BW), F32), S((2, nblk, LRU_BW, LRU_BW), F32),
                   S((PV_ROWS, C), F32)],
        scratch_shapes=[pltpu.VMEM((2, L, LRU_BW), F32), pltpu.VMEM((L, LRU_BW), F32),
                        pltpu.VMEM((2, L, LRU_BW), F32), pltpu.VMEM((2, L, LRU_BW), F32)],
        compiler_params=_cp("parallel", "arbitrary"),
    )(z, z, hf, hb, dyg, pvec, wa, wx)


QKV_NB = 512


def _interleave(a):
    *lead, L, D = a.shape
    return a.reshape(*lead, N_SEG, L // N_SEG, D).swapaxes(-3, -2).reshape(*lead, L, D)


def _deinterleave(a):
    *lead, L, D = a.shape
    return a.reshape(*lead, L // N_SEG, N_SEG, D).swapaxes(-3, -2).reshape(*lead, L, D)


def _local_step(x3, tgt3, w, fetch, send):
    Bl, L, D = x3.shape
    T = Bl * L
    x = _interleave(x3).reshape(T, D)
    tgt = _interleave(tgt3).reshape(T, D)
    gm = [w["g_mix"][i:i + 1] for i in range(2)]
    gl = [w["g_mlp"][i:i + 1] for i in range(2)]

    w0 = fetch(0, ())
    nb_in = w0["w_in"].shape[-1]
    z, h0 = _norm_matmul(x, gm[0], w0["w_in"], "rg_in")
    yg, hf, hb = _rg_fwd(z, w0["pvec"], w["wa"], w["wx"], L, "rg_fwd")
    w1 = fetch(1, (yg,))
    x1 = _matmul_res(yg, w1["w_out"], x, "rg_out")
    w1.update(fetch(4, (x1,)))
    fb = w1["w_up0"].shape[-1]
    x2, a0, hm0 = _mlp_fwd(x1, gl[0], w1["w_up0"], w1["w_down0"], "mlp0_fwd")
    w2 = fetch(2, (x2,))
    qkv, h1 = _norm_matmul(x2, gm[1], w2["w_qkv"], "at_qkv")
    cos, sin = [_interleave(t) for t in _rope_tables(L)]
    qn, kn, vb = _qk_prep(qkv, w["qg"], w["kg"], cos, sin, L, "at_prep")
    o = _attn_fwd(qn, kn, vb, L, "at_fwd")
    x3_ = _matmul_res(o, w2["w_o"], x2, "at_out")
    w3 = fetch(3, (x3_,))
    x4, a1, hm1 = _mlp_fwd(x3_, gl[1], w3["w_up1"], w3["w_down1"], "mlp1_fwd")
    dx4, loss, dgf = _final_loss(x4, tgt, w["g_fin"], "loss_head")

    dx3, da1, dob1, dgl1 = _mlp_bwd_dx(x3_, dx4, a1, gl[1], w3["w_up1"], w3["w_down1"], "mlp1_bwd_dx")
    dwu1, dwd1 = _mlp_bwd_dw(hm1, da1, a1, dob1, fb, "mlp1_bwd_dw")
    sent = send(3, dict(w_up1=dwu1, w_down1=dwd1))
    do, dx3b = _matmul_nt(dx3, w2["w_o"], "at_out_bwd", BF16, after=sent)
    dwo = _matmul_tn(o, dx3b[None], QKV_NB, "at_out_dw", blocked=False)
    dq, dk, dv = _attn_bwd(qn, kn, vb, do, o, L, "at_bwd")
    dqkv, dqg, dkg = _qk_prep_bwd(qkv, dq, dk, dv, w["qg"], w["kg"], cos, sin, L, "at_prep_bwd")
    dwqkv = _matmul_tn(h1, dqkv[None], QKV_NB, "at_qkv_dw", blocked=False)
    sent = send(2, dict(w_qkv=dwqkv, w_o=dwo))
    dx2, dgm1 = _nt_normbwd(dqkv[None], w2["w_qkv"], x2, gm[1], dx3, "at_qkv_bwd", after=sent)
    dx1, da0, dob0, dgl0 = _mlp_bwd_dx(x1, dx2, a0, gl[0], w1["w_up0"], w1["w_down0"], "mlp0_bwd_dx")
    dwu0, dwd0 = _mlp_bwd_dw(hm0, da0, a0, dob0, fb, "mlp0_bwd_dw")
    sent = send(1, dict(w_up0=dwu0, w_down0=dwd0))
    dyg, dx1b = _matmul_nt(dx1, w1["w_out"], "rg_out_bwd", F32, after=sent)
    dwout = _matmul_tn(yg, dx1b[None], QKV_NB, "rg_out_dw", blocked=False)
    dz, dwa, dwx, dpv = _rg_bwd(z, hf, hb, dyg, w0["pvec"], w["wa"], w["wx"], L, "rg_bwd")
    dwin = _matmul_tn(h0, dz, nb_in, "rg_in_dw", blocked=True)
    sent = send(0, dict(w_in=dwin, w_out=dwout, pvec=dpv, wa=dwa, wx=dwx))
    dx0, dgm0 = _nt_normbwd(dz, w0["w_in"], x, gm[0], dx1, "rg_in_bwd", after=sent)
    send(-1, dict(g_mix=[dgm0, dgm1], g_mlp=[dgl0, dgl1], g_fin=dgf, conv_b=dpv[PV_CONV_B], qg=dqg, kg=dkg, loss=loss))
    return _deinterleave(dx0.reshape(Bl, L, D))


MESH = pl.DeviceIdType.MESH
ANY = pl.BlockSpec(memory_space=pl.ANY)
N_PEERS = N_DEV - 1


def _my_place():
    return lax.axis_index("x"), lax.axis_index("y"), lax.axis_index("c")


def _flat(px, py, pc):
    return 4 * px + 2 * py + pc


def _all_gather(shards, name):
    n = len(shards)

    def body(*refs):
        ins, outs = refs[:n], refs[n:2 * n]
        send_sems, recv_sems, local_sems = refs[2 * n:]
        x, y, c = _my_place()
        me, sibling = (x, y, c), (x, y, 1 - c)
        chips = [(1 - x, y), (x, 1 - y), (1 - x, 1 - y)]

        def copy(a, k, block, to, src=None):
            dst = outs[a].at[_flat(*block)]
            return pltpu.make_async_remote_copy(
                src_ref=dst if src is None else src, dst_ref=dst,
                send_sem=send_sems.at[a, k], recv_sem=recv_sems.at[a, k],
                device_id=to, device_id_type=MESH)

        mine = [pltpu.make_async_copy(ins[a], outs[a].at[_flat(*me)], local_sems.at[a]) for a in range(n)]
        for cp in mine:
            cp.start()
        first = []
        for a in range(n):
            first.append(copy(a, 0, me, sibling, src=ins[a]))
            first += [copy(a, 1 + j, me, (*chip, c), src=ins[a]) for j, chip in enumerate(chips)]
        for cp in first:
            cp.start()
        passed = []
        for j, chip in enumerate(chips):
            for a in range(n):
                copy(a, 1 + j, (*chip, c), me).wait_recv()
                fwd = copy(a, 4 + j, (*chip, c), sibling)
                fwd.start()
                passed.append(fwd)
        for a in range(n):
            copy(a, 0, sibling, me).wait_recv()
            for j, chip in enumerate(chips):
                copy(a, 4 + j, (*chip, 1 - c), me).wait_recv()
        for cp in first + passed:
            cp.wait_send()
        for cp in mine:
            cp.wait()

    return pl.pallas_call(
        body, name=name,
        in_specs=[ANY] * n, out_specs=[ANY] * n,
        out_shape=[S((N_DEV,) + s.shape, s.dtype) for s in shards],
        scratch_shapes=[pltpu.SemaphoreType.DMA((n, N_PEERS)), pltpu.SemaphoreType.DMA((n, N_PEERS)),
                        pltpu.SemaphoreType.DMA((n,))],
    )(*shards)


HBM = pl.BlockSpec(memory_space=pltpu.HBM)
SEM = pl.BlockSpec(memory_space=pltpu.SEMAPHORE)
SIDE_EFFECT = pltpu.SideEffectType.DATAFLOW_SIDE_EFFECTING
SEMS_PER_GROUP = 3


NEAR_PEERS = (1, 2, 4, 6)
FAR_CHIPS = (2, 4, 6)


def _exchange_copies(srcs, lands, sems, mode):
    send_sems, recv_sems, local_sems = sems
    scatter = mode == "scatter"
    x, y, c = _my_place()
    me = _flat(x, y, c)
    remote, local = [], []
    for a in range(len(srcs)):
        for r in (NEAR_PEERS if mode == "near" else range(1, N_DEV)):
            peer = (1 - x if r & 4 else x, 1 - y if r & 2 else y, 1 - c if r & 1 else c)
            remote.append(pltpu.make_async_remote_copy(
                src_ref=srcs[a].at[_flat(*peer)] if scatter else srcs[a], dst_ref=lands[a].at[me],
                send_sem=send_sems.at[a * N_PEERS + r - 1], recv_sem=recv_sems.at[a * N_PEERS + r - 1],
                device_id=peer, device_id_type=MESH))
        local.append(pltpu.make_async_copy(srcs[a].at[me] if scatter else srcs[a], lands[a].at[me], local_sems.at[a]))
    return remote, local


def _exchange_start(groups, modes, name):
    sizes = [len(g) for g in groups]
    srcs = [pltpu.with_memory_space_constraint(a, pltpu.HBM) for g in groups for a in g]
    n = len(srcs)
    scatter_of = [m == "scatter" for g, m in zip(groups, modes) for _ in g]
    lands = [pltpu.with_memory_space_constraint(lax.empty(a.shape if sc else (N_DEV,) + a.shape, a.dtype), pltpu.HBM)
             for a, sc in zip(srcs, scatter_of)]
    n_sem = SEMS_PER_GROUP * len(groups)

    def body(*refs):
        src_refs, land_refs, sem_refs, token = refs[:n], refs[n:2 * n], refs[2 * n:2 * n + n_sem], refs[-1]
        off = 0
        for gi, k in enumerate(sizes):
            remote, local = _exchange_copies(src_refs[off:off + k], land_refs[off:off + k],
                                             sem_refs[SEMS_PER_GROUP * gi:SEMS_PER_GROUP * (gi + 1)], modes[gi])
            for cp in local + remote:
                cp.start()
            off += k
        token[...] = jnp.zeros_like(token)

    sem_shapes = []
    for k in sizes:
        sem_shapes += [pltpu.SemaphoreType.DMA((k * N_PEERS,)), pltpu.SemaphoreType.DMA((k * N_PEERS,)),
                       pltpu.SemaphoreType.DMA((k,))]
    outs = pl.pallas_call(
        body, name=name,
        out_shape=sem_shapes + [pltpu.HBM(a.shape, a.dtype) for a in srcs + lands] + [S((8, 128), F32)],
        in_specs=[HBM] * (2 * n),
        out_specs=[SEM] * n_sem + [HBM] * (2 * n) + [pl.BlockSpec(memory_space=pltpu.VMEM)],
        input_output_aliases={i: n_sem + i for i in range(2 * n)},
        compiler_params=pltpu.CompilerParams(has_side_effects=SIDE_EFFECT),
    )(*srcs, *lands)
    sems, thru, token = outs[:n_sem], outs[n_sem:n_sem + 2 * n], outs[-1]
    per_group, off = [], 0
    for gi, k in enumerate(sizes):
        per_group.append((sems[SEMS_PER_GROUP * gi:SEMS_PER_GROUP * (gi + 1)], thru[off:off + k], thru[n + off:n + off + k]))
        off += k
    return per_group, token


def _exchange_wait(group, after, mode, name):
    sems, srcs, lands = group
    k = len(srcs)

    def body(*refs):
        remote, local = _exchange_copies(refs[:k], refs[k:2 * k], refs[2 * k:2 * k + SEMS_PER_GROUP], mode)
        for cp in remote:
            cp.wait_send()
            cp.wait_recv()
        for cp in local:
            cp.wait()

    outs = pl.pallas_call(
        body, name=name,
        out_shape=[pltpu.HBM(a.shape, a.dtype) for a in list(srcs) + list(lands)],
        in_specs=[HBM] * (2 * k) + [SEM] * SEMS_PER_GROUP + [ANY] * len(after),
        out_specs=[HBM] * (2 * k),
        input_output_aliases={i: i for i in range(2 * k)},
        compiler_params=pltpu.CompilerParams(has_side_effects=SIDE_EFFECT),
    )(*srcs, *lands, *sems, *after)
    return outs[k:]


def _forward_copies(lands, sems):
    send_sems, recv_sems = sems
    x, y, c = _my_place()
    mine, theirs = [], []
    for a in range(len(lands)):
        for k, r in enumerate(FAR_CHIPS):
            px, py = (1 - x if r & 4 else x), (1 - y if r & 2 else y)
            for out, core in ((mine, c), (theirs, 1 - c)):
                blk = lands[a].at[_flat(px, py, core)]
                out.append(pltpu.make_async_remote_copy(
                    src_ref=blk, dst_ref=blk, send_sem=send_sems.at[a * len(FAR_CHIPS) + k],
                    recv_sem=recv_sems.at[a * len(FAR_CHIPS) + k], device_id=(x, y, 1 - c), device_id_type=MESH))
    return mine, theirs


def _forward_start(groups, name):
    sizes = [len(g) for g in groups]
    lands = [a for g in groups for a in g]
    n = len(lands)
    n_sem = 2 * len(groups)

    def body(*refs):
        land_refs, sem_refs, token = refs[:n], refs[n:n + n_sem], refs[-1]
        off = 0
        for gi, k in enumerate(sizes):
            mine, _ = _forward_copies(land_refs[off:off + k], sem_refs[2 * gi:2 * gi + 2])
            for cp in mine:
                cp.start()
            off += k
        token[...] = jnp.zeros_like(token)

    sem_shapes = []
    for k in sizes:
        sem_shapes += [pltpu.SemaphoreType.DMA((k * len(FAR_CHIPS),))] * 2
    outs = pl.pallas_call(
        body, name=name,
        out_shape=sem_shapes + [pltpu.HBM(a.shape, a.dtype) for a in lands] + [S((8, 128), F32)],
        in_specs=[HBM] * n,
        out_specs=[SEM] * n_sem + [HBM] * n + [pl.BlockSpec(memory_space=pltpu.VMEM)],
        input_output_aliases={i: n_sem + i for i in range(n)},
        compiler_params=pltpu.CompilerParams(has_side_effects=SIDE_EFFECT),
    )(*lands)
    per_group, off = [], 0
    for gi, k in enumerate(sizes):
        per_group.append((outs[2 * gi:2 * gi + 2], outs[n_sem + off:n_sem + off + k]))
        off += k
    return per_group


def _forward_wait(group, after, name):
    sems, lands = group
    k = len(lands)

    def body(*refs):
        mine, theirs = _forward_copies(refs[:k], refs[k:k + 2])
        for cp in mine:
            cp.wait_send()
        for cp in theirs:
            cp.wait_recv()

    return pl.pallas_call(
        body, name=name,
        out_shape=[pltpu.HBM(a.shape, a.dtype) for a in lands],
        in_specs=[HBM] * k + [SEM] * 2 + [ANY] * len(after),
        out_specs=[HBM] * k,
        input_output_aliases={i: i for i in range(k)},
        compiler_params=pltpu.CompilerParams(has_side_effects=SIDE_EFFECT),
    )(*lands, *sems, *after)


def _row_tile(rows, cols):
    want = max(16, (128 * 1024) // cols)
    if rows <= want:
        return rows
    t = want - want % 16
    while rows % t:
        t -= 16
    return t


def _sum_parts(parts, name):
    P, R, C = parts.shape
    tr = _row_tile(R, C)

    def body(p_ref, o_ref):
        g = p_ref[0].astype(F32)
        for i in range(1, P):
            g = g + p_ref[i].astype(F32)
        o_ref[...] = g

    return pl.pallas_call(
        body, name=name, grid=(R // tr,),
        in_specs=[pl.BlockSpec((P, tr, C), lambda i: (0, i, 0))],
        out_specs=pl.BlockSpec((tr, C), lambda i: (i, 0)),
        out_shape=S((R, C), F32),
        compiler_params=_cp("parallel"),
    )(parts)


def _adamw(parts, w, m, v, name):
    P, R, C = parts.shape
    tr = _row_tile(R, C)
    c1 = 1.0 - ADAM_B1 ** ADAM_STEP
    c2 = 1.0 - ADAM_B2 ** ADAM_STEP

    def body(p_ref, w_ref, m_ref, v_ref, g_ref, d_ref, mo_ref, vo_ref):
        g = p_ref[0].astype(F32)
        for i in range(1, P):
            g = g + p_ref[i].astype(F32)
        mn = ADAM_B1 * m_ref[...] + (1.0 - ADAM_B1) * g
        vn = ADAM_B2 * v_ref[...] + (1.0 - ADAM_B2) * (g * g)
        g_ref[...] = g
        mo_ref[...] = mn
        vo_ref[...] = vn
        d_ref[...] = (-ADAM_LR) * ((mn / c1) / (jnp.sqrt(vn / c2) + ADAM_EPS) + ADAM_WD * w_ref[...])

    blk = pl.BlockSpec((tr, C), lambda i: (i, 0))
    return pl.pallas_call(
        body, name=name, grid=(R // tr,),
        in_specs=[pl.BlockSpec((P, tr, C), lambda i: (0, i, 0)), blk, blk, blk],
        out_specs=[blk, blk, blk, blk],
        out_shape=[S((R, C), F32)] * 4,
        compiler_params=_cp("parallel"),
    )(parts, w, m, v)


def _adamw_layer(parts, w3, m3, v3, layer, prev, name):
    P, R, C = parts.shape
    NL = w3.shape[0]
    tr = _row_tile(R, C)
    c1 = 1.0 - ADAM_B1 ** ADAM_STEP
    c2 = 1.0 - ADAM_B2 ** ADAM_STEP
    n_prev = 0 if prev is None else len(prev)

    def body(p_ref, w_ref, m_ref, v_ref, *rest):
        g_ref, d_ref, mo_ref, vo_ref = rest[n_prev:]
        g = p_ref[0].astype(F32)
        for i in range(1, P):
            g = g + p_ref[i].astype(F32)
        mn = ADAM_B1 * m_ref[...] + (1.0 - ADAM_B1) * g
        vn = ADAM_B2 * v_ref[...] + (1.0 - ADAM_B2) * (g * g)
        g_ref[...] = g
        mo_ref[...] = mn
        vo_ref[...] = vn
        d_ref[...] = (-ADAM_LR) * ((mn / c1) / (jnp.sqrt(vn / c2) + ADAM_EPS) + ADAM_WD * w_ref[...])

    blk = pl.BlockSpec((None, tr, C), lambda i: (layer, i, 0))
    return pl.pallas_call(
        body, name=name, grid=(R // tr,),
        in_specs=[pl.BlockSpec((P, tr, C), lambda i: (0, i, 0)), blk, blk, blk] + [ANY] * n_prev,
        out_specs=[blk, blk, blk, blk],
        out_shape=[S((NL, R, C), F32)] * 4,
        input_output_aliases={4 + k: k for k in range(n_prev)},
        compiler_params=_cp("parallel"),
    )(parts, w3, m3, v3, *(prev or ()))


def _adamw_nd(parts, w, m, v, name):
    shp = w.shape
    C = shp[-1]
    outs = _adamw(parts.reshape(parts.shape[0], -1, C), w.reshape(-1, C), m.reshape(-1, C), v.reshape(-1, C), name)
    return [o.reshape(shp) for o in outs]


TILE_ROWS = 8


def _rows8(a):
    r = a.reshape(-1, 128)
    return jnp.pad(r, ((0, (-r.shape[0]) % TILE_ROWS), (0, 0)))


REP_SMALL_ROWS = 128


def _pack_rows(arrs, total=None):
    rows = jnp.concatenate([_rows8(a) for a in arrs], axis=0)
    if total is not None:
        rows = jnp.pad(rows, ((0, total - rows.shape[0]), (0, 0)))
    assert rows.shape[0] % (TILE_ROWS * N_DEV) == 0
    return rows


def _unpack_rows(rows, like):
    out, r = [], 0
    for a in like:
        n = a.size // 128
        out.append(rows[r:r + n].reshape(a.shape))
        r += n + (-n) % TILE_ROWS
    return out


def _small_pack(cw, ba, bx, lam):
    pad8 = lambda a: jnp.pad(a, ((0, TILE_ROWS - a.shape[0]), (0, 0)))
    return jnp.concatenate([pad8(cw[0, :, 0, :]), pad8(ba[0]), pad8(bx[0]), pad8(lam[0]),
                            jnp.zeros((PV_ROWS - PV_CONV_B, LRU_BW), F32)], axis=0)


def kernel(x, norm_mix_g, norm_mlp_g, rg_w_in, rg_conv_w, rg_conv_b, rg_w_a, rg_b_a, rg_w_x, rg_b_x, rg_lam, rg_w_out, at_w_qkv, at_q_g, at_k_g, at_w_o, mlp_w_up, mlp_w_down, final_g, loss_target, m_norm_mix_g, m_norm_mlp_g, m_rg_w_in, m_rg_conv_w, m_rg_conv_b, m_rg_w_a, m_rg_b_a, m_rg_w_x, m_rg_b_x, m_rg_lam, m_rg_w_out, m_at_w_qkv, m_at_q_g, m_at_k_g, m_at_w_o, m_mlp_w_up, m_mlp_w_down, m_final_g, v_norm_mix_g, v_norm_mlp_g, v_rg_w_in, v_rg_conv_w, v_rg_conv_b, v_rg_w_a, v_rg_b_a, v_rg_w_x, v_rg_b_x, v_rg_lam, v_rg_w_out, v_at_w_qkv, v_at_q_g, v_at_k_g, v_at_w_o, v_mlp_w_up, v_mlp_w_down, v_final_g):
    D = x.shape[-1]
    bf = lambda a: a.astype(BF16)

    sp_w = _small_pack(rg_conv_w, rg_b_a, rg_b_x, rg_lam)
    started, _ = _exchange_start(
        [[bf(rg_w_in[0]), sp_w], [bf(rg_w_out[0])], [bf(mlp_w_up[0]), bf(mlp_w_down[0])],
         [bf(at_w_qkv[0]), bf(at_w_o[0])], [bf(mlp_w_up[1]), bf(mlp_w_down[1])]],
        ["near", "near", "near", "gather", "gather"], "gather_start")
    gathers = dict(zip((0, 1, 4, 2, 3), started))
    forwards = {}

    def fetch(stage, after):
        after = tuple(after)
        if stage == 0:
            got = _exchange_wait(gathers[0], after, "near", "gather_wait0")
            g_in, g_sp = _forward_wait(_forward_start([got], "forward_start0")[0], (), "forward_wait0")
            pvec = g_sp.transpose(1, 0, 2).reshape(PV_ROWS, D)
            pvec = jnp.concatenate([pvec[:PV_CONV_B], jnp.broadcast_to(rg_conv_b, (PV_ROWS - PV_CONV_B, D))], axis=0)
            return dict(w_in=g_in, pvec=pvec)
        if stage == 1:
            near = [_exchange_wait(gathers[s], after, "near", "gather_wait%d" % s) for s in (1, 4)]
            f_out, forwards[4] = _forward_start(near, "forward_start1")
            g_out, = _forward_wait(f_out, (), "forward_wait1")
            return dict(w_out=g_out.reshape(D, D))
        if stage == 4:
            g_up0, g_dn0 = _forward_wait(forwards[4], after, "forward_wait4")
            return dict(w_up0=g_up0, w_down0=g_dn0.reshape(-1, D))
        got = _exchange_wait(gathers[stage], after, "gather", "gather_wait%d" % stage)
        if stage == 2:
            g_qkv = got[0]
            cols = g_qkv.shape[0] * g_qkv.shape[2]
            return dict(w_qkv=g_qkv.transpose(1, 0, 2).reshape(D, cols // QKV_NB, QKV_NB).transpose(1, 0, 2),
                        w_o=got[1].reshape(D, D))
        return dict(w_up1=got[0], w_down1=got[1].reshape(-1, D))

    scatters = {}

    def send(stage, g):
        if stage == 3:
            arrs = [g["w_up1"], g["w_down1"].reshape(N_DEV, -1, D)]
        elif stage == 2:
            arrs = [g["w_qkv"].reshape(D, N_DEV, -1).transpose(1, 0, 2), g["w_o"].reshape(N_DEV, -1, D)]
        elif stage == 1:
            arrs = [g["w_up0"], g["w_down0"].reshape(N_DEV, -1, D)]
        elif stage == 0:
            arrs = [g["w_in"], g["w_out"].reshape(N_DEV, -1, D),
                    g["pvec"].reshape(PV_ROWS, N_DEV, LRU_BW).transpose(1, 0, 2),
                    bf(g["wa"]).reshape(N_DEV, -1, 128), bf(g["wx"]).reshape(N_DEV, -1, 128)]
        else:
            small = _pack_rows(g["g_mix"] + g["g_mlp"] + [g["g_fin"], g["conv_b"], g["qg"], g["kg"], g["loss"]], REP_SMALL_ROWS)
            arrs = [small.reshape(N_DEV, -1, 128)]
        (group,), token = _exchange_start([arrs], ["scatter"], "scatter_start%d" % (stage % 5))
        scatters[stage] = (group, token)
        return (token,)

    w = dict(g_mix=norm_mix_g, g_mlp=norm_mlp_g, g_fin=final_g[None], qg=at_q_g, kg=at_k_g,
             wa=bf(rg_w_a[0]), wx=bf(rg_w_x[0]))
    grad_x = _local_step(x, loss_target, w, fetch, send)

    res = {}
    r_up1, r_dn1 = _exchange_wait(scatters[3][0], (scatters[-1][1],), "scatter", "scatter_wait3")
    up = _adamw_layer(r_up1, mlp_w_up, m_mlp_w_up, v_mlp_w_up, 1, None, "adam_mlp_w_up1")
    dn = _adamw_layer(r_dn1, mlp_w_down, m_mlp_w_down, v_mlp_w_down, 1, None, "adam_mlp_w_down1")
    r_qkv, r_o = _exchange_wait(scatters[2][0], (dn[0],), "scatter", "scatter_wait2")
    res["at_w_qkv"] = _adamw_nd(r_qkv[:, None], at_w_qkv, m_at_w_qkv, v_at_w_qkv, "adam_at_w_qkv")
    res["at_w_o"] = _adamw_nd(r_o[:, None], at_w_o, m_at_w_o, v_at_w_o, "adam_at_w_o")
    r_up0, r_dn0 = _exchange_wait(scatters[1][0], (res["at_w_o"][0],), "scatter", "scatter_wait1")
    res["mlp_w_up"] = _adamw_layer(r_up0, mlp_w_up, m_mlp_w_up, v_mlp_w_up, 0, up, "adam_mlp_w_up0")
    res["mlp_w_down"] = _adamw_layer(r_dn0, mlp_w_down, m_mlp_w_down, v_mlp_w_down, 0, dn, "adam_mlp_w_down0")
    r_in, r_out, r_sp, r_wa, r_wx = _exchange_wait(scatters[0][0], (res["mlp_w_down"][0],), "scatter", "scatter_wait0")
    res["rg_w_in"] = _adamw_nd(r_in[:, None], rg_w_in, m_rg_w_in, v_rg_w_in, "adam_rg_w_in")
    res["rg_w_out"] = _adamw_nd(r_out[:, None], rg_w_out, m_rg_w_out, v_rg_w_out, "adam_rg_w_out")
    sp_res = _adamw(r_sp, sp_w, _small_pack(m_rg_conv_w, m_rg_b_a, m_rg_b_x, m_rg_lam),
                    _small_pack(v_rg_conv_w, v_rg_b_a, v_rg_b_x, v_rg_lam), "adam_small")
    for o in sp_res:
        res.setdefault("rg_conv_w", []).append(o[PV_CONV_W:PV_CONV_W + CONV_W][None, :, None, :])
        res.setdefault("rg_b_a", []).append(o[PV_B_A:PV_B_A + 2][None])
        res.setdefault("rg_b_x", []).append(o[PV_B_X:PV_B_X + 2][None])
        res.setdefault("rg_lam", []).append(o[PV_LAM:PV_LAM + 2][None])

    r_small, = _exchange_wait(scatters[-1][0], (sp_res[0],), "scatter", "scatter_wait4")
    wa_sum, wx_sum, small_sum = _all_gather(
        [_sum_parts(r_wa, "reduce_w_a"), _sum_parts(r_wx, "reduce_w_x"), _sum_parts(r_small, "reduce_rep_small")],
        "gather_replicated")
    rows = lambda a: a.reshape(-1, 128)
    res["rg_w_a"] = [o.reshape(rg_w_a.shape) for o in _adamw(
        wa_sum.reshape(1, -1, 128), rows(rg_w_a), rows(m_rg_w_a), rows(v_rg_w_a), "adam_rg_w_a")]
    res["rg_w_x"] = [o.reshape(rg_w_x.shape) for o in _adamw(
        wx_sum.reshape(1, -1, 128), rows(rg_w_x), rows(m_rg_w_x), rows(v_rg_w_x), "adam_rg_w_x")]
    nil = jnp.zeros((1, 128), F32)
    w_like = [norm_mix_g, norm_mlp_g, final_g, rg_conv_b, at_q_g, at_k_g, nil]
    m_like = [m_norm_mix_g, m_norm_mlp_g, m_final_g, m_rg_conv_b, m_at_q_g, m_at_k_g, nil]
    v_like = [v_norm_mix_g, v_norm_mlp_g, v_final_g, v_rg_conv_b, v_at_q_g, v_at_k_g, nil]
    names = ["norm_mix_g", "norm_mlp_g", "final_g", "rg_conv_b", "at_q_g", "at_k_g", "loss"]
    outs = _adamw(small_sum.reshape(1, -1, 128), _pack_rows(w_like, REP_SMALL_ROWS), _pack_rows(m_like, REP_SMALL_ROWS),
                  _pack_rows(v_like, REP_SMALL_ROWS), "adam_rep_small")
    for o in outs:
        for nm, val in zip(names, _unpack_rows(o, w_like)):
            res.setdefault(nm, []).append(val)
    loss = res["loss"][0][0, 0]

    order = ["norm_mix_g", "norm_mlp_g", "rg_w_in", "rg_conv_w", "rg_conv_b", "rg_w_a", "rg_b_a", "rg_w_x", "rg_b_x",
             "rg_lam", "rg_w_out", "at_w_qkv", "at_q_g", "at_k_g", "at_w_o", "mlp_w_up", "mlp_w_down", "final_g"]
    return (loss, grad_x, *[res[nm][k] for k in range(4) for nm in order])
```

```python
import functools
import math

import jax
import jax.numpy as jnp
from jax import lax
from jax.experimental import pallas as pl
from jax.experimental.pallas import tpu as pltpu

F32 = jnp.float32
BF16 = jnp.bfloat16
S = jax.ShapeDtypeStruct

EPS = 1e-6
HEAD_DIM = 128
N_KV = 2
GRID_W = 64
ROPE_THETA = 10000.0
LRU_BW = 128
RG_C = 8.0
CONV_W = 4
N_DEV = 8
N_SEG = 8
SCAN_UNROLL = 8
TN_STEP_COLS = 512
VMEM_LIMIT_V7X = 56 * 1024 * 1024
SOFTMAX_SCALE = 1.0 / math.sqrt(HEAD_DIM)
GELU_K = math.sqrt(2.0 / math.pi)
GELU_C = 0.044715

ADAM_LR = 0.001
ADAM_B1 = 0.9
ADAM_B2 = 0.999
ADAM_EPS = 1e-08
ADAM_WD = 0.01
ADAM_STEP = 10

NT = (((1,), (1,)), ((), ()))
TN = (((0,), (0,)), ((), ()))


def _cp(*sem):
    return pltpu.CompilerParams(dimension_semantics=sem, vmem_limit_bytes=VMEM_LIMIT_V7X)


def _rms_r(xv):
    return lax.rsqrt(jnp.mean(xv * xv, axis=-1, keepdims=True) + EPS)


def _rms_bwd(dh, xv, g):
    r = _rms_r(xv)
    xh = xv * r
    dg = jnp.sum(dh * xh, axis=0, keepdims=True)
    dxh = dh * g
    dx = r * (dxh - xh * jnp.mean(dxh * xh, axis=-1, keepdims=True))
    return dx, dg


def _dot(a, b):
    return jnp.dot(a, b, preferred_element_type=F32)


def _dot_nt(a, b):
    return lax.dot_general(a, b, NT, preferred_element_type=F32)


def _dot_tn(a, b):
    return lax.dot_general(a, b, TN, preferred_element_type=F32)


def _norm_matmul(x, g, wblk, name, out_dtype=F32):
    T, D = x.shape
    NB, _, nb = wblk.shape
    tm = min(T, 512)

    def body(x_ref, g_ref, w_ref, o_ref, h_ref):
        xv = x_ref[...]
        hb = (xv * _rms_r(xv) * g_ref[...]).astype(BF16)
        h_ref[...] = hb
        for q in range(NB):
            o_ref[:, q * nb:(q + 1) * nb] = _dot(hb, w_ref[q]).astype(o_ref.dtype)

    return pl.pallas_call(
        body, name=name, grid=(T // tm,),
        in_specs=[pl.BlockSpec((tm, D), lambda i: (i, 0)),
                  pl.BlockSpec((1, D), lambda i: (0, 0)),
                  pl.BlockSpec((NB, D, nb), lambda i: (0, 0, 0))],
        out_specs=[pl.BlockSpec((tm, NB * nb), lambda i: (i, 0)),
                   pl.BlockSpec((tm, D), lambda i: (i, 0))],
        out_shape=[S((T, NB * nb), out_dtype), S((T, D), BF16)],
        compiler_params=_cp("parallel"),
    )(x, g, wblk)


def _matmul_res(a, w, res, name):
    T, K = a.shape
    N = w.shape[1]
    tm = min(T, 512)

    def body(a_ref, w_ref, r_ref, o_ref):
        o_ref[...] = r_ref[...] + _dot(a_ref[...], w_ref[...])

    return pl.pallas_call(
        body, name=name, grid=(T // tm,),
        in_specs=[pl.BlockSpec((tm, K), lambda i: (i, 0)),
                  pl.BlockSpec((K, N), lambda i: (0, 0)),
                  pl.BlockSpec((tm, N), lambda i: (i, 0))],
        out_specs=pl.BlockSpec((tm, N), lambda i: (i, 0)),
        out_shape=S((T, N), F32),
        compiler_params=_cp("parallel"),
    )(a, w, res)


def _matmul_nt(a, w, name, out_dtype, after=()):
    T, N = a.shape
    K = w.shape[0]
    tm = min(T, 512)

    def body(a_ref, w_ref, *rest):
        o_ref, ab_ref = rest[len(after):]
        ab = a_ref[...].astype(BF16)
        ab_ref[...] = ab
        o_ref[...] = _dot_nt(ab, w_ref[...]).astype(o_ref.dtype)

    return pl.pallas_call(
        body, name=name, grid=(T // tm,),
        in_specs=[pl.BlockSpec((tm, N), lambda i: (i, 0)),
                  pl.BlockSpec((K, N), lambda i: (0, 0))] + [pl.BlockSpec(memory_space=pl.ANY)] * len(after),
        out_specs=[pl.BlockSpec((tm, K), lambda i: (i, 0)),
                   pl.BlockSpec((tm, N), lambda i: (i, 0))],
        out_shape=[S((T, K), out_dtype), S((T, N), BF16)],
        compiler_params=_cp("parallel"),
    )(a, w, *after)


def _matmul_tn(a, b3, nb, name, blocked, after=()):
    T, M = a.shape
    SB, _, N = b3.shape
    per = N // nb
    NB = SB * per
    tk = min(T, 1024)
    nk = T // tk
    jb = max(1, TN_STEP_COLS // nb) if blocked else 1
    assert per % jb == 0
    if blocked:
        out_spec, out_shape = pl.BlockSpec((jb, M, nb), lambda j, k: (j, 0, 0)), S((NB, M, nb), BF16)
    else:
        assert SB == 1
        out_spec, out_shape = pl.BlockSpec((M, nb), lambda j, k: (0, j)), S((M, N), BF16)

    def body(a_ref, b_ref, *rest):
        o_ref, acc_ref = rest[len(after):]
        k = pl.program_id(1)

        @pl.when(k == 0)
        def _():
            acc_ref[...] = jnp.zeros_like(acc_ref)

        av = a_ref[...]
        for q in range(jb):
            acc_ref[q] += _dot_tn(av, b_ref[:, q * nb:(q + 1) * nb])

        @pl.when(k == nk - 1)
        def _():
            if blocked:
                o_ref[...] = acc_ref[...].astype(BF16)
            else:
                o_ref[...] = acc_ref[0].astype(BF16)

    return pl.pallas_call(
        body, name=name, grid=(NB // jb, nk),
        in_specs=[pl.BlockSpec((tk, M), lambda j, k: (k, 0)),
                  pl.BlockSpec((None, tk, jb * nb), lambda j, k: ((j * jb) // per, k, ((j * jb) % per) // jb))]
        + [pl.BlockSpec(memory_space=pl.ANY)] * len(after),
        out_specs=out_spec,
        out_shape=out_shape,
        scratch_shapes=[pltpu.VMEM((jb, M, nb), F32)],
        compiler_params=_cp("parallel", "arbitrary"),
    )(a, b3, *after)


def _nt_normbwd(dz3, wblk, x, g, dres, name, after=()):
    T, D = x.shape
    NB, _, nb = wblk.shape
    SB, _, N = dz3.shape
    per = N // nb
    tm = min(T, 512)

    def body(dz_ref, w_ref, x_ref, g_ref, dr_ref, *rest):
        dx_ref, dg_ref = rest[len(after):]

        @pl.when(pl.program_id(0) == 0)
        def _():
            dg_ref[...] = jnp.zeros_like(dg_ref)

        dh = None
        for q in range(NB):
            cols = slice((q % per) * nb, (q % per + 1) * nb)
            part = _dot_nt(dz_ref[q // per, :, cols], w_ref[q])
            dh = part if dh is None else dh + part
        dx, dg = _rms_bwd(dh, x_ref[...], g_ref[...])
        dx_ref[...] = dr_ref[...] + dx
        dg_ref[...] += dg

    return pl.pallas_call(
        body, name=name, grid=(T // tm,),
        in_specs=[pl.BlockSpec((SB, tm, N), lambda i: (0, i, 0)),
                  pl.BlockSpec((NB, D, nb), lambda i: (0, 0, 0)),
                  pl.BlockSpec((tm, D), lambda i: (i, 0)),
                  pl.BlockSpec((1, D), lambda i: (0, 0)),
                  pl.BlockSpec((tm, D), lambda i: (i, 0))] + [pl.BlockSpec(memory_space=pl.ANY)] * len(after),
        out_specs=[pl.BlockSpec((tm, D), lambda i: (i, 0)),
                   pl.BlockSpec((1, D), lambda i: (0, 0))],
        out_shape=[S((T, D), F32), S((1, D), F32)],
        compiler_params=_cp("arbitrary"),
    )(dz3, wblk, x, g, dres, *after)


def _loss_head(xv, tv, gv, D):
    err = xv * _rms_r(xv) * gv - tv
    e2 = jnp.sum(jnp.sum(err * err, axis=-1, keepdims=True), axis=0, keepdims=True)
    dx, dg = _rms_bwd(err * (1.0 / D), xv, gv)
    return (0.5 / D) * e2, dx, dg


def _mlp_fwd(x, g, wup, wdown, name, head=None):
    T, D = x.shape
    NB, _, fb = wup.shape
    tm = min(T, 1024)
    n_head = 0 if head is None else 2

    def body(x_ref, g_ref, wu_ref, wd_ref, *rest):
        xo_ref, a_ref, h_ref = rest[n_head:n_head + 3]
        acc_ref = rest[-1]
        i, j = pl.program_id(0), pl.program_id(1)

        @pl.when(j == 0)
        def _():
            xv = x_ref[...]
            h_ref[...] = (xv * _rms_r(xv) * g_ref[...]).astype(BF16)
            acc_ref[...] = xv

        a = _dot(h_ref[...], wu_ref[...])
        a_ref[...] = a.astype(BF16)
        u = jnp.maximum(a, 0.0)
        acc_ref[...] += _dot((u * u).astype(BF16), wd_ref[...])

        if head is None:
            @pl.when(j == NB - 1)
            def _():
                xo_ref[...] = acc_ref[...]
        else:
            t_ref, gf_ref = rest[:2]
            loss_ref, dgf_ref = rest[n_head + 3:n_head + 5]

            @pl.when((i == 0) & (j == 0))
            def _():
                loss_ref[...] = jnp.zeros_like(loss_ref)
                dgf_ref[...] = jnp.zeros_like(dgf_ref)

            @pl.when(j == NB - 1)
            def _():
                e2, dx, dg = _loss_head(acc_ref[...], t_ref[...], gf_ref[...], D)
                xo_ref[...] = dx
                loss_ref[...] += e2
                dgf_ref[...] += dg

    row = pl.BlockSpec((tm, D), lambda i, j: (i, 0))
    vec = pl.BlockSpec((1, D), lambda i, j: (0, 0))
    in_specs = [row, vec, pl.BlockSpec((None, D, fb), lambda i, j: (j, 0, 0)), pl.BlockSpec((fb, D), lambda i, j: (j, 0))]
    out_specs = [row, pl.BlockSpec((tm, fb), lambda i, j: (i, j)), row]
    out_shape = [S((T, D), F32), S((T, NB * fb), BF16), S((T, D), BF16)]
    if head is not None:
        in_specs += [row, vec]
        out_specs += [pl.BlockSpec((1, 128), lambda i, j: (0, 0)), vec]
        out_shape += [S((1, 128), F32), S((1, D), F32)]
    return pl.pallas_call(
        body, name=name, grid=(T // tm, NB),
        in_specs=in_specs, out_specs=out_specs, out_shape=out_shape,
        scratch_shapes=[pltpu.VMEM((tm, D), F32)],
        compiler_params=_cp("parallel" if head is None else "arbitrary", "arbitrary"),
    )(x, g, wup, wdown, *(head or ()))


def _mlp_bwd_dx(x, dout, a, g, wup, wdown, name):
    T, D = x.shape
    NB, _, fb = wup.shape
    tm = min(T, 1024)

    def body(x_ref, do_ref, a_ref, g_ref, wu_ref, wd_ref, dx_ref, da_ref, dob_ref, dg_ref, acc_ref):
        i, j = pl.program_id(0), pl.program_id(1)

        @pl.when(j == 0)
        def _():
            dob_ref[...] = do_ref[...].astype(BF16)
            acc_ref[...] = jnp.zeros_like(acc_ref)

        @pl.when((i == 0) & (j == 0))
        def _():
            dg_ref[...] = jnp.zeros_like(dg_ref)

        du2 = _dot_nt(dob_ref[...], wd_ref[...])
        u = jnp.maximum(a_ref[...].astype(F32), 0.0)
        da = (du2 * (2.0 * u)).astype(BF16)
        da_ref[...] = da
        acc_ref[...] += _dot_nt(da, wu_ref[...])

        @pl.when(j == NB - 1)
        def _():
            dx, dg = _rms_bwd(acc_ref[...], x_ref[...], g_ref[...])
            dx_ref[...] = do_ref[...] + dx
            dg_ref[...] += dg

    return pl.pallas_call(
        body, name=name, grid=(T // tm, NB),
        in_specs=[pl.BlockSpec((tm, D), lambda i, j: (i, 0)),
                  pl.BlockSpec((tm, D), lambda i, j: (i, 0)),
                  pl.BlockSpec((tm, fb), lambda i, j: (i, j)),
                  pl.BlockSpec((1, D), lambda i, j: (0, 0)),
                  pl.BlockSpec((None, D, fb), lambda i, j: (j, 0, 0)),
                  pl.BlockSpec((fb, D), lambda i, j: (j, 0))],
        out_specs=[pl.BlockSpec((tm, D), lambda i, j: (i, 0)),
                   pl.BlockSpec((tm, fb), lambda i, j: (i, j)),
                   pl.BlockSpec((tm, D), lambda i, j: (i, 0)),
                   pl.BlockSpec((1, D), lambda i, j: (0, 0))],
        out_shape=[S((T, D), F32), S((T, NB * fb), BF16), S((T, D), BF16), S((1, D), F32)],
        scratch_shapes=[pltpu.VMEM((tm, D), F32)],
        compiler_params=_cp("arbitrary", "arbitrary"),
    )(x, dout, a, g, wup, wdown)


def _mlp_bwd_dw(h, da, a, dob, fb, name):
    T, D = h.shape
    F = a.shape[1]
    NB = F // fb
    tk = min(T, 1024)
    nk = T // tk

    def body(h_ref, da_ref, a_ref, dob_ref, dwu_ref, dwd_ref, au_ref, ad_ref):
        k = pl.program_id(1)

        @pl.when(k == 0)
        def _():
            au_ref[...] = jnp.zeros_like(au_ref)
            ad_ref[...] = jnp.zeros_like(ad_ref)

        au_ref[...] += _dot_tn(h_ref[...], da_ref[...])
        u = jnp.maximum(a_ref[...].astype(F32), 0.0)
        ad_ref[...] += _dot_tn((u * u).astype(BF16), dob_ref[...])

        @pl.when(k == nk - 1)
        def _():
            dwu_ref[...] = au_ref[...].astype(BF16)
            dwd_ref[...] = ad_ref[...].astype(BF16)

    return pl.pallas_call(
        body, name=name, grid=(NB, nk),
        in_specs=[pl.BlockSpec((tk, D), lambda j, k: (k, 0)),
                  pl.BlockSpec((tk, fb), lambda j, k: (k, j)),
                  pl.BlockSpec((tk, fb), lambda j, k: (k, j)),
                  pl.BlockSpec((tk, D), lambda j, k: (k, 0))],
        out_specs=[pl.BlockSpec((None, D, fb), lambda j, k: (j, 0, 0)),
                   pl.BlockSpec((fb, D), lambda j, k: (j, 0))],
        out_shape=[S((NB, D, fb), BF16), S((F, D), BF16)],
        scratch_shapes=[pltpu.VMEM((D, fb), F32), pltpu.VMEM((fb, D), F32)],
        compiler_params=_cp("parallel", "arbitrary"),
    )(h, da, a, dob)


def _rope_tables(L):
    nf = HEAD_DIM // 4
    t = jnp.arange(L, dtype=jnp.int32)
    row = (t // GRID_W).astype(F32)
    col = (t % GRID_W).astype(F32)
    inv = ROPE_THETA ** (-jnp.arange(nf, dtype=F32) / nf)
    ar = row[:, None] * inv
    ac = col[:, None] * inv
    cos = jnp.concatenate([jnp.cos(ar), jnp.cos(ar), jnp.cos(ac), jnp.cos(ac)], axis=-1)
    sin = jnp.concatenate([-jnp.sin(ar), jnp.sin(ar), -jnp.sin(ac), jnp.sin(ac)], axis=-1)
    return cos, sin


def _swap32(x):
    lane = lax.broadcasted_iota(jnp.int32, x.shape, 1)
    up = pltpu.roll(x, HEAD_DIM - 32, 1)
    down = pltpu.roll(x, 32, 1)
    return jnp.where((lane % 64) < 32, up, down)


def _qk_prep(qkv, qg, kg, cos, sin, L, name):
    T, W = qkv.shape
    nh = W // HEAD_DIM - 2 * N_KV
    tm = min(L, 512)
    lb = L // tm

    def body(qkv_ref, qg_ref, kg_ref, cos_ref, sin_ref, q_ref, k_ref, v_ref):
        c = cos_ref[...]
        s = sin_ref[...]
        for h in range(nh + N_KV):
            xh = qkv_ref[:, h * HEAD_DIM:(h + 1) * HEAD_DIM]
            gv = qg_ref[...] if h < nh else kg_ref[...]
            y = xh * _rms_r(xh) * gv
            y = (y * c + _swap32(y) * s).astype(BF16)
            if h < nh:
                q_ref[:, h * HEAD_DIM:(h + 1) * HEAD_DIM] = y
            else:
                k_ref[:, (h - nh) * HEAD_DIM:(h - nh + 1) * HEAD_DIM] = y
        v_ref[...] = qkv_ref[:, (nh + N_KV) * HEAD_DIM:].astype(BF16)

    return pl.pallas_call(
        body, name=name, grid=(T // tm,),
        in_specs=[pl.BlockSpec((tm, W), lambda i: (i, 0)),
                  pl.BlockSpec((1, HEAD_DIM), lambda i: (0, 0)),
                  pl.BlockSpec((1, HEAD_DIM), lambda i: (0, 0)),
                  pl.BlockSpec((tm, HEAD_DIM), lambda i: (i % lb, 0)),
                  pl.BlockSpec((tm, HEAD_DIM), lambda i: (i % lb, 0))],
        out_specs=[pl.BlockSpec((tm, nh * HEAD_DIM), lambda i: (i, 0)),
                   pl.BlockSpec((tm, N_KV * HEAD_DIM), lambda i: (i, 0)),
                   pl.BlockSpec((tm, N_KV * HEAD_DIM), lambda i: (i, 0))],
        out_shape=[S((T, nh * HEAD_DIM), BF16), S((T, N_KV * HEAD_DIM), BF16), S((T, N_KV * HEAD_DIM), BF16)],
        compiler_params=_cp("parallel"),
    )(qkv, qg, kg, cos, sin)


def _qk_prep_bwd(qkv, dq, dk, dv, qg, kg, cos, sin, L, name):
    T, W = qkv.shape
    nh = W // HEAD_DIM - 2 * N_KV
    tm = min(L, 512)
    lb = L // tm

    def body(qkv_ref, dq_ref, dk_ref, dv_ref, qg_ref, kg_ref, cos_ref, sin_ref, dz_ref, dqg_ref, dkg_ref):
        @pl.when(pl.program_id(0) == 0)
        def _():
            dqg_ref[...] = jnp.zeros_like(dqg_ref)
            dkg_ref[...] = jnp.zeros_like(dkg_ref)

        c = cos_ref[...]
        s = sin_ref[...]
        for h in range(nh + N_KV):
            cols = slice(h * HEAD_DIM, (h + 1) * HEAD_DIM)
            if h < nh:
                dout, gv, dg_ref = dq_ref[:, cols], qg_ref[...], dqg_ref
            else:
                kc = slice((h - nh) * HEAD_DIM, (h - nh + 1) * HEAD_DIM)
                dout, gv, dg_ref = dk_ref[:, kc], kg_ref[...], dkg_ref
            dy = dout * c - _swap32(dout) * s
            dx, dg = _rms_bwd(dy, qkv_ref[:, cols], gv)
            dg_ref[...] += dg
            dz_ref[:, cols] = dx.astype(BF16)
        dz_ref[:, (nh + N_KV) * HEAD_DIM:] = dv_ref[...].astype(BF16)

    return pl.pallas_call(
        body, name=name, grid=(T // tm,),
        in_specs=[pl.BlockSpec((tm, W), lambda i: (i, 0)),
                  pl.BlockSpec((tm, nh * HEAD_DIM), lambda i: (i, 0)),
                  pl.BlockSpec((tm, N_KV * HEAD_DIM), lambda i: (i, 0)),
                  pl.BlockSpec((tm, N_KV * HEAD_DIM), lambda i: (i, 0)),
                  pl.BlockSpec((1, HEAD_DIM), lambda i: (0, 0)),
                  pl.BlockSpec((1, HEAD_DIM), lambda i: (0, 0)),
                  pl.BlockSpec((tm, HEAD_DIM), lambda i: (i % lb, 0)),
                  pl.BlockSpec((tm, HEAD_DIM), lambda i: (i % lb, 0))],
        out_specs=[pl.BlockSpec((tm, W), lambda i: (i, 0)),
                   pl.BlockSpec((1, HEAD_DIM), lambda i: (0, 0)),
                   pl.BlockSpec((1, HEAD_DIM), lambda i: (0, 0))],
        out_shape=[S((T, W), BF16), S((1, HEAD_DIM), F32), S((1, HEAD_DIM), F32)],
        compiler_params=_cp("arbitrary"),
    )(qkv, dq, dk, dv, qg, kg, cos, sin)


EXP2_SCALE = SOFTMAX_SCALE * math.log2(math.e)
ATTN_SUB = 128
ATTN_TQ = 1024


def _softmax_rows(q, k):
    s = _dot_nt(q, k)
    e = jnp.exp2((s - jnp.max(s, axis=-1, keepdims=True)) * EXP2_SCALE)
    return e, jnp.sum(e, axis=-1, keepdims=True)


def _attn_fwd(q, k, v, L, name):
    T = q.shape[0]
    nh = q.shape[1] // HEAD_DIM
    G = nh // N_KV
    B = T // L
    tq = min(L, ATTN_TQ)
    nq = L // tq
    sub = min(tq, ATTN_SUB)

    def body(q_ref, k_ref, v_ref, o_ref):
        for h in range(tq // sub):
            rows = slice(h * sub, (h + 1) * sub)
            e, l = _softmax_rows(q_ref[rows, :], k_ref[...])
            o_ref[rows, :] = (_dot(e.astype(BF16), v_ref[...]) / l).astype(BF16)

    qspec = pl.BlockSpec((tq, HEAD_DIM), lambda b, kv, g, qi: (b * nq + qi, kv * G + g))
    kspec = pl.BlockSpec((L, HEAD_DIM), lambda b, kv, g, qi: (b, kv))
    return pl.pallas_call(
        body, name=name, grid=(B, N_KV, G, nq),
        in_specs=[qspec, kspec, kspec],
        out_specs=qspec,
        out_shape=S((T, nh * HEAD_DIM), BF16),
        compiler_params=_cp("parallel", "parallel", "parallel", "parallel"),
    )(q, k, v)


def _attn_bwd(q, k, v, do, o, L, name):
    T = q.shape[0]
    nh = q.shape[1] // HEAD_DIM
    G = nh // N_KV
    B = T // L
    tq = min(L, ATTN_TQ)
    nq = L // tq

    sub = min(tq, ATTN_SUB)

    def body(q_ref, k_ref, v_ref, do_ref, o_ref, dq_ref, dk_ref, dv_ref, ds_scr, p_scr):
        first = (pl.program_id(2) == 0) & (pl.program_id(3) == 0)
        last = (pl.program_id(2) == G - 1) & (pl.program_id(3) == nq - 1)

        @pl.when(first)
        def _():
            dk_ref[...] = jnp.zeros_like(dk_ref)
            dv_ref[...] = jnp.zeros_like(dv_ref)

        for h in range(tq // sub):
            rows = slice(h * sub, (h + 1) * sub)
            dov = do_ref[rows, :]
            e, l = _softmax_rows(q_ref[rows, :], k_ref[...])
            p = e * (1.0 / l)
            dsum = jnp.sum(dov.astype(F32) * o_ref[rows, :].astype(F32), axis=-1, keepdims=True)
            ds_scr[rows, :] = (p * (_dot_nt(dov, v_ref[...]) - dsum)).astype(BF16)
            p_scr[rows, :] = p.astype(BF16)
        ds = ds_scr[...]
        dq_ref[...] = _dot(ds, k_ref[...]) * SOFTMAX_SCALE
        dk_ref[...] += _dot_tn(ds, q_ref[...])
        dv_ref[...] += _dot_tn(p_scr[...], do_ref[...])

        @pl.when(last)
        def _():
            dk_ref[...] = dk_ref[...] * SOFTMAX_SCALE

    qspec = pl.BlockSpec((tq, HEAD_DIM), lambda b, kv, g, qi: (b * nq + qi, kv * G + g))
    kspec = pl.BlockSpec((L, HEAD_DIM), lambda b, kv, g, qi: (b, kv))
    return pl.pallas_call(
        body, name=name, grid=(B, N_KV, G, nq),
        in_specs=[qspec, kspec, kspec, qspec, qspec],
        out_specs=[qspec, kspec, kspec],
        out_shape=[S((T, nh * HEAD_DIM), F32), S((T, N_KV * HEAD_DIM), F32), S((T, N_KV * HEAD_DIM), F32)],
        scratch_shapes=[pltpu.VMEM((tq, L), BF16), pltpu.VMEM((tq, L), BF16)],
        compiler_params=_cp("parallel", "parallel", "arbitrary", "arbitrary"),
    )(q, k, v, do, o)


PV_CONV_W = 0
PV_B_A = 8
PV_B_X = 16
PV_LAM = 24
PV_CONV_B = 32
PV_ROWS = 40


def _shift_rows(x, k):
    if k == 0:
        return x
    L = x.shape[0]
    n = N_SEG * abs(k)
    seg = lax.broadcasted_iota(jnp.int32, (n, x.shape[1]), 0) % N_SEG
    if k > 0:
        edge = jnp.where(seg == 0, 0.0, pltpu.roll(x[L - n:], 1, 0))
        return jnp.concatenate([edge, x[:L - n]], axis=0)
    edge = jnp.where(seg == N_SEG - 1, 0.0, pltpu.roll(x[:n], n - 1, 0))
    return jnp.concatenate([x[n:], edge], axis=0)


def _conv_taps(rec, pv):
    c = pv[PV_CONV_B:PV_CONV_B + 1]
    for j in range(CONV_W):
        c = c + pv[PV_CONV_W + j:PV_CONV_W + j + 1] * _shift_rows(rec, 2 - j)
    return c


def _sigmoid(x):
    return 0.5 * jnp.tanh(0.5 * x) + 0.5


EXPM1_SERIES_BELOW = 0.03


def _rg_gates(c, cbf, wa, wx, ba, bx, lam):
    r = _sigmoid(_dot(cbf, wa) + ba)
    i = _sigmoid(_dot(cbf, wx) + bx)
    sp = jnp.maximum(-lam, 0.0) + jnp.log1p(jnp.exp(-jnp.abs(lam)))
    la = r * ((-RG_C) * sp)
    a = jnp.exp(la)
    a2 = a * a
    x = la + la
    series = -(x * ((x * (1.0 / 6.0) + 0.5) * x + 1.0))
    om = jnp.where(x > -EXPM1_SERIES_BELOW, series, 1.0 - a2)
    rm = lax.rsqrt(om)
    return r, i, a, om * rm, rm, a2, sp


def _gelu(x):
    t = jnp.tanh(GELU_K * (x + GELU_C * x * x * x))
    return 0.5 * x * (1.0 + t), t


def _scan_pair(af_ref, uf_ref, ab_ref, ub_ref, hf_ref, hb_ref, pf_ref, pb_ref, L):
    ls = L // N_SEG
    zero = jnp.zeros((N_SEG, LRU_BW), F32)
    one = jnp.ones((N_SEG, LRU_BW), F32)
    tile = lambda t: pl.ds(pl.multiple_of(t * N_SEG, N_SEG), N_SEG)

    def steps(tc, carry):
        hf, pf, hb, pb = carry
        for q in range(SCAN_UNROLL):
            t = tc * SCAN_UNROLL + q
            rf, rb = tile(t), tile(ls - 1 - t)
            af = af_ref[rf, :]
            hf = af * hf + uf_ref[rf, :]
            pf = pf * af
            hf_ref[rf, :] = hf
            pf_ref[rf, :] = pf
            ab = ab_ref[rb, :]
            hb = ab * hb + ub_ref[rb, :]
            pb = pb * ab
            hb_ref[rb, :] = hb
            pb_ref[rb, :] = pb
        return hf, pf, hb, pb

    hf_e, pf_e, hb_e, pb_e = lax.fori_loop(0, ls // SCAN_UNROLL, steps, (zero, one, zero, one))

    rows, cin = [], jnp.zeros((1, LRU_BW), F32)
    for s in range(N_SEG):
        rows.append(cin)
        cin = hf_e[s:s + 1] + pf_e[s:s + 1] * cin
    cf = jnp.concatenate(rows, axis=0)
    rows, cin = [], jnp.zeros((1, LRU_BW), F32)
    for s in reversed(range(N_SEG)):
        rows.append(cin)
        cin = hb_e[s:s + 1] + pb_e[s:s + 1] * cin
    cb = jnp.concatenate(rows[::-1], axis=0)

    def fix(tc, _):
        for q in range(SCAN_UNROLL):
            r = tile(tc * SCAN_UNROLL + q)
            hf_ref[r, :] = hf_ref[r, :] + pf_ref[r, :] * cf
            hb_ref[r, :] = hb_ref[r, :] + pb_ref[r, :] * cb
        return 0

    lax.fori_loop(0, ls // SCAN_UNROLL, fix, 0)


def _rg_specs(L, D, nblk):
    slab = lambda off: pl.BlockSpec((L, LRU_BW), lambda cb, b: (b, off + cb))
    wspec = pl.BlockSpec((2, None, LRU_BW, LRU_BW), lambda cb, b: (0, cb, 0, 0))
    pvspec = pl.BlockSpec((PV_ROWS, LRU_BW), lambda cb, b: (0, cb))
    return slab, wspec, pvspec


def _rg_fwd(z, pvec, wa, wx, L, name):
    T, C2 = z.shape
    C = C2 // 2
    nblk = C // LRU_BW
    B = T // L
    slab, wspec, pvspec = _rg_specs(L, C, nblk)

    def body(gp_ref, rec_ref, pv_ref, wa_ref, wx_ref, yg_ref, hf_ref, hb_ref, a_scr, u_scr, p_scr):
        pv = pv_ref[...]
        c = _conv_taps(rec_ref[...], pv)
        cbf = c.astype(BF16)
        for d in range(2):
            _, i, a, m, _, _, _ = _rg_gates(c, cbf, wa_ref[d], wx_ref[d], pv[PV_B_A + d:PV_B_A + d + 1],
                                      pv[PV_B_X + d:PV_B_X + d + 1], pv[PV_LAM + d:PV_LAM + d + 1])
            a_scr[d] = a
            u_scr[d] = m * (i * c)
        _scan_pair(a_scr.at[0], u_scr.at[0], a_scr.at[1], u_scr.at[1], hf_ref, hb_ref, p_scr.at[0], p_scr.at[1], L)
        gate, _ = _gelu(gp_ref[...])
        yg_ref[...] = ((hf_ref[...] + hb_ref[...]) * gate).astype(BF16)

    return pl.pallas_call(
        body, name=name, grid=(nblk, B),
        in_specs=[slab(0), slab(nblk), pvspec, wspec, wspec],
        out_specs=[slab(0), slab(0), slab(0)],
        out_shape=[S((T, C), BF16), S((T, C), F32), S((T, C), F32)],
        scratch_shapes=[pltpu.VMEM((2, L, LRU_BW), F32)] * 3,
        compiler_params=_cp("parallel", "parallel"),
    )(z, z, pvec, wa, wx)


def _rg_bwd(z, hf, hb, dyg, pvec, wa, wx, L, name):
    T, C2 = z.shape
    C = C2 // 2
    nblk = C // LRU_BW
    B = T // L
    slab, wspec, pvspec = _rg_specs(L, C, nblk)

    def body(gp_ref, rec_ref, hf_ref, hb_ref, dyg_ref, pv_ref, wa_ref, wx_ref,
             dz_ref, dwa_ref, dwx_ref, dpv_ref, a_scr, u_scr, d_scr, p_scr):
        @pl.when(pl.program_id(1) == 0)
        def _():
            dwa_ref[...] = jnp.zeros_like(dwa_ref)
            dwx_ref[...] = jnp.zeros_like(dwx_ref)
            dpv_ref[...] = jnp.zeros_like(dpv_ref)

        pv = pv_ref[...]
        rec = rec_ref[...]
        c = _conv_taps(rec, pv)
        cbf = c.astype(BF16)
        gp = gp_ref[...]
        gate, th = _gelu(gp)
        dgelu = 0.5 * (1.0 + th) + 0.5 * gp * (1.0 - th * th) * GELU_K * (1.0 + 3.0 * GELU_C * gp * gp)
        dyg = dyg_ref[...]
        dz_ref[0] = (dyg * (hf_ref[...] + hb_ref[...]) * dgelu).astype(BF16)
        dy = dyg * gate

        gates = []
        for d in range(2):
            gates.append(_rg_gates(c, cbf, wa_ref[d], wx_ref[d], pv[PV_B_A + d:PV_B_A + d + 1],
                                   pv[PV_B_X + d:PV_B_X + d + 1], pv[PV_LAM + d:PV_LAM + d + 1]))
        a_scr[0] = _shift_rows(gates[1][2], 1)
        a_scr[1] = _shift_rows(gates[0][2], -1)
        u_scr[...] = dy
        _scan_pair(a_scr.at[0], u_scr, a_scr.at[1], u_scr, d_scr.at[1], d_scr.at[0], p_scr.at[0], p_scr.at[1], L)

        dc = jnp.zeros_like(c)
        rows = []
        for d in range(2):
            r, i, a, m, rm, a2, sp = gates[d]
            delta = d_scr[d]
            hnb = _shift_rows(hf_ref[...], 1) if d == 0 else _shift_rows(hb_ref[...], -1)
            da = delta * hnb
            dm = delta * (i * c)
            di = delta * (m * c)
            dc = dc + delta * (m * i)
            dla = da * a - dm * (a2 * rm)
            dpa = (dla * ((-RG_C) * sp)) * (r * (1.0 - r))
            dpx = di * (i * (1.0 - i))
            dsp = (-RG_C) * jnp.sum(dla * r, axis=0, keepdims=True)
            lam = pv[PV_LAM + d:PV_LAM + d + 1]
            rows.append((jnp.sum(dpa, axis=0, keepdims=True), jnp.sum(dpx, axis=0, keepdims=True),
                         -dsp * _sigmoid(-lam)))
            dpab = dpa.astype(BF16)
            dpxb = dpx.astype(BF16)
            dwa_ref[d] += _dot_tn(cbf, dpab)
            dwx_ref[d] += _dot_tn(cbf, dpxb)
            dc = dc + _dot_nt(dpab, wa_ref[d]) + _dot_nt(dpxb, wx_ref[d])

        drec = jnp.zeros_like(c)
        dcw = []
        for j in range(CONV_W):
            drec = drec + pv[PV_CONV_W + j:PV_CONV_W + j + 1] * _shift_rows(dc, j - 2)
            dcw.append(jnp.sum(dc * _shift_rows(rec, 2 - j), axis=0, keepdims=True))
        dz_ref[1] = drec.astype(BF16)
        for j in range(CONV_W):
            dpv_ref[PV_CONV_W + j:PV_CONV_W + j + 1, :] += dcw[j]
        for d in range(2):
            dpv_ref[PV_B_A + d:PV_B_A + d + 1, :] += rows[d][0]
            dpv_ref[PV_B_X + d:PV_B_X + d + 1, :] += rows[d][1]
            dpv_ref[PV_LAM + d:PV_LAM + d + 1, :] += rows[d][2]
        dpv_ref[PV_CONV_B:PV_CONV_B + 1, :] += jnp.sum(dc, axis=0, keepdims=True)

    return pl.pallas_call(
        body, name=name, grid=(nblk, B),
        in_specs=[slab(0), slab(nblk), slab(0), slab(0), slab(0), pvspec, wspec, wspec],
        out_specs=[pl.BlockSpec((2, L, LRU_BW), lambda cb, b: (0, b, cb)), wspec, wspec, pvspec],
        out_shape=[S((2, T, C), BF16), S((2, nblk, LRU_BW, LRU_BW), F32), S((2, nblk, LRU_BW, LRU_BW), F32),
                   S((PV_ROWS, C), F32)],
        scratch_shapes=[pltpu.VMEM((2, L, LRU_BW), F32), pltpu.VMEM((L, LRU_BW), F32),
                        pltpu.VMEM((2, L, LRU_BW), F32), pltpu.VMEM((2, L, LRU_BW), F32)],
        compiler_params=_cp("parallel", "arbitrary"),
    )(z, z, hf, hb, dyg, pvec, wa, wx)


QKV_NB = 512


def _interleave(a):
    *lead, L, D = a.shape
    return a.reshape(*lead, N_SEG, L // N_SEG, D).swapaxes(-3, -2).reshape(*lead, L, D)


def _deinterleave(a):
    *lead, L, D = a.shape
    return a.reshape(*lead, L // N_SEG, N_SEG, D).swapaxes(-3, -2).reshape(*lead, L, D)


def _local_step(x3, tgt3, w, fetch, send):
    Bl, L, D = x3.shape
    T = Bl * L
    x = _interleave(x3).reshape(T, D)
    tgt = _interleave(tgt3).reshape(T, D)
    gm = [w["g_mix"][i:i + 1] for i in range(2)]
    gl = [w["g_mlp"][i:i + 1] for i in range(2)]

    w0 = fetch(0, ())
    nb_in = w0["w_in"].shape[-1]
    z, h0 = _norm_matmul(x, gm[0], w0["w_in"], "rg_in")
    yg, hf, hb = _rg_fwd(z, w0["pvec"], w["wa"], w["wx"], L, "rg_fwd")
    w1 = fetch(1, (yg,))
    x1 = _matmul_res(yg, w1["w_out"], x, "rg_out")
    w1.update(fetch(4, (x1,)))
    fb = w1["w_up0"].shape[-1]
    x2, a0, hm0 = _mlp_fwd(x1, gl[0], w1["w_up0"], w1["w_down0"], "mlp0_fwd")
    w2 = fetch(2, (x2,))
    qkv, h1 = _norm_matmul(x2, gm[1], w2["w_qkv"], "at_qkv")
    cos, sin = [_interleave(t) for t in _rope_tables(L)]
    qn, kn, vb = _qk_prep(qkv, w["qg"], w["kg"], cos, sin, L, "at_prep")
    o = _attn_fwd(qn, kn, vb, L, "at_fwd")
    x3_ = _matmul_res(o, w2["w_o"], x2, "at_out")
    w3 = fetch(3, (x3_,))
    dx4, a1, hm1, loss, dgf = _mlp_fwd(x3_, gl[1], w3["w_up1"], w3["w_down1"], "mlp1_fwd", head=(tgt, w["g_fin"]))

    dx3, da1, dob1, dgl1 = _mlp_bwd_dx(x3_, dx4, a1, gl[1], w3["w_up1"], w3["w_down1"], "mlp1_bwd_dx")
    dwu1, dwd1 = _mlp_bwd_dw(hm1, da1, a1, dob1, fb, "mlp1_bwd_dw")
    sent = send(3, dict(w_up1=dwu1, w_down1=dwd1))
    do, dx3b = _matmul_nt(dx3, w2["w_o"], "at_out_bwd", BF16, after=sent)
    dwo = _matmul_tn(o, dx3b[None], QKV_NB, "at_out_dw", blocked=False)
    dq, dk, dv = _attn_bwd(qn, kn, vb, do, o, L, "at_bwd")
    dqkv, dqg, dkg = _qk_prep_bwd(qkv, dq, dk, dv, w["qg"], w["kg"], cos, sin, L, "at_prep_bwd")
    dwqkv = _matmul_tn(h1, dqkv[None], QKV_NB, "at_qkv_dw", blocked=False)
    sent = send(2, dict(w_qkv=dwqkv, w_o=dwo))
    dx2, dgm1 = _nt_normbwd(dqkv[None], w2["w_qkv"], x2, gm[1], dx3, "at_qkv_bwd", after=sent)
    dx1, da0, dob0, dgl0 = _mlp_bwd_dx(x1, dx2, a0, gl[0], w1["w_up0"], w1["w_down0"], "mlp0_bwd_dx")
    dwu0, dwd0 = _mlp_bwd_dw(hm0, da0, a0, dob0, fb, "mlp0_bwd_dw")
    sent = send(1, dict(w_up0=dwu0, w_down0=dwd0))
    dyg, dx1b = _matmul_nt(dx1, w1["w_out"], "rg_out_bwd", F32, after=sent)
    dwout = _matmul_tn(yg, dx1b[None], QKV_NB, "rg_out_dw", blocked=False)
    dz, dwa, dwx, dpv = _rg_bwd(z, hf, hb, dyg, w0["pvec"], w["wa"], w["wx"], L, "rg_bwd")
    sent = send(4, dict(w_out=dwout, pvec=dpv, wa=dwa, wx=dwx))
    dwin = _matmul_tn(h0, dz, nb_in, "rg_in_dw", blocked=True, after=sent)
    sent = send(0, dict(w_in=dwin))
    dx0, dgm0 = _nt_normbwd(dz, w0["w_in"], x, gm[0], dx1, "rg_in_bwd", after=sent)
    send(-1, dict(g_mix=[dgm0, dgm1], g_mlp=[dgl0, dgl1], g_fin=dgf, conv_b=dpv[PV_CONV_B], qg=dqg, kg=dkg, loss=loss))
    return _deinterleave(dx0.reshape(Bl, L, D))


MESH = pl.DeviceIdType.MESH
ANY = pl.BlockSpec(memory_space=pl.ANY)
N_PEERS = N_DEV - 1


def _my_place():
    return lax.axis_index("x"), lax.axis_index("y"), lax.axis_index("c")


def _flat(px, py, pc):
    return 4 * px + 2 * py + pc


def _all_gather(shards, name):
    n = len(shards)

    def body(*refs):
        ins, outs = refs[:n], refs[n:2 * n]
        send_sems, recv_sems, local_sems = refs[2 * n:]
        x, y, c = _my_place()
        me, sibling = (x, y, c), (x, y, 1 - c)
        chips = [(1 - x, y), (x, 1 - y), (1 - x, 1 - y)]

        def copy(a, k, block, to, src=None):
            dst = outs[a].at[_flat(*block)]
            return pltpu.make_async_remote_copy(
                src_ref=dst if src is None else src, dst_ref=dst,
                send_sem=send_sems.at[a, k], recv_sem=recv_sems.at[a, k],
                device_id=to, device_id_type=MESH)

        mine = [pltpu.make_async_copy(ins[a], outs[a].at[_flat(*me)], local_sems.at[a]) for a in range(n)]
        for cp in mine:
            cp.start()
        first = []
        for a in range(n):
            first.append(copy(a, 0, me, sibling, src=ins[a]))
            first += [copy(a, 1 + j, me, (*chip, c), src=ins[a]) for j, chip in enumerate(chips)]
        for cp in first:
            cp.start()
        passed = []
        for j, chip in enumerate(chips):
            for a in range(n):
                copy(a, 1 + j, (*chip, c), me).wait_recv()
                fwd = copy(a, 4 + j, (*chip, c), sibling)
                fwd.start()
                passed.append(fwd)
        for a in range(n):
            copy(a, 0, sibling, me).wait_recv()
            for j, chip in enumerate(chips):
                copy(a, 4 + j, (*chip, 1 - c), me).wait_recv()
        for cp in first + passed:
            cp.wait_send()
        for cp in mine:
            cp.wait()

    return pl.pallas_call(
        body, name=name,
        in_specs=[ANY] * n, out_specs=[ANY] * n,
        out_shape=[S((N_DEV,) + s.shape, s.dtype) for s in shards],
        scratch_shapes=[pltpu.SemaphoreType.DMA((n, N_PEERS)), pltpu.SemaphoreType.DMA((n, N_PEERS)),
                        pltpu.SemaphoreType.DMA((n,))],
    )(*shards)


HBM = pl.BlockSpec(memory_space=pltpu.HBM)
SEM = pl.BlockSpec(memory_space=pltpu.SEMAPHORE)
SIDE_EFFECT = pltpu.SideEffectType.DATAFLOW_SIDE_EFFECTING
SEMS_PER_GROUP = 3


NEAR_PEERS = (1, 2, 4, 6)
FAR_CHIPS = (2, 4, 6)


def _exchange_copies(srcs, lands, sems, mode):
    send_sems, recv_sems, local_sems = sems
    scatter = mode == "scatter"
    x, y, c = _my_place()
    me = _flat(x, y, c)
    remote, local = [], []
    for a in range(len(srcs)):
        for r in (NEAR_PEERS if mode == "near" else range(1, N_DEV)):
            peer = (1 - x if r & 4 else x, 1 - y if r & 2 else y, 1 - c if r & 1 else c)
            remote.append(pltpu.make_async_remote_copy(
                src_ref=srcs[a].at[_flat(*peer)] if scatter else srcs[a], dst_ref=lands[a].at[me],
                send_sem=send_sems.at[a * N_PEERS + r - 1], recv_sem=recv_sems.at[a * N_PEERS + r - 1],
                device_id=peer, device_id_type=MESH))
        local.append(pltpu.make_async_copy(srcs[a].at[me] if scatter else srcs[a], lands[a].at[me], local_sems.at[a]))
    return remote, local


def _exchange_start(groups, modes, name):
    sizes = [len(g) for g in groups]
    srcs = [pltpu.with_memory_space_constraint(a, pltpu.HBM) for g in groups for a in g]
    n = len(srcs)
    scatter_of = [m == "scatter" for g, m in zip(groups, modes) for _ in g]
    lands = [pltpu.with_memory_space_constraint(lax.empty(a.shape if sc else (N_DEV,) + a.shape, a.dtype), pltpu.HBM)
             for a, sc in zip(srcs, scatter_of)]
    n_sem = SEMS_PER_GROUP * len(groups)

    def body(*refs):
        src_refs, land_refs, sem_refs, token = refs[:n], refs[n:2 * n], refs[2 * n:2 * n + n_sem], refs[-1]
        off = 0
        for gi, k in enumerate(sizes):
            remote, local = _exchange_copies(src_refs[off:off + k], land_refs[off:off + k],
                                             sem_refs[SEMS_PER_GROUP * gi:SEMS_PER_GROUP * (gi + 1)], modes[gi])
            for cp in local + remote:
                cp.start()
            off += k
        token[...] = jnp.zeros_like(token)

    sem_shapes = []
    for k in sizes:
        sem_shapes += [pltpu.SemaphoreType.DMA((k * N_PEERS,)), pltpu.SemaphoreType.DMA((k * N_PEERS,)),
                       pltpu.SemaphoreType.DMA((k,))]
    outs = pl.pallas_call(
        body, name=name,
        out_shape=sem_shapes + [pltpu.HBM(a.shape, a.dtype) for a in srcs + lands] + [S((8, 128), F32)],
        in_specs=[HBM] * (2 * n),
        out_specs=[SEM] * n_sem + [HBM] * (2 * n) + [pl.BlockSpec(memory_space=pltpu.VMEM)],
        input_output_aliases={i: n_sem + i for i in range(2 * n)},
        compiler_params=pltpu.CompilerParams(has_side_effects=SIDE_EFFECT),
    )(*srcs, *lands)
    sems, thru, token = outs[:n_sem], outs[n_sem:n_sem + 2 * n], outs[-1]
    per_group, off = [], 0
    for gi, k in enumerate(sizes):
        per_group.append((sems[SEMS_PER_GROUP * gi:SEMS_PER_GROUP * (gi + 1)], thru[off:off + k], thru[n + off:n + off + k]))
        off += k
    return per_group, token


def _exchange_wait(group, after, mode, name):
    sems, srcs, lands = group
    k = len(srcs)

    def body(*refs):
        remote, local = _exchange_copies(refs[:k], refs[k:2 * k], refs[2 * k:2 * k + SEMS_PER_GROUP], mode)
        for cp in remote:
            cp.wait_send()
            cp.wait_recv()
        for cp in local:
            cp.wait()

    outs = pl.pallas_call(
        body, name=name,
        out_shape=[pltpu.HBM(a.shape, a.dtype) for a in list(srcs) + list(lands)],
        in_specs=[HBM] * (2 * k) + [SEM] * SEMS_PER_GROUP + [ANY] * len(after),
        out_specs=[HBM] * (2 * k),
        input_output_aliases={i: i for i in range(2 * k)},
        compiler_params=pltpu.CompilerParams(has_side_effects=SIDE_EFFECT),
    )(*srcs, *lands, *sems, *after)
    return outs[k:]


def _forward_copies(lands, sems):
    send_sems, recv_sems = sems
    x, y, c = _my_place()
    mine, theirs = [], []
    for a in range(len(lands)):
        for k, r in enumerate(FAR_CHIPS):
            px, py = (1 - x if r & 4 else x), (1 - y if r & 2 else y)
            for out, core in ((mine, c), (theirs, 1 - c)):
                blk = lands[a].at[_flat(px, py, core)]
                out.append(pltpu.make_async_remote_copy(
                    src_ref=blk, dst_ref=blk, send_sem=send_sems.at[a * len(FAR_CHIPS) + k],
                    recv_sem=recv_sems.at[a * len(FAR_CHIPS) + k], device_id=(x, y, 1 - c), device_id_type=MESH))
    return mine, theirs


def _forward_start(groups, name):
    sizes = [len(g) for g in groups]
    lands = [a for g in groups for a in g]
    n = len(lands)
    n_sem = 2 * len(groups)

    def body(*refs):
        land_refs, sem_refs, token = refs[:n], refs[n:n + n_sem], refs[-1]
        off = 0
        for gi, k in enumerate(sizes):
            mine, _ = _forward_copies(land_refs[off:off + k], sem_refs[2 * gi:2 * gi + 2])
            for cp in mine:
                cp.start()
            off += k
        token[...] = jnp.zeros_like(token)

    sem_shapes = []
    for k in sizes:
        sem_shapes += [pltpu.SemaphoreType.DMA((k * len(FAR_CHIPS),))] * 2
    outs = pl.pallas_call(
        body, name=name,
        out_shape=sem_shapes + [pltpu.HBM(a.shape, a.dtype) for a in lands] + [S((8, 128), F32)],
        in_specs=[HBM] * n,
        out_specs=[SEM] * n_sem + [HBM] * n + [pl.BlockSpec(memory_space=pltpu.VMEM)],
        input_output_aliases={i: n_sem + i for i in range(n)},
        compiler_params=pltpu.CompilerParams(has_side_effects=SIDE_EFFECT),
    )(*lands)
    per_group, off = [], 0
    for gi, k in enumerate(sizes):
        per_group.append((outs[2 * gi:2 * gi + 2], outs[n_sem + off:n_sem + off + k]))
        off += k
    return per_group


def _forward_wait(group, after, name):
    sems, lands = group
    k = len(lands)

    def body(*refs):
        mine, theirs = _forward_copies(refs[:k], refs[k:k + 2])
        for cp in mine:
            cp.wait_send()
        for cp in theirs:
            cp.wait_recv()

    return pl.pallas_call(
        body, name=name,
        out_shape=[pltpu.HBM(a.shape, a.dtype) for a in lands],
        in_specs=[HBM] * k + [SEM] * 2 + [ANY] * len(after),
        out_specs=[HBM] * k,
        input_output_aliases={i: i for i in range(k)},
        compiler_params=pltpu.CompilerParams(has_side_effects=SIDE_EFFECT),
    )(*lands, *sems, *after)


def _row_tile(rows, cols):
    want = max(16, (128 * 1024) // cols)
    if rows <= want:
        return rows
    t = want - want % 16
    while rows % t:
        t -= 16
    return t


def _sum_parts(parts, name):
    P, R, C = parts.shape
    tr = _row_tile(R, C)

    def body(p_ref, o_ref):
        g = p_ref[0].astype(F32)
        for i in range(1, P):
            g = g + p_ref[i].astype(F32)
        o_ref[...] = g

    return pl.pallas_call(
        body, name=name, grid=(R // tr,),
        in_specs=[pl.BlockSpec((P, tr, C), lambda i: (0, i, 0))],
        out_specs=pl.BlockSpec((tr, C), lambda i: (i, 0)),
        out_shape=S((R, C), F32),
        compiler_params=_cp("parallel"),
    )(parts)


def _adamw(parts, w, m, v, name):
    P, R, C = parts.shape
    tr = _row_tile(R, C)
    c1 = 1.0 - ADAM_B1 ** ADAM_STEP
    c2 = 1.0 - ADAM_B2 ** ADAM_STEP

    def body(p_ref, w_ref, m_ref, v_ref, g_ref, d_ref, mo_ref, vo_ref):
        g = p_ref[0].astype(F32)
        for i in range(1, P):
            g = g + p_ref[i].astype(F32)
        mn = ADAM_B1 * m_ref[...] + (1.0 - ADAM_B1) * g
        vn = ADAM_B2 * v_ref[...] + (1.0 - ADAM_B2) * (g * g)
        g_ref[...] = g
        mo_ref[...] = mn
        vo_ref[...] = vn
        d_ref[...] = (-ADAM_LR) * ((mn / c1) / (jnp.sqrt(vn / c2) + ADAM_EPS) + ADAM_WD * w_ref[...])

    blk = pl.BlockSpec((tr, C), lambda i: (i, 0))
    return pl.pallas_call(
        body, name=name, grid=(R // tr,),
        in_specs=[pl.BlockSpec((P, tr, C), lambda i: (0, i, 0)), blk, blk, blk],
        out_specs=[blk, blk, blk, blk],
        out_shape=[S((R, C), F32)] * 4,
        compiler_params=_cp("parallel"),
    )(parts, w, m, v)


def _adamw_layer(parts, w3, m3, v3, layer, prev, name):
    P, R, C = parts.shape
    NL = w3.shape[0]
    tr = _row_tile(R, C)
    c1 = 1.0 - ADAM_B1 ** ADAM_STEP
    c2 = 1.0 - ADAM_B2 ** ADAM_STEP
    n_prev = 0 if prev is None else len(prev)

    def body(p_ref, w_ref, m_ref, v_ref, *rest):
        g_ref, d_ref, mo_ref, vo_ref = rest[n_prev:]
        g = p_ref[0].astype(F32)
        for i in range(1, P):
            g = g + p_ref[i].astype(F32)
        mn = ADAM_B1 * m_ref[...] + (1.0 - ADAM_B1) * g
        vn = ADAM_B2 * v_ref[...] + (1.0 - ADAM_B2) * (g * g)
        g_ref[...] = g
        mo_ref[...] = mn
        vo_ref[...] = vn
        d_ref[...] = (-ADAM_LR) * ((mn / c1) / (jnp.sqrt(vn / c2) + ADAM_EPS) + ADAM_WD * w_ref[...])

    blk = pl.BlockSpec((None, tr, C), lambda i: (layer, i, 0))
    return pl.pallas_call(
        body, name=name, grid=(R // tr,),
        in_specs=[pl.BlockSpec((P, tr, C), lambda i: (0, i, 0)), blk, blk, blk] + [ANY] * n_prev,
        out_specs=[blk, blk, blk, blk],
        out_shape=[S((NL, R, C), F32)] * 4,
        input_output_aliases={4 + k: k for k in range(n_prev)},
        compiler_params=_cp("parallel"),
    )(parts, w3, m3, v3, *(prev or ()))


def _adamw_nd(parts, w, m, v, name):
    shp = w.shape
    C = shp[-1]
    outs = _adamw(parts.reshape(parts.shape[0], -1, C), w.reshape(-1, C), m.reshape(-1, C), v.reshape(-1, C), name)
    return [o.reshape(shp) for o in outs]


TILE_ROWS = 8


def _rows8(a):
    r = a.reshape(-1, 128)
    return jnp.pad(r, ((0, (-r.shape[0]) % TILE_ROWS), (0, 0)))


REP_SMALL_ROWS = 128


def _pack_rows(arrs, total=None):
    rows = jnp.concatenate([_rows8(a) for a in arrs], axis=0)
    if total is not None:
        rows = jnp.pad(rows, ((0, total - rows.shape[0]), (0, 0)))
    assert rows.shape[0] % (TILE_ROWS * N_DEV) == 0
    return rows


def _unpack_rows(rows, like):
    out, r = [], 0
    for a in like:
        n = a.size // 128
        out.append(rows[r:r + n].reshape(a.shape))
        r += n + (-n) % TILE_ROWS
    return out


def _small_pack(cw, ba, bx, lam):
    pad8 = lambda a: jnp.pad(a, ((0, TILE_ROWS - a.shape[0]), (0, 0)))
    return jnp.concatenate([pad8(cw[0, :, 0, :]), pad8(ba[0]), pad8(bx[0]), pad8(lam[0]),
                            jnp.zeros((PV_ROWS - PV_CONV_B, LRU_BW), F32)], axis=0)


def kernel(x, norm_mix_g, norm_mlp_g, rg_w_in, rg_conv_w, rg_conv_b, rg_w_a, rg_b_a, rg_w_x, rg_b_x, rg_lam, rg_w_out, at_w_qkv, at_q_g, at_k_g, at_w_o, mlp_w_up, mlp_w_down, final_g, loss_target, m_norm_mix_g, m_norm_mlp_g, m_rg_w_in, m_rg_conv_w, m_rg_conv_b, m_rg_w_a, m_rg_b_a, m_rg_w_x, m_rg_b_x, m_rg_lam, m_rg_w_out, m_at_w_qkv, m_at_q_g, m_at_k_g, m_at_w_o, m_mlp_w_up, m_mlp_w_down, m_final_g, v_norm_mix_g, v_norm_mlp_g, v_rg_w_in, v_rg_conv_w, v_rg_conv_b, v_rg_w_a, v_rg_b_a, v_rg_w_x, v_rg_b_x, v_rg_lam, v_rg_w_out, v_at_w_qkv, v_at_q_g, v_at_k_g, v_at_w_o, v_mlp_w_up, v_mlp_w_down, v_final_g):
    D = x.shape[-1]
    bf = lambda a: a.astype(BF16)

    sp_w = _small_pack(rg_conv_w, rg_b_a, rg_b_x, rg_lam)
    started, _ = _exchange_start(
        [[bf(rg_w_in[0]), sp_w], [bf(rg_w_out[0])], [bf(mlp_w_up[0]), bf(mlp_w_down[0])],
         [bf(at_w_qkv[0]), bf(at_w_o[0])], [bf(mlp_w_up[1]), bf(mlp_w_down[1])]],
        ["near", "near", "near", "gather", "gather"], "gather_start")
    gathers = dict(zip((0, 1, 4, 2, 3), started))
    forwards = {}

    def fetch(stage, after):
        after = tuple(after)
        if stage == 0:
            got = _exchange_wait(gathers[0], after, "near", "gather_wait0")
            g_in, g_sp = _forward_wait(_forward_start([got], "forward_start0")[0], (), "forward_wait0")
            pvec = g_sp.transpose(1, 0, 2).reshape(PV_ROWS, D)
            pvec = jnp.concatenate([pvec[:PV_CONV_B], jnp.broadcast_to(rg_conv_b, (PV_ROWS - PV_CONV_B, D))], axis=0)
            return dict(w_in=g_in, pvec=pvec)
        if stage == 1:
            near = [_exchange_wait(gathers[s], after, "near", "gather_wait%d" % s) for s in (1, 4)]
            f_out, forwards[4] = _forward_start(near, "forward_start1")
            g_out, = _forward_wait(f_out, (), "forward_wait1")
            return dict(w_out=g_out.reshape(D, D))
        if stage == 4:
            g_up0, g_dn0 = _forward_wait(forwards[4], after, "forward_wait4")
            return dict(w_up0=g_up0, w_down0=g_dn0.reshape(-1, D))
        got = _exchange_wait(gathers[stage], after, "gather", "gather_wait%d" % stage)
        if stage == 2:
            g_qkv = got[0]
            cols = g_qkv.shape[0] * g_qkv.shape[2]
            return dict(w_qkv=g_qkv.transpose(1, 0, 2).reshape(D, cols // QKV_NB, QKV_NB).transpose(1, 0, 2),
                        w_o=got[1].reshape(D, D))
        return dict(w_up1=got[0], w_down1=got[1].reshape(-1, D))

    scatters = {}

    def send(stage, g):
        if stage == 3:
            arrs = [g["w_up1"], g["w_down1"].reshape(N_DEV, -1, D)]
        elif stage == 2:
            arrs = [g["w_qkv"].reshape(D, N_DEV, -1).transpose(1, 0, 2), g["w_o"].reshape(N_DEV, -1, D)]
        elif stage == 1:
            arrs = [g["w_up0"], g["w_down0"].reshape(N_DEV, -1, D)]
        elif stage == 4:
            arrs = [g["w_out"].reshape(N_DEV, -1, D), g["pvec"].reshape(PV_ROWS, N_DEV, LRU_BW).transpose(1, 0, 2),
                    bf(g["wa"]).reshape(N_DEV, -1, 128), bf(g["wx"]).reshape(N_DEV, -1, 128)]
        elif stage == 0:
            arrs = [g["w_in"]]
        else:
            small = _pack_rows(g["g_mix"] + g["g_mlp"] + [g["g_fin"], g["conv_b"], g["qg"], g["kg"], g["loss"]], REP_SMALL_ROWS)
            arrs = [small.reshape(N_DEV, -1, 128)]
        (group,), token = _exchange_start([arrs], ["scatter"], "scatter_start%d" % (stage % 6))
        scatters[stage] = (group, token)
        return (token,)

    w = dict(g_mix=norm_mix_g, g_mlp=norm_mlp_g, g_fin=final_g[None], qg=at_q_g, kg=at_k_g,
             wa=bf(rg_w_a[0]), wx=bf(rg_w_x[0]))
    grad_x = _local_step(x, loss_target, w, fetch, send)

    res = {}
    r_up1, r_dn1 = _exchange_wait(scatters[3][0], (scatters[-1][1],), "scatter", "scatter_wait3")
    r_out, r_sp, r_wa, r_wx = _exchange_wait(scatters[4][0], (r_up1,), "scatter", "scatter_wait4")
    (rep_gather,), _ = _exchange_start([[_sum_parts(r_wa, "reduce_w_a"), _sum_parts(r_wx, "reduce_w_x")]], ["gather"],
                                       "rep_gather_start")
    up = _adamw_layer(r_up1, mlp_w_up, m_mlp_w_up, v_mlp_w_up, 1, None, "adam_mlp_w_up1")
    dn = _adamw_layer(r_dn1, mlp_w_down, m_mlp_w_down, v_mlp_w_down, 1, None, "adam_mlp_w_down1")
    r_qkv, r_o = _exchange_wait(scatters[2][0], (dn[0],), "scatter", "scatter_wait2")
    res["at_w_qkv"] = _adamw_nd(r_qkv[:, None], at_w_qkv, m_at_w_qkv, v_at_w_qkv, "adam_at_w_qkv")
    res["at_w_o"] = _adamw_nd(r_o[:, None], at_w_o, m_at_w_o, v_at_w_o, "adam_at_w_o")
    r_up0, r_dn0 = _exchange_wait(scatters[1][0], (res["at_w_o"][0],), "scatter", "scatter_wait1")
    res["mlp_w_up"] = _adamw_layer(r_up0, mlp_w_up, m_mlp_w_up, v_mlp_w_up, 0, up, "adam_mlp_w_up0")
    res["mlp_w_down"] = _adamw_layer(r_dn0, mlp_w_down, m_mlp_w_down, v_mlp_w_down, 0, dn, "adam_mlp_w_down0")
    r_in, = _exchange_wait(scatters[0][0], (res["mlp_w_down"][0],), "scatter", "scatter_wait0")
    res["rg_w_in"] = _adamw_nd(r_in[:, None], rg_w_in, m_rg_w_in, v_rg_w_in, "adam_rg_w_in")
    res["rg_w_out"] = _adamw_nd(r_out[:, None], rg_w_out, m_rg_w_out, v_rg_w_out, "adam_rg_w_out")
    sp_res = _adamw(r_sp, sp_w, _small_pack(m_rg_conv_w, m_rg_b_a, m_rg_b_x, m_rg_lam),
                    _small_pack(v_rg_conv_w, v_rg_b_a, v_rg_b_x, v_rg_lam), "adam_small")
    for o in sp_res:
        res.setdefault("rg_conv_w", []).append(o[PV_CONV_W:PV_CONV_W + CONV_W][None, :, None, :])
        res.setdefault("rg_b_a", []).append(o[PV_B_A:PV_B_A + 2][None])
        res.setdefault("rg_b_x", []).append(o[PV_B_X:PV_B_X + 2][None])
        res.setdefault("rg_lam", []).append(o[PV_LAM:PV_LAM + 2][None])

    r_small, = _exchange_wait(scatters[-1][0], (sp_res[0],), "scatter", "scatter_wait5")
    small_sum, = _all_gather([_sum_parts(r_small, "reduce_rep_small")], "gather_replicated")
    wa_sum, wx_sum = _exchange_wait(rep_gather, (small_sum,), "gather", "rep_gather_wait")
    rows = lambda a: a.reshape(-1, 128)
    res["rg_w_a"] = [o.reshape(rg_w_a.shape) for o in _adamw(
        wa_sum.reshape(1, -1, 128), rows(rg_w_a), rows(m_rg_w_a), rows(v_rg_w_a), "adam_rg_w_a")]
    res["rg_w_x"] = [o.reshape(rg_w_x.shape) for o in _adamw(
        wx_sum.reshape(1, -1, 128), rows(rg_w_x), rows(m_rg_w_x), rows(v_rg_w_x), "adam_rg_w_x")]
    nil = jnp.zeros((1, 128), F32)
    w_like = [norm_mix_g, norm_mlp_g, final_g, rg_conv_b, at_q_g, at_k_g, nil]
    m_like = [m_norm_mix_g, m_norm_mlp_g, m_final_g, m_rg_conv_b, m_at_q_g, m_at_k_g, nil]
    v_like = [v_norm_mix_g, v_norm_mlp_g, v_final_g, v_rg_conv_b, v_at_q_g, v_at_k_g, nil]
    names = ["norm_mix_g", "norm_mlp_g", "final_g", "rg_conv_b", "at_q_g", "at_k_g", "loss"]
    outs = _adamw(small_sum.reshape(1, -1, 128), _pack_rows(w_like, REP_SMALL_ROWS), _pack_rows(m_like, REP_SMALL_ROWS),
                  _pack_rows(v_like, REP_SMALL_ROWS), "adam_rep_small")
    for o in outs:
        for nm, val in zip(names, _unpack_rows(o, w_like)):
            res.setdefault(nm, []).append(val)
    loss = res["loss"][0][0, 0]

    order = ["norm_mix_g", "norm_mlp_g", "rg_w_in", "rg_conv_w", "rg_conv_b", "rg_w_a", "rg_b_a", "rg_w_x", "rg_b_x",
             "rg_lam", "rg_w_out", "at_w_qkv", "at_q_g", "at_k_g", "at_w_o", "mlp_w_up", "mlp_w_down", "final_g"]
    return (loss, grad_x, *[res[nm][k] for k in range(4) for nm in order])
```

```python
import functools
import math

import jax
import jax.numpy as jnp
from jax import lax
from jax.experimental import pallas as pl
from jax.experimental.pallas import tpu as pltpu

F32 = jnp.float32
BF16 = jnp.bfloat16
S = jax.ShapeDtypeStruct

EPS = 1e-6
HEAD_DIM = 128
N_KV = 2
GRID_W = 64
ROPE_THETA = 10000.0
LRU_BW = 128
RG_C = 8.0
CONV_W = 4
N_DEV = 8
N_SEG = 8
SCAN_UNROLL = 8
TN_STEP_COLS = 512
VMEM_LIMIT_V7X = 56 * 1024 * 1024
SOFTMAX_SCALE = 1.0 / math.sqrt(HEAD_DIM)
GELU_K = math.sqrt(2.0 / math.pi)
GELU_C = 0.044715

ADAM_LR = 0.001
ADAM_B1 = 0.9
ADAM_B2 = 0.999
ADAM_EPS = 1e-08
ADAM_WD = 0.01
ADAM_STEP = 10

NT = (((1,), (1,)), ((), ()))
TN = (((0,), (0,)), ((), ()))


def _cp(*sem):
    return pltpu.CompilerParams(dimension_semantics=sem, vmem_limit_bytes=VMEM_LIMIT_V7X)


def _rms_r(xv):
    return lax.rsqrt(jnp.mean(xv * xv, axis=-1, keepdims=True) + EPS)


def _rms_bwd(dh, xv, g):
    r = _rms_r(xv)
    xh = xv * r
    dg = jnp.sum(dh * xh, axis=0, keepdims=True)
    dxh = dh * g
    dx = r * (dxh - xh * jnp.mean(dxh * xh, axis=-1, keepdims=True))
    return dx, dg


def _dot(a, b):
    return jnp.dot(a, b, preferred_element_type=F32)


def _dot_nt(a, b):
    return lax.dot_general(a, b, NT, preferred_element_type=F32)


def _dot_tn(a, b):
    return lax.dot_general(a, b, TN, preferred_element_type=F32)


def _norm_matmul(x, g, wblk, name, out_dtype=F32):
    T, D = x.shape
    NB, _, nb = wblk.shape
    tm = min(T, 512)

    def body(x_ref, g_ref, w_ref, o_ref, h_ref):
        xv = x_ref[...]
        hb = (xv * _rms_r(xv) * g_ref[...]).astype(BF16)
        h_ref[...] = hb
        for q in range(NB):
            o_ref[:, q * nb:(q + 1) * nb] = _dot(hb, w_ref[q]).astype(o_ref.dtype)

    return pl.pallas_call(
        body, name=name, grid=(T // tm,),
        in_specs=[pl.BlockSpec((tm, D), lambda i: (i, 0)),
                  pl.BlockSpec((1, D), lambda i: (0, 0)),
                  pl.BlockSpec((NB, D, nb), lambda i: (0, 0, 0))],
        out_specs=[pl.BlockSpec((tm, NB * nb), lambda i: (i, 0)),
                   pl.BlockSpec((tm, D), lambda i: (i, 0))],
        out_shape=[S((T, NB * nb), out_dtype), S((T, D), BF16)],
        compiler_params=_cp("parallel"),
    )(x, g, wblk)


def _matmul_res(a, w, res, name):
    T, K = a.shape
    N = w.shape[1]
    tm = min(T, 512)

    def body(a_ref, w_ref, r_ref, o_ref):
        o_ref[...] = r_ref[...] + _dot(a_ref[...], w_ref[...])

    return pl.pallas_call(
        body, name=name, grid=(T // tm,),
        in_specs=[pl.BlockSpec((tm, K), lambda i: (i, 0)),
                  pl.BlockSpec((K, N), lambda i: (0, 0)),
                  pl.BlockSpec((tm, N), lambda i: (i, 0))],
        out_specs=pl.BlockSpec((tm, N), lambda i: (i, 0)),
        out_shape=S((T, N), F32),
        compiler_params=_cp("parallel"),
    )(a, w, res)


def _matmul_nt(a, w, name, out_dtype, after=()):
    T, N = a.shape
    K = w.shape[0]
    tm = min(T, 512)

    def body(a_ref, w_ref, *rest):
        o_ref, ab_ref = rest[len(after):]
        ab = a_ref[...].astype(BF16)
        ab_ref[...] = ab
        o_ref[...] = _dot_nt(ab, w_ref[...]).astype(o_ref.dtype)

    return pl.pallas_call(
        body, name=name, grid=(T // tm,),
        in_specs=[pl.BlockSpec((tm, N), lambda i: (i, 0)),
                  pl.BlockSpec((K, N), lambda i: (0, 0))] + [pl.BlockSpec(memory_space=pl.ANY)] * len(after),
        out_specs=[pl.BlockSpec((tm, K), lambda i: (i, 0)),
                   pl.BlockSpec((tm, N), lambda i: (i, 0))],
        out_shape=[S((T, K), out_dtype), S((T, N), BF16)],
        compiler_params=_cp("parallel"),
    )(a, w, *after)


def _matmul_tn(a, b3, nb, name, blocked, after=()):
    T, M = a.shape
    SB, _, N = b3.shape
    per = N // nb
    NB = SB * per
    tk = min(T, 1024)
    nk = T // tk
    jb = max(1, TN_STEP_COLS // nb) if blocked else 1
    assert per % jb == 0
    if blocked:
        out_spec, out_shape = pl.BlockSpec((jb, M, nb), lambda j, k: (j, 0, 0)), S((NB, M, nb), BF16)
    else:
        assert SB == 1
        out_spec, out_shape = pl.BlockSpec((M, nb), lambda j, k: (0, j)), S((M, N), BF16)

    def body(a_ref, b_ref, *rest):
        o_ref, acc_ref = rest[len(after):]
        k = pl.program_id(1)

        @pl.when(k == 0)
        def _():
            acc_ref[...] = jnp.zeros_like(acc_ref)

        av = a_ref[...]
        for q in range(jb):
            acc_ref[q] += _dot_tn(av, b_ref[:, q * nb:(q + 1) * nb])

        @pl.when(k == nk - 1)
        def _():
            if blocked:
                o_ref[...] = acc_ref[...].astype(BF16)
            else:
                o_ref[...] = acc_ref[0].astype(BF16)

    return pl.pallas_call(
        body, name=name, grid=(NB // jb, nk),
        in_specs=[pl.BlockSpec((tk, M), lambda j, k: (k, 0)),
                  pl.BlockSpec((None, tk, jb * nb), lambda j, k: ((j * jb) // per, k, ((j * jb) % per) // jb))]
        + [pl.BlockSpec(memory_space=pl.ANY)] * len(after),
        out_specs=out_spec,
        out_shape=out_shape,
        scratch_shapes=[pltpu.VMEM((jb, M, nb), F32)],
        compiler_params=_cp("parallel", "arbitrary"),
    )(a, b3, *after)


def _nt_normbwd(dz3, wblk, x, g, dres, name, after=()):
    T, D = x.shape
    NB, _, nb = wblk.shape
    SB, _, N = dz3.shape
    per = N // nb
    tm = min(T, 512)

    def body(dz_ref, w_ref, x_ref, g_ref, dr_ref, *rest):
        dx_ref, dg_ref = rest[len(after):]

        @pl.when(pl.program_id(0) == 0)
        def _():
            dg_ref[...] = jnp.zeros_like(dg_ref)

        dh = None
        for q in range(NB):
            cols = slice((q % per) * nb, (q % per + 1) * nb)
            part = _dot_nt(dz_ref[q // per, :, cols], w_ref[q])
            dh = part if dh is None else dh + part
        dx, dg = _rms_bwd(dh, x_ref[...], g_ref[...])
        dx_ref[...] = dr_ref[...] + dx
        dg_ref[...] += dg

    return pl.pallas_call(
        body, name=name, grid=(T // tm,),
        in_specs=[pl.BlockSpec((SB, tm, N), lambda i: (0, i, 0)),
                  pl.BlockSpec((NB, D, nb), lambda i: (0, 0, 0)),
                  pl.BlockSpec((tm, D), lambda i: (i, 0)),
                  pl.BlockSpec((1, D), lambda i: (0, 0)),
                  pl.BlockSpec((tm, D), lambda i: (i, 0))] + [pl.BlockSpec(memory_space=pl.ANY)] * len(after),
        out_specs=[pl.BlockSpec((tm, D), lambda i: (i, 0)),
                   pl.BlockSpec((1, D), lambda i: (0, 0))],
        out_shape=[S((T, D), F32), S((1, D), F32)],
        compiler_params=_cp("arbitrary"),
    )(dz3, wblk, x, g, dres, *after)


def _loss_head(xv, tv, gv, D):
    err = xv * _rms_r(xv) * gv - tv
    e2 = jnp.sum(jnp.sum(err * err, axis=-1, keepdims=True), axis=0, keepdims=True)
    dx, dg = _rms_bwd(err * (1.0 / D), xv, gv)
    return (0.5 / D) * e2, dx, dg


def _mlp_fwd(x, g, wup, wdown, name, head=None):
    T, D = x.shape
    NB, _, fb = wup.shape
    tm = min(T, 1024)
    n_head = 0 if head is None else 2

    def body(x_ref, g_ref, wu_ref, wd_ref, *rest):
        xo_ref, a_ref, h_ref = rest[n_head:n_head + 3]
        acc_ref = rest[-1]
        i, j = pl.program_id(0), pl.program_id(1)

        @pl.when(j == 0)
        def _():
            xv = x_ref[...]
            h_ref[...] = (xv * _rms_r(xv) * g_ref[...]).astype(BF16)
            acc_ref[...] = xv

        a = _dot(h_ref[...], wu_ref[...])
        a_ref[...] = a.astype(BF16)
        u = jnp.maximum(a, 0.0)
        acc_ref[...] += _dot((u * u).astype(BF16), wd_ref[...])

        if head is None:
            @pl.when(j == NB - 1)
            def _():
                xo_ref[...] = acc_ref[...]
        else:
            t_ref, gf_ref = rest[:2]
            loss_ref, dgf_ref = rest[n_head + 3:n_head + 5]

            @pl.when((i == 0) & (j == 0))
            def _():
                loss_ref[...] = jnp.zeros_like(loss_ref)
                dgf_ref[...] = jnp.zeros_like(dgf_ref)

            @pl.when(j == NB - 1)
            def _():
                e2, dx, dg = _loss_head(acc_ref[...], t_ref[...], gf_ref[...], D)
                xo_ref[...] = dx
                loss_ref[...] += e2
                dgf_ref[...] += dg

    row = pl.BlockSpec((tm, D), lambda i, j: (i, 0))
    vec = pl.BlockSpec((1, D), lambda i, j: (0, 0))
    in_specs = [row, vec, pl.BlockSpec((None, D, fb), lambda i, j: (j, 0, 0)), pl.BlockSpec((fb, D), lambda i, j: (j, 0))]
    out_specs = [row, pl.BlockSpec((tm, fb), lambda i, j: (i, j)), row]
    out_shape = [S((T, D), F32), S((T, NB * fb), BF16), S((T, D), BF16)]
    if head is not None:
        in_specs += [row, vec]
        out_specs += [pl.BlockSpec((1, 128), lambda i, j: (0, 0)), vec]
        out_shape += [S((1, 128), F32), S((1, D), F32)]
    return pl.pallas_call(
        body, name=name, grid=(T // tm, NB),
        in_specs=in_specs, out_specs=out_specs, out_shape=out_shape,
        scratch_shapes=[pltpu.VMEM((tm, D), F32)],
        compiler_params=_cp("parallel" if head is None else "arbitrary", "arbitrary"),
    )(x, g, wup, wdown, *(head or ()))


def _mlp_bwd_dx(x, dout, a, g, wup, wdown, name):
    T, D = x.shape
    NB, _, fb = wup.shape
    tm = min(T, 1024)

    def body(x_ref, do_ref, a_ref, g_ref, wu_ref, wd_ref, dx_ref, da_ref, dob_ref, dg_ref, acc_ref):
        i, j = pl.program_id(0), pl.program_id(1)

        @pl.when(j == 0)
        def _():
            dob_ref[...] = do_ref[...].astype(BF16)
            acc_ref[...] = jnp.zeros_like(acc_ref)

        @pl.when((i == 0) & (j == 0))
        def _():
            dg_ref[...] = jnp.zeros_like(dg_ref)

        du2 = _dot_nt(dob_ref[...], wd_ref[...])
        u = jnp.maximum(a_ref[...].astype(F32), 0.0)
        da = (du2 * (2.0 * u)).astype(BF16)
        da_ref[...] = da
        acc_ref[...] += _dot_nt(da, wu_ref[...])

        @pl.when(j == NB - 1)
        def _():
            dx, dg = _rms_bwd(acc_ref[...], x_ref[...], g_ref[...])
            dx_ref[...] = do_ref[...] + dx
            dg_ref[...] += dg

    return pl.pallas_call(
        body, name=name, grid=(T // tm, NB),
        in_specs=[pl.BlockSpec((tm, D), lambda i, j: (i, 0)),
                  pl.BlockSpec((tm, D), lambda i, j: (i, 0)),
                  pl.BlockSpec((tm, fb), lambda i, j: (i, j)),
                  pl.BlockSpec((1, D), lambda i, j: (0, 0)),
                  pl.BlockSpec((None, D, fb), lambda i, j: (j, 0, 0)),
                  pl.BlockSpec((fb, D), lambda i, j: (j, 0))],
        out_specs=[pl.BlockSpec((tm, D), lambda i, j: (i, 0)),
                   pl.BlockSpec((tm, fb), lambda i, j: (i, j)),
                   pl.BlockSpec((tm, D), lambda i, j: (i, 0)),
                   pl.BlockSpec((1, D), lambda i, j: (0, 0))],
        out_shape=[S((T, D), F32), S((T, NB * fb), BF16), S((T, D), BF16), S((1, D), F32)],
        scratch_shapes=[pltpu.VMEM((tm, D), F32)],
        compiler_params=_cp("arbitrary", "arbitrary"),
    )(x, dout, a, g, wup, wdown)


def _mlp_bwd_dw(h, da, a, dob, fb, name):
    T, D = h.shape
    F = a.shape[1]
    NB = F // fb
    tk = min(T, 1024)
    nk = T // tk

    def body(h_ref, da_ref, a_ref, dob_ref, dwu_ref, dwd_ref, au_ref, ad_ref):
        k = pl.program_id(1)

        @pl.when(k == 0)
        def _():
            au_ref[...] = jnp.zeros_like(au_ref)
            ad_ref[...] = jnp.zeros_like(ad_ref)

        au_ref[...] += _dot_tn(h_ref[...], da_ref[...])
        u = jnp.maximum(a_ref[...].astype(F32), 0.0)
        ad_ref[...] += _dot_tn((u * u).astype(BF16), dob_ref[...])

        @pl.when(k == nk - 1)
        def _():
            dwu_ref[...] = au_ref[...].astype(BF16)
            dwd_ref[...] = ad_ref[...].astype(BF16)

    return pl.pallas_call(
        body, name=name, grid=(NB, nk),
        in_specs=[pl.BlockSpec((tk, D), lambda j, k: (k, 0)),
                  pl.BlockSpec((tk, fb), lambda j, k: (k, j)),
                  pl.BlockSpec((tk, fb), lambda j, k: (k, j)),
                  pl.BlockSpec((tk, D), lambda j, k: (k, 0))],
        out_specs=[pl.BlockSpec((None, D, fb), lambda j, k: (j, 0, 0)),
                   pl.BlockSpec((fb, D), lambda j, k: (j, 0))],
        out_shape=[S((NB, D, fb), BF16), S((F, D), BF16)],
        scratch_shapes=[pltpu.VMEM((D, fb), F32), pltpu.VMEM((fb, D), F32)],
        compiler_params=_cp("parallel", "arbitrary"),
    )(h, da, a, dob)


def _rope_tables(L):
    nf = HEAD_DIM // 4
    t = jnp.arange(L, dtype=jnp.int32)
    row = (t // GRID_W).astype(F32)
    col = (t % GRID_W).astype(F32)
    inv = ROPE_THETA ** (-jnp.arange(nf, dtype=F32) / nf)
    ar = row[:, None] * inv
    ac = col[:, None] * inv
    cos = jnp.concatenate([jnp.cos(ar), jnp.cos(ar), jnp.cos(ac), jnp.cos(ac)], axis=-1)
    sin = jnp.concatenate([-jnp.sin(ar), jnp.sin(ar), -jnp.sin(ac), jnp.sin(ac)], axis=-1)
    return cos, sin


def _swap32(x):
    lane = lax.broadcasted_iota(jnp.int32, x.shape, 1)
    up = pltpu.roll(x, HEAD_DIM - 32, 1)
    down = pltpu.roll(x, 32, 1)
    return jnp.where((lane % 64) < 32, up, down)


def _qk_prep(qkv, qg, kg, cos, sin, L, name):
    T, W = qkv.shape
    nh = W // HEAD_DIM - 2 * N_KV
    tm = min(L, 512)
    lb = L // tm

    def body(qkv_ref, qg_ref, kg_ref, cos_ref, sin_ref, q_ref, k_ref, v_ref):
        c = cos_ref[...]
        s = sin_ref[...]
        for h in range(nh + N_KV):
            xh = qkv_ref[:, h * HEAD_DIM:(h + 1) * HEAD_DIM]
            gv = qg_ref[...] if h < nh else kg_ref[...]
            y = xh * _rms_r(xh) * gv
            y = (y * c + _swap32(y) * s).astype(BF16)
            if h < nh:
                q_ref[:, h * HEAD_DIM:(h + 1) * HEAD_DIM] = y
            else:
                k_ref[:, (h - nh) * HEAD_DIM:(h - nh + 1) * HEAD_DIM] = y
        v_ref[...] = qkv_ref[:, (nh + N_KV) * HEAD_DIM:].astype(BF16)

    return pl.pallas_call(
        body, name=name, grid=(T // tm,),
        in_specs=[pl.BlockSpec((tm, W), lambda i: (i, 0)),
                  pl.BlockSpec((1, HEAD_DIM), lambda i: (0, 0)),
                  pl.BlockSpec((1, HEAD_DIM), lambda i: (0, 0)),
                  pl.BlockSpec((tm, HEAD_DIM), lambda i: (i % lb, 0)),
                  pl.BlockSpec((tm, HEAD_DIM), lambda i: (i % lb, 0))],
        out_specs=[pl.BlockSpec((tm, nh * HEAD_DIM), lambda i: (i, 0)),
                   pl.BlockSpec((tm, N_KV * HEAD_DIM), lambda i: (i, 0)),
                   pl.BlockSpec((tm, N_KV * HEAD_DIM), lambda i: (i, 0))],
        out_shape=[S((T, nh * HEAD_DIM), BF16), S((T, N_KV * HEAD_DIM), BF16), S((T, N_KV * HEAD_DIM), BF16)],
        compiler_params=_cp("parallel"),
    )(qkv, qg, kg, cos, sin)


def _qk_prep_bwd(qkv, dq, dk, dv, qg, kg, cos, sin, L, name):
    T, W = qkv.shape
    nh = W // HEAD_DIM - 2 * N_KV
    tm = min(L, 512)
    lb = L // tm

    def body(qkv_ref, dq_ref, dk_ref, dv_ref, qg_ref, kg_ref, cos_ref, sin_ref, dz_ref, dqg_ref, dkg_ref):
        @pl.when(pl.program_id(0) == 0)
        def _():
            dqg_ref[...] = jnp.zeros_like(dqg_ref)
            dkg_ref[...] = jnp.zeros_like(dkg_ref)

        c = cos_ref[...]
        s = sin_ref[...]
        for h in range(nh + N_KV):
            cols = slice(h * HEAD_DIM, (h + 1) * HEAD_DIM)
            if h < nh:
                dout, gv, dg_ref = dq_ref[:, cols], qg_ref[...], dqg_ref
            else:
                kc = slice((h - nh) * HEAD_DIM, (h - nh + 1) * HEAD_DIM)
                dout, gv, dg_ref = dk_ref[:, kc], kg_ref[...], dkg_ref
            dy = dout * c - _swap32(dout) * s
            dx, dg = _rms_bwd(dy, qkv_ref[:, cols], gv)
            dg_ref[...] += dg
            dz_ref[:, cols] = dx.astype(BF16)
        dz_ref[:, (nh + N_KV) * HEAD_DIM:] = dv_ref[...].astype(BF16)

    return pl.pallas_call(
        body, name=name, grid=(T // tm,),
        in_specs=[pl.BlockSpec((tm, W), lambda i: (i, 0)),
                  pl.BlockSpec((tm, nh * HEAD_DIM), lambda i: (i, 0)),
                  pl.BlockSpec((tm, N_KV * HEAD_DIM), lambda i: (i, 0)),
                  pl.BlockSpec((tm, N_KV * HEAD_DIM), lambda i: (i, 0)),
                  pl.BlockSpec((1, HEAD_DIM), lambda i: (0, 0)),
                  pl.BlockSpec((1, HEAD_DIM), lambda i: (0, 0)),
                  pl.BlockSpec((tm, HEAD_DIM), lambda i: (i % lb, 0)),
                  pl.BlockSpec((tm, HEAD_DIM), lambda i: (i % lb, 0))],
        out_specs=[pl.BlockSpec((tm, W), lambda i: (i, 0)),
                   pl.BlockSpec((1, HEAD_DIM), lambda i: (0, 0)),
                   pl.BlockSpec((1, HEAD_DIM), lambda i: (0, 0))],
        out_shape=[S((T, W), BF16), S((1, HEAD_DIM), F32), S((1, HEAD_DIM), F32)],
        compiler_params=_cp("arbitrary"),
    )(qkv, dq, dk, dv, qg, kg, cos, sin)


EXP2_SCALE = SOFTMAX_SCALE * math.log2(math.e)
ATTN_SUB = 128
ATTN_TQ = 1024


def _softmax_rows(q, k):
    s = _dot_nt(q, k)
    e = jnp.exp2((s - jnp.max(s, axis=-1, keepdims=True)) * EXP2_SCALE)
    return e, jnp.sum(e, axis=-1, keepdims=True)


def _attn_fwd(q, k, v, L, name):
    T = q.shape[0]
    nh = q.shape[1] // HEAD_DIM
    G = nh // N_KV
    B = T // L
    tq = min(L, ATTN_TQ)
    nq = L // tq
    sub = min(tq, ATTN_SUB)

    def body(q_ref, k_ref, v_ref, o_ref):
        for h in range(tq // sub):
            rows = slice(h * sub, (h + 1) * sub)
            e, l = _softmax_rows(q_ref[rows, :], k_ref[...])
            o_ref[rows, :] = (_dot(e.astype(BF16), v_ref[...]) / l).astype(BF16)

    qspec = pl.BlockSpec((tq, HEAD_DIM), lambda b, kv, g, qi: (b * nq + qi, kv * G + g))
    kspec = pl.BlockSpec((L, HEAD_DIM), lambda b, kv, g, qi: (b, kv))
    return pl.pallas_call(
        body, name=name, grid=(B, N_KV, G, nq),
        in_specs=[qspec, kspec, kspec],
        out_specs=qspec,
        out_shape=S((T, nh * HEAD_DIM), BF16),
        compiler_params=_cp("parallel", "parallel", "parallel", "parallel"),
    )(q, k, v)


def _attn_bwd(q, k, v, do, o, L, name):
    T = q.shape[0]
    nh = q.shape[1] // HEAD_DIM
    G = nh // N_KV
    B = T // L
    tq = min(L, ATTN_TQ)
    nq = L // tq

    sub = min(tq, ATTN_SUB)

    def body(q_ref, k_ref, v_ref, do_ref, o_ref, dq_ref, dk_ref, dv_ref, ds_scr, p_scr):
        first = (pl.program_id(2) == 0) & (pl.program_id(3) == 0)
        last = (pl.program_id(2) == G - 1) & (pl.program_id(3) == nq - 1)

        @pl.when(first)
        def _():
            dk_ref[...] = jnp.zeros_like(dk_ref)
            dv_ref[...] = jnp.zeros_like(dv_ref)

        for h in range(tq // sub):
            rows = slice(h * sub, (h + 1) * sub)
            dov = do_ref[rows, :]
            e, l = _softmax_rows(q_ref[rows, :], k_ref[...])
            p = e * (1.0 / l)
            dsum = jnp.sum(dov.astype(F32) * o_ref[rows, :].astype(F32), axis=-1, keepdims=True)
            ds_scr[rows, :] = (p * (_dot_nt(dov, v_ref[...]) - dsum)).astype(BF16)
            p_scr[rows, :] = p.astype(BF16)
        ds = ds_scr[...]
        dq_ref[...] = _dot(ds, k_ref[...]) * SOFTMAX_SCALE
        dk_ref[...] += _dot_tn(ds, q_ref[...])
        dv_ref[...] += _dot_tn(p_scr[...], do_ref[...])

        @pl.when(last)
        def _():
            dk_ref[...] = dk_ref[...] * SOFTMAX_SCALE

    qspec = pl.BlockSpec((tq, HEAD_DIM), lambda b, kv, g, qi: (b * nq + qi, kv * G + g))
    kspec = pl.BlockSpec((L, HEAD_DIM), lambda b, kv, g, qi: (b, kv))
    return pl.pallas_call(
        body, name=name, grid=(B, N_KV, G, nq),
        in_specs=[qspec, kspec, kspec, qspec, qspec],
        out_specs=[qspec, kspec, kspec],
        out_shape=[S((T, nh * HEAD_DIM), F32), S((T, N_KV * HEAD_DIM), F32), S((T, N_KV * HEAD_DIM), F32)],
        scratch_shapes=[pltpu.VMEM((tq, L), BF16), pltpu.VMEM((tq, L), BF16)],
        compiler_params=_cp("parallel", "parallel", "arbitrary", "arbitrary"),
    )(q, k, v, do, o)


PV_CONV_W = 0
PV_B_A = 8
PV_B_X = 16
PV_LAM = 24
PV_CONV_B = 32
PV_ROWS = 40


def _shift_rows(x, k):
    if k == 0:
        return x
    L = x.shape[0]
    n = N_SEG * abs(k)
    seg = lax.broadcasted_iota(jnp.int32, (n, x.shape[1]), 0) % N_SEG
    if k > 0:
        edge = jnp.where(seg == 0, 0.0, pltpu.roll(x[L - n:], 1, 0))
        return jnp.concatenate([edge, x[:L - n]], axis=0)
    edge = jnp.where(seg == N_SEG - 1, 0.0, pltpu.roll(x[:n], n - 1, 0))
    return jnp.concatenate([x[n:], edge], axis=0)


def _conv_taps(rec, pv):
    c = pv[PV_CONV_B:PV_CONV_B + 1]
    for j in range(CONV_W):
        c = c + pv[PV_CONV_W + j:PV_CONV_W + j + 1] * _shift_rows(rec, 2 - j)
    return c


def _sigmoid(x):
    return 0.5 * jnp.tanh(0.5 * x) + 0.5


EXPM1_SERIES_BELOW = 0.03


def _rg_gates(c, cbf, wa, wx, ba, bx, lam):
    r = _sigmoid(_dot(cbf, wa) + ba)
    i = _sigmoid(_dot(cbf, wx) + bx)
    sp = jnp.maximum(-lam, 0.0) + jnp.log1p(jnp.exp(-jnp.abs(lam)))
    la = r * ((-RG_C) * sp)
    a = jnp.exp(la)
    a2 = a * a
    x = la + la
    series = -(x * ((x * (1.0 / 6.0) + 0.5) * x + 1.0))
    om = jnp.where(x > -EXPM1_SERIES_BELOW, series, 1.0 - a2)
    rm = lax.rsqrt(om)
    return r, i, a, om * rm, rm, a2, sp


def _gelu(x):
    t = jnp.tanh(GELU_K * (x + GELU_C * x * x * x))
    return 0.5 * x * (1.0 + t), t


def _scan_pair(af_ref, uf_ref, ab_ref, ub_ref, hf_ref, hb_ref, pf_ref, pb_ref, L):
    ls = L // N_SEG
    zero = jnp.zeros((N_SEG, LRU_BW), F32)
    one = jnp.ones((N_SEG, LRU_BW), F32)
    tile = lambda t: pl.ds(pl.multiple_of(t * N_SEG, N_SEG), N_SEG)

    def steps(tc, carry):
        hf, pf, hb, pb = carry
        for q in range(SCAN_UNROLL):
            t = tc * SCAN_UNROLL + q
            rf, rb = tile(t), tile(ls - 1 - t)
            af = af_ref[rf, :]
            hf = af * hf + uf_ref[rf, :]
            pf = pf * af
            hf_ref[rf, :] = hf
            pf_ref[rf, :] = pf
            ab = ab_ref[rb, :]
            hb = ab * hb + ub_ref[rb, :]
            pb = pb * ab
            hb_ref[rb, :] = hb
            pb_ref[rb, :] = pb
        return hf, pf, hb, pb

    hf_e, pf_e, hb_e, pb_e = lax.fori_loop(0, ls // SCAN_UNROLL, steps, (zero, one, zero, one))

    rows, cin = [], jnp.zeros((1, LRU_BW), F32)
    for s in range(N_SEG):
        rows.append(cin)
        cin = hf_e[s:s + 1] + pf_e[s:s + 1] * cin
    cf = jnp.concatenate(rows, axis=0)
    rows, cin = [], jnp.zeros((1, LRU_BW), F32)
    for s in reversed(range(N_SEG)):
        rows.append(cin)
        cin = hb_e[s:s + 1] + pb_e[s:s + 1] * cin
    cb = jnp.concatenate(rows[::-1], axis=0)

    def fix(tc, _):
        for q in range(SCAN_UNROLL):
            r = tile(tc * SCAN_UNROLL + q)
            hf_ref[r, :] = hf_ref[r, :] + pf_ref[r, :] * cf
            hb_ref[r, :] = hb_ref[r, :] + pb_ref[r, :] * cb
        return 0

    lax.fori_loop(0, ls // SCAN_UNROLL, fix, 0)


def _rg_specs(L, D, nblk):
    slab = lambda off: pl.BlockSpec((L, LRU_BW), lambda cb, b: (b, off + cb))
    wspec = pl.BlockSpec((2, None, LRU_BW, LRU_BW), lambda cb, b: (0, cb, 0, 0))
    pvspec = pl.BlockSpec((PV_ROWS, LRU_BW), lambda cb, b: (0, cb))
    return slab, wspec, pvspec


def _rg_fwd(z, pvec, wa, wx, L, name):
    T, C2 = z.shape
    C = C2 // 2
    nblk = C // LRU_BW
    B = T // L
    slab, wspec, pvspec = _rg_specs(L, C, nblk)

    def body(gp_ref, rec_ref, pv_ref, wa_ref, wx_ref, yg_ref, hf_ref, hb_ref, a_scr, u_scr, p_scr):
        pv = pv_ref[...]
        c = _conv_taps(rec_ref[...], pv)
        cbf = c.astype(BF16)
        for d in range(2):
            _, i, a, m, _, _, _ = _rg_gates(c, cbf, wa_ref[d], wx_ref[d], pv[PV_B_A + d:PV_B_A + d + 1],
                                      pv[PV_B_X + d:PV_B_X + d + 1], pv[PV_LAM + d:PV_LAM + d + 1])
            a_scr[d] = a
            u_scr[d] = m * (i * c)
        _scan_pair(a_scr.at[0], u_scr.at[0], a_scr.at[1], u_scr.at[1], hf_ref, hb_ref, p_scr.at[0], p_scr.at[1], L)
        gate, _ = _gelu(gp_ref[...])
        yg_ref[...] = ((hf_ref[...] + hb_ref[...]) * gate).astype(BF16)

    return pl.pallas_call(
        body, name=name, grid=(nblk, B),
        in_specs=[slab(0), slab(nblk), pvspec, wspec, wspec],
        out_specs=[slab(0), slab(0), slab(0)],
        out_shape=[S((T, C), BF16), S((T, C), F32), S((T, C), F32)],
        scratch_shapes=[pltpu.VMEM((2, L, LRU_BW), F32)] * 3,
        compiler_params=_cp("parallel", "parallel"),
    )(z, z, pvec, wa, wx)


def _rg_bwd(z, hf, hb, dyg, pvec, wa, wx, L, name):
    T, C2 = z.shape
    C = C2 // 2
    nblk = C // LRU_BW
    B = T // L
    slab, wspec, pvspec = _rg_specs(L, C, nblk)

    def body(gp_ref, rec_ref, hf_ref, hb_ref, dyg_ref, pv_ref, wa_ref, wx_ref,
             dz_ref, dwa_ref, dwx_ref, dpv_ref, a_scr, u_scr, d_scr, p_scr):
        @pl.when(pl.program_id(1) == 0)
        def _():
            dwa_ref[...] = jnp.zeros_like(dwa_ref)
            dwx_ref[...] = jnp.zeros_like(dwx_ref)
            dpv_ref[...] = jnp.zeros_like(dpv_ref)

        pv = pv_ref[...]
        rec = rec_ref[...]
        c = _conv_taps(rec, pv)
        cbf = c.astype(BF16)
        gp = gp_ref[...]
        gate, th = _gelu(gp)
        dgelu = 0.5 * (1.0 + th) + 0.5 * gp * (1.0 - th * th) * GELU_K * (1.0 + 3.0 * GELU_C * gp * gp)
        dyg = dyg_ref[...]
        dz_ref[0] = (dyg * (hf_ref[...] + hb_ref[...]) * dgelu).astype(BF16)
        dy = dyg * gate

        gates = []
        for d in range(2):
            gates.append(_rg_gates(c, cbf, wa_ref[d], wx_ref[d], pv[PV_B_A + d:PV_B_A + d + 1],
                                   pv[PV_B_X + d:PV_B_X + d + 1], pv[PV_LAM + d:PV_LAM + d + 1]))
        a_scr[0] = _shift_rows(gates[1][2], 1)
        a_scr[1] = _shift_rows(gates[0][2], -1)
        u_scr[...] = dy
        _scan_pair(a_scr.at[0], u_scr, a_scr.at[1], u_scr, d_scr.at[1], d_scr.at[0], p_scr.at[0], p_scr.at[1], L)

        dc = jnp.zeros_like(c)
        rows = []
        for d in range(2):
            r, i, a, m, rm, a2, sp = gates[d]
            delta = d_scr[d]
            hnb = _shift_rows(hf_ref[...], 1) if d == 0 else _shift_rows(hb_ref[...], -1)
            da = delta * hnb
            dm = delta * (i * c)
            di = delta * (m * c)
            dc = dc + delta * (m * i)
            dla = da * a - dm * (a2 * rm)
            dpa = (dla * ((-RG_C) * sp)) * (r * (1.0 - r))
            dpx = di * (i * (1.0 - i))
            dsp = (-RG_C) * jnp.sum(dla * r, axis=0, keepdims=True)
            lam = pv[PV_LAM + d:PV_LAM + d + 1]
            rows.append((jnp.sum(dpa, axis=0, keepdims=True), jnp.sum(dpx, axis=0, keepdims=True),
                         -dsp * _sigmoid(-lam)))
            dpab = dpa.astype(BF16)
            dpxb = dpx.astype(BF16)
            dwa_ref[d] += _dot_tn(cbf, dpab)
            dwx_ref[d] += _dot_tn(cbf, dpxb)
            dc = dc + _dot_nt(dpab, wa_ref[d]) + _dot_nt(dpxb, wx_ref[d])

        drec = jnp.zeros_like(c)
        dcw = []
        for j in range(CONV_W):
            drec = drec + pv[PV_CONV_W + j:PV_CONV_W + j + 1] * _shift_rows(dc, j - 2)
            dcw.append(jnp.sum(dc * _shift_rows(rec, 2 - j), axis=0, keepdims=True))
        dz_ref[1] = drec.astype(BF16)
        for j in range(CONV_W):
            dpv_ref[PV_CONV_W + j:PV_CONV_W + j + 1, :] += dcw[j]
        for d in range(2):
            dpv_ref[PV_B_A + d:PV_B_A + d + 1, :] += rows[d][0]
            dpv_ref[PV_B_X + d:PV_B_X + d + 1, :] += rows[d][1]
            dpv_ref[PV_LAM + d:PV_LAM + d + 1, :] += rows[d][2]
        dpv_ref[PV_CONV_B:PV_CONV_B + 1, :] += jnp.sum(dc, axis=0, keepdims=True)

    return pl.pallas_call(
        body, name=name, grid=(nblk, B),
        in_specs=[slab(0), slab(nblk), slab(0), slab(0), slab(0), pvspec, wspec, wspec],
        out_specs=[pl.BlockSpec((2, L, LRU_BW), lambda cb, b: (0, b, cb)), wspec, wspec, pvspec],
        out_shape=[S((2, T, C), BF16), S((2, nblk, LRU_BW, LRU_BW), F32), S((2, nblk, LRU_BW, LRU_BW), F32),
                   S((PV_ROWS, C), F32)],
        scratch_shapes=[pltpu.VMEM((2, L, LRU_BW), F32), pltpu.VMEM((L, LRU_BW), F32),
                        pltpu.VMEM((2, L, LRU_BW), F32), pltpu.VMEM((2, L, LRU_BW), F32)],
        compiler_params=_cp("parallel", "arbitrary"),
    )(z, z, hf, hb, dyg, pvec, wa, wx)


QKV_NB = 512


def _interleave(a):
    *lead, L, D = a.shape
    return a.reshape(*lead, N_SEG, L // N_SEG, D).swapaxes(-3, -2).reshape(*lead, L, D)


def _deinterleave(a):
    *lead, L, D = a.shape
    return a.reshape(*lead, L // N_SEG, N_SEG, D).swapaxes(-3, -2).reshape(*lead, L, D)


def _local_step(x3, tgt3, w, fetch, send):
    Bl, L, D = x3.shape
    T = Bl * L
    x = _interleave(x3).reshape(T, D)
    tgt = _interleave(tgt3).reshape(T, D)
    gm = [w["g_mix"][i:i + 1] for i in range(2)]
    gl = [w["g_mlp"][i:i + 1] for i in range(2)]

    w0 = fetch(0, ())
    nb_in = w0["w_in"].shape[-1]
    z, h0 = _norm_matmul(x, gm[0], w0["w_in"], "rg_in")
    yg, hf, hb = _rg_fwd(z, w0["pvec"], w["wa"], w["wx"], L, "rg_fwd")
    w1 = fetch(1, (yg,))
    x1 = _matmul_res(yg, w1["w_out"], x, "rg_out")
    w1.update(fetch(4, (x1,)))
    fb = w1["w_up0"].shape[-1]
    x2, a0, hm0 = _mlp_fwd(x1, gl[0], w1["w_up0"], w1["w_down0"], "mlp0_fwd")
    w2 = fetch(2, (x2,))
    qkv, h1 = _norm_matmul(x2, gm[1], w2["w_qkv"], "at_qkv")
    cos, sin = [_interleave(t) for t in _rope_tables(L)]
    qn, kn, vb = _qk_prep(qkv, w["qg"], w["kg"], cos, sin, L, "at_prep")
    o = _attn_fwd(qn, kn, vb, L, "at_fwd")
    x3_ = _matmul_res(o, w2["w_o"], x2, "at_out")
    w3 = fetch(3, (x3_,))
    dx4, a1, hm1, loss, dgf = _mlp_fwd(x3_, gl[1], w3["w_up1"], w3["w_down1"], "mlp1_fwd", head=(tgt, w["g_fin"]))

    dx3, da1, dob1, dgl1 = _mlp_bwd_dx(x3_, dx4, a1, gl[1], w3["w_up1"], w3["w_down1"], "mlp1_bwd_dx")
    dwu1, dwd1 = _mlp_bwd_dw(hm1, da1, a1, dob1, fb, "mlp1_bwd_dw")
    sent = send(3, dict(w_up1=dwu1, w_down1=dwd1))
    do, dx3b = _matmul_nt(dx3, w2["w_o"], "at_out_bwd", BF16, after=sent)
    dwo = _matmul_tn(o, dx3b[None], QKV_NB, "at_out_dw", blocked=False)
    dq, dk, dv = _attn_bwd(qn, kn, vb, do, o, L, "at_bwd")
    dqkv, dqg, dkg = _qk_prep_bwd(qkv, dq, dk, dv, w["qg"], w["kg"], cos, sin, L, "at_prep_bwd")
    dwqkv = _matmul_tn(h1, dqkv[None], QKV_NB, "at_qkv_dw", blocked=False)
    sent = send(2, dict(w_qkv=dwqkv, w_o=dwo))
    dx2, dgm1 = _nt_normbwd(dqkv[None], w2["w_qkv"], x2, gm[1], dx3, "at_qkv_bwd", after=sent)
    dx1, da0, dob0, dgl0 = _mlp_bwd_dx(x1, dx2, a0, gl[0], w1["w_up0"], w1["w_down0"], "mlp0_bwd_dx")
    dwu0, dwd0 = _mlp_bwd_dw(hm0, da0, a0, dob0, fb, "mlp0_bwd_dw")
    sent = send(1, dict(w_up0=dwu0, w_down0=dwd0))
    dyg, dx1b = _matmul_nt(dx1, w1["w_out"], "rg_out_bwd", F32, after=sent)
    dwout = _matmul_tn(yg, dx1b[None], QKV_NB, "rg_out_dw", blocked=False)
    dz, dwa, dwx, dpv = _rg_bwd(z, hf, hb, dyg, w0["pvec"], w["wa"], w["wx"], L, "rg_bwd")
    sent = send(4, dict(w_out=dwout, pvec=dpv, wa=dwa, wx=dwx))
    dwin = _matmul_tn(h0, dz, nb_in, "rg_in_dw", blocked=True, after=sent)
    sent = send(0, dict(w_in=dwin))
    dx0, dgm0 = _nt_normbwd(dz, w0["w_in"], x, gm[0], dx1, "rg_in_bwd", after=sent)
    send(-1, dict(g_mix=[dgm0, dgm1], g_mlp=[dgl0, dgl1], g_fin=dgf, conv_b=dpv[PV_CONV_B:PV_CONV_B + 1], qg=dqg, kg=dkg, loss=loss))
    return _deinterleave(dx0.reshape(Bl, L, D))


MESH = pl.DeviceIdType.MESH
ANY = pl.BlockSpec(memory_space=pl.ANY)
N_PEERS = N_DEV - 1


def _my_place():
    return lax.axis_index("x"), lax.axis_index("y"), lax.axis_index("c")


def _flat(px, py, pc):
    return 4 * px + 2 * py + pc


def _all_gather(shards, name):
    n = len(shards)

    def body(*refs):
        ins, outs = refs[:n], refs[n:2 * n]
        send_sems, recv_sems, local_sems = refs[2 * n:]
        x, y, c = _my_place()
        me, sibling = (x, y, c), (x, y, 1 - c)
        chips = [(1 - x, y), (x, 1 - y), (1 - x, 1 - y)]

        def copy(a, k, block, to, src=None):
            dst = outs[a].at[_flat(*block)]
            return pltpu.make_async_remote_copy(
                src_ref=dst if src is None else src, dst_ref=dst,
                send_sem=send_sems.at[a, k], recv_sem=recv_sems.at[a, k],
                device_id=to, device_id_type=MESH)

        mine = [pltpu.make_async_copy(ins[a], outs[a].at[_flat(*me)], local_sems.at[a]) for a in range(n)]
        for cp in mine:
            cp.start()
        first = []
        for a in range(n):
            first.append(copy(a, 0, me, sibling, src=ins[a]))
            first += [copy(a, 1 + j, me, (*chip, c), src=ins[a]) for j, chip in enumerate(chips)]
        for cp in first:
            cp.start()
        passed = []
        for j, chip in enumerate(chips):
            for a in range(n):
                copy(a, 1 + j, (*chip, c), me).wait_recv()
                fwd = copy(a, 4 + j, (*chip, c), sibling)
                fwd.start()
                passed.append(fwd)
        for a in range(n):
            copy(a, 0, sibling, me).wait_recv()
            for j, chip in enumerate(chips):
                copy(a, 4 + j, (*chip, 1 - c), me).wait_recv()
        for cp in first + passed:
            cp.wait_send()
        for cp in mine:
            cp.wait()

    return pl.pallas_call(
        body, name=name,
        in_specs=[ANY] * n, out_specs=[ANY] * n,
        out_shape=[S((N_DEV,) + s.shape, s.dtype) for s in shards],
        scratch_shapes=[pltpu.SemaphoreType.DMA((n, N_PEERS)), pltpu.SemaphoreType.DMA((n, N_PEERS)),
                        pltpu.SemaphoreType.DMA((n,))],
    )(*shards)


HBM = pl.BlockSpec(memory_space=pltpu.HBM)
SEM = pl.BlockSpec(memory_space=pltpu.SEMAPHORE)
SIDE_EFFECT = pltpu.SideEffectType.DATAFLOW_SIDE_EFFECTING
SEMS_PER_GROUP = 3


NEAR_PEERS = (1, 2, 4, 6)
FAR_CHIPS = (2, 4, 6)


def _exchange_copies(srcs, lands, sems, mode):
    send_sems, recv_sems, local_sems = sems
    scatter = mode == "scatter"
    x, y, c = _my_place()
    me = _flat(x, y, c)
    remote, local = [], []
    for a in range(len(srcs)):
        for r in (NEAR_PEERS if mode == "near" else range(1, N_DEV)):
            peer = (1 - x if r & 4 else x, 1 - y if r & 2 else y, 1 - c if r & 1 else c)
            remote.append(pltpu.make_async_remote_copy(
                src_ref=srcs[a].at[_flat(*peer)] if scatter else srcs[a], dst_ref=lands[a].at[me],
                send_sem=send_sems.at[a * N_PEERS + r - 1], recv_sem=recv_sems.at[a * N_PEERS + r - 1],
                device_id=peer, device_id_type=MESH))
        local.append(pltpu.make_async_copy(srcs[a].at[me] if scatter else srcs[a], lands[a].at[me], local_sems.at[a]))
    return remote, local


def _exchange_start(groups, modes, name):
    sizes = [len(g) for g in groups]
    srcs = [pltpu.with_memory_space_constraint(a, pltpu.HBM) for g in groups for a in g]
    n = len(srcs)
    scatter_of = [m == "scatter" for g, m in zip(groups, modes) for _ in g]
    lands = [pltpu.with_memory_space_constraint(lax.empty(a.shape if sc else (N_DEV,) + a.shape, a.dtype), pltpu.HBM)
             for a, sc in zip(srcs, scatter_of)]
    n_sem = SEMS_PER_GROUP * len(groups)

    def body(*refs):
        src_refs, land_refs, sem_refs, token = refs[:n], refs[n:2 * n], refs[2 * n:2 * n + n_sem], refs[-1]
        off = 0
        for gi, k in enumerate(sizes):
            remote, local = _exchange_copies(src_refs[off:off + k], land_refs[off:off + k],
                                             sem_refs[SEMS_PER_GROUP * gi:SEMS_PER_GROUP * (gi + 1)], modes[gi])
            for cp in local + remote:
                cp.start()
            off += k
        token[...] = jnp.zeros_like(token)

    sem_shapes = []
    for k in sizes:
        sem_shapes += [pltpu.SemaphoreType.DMA((k * N_PEERS,)), pltpu.SemaphoreType.DMA((k * N_PEERS,)),
                       pltpu.SemaphoreType.DMA((k,))]
    outs = pl.pallas_call(
        body, name=name,
        out_shape=sem_shapes + [pltpu.HBM(a.shape, a.dtype) for a in srcs + lands] + [S((8, 128), F32)],
        in_specs=[HBM] * (2 * n),
        out_specs=[SEM] * n_sem + [HBM] * (2 * n) + [pl.BlockSpec(memory_space=pltpu.VMEM)],
        input_output_aliases={i: n_sem + i for i in range(2 * n)},
        compiler_params=pltpu.CompilerParams(has_side_effects=SIDE_EFFECT),
    )(*srcs, *lands)
    sems, thru, token = outs[:n_sem], outs[n_sem:n_sem + 2 * n], outs[-1]
    per_group, off = [], 0
    for gi, k in enumerate(sizes):
        per_group.append((sems[SEMS_PER_GROUP * gi:SEMS_PER_GROUP * (gi + 1)], thru[off:off + k], thru[n + off:n + off + k]))
        off += k
    return per_group, token


def _exchange_wait(group, after, mode, name):
    sems, srcs, lands = group
    k = len(srcs)

    def body(*refs):
        remote, local = _exchange_copies(refs[:k], refs[k:2 * k], refs[2 * k:2 * k + SEMS_PER_GROUP], mode)
        for cp in remote:
            cp.wait_send()
            cp.wait_recv()
        for cp in local:
            cp.wait()

    outs = pl.pallas_call(
        body, name=name,
        out_shape=[pltpu.HBM(a.shape, a.dtype) for a in list(srcs) + list(lands)],
        in_specs=[HBM] * (2 * k) + [SEM] * SEMS_PER_GROUP + [ANY] * len(after),
        out_specs=[HBM] * (2 * k),
        input_output_aliases={i: i for i in range(2 * k)},
        compiler_params=pltpu.CompilerParams(has_side_effects=SIDE_EFFECT),
    )(*srcs, *lands, *sems, *after)
    return outs[k:]


def _forward_copies(lands, sems):
    send_sems, recv_sems = sems
    x, y, c = _my_place()
    mine, theirs = [], []
    for a in range(len(lands)):
        for k, r in enumerate(FAR_CHIPS):
            px, py = (1 - x if r & 4 else x), (1 - y if r & 2 else y)
            for out, core in ((mine, c), (theirs, 1 - c)):
                blk = lands[a].at[_flat(px, py, core)]
                out.append(pltpu.make_async_remote_copy(
                    src_ref=blk, dst_ref=blk, send_sem=send_sems.at[a * len(FAR_CHIPS) + k],
                    recv_sem=recv_sems.at[a * len(FAR_CHIPS) + k], device_id=(x, y, 1 - c), device_id_type=MESH))
    return mine, theirs


def _forward_start(groups, name):
    sizes = [len(g) for g in groups]
    lands = [a for g in groups for a in g]
    n = len(lands)
    n_sem = 2 * len(groups)

    def body(*refs):
        land_refs, sem_refs, token = refs[:n], refs[n:n + n_sem], refs[-1]
        off = 0
        for gi, k in enumerate(sizes):
            mine, _ = _forward_copies(land_refs[off:off + k], sem_refs[2 * gi:2 * gi + 2])
            for cp in mine:
                cp.start()
            off += k
        token[...] = jnp.zeros_like(token)

    sem_shapes = []
    for k in sizes:
        sem_shapes += [pltpu.SemaphoreType.DMA((k * len(FAR_CHIPS),))] * 2
    outs = pl.pallas_call(
        body, name=name,
        out_shape=sem_shapes + [pltpu.HBM(a.shape, a.dtype) for a in lands] + [S((8, 128), F32)],
        in_specs=[HBM] * n,
        out_specs=[SEM] * n_sem + [HBM] * n + [pl.BlockSpec(memory_space=pltpu.VMEM)],
        input_output_aliases={i: n_sem + i for i in range(n)},
        compiler_params=pltpu.CompilerParams(has_side_effects=SIDE_EFFECT),
    )(*lands)
    per_group, off = [], 0
    for gi, k in enumerate(sizes):
        per_group.append((outs[2 * gi:2 * gi + 2], outs[n_sem + off:n_sem + off + k]))
        off += k
    return per_group


def _forward_wait(group, after, name):
    sems, lands = group
    k = len(lands)

    def body(*refs):
        mine, theirs = _forward_copies(refs[:k], refs[k:k + 2])
        for cp in mine:
            cp.wait_send()
        for cp in theirs:
            cp.wait_recv()

    return pl.pallas_call(
        body, name=name,
        out_shape=[pltpu.HBM(a.shape, a.dtype) for a in lands],
        in_specs=[HBM] * k + [SEM] * 2 + [ANY] * len(after),
        out_specs=[HBM] * k,
        input_output_aliases={i: i for i in range(k)},
        compiler_params=pltpu.CompilerParams(has_side_effects=SIDE_EFFECT),
    )(*lands, *sems, *after)


def _row_tile(rows, cols):
    want = max(16, (128 * 1024) // cols)
    if rows <= want:
        return rows
    t = want - want % 16
    while rows % t:
        t -= 16
    return t


def _sum_parts(parts, name):
    P, R, C = parts.shape
    tr = _row_tile(R, C)

    def body(p_ref, o_ref):
        g = p_ref[0].astype(F32)
        for i in range(1, P):
            g = g + p_ref[i].astype(F32)
        o_ref[...] = g

    return pl.pallas_call(
        body, name=name, grid=(R // tr,),
        in_specs=[pl.BlockSpec((P, tr, C), lambda i: (0, i, 0))],
        out_specs=pl.BlockSpec((tr, C), lambda i: (i, 0)),
        out_shape=S((R, C), F32),
        compiler_params=_cp("parallel"),
    )(parts)


def _adamw(parts, w, m, v, name, after=()):
    P, R, C = parts.shape
    tr = _row_tile(R, C)
    c1 = 1.0 - ADAM_B1 ** ADAM_STEP
    c2 = 1.0 - ADAM_B2 ** ADAM_STEP

    def body(p_ref, w_ref, m_ref, v_ref, *rest):
        g_ref, d_ref, mo_ref, vo_ref = rest[len(after):]
        g = p_ref[0].astype(F32)
        for i in range(1, P):
            g = g + p_ref[i].astype(F32)
        mn = ADAM_B1 * m_ref[...] + (1.0 - ADAM_B1) * g
        vn = ADAM_B2 * v_ref[...] + (1.0 - ADAM_B2) * (g * g)
        g_ref[...] = g
        mo_ref[...] = mn
        vo_ref[...] = vn
        d_ref[...] = (-ADAM_LR) * ((mn / c1) / (jnp.sqrt(vn / c2) + ADAM_EPS) + ADAM_WD * w_ref[...])

    blk = pl.BlockSpec((tr, C), lambda i: (i, 0))
    return pl.pallas_call(
        body, name=name, grid=(R // tr,),
        in_specs=[pl.BlockSpec((P, tr, C), lambda i: (0, i, 0)), blk, blk, blk] + [ANY] * len(after),
        out_specs=[blk, blk, blk, blk],
        out_shape=[S((R, C), F32)] * 4,
        compiler_params=_cp("parallel"),
    )(parts, w, m, v, *after)


def _adamw_layer(parts, w3, m3, v3, layer, prev, name, after=()):
    P, R, C = parts.shape
    NL = w3.shape[0]
    tr = _row_tile(R, C)
    c1 = 1.0 - ADAM_B1 ** ADAM_STEP
    c2 = 1.0 - ADAM_B2 ** ADAM_STEP
    n_prev = 0 if prev is None else len(prev)

    def body(p_ref, w_ref, m_ref, v_ref, *rest):
        g_ref, d_ref, mo_ref, vo_ref = rest[n_prev + len(after):]
        g = p_ref[0].astype(F32)
        for i in range(1, P):
            g = g + p_ref[i].astype(F32)
        mn = ADAM_B1 * m_ref[...] + (1.0 - ADAM_B1) * g
        vn = ADAM_B2 * v_ref[...] + (1.0 - ADAM_B2) * (g * g)
        g_ref[...] = g
        mo_ref[...] = mn
        vo_ref[...] = vn
        d_ref[...] = (-ADAM_LR) * ((mn / c1) / (jnp.sqrt(vn / c2) + ADAM_EPS) + ADAM_WD * w_ref[...])

    blk = pl.BlockSpec((None, tr, C), lambda i: (layer, i, 0))
    return pl.pallas_call(
        body, name=name, grid=(R // tr,),
        in_specs=[pl.BlockSpec((P, tr, C), lambda i: (0, i, 0)), blk, blk, blk] + [ANY] * (n_prev + len(after)),
        out_specs=[blk, blk, blk, blk],
        out_shape=[S((NL, R, C), F32)] * 4,
        input_output_aliases={4 + k: k for k in range(n_prev)},
        compiler_params=_cp("parallel"),
    )(parts, w3, m3, v3, *(prev or ()), *after)


VMEM_WHOLE = pl.BlockSpec(memory_space=pltpu.VMEM)


def _pack_vectors(vectors, starts, rows, name):
    def body(*refs):
        o_ref = refs[-1]
        o_ref[...] = jnp.zeros_like(o_ref)
        for v_ref, r0 in zip(refs[:-1], starts):
            for j in range(v_ref.shape[1] // 128):
                o_ref[r0 + j:r0 + j + 1, :] = v_ref[:, j * 128:(j + 1) * 128]

    return pl.pallas_call(body, name=name, in_specs=[VMEM_WHOLE] * len(vectors), out_specs=VMEM_WHOLE,
                          out_shape=S((rows, 128), F32))(*vectors)


def _adamw_vectors(g_pack, params, keep_rows, name, after=()):
    n = len(params)
    P = g_pack.shape[0]
    c1 = 1.0 - ADAM_B1 ** ADAM_STEP
    c2 = 1.0 - ADAM_B2 ** ADAM_STEP

    def body(g_ref, *refs):
        ins, outs = refs[:3 * n], refs[3 * n + len(after):]
        gs = g_ref[0]
        for i in range(1, P):
            gs = gs + g_ref[i]
        for pi, (_, _, _, slots) in enumerate(params):
            w_ref, m_ref, v_ref = ins[3 * pi:3 * pi + 3]
            g_out, d_out, m_out, v_out = outs[4 * pi:4 * pi + 4]
            for idx, row in slots:
                g = gs[row:row + 1, :]
                mn = ADAM_B1 * m_ref[idx] + (1.0 - ADAM_B1) * g
                vn = ADAM_B2 * v_ref[idx] + (1.0 - ADAM_B2) * (g * g)
                g_out[idx] = g
                m_out[idx] = mn
                v_out[idx] = vn
                d_out[idx] = (-ADAM_LR) * ((mn / c1) / (jnp.sqrt(vn / c2) + ADAM_EPS) + ADAM_WD * w_ref[idx])
        outs[-1][...] = jnp.concatenate([gs[r:r + 1, :] for r in keep_rows], axis=0)

    flat = [a for w, m, v, _ in params for a in (w, m, v)]
    out_shape = [S(w.shape, F32) for w, _, _, _ in params for _ in range(4)] + [S((len(keep_rows), 128), F32)]
    outs = pl.pallas_call(
        body, name=name,
        in_specs=[VMEM_WHOLE] * (1 + len(flat)) + [ANY] * len(after),
        out_specs=[VMEM_WHOLE] * len(out_shape), out_shape=out_shape,
    )(g_pack, *flat, *after)
    return [outs[4 * i:4 * i + 4] for i in range(n)], outs[-1]


def _adamw_nd(parts, w, m, v, name, after=()):
    shp = w.shape
    C = shp[-1]
    outs = _adamw(parts.reshape(parts.shape[0], -1, C), w.reshape(-1, C), m.reshape(-1, C), v.reshape(-1, C), name, after)
    return [o.reshape(shp) for o in outs]


TILE_ROWS = 8


REP_SMALL_ROWS = 128
REP_GRAD_STARTS = (0, 8, 16, 24, 32, 40, 48, 56, 64)
REP_SMALL_STARTS = (0, 16, 32, 40, 48, 56)
REP_LOSS_ROW = 64


def _small_pack(cw, ba, bx, lam):
    pad8 = lambda a: jnp.pad(a, ((0, TILE_ROWS - a.shape[0]), (0, 0)))
    return jnp.concatenate([pad8(cw[0, :, 0, :]), pad8(ba[0]), pad8(bx[0]), pad8(lam[0]),
                            jnp.zeros((PV_ROWS - PV_CONV_B, LRU_BW), F32)], axis=0)


def kernel(x, norm_mix_g, norm_mlp_g, rg_w_in, rg_conv_w, rg_conv_b, rg_w_a, rg_b_a, rg_w_x, rg_b_x, rg_lam, rg_w_out, at_w_qkv, at_q_g, at_k_g, at_w_o, mlp_w_up, mlp_w_down, final_g, loss_target, m_norm_mix_g, m_norm_mlp_g, m_rg_w_in, m_rg_conv_w, m_rg_conv_b, m_rg_w_a, m_rg_b_a, m_rg_w_x, m_rg_b_x, m_rg_lam, m_rg_w_out, m_at_w_qkv, m_at_q_g, m_at_k_g, m_at_w_o, m_mlp_w_up, m_mlp_w_down, m_final_g, v_norm_mix_g, v_norm_mlp_g, v_rg_w_in, v_rg_conv_w, v_rg_conv_b, v_rg_w_a, v_rg_b_a, v_rg_w_x, v_rg_b_x, v_rg_lam, v_rg_w_out, v_at_w_qkv, v_at_q_g, v_at_k_g, v_at_w_o, v_mlp_w_up, v_mlp_w_down, v_final_g):
    D = x.shape[-1]
    bf = lambda a: a.astype(BF16)

    sp_w = _small_pack(rg_conv_w, rg_b_a, rg_b_x, rg_lam)
    started, _ = _exchange_start(
        [[bf(rg_w_in[0]), sp_w], [bf(rg_w_out[0])], [bf(mlp_w_up[0]), bf(mlp_w_down[0])],
         [bf(at_w_qkv[0]), bf(at_w_o[0])], [bf(mlp_w_up[1]), bf(mlp_w_down[1])]],
        ["near", "near", "near", "gather", "gather"], "gather_start")
    gathers = dict(zip((0, 1, 4, 2, 3), started))
    forwards = {}

    def fetch(stage, after):
        after = tuple(after)
        if stage == 0:
            got = _exchange_wait(gathers[0], after, "near", "gather_wait0")
            g_in, g_sp = _forward_wait(_forward_start([got], "forward_start0")[0], (), "forward_wait0")
            pvec = g_sp.transpose(1, 0, 2).reshape(PV_ROWS, D)
            pvec = jnp.concatenate([pvec[:PV_CONV_B], jnp.broadcast_to(rg_conv_b, (PV_ROWS - PV_CONV_B, D))], axis=0)
            return dict(w_in=g_in, pvec=pvec)
        if stage == 1:
            near = [_exchange_wait(gathers[s], after, "near", "gather_wait%d" % s) for s in (1, 4)]
            f_out, forwards[4] = _forward_start(near, "forward_start1")
            g_out, = _forward_wait(f_out, (), "forward_wait1")
            return dict(w_out=g_out.reshape(D, D))
        if stage == 4:
            g_up0, g_dn0 = _forward_wait(forwards[4], after, "forward_wait4")
            return dict(w_up0=g_up0, w_down0=g_dn0.reshape(-1, D))
        got = _exchange_wait(gathers[stage], after, "gather", "gather_wait%d" % stage)
        if stage == 2:
            g_qkv = got[0]
            cols = g_qkv.shape[0] * g_qkv.shape[2]
            return dict(w_qkv=g_qkv.transpose(1, 0, 2).reshape(D, cols // QKV_NB, QKV_NB).transpose(1, 0, 2),
                        w_o=got[1].reshape(D, D))
        return dict(w_up1=got[0], w_down1=got[1].reshape(-1, D))

    scatters = {}

    def send(stage, g):
        if stage == 3:
            arrs = [g["w_up1"], g["w_down1"].reshape(N_DEV, -1, D)]
        elif stage == 2:
            arrs = [g["w_qkv"].reshape(D, N_DEV, -1).transpose(1, 0, 2), g["w_o"].reshape(N_DEV, -1, D)]
        elif stage == 1:
            arrs = [g["w_up0"], g["w_down0"].reshape(N_DEV, -1, D)]
        elif stage == 4:
            arrs = [g["w_out"].reshape(N_DEV, -1, D), g["pvec"].reshape(PV_ROWS, N_DEV, LRU_BW).transpose(1, 0, 2),
                    bf(g["wa"]).reshape(N_DEV, -1, 128), bf(g["wx"]).reshape(N_DEV, -1, 128)]
        elif stage == 0:
            arrs = [g["w_in"]]
        else:
            small = _pack_vectors(g["g_mix"] + g["g_mlp"] + [g["g_fin"], g["conv_b"], g["qg"], g["kg"], g["loss"]],
                                  REP_GRAD_STARTS, REP_SMALL_ROWS, "pack_rep_small")
            arrs = [small.reshape(N_DEV, -1, 128)]
        (group,), token = _exchange_start([arrs], ["scatter"], "scatter_start%d" % (stage % 6))
        scatters[stage] = (group, token)
        return (token,)

    w = dict(g_mix=norm_mix_g, g_mlp=norm_mlp_g, g_fin=final_g[None], qg=at_q_g, kg=at_k_g,
             wa=bf(rg_w_a[0]), wx=bf(rg_w_x[0]))
    grad_x = _local_step(x, loss_target, w, fetch, send)

    res = {}
    r_up1, r_dn1 = _exchange_wait(scatters[3][0], (scatters[-1][1],), "scatter", "scatter_wait3")
    r_out, r_sp, r_wa, r_wx = _exchange_wait(scatters[4][0], (r_up1,), "scatter", "scatter_wait4")
    (rep_gather,), rep_token = _exchange_start(
        [[_sum_parts(r_wa, "reduce_w_a"), _sum_parts(r_wx, "reduce_w_x")]], ["gather"], "rep_gather_start")
    up = _adamw_layer(r_up1, mlp_w_up, m_mlp_w_up, v_mlp_w_up, 1, None, "adam_mlp_w_up1", after=(rep_token,))
    dn = _adamw_layer(r_dn1, mlp_w_down, m_mlp_w_down, v_mlp_w_down, 1, None, "adam_mlp_w_down1")
    r_qkv, r_o = _exchange_wait(scatters[2][0], (dn[0],), "scatter", "scatter_wait2")
    res["at_w_qkv"] = _adamw_nd(r_qkv[:, None], at_w_qkv, m_at_w_qkv, v_at_w_qkv, "adam_at_w_qkv")
    res["at_w_o"] = _adamw_nd(r_o[:, None], at_w_o, m_at_w_o, v_at_w_o, "adam_at_w_o")
    r_up0, r_dn0 = _exchange_wait(scatters[1][0], (res["at_w_o"][0],), "scatter", "scatter_wait1")
    res["mlp_w_up"] = _adamw_layer(r_up0, mlp_w_up, m_mlp_w_up, v_mlp_w_up, 0, up, "adam_mlp_w_up0")
    res["mlp_w_down"] = _adamw_layer(r_dn0, mlp_w_down, m_mlp_w_down, v_mlp_w_down, 0, dn, "adam_mlp_w_down0")
    r_in, = _exchange_wait(scatters[0][0], (res["mlp_w_down"][0],), "scatter", "scatter_wait0")
    res["rg_w_in"] = _adamw_nd(r_in[:, None], rg_w_in, m_rg_w_in, v_rg_w_in, "adam_rg_w_in")
    res["rg_w_out"] = _adamw_nd(r_out[:, None], rg_w_out, m_rg_w_out, v_rg_w_out, "adam_rg_w_out", (res["rg_w_in"][0],))
    whole, lane = slice(None), slice(0, 1)
    two_rows = lambda r0: [((0, slice(d, d + 1), whole), r0 + d) for d in range(2)]
    (res["rg_conv_w"], res["rg_b_a"], res["rg_b_x"], res["rg_lam"]), _ = _adamw_vectors(
        r_sp, [(rg_conv_w, m_rg_conv_w, v_rg_conv_w, [((0, t, lane, whole), PV_CONV_W + t) for t in range(CONV_W)]),
               (rg_b_a, m_rg_b_a, v_rg_b_a, two_rows(PV_B_A)), (rg_b_x, m_rg_b_x, v_rg_b_x, two_rows(PV_B_X)),
               (rg_lam, m_rg_lam, v_rg_lam, two_rows(PV_LAM))], [0], "adam_small", (res["rg_w_out"][0],))

    r_small, = _exchange_wait(scatters[-1][0], (res["rg_lam"][0],), "scatter", "scatter_wait5")
    small_sum, = _all_gather([_sum_parts(r_small, "reduce_rep_small")], "gather_replicated")
    wa_sum, wx_sum = _exchange_wait(rep_gather, (small_sum,), "gather", "rep_gather_wait")
    rows = lambda a: a.reshape(-1, 128)
    res["rg_w_a"] = [o.reshape(rg_w_a.shape) for o in _adamw(
        wa_sum.reshape(1, -1, 128), rows(rg_w_a), rows(m_rg_w_a), rows(v_rg_w_a), "adam_rg_w_a")]
    res["rg_w_x"] = [o.reshape(rg_w_x.shape) for o in _adamw(
        wx_sum.reshape(1, -1, 128), rows(rg_w_x), rows(m_rg_w_x), rows(v_rg_w_x), "adam_rg_w_x")]
    def vec_slots(a, r0):
        per = a.shape[1] // 128
        return [((slice(l, l + 1), slice(128 * j, 128 * (j + 1))), r0 + l * per + j) for l in range(a.shape[0]) for j in range(per)]

    fin = [final_g[None], m_final_g[None], v_final_g[None]]
    vecs = [(norm_mix_g, m_norm_mix_g, v_norm_mix_g), (norm_mlp_g, m_norm_mlp_g, v_norm_mlp_g), fin,
            (rg_conv_b, m_rg_conv_b, v_rg_conv_b), (at_q_g, m_at_q_g, v_at_q_g), (at_k_g, m_at_k_g, v_at_k_g)]
    outs, kept = _adamw_vectors(
        small_sum.reshape(1, -1, 128),
        [(wv, mv, vv, vec_slots(wv, r0)) for (wv, mv, vv), r0 in zip(vecs, REP_SMALL_STARTS)], [REP_LOSS_ROW], "adam_rep_small")
    for nm, o in zip(["norm_mix_g", "norm_mlp_g", "final_g", "rg_conv_b", "at_q_g", "at_k_g"], outs):
        res[nm] = [a[0] for a in o] if nm == "final_g" else o
    loss = kept[0, 0]

    order = ["norm_mix_g", "norm_mlp_g", "rg_w_in", "rg_conv_w", "rg_conv_b", "rg_w_a", "rg_b_a", "rg_w_x", "rg_b_x",
             "rg_lam", "rg_w_out", "at_w_qkv", "at_q_g", "at_k_g", "at_w_o", "mlp_w_up", "mlp_w_down", "final_g"]
    return (loss, grad_x, *[res[nm][k] for k in range(4) for nm in order])
```

```python
import functools
import math

import jax
import jax.numpy as jnp
from jax import lax
from jax.experimental import pallas as pl
from jax.experimental.pallas import tpu as pltpu

F32 = jnp.float32
BF16 = jnp.bfloat16
S = jax.ShapeDtypeStruct

EPS = 1e-6
HEAD_DIM = 128
N_KV = 2
GRID_W = 64
ROPE_THETA = 10000.0
LRU_BW = 128
RG_C = 8.0
CONV_W = 4
N_DEV = 8
N_SEG = 8
SCAN_UNROLL = 8
TN_STEP_COLS = 512
PROJ_TM = 1024
VMEM_LIMIT_V7X = 56 * 1024 * 1024
SOFTMAX_SCALE = 1.0 / math.sqrt(HEAD_DIM)
GELU_K = math.sqrt(2.0 / math.pi)
GELU_C = 0.044715

ADAM_LR = 0.001
ADAM_B1 = 0.9
ADAM_B2 = 0.999
ADAM_EPS = 1e-08
ADAM_WD = 0.01
ADAM_STEP = 10

NT = (((1,), (1,)), ((), ()))
TN = (((0,), (0,)), ((), ()))


def _cp(*sem):
    return pltpu.CompilerParams(dimension_semantics=sem, vmem_limit_bytes=VMEM_LIMIT_V7X)


def _rms_r(xv):
    return lax.rsqrt(jnp.mean(xv * xv, axis=-1, keepdims=True) + EPS)


def _rms_bwd(dh, xv, g):
    r = _rms_r(xv)
    xh = xv * r
    dg = jnp.sum(dh * xh, axis=0, keepdims=True)
    dxh = dh * g
    dx = r * (dxh - xh * jnp.mean(dxh * xh, axis=-1, keepdims=True))
    return dx, dg


def _dot(a, b):
    return jnp.dot(a, b, preferred_element_type=F32)


def _dot_nt(a, b):
    return lax.dot_general(a, b, NT, preferred_element_type=F32)


def _dot_tn(a, b):
    return lax.dot_general(a, b, TN, preferred_element_type=F32)


def _norm_matmul(x, g, wblk, name, out_dtype=F32, transposed=False):
    T, D = x.shape
    NB, nb = wblk.shape[0], wblk.shape[1 if transposed else 2]
    mm = _dot_nt if transposed else _dot
    tm = min(T, PROJ_TM)

    def body(x_ref, g_ref, w_ref, o_ref, h_ref):
        xv = x_ref[...]
        hb = (xv * _rms_r(xv) * g_ref[...]).astype(BF16)
        h_ref[...] = hb
        for q in range(NB):
            o_ref[:, q * nb:(q + 1) * nb] = mm(hb, w_ref[q]).astype(o_ref.dtype)

    return pl.pallas_call(
        body, name=name, grid=(T // tm,),
        in_specs=[pl.BlockSpec((tm, D), lambda i: (i, 0)),
                  pl.BlockSpec((1, D), lambda i: (0, 0)),
                  pl.BlockSpec(wblk.shape, lambda i: (0, 0, 0))],
        out_specs=[pl.BlockSpec((tm, NB * nb), lambda i: (i, 0)),
                   pl.BlockSpec((tm, D), lambda i: (i, 0))],
        out_shape=[S((T, NB * nb), out_dtype), S((T, D), BF16)],
        compiler_params=_cp("parallel"),
    )(x, g, wblk)


def _matmul_res(a, w, res, name):
    T, K = a.shape
    N = w.shape[1]
    tm = min(T, PROJ_TM)

    def body(a_ref, w_ref, r_ref, o_ref):
        o_ref[...] = r_ref[...] + _dot(a_ref[...], w_ref[...])

    return pl.pallas_call(
        body, name=name, grid=(T // tm,),
        in_specs=[pl.BlockSpec((tm, K), lambda i: (i, 0)),
                  pl.BlockSpec((K, N), lambda i: (0, 0)),
                  pl.BlockSpec((tm, N), lambda i: (i, 0))],
        out_specs=pl.BlockSpec((tm, N), lambda i: (i, 0)),
        out_shape=S((T, N), F32),
        compiler_params=_cp("parallel"),
    )(a, w, res)


def _matmul_nt(a, w, name, out_dtype, after=()):
    T, N = a.shape
    K = w.shape[0]
    tm = min(T, PROJ_TM)

    def body(a_ref, w_ref, *rest):
        o_ref, ab_ref = rest[len(after):]
        ab = a_ref[...].astype(BF16)
        ab_ref[...] = ab
        o_ref[...] = _dot_nt(ab, w_ref[...]).astype(o_ref.dtype)

    return pl.pallas_call(
        body, name=name, grid=(T // tm,),
        in_specs=[pl.BlockSpec((tm, N), lambda i: (i, 0)),
                  pl.BlockSpec((K, N), lambda i: (0, 0))] + [pl.BlockSpec(memory_space=pl.ANY)] * len(after),
        out_specs=[pl.BlockSpec((tm, K), lambda i: (i, 0)),
                   pl.BlockSpec((tm, N), lambda i: (i, 0))],
        out_shape=[S((T, K), out_dtype), S((T, N), BF16)],
        compiler_params=_cp("parallel"),
    )(a, w, *after)


def _matmul_tn(a, b3, nb, name, blocked, after=()):
    T, M = a.shape
    SB, _, N = b3.shape
    per = N // nb
    NB = SB * per
    tk = min(T, 1024)
    nk = T // tk
    jb = max(1, TN_STEP_COLS // nb) if blocked else 1
    assert per % jb == 0
    if blocked:
        out_spec, out_shape = pl.BlockSpec((jb, M, nb), lambda j, k: (j, 0, 0)), S((NB, M, nb), BF16)
    else:
        assert SB == 1
        out_spec, out_shape = pl.BlockSpec((M, nb), lambda j, k: (0, j)), S((M, N), BF16)

    def body(a_ref, b_ref, *rest):
        o_ref, acc_ref = rest[len(after):]
        k = pl.program_id(1)

        @pl.when(k == 0)
        def _():
            acc_ref[...] = jnp.zeros_like(acc_ref)

        av = a_ref[...]
        for q in range(jb):
            acc_ref[q] += _dot_tn(av, b_ref[:, q * nb:(q + 1) * nb])

        @pl.when(k == nk - 1)
        def _():
            if blocked:
                o_ref[...] = acc_ref[...].astype(BF16)
            else:
                o_ref[...] = acc_ref[0].astype(BF16)

    return pl.pallas_call(
        body, name=name, grid=(NB // jb, nk),
        in_specs=[pl.BlockSpec((tk, M), lambda j, k: (k, 0)),
                  pl.BlockSpec((None, tk, jb * nb), lambda j, k: ((j * jb) // per, k, ((j * jb) % per) // jb))]
        + [pl.BlockSpec(memory_space=pl.ANY)] * len(after),
        out_specs=out_spec,
        out_shape=out_shape,
        scratch_shapes=[pltpu.VMEM((jb, M, nb), F32)],
        compiler_params=_cp("parallel", "arbitrary"),
    )(a, b3, *after)


def _nt_normbwd(dz3, wblk, x, g, dres, name, after=(), transposed=False):
    T, D = x.shape
    NB, nb = wblk.shape[0], wblk.shape[1 if transposed else 2]
    mm = _dot if transposed else _dot_nt
    SB, _, N = dz3.shape
    per = N // nb
    tm = min(T, PROJ_TM)

    def body(dz_ref, w_ref, x_ref, g_ref, dr_ref, *rest):
        dx_ref, dg_ref = rest[len(after):]

        @pl.when(pl.program_id(0) == 0)
        def _():
            dg_ref[...] = jnp.zeros_like(dg_ref)

        dh = None
        for q in range(NB):
            cols = slice((q % per) * nb, (q % per + 1) * nb)
            part = mm(dz_ref[q // per, :, cols], w_ref[q])
            dh = part if dh is None else dh + part
        dx, dg = _rms_bwd(dh, x_ref[...], g_ref[...])
        dx_ref[...] = dr_ref[...] + dx
        dg_ref[...] += dg

    return pl.pallas_call(
        body, name=name, grid=(T // tm,),
        in_specs=[pl.BlockSpec((SB, tm, N), lambda i: (0, i, 0)),
                  pl.BlockSpec(wblk.shape, lambda i: (0, 0, 0)),
                  pl.BlockSpec((tm, D), lambda i: (i, 0)),
                  pl.BlockSpec((1, D), lambda i: (0, 0)),
                  pl.BlockSpec((tm, D), lambda i: (i, 0))] + [pl.BlockSpec(memory_space=pl.ANY)] * len(after),
        out_specs=[pl.BlockSpec((tm, D), lambda i: (i, 0)),
                   pl.BlockSpec((1, D), lambda i: (0, 0))],
        out_shape=[S((T, D), F32), S((1, D), F32)],
        compiler_params=_cp("arbitrary"),
    )(dz3, wblk, x, g, dres, *after)


def _loss_head(xv, tv, gv, D):
    err = xv * _rms_r(xv) * gv - tv
    e2 = jnp.sum(jnp.sum(err * err, axis=-1, keepdims=True), axis=0, keepdims=True)
    dx, dg = _rms_bwd(err * (1.0 / D), xv, gv)
    return (0.5 / D) * e2, dx, dg


def _mlp_fwd(x, g, wup, wdown, name, head=None):
    T, D = x.shape
    NB, _, fb = wup.shape
    tm = min(T, 1024)
    n_head = 0 if head is None else 2

    def body(x_ref, g_ref, wu_ref, wd_ref, *rest):
        xo_ref, a_ref, h_ref = rest[n_head:n_head + 3]
        acc_ref = rest[-1]
        i, j = pl.program_id(0), pl.program_id(1)

        @pl.when(j == 0)
        def _():
            xv = x_ref[...]
            h_ref[...] = (xv * _rms_r(xv) * g_ref[...]).astype(BF16)
            acc_ref[...] = xv

        a = _dot(h_ref[...], wu_ref[...])
        a_ref[...] = a.astype(BF16)
        u = jnp.maximum(a, 0.0)
        acc_ref[...] += _dot((u * u).astype(BF16), wd_ref[...])

        if head is None:
            @pl.when(j == NB - 1)
            def _():
                xo_ref[...] = acc_ref[...]
        else:
            t_ref, gf_ref = rest[:2]
            loss_ref, dgf_ref = rest[n_head + 3:n_head + 5]

            @pl.when((i == 0) & (j == 0))
            def _():
                loss_ref[...] = jnp.zeros_like(loss_ref)
                dgf_ref[...] = jnp.zeros_like(dgf_ref)

            @pl.when(j == NB - 1)
            def _():
                e2, dx, dg = _loss_head(acc_ref[...], t_ref[...], gf_ref[...], D)
                xo_ref[...] = dx
                loss_ref[...] += e2
                dgf_ref[...] += dg

    row = pl.BlockSpec((tm, D), lambda i, j: (i, 0))
    vec = pl.BlockSpec((1, D), lambda i, j: (0, 0))
    in_specs = [row, vec, pl.BlockSpec((None, D, fb), lambda i, j: (j, 0, 0)), pl.BlockSpec((fb, D), lambda i, j: (j, 0))]
    out_specs = [row, pl.BlockSpec((tm, fb), lambda i, j: (i, j)), row]
    out_shape = [S((T, D), F32), S((T, NB * fb), BF16), S((T, D), BF16)]
    if head is not None:
        in_specs += [row, vec]
        out_specs += [pl.BlockSpec((1, 128), lambda i, j: (0, 0)), vec]
        out_shape += [S((1, 128), F32), S((1, D), F32)]
    return pl.pallas_call(
        body, name=name, grid=(T // tm, NB),
        in_specs=in_specs, out_specs=out_specs, out_shape=out_shape,
        scratch_shapes=[pltpu.VMEM((tm, D), F32)],
        compiler_params=_cp("parallel" if head is None else "arbitrary", "arbitrary"),
    )(x, g, wup, wdown, *(head or ()))


def _mlp_bwd_dx(x, dout, a, g, wup, wdown, name):
    T, D = x.shape
    NB, _, fb = wup.shape
    tm = min(T, 1024)

    def body(x_ref, do_ref, a_ref, g_ref, wu_ref, wd_ref, dx_ref, da_ref, dob_ref, dg_ref, acc_ref):
        i, j = pl.program_id(0), pl.program_id(1)

        @pl.when(j == 0)
        def _():
            dob_ref[...] = do_ref[...].astype(BF16)
            acc_ref[...] = jnp.zeros_like(acc_ref)

        @pl.when((i == 0) & (j == 0))
        def _():
            dg_ref[...] = jnp.zeros_like(dg_ref)

        du2 = _dot_nt(dob_ref[...], wd_ref[...])
        u = jnp.maximum(a_ref[...].astype(F32), 0.0)
        da = (du2 * (2.0 * u)).astype(BF16)
        da_ref[...] = da
        acc_ref[...] += _dot_nt(da, wu_ref[...])

        @pl.when(j == NB - 1)
        def _():
            dx, dg = _rms_bwd(acc_ref[...], x_ref[...], g_ref[...])
            dx_ref[...] = do_ref[...] + dx
            dg_ref[...] += dg

    return pl.pallas_call(
        body, name=name, grid=(T // tm, NB),
        in_specs=[pl.BlockSpec((tm, D), lambda i, j: (i, 0)),
                  pl.BlockSpec((tm, D), lambda i, j: (i, 0)),
                  pl.BlockSpec((tm, fb), lambda i, j: (i, j)),
                  pl.BlockSpec((1, D), lambda i, j: (0, 0)),
                  pl.BlockSpec((None, D, fb), lambda i, j: (j, 0, 0)),
                  pl.BlockSpec((fb, D), lambda i, j: (j, 0))],
        out_specs=[pl.BlockSpec((tm, D), lambda i, j: (i, 0)),
                   pl.BlockSpec((tm, fb), lambda i, j: (i, j)),
                   pl.BlockSpec((tm, D), lambda i, j: (i, 0)),
                   pl.BlockSpec((1, D), lambda i, j: (0, 0))],
        out_shape=[S((T, D), F32), S((T, NB * fb), BF16), S((T, D), BF16), S((1, D), F32)],
        scratch_shapes=[pltpu.VMEM((tm, D), F32)],
        compiler_params=_cp("arbitrary", "arbitrary"),
    )(x, dout, a, g, wup, wdown)


def _mlp_bwd_dw(h, da, a, dob, fb, name):
    T, D = h.shape
    F = a.shape[1]
    NB = F // fb
    tk = min(T, 1024)
    nk = T // tk

    def body(h_ref, da_ref, a_ref, dob_ref, dwu_ref, dwd_ref, au_ref, ad_ref):
        k = pl.program_id(1)

        @pl.when(k == 0)
        def _():
            au_ref[...] = jnp.zeros_like(au_ref)
            ad_ref[...] = jnp.zeros_like(ad_ref)

        au_ref[...] += _dot_tn(h_ref[...], da_ref[...])
        u = jnp.maximum(a_ref[...].astype(F32), 0.0)
        ad_ref[...] += _dot_tn((u * u).astype(BF16), dob_ref[...])

        @pl.when(k == nk - 1)
        def _():
            dwu_ref[...] = au_ref[...].astype(BF16)
            dwd_ref[...] = ad_ref[...].astype(BF16)

    return pl.pallas_call(
        body, name=name, grid=(NB, nk),
        in_specs=[pl.BlockSpec((tk, D), lambda j, k: (k, 0)),
                  pl.BlockSpec((tk, fb), lambda j, k: (k, j)),
                  pl.BlockSpec((tk, fb), lambda j, k: (k, j)),
                  pl.BlockSpec((tk, D), lambda j, k: (k, 0))],
        out_specs=[pl.BlockSpec((None, D, fb), lambda j, k: (j, 0, 0)),
                   pl.BlockSpec((fb, D), lambda j, k: (j, 0))],
        out_shape=[S((NB, D, fb), BF16), S((F, D), BF16)],
        scratch_shapes=[pltpu.VMEM((D, fb), F32), pltpu.VMEM((fb, D), F32)],
        compiler_params=_cp("parallel", "arbitrary"),
    )(h, da, a, dob)


def _rope_tables(L):
    nf = HEAD_DIM // 4
    t = jnp.arange(L, dtype=jnp.int32)
    row = (t // GRID_W).astype(F32)
    col = (t % GRID_W).astype(F32)
    inv = ROPE_THETA ** (-jnp.arange(nf, dtype=F32) / nf)
    ar = row[:, None] * inv
    ac = col[:, None] * inv
    cos = jnp.concatenate([jnp.cos(ar), jnp.cos(ar), jnp.cos(ac), jnp.cos(ac)], axis=-1)
    sin = jnp.concatenate([-jnp.sin(ar), jnp.sin(ar), -jnp.sin(ac), jnp.sin(ac)], axis=-1)
    return cos, sin


def _swap32(x):
    lane = lax.broadcasted_iota(jnp.int32, x.shape, 1)
    up = pltpu.roll(x, HEAD_DIM - 32, 1)
    down = pltpu.roll(x, 32, 1)
    return jnp.where((lane % 64) < 32, up, down)


def _qk_prep(qkv, qg, kg, cos, sin, L, name):
    T, W = qkv.shape
    nh = W // HEAD_DIM - 2 * N_KV
    tm = min(L, 512)
    lb = L // tm

    def body(qkv_ref, qg_ref, kg_ref, cos_ref, sin_ref, q_ref, k_ref, v_ref):
        c = cos_ref[...]
        s = sin_ref[...]
        for h in range(nh + N_KV):
            xh = qkv_ref[:, h * HEAD_DIM:(h + 1) * HEAD_DIM]
            gv = qg_ref[...] if h < nh else kg_ref[...]
            y = xh * _rms_r(xh) * gv
            y = (y * c + _swap32(y) * s).astype(BF16)
            if h < nh:
                q_ref[:, h * HEAD_DIM:(h + 1) * HEAD_DIM] = y
            else:
                k_ref[:, (h - nh) * HEAD_DIM:(h - nh + 1) * HEAD_DIM] = y
        v_ref[...] = qkv_ref[:, (nh + N_KV) * HEAD_DIM:].astype(BF16)

    return pl.pallas_call(
        body, name=name, grid=(T // tm,),
        in_specs=[pl.BlockSpec((tm, W), lambda i: (i, 0)),
                  pl.BlockSpec((1, HEAD_DIM), lambda i: (0, 0)),
                  pl.BlockSpec((1, HEAD_DIM), lambda i: (0, 0)),
                  pl.BlockSpec((tm, HEAD_DIM), lambda i: (i % lb, 0)),
                  pl.BlockSpec((tm, HEAD_DIM), lambda i: (i % lb, 0))],
        out_specs=[pl.BlockSpec((tm, nh * HEAD_DIM), lambda i: (i, 0)),
                   pl.BlockSpec((tm, N_KV * HEAD_DIM), lambda i: (i, 0)),
                   pl.BlockSpec((tm, N_KV * HEAD_DIM), lambda i: (i, 0))],
        out_shape=[S((T, nh * HEAD_DIM), BF16), S((T, N_KV * HEAD_DIM), BF16), S((T, N_KV * HEAD_DIM), BF16)],
        compiler_params=_cp("parallel"),
    )(qkv, qg, kg, cos, sin)


def _qk_prep_bwd(qkv, dq, dk, dv, qg, kg, cos, sin, L, name):
    T, W = qkv.shape
    nh = W // HEAD_DIM - 2 * N_KV
    tm = min(L, 512)
    lb = L // tm

    def body(qkv_ref, dq_ref, dk_ref, dv_ref, qg_ref, kg_ref, cos_ref, sin_ref, dz_ref, dqg_ref, dkg_ref):
        @pl.when(pl.program_id(0) == 0)
        def _():
            dqg_ref[...] = jnp.zeros_like(dqg_ref)
            dkg_ref[...] = jnp.zeros_like(dkg_ref)

        c = cos_ref[...]
        s = sin_ref[...]
        for h in range(nh + N_KV):
            cols = slice(h * HEAD_DIM, (h + 1) * HEAD_DIM)
            if h < nh:
                dout, gv, dg_ref = dq_ref[:, cols], qg_ref[...], dqg_ref
            else:
                kc = slice((h - nh) * HEAD_DIM, (h - nh + 1) * HEAD_DIM)
                dout, gv, dg_ref = dk_ref[:, kc], kg_ref[...], dkg_ref
            dy = dout * c - _swap32(dout) * s
            dx, dg = _rms_bwd(dy, qkv_ref[:, cols], gv)
            dg_ref[...] += dg
            dz_ref[:, cols] = dx.astype(BF16)
        dz_ref[:, (nh + N_KV) * HEAD_DIM:] = dv_ref[...].astype(BF16)

    return pl.pallas_call(
        body, name=name, grid=(T // tm,),
        in_specs=[pl.BlockSpec((tm, W), lambda i: (i, 0)),
                  pl.BlockSpec((tm, nh * HEAD_DIM), lambda i: (i, 0)),
                  pl.BlockSpec((tm, N_KV * HEAD_DIM), lambda i: (i, 0)),
                  pl.BlockSpec((tm, N_KV * HEAD_DIM), lambda i: (i, 0)),
                  pl.BlockSpec((1, HEAD_DIM), lambda i: (0, 0)),
                  pl.BlockSpec((1, HEAD_DIM), lambda i: (0, 0)),
                  pl.BlockSpec((tm, HEAD_DIM), lambda i: (i % lb, 0)),
                  pl.BlockSpec((tm, HEAD_DIM), lambda i: (i % lb, 0))],
        out_specs=[pl.BlockSpec((tm, W), lambda i: (i, 0)),
                   pl.BlockSpec((1, HEAD_DIM), lambda i: (0, 0)),
                   pl.BlockSpec((1, HEAD_DIM), lambda i: (0, 0))],
        out_shape=[S((T, W), BF16), S((1, HEAD_DIM), F32), S((1, HEAD_DIM), F32)],
        compiler_params=_cp("arbitrary"),
    )(qkv, dq, dk, dv, qg, kg, cos, sin)


EXP2_SCALE = SOFTMAX_SCALE * math.log2(math.e)
ATTN_SUB = 128
ATTN_TQ = 1024


def _softmax_rows(q, k):
    s = _dot_nt(q, k)
    e = jnp.exp2((s - jnp.max(s, axis=-1, keepdims=True)) * EXP2_SCALE)
    return e, jnp.sum(e, axis=-1, keepdims=True)


def _attn_fwd(q, k, v, L, name):
    T = q.shape[0]
    nh = q.shape[1] // HEAD_DIM
    G = nh // N_KV
    B = T // L
    tq = min(L, ATTN_TQ)
    nq = L // tq
    sub = min(tq, ATTN_SUB)

    def body(q_ref, k_ref, v_ref, o_ref):
        for h in range(tq // sub):
            rows = slice(h * sub, (h + 1) * sub)
            e, l = _softmax_rows(q_ref[rows, :], k_ref[...])
            o_ref[rows, :] = (_dot(e.astype(BF16), v_ref[...]) / l).astype(BF16)

    qspec = pl.BlockSpec((tq, HEAD_DIM), lambda b, kv, g, qi: (b * nq + qi, kv * G + g))
    kspec = pl.BlockSpec((L, HEAD_DIM), lambda b, kv, g, qi: (b, kv))
    return pl.pallas_call(
        body, name=name, grid=(B, N_KV, G, nq),
        in_specs=[qspec, kspec, kspec],
        out_specs=qspec,
        out_shape=S((T, nh * HEAD_DIM), BF16),
        compiler_params=_cp("parallel", "parallel", "parallel", "parallel"),
    )(q, k, v)


def _attn_bwd(q, k, v, do, o, L, name):
    T = q.shape[0]
    nh = q.shape[1] // HEAD_DIM
    G = nh // N_KV
    B = T // L
    tq = min(L, ATTN_TQ)
    nq = L // tq

    sub = min(tq, ATTN_SUB)

    def body(q_ref, k_ref, v_ref, do_ref, o_ref, dq_ref, dk_ref, dv_ref, ds_scr, p_scr):
        first = (pl.program_id(2) == 0) & (pl.program_id(3) == 0)
        last = (pl.program_id(2) == G - 1) & (pl.program_id(3) == nq - 1)

        @pl.when(first)
        def _():
            dk_ref[...] = jnp.zeros_like(dk_ref)
            dv_ref[...] = jnp.zeros_like(dv_ref)

        for h in range(tq // sub):
            rows = slice(h * sub, (h + 1) * sub)
            dov = do_ref[rows, :]
            e, l = _softmax_rows(q_ref[rows, :], k_ref[...])
            p = e * (1.0 / l)
            dsum = jnp.sum(dov.astype(F32) * o_ref[rows, :].astype(F32), axis=-1, keepdims=True)
            ds_scr[rows, :] = (p * (_dot_nt(dov, v_ref[...]) - dsum)).astype(BF16)
            p_scr[rows, :] = p.astype(BF16)
        ds = ds_scr[...]
        dq_ref[...] = _dot(ds, k_ref[...]) * SOFTMAX_SCALE
        dk_ref[...] += _dot_tn(ds, q_ref[...])
        dv_ref[...] += _dot_tn(p_scr[...], do_ref[...])

        @pl.when(last)
        def _():
            dk_ref[...] = dk_ref[...] * SOFTMAX_SCALE

    qspec = pl.BlockSpec((tq, HEAD_DIM), lambda b, kv, g, qi: (b * nq + qi, kv * G + g))
    kspec = pl.BlockSpec((L, HEAD_DIM), lambda b, kv, g, qi: (b, kv))
    return pl.pallas_call(
        body, name=name, grid=(B, N_KV, G, nq),
        in_specs=[qspec, kspec, kspec, qspec, qspec],
        out_specs=[qspec, kspec, kspec],
        out_shape=[S((T, nh * HEAD_DIM), F32), S((T, N_KV * HEAD_DIM), F32), S((T, N_KV * HEAD_DIM), F32)],
        scratch_shapes=[pltpu.VMEM((tq, L), BF16), pltpu.VMEM((tq, L), BF16)],
        compiler_params=_cp("parallel", "parallel", "arbitrary", "arbitrary"),
    )(q, k, v, do, o)


PV_CONV_W = 0
PV_B_A = 8
PV_B_X = 16
PV_LAM = 24
PV_CONV_B = 32
PV_ROWS = 40


def _shift_rows(x, k):
    if k == 0:
        return x
    L = x.shape[0]
    n = N_SEG * abs(k)
    seg = lax.broadcasted_iota(jnp.int32, (n, x.shape[1]), 0) % N_SEG
    if k > 0:
        edge = jnp.where(seg == 0, 0.0, pltpu.roll(x[L - n:], 1, 0))
        return jnp.concatenate([edge, x[:L - n]], axis=0)
    edge = jnp.where(seg == N_SEG - 1, 0.0, pltpu.roll(x[:n], n - 1, 0))
    return jnp.concatenate([x[n:], edge], axis=0)


def _conv_taps(rec, pv):
    c = pv[PV_CONV_B:PV_CONV_B + 1]
    for j in range(CONV_W):
        c = c + pv[PV_CONV_W + j:PV_CONV_W + j + 1] * _shift_rows(rec, 2 - j)
    return c


def _sigmoid(x):
    return 0.5 * jnp.tanh(0.5 * x) + 0.5


EXPM1_SERIES_BELOW = 0.03


def _rg_gates(c, cbf, wa, wx, ba, bx, lam):
    r = _sigmoid(_dot(cbf, wa) + ba)
    i = _sigmoid(_dot(cbf, wx) + bx)
    sp = jnp.maximum(-lam, 0.0) + jnp.log1p(jnp.exp(-jnp.abs(lam)))
    la = r * ((-RG_C) * sp)
    a = jnp.exp(la)
    a2 = a * a
    x = la + la
    series = -(x * ((x * (1.0 / 6.0) + 0.5) * x + 1.0))
    om = jnp.where(x > -EXPM1_SERIES_BELOW, series, 1.0 - a2)
    rm = lax.rsqrt(om)
    return r, i, a, om * rm, rm, a2, sp


def _gelu(x):
    t = jnp.tanh(GELU_K * (x + GELU_C * x * x * x))
    return 0.5 * x * (1.0 + t), t


def _scan_pair(af_ref, uf_ref, ab_ref, ub_ref, hf_ref, hb_ref, pf_ref, pb_ref, L):
    ls = L // N_SEG
    zero = jnp.zeros((N_SEG, LRU_BW), F32)
    one = jnp.ones((N_SEG, LRU_BW), F32)
    tile = lambda t: pl.ds(pl.multiple_of(t * N_SEG, N_SEG), N_SEG)

    def steps(tc, carry):
        hf, pf, hb, pb = carry
        for q in range(SCAN_UNROLL):
            t = tc * SCAN_UNROLL + q
            rf, rb = tile(t), tile(ls - 1 - t)
            af = af_ref[rf, :]
            hf = af * hf + uf_ref[rf, :]
            pf = pf * af
            hf_ref[rf, :] = hf
            pf_ref[rf, :] = pf
            ab = ab_ref[rb, :]
            hb = ab * hb + ub_ref[rb, :]
            pb = pb * ab
            hb_ref[rb, :] = hb
            pb_ref[rb, :] = pb
        return hf, pf, hb, pb

    hf_e, pf_e, hb_e, pb_e = lax.fori_loop(0, ls // SCAN_UNROLL, steps, (zero, one, zero, one))

    rows, cin = [], jnp.zeros((1, LRU_BW), F32)
    for s in range(N_SEG):
        rows.append(cin)
        cin = hf_e[s:s + 1] + pf_e[s:s + 1] * cin
    cf = jnp.concatenate(rows, axis=0)
    rows, cin = [], jnp.zeros((1, LRU_BW), F32)
    for s in reversed(range(N_SEG)):
        rows.append(cin)
        cin = hb_e[s:s + 1] + pb_e[s:s + 1] * cin
    cb = jnp.concatenate(rows[::-1], axis=0)

    def fix(tc, _):
        for q in range(SCAN_UNROLL):
            r = tile(tc * SCAN_UNROLL + q)
            hf_ref[r, :] = hf_ref[r, :] + pf_ref[r, :] * cf
            hb_ref[r, :] = hb_ref[r, :] + pb_ref[r, :] * cb
        return 0

    lax.fori_loop(0, ls // SCAN_UNROLL, fix, 0)


def _rg_specs(L, D, nblk):
    slab = lambda off: pl.BlockSpec((L, LRU_BW), lambda cb, b: (b, off + cb))
    wspec = pl.BlockSpec((2, None, LRU_BW, LRU_BW), lambda cb, b: (0, cb, 0, 0))
    pvspec = pl.BlockSpec((PV_ROWS, LRU_BW), lambda cb, b: (0, cb))
    return slab, wspec, pvspec


def _rg_fwd(z, pvec, wa, wx, L, name):
    T, C2 = z.shape
    C = C2 // 2
    nblk = C // LRU_BW
    B = T // L
    slab, wspec, pvspec = _rg_specs(L, C, nblk)

    def body(gp_ref, rec_ref, pv_ref, wa_ref, wx_ref, yg_ref, hf_ref, hb_ref, a_scr, u_scr, p_scr):
        pv = pv_ref[...]
        c = _conv_taps(rec_ref[...], pv)
        cbf = c.astype(BF16)
        for d in range(2):
            _, i, a, m, _, _, _ = _rg_gates(c, cbf, wa_ref[d], wx_ref[d], pv[PV_B_A + d:PV_B_A + d + 1],
                                      pv[PV_B_X + d:PV_B_X + d + 1], pv[PV_LAM + d:PV_LAM + d + 1])
            a_scr[d] = a
            u_scr[d] = m * (i * c)
        _scan_pair(a_scr.at[0], u_scr.at[0], a_scr.at[1], u_scr.at[1], hf_ref, hb_ref, p_scr.at[0], p_scr.at[1], L)
        gate, _ = _gelu(gp_ref[...])
        yg_ref[...] = ((hf_ref[...] + hb_ref[...]) * gate).astype(BF16)

    return pl.pallas_call(
        body, name=name, grid=(nblk, B),
        in_specs=[slab(0), slab(nblk), pvspec, wspec, wspec],
        out_specs=[slab(0), slab(0), slab(0)],
        out_shape=[S((T, C), BF16), S((T, C), F32), S((T, C), F32)],
        scratch_shapes=[pltpu.VMEM((2, L, LRU_BW), F32)] * 3,
        compiler_params=_cp("parallel", "parallel"),
    )(z, z, pvec, wa, wx)


def _rg_bwd(z, hf, hb, dyg, pvec, wa, wx, L, name):
    T, C2 = z.shape
    C = C2 // 2
    nblk = C // LRU_BW
    B = T // L
    slab, wspec, pvspec = _rg_specs(L, C, nblk)

    def body(gp_ref, rec_ref, hf_ref, hb_ref, dyg_ref, pv_ref, wa_ref, wx_ref,
             dz_ref, dwa_ref, dwx_ref, dpv_ref, a_scr, u_scr, d_scr, p_scr):
        @pl.when(pl.program_id(1) == 0)
        def _():
            dwa_ref[...] = jnp.zeros_like(dwa_ref)
            dwx_ref[...] = jnp.zeros_like(dwx_ref)
            dpv_ref[...] = jnp.zeros_like(dpv_ref)

        pv = pv_ref[...]
        rec = rec_ref[...]
        c = _conv_taps(rec, pv)
        cbf = c.astype(BF16)
        gp = gp_ref[...]
        gate, th = _gelu(gp)
        dgelu = 0.5 * (1.0 + th) + 0.5 * gp * (1.0 - th * th) * GELU_K * (1.0 + 3.0 * GELU_C * gp * gp)
        dyg = dyg_ref[...]
        dz_ref[0] = (dyg * (hf_ref[...] + hb_ref[...]) * dgelu).astype(BF16)
        dy = dyg * gate

        gates = []
        for d in range(2):
            gates.append(_rg_gates(c, cbf, wa_ref[d], wx_ref[d], pv[PV_B_A + d:PV_B_A + d + 1],
                                   pv[PV_B_X + d:PV_B_X + d + 1], pv[PV_LAM + d:PV_LAM + d + 1]))
        a_scr[0] = _shift_rows(gates[1][2], 1)
        a_scr[1] = _shift_rows(gates[0][2], -1)
        u_scr[...] = dy
        _scan_pair(a_scr.at[0], u_scr, a_scr.at[1], u_scr, d_scr.at[1], d_scr.at[0], p_scr.at[0], p_scr.at[1], L)

        dc = jnp.zeros_like(c)
        rows = []
        for d in range(2):
            r, i, a, m, rm, a2, sp = gates[d]
            delta = d_scr[d]
            hnb = _shift_rows(hf_ref[...], 1) if d == 0 else _shift_rows(hb_ref[...], -1)
            da = delta * hnb
            dm = delta * (i * c)
            di = delta * (m * c)
            dc = dc + delta * (m * i)
            dla = da * a - dm * (a2 * rm)
            dpa = (dla * ((-RG_C) * sp)) * (r * (1.0 - r))
            dpx = di * (i * (1.0 - i))
            dsp = (-RG_C) * jnp.sum(dla * r, axis=0, keepdims=True)
            lam = pv[PV_LAM + d:PV_LAM + d + 1]
            rows.append((jnp.sum(dpa, axis=0, keepdims=True), jnp.sum(dpx, axis=0, keepdims=True),
                         -dsp * _sigmoid(-lam)))
            dpab = dpa.astype(BF16)
            dpxb = dpx.astype(BF16)
            dwa_ref[d] += _dot_tn(cbf, dpab)
            dwx_ref[d] += _dot_tn(cbf, dpxb)
            dc = dc + _dot_nt(dpab, wa_ref[d]) + _dot_nt(dpxb, wx_ref[d])

        drec = jnp.zeros_like(c)
        dcw = []
        for j in range(CONV_W):
            drec = drec + pv[PV_CONV_W + j:PV_CONV_W + j + 1] * _shift_rows(dc, j - 2)
            dcw.append(jnp.sum(dc * _shift_rows(rec, 2 - j), axis=0, keepdims=True))
        dz_ref[1] = drec.astype(BF16)
        for j in range(CONV_W):
            dpv_ref[PV_CONV_W + j:PV_CONV_W + j + 1, :] += dcw[j]
        for d in range(2):
            dpv_ref[PV_B_A + d:PV_B_A + d + 1, :] += rows[d][0]
            dpv_ref[PV_B_X + d:PV_B_X + d + 1, :] += rows[d][1]
            dpv_ref[PV_LAM + d:PV_LAM + d + 1, :] += rows[d][2]
        dpv_ref[PV_CONV_B:PV_CONV_B + 1, :] += jnp.sum(dc, axis=0, keepdims=True)

    return pl.pallas_call(
        body, name=name, grid=(nblk, B),
        in_specs=[slab(0), slab(nblk), slab(0), slab(0), slab(0), pvspec, wspec, wspec],
        out_specs=[pl.BlockSpec((2, L, LRU_BW), lambda cb, b: (0, b, cb)), wspec, wspec, pvspec],
        out_shape=[S((2, T, C), BF16), S((2, nblk, LRU_BW, LRU_BW), F32), S((2, nblk, LRU_BW, LRU_BW), F32),
                   S((PV_ROWS, C), F32)],
        scratch_shapes=[pltpu.VMEM((2, L, LRU_BW), F32), pltpu.VMEM((L, LRU_BW), F32),
                        pltpu.VMEM((2, L, LRU_BW), F32), pltpu.VMEM((2, L, LRU_BW), F32)],
        compiler_params=_cp("parallel", "arbitrary"),
    )(z, z, hf, hb, dyg, pvec, wa, wx)


QKV_NB = 512


def _interleave(a):
    *lead, L, D = a.shape
    return a.reshape(*lead, N_SEG, L // N_SEG, D).swapaxes(-3, -2).reshape(*lead, L, D)


def _deinterleave(a):
    *lead, L, D = a.shape
    return a.reshape(*lead, L // N_SEG, N_SEG, D).swapaxes(-3, -2).reshape(*lead, L, D)


def _local_step(x3, tgt3, w, fetch, send):
    Bl, L, D = x3.shape
    T = Bl * L
    x = _interleave(x3).reshape(T, D)
    tgt = _interleave(tgt3).reshape(T, D)
    gm = [w["g_mix"][i:i + 1] for i in range(2)]
    gl = [w["g_mlp"][i:i + 1] for i in range(2)]

    w0 = fetch(0, ())
    nb_in = w0["w_in"].shape[-1]
    z, h0 = _norm_matmul(x, gm[0], w0["w_in"], "rg_in")
    yg, hf, hb = _rg_fwd(z, w0["pvec"], w["wa"], w["wx"], L, "rg_fwd")
    w1 = fetch(1, (yg,))
    x1 = _matmul_res(yg, w1["w_out"], x, "rg_out")
    w1.update(fetch(4, (x1,)))
    fb = w1["w_up0"].shape[-1]
    x2, a0, hm0 = _mlp_fwd(x1, gl[0], w1["w_up0"], w1["w_down0"], "mlp0_fwd")
    w2 = fetch(2, (x2,))
    qkv, h1 = _norm_matmul(x2, gm[1], w2["w_qkv"], "at_qkv", transposed=True)
    cos, sin = [_interleave(t) for t in _rope_tables(L)]
    qn, kn, vb = _qk_prep(qkv, w["qg"], w["kg"], cos, sin, L, "at_prep")
    o = _attn_fwd(qn, kn, vb, L, "at_fwd")
    x3_ = _matmul_res(o, w2["w_o"], x2, "at_out")
    w3 = fetch(3, (x3_,))
    dx4, a1, hm1, loss, dgf = _mlp_fwd(x3_, gl[1], w3["w_up1"], w3["w_down1"], "mlp1_fwd", head=(tgt, w["g_fin"]))

    dx3, da1, dob1, dgl1 = _mlp_bwd_dx(x3_, dx4, a1, gl[1], w3["w_up1"], w3["w_down1"], "mlp1_bwd_dx")
    dwu1, dwd1 = _mlp_bwd_dw(hm1, da1, a1, dob1, fb, "mlp1_bwd_dw")
    sent = send(3, dict(w_up1=dwu1, w_down1=dwd1))
    do, dx3b = _matmul_nt(dx3, w2["w_o"], "at_out_bwd", BF16, after=sent)
    dwo = _matmul_tn(o, dx3b[None], QKV_NB, "at_out_dw", blocked=False)
    dq, dk, dv = _attn_bwd(qn, kn, vb, do, o, L, "at_bwd")
    dqkv, dqg, dkg = _qk_prep_bwd(qkv, dq, dk, dv, w["qg"], w["kg"], cos, sin, L, "at_prep_bwd")
    dwqkv = _matmul_tn(dqkv, h1[None], QKV_NB, "at_qkv_dw", blocked=False)
    sent = send(2, dict(w_qkv=dwqkv, w_o=dwo))
    dx2, dgm1 = _nt_normbwd(dqkv[None], w2["w_qkv"], x2, gm[1], dx3, "at_qkv_bwd", after=sent, transposed=True)
    dx1, da0, dob0, dgl0 = _mlp_bwd_dx(x1, dx2, a0, gl[0], w1["w_up0"], w1["w_down0"], "mlp0_bwd_dx")
    dwu0, dwd0 = _mlp_bwd_dw(hm0, da0, a0, dob0, fb, "mlp0_bwd_dw")
    sent = send(1, dict(w_up0=dwu0, w_down0=dwd0))
    dyg, dx1b = _matmul_nt(dx1, w1["w_out"], "rg_out_bwd", F32, after=sent)
    dwout = _matmul_tn(yg, dx1b[None], QKV_NB, "rg_out_dw", blocked=False)
    dz, dwa, dwx, dpv = _rg_bwd(z, hf, hb, dyg, w0["pvec"], w["wa"], w["wx"], L, "rg_bwd")
    sent = send(4, dict(w_out=dwout, pvec=dpv, wa=dwa, wx=dwx))
    dwin = _matmul_tn(h0, dz, nb_in, "rg_in_dw", blocked=True, after=sent)
    sent = send(0, dict(w_in=dwin))
    dx0, dgm0 = _nt_normbwd(dz, w0["w_in"], x, gm[0], dx1, "rg_in_bwd", after=sent)
    send(-1, dict(g_mix=[dgm0, dgm1], g_mlp=[dgl0, dgl1], g_fin=dgf, conv_b=dpv[PV_CONV_B:PV_CONV_B + 1], qg=dqg, kg=dkg, loss=loss))
    return _deinterleave(dx0.reshape(Bl, L, D))


MESH = pl.DeviceIdType.MESH
ANY = pl.BlockSpec(memory_space=pl.ANY)
N_PEERS = N_DEV - 1


def _my_place():
    return lax.axis_index("x"), lax.axis_index("y"), lax.axis_index("c")


def _flat(px, py, pc):
    return 4 * px + 2 * py + pc


def _all_gather(shards, name):
    n = len(shards)

    def body(*refs):
        ins, outs = refs[:n], refs[n:2 * n]
        send_sems, recv_sems, local_sems = refs[2 * n:]
        x, y, c = _my_place()
        me, sibling = (x, y, c), (x, y, 1 - c)
        chips = [(1 - x, y), (x, 1 - y), (1 - x, 1 - y)]

        def copy(a, k, block, to, src=None):
            dst = outs[a].at[_flat(*block)]
            return pltpu.make_async_remote_copy(
                src_ref=dst if src is None else src, dst_ref=dst,
                send_sem=send_sems.at[a, k], recv_sem=recv_sems.at[a, k],
                device_id=to, device_id_type=MESH)

        mine = [pltpu.make_async_copy(ins[a], outs[a].at[_flat(*me)], local_sems.at[a]) for a in range(n)]
        for cp in mine:
            cp.start()
        first = []
        for a in range(n):
            first.append(copy(a, 0, me, sibling, src=ins[a]))
            first += [copy(a, 1 + j, me, (*chip, c), src=ins[a]) for j, chip in enumerate(chips)]
        for cp in first:
            cp.start()
        passed = []
        for j, chip in enumerate(chips):
            for a in range(n):
                copy(a, 1 + j, (*chip, c), me).wait_recv()
                fwd = copy(a, 4 + j, (*chip, c), sibling)
                fwd.start()
                passed.append(fwd)
        for a in range(n):
            copy(a, 0, sibling, me).wait_recv()
            for j, chip in enumerate(chips):
                copy(a, 4 + j, (*chip, 1 - c), me).wait_recv()
        for cp in first + passed:
            cp.wait_send()
        for cp in mine:
            cp.wait()

    return pl.pallas_call(
        body, name=name,
        in_specs=[ANY] * n, out_specs=[ANY] * n,
        out_shape=[S((N_DEV,) + s.shape, s.dtype) for s in shards],
        scratch_shapes=[pltpu.SemaphoreType.DMA((n, N_PEERS)), pltpu.SemaphoreType.DMA((n, N_PEERS)),
                        pltpu.SemaphoreType.DMA((n,))],
    )(*shards)


HBM = pl.BlockSpec(memory_space=pltpu.HBM)
SEM = pl.BlockSpec(memory_space=pltpu.SEMAPHORE)
SIDE_EFFECT = pltpu.SideEffectType.DATAFLOW_SIDE_EFFECTING
SEMS_PER_GROUP = 3


NEAR_PEERS = (1, 2, 4, 6)
FAR_CHIPS = (2, 4, 6)


def _exchange_copies(srcs, lands, sems, mode):
    send_sems, recv_sems, local_sems = sems
    scatter = mode == "scatter"
    x, y, c = _my_place()
    me = _flat(x, y, c)
    remote, local = [], []
    for a in range(len(srcs)):
        for r in (NEAR_PEERS if mode == "near" else range(1, N_DEV)):
            peer = (1 - x if r & 4 else x, 1 - y if r & 2 else y, 1 - c if r & 1 else c)
            remote.append(pltpu.make_async_remote_copy(
                src_ref=srcs[a].at[_flat(*peer)] if scatter else srcs[a], dst_ref=lands[a].at[me],
                send_sem=send_sems.at[a * N_PEERS + r - 1], recv_sem=recv_sems.at[a * N_PEERS + r - 1],
                device_id=peer, device_id_type=MESH))
        local.append(pltpu.make_async_copy(srcs[a].at[me] if scatter else srcs[a], lands[a].at[me], local_sems.at[a]))
    return remote, local


def _exchange_start(groups, modes, name):
    sizes = [len(g) for g in groups]
    srcs = [pltpu.with_memory_space_constraint(a, pltpu.HBM) for g in groups for a in g]
    n = len(srcs)
    scatter_of = [m == "scatter" for g, m in zip(groups, modes) for _ in g]
    lands = [pltpu.with_memory_space_constraint(lax.empty(a.shape if sc else (N_DEV,) + a.shape, a.dtype), pltpu.HBM)
             for a, sc in zip(srcs, scatter_of)]
    n_sem = SEMS_PER_GROUP * len(groups)

    def body(*refs):
        src_refs, land_refs, sem_refs, token = refs[:n], refs[n:2 * n], refs[2 * n:2 * n + n_sem], refs[-1]
        off = 0
        for gi, k in enumerate(sizes):
            remote, local = _exchange_copies(src_refs[off:off + k], land_refs[off:off + k],
                                             sem_refs[SEMS_PER_GROUP * gi:SEMS_PER_GROUP * (gi + 1)], modes[gi])
            for cp in local + remote:
                cp.start()
            off += k
        token[...] = jnp.zeros_like(token)

    sem_shapes = []
    for k in sizes:
        sem_shapes += [pltpu.SemaphoreType.DMA((k * N_PEERS,)), pltpu.SemaphoreType.DMA((k * N_PEERS,)),
                       pltpu.SemaphoreType.DMA((k,))]
    outs = pl.pallas_call(
        body, name=name,
        out_shape=sem_shapes + [pltpu.HBM(a.shape, a.dtype) for a in srcs + lands] + [S((8, 128), F32)],
        in_specs=[HBM] * (2 * n),
        out_specs=[SEM] * n_sem + [HBM] * (2 * n) + [pl.BlockSpec(memory_space=pltpu.VMEM)],
        input_output_aliases={i: n_sem + i for i in range(2 * n)},
        compiler_params=pltpu.CompilerParams(has_side_effects=SIDE_EFFECT),
    )(*srcs, *lands)
    sems, thru, token = outs[:n_sem], outs[n_sem:n_sem + 2 * n], outs[-1]
    per_group, off = [], 0
    for gi, k in enumerate(sizes):
        per_group.append((sems[SEMS_PER_GROUP * gi:SEMS_PER_GROUP * (gi + 1)], thru[off:off + k], thru[n + off:n + off + k]))
        off += k
    return per_group, token


def _exchange_wait(group, after, mode, name):
    sems, srcs, lands = group
    k = len(srcs)

    def body(*refs):
        remote, local = _exchange_copies(refs[:k], refs[k:2 * k], refs[2 * k:2 * k + SEMS_PER_GROUP], mode)
        for cp in remote:
            cp.wait_send()
            cp.wait_recv()
        for cp in local:
            cp.wait()

    outs = pl.pallas_call(
        body, name=name,
        out_shape=[pltpu.HBM(a.shape, a.dtype) for a in list(srcs) + list(lands)],
        in_specs=[HBM] * (2 * k) + [SEM] * SEMS_PER_GROUP + [ANY] * len(after),
        out_specs=[HBM] * (2 * k),
        input_output_aliases={i: i for i in range(2 * k)},
        compiler_params=pltpu.CompilerParams(has_side_effects=SIDE_EFFECT),
    )(*srcs, *lands, *sems, *after)
    return outs[k:]


def _forward_copies(lands, sems):
    send_sems, recv_sems = sems
    x, y, c = _my_place()
    mine, theirs = [], []
    for a in range(len(lands)):
        for k, r in enumerate(FAR_CHIPS):
            px, py = (1 - x if r & 4 else x), (1 - y if r & 2 else y)
            for out, core in ((mine, c), (theirs, 1 - c)):
                blk = lands[a].at[_flat(px, py, core)]
                out.append(pltpu.make_async_remote_copy(
                    src_ref=blk, dst_ref=blk, send_sem=send_sems.at[a * len(FAR_CHIPS) + k],
                    recv_sem=recv_sems.at[a * len(FAR_CHIPS) + k], device_id=(x, y, 1 - c), device_id_type=MESH))
    return mine, theirs


def _forward_start(groups, name):
    sizes = [len(g) for g in groups]
    lands = [a for g in groups for a in g]
    n = len(lands)
    n_sem = 2 * len(groups)

    def body(*refs):
        land_refs, sem_refs, token = refs[:n], refs[n:n + n_sem], refs[-1]
        off = 0
        for gi, k in enumerate(sizes):
            mine, _ = _forward_copies(land_refs[off:off + k], sem_refs[2 * gi:2 * gi + 2])
            for cp in mine:
                cp.start()
            off += k
        token[...] = jnp.zeros_like(token)

    sem_shapes = []
    for k in sizes:
        sem_shapes += [pltpu.SemaphoreType.DMA((k * len(FAR_CHIPS),))] * 2
    outs = pl.pallas_call(
        body, name=name,
        out_shape=sem_shapes + [pltpu.HBM(a.shape, a.dtype) for a in lands] + [S((8, 128), F32)],
        in_specs=[HBM] * n,
        out_specs=[SEM] * n_sem + [HBM] * n + [pl.BlockSpec(memory_space=pltpu.VMEM)],
        input_output_aliases={i: n_sem + i for i in range(n)},
        compiler_params=pltpu.CompilerParams(has_side_effects=SIDE_EFFECT),
    )(*lands)
    per_group, off = [], 0
    for gi, k in enumerate(sizes):
        per_group.append((outs[2 * gi:2 * gi + 2], outs[n_sem + off:n_sem + off + k]))
        off += k
    return per_group


def _forward_wait(group, after, name):
    sems, lands = group
    k = len(lands)

    def body(*refs):
        mine, theirs = _forward_copies(refs[:k], refs[k:k + 2])
        for cp in mine:
            cp.wait_send()
        for cp in theirs:
            cp.wait_recv()

    return pl.pallas_call(
        body, name=name,
        out_shape=[pltpu.HBM(a.shape, a.dtype) for a in lands],
        in_specs=[HBM] * k + [SEM] * 2 + [ANY] * len(after),
        out_specs=[HBM] * k,
        input_output_aliases={i: i for i in range(k)},
        compiler_params=pltpu.CompilerParams(has_side_effects=SIDE_EFFECT),
    )(*lands, *sems, *after)


def _row_tile(rows, cols):
    want = max(16, (128 * 1024) // cols)
    if rows <= want:
        return rows
    t = want - want % 16
    while rows % t:
        t -= 16
    return t


def _sum_parts(parts, name):
    P, R, C = parts.shape
    tr = _row_tile(R, C)

    def body(p_ref, o_ref):
        g = p_ref[0].astype(F32)
        for i in range(1, P):
            g = g + p_ref[i].astype(F32)
        o_ref[...] = g

    return pl.pallas_call(
        body, name=name, grid=(R // tr,),
        in_specs=[pl.BlockSpec((P, tr, C), lambda i: (0, i, 0))],
        out_specs=pl.BlockSpec((tr, C), lambda i: (i, 0)),
        out_shape=S((R, C), F32),
        compiler_params=_cp("parallel"),
    )(parts)


def _adamw(parts, w, m, v, name, after=()):
    P, R, C = parts.shape
    tr = _row_tile(R, C)
    c1 = 1.0 - ADAM_B1 ** ADAM_STEP
    c2 = 1.0 - ADAM_B2 ** ADAM_STEP

    def body(p_ref, w_ref, m_ref, v_ref, *rest):
        g_ref, d_ref, mo_ref, vo_ref = rest[len(after):]
        g = p_ref[0].astype(F32)
        for i in range(1, P):
            g = g + p_ref[i].astype(F32)
        mn = ADAM_B1 * m_ref[...] + (1.0 - ADAM_B1) * g
        vn = ADAM_B2 * v_ref[...] + (1.0 - ADAM_B2) * (g * g)
        g_ref[...] = g
        mo_ref[...] = mn
        vo_ref[...] = vn
        d_ref[...] = (-ADAM_LR) * ((mn / c1) / (jnp.sqrt(vn / c2) + ADAM_EPS) + ADAM_WD * w_ref[...])

    blk = pl.BlockSpec((tr, C), lambda i: (i, 0))
    return pl.pallas_call(
        body, name=name, grid=(R // tr,),
        in_specs=[pl.BlockSpec((P, tr, C), lambda i: (0, i, 0)), blk, blk, blk] + [ANY] * len(after),
        out_specs=[blk, blk, blk, blk],
        out_shape=[S((R, C), F32)] * 4,
        compiler_params=_cp("parallel"),
    )(parts, w, m, v, *after)


def _adamw_layer(parts, w3, m3, v3, layer, prev, name, after=()):
    P, R, C = parts.shape
    NL = w3.shape[0]
    tr = _row_tile(R, C)
    c1 = 1.0 - ADAM_B1 ** ADAM_STEP
    c2 = 1.0 - ADAM_B2 ** ADAM_STEP
    n_prev = 0 if prev is None else len(prev)

    def body(p_ref, w_ref, m_ref, v_ref, *rest):
        g_ref, d_ref, mo_ref, vo_ref = rest[n_prev + len(after):]
        g = p_ref[0].astype(F32)
        for i in range(1, P):
            g = g + p_ref[i].astype(F32)
        mn = ADAM_B1 * m_ref[...] + (1.0 - ADAM_B1) * g
        vn = ADAM_B2 * v_ref[...] + (1.0 - ADAM_B2) * (g * g)
        g_ref[...] = g
        mo_ref[...] = mn
        vo_ref[...] = vn
        d_ref[...] = (-ADAM_LR) * ((mn / c1) / (jnp.sqrt(vn / c2) + ADAM_EPS) + ADAM_WD * w_ref[...])

    blk = pl.BlockSpec((None, tr, C), lambda i: (layer, i, 0))
    return pl.pallas_call(
        body, name=name, grid=(R // tr,),
        in_specs=[pl.BlockSpec((P, tr, C), lambda i: (0, i, 0)), blk, blk, blk] + [ANY] * (n_prev + len(after)),
        out_specs=[blk, blk, blk, blk],
        out_shape=[S((NL, R, C), F32)] * 4,
        input_output_aliases={4 + k: k for k in range(n_prev)},
        compiler_params=_cp("parallel"),
    )(parts, w3, m3, v3, *(prev or ()), *after)


VMEM_WHOLE = pl.BlockSpec(memory_space=pltpu.VMEM)


def _pack_vectors(vectors, starts, rows, name):
    def body(*refs):
        o_ref = refs[-1]
        o_ref[...] = jnp.zeros_like(o_ref)
        for v_ref, r0 in zip(refs[:-1], starts):
            for j in range(v_ref.shape[1] // 128):
                o_ref[r0 + j:r0 + j + 1, :] = v_ref[:, j * 128:(j + 1) * 128]

    return pl.pallas_call(body, name=name, in_specs=[VMEM_WHOLE] * len(vectors), out_specs=VMEM_WHOLE,
                          out_shape=S((rows, 128), F32))(*vectors)


def _adamw_vectors(g_pack, params, keep_rows, name, after=()):
    n = len(params)
    P = g_pack.shape[0]
    c1 = 1.0 - ADAM_B1 ** ADAM_STEP
    c2 = 1.0 - ADAM_B2 ** ADAM_STEP

    def body(g_ref, *refs):
        ins, outs = refs[:3 * n], refs[3 * n + len(after):]
        gs = g_ref[0]
        for i in range(1, P):
            gs = gs + g_ref[i]
        for pi, (_, _, _, slots) in enumerate(params):
            w_ref, m_ref, v_ref = ins[3 * pi:3 * pi + 3]
            g_out, d_out, m_out, v_out = outs[4 * pi:4 * pi + 4]
            for idx, row in slots:
                g = gs[row:row + 1, :]
                mn = ADAM_B1 * m_ref[idx] + (1.0 - ADAM_B1) * g
                vn = ADAM_B2 * v_ref[idx] + (1.0 - ADAM_B2) * (g * g)
                g_out[idx] = g
                m_out[idx] = mn
                v_out[idx] = vn
                d_out[idx] = (-ADAM_LR) * ((mn / c1) / (jnp.sqrt(vn / c2) + ADAM_EPS) + ADAM_WD * w_ref[idx])
        outs[-1][...] = jnp.concatenate([gs[r:r + 1, :] for r in keep_rows], axis=0)

    flat = [a for w, m, v, _ in params for a in (w, m, v)]
    out_shape = [S(w.shape, F32) for w, _, _, _ in params for _ in range(4)] + [S((len(keep_rows), 128), F32)]
    outs = pl.pallas_call(
        body, name=name,
        in_specs=[VMEM_WHOLE] * (1 + len(flat)) + [ANY] * len(after),
        out_specs=[VMEM_WHOLE] * len(out_shape), out_shape=out_shape,
    )(g_pack, *flat, *after)
    return [outs[4 * i:4 * i + 4] for i in range(n)], outs[-1]


def _adamw_nd(parts, w, m, v, name, after=()):
    shp = w.shape
    C = shp[-1]
    outs = _adamw(parts.reshape(parts.shape[0], -1, C), w.reshape(-1, C), m.reshape(-1, C), v.reshape(-1, C), name, after)
    return [o.reshape(shp) for o in outs]


TILE_ROWS = 8


REP_SMALL_ROWS = 128
REP_GRAD_STARTS = (0, 8, 16, 24, 32, 40, 48, 56, 64)
REP_SMALL_STARTS = (0, 16, 32, 40, 48, 56)
REP_LOSS_ROW = 64


def _small_pack(cw, ba, bx, lam):
    pad8 = lambda a: jnp.pad(a, ((0, TILE_ROWS - a.shape[0]), (0, 0)))
    return jnp.concatenate([pad8(cw[0, :, 0, :]), pad8(ba[0]), pad8(bx[0]), pad8(lam[0]),
                            jnp.zeros((PV_ROWS - PV_CONV_B, LRU_BW), F32)], axis=0)


def kernel(x, norm_mix_g, norm_mlp_g, rg_w_in, rg_conv_w, rg_conv_b, rg_w_a, rg_b_a, rg_w_x, rg_b_x, rg_lam, rg_w_out, at_w_qkv, at_q_g, at_k_g, at_w_o, mlp_w_up, mlp_w_down, final_g, loss_target, m_norm_mix_g, m_norm_mlp_g, m_rg_w_in, m_rg_conv_w, m_rg_conv_b, m_rg_w_a, m_rg_b_a, m_rg_w_x, m_rg_b_x, m_rg_lam, m_rg_w_out, m_at_w_qkv, m_at_q_g, m_at_k_g, m_at_w_o, m_mlp_w_up, m_mlp_w_down, m_final_g, v_norm_mix_g, v_norm_mlp_g, v_rg_w_in, v_rg_conv_w, v_rg_conv_b, v_rg_w_a, v_rg_b_a, v_rg_w_x, v_rg_b_x, v_rg_lam, v_rg_w_out, v_at_w_qkv, v_at_q_g, v_at_k_g, v_at_w_o, v_mlp_w_up, v_mlp_w_down, v_final_g):
    D = x.shape[-1]
    bf = lambda a: a.astype(BF16)

    sp_w = _small_pack(rg_conv_w, rg_b_a, rg_b_x, rg_lam)
    started, _ = _exchange_start(
        [[bf(rg_w_in[0]), sp_w], [bf(rg_w_out[0])], [bf(mlp_w_up[0]), bf(mlp_w_down[0])],
         [bf(at_w_qkv[0]).T, bf(at_w_o[0])], [bf(mlp_w_up[1]), bf(mlp_w_down[1])]],
        ["near", "near", "near", "gather", "gather"], "gather_start")
    gathers = dict(zip((0, 1, 4, 2, 3), started))
    forwards = {}

    def fetch(stage, after):
        after = tuple(after)
        if stage == 0:
            got = _exchange_wait(gathers[0], after, "near", "gather_wait0")
            g_in, g_sp = _forward_wait(_forward_start([got], "forward_start0")[0], (), "forward_wait0")
            pvec = g_sp.transpose(1, 0, 2).reshape(PV_ROWS, D)
            pvec = jnp.concatenate([pvec[:PV_CONV_B], jnp.broadcast_to(rg_conv_b, (PV_ROWS - PV_CONV_B, D))], axis=0)
            return dict(w_in=g_in, pvec=pvec)
        if stage == 1:
            near = [_exchange_wait(gathers[s], after, "near", "gather_wait%d" % s) for s in (1, 4)]
            f_out, forwards[4] = _forward_start(near, "forward_start1")
            g_out, = _forward_wait(f_out, (), "forward_wait1")
            return dict(w_out=g_out.reshape(D, D))
        if stage == 4:
            g_up0, g_dn0 = _forward_wait(forwards[4], after, "forward_wait4")
            return dict(w_up0=g_up0, w_down0=g_dn0.reshape(-1, D))
        got = _exchange_wait(gathers[stage], after, "gather", "gather_wait%d" % stage)
        if stage == 2:
            return dict(w_qkv=got[0].reshape(-1, QKV_NB, D), w_o=got[1].reshape(D, D))
        return dict(w_up1=got[0], w_down1=got[1].reshape(-1, D))

    scatters = {}

    def send(stage, g):
        if stage == 3:
            arrs = [g["w_up1"], g["w_down1"].reshape(N_DEV, -1, D)]
        elif stage == 2:
            arrs = [g["w_qkv"].reshape(N_DEV, -1, D), g["w_o"].reshape(N_DEV, -1, D)]
        elif stage == 1:
            arrs = [g["w_up0"], g["w_down0"].reshape(N_DEV, -1, D)]
        elif stage == 4:
            arrs = [g["w_out"].reshape(N_DEV, -1, D), g["pvec"].reshape(PV_ROWS, N_DEV, LRU_BW).transpose(1, 0, 2),
                    bf(g["wa"]).reshape(N_DEV, -1, 128), bf(g["wx"]).reshape(N_DEV, -1, 128)]
        elif stage == 0:
            arrs = [g["w_in"]]
        else:
            small = _pack_vectors(g["g_mix"] + g["g_mlp"] + [g["g_fin"], g["conv_b"], g["qg"], g["kg"], g["loss"]],
                                  REP_GRAD_STARTS, REP_SMALL_ROWS, "pack_rep_small")
            arrs = [small.reshape(N_DEV, -1, 128)]
        (group,), token = _exchange_start([arrs], ["scatter"], "scatter_start%d" % (stage % 6))
        scatters[stage] = (group, token)
        return (token,)

    w = dict(g_mix=norm_mix_g, g_mlp=norm_mlp_g, g_fin=final_g[None], qg=at_q_g, kg=at_k_g,
             wa=bf(rg_w_a[0]), wx=bf(rg_w_x[0]))
    grad_x = _local_step(x, loss_target, w, fetch, send)

    res = {}
    r_up1, r_dn1 = _exchange_wait(scatters[3][0], (scatters[-1][1],), "scatter", "scatter_wait3")
    r_out, r_sp, r_wa, r_wx = _exchange_wait(scatters[4][0], (r_up1,), "scatter", "scatter_wait4")
    (rep_gather,), rep_token = _exchange_start(
        [[_sum_parts(r_wa, "reduce_w_a"), _sum_parts(r_wx, "reduce_w_x")]], ["gather"], "rep_gather_start")
    up = _adamw_layer(r_up1, mlp_w_up, m_mlp_w_up, v_mlp_w_up, 1, None, "adam_mlp_w_up1", after=(rep_token,))
    dn = _adamw_layer(r_dn1, mlp_w_down, m_mlp_w_down, v_mlp_w_down, 1, None, "adam_mlp_w_down1")
    r_qkv, r_o = _exchange_wait(scatters[2][0], (dn[0],), "scatter", "scatter_wait2")
    tr = lambda a: a[0].T
    res["at_w_qkv"] = [o.T[None] for o in _adamw_nd(r_qkv, tr(at_w_qkv), tr(m_at_w_qkv), tr(v_at_w_qkv), "adam_at_w_qkv")]
    res["at_w_o"] = _adamw_nd(r_o[:, None], at_w_o, m_at_w_o, v_at_w_o, "adam_at_w_o")
    r_up0, r_dn0 = _exchange_wait(scatters[1][0], (res["at_w_o"][0],), "scatter", "scatter_wait1")
    res["mlp_w_up"] = _adamw_layer(r_up0, mlp_w_up, m_mlp_w_up, v_mlp_w_up, 0, up, "adam_mlp_w_up0")
    res["mlp_w_down"] = _adamw_layer(r_dn0, mlp_w_down, m_mlp_w_down, v_mlp_w_down, 0, dn, "adam_mlp_w_down0")
    r_in, = _exchange_wait(scatters[0][0], (res["mlp_w_down"][0],), "scatter", "scatter_wait0")
    res["rg_w_in"] = _adamw_nd(r_in[:, None], rg_w_in, m_rg_w_in, v_rg_w_in, "adam_rg_w_in")
    res["rg_w_out"] = _adamw_nd(r_out[:, None], rg_w_out, m_rg_w_out, v_rg_w_out, "adam_rg_w_out", (res["rg_w_in"][0],))
    whole, lane = slice(None), slice(0, 1)
    two_rows = lambda r0: [((0, slice(d, d + 1), whole), r0 + d) for d in range(2)]
    (res["rg_conv_w"], res["rg_b_a"], res["rg_b_x"], res["rg_lam"]), _ = _adamw_vectors(
        r_sp, [(rg_conv_w, m_rg_conv_w, v_rg_conv_w, [((0, t, lane, whole), PV_CONV_W + t) for t in range(CONV_W)]),
               (rg_b_a, m_rg_b_a, v_rg_b_a, two_rows(PV_B_A)), (rg_b_x, m_rg_b_x, v_rg_b_x, two_rows(PV_B_X)),
               (rg_lam, m_rg_lam, v_rg_lam, two_rows(PV_LAM))], [0], "adam_small", (res["rg_w_out"][0],))

    r_small, = _exchange_wait(scatters[-1][0], (res["rg_lam"][0],), "scatter", "scatter_wait5")
    small_sum, = _all_gather([_sum_parts(r_small, "reduce_rep_small")], "gather_replicated")
    wa_sum, wx_sum = _exchange_wait(rep_gather, (small_sum,), "gather", "rep_gather_wait")
    rows = lambda a: a.reshape(-1, 128)
    res["rg_w_a"] = [o.reshape(rg_w_a.shape) for o in _adamw(
        wa_sum.reshape(1, -1, 128), rows(rg_w_a), rows(m_rg_w_a), rows(v_rg_w_a), "adam_rg_w_a")]
    res["rg_w_x"] = [o.reshape(rg_w_x.shape) for o in _adamw(
        wx_sum.reshape(1, -1, 128), rows(rg_w_x), rows(m_rg_w_x), rows(v_rg_w_x), "adam_rg_w_x")]
    def vec_slots(a, r0):
        per = a.shape[1] // 128
        return [((slice(l, l + 1), slice(128 * j, 128 * (j + 1))), r0 + l * per + j) for l in range(a.shape[0]) for j in range(per)]

    fin = [final_g[None], m_final_g[None], v_final_g[None]]
    vecs = [(norm_mix_g, m_norm_mix_g, v_norm_mix_g), (norm_mlp_g, m_norm_mlp_g, v_norm_mlp_g), fin,
            (rg_conv_b, m_rg_conv_b, v_rg_conv_b), (at_q_g, m_at_q_g, v_at_q_g), (at_k_g, m_at_k_g, v_at_k_g)]
    outs, kept = _adamw_vectors(
        small_sum.reshape(1, -1, 128),
        [(wv, mv, vv, vec_slots(wv, r0)) for (wv, mv, vv), r0 in zip(vecs, REP_SMALL_STARTS)], [REP_LOSS_ROW], "adam_rep_small")
    for nm, o in zip(["norm_mix_g", "norm_mlp_g", "final_g", "rg_conv_b", "at_q_g", "at_k_g"], outs):
        res[nm] = [a[0] for a in o] if nm == "final_g" else o
    loss = kept[0, 0]

    order = ["norm_mix_g", "norm_mlp_g", "rg_w_in", "rg_conv_w", "rg_conv_b", "rg_w_a", "rg_b_a", "rg_w_x", "rg_b_x",
             "rg_lam", "rg_w_out", "at_w_qkv", "at_q_g", "at_k_g", "at_w_o", "mlp_w_up", "mlp_w_down", "final_g"]
    return (loss, grad_x, *[res[nm][k] for k in range(4) for nm in order])
```

```python
import functools
import math

import jax
import jax.numpy as jnp
from jax import lax
from jax.experimental import pallas as pl
from jax.experimental.pallas import tpu as pltpu

F32 = jnp.float32
BF16 = jnp.bfloat16
S = jax.ShapeDtypeStruct

EPS = 1e-6
HEAD_DIM = 128
N_KV = 2
GRID_W = 64
ROPE_THETA = 10000.0
LRU_BW = 128
RG_C = 8.0
CONV_W = 4
N_DEV = 8
N_SEG = 8
SCAN_UNROLL = 8
TN_STEP_COLS = 512
PROJ_TM = 1024
VMEM_LIMIT_V7X = 56 * 1024 * 1024
SOFTMAX_SCALE = 1.0 / math.sqrt(HEAD_DIM)
GELU_K = math.sqrt(2.0 / math.pi)
GELU_C = 0.044715

ADAM_LR = 0.001
ADAM_B1 = 0.9
ADAM_B2 = 0.999
ADAM_EPS = 1e-08
ADAM_WD = 0.01
ADAM_STEP = 10

NT = (((1,), (1,)), ((), ()))
TN = (((0,), (0,)), ((), ()))


def _cp(*sem):
    return pltpu.CompilerParams(dimension_semantics=sem, vmem_limit_bytes=VMEM_LIMIT_V7X)


def _rms_r(xv):
    return lax.rsqrt(jnp.mean(xv * xv, axis=-1, keepdims=True) + EPS)


def _rms_bwd(dh, xv, g):
    r = _rms_r(xv)
    xh = xv * r
    dg = jnp.sum(dh * xh, axis=0, keepdims=True)
    dxh = dh * g
    dx = r * (dxh - xh * jnp.mean(dxh * xh, axis=-1, keepdims=True))
    return dx, dg


def _dot(a, b):
    return jnp.dot(a, b, preferred_element_type=F32)


def _dot_nt(a, b):
    return lax.dot_general(a, b, NT, preferred_element_type=F32)


def _dot_tn(a, b):
    return lax.dot_general(a, b, TN, preferred_element_type=F32)


def _norm_matmul(x, g, wblk, name, out_dtype=F32, transposed=False):
    T, D = x.shape
    NB, nb = wblk.shape[0], wblk.shape[1 if transposed else 2]
    mm = _dot_nt if transposed else _dot
    tm = min(T, PROJ_TM)

    def body(x_ref, g_ref, w_ref, o_ref, h_ref):
        xv = x_ref[...]
        hb = (xv * _rms_r(xv) * g_ref[...]).astype(BF16)
        h_ref[...] = hb
        for q in range(NB):
            o_ref[:, q * nb:(q + 1) * nb] = mm(hb, w_ref[q]).astype(o_ref.dtype)

    return pl.pallas_call(
        body, name=name, grid=(T // tm,),
        in_specs=[pl.BlockSpec((tm, D), lambda i: (i, 0)),
                  pl.BlockSpec((1, D), lambda i: (0, 0)),
                  pl.BlockSpec(wblk.shape, lambda i: (0, 0, 0))],
        out_specs=[pl.BlockSpec((tm, NB * nb), lambda i: (i, 0)),
                   pl.BlockSpec((tm, D), lambda i: (i, 0))],
        out_shape=[S((T, NB * nb), out_dtype), S((T, D), BF16)],
        compiler_params=_cp("parallel"),
    )(x, g, wblk)


def _matmul_res(a, w, res, name):
    T, K = a.shape
    N = w.shape[1]
    tm = min(T, PROJ_TM)

    def body(a_ref, w_ref, r_ref, o_ref):
        o_ref[...] = r_ref[...] + _dot(a_ref[...], w_ref[...])

    return pl.pallas_call(
        body, name=name, grid=(T // tm,),
        in_specs=[pl.BlockSpec((tm, K), lambda i: (i, 0)),
                  pl.BlockSpec((K, N), lambda i: (0, 0)),
                  pl.BlockSpec((tm, N), lambda i: (i, 0))],
        out_specs=pl.BlockSpec((tm, N), lambda i: (i, 0)),
        out_shape=S((T, N), F32),
        compiler_params=_cp("parallel"),
    )(a, w, res)


def _matmul_nt(a, w, name, out_dtype, after=()):
    T, N = a.shape
    K = w.shape[0]
    tm = min(T, PROJ_TM)

    def body(a_ref, w_ref, *rest):
        o_ref, ab_ref = rest[len(after):]
        ab = a_ref[...].astype(BF16)
        ab_ref[...] = ab
        o_ref[...] = _dot_nt(ab, w_ref[...]).astype(o_ref.dtype)

    return pl.pallas_call(
        body, name=name, grid=(T // tm,),
        in_specs=[pl.BlockSpec((tm, N), lambda i: (i, 0)),
                  pl.BlockSpec((K, N), lambda i: (0, 0))] + [pl.BlockSpec(memory_space=pl.ANY)] * len(after),
        out_specs=[pl.BlockSpec((tm, K), lambda i: (i, 0)),
                   pl.BlockSpec((tm, N), lambda i: (i, 0))],
        out_shape=[S((T, K), out_dtype), S((T, N), BF16)],
        compiler_params=_cp("parallel"),
    )(a, w, *after)


def _matmul_tn(a, b3, nb, name, blocked, after=()):
    T, M = a.shape
    SB, _, N = b3.shape
    per = N // nb
    NB = SB * per
    tk = min(T, 1024)
    nk = T // tk
    jb = max(1, TN_STEP_COLS // nb) if blocked else 1
    assert per % jb == 0
    if blocked:
        out_spec, out_shape = pl.BlockSpec((jb, M, nb), lambda j, k: (j, 0, 0)), S((NB, M, nb), BF16)
    else:
        assert SB == 1
        out_spec, out_shape = pl.BlockSpec((M, nb), lambda j, k: (0, j)), S((M, N), BF16)

    def body(a_ref, b_ref, *rest):
        o_ref, acc_ref = rest[len(after):]
        k = pl.program_id(1)

        @pl.when(k == 0)
        def _():
            acc_ref[...] = jnp.zeros_like(acc_ref)

        av = a_ref[...]
        for q in range(jb):
            acc_ref[q] += _dot_tn(av, b_ref[:, q * nb:(q + 1) * nb])

        @pl.when(k == nk - 1)
        def _():
            if blocked:
                o_ref[...] = acc_ref[...].astype(BF16)
            else:
                o_ref[...] = acc_ref[0].astype(BF16)

    return pl.pallas_call(
        body, name=name, grid=(NB // jb, nk),
        in_specs=[pl.BlockSpec((tk, M), lambda j, k: (k, 0)),
                  pl.BlockSpec((None, tk, jb * nb), lambda j, k: ((j * jb) // per, k, ((j * jb) % per) // jb))]
        + [pl.BlockSpec(memory_space=pl.ANY)] * len(after),
        out_specs=out_spec,
        out_shape=out_shape,
        scratch_shapes=[pltpu.VMEM((jb, M, nb), F32)],
        compiler_params=_cp("parallel", "arbitrary"),
    )(a, b3, *after)


def _nt_normbwd(dz3, wblk, x, g, dres, name, after=(), transposed=False):
    T, D = x.shape
    NB, nb = wblk.shape[0], wblk.shape[1 if transposed else 2]
    mm = _dot if transposed else _dot_nt
    SB, _, N = dz3.shape
    per = N // nb
    tm = min(T, PROJ_TM)

    def body(dz_ref, w_ref, x_ref, g_ref, dr_ref, *rest):
        dx_ref, dg_ref = rest[len(after):]

        @pl.when(pl.program_id(0) == 0)
        def _():
            dg_ref[...] = jnp.zeros_like(dg_ref)

        dh = None
        for q in range(NB):
            cols = slice((q % per) * nb, (q % per + 1) * nb)
            part = mm(dz_ref[q // per, :, cols], w_ref[q])
            dh = part if dh is None else dh + part
        dx, dg = _rms_bwd(dh, x_ref[...], g_ref[...])
        dx_ref[...] = dr_ref[...] + dx
        dg_ref[...] += dg

    return pl.pallas_call(
        body, name=name, grid=(T // tm,),
        in_specs=[pl.BlockSpec((SB, tm, N), lambda i: (0, i, 0)),
                  pl.BlockSpec(wblk.shape, lambda i: (0, 0, 0)),
                  pl.BlockSpec((tm, D), lambda i: (i, 0)),
                  pl.BlockSpec((1, D), lambda i: (0, 0)),
                  pl.BlockSpec((tm, D), lambda i: (i, 0))] + [pl.BlockSpec(memory_space=pl.ANY)] * len(after),
        out_specs=[pl.BlockSpec((tm, D), lambda i: (i, 0)),
                   pl.BlockSpec((1, D), lambda i: (0, 0))],
        out_shape=[S((T, D), F32), S((1, D), F32)],
        compiler_params=_cp("arbitrary"),
    )(dz3, wblk, x, g, dres, *after)


def _loss_head(xv, tv, gv, D):
    err = xv * _rms_r(xv) * gv - tv
    e2 = jnp.sum(jnp.sum(err * err, axis=-1, keepdims=True), axis=0, keepdims=True)
    dx, dg = _rms_bwd(err * (1.0 / D), xv, gv)
    return (0.5 / D) * e2, dx, dg


def _mlp_fwd(x, g, wup, wdown, name, head=None):
    T, D = x.shape
    NB, _, fb = wup.shape
    tm = min(T, 1024)
    n_head = 0 if head is None else 2

    def body(x_ref, g_ref, wu_ref, wd_ref, *rest):
        xo_ref, a_ref, h_ref = rest[n_head:n_head + 3]
        acc_ref = rest[-1]
        i, j = pl.program_id(0), pl.program_id(1)

        @pl.when(j == 0)
        def _():
            xv = x_ref[...]
            h_ref[...] = (xv * _rms_r(xv) * g_ref[...]).astype(BF16)
            acc_ref[...] = xv

        a = _dot(h_ref[...], wu_ref[...])
        a_ref[...] = a.astype(BF16)
        u = jnp.maximum(a, 0.0)
        acc_ref[...] += _dot((u * u).astype(BF16), wd_ref[...])

        if head is None:
            @pl.when(j == NB - 1)
            def _():
                xo_ref[...] = acc_ref[...]
        else:
            t_ref, gf_ref = rest[:2]
            loss_ref, dgf_ref = rest[n_head + 3:n_head + 5]

            @pl.when((i == 0) & (j == 0))
            def _():
                loss_ref[...] = jnp.zeros_like(loss_ref)
                dgf_ref[...] = jnp.zeros_like(dgf_ref)

            @pl.when(j == NB - 1)
            def _():
                e2, dx, dg = _loss_head(acc_ref[...], t_ref[...], gf_ref[...], D)
                xo_ref[...] = dx
                loss_ref[...] += e2
                dgf_ref[...] += dg

    row = pl.BlockSpec((tm, D), lambda i, j: (i, 0))
    vec = pl.BlockSpec((1, D), lambda i, j: (0, 0))
    in_specs = [row, vec, pl.BlockSpec((None, D, fb), lambda i, j: (j, 0, 0)), pl.BlockSpec((fb, D), lambda i, j: (j, 0))]
    out_specs = [row, pl.BlockSpec((tm, fb), lambda i, j: (i, j)), row]
    out_shape = [S((T, D), F32), S((T, NB * fb), BF16), S((T, D), BF16)]
    if head is not None:
        in_specs += [row, vec]
        out_specs += [pl.BlockSpec((1, 128), lambda i, j: (0, 0)), vec]
        out_shape += [S((1, 128), F32), S((1, D), F32)]
    return pl.pallas_call(
        body, name=name, grid=(T // tm, NB),
        in_specs=in_specs, out_specs=out_specs, out_shape=out_shape,
        scratch_shapes=[pltpu.VMEM((tm, D), F32)],
        compiler_params=_cp("parallel" if head is None else "arbitrary", "arbitrary"),
    )(x, g, wup, wdown, *(head or ()))


def _mlp_bwd_dx(x, dout, a, g, wup, wdown, name):
    T, D = x.shape
    NB, _, fb = wup.shape
    tm = min(T, 1024)

    def body(x_ref, do_ref, a_ref, g_ref, wu_ref, wd_ref, dx_ref, da_ref, dob_ref, dg_ref, acc_ref):
        i, j = pl.program_id(0), pl.program_id(1)

        @pl.when(j == 0)
        def _():
            dob_ref[...] = do_ref[...].astype(BF16)
            acc_ref[...] = jnp.zeros_like(acc_ref)

        @pl.when((i == 0) & (j == 0))
        def _():
            dg_ref[...] = jnp.zeros_like(dg_ref)

        du2 = _dot_nt(dob_ref[...], wd_ref[...])
        u = jnp.maximum(a_ref[...].astype(F32), 0.0)
        da = (du2 * (2.0 * u)).astype(BF16)
        da_ref[...] = da
        acc_ref[...] += _dot_nt(da, wu_ref[...])

        @pl.when(j == NB - 1)
        def _():
            dx, dg = _rms_bwd(acc_ref[...], x_ref[...], g_ref[...])
            dx_ref[...] = do_ref[...] + dx
            dg_ref[...] += dg

    return pl.pallas_call(
        body, name=name, grid=(T // tm, NB),
        in_specs=[pl.BlockSpec((tm, D), lambda i, j: (i, 0)),
                  pl.BlockSpec((tm, D), lambda i, j: (i, 0)),
                  pl.BlockSpec((tm, fb), lambda i, j: (i, j)),
                  pl.BlockSpec((1, D), lambda i, j: (0, 0)),
                  pl.BlockSpec((None, D, fb), lambda i, j: (j, 0, 0)),
                  pl.BlockSpec((fb, D), lambda i, j: (j, 0))],
        out_specs=[pl.BlockSpec((tm, D), lambda i, j: (i, 0)),
                   pl.BlockSpec((tm, fb), lambda i, j: (i, j)),
                   pl.BlockSpec((tm, D), lambda i, j: (i, 0)),
                   pl.BlockSpec((1, D), lambda i, j: (0, 0))],
        out_shape=[S((T, D), F32), S((T, NB * fb), BF16), S((T, D), BF16), S((1, D), F32)],
        scratch_shapes=[pltpu.VMEM((tm, D), F32)],
        compiler_params=_cp("arbitrary", "arbitrary"),
    )(x, dout, a, g, wup, wdown)


def _mlp_bwd_dw(h, da, a, dob, fb, name):
    T, D = h.shape
    F = a.shape[1]
    NB = F // fb
    tk = min(T, 1024)
    nk = T // tk

    def body(h_ref, da_ref, a_ref, dob_ref, dwu_ref, dwd_ref, au_ref, ad_ref):
        k = pl.program_id(1)

        @pl.when(k == 0)
        def _():
            au_ref[...] = jnp.zeros_like(au_ref)
            ad_ref[...] = jnp.zeros_like(ad_ref)

        au_ref[...] += _dot_tn(h_ref[...], da_ref[...])
        u = jnp.maximum(a_ref[...].astype(F32), 0.0)
        ad_ref[...] += _dot_tn((u * u).astype(BF16), dob_ref[...])

        @pl.when(k == nk - 1)
        def _():
            dwu_ref[...] = au_ref[...].astype(BF16)
            dwd_ref[...] = ad_ref[...].astype(BF16)

    return pl.pallas_call(
        body, name=name, grid=(NB, nk),
        in_specs=[pl.BlockSpec((tk, D), lambda j, k: (k, 0)),
                  pl.BlockSpec((tk, fb), lambda j, k: (k, j)),
                  pl.BlockSpec((tk, fb), lambda j, k: (k, j)),
                  pl.BlockSpec((tk, D), lambda j, k: (k, 0))],
        out_specs=[pl.BlockSpec((None, D, fb), lambda j, k: (j, 0, 0)),
                   pl.BlockSpec((fb, D), lambda j, k: (j, 0))],
        out_shape=[S((NB, D, fb), BF16), S((F, D), BF16)],
        scratch_shapes=[pltpu.VMEM((D, fb), F32), pltpu.VMEM((fb, D), F32)],
        compiler_params=_cp("parallel", "arbitrary"),
    )(h, da, a, dob)


def _rope_tables(L):
    nf = HEAD_DIM // 4
    t = jnp.arange(L, dtype=jnp.int32)
    row = (t // GRID_W).astype(F32)
    col = (t % GRID_W).astype(F32)
    inv = ROPE_THETA ** (-jnp.arange(nf, dtype=F32) / nf)
    ar = row[:, None] * inv
    ac = col[:, None] * inv
    cos = jnp.concatenate([jnp.cos(ar), jnp.cos(ar), jnp.cos(ac), jnp.cos(ac)], axis=-1)
    sin = jnp.concatenate([-jnp.sin(ar), jnp.sin(ar), -jnp.sin(ac), jnp.sin(ac)], axis=-1)
    return cos, sin


def _swap32(x):
    lane = lax.broadcasted_iota(jnp.int32, x.shape, 1)
    up = pltpu.roll(x, HEAD_DIM - 32, 1)
    down = pltpu.roll(x, 32, 1)
    return jnp.where((lane % 64) < 32, up, down)


def _qk_prep(qkv, qg, kg, cos, sin, L, name):
    T, W = qkv.shape
    nh = W // HEAD_DIM - 2 * N_KV
    tm = min(L, 512)
    lb = L // tm

    def body(qkv_ref, qg_ref, kg_ref, cos_ref, sin_ref, q_ref, k_ref, v_ref):
        c = cos_ref[...]
        s = sin_ref[...]
        for h in range(nh + N_KV):
            xh = qkv_ref[:, h * HEAD_DIM:(h + 1) * HEAD_DIM]
            gv = qg_ref[...] if h < nh else kg_ref[...]
            y = xh * _rms_r(xh) * gv
            y = (y * c + _swap32(y) * s).astype(BF16)
            if h < nh:
                q_ref[:, h * HEAD_DIM:(h + 1) * HEAD_DIM] = y
            else:
                k_ref[:, (h - nh) * HEAD_DIM:(h - nh + 1) * HEAD_DIM] = y
        v_ref[...] = qkv_ref[:, (nh + N_KV) * HEAD_DIM:].astype(BF16)

    return pl.pallas_call(
        body, name=name, grid=(T // tm,),
        in_specs=[pl.BlockSpec((tm, W), lambda i: (i, 0)),
                  pl.BlockSpec((1, HEAD_DIM), lambda i: (0, 0)),
                  pl.BlockSpec((1, HEAD_DIM), lambda i: (0, 0)),
                  pl.BlockSpec((tm, HEAD_DIM), lambda i: (i % lb, 0)),
                  pl.BlockSpec((tm, HEAD_DIM), lambda i: (i % lb, 0))],
        out_specs=[pl.BlockSpec((tm, nh * HEAD_DIM), lambda i: (i, 0)),
                   pl.BlockSpec((tm, N_KV * HEAD_DIM), lambda i: (i, 0)),
                   pl.BlockSpec((tm, N_KV * HEAD_DIM), lambda i: (i, 0))],
        out_shape=[S((T, nh * HEAD_DIM), BF16), S((T, N_KV * HEAD_DIM), BF16), S((T, N_KV * HEAD_DIM), BF16)],
        compiler_params=_cp("parallel"),
    )(qkv, qg, kg, cos, sin)


def _qk_prep_bwd(qkv, dq, dk, dv, qg, kg, cos, sin, L, name):
    T, W = qkv.shape
    nh = W // HEAD_DIM - 2 * N_KV
    tm = min(L, 512)
    lb = L // tm

    def body(qkv_ref, dq_ref, dk_ref, dv_ref, qg_ref, kg_ref, cos_ref, sin_ref, dz_ref, dqg_ref, dkg_ref):
        @pl.when(pl.program_id(0) == 0)
        def _():
            dqg_ref[...] = jnp.zeros_like(dqg_ref)
            dkg_ref[...] = jnp.zeros_like(dkg_ref)

        c = cos_ref[...]
        s = sin_ref[...]
        for h in range(nh + N_KV):
            cols = slice(h * HEAD_DIM, (h + 1) * HEAD_DIM)
            if h < nh:
                dout, gv, dg_ref = dq_ref[:, cols], qg_ref[...], dqg_ref
            else:
                kc = slice((h - nh) * HEAD_DIM, (h - nh + 1) * HEAD_DIM)
                dout, gv, dg_ref = dk_ref[:, kc], kg_ref[...], dkg_ref
            dy = dout * c - _swap32(dout) * s
            dx, dg = _rms_bwd(dy, qkv_ref[:, cols], gv)
            dg_ref[...] += dg
            dz_ref[:, cols] = dx.astype(BF16)
        dz_ref[:, (nh + N_KV) * HEAD_DIM:] = dv_ref[...].astype(BF16)

    return pl.pallas_call(
        body, name=name, grid=(T // tm,),
        in_specs=[pl.BlockSpec((tm, W), lambda i: (i, 0)),
                  pl.BlockSpec((tm, nh * HEAD_DIM), lambda i: (i, 0)),
                  pl.BlockSpec((tm, N_KV * HEAD_DIM), lambda i: (i, 0)),
                  pl.BlockSpec((tm, N_KV * HEAD_DIM), lambda i: (i, 0)),
                  pl.BlockSpec((1, HEAD_DIM), lambda i: (0, 0)),
                  pl.BlockSpec((1, HEAD_DIM), lambda i: (0, 0)),
                  pl.BlockSpec((tm, HEAD_DIM), lambda i: (i % lb, 0)),
                  pl.BlockSpec((tm, HEAD_DIM), lambda i: (i % lb, 0))],
        out_specs=[pl.BlockSpec((tm, W), lambda i: (i, 0)),
                   pl.BlockSpec((1, HEAD_DIM), lambda i: (0, 0)),
                   pl.BlockSpec((1, HEAD_DIM), lambda i: (0, 0))],
        out_shape=[S((T, W), BF16), S((1, HEAD_DIM), F32), S((1, HEAD_DIM), F32)],
        compiler_params=_cp("arbitrary"),
    )(qkv, dq, dk, dv, qg, kg, cos, sin)


EXP2_SCALE = SOFTMAX_SCALE * math.log2(math.e)
ATTN_SUB = 256
ATTN_TQ = 1024


def _softmax_rows(q, k):
    s = _dot_nt(q, k)
    e = jnp.exp2((s - jnp.max(s, axis=-1, keepdims=True)) * EXP2_SCALE)
    return e, jnp.sum(e, axis=-1, keepdims=True)


def _attn_fwd(q, k, v, L, name):
    T = q.shape[0]
    nh = q.shape[1] // HEAD_DIM
    G = nh // N_KV
    B = T // L
    tq = min(L, ATTN_TQ)
    nq = L // tq
    sub = min(tq, ATTN_SUB)

    def body(q_ref, k_ref, v_ref, o_ref):
        for h in range(tq // sub):
            rows = slice(h * sub, (h + 1) * sub)
            e, l = _softmax_rows(q_ref[rows, :], k_ref[...])
            o_ref[rows, :] = (_dot(e.astype(BF16), v_ref[...]) / l).astype(BF16)

    qspec = pl.BlockSpec((tq, HEAD_DIM), lambda b, kv, g, qi: (b * nq + qi, kv * G + g))
    kspec = pl.BlockSpec((L, HEAD_DIM), lambda b, kv, g, qi: (b, kv))
    return pl.pallas_call(
        body, name=name, grid=(B, N_KV, G, nq),
        in_specs=[qspec, kspec, kspec],
        out_specs=qspec,
        out_shape=S((T, nh * HEAD_DIM), BF16),
        compiler_params=_cp("parallel", "parallel", "parallel", "parallel"),
    )(q, k, v)


def _attn_bwd(q, k, v, do, o, L, name):
    T = q.shape[0]
    nh = q.shape[1] // HEAD_DIM
    G = nh // N_KV
    B = T // L
    tq = min(L, ATTN_TQ)
    nq = L // tq

    sub = min(tq, ATTN_SUB)

    def body(q_ref, k_ref, v_ref, do_ref, o_ref, dq_ref, dk_ref, dv_ref, ds_scr, p_scr):
        first = (pl.program_id(2) == 0) & (pl.program_id(3) == 0)
        last = (pl.program_id(2) == G - 1) & (pl.program_id(3) == nq - 1)

        @pl.when(first)
        def _():
            dk_ref[...] = jnp.zeros_like(dk_ref)
            dv_ref[...] = jnp.zeros_like(dv_ref)

        for h in range(tq // sub):
            rows = slice(h * sub, (h + 1) * sub)
            dov = do_ref[rows, :]
            e, l = _softmax_rows(q_ref[rows, :], k_ref[...])
            p = e * (1.0 / l)
            dsum = jnp.sum(dov.astype(F32) * o_ref[rows, :].astype(F32), axis=-1, keepdims=True)
            ds_scr[rows, :] = (p * (_dot_nt(dov, v_ref[...]) - dsum)).astype(BF16)
            p_scr[rows, :] = p.astype(BF16)
        ds = ds_scr[...]
        dq_ref[...] = _dot(ds, k_ref[...]) * SOFTMAX_SCALE
        dk_ref[...] += _dot_tn(ds, q_ref[...])
        dv_ref[...] += _dot_tn(p_scr[...], do_ref[...])

        @pl.when(last)
        def _():
            dk_ref[...] = dk_ref[...] * SOFTMAX_SCALE

    qspec = pl.BlockSpec((tq, HEAD_DIM), lambda b, kv, g, qi: (b * nq + qi, kv * G + g))
    kspec = pl.BlockSpec((L, HEAD_DIM), lambda b, kv, g, qi: (b, kv))
    return pl.pallas_call(
        body, name=name, grid=(B, N_KV, G, nq),
        in_specs=[qspec, kspec, kspec, qspec, qspec],
        out_specs=[qspec, kspec, kspec],
        out_shape=[S((T, nh * HEAD_DIM), F32), S((T, N_KV * HEAD_DIM), F32), S((T, N_KV * HEAD_DIM), F32)],
        scratch_shapes=[pltpu.VMEM((tq, L), BF16), pltpu.VMEM((tq, L), BF16)],
        compiler_params=_cp("parallel", "parallel", "arbitrary", "arbitrary"),
    )(q, k, v, do, o)


PV_CONV_W = 0
PV_B_A = 8
PV_B_X = 16
PV_LAM = 24
PV_CONV_B = 32
PV_ROWS = 40


def _shift_rows(x, k):
    if k == 0:
        return x
    L = x.shape[0]
    n = N_SEG * abs(k)
    seg = lax.broadcasted_iota(jnp.int32, (n, x.shape[1]), 0) % N_SEG
    if k > 0:
        edge = jnp.where(seg == 0, 0.0, pltpu.roll(x[L - n:], 1, 0))
        return jnp.concatenate([edge, x[:L - n]], axis=0)
    edge = jnp.where(seg == N_SEG - 1, 0.0, pltpu.roll(x[:n], n - 1, 0))
    return jnp.concatenate([x[n:], edge], axis=0)


def _conv_taps(rec, pv):
    c = pv[PV_CONV_B:PV_CONV_B + 1]
    for j in range(CONV_W):
        c = c + pv[PV_CONV_W + j:PV_CONV_W + j + 1] * _shift_rows(rec, 2 - j)
    return c


def _sigmoid(x):
    return 0.5 * jnp.tanh(0.5 * x) + 0.5


EXPM1_SERIES_BELOW = 0.03


def _rg_gates(c, cbf, wa, wx, ba, bx, lam):
    r = _sigmoid(_dot(cbf, wa) + ba)
    i = _sigmoid(_dot(cbf, wx) + bx)
    sp = jnp.maximum(-lam, 0.0) + jnp.log1p(jnp.exp(-jnp.abs(lam)))
    la = r * ((-RG_C) * sp)
    a = jnp.exp(la)
    a2 = a * a
    x = la + la
    series = -(x * ((x * (1.0 / 6.0) + 0.5) * x + 1.0))
    om = jnp.where(x > -EXPM1_SERIES_BELOW, series, 1.0 - a2)
    rm = lax.rsqrt(om)
    return r, i, a, om * rm, rm, a2, sp


def _gelu(x):
    t = jnp.tanh(GELU_K * (x + GELU_C * x * x * x))
    return 0.5 * x * (1.0 + t), t


def _scan_pair(af_ref, uf_ref, ab_ref, ub_ref, hf_ref, hb_ref, pf_ref, pb_ref, L):
    ls = L // N_SEG
    zero = jnp.zeros((N_SEG, LRU_BW), F32)
    one = jnp.ones((N_SEG, LRU_BW), F32)
    tile = lambda t: pl.ds(pl.multiple_of(t * N_SEG, N_SEG), N_SEG)

    def steps(tc, carry):
        hf, pf, hb, pb = carry
        for q in range(SCAN_UNROLL):
            t = tc * SCAN_UNROLL + q
            rf, rb = tile(t), tile(ls - 1 - t)
            af = af_ref[rf, :]
            hf = af * hf + uf_ref[rf, :]
            pf = pf * af
            hf_ref[rf, :] = hf
            pf_ref[rf, :] = pf
            ab = ab_ref[rb, :]
            hb = ab * hb + ub_ref[rb, :]
            pb = pb * ab
            hb_ref[rb, :] = hb
            pb_ref[rb, :] = pb
        return hf, pf, hb, pb

    hf_e, pf_e, hb_e, pb_e = lax.fori_loop(0, ls // SCAN_UNROLL, steps, (zero, one, zero, one))

    rows, cin = [], jnp.zeros((1, LRU_BW), F32)
    for s in range(N_SEG):
        rows.append(cin)
        cin = hf_e[s:s + 1] + pf_e[s:s + 1] * cin
    cf = jnp.concatenate(rows, axis=0)
    rows, cin = [], jnp.zeros((1, LRU_BW), F32)
    for s in reversed(range(N_SEG)):
        rows.append(cin)
        cin = hb_e[s:s + 1] + pb_e[s:s + 1] * cin
    cb = jnp.concatenate(rows[::-1], axis=0)

    def fix(tc, _):
        for q in range(SCAN_UNROLL):
            r = tile(tc * SCAN_UNROLL + q)
            hf_ref[r, :] = hf_ref[r, :] + pf_ref[r, :] * cf
            hb_ref[r, :] = hb_ref[r, :] + pb_ref[r, :] * cb
        return 0

    lax.fori_loop(0, ls // SCAN_UNROLL, fix, 0)


def _rg_specs(L, D, nblk):
    slab = lambda off: pl.BlockSpec((L, LRU_BW), lambda cb, b: (b, off + cb))
    wspec = pl.BlockSpec((2, None, LRU_BW, LRU_BW), lambda cb, b: (0, cb, 0, 0))
    pvspec = pl.BlockSpec((PV_ROWS, LRU_BW), lambda cb, b: (0, cb))
    return slab, wspec, pvspec


def _rg_fwd(z, pvec, wa, wx, L, name):
    T, C2 = z.shape
    C = C2 // 2
    nblk = C // LRU_BW
    B = T // L
    slab, wspec, pvspec = _rg_specs(L, C, nblk)

    def body(gp_ref, rec_ref, pv_ref, wa_ref, wx_ref, yg_ref, hf_ref, hb_ref, a_scr, u_scr, p_scr):
        pv = pv_ref[...]
        c = _conv_taps(rec_ref[...], pv)
        cbf = c.astype(BF16)
        for d in range(2):
            _, i, a, m, _, _, _ = _rg_gates(c, cbf, wa_ref[d], wx_ref[d], pv[PV_B_A + d:PV_B_A + d + 1],
                                      pv[PV_B_X + d:PV_B_X + d + 1], pv[PV_LAM + d:PV_LAM + d + 1])
            a_scr[d] = a
            u_scr[d] = m * (i * c)
        _scan_pair(a_scr.at[0], u_scr.at[0], a_scr.at[1], u_scr.at[1], hf_ref, hb_ref, p_scr.at[0], p_scr.at[1], L)
        gate, _ = _gelu(gp_ref[...])
        yg_ref[...] = ((hf_ref[...] + hb_ref[...]) * gate).astype(BF16)

    return pl.pallas_call(
        body, name=name, grid=(nblk, B),
        in_specs=[slab(0), slab(nblk), pvspec, wspec, wspec],
        out_specs=[slab(0), slab(0), slab(0)],
        out_shape=[S((T, C), BF16), S((T, C), F32), S((T, C), F32)],
        scratch_shapes=[pltpu.VMEM((2, L, LRU_BW), F32)] * 3,
        compiler_params=_cp("parallel", "parallel"),
    )(z, z, pvec, wa, wx)


def _rg_bwd(z, hf, hb, dyg, pvec, wa, wx, L, name):
    T, C2 = z.shape
    C = C2 // 2
    nblk = C // LRU_BW
    B = T // L
    slab, wspec, pvspec = _rg_specs(L, C, nblk)

    def body(gp_ref, rec_ref, hf_ref, hb_ref, dyg_ref, pv_ref, wa_ref, wx_ref,
             dz_ref, dwa_ref, dwx_ref, dpv_ref, a_scr, u_scr, d_scr, p_scr):
        @pl.when(pl.program_id(1) == 0)
        def _():
            dwa_ref[...] = jnp.zeros_like(dwa_ref)
            dwx_ref[...] = jnp.zeros_like(dwx_ref)
            dpv_ref[...] = jnp.zeros_like(dpv_ref)

        pv = pv_ref[...]
        rec = rec_ref[...]
        c = _conv_taps(rec, pv)
        cbf = c.astype(BF16)
        gp = gp_ref[...]
        gate, th = _gelu(gp)
        dgelu = 0.5 * (1.0 + th) + 0.5 * gp * (1.0 - th * th) * GELU_K * (1.0 + 3.0 * GELU_C * gp * gp)
        dyg = dyg_ref[...]
        dz_ref[0] = (dyg * (hf_ref[...] + hb_ref[...]) * dgelu).astype(BF16)
        dy = dyg * gate

        gates = []
        for d in range(2):
            gates.append(_rg_gates(c, cbf, wa_ref[d], wx_ref[d], pv[PV_B_A + d:PV_B_A + d + 1],
                                   pv[PV_B_X + d:PV_B_X + d + 1], pv[PV_LAM + d:PV_LAM + d + 1]))
        a_scr[0] = _shift_rows(gates[1][2], 1)
        a_scr[1] = _shift_rows(gates[0][2], -1)
        u_scr[...] = dy
        _scan_pair(a_scr.at[0], u_scr, a_scr.at[1], u_scr, d_scr.at[1], d_scr.at[0], p_scr.at[0], p_scr.at[1], L)

        dc = jnp.zeros_like(c)
        rows = []
        for d in range(2):
            r, i, a, m, rm, a2, sp = gates[d]
            delta = d_scr[d]
            hnb = _shift_rows(hf_ref[...], 1) if d == 0 else _shift_rows(hb_ref[...], -1)
            da = delta * hnb
            dm = delta * (i * c)
            di = delta * (m * c)
            dc = dc + delta * (m * i)
            dla = da * a - dm * (a2 * rm)
            dpa = (dla * ((-RG_C) * sp)) * (r * (1.0 - r))
            dpx = di * (i * (1.0 - i))
            dsp = (-RG_C) * jnp.sum(dla * r, axis=0, keepdims=True)
            lam = pv[PV_LAM + d:PV_LAM + d + 1]
            rows.append((jnp.sum(dpa, axis=0, keepdims=True), jnp.sum(dpx, axis=0, keepdims=True),
                         -dsp * _sigmoid(-lam)))
            dpab = dpa.astype(BF16)
            dpxb = dpx.astype(BF16)
            dwa_ref[d] += _dot_tn(cbf, dpab)
            dwx_ref[d] += _dot_tn(cbf, dpxb)
            dc = dc + _dot_nt(dpab, wa_ref[d]) + _dot_nt(dpxb, wx_ref[d])

        drec = jnp.zeros_like(c)
        dcw = []
        for j in range(CONV_W):
            drec = drec + pv[PV_CONV_W + j:PV_CONV_W + j + 1] * _shift_rows(dc, j - 2)
            dcw.append(jnp.sum(dc * _shift_rows(rec, 2 - j), axis=0, keepdims=True))
        dz_ref[1] = drec.astype(BF16)
        for j in range(CONV_W):
            dpv_ref[PV_CONV_W + j:PV_CONV_W + j + 1, :] += dcw[j]
        for d in range(2):
            dpv_ref[PV_B_A + d:PV_B_A + d + 1, :] += rows[d][0]
            dpv_ref[PV_B_X + d:PV_B_X + d + 1, :] += rows[d][1]
            dpv_ref[PV_LAM + d:PV_LAM + d + 1, :] += rows[d][2]
        dpv_ref[PV_CONV_B:PV_CONV_B + 1, :] += jnp.sum(dc, axis=0, keepdims=True)

    return pl.pallas_call(
        body, name=name, grid=(nblk, B),
        in_specs=[slab(0), slab(nblk), slab(0), slab(0), slab(0), pvspec, wspec, wspec],
        out_specs=[pl.BlockSpec((2, L, LRU_BW), lambda cb, b: (0, b, cb)), wspec, wspec, pvspec],
        out_shape=[S((2, T, C), BF16), S((2, nblk, LRU_BW, LRU_BW), F32), S((2, nblk, LRU_BW, LRU_BW), F32),
                   S((PV_ROWS, C), F32)],
        scratch_shapes=[pltpu.VMEM((2, L, LRU_BW), F32), pltpu.VMEM((L, LRU_BW), F32),
                        pltpu.VMEM((2, L, LRU_BW), F32), pltpu.VMEM((2, L, LRU_BW), F32)],
        compiler_params=_cp("parallel", "arbitrary"),
    )(z, z, hf, hb, dyg, pvec, wa, wx)


QKV_NB = 512


def _interleave(a):
    *lead, L, D = a.shape
    return a.reshape(*lead, N_SEG, L // N_SEG, D).swapaxes(-3, -2).reshape(*lead, L, D)


def _deinterleave(a):
    *lead, L, D = a.shape
    return a.reshape(*lead, L // N_SEG, N_SEG, D).swapaxes(-3, -2).reshape(*lead, L, D)


def _local_step(x3, tgt3, w, fetch, send):
    Bl, L, D = x3.shape
    T = Bl * L
    x = _interleave(x3).reshape(T, D)
    tgt = _interleave(tgt3).reshape(T, D)
    gm = [w["g_mix"][i:i + 1] for i in range(2)]
    gl = [w["g_mlp"][i:i + 1] for i in range(2)]

    w0 = fetch(0, ())
    nb_in = w0["w_in"].shape[-1]
    z, h0 = _norm_matmul(x, gm[0], w0["w_in"], "rg_in")
    yg, hf, hb = _rg_fwd(z, w0["pvec"], w["wa"], w["wx"], L, "rg_fwd")
    w1 = fetch(1, (yg,))
    x1 = _matmul_res(yg, w1["w_out"], x, "rg_out")
    w1.update(fetch(4, (x1,)))
    fb = w1["w_up0"].shape[-1]
    x2, a0, hm0 = _mlp_fwd(x1, gl[0], w1["w_up0"], w1["w_down0"], "mlp0_fwd")
    w2 = fetch(2, (x2,))
    qkv, h1 = _norm_matmul(x2, gm[1], w2["w_qkv"], "at_qkv", transposed=True)
    cos, sin = [_interleave(t) for t in _rope_tables(L)]
    qn, kn, vb = _qk_prep(qkv, w["qg"], w["kg"], cos, sin, L, "at_prep")
    o = _attn_fwd(qn, kn, vb, L, "at_fwd")
    x3_ = _matmul_res(o, w2["w_o"], x2, "at_out")
    w3 = fetch(3, (x3_,))
    dx4, a1, hm1, loss, dgf = _mlp_fwd(x3_, gl[1], w3["w_up1"], w3["w_down1"], "mlp1_fwd", head=(tgt, w["g_fin"]))

    dx3, da1, dob1, dgl1 = _mlp_bwd_dx(x3_, dx4, a1, gl[1], w3["w_up1"], w3["w_down1"], "mlp1_bwd_dx")
    dwu1, dwd1 = _mlp_bwd_dw(hm1, da1, a1, dob1, fb, "mlp1_bwd_dw")
    sent = send(3, dict(w_up1=dwu1, w_down1=dwd1))
    do, dx3b = _matmul_nt(dx3, w2["w_o"], "at_out_bwd", BF16, after=sent)
    dwo = _matmul_tn(o, dx3b[None], QKV_NB, "at_out_dw", blocked=False)
    dq, dk, dv = _attn_bwd(qn, kn, vb, do, o, L, "at_bwd")
    dqkv, dqg, dkg = _qk_prep_bwd(qkv, dq, dk, dv, w["qg"], w["kg"], cos, sin, L, "at_prep_bwd")
    dwqkv = _matmul_tn(dqkv, h1[None], QKV_NB, "at_qkv_dw", blocked=False)
    sent = send(2, dict(w_qkv=dwqkv, w_o=dwo))
    dx2, dgm1 = _nt_normbwd(dqkv[None], w2["w_qkv"], x2, gm[1], dx3, "at_qkv_bwd", after=sent, transposed=True)
    dx1, da0, dob0, dgl0 = _mlp_bwd_dx(x1, dx2, a0, gl[0], w1["w_up0"], w1["w_down0"], "mlp0_bwd_dx")
    dwu0, dwd0 = _mlp_bwd_dw(hm0, da0, a0, dob0, fb, "mlp0_bwd_dw")
    sent = send(1, dict(w_up0=dwu0, w_down0=dwd0))
    dyg, dx1b = _matmul_nt(dx1, w1["w_out"], "rg_out_bwd", F32, after=sent)
    dwout = _matmul_tn(yg, dx1b[None], QKV_NB, "rg_out_dw", blocked=False)
    dz, dwa, dwx, dpv = _rg_bwd(z, hf, hb, dyg, w0["pvec"], w["wa"], w["wx"], L, "rg_bwd")
    sent = send(4, dict(w_out=dwout, pvec=dpv, wa=dwa, wx=dwx))
    dwin = _matmul_tn(h0, dz, nb_in, "rg_in_dw", blocked=True, after=sent)
    sent = send(0, dict(w_in=dwin))
    dx0, dgm0 = _nt_normbwd(dz, w0["w_in"], x, gm[0], dx1, "rg_in_bwd", after=sent)
    send(-1, dict(g_mix=[dgm0, dgm1], g_mlp=[dgl0, dgl1], g_fin=dgf, conv_b=dpv[PV_CONV_B:PV_CONV_B + 1], qg=dqg, kg=dkg, loss=loss))
    return _deinterleave(dx0.reshape(Bl, L, D))


MESH = pl.DeviceIdType.MESH
ANY = pl.BlockSpec(memory_space=pl.ANY)
N_PEERS = N_DEV - 1


def _my_place():
    return lax.axis_index("x"), lax.axis_index("y"), lax.axis_index("c")


def _flat(px, py, pc):
    return 4 * px + 2 * py + pc


def _all_gather(shards, name):
    n = len(shards)

    def body(*refs):
        ins, outs = refs[:n], refs[n:2 * n]
        send_sems, recv_sems, local_sems = refs[2 * n:]
        x, y, c = _my_place()
        me, sibling = (x, y, c), (x, y, 1 - c)
        chips = [(1 - x, y), (x, 1 - y), (1 - x, 1 - y)]

        def copy(a, k, block, to, src=None):
            dst = outs[a].at[_flat(*block)]
            return pltpu.make_async_remote_copy(
                src_ref=dst if src is None else src, dst_ref=dst,
                send_sem=send_sems.at[a, k], recv_sem=recv_sems.at[a, k],
                device_id=to, device_id_type=MESH)

        mine = [pltpu.make_async_copy(ins[a], outs[a].at[_flat(*me)], local_sems.at[a]) for a in range(n)]
        for cp in mine:
            cp.start()
        first = []
        for a in range(n):
            first.append(copy(a, 0, me, sibling, src=ins[a]))
            first += [copy(a, 1 + j, me, (*chip, c), src=ins[a]) for j, chip in enumerate(chips)]
        for cp in first:
            cp.start()
        passed = []
        for j, chip in enumerate(chips):
            for a in range(n):
                copy(a, 1 + j, (*chip, c), me).wait_recv()
                fwd = copy(a, 4 + j, (*chip, c), sibling)
                fwd.start()
                passed.append(fwd)
        for a in range(n):
            copy(a, 0, sibling, me).wait_recv()
            for j, chip in enumerate(chips):
                copy(a, 4 + j, (*chip, 1 - c), me).wait_recv()
        for cp in first + passed:
            cp.wait_send()
        for cp in mine:
            cp.wait()

    return pl.pallas_call(
        body, name=name,
        in_specs=[ANY] * n, out_specs=[ANY] * n,
        out_shape=[S((N_DEV,) + s.shape, s.dtype) for s in shards],
        scratch_shapes=[pltpu.SemaphoreType.DMA((n, N_PEERS)), pltpu.SemaphoreType.DMA((n, N_PEERS)),
                        pltpu.SemaphoreType.DMA((n,))],
    )(*shards)


HBM = pl.BlockSpec(memory_space=pltpu.HBM)
SEM = pl.BlockSpec(memory_space=pltpu.SEMAPHORE)
SIDE_EFFECT = pltpu.SideEffectType.DATAFLOW_SIDE_EFFECTING
SEMS_PER_GROUP = 3


NEAR_PEERS = (1, 2, 4, 6)
FAR_CHIPS = (2, 4, 6)


def _exchange_copies(srcs, lands, sems, mode):
    send_sems, recv_sems, local_sems = sems
    scatter = mode == "scatter"
    x, y, c = _my_place()
    me = _flat(x, y, c)
    remote, local = [], []
    for a in range(len(srcs)):
        for r in (NEAR_PEERS if mode == "near" else range(1, N_DEV)):
            peer = (1 - x if r & 4 else x, 1 - y if r & 2 else y, 1 - c if r & 1 else c)
            remote.append(pltpu.make_async_remote_copy(
                src_ref=srcs[a].at[_flat(*peer)] if scatter else srcs[a], dst_ref=lands[a].at[me],
                send_sem=send_sems.at[a * N_PEERS + r - 1], recv_sem=recv_sems.at[a * N_PEERS + r - 1],
                device_id=peer, device_id_type=MESH))
        local.append(pltpu.make_async_copy(srcs[a].at[me] if scatter else srcs[a], lands[a].at[me], local_sems.at[a]))
    return remote, local


def _exchange_start(groups, modes, name):
    sizes = [len(g) for g in groups]
    srcs = [pltpu.with_memory_space_constraint(a, pltpu.HBM) for g in groups for a in g]
    n = len(srcs)
    scatter_of = [m == "scatter" for g, m in zip(groups, modes) for _ in g]
    lands = [pltpu.with_memory_space_constraint(lax.empty(a.shape if sc else (N_DEV,) + a.shape, a.dtype), pltpu.HBM)
             for a, sc in zip(srcs, scatter_of)]
    n_sem = SEMS_PER_GROUP * len(groups)

    def body(*refs):
        src_refs, land_refs, sem_refs, token = refs[:n], refs[n:2 * n], refs[2 * n:2 * n + n_sem], refs[-1]
        off = 0
        for gi, k in enumerate(sizes):
            remote, local = _exchange_copies(src_refs[off:off + k], land_refs[off:off + k],
                                             sem_refs[SEMS_PER_GROUP * gi:SEMS_PER_GROUP * (gi + 1)], modes[gi])
            for cp in local + remote:
                cp.start()
            off += k
        token[...] = jnp.zeros_like(token)

    sem_shapes = []
    for k in sizes:
        sem_shapes += [pltpu.SemaphoreType.DMA((k * N_PEERS,)), pltpu.SemaphoreType.DMA((k * N_PEERS,)),
                       pltpu.SemaphoreType.DMA((k,))]
    outs = pl.pallas_call(
        body, name=name,
        out_shape=sem_shapes + [pltpu.HBM(a.shape, a.dtype) for a in srcs + lands] + [S((8, 128), F32)],
        in_specs=[HBM] * (2 * n),
        out_specs=[SEM] * n_sem + [HBM] * (2 * n) + [pl.BlockSpec(memory_space=pltpu.VMEM)],
        input_output_aliases={i: n_sem + i for i in range(2 * n)},
        compiler_params=pltpu.CompilerParams(has_side_effects=SIDE_EFFECT),
    )(*srcs, *lands)
    sems, thru, token = outs[:n_sem], outs[n_sem:n_sem + 2 * n], outs[-1]
    per_group, off = [], 0
    for gi, k in enumerate(sizes):
        per_group.append((sems[SEMS_PER_GROUP * gi:SEMS_PER_GROUP * (gi + 1)], thru[off:off + k], thru[n + off:n + off + k]))
        off += k
    return per_group, token


def _exchange_wait(group, after, mode, name):
    sems, srcs, lands = group
    k = len(srcs)

    def body(*refs):
        remote, local = _exchange_copies(refs[:k], refs[k:2 * k], refs[2 * k:2 * k + SEMS_PER_GROUP], mode)
        for cp in remote:
            cp.wait_send()
            cp.wait_recv()
        for cp in local:
            cp.wait()

    outs = pl.pallas_call(
        body, name=name,
        out_shape=[pltpu.HBM(a.shape, a.dtype) for a in list(srcs) + list(lands)],
        in_specs=[HBM] * (2 * k) + [SEM] * SEMS_PER_GROUP + [ANY] * len(after),
        out_specs=[HBM] * (2 * k),
        input_output_aliases={i: i for i in range(2 * k)},
        compiler_params=pltpu.CompilerParams(has_side_effects=SIDE_EFFECT),
    )(*srcs, *lands, *sems, *after)
    return outs[k:]


def _forward_copies(lands, sems):
    send_sems, recv_sems = sems
    x, y, c = _my_place()
    mine, theirs = [], []
    for a in range(len(lands)):
        for k, r in enumerate(FAR_CHIPS):
            px, py = (1 - x if r & 4 else x), (1 - y if r & 2 else y)
            for out, core in ((mine, c), (theirs, 1 - c)):
                blk = lands[a].at[_flat(px, py, core)]
                out.append(pltpu.make_async_remote_copy(
                    src_ref=blk, dst_ref=blk, send_sem=send_sems.at[a * len(FAR_CHIPS) + k],
                    recv_sem=recv_sems.at[a * len(FAR_CHIPS) + k], device_id=(x, y, 1 - c), device_id_type=MESH))
    return mine, theirs


def _forward_start(groups, name):
    sizes = [len(g) for g in groups]
    lands = [a for g in groups for a in g]
    n = len(lands)
    n_sem = 2 * len(groups)

    def body(*refs):
        land_refs, sem_refs, token = refs[:n], refs[n:n + n_sem], refs[-1]
        off = 0
        for gi, k in enumerate(sizes):
            mine, _ = _forward_copies(land_refs[off:off + k], sem_refs[2 * gi:2 * gi + 2])
            for cp in mine:
                cp.start()
            off += k
        token[...] = jnp.zeros_like(token)

    sem_shapes = []
    for k in sizes:
        sem_shapes += [pltpu.SemaphoreType.DMA((k * len(FAR_CHIPS),))] * 2
    outs = pl.pallas_call(
        body, name=name,
        out_shape=sem_shapes + [pltpu.HBM(a.shape, a.dtype) for a in lands] + [S((8, 128), F32)],
        in_specs=[HBM] * n,
        out_specs=[SEM] * n_sem + [HBM] * n + [pl.BlockSpec(memory_space=pltpu.VMEM)],
        input_output_aliases={i: n_sem + i for i in range(n)},
        compiler_params=pltpu.CompilerParams(has_side_effects=SIDE_EFFECT),
    )(*lands)
    per_group, off = [], 0
    for gi, k in enumerate(sizes):
        per_group.append((outs[2 * gi:2 * gi + 2], outs[n_sem + off:n_sem + off + k]))
        off += k
    return per_group


def _forward_wait(group, after, name):
    sems, lands = group
    k = len(lands)

    def body(*refs):
        mine, theirs = _forward_copies(refs[:k], refs[k:k + 2])
        for cp in mine:
            cp.wait_send()
        for cp in theirs:
            cp.wait_recv()

    return pl.pallas_call(
        body, name=name,
        out_shape=[pltpu.HBM(a.shape, a.dtype) for a in lands],
        in_specs=[HBM] * k + [SEM] * 2 + [ANY] * len(after),
        out_specs=[HBM] * k,
        input_output_aliases={i: i for i in range(k)},
        compiler_params=pltpu.CompilerParams(has_side_effects=SIDE_EFFECT),
    )(*lands, *sems, *after)


def _row_tile(rows, cols):
    want = max(16, (128 * 1024) // cols)
    if rows <= want:
        return rows
    t = want - want % 16
    while rows % t:
        t -= 16
    return t


def _sum_parts(parts, name, after=()):
    P, R, C = parts.shape
    tr = _row_tile(R, C)

    def body(p_ref, *rest):
        o_ref = rest[-1]
        g = p_ref[0].astype(F32)
        for i in range(1, P):
            g = g + p_ref[i].astype(F32)
        o_ref[...] = g

    return pl.pallas_call(
        body, name=name, grid=(R // tr,),
        in_specs=[pl.BlockSpec((P, tr, C), lambda i: (0, i, 0))] + [ANY] * len(after),
        out_specs=pl.BlockSpec((tr, C), lambda i: (i, 0)),
        out_shape=S((R, C), F32),
        compiler_params=_cp("parallel"),
    )(parts, *after)


def _adamw(parts, w, m, v, name, after=()):
    P, R, C = parts.shape
    tr = _row_tile(R, C)
    c1 = 1.0 - ADAM_B1 ** ADAM_STEP
    c2 = 1.0 - ADAM_B2 ** ADAM_STEP

    def body(p_ref, w_ref, m_ref, v_ref, *rest):
        g_ref, d_ref, mo_ref, vo_ref = rest[len(after):]
        g = p_ref[0].astype(F32)
        for i in range(1, P):
            g = g + p_ref[i].astype(F32)
        mn = ADAM_B1 * m_ref[...] + (1.0 - ADAM_B1) * g
        vn = ADAM_B2 * v_ref[...] + (1.0 - ADAM_B2) * (g * g)
        g_ref[...] = g
        mo_ref[...] = mn
        vo_ref[...] = vn
        d_ref[...] = (-ADAM_LR) * ((mn / c1) / (jnp.sqrt(vn / c2) + ADAM_EPS) + ADAM_WD * w_ref[...])

    blk = pl.BlockSpec((tr, C), lambda i: (i, 0))
    return pl.pallas_call(
        body, name=name, grid=(R // tr,),
        in_specs=[pl.BlockSpec((P, tr, C), lambda i: (0, i, 0)), blk, blk, blk] + [ANY] * len(after),
        out_specs=[blk, blk, blk, blk],
        out_shape=[S((R, C), F32)] * 4,
        compiler_params=_cp("parallel"),
    )(parts, w, m, v, *after)


def _adamw_layer(parts, w3, m3, v3, layer, prev, name, after=()):
    P, R, C = parts.shape
    NL = w3.shape[0]
    tr = _row_tile(R, C)
    c1 = 1.0 - ADAM_B1 ** ADAM_STEP
    c2 = 1.0 - ADAM_B2 ** ADAM_STEP
    n_prev = 0 if prev is None else len(prev)

    def body(p_ref, w_ref, m_ref, v_ref, *rest):
        g_ref, d_ref, mo_ref, vo_ref = rest[n_prev + len(after):]
        g = p_ref[0].astype(F32)
        for i in range(1, P):
            g = g + p_ref[i].astype(F32)
        mn = ADAM_B1 * m_ref[...] + (1.0 - ADAM_B1) * g
        vn = ADAM_B2 * v_ref[...] + (1.0 - ADAM_B2) * (g * g)
        g_ref[...] = g
        mo_ref[...] = mn
        vo_ref[...] = vn
        d_ref[...] = (-ADAM_LR) * ((mn / c1) / (jnp.sqrt(vn / c2) + ADAM_EPS) + ADAM_WD * w_ref[...])

    blk = pl.BlockSpec((None, tr, C), lambda i: (layer, i, 0))
    return pl.pallas_call(
        body, name=name, grid=(R // tr,),
        in_specs=[pl.BlockSpec((P, tr, C), lambda i: (0, i, 0)), blk, blk, blk] + [ANY] * (n_prev + len(after)),
        out_specs=[blk, blk, blk, blk],
        out_shape=[S((NL, R, C), F32)] * 4,
        input_output_aliases={4 + k: k for k in range(n_prev)},
        compiler_params=_cp("parallel"),
    )(parts, w3, m3, v3, *(prev or ()), *after)


VMEM_WHOLE = pl.BlockSpec(memory_space=pltpu.VMEM)


def _pack_vectors(vectors, starts, rows, name):
    def body(*refs):
        o_ref = refs[-1]
        o_ref[...] = jnp.zeros_like(o_ref)
        for v_ref, r0 in zip(refs[:-1], starts):
            for j in range(v_ref.shape[1] // 128):
                o_ref[r0 + j:r0 + j + 1, :] = v_ref[:, j * 128:(j + 1) * 128]

    return pl.pallas_call(body, name=name, in_specs=[VMEM_WHOLE] * len(vectors), out_specs=VMEM_WHOLE,
                          out_shape=S((rows, 128), F32))(*vectors)


def _adamw_vectors(g_pack, params, keep_rows, name, after=()):
    n = len(params)
    P = g_pack.shape[0]
    c1 = 1.0 - ADAM_B1 ** ADAM_STEP
    c2 = 1.0 - ADAM_B2 ** ADAM_STEP

    def body(g_ref, *refs):
        ins, outs = refs[:3 * n], refs[3 * n + len(after):]
        gs = g_ref[0]
        for i in range(1, P):
            gs = gs + g_ref[i]
        for pi, (_, _, _, slots) in enumerate(params):
            w_ref, m_ref, v_ref = ins[3 * pi:3 * pi + 3]
            g_out, d_out, m_out, v_out = outs[4 * pi:4 * pi + 4]
            for idx, row in slots:
                g = gs[row:row + 1, :]
                mn = ADAM_B1 * m_ref[idx] + (1.0 - ADAM_B1) * g
                vn = ADAM_B2 * v_ref[idx] + (1.0 - ADAM_B2) * (g * g)
                g_out[idx] = g
                m_out[idx] = mn
                v_out[idx] = vn
                d_out[idx] = (-ADAM_LR) * ((mn / c1) / (jnp.sqrt(vn / c2) + ADAM_EPS) + ADAM_WD * w_ref[idx])
        outs[-1][...] = jnp.concatenate([gs[r:r + 1, :] for r in keep_rows], axis=0)

    flat = [a for w, m, v, _ in params for a in (w, m, v)]
    out_shape = [S(w.shape, F32) for w, _, _, _ in params for _ in range(4)] + [S((len(keep_rows), 128), F32)]
    outs = pl.pallas_call(
        body, name=name,
        in_specs=[VMEM_WHOLE] * (1 + len(flat)) + [ANY] * len(after),
        out_specs=[VMEM_WHOLE] * len(out_shape), out_shape=out_shape,
    )(g_pack, *flat, *after)
    return [outs[4 * i:4 * i + 4] for i in range(n)], outs[-1]


def _adamw_nd(parts, w, m, v, name, after=()):
    shp = w.shape
    C = shp[-1]
    outs = _adamw(parts.reshape(parts.shape[0], -1, C), w.reshape(-1, C), m.reshape(-1, C), v.reshape(-1, C), name, after)
    return [o.reshape(shp) for o in outs]


TILE_ROWS = 8


REP_SMALL_ROWS = 128
REP_GRAD_STARTS = (0, 8, 16, 24, 32, 40, 48, 56, 64)
REP_SMALL_STARTS = (0, 16, 32, 40, 48, 56)
REP_LOSS_ROW = 64


def _small_pack(cw, ba, bx, lam):
    pad8 = lambda a: jnp.pad(a, ((0, TILE_ROWS - a.shape[0]), (0, 0)))
    return jnp.concatenate([pad8(cw[0, :, 0, :]), pad8(ba[0]), pad8(bx[0]), pad8(lam[0]),
                            jnp.zeros((PV_ROWS - PV_CONV_B, LRU_BW), F32)], axis=0)


def kernel(x, norm_mix_g, norm_mlp_g, rg_w_in, rg_conv_w, rg_conv_b, rg_w_a, rg_b_a, rg_w_x, rg_b_x, rg_lam, rg_w_out, at_w_qkv, at_q_g, at_k_g, at_w_o, mlp_w_up, mlp_w_down, final_g, loss_target, m_norm_mix_g, m_norm_mlp_g, m_rg_w_in, m_rg_conv_w, m_rg_conv_b, m_rg_w_a, m_rg_b_a, m_rg_w_x, m_rg_b_x, m_rg_lam, m_rg_w_out, m_at_w_qkv, m_at_q_g, m_at_k_g, m_at_w_o, m_mlp_w_up, m_mlp_w_down, m_final_g, v_norm_mix_g, v_norm_mlp_g, v_rg_w_in, v_rg_conv_w, v_rg_conv_b, v_rg_w_a, v_rg_b_a, v_rg_w_x, v_rg_b_x, v_rg_lam, v_rg_w_out, v_at_w_qkv, v_at_q_g, v_at_k_g, v_at_w_o, v_mlp_w_up, v_mlp_w_down, v_final_g):
    D = x.shape[-1]
    bf = lambda a: a.astype(BF16)

    sp_w = _small_pack(rg_conv_w, rg_b_a, rg_b_x, rg_lam)
    started, _ = _exchange_start(
        [[bf(rg_w_in[0]), sp_w], [bf(rg_w_out[0])], [bf(mlp_w_up[0]), bf(mlp_w_down[0])],
         [bf(at_w_qkv[0]).T, bf(at_w_o[0])], [bf(mlp_w_up[1]), bf(mlp_w_down[1])]],
        ["near", "near", "near", "gather", "gather"], "gather_start")
    gathers = dict(zip((0, 1, 4, 2, 3), started))
    forwards = {}

    def fetch(stage, after):
        after = tuple(after)
        if stage == 0:
            got = _exchange_wait(gathers[0], after, "near", "gather_wait0")
            g_in, g_sp = _forward_wait(_forward_start([got], "forward_start0")[0], (), "forward_wait0")
            pvec = g_sp.transpose(1, 0, 2).reshape(PV_ROWS, D)
            pvec = jnp.concatenate([pvec[:PV_CONV_B], jnp.broadcast_to(rg_conv_b, (PV_ROWS - PV_CONV_B, D))], axis=0)
            return dict(w_in=g_in, pvec=pvec)
        if stage == 1:
            near = [_exchange_wait(gathers[s], after, "near", "gather_wait%d" % s) for s in (1, 4)]
            f_out, forwards[4] = _forward_start(near, "forward_start1")
            g_out, = _forward_wait(f_out, (), "forward_wait1")
            return dict(w_out=g_out.reshape(D, D))
        if stage == 4:
            g_up0, g_dn0 = _forward_wait(forwards[4], after, "forward_wait4")
            return dict(w_up0=g_up0, w_down0=g_dn0.reshape(-1, D))
        got = _exchange_wait(gathers[stage], after, "gather", "gather_wait%d" % stage)
        if stage == 2:
            return dict(w_qkv=got[0].reshape(-1, QKV_NB, D), w_o=got[1].reshape(D, D))
        return dict(w_up1=got[0], w_down1=got[1].reshape(-1, D))

    scatters = {}

    def send(stage, g):
        if stage == 3:
            arrs = [g["w_up1"], g["w_down1"].reshape(N_DEV, -1, D)]
        elif stage == 2:
            arrs = [g["w_qkv"].reshape(N_DEV, -1, D), g["w_o"].reshape(N_DEV, -1, D)]
        elif stage == 1:
            arrs = [g["w_up0"], g["w_down0"].reshape(N_DEV, -1, D)]
        elif stage == 4:
            arrs = [g["w_out"].reshape(N_DEV, -1, D), g["pvec"].reshape(PV_ROWS, N_DEV, LRU_BW).transpose(1, 0, 2),
                    bf(g["wa"]).reshape(N_DEV, -1, 128), bf(g["wx"]).reshape(N_DEV, -1, 128)]
        elif stage == 0:
            arrs = [g["w_in"]]
        else:
            small = _pack_vectors(g["g_mix"] + g["g_mlp"] + [g["g_fin"], g["conv_b"], g["qg"], g["kg"], g["loss"]],
                                  REP_GRAD_STARTS, REP_SMALL_ROWS, "pack_rep_small")
            arrs = [small.reshape(N_DEV, -1, 128)]
        (group,), token = _exchange_start([arrs], ["scatter"], "scatter_start%d" % (stage % 6))
        scatters[stage] = (group, token)
        return (token,)

    w = dict(g_mix=norm_mix_g, g_mlp=norm_mlp_g, g_fin=final_g[None], qg=at_q_g, kg=at_k_g,
             wa=bf(rg_w_a[0]), wx=bf(rg_w_x[0]))
    grad_x = _local_step(x, loss_target, w, fetch, send)

    res = {}
    r_up1, r_dn1 = _exchange_wait(scatters[3][0], (scatters[-1][1],), "scatter", "scatter_wait3")
    r_out, r_sp, r_wa, r_wx = _exchange_wait(scatters[4][0], (r_up1,), "scatter", "scatter_wait4")
    wa_part = _sum_parts(r_wa, "reduce_w_a")
    wx_part = _sum_parts(r_wx, "reduce_w_x", (wa_part,))
    (rep_gather,), rep_token = _exchange_start([[wa_part, wx_part]], ["gather"], "rep_gather_start")
    up = _adamw_layer(r_up1, mlp_w_up, m_mlp_w_up, v_mlp_w_up, 1, None, "adam_mlp_w_up1", after=(rep_token,))
    dn = _adamw_layer(r_dn1, mlp_w_down, m_mlp_w_down, v_mlp_w_down, 1, None, "adam_mlp_w_down1", after=(up[1],))
    r_qkv, r_o = _exchange_wait(scatters[2][0], (dn[1],), "scatter", "scatter_wait2")
    tr = lambda a: a[0].T
    qkv_t = _adamw_nd(r_qkv, tr(at_w_qkv), tr(m_at_w_qkv), tr(v_at_w_qkv), "adam_at_w_qkv")
    res["at_w_qkv"] = [o.T[None] for o in qkv_t]
    res["at_w_o"] = _adamw_nd(r_o[:, None], at_w_o, m_at_w_o, v_at_w_o, "adam_at_w_o", (qkv_t[1],))
    r_up0, r_dn0 = _exchange_wait(scatters[1][0], (res["at_w_o"][1],), "scatter", "scatter_wait1")
    res["mlp_w_up"] = _adamw_layer(r_up0, mlp_w_up, m_mlp_w_up, v_mlp_w_up, 0, up, "adam_mlp_w_up0")
    res["mlp_w_down"] = _adamw_layer(r_dn0, mlp_w_down, m_mlp_w_down, v_mlp_w_down, 0, dn, "adam_mlp_w_down0",
                                     after=(res["mlp_w_up"][1],))
    r_in, = _exchange_wait(scatters[0][0], (res["mlp_w_down"][1],), "scatter", "scatter_wait0")
    res["rg_w_in"] = _adamw_nd(r_in[:, None], rg_w_in, m_rg_w_in, v_rg_w_in, "adam_rg_w_in")
    res["rg_w_out"] = _adamw_nd(r_out[:, None], rg_w_out, m_rg_w_out, v_rg_w_out, "adam_rg_w_out", (res["rg_w_in"][1],))
    whole, lane = slice(None), slice(0, 1)
    two_rows = lambda r0: [((0, slice(d, d + 1), whole), r0 + d) for d in range(2)]
    (res["rg_conv_w"], res["rg_b_a"], res["rg_b_x"], res["rg_lam"]), _ = _adamw_vectors(
        r_sp, [(rg_conv_w, m_rg_conv_w, v_rg_conv_w, [((0, t, lane, whole), PV_CONV_W + t) for t in range(CONV_W)]),
               (rg_b_a, m_rg_b_a, v_rg_b_a, two_rows(PV_B_A)), (rg_b_x, m_rg_b_x, v_rg_b_x, two_rows(PV_B_X)),
               (rg_lam, m_rg_lam, v_rg_lam, two_rows(PV_LAM))], [0], "adam_small", (res["rg_w_out"][1],))

    r_small, = _exchange_wait(scatters[-1][0], (res["rg_lam"][1],), "scatter", "scatter_wait5")
    small_sum, = _all_gather([_sum_parts(r_small, "reduce_rep_small")], "gather_replicated")
    wa_sum, wx_sum = _exchange_wait(rep_gather, (small_sum,), "gather", "rep_gather_wait")
    rows = lambda a: a.reshape(-1, 128)
    wa_res = _adamw(wa_sum.reshape(1, -1, 128), rows(rg_w_a), rows(m_rg_w_a), rows(v_rg_w_a), "adam_rg_w_a")
    wx_res = _adamw(wx_sum.reshape(1, -1, 128), rows(rg_w_x), rows(m_rg_w_x), rows(v_rg_w_x), "adam_rg_w_x", (wa_res[1],))
    res["rg_w_a"] = [o.reshape(rg_w_a.shape) for o in wa_res]
    res["rg_w_x"] = [o.reshape(rg_w_x.shape) for o in wx_res]
    def vec_slots(a, r0):
        per = a.shape[1] // 128
        return [((slice(l, l + 1), slice(128 * j, 128 * (j + 1))), r0 + l * per + j) for l in range(a.shape[0]) for j in range(per)]

    fin = [final_g[None], m_final_g[None], v_final_g[None]]
    vecs = [(norm_mix_g, m_norm_mix_g, v_norm_mix_g), (norm_mlp_g, m_norm_mlp_g, v_norm_mlp_g), fin,
            (rg_conv_b, m_rg_conv_b, v_rg_conv_b), (at_q_g, m_at_q_g, v_at_q_g), (at_k_g, m_at_k_g, v_at_k_g)]
    outs, kept = _adamw_vectors(
        small_sum.reshape(1, -1, 128),
        [(wv, mv, vv, vec_slots(wv, r0)) for (wv, mv, vv), r0 in zip(vecs, REP_SMALL_STARTS)], [REP_LOSS_ROW], "adam_rep_small", (wx_res[1],))
    for nm, o in zip(["norm_mix_g", "norm_mlp_g", "final_g", "rg_conv_b", "at_q_g", "at_k_g"], outs):
        res[nm] = [a[0] for a in o] if nm == "final_g" else o
    loss = kept[0, 0]

    order = ["norm_mix_g", "norm_mlp_g", "rg_w_in", "rg_conv_w", "rg_conv_b", "rg_w_a", "rg_b_a", "rg_w_x", "rg_b_x",
             "rg_lam", "rg_w_out", "at_w_qkv", "at_q_g", "at_k_g", "at_w_o", "mlp_w_up", "mlp_w_down", "final_g"]
    return (loss, grad_x, *[res[nm][k] for k in range(4) for nm in order])
```

```python
import functools
import math

import jax
import jax.numpy as jnp
from jax import lax
from jax.experimental import pallas as pl
from jax.experimental.pallas import tpu as pltpu

F32 = jnp.float32
BF16 = jnp.bfloat16
S = jax.ShapeDtypeStruct

EPS = 1e-6
HEAD_DIM = 128
N_KV = 2
GRID_W = 64
ROPE_THETA = 10000.0
LRU_BW = 128
RG_C = 8.0
CONV_W = 4
N_DEV = 8
N_SEG = 8
SCAN_UNROLL = 8
TN_STEP_COLS = 512
PROJ_TM = 1024
MLP_TM = 512
VMEM_LIMIT_V7X = 56 * 1024 * 1024
SOFTMAX_SCALE = 1.0 / math.sqrt(HEAD_DIM)
GELU_K = math.sqrt(2.0 / math.pi)
GELU_C = 0.044715

ADAM_LR = 0.001
ADAM_B1 = 0.9
ADAM_B2 = 0.999
ADAM_EPS = 1e-08
ADAM_WD = 0.01
ADAM_STEP = 10

NT = (((1,), (1,)), ((), ()))
TN = (((0,), (0,)), ((), ()))


def _cp(*sem):
    return pltpu.CompilerParams(dimension_semantics=sem, vmem_limit_bytes=VMEM_LIMIT_V7X)


def _rms_r(xv):
    return lax.rsqrt(jnp.mean(xv * xv, axis=-1, keepdims=True) + EPS)


def _rms_bwd(dh, xv, g):
    r = _rms_r(xv)
    xh = xv * r
    dg = jnp.sum(dh * xh, axis=0, keepdims=True)
    dxh = dh * g
    dx = r * (dxh - xh * jnp.mean(dxh * xh, axis=-1, keepdims=True))
    return dx, dg


def _dot(a, b):
    return jnp.dot(a, b, preferred_element_type=F32)


def _dot_nt(a, b):
    return lax.dot_general(a, b, NT, preferred_element_type=F32)


def _dot_tn(a, b):
    return lax.dot_general(a, b, TN, preferred_element_type=F32)


def _norm_matmul(x, g, wblk, name, out_dtype=F32, transposed=False):
    T, D = x.shape
    NB, nb = wblk.shape[0], wblk.shape[1 if transposed else 2]
    mm = _dot_nt if transposed else _dot
    tm = min(T, PROJ_TM)

    def body(x_ref, g_ref, w_ref, o_ref, h_ref):
        xv = x_ref[...]
        hb = (xv * _rms_r(xv) * g_ref[...]).astype(BF16)
        h_ref[...] = hb
        for q in range(NB):
            o_ref[:, q * nb:(q + 1) * nb] = mm(hb, w_ref[q]).astype(o_ref.dtype)

    return pl.pallas_call(
        body, name=name, grid=(T // tm,),
        in_specs=[pl.BlockSpec((tm, D), lambda i: (i, 0)),
                  pl.BlockSpec((1, D), lambda i: (0, 0)),
                  pl.BlockSpec(wblk.shape, lambda i: (0, 0, 0))],
        out_specs=[pl.BlockSpec((tm, NB * nb), lambda i: (i, 0)),
                   pl.BlockSpec((tm, D), lambda i: (i, 0))],
        out_shape=[S((T, NB * nb), out_dtype), S((T, D), BF16)],
        compiler_params=_cp("parallel"),
    )(x, g, wblk)


def _matmul_res(a, w, res, name):
    T, K = a.shape
    N = w.shape[1]
    tm = min(T, PROJ_TM)

    def body(a_ref, w_ref, r_ref, o_ref):
        o_ref[...] = r_ref[...] + _dot(a_ref[...], w_ref[...])

    return pl.pallas_call(
        body, name=name, grid=(T // tm,),
        in_specs=[pl.BlockSpec((tm, K), lambda i: (i, 0)),
                  pl.BlockSpec((K, N), lambda i: (0, 0)),
                  pl.BlockSpec((tm, N), lambda i: (i, 0))],
        out_specs=pl.BlockSpec((tm, N), lambda i: (i, 0)),
        out_shape=S((T, N), F32),
        compiler_params=_cp("parallel"),
    )(a, w, res)


def _matmul_nt(a, w, name, out_dtype, after=()):
    T, N = a.shape
    K = w.shape[0]
    tm = min(T, PROJ_TM)

    def body(a_ref, w_ref, *rest):
        o_ref, ab_ref = rest[len(after):]
        ab = a_ref[...].astype(BF16)
        ab_ref[...] = ab
        o_ref[...] = _dot_nt(ab, w_ref[...]).astype(o_ref.dtype)

    return pl.pallas_call(
        body, name=name, grid=(T // tm,),
        in_specs=[pl.BlockSpec((tm, N), lambda i: (i, 0)),
                  pl.BlockSpec((K, N), lambda i: (0, 0))] + [pl.BlockSpec(memory_space=pl.ANY)] * len(after),
        out_specs=[pl.BlockSpec((tm, K), lambda i: (i, 0)),
                   pl.BlockSpec((tm, N), lambda i: (i, 0))],
        out_shape=[S((T, K), out_dtype), S((T, N), BF16)],
        compiler_params=_cp("parallel"),
    )(a, w, *after)


def _matmul_tn(a, b3, nb, name, blocked, after=()):
    T, M = a.shape
    SB, _, N = b3.shape
    per = N // nb
    NB = SB * per
    tk = min(T, 1024)
    nk = T // tk
    jb = max(1, TN_STEP_COLS // nb) if blocked else 1
    assert per % jb == 0
    if blocked:
        out_spec, out_shape = pl.BlockSpec((jb, M, nb), lambda j, k: (j, 0, 0)), S((NB, M, nb), BF16)
    else:
        assert SB == 1
        out_spec, out_shape = pl.BlockSpec((M, nb), lambda j, k: (0, j)), S((M, N), BF16)

    def body(a_ref, b_ref, *rest):
        o_ref, acc_ref = rest[len(after):]
        k = pl.program_id(1)

        @pl.when(k == 0)
        def _():
            acc_ref[...] = jnp.zeros_like(acc_ref)

        av = a_ref[...]
        for q in range(jb):
            acc_ref[q] += _dot_tn(av, b_ref[:, q * nb:(q + 1) * nb])

        @pl.when(k == nk - 1)
        def _():
            if blocked:
                o_ref[...] = acc_ref[...].astype(BF16)
            else:
                o_ref[...] = acc_ref[0].astype(BF16)

    return pl.pallas_call(
        body, name=name, grid=(NB // jb, nk),
        in_specs=[pl.BlockSpec((tk, M), lambda j, k: (k, 0)),
                  pl.BlockSpec((None, tk, jb * nb), lambda j, k: ((j * jb) // per, k, ((j * jb) % per) // jb))]
        + [pl.BlockSpec(memory_space=pl.ANY)] * len(after),
        out_specs=out_spec,
        out_shape=out_shape,
        scratch_shapes=[pltpu.VMEM((jb, M, nb), F32)],
        compiler_params=_cp("parallel", "arbitrary"),
    )(a, b3, *after)


def _nt_normbwd(dz3, wblk, x, g, dres, name, after=(), transposed=False):
    T, D = x.shape
    NB, nb = wblk.shape[0], wblk.shape[1 if transposed else 2]
    mm = _dot if transposed else _dot_nt
    SB, _, N = dz3.shape
    per = N // nb
    tm = min(T, PROJ_TM)

    def body(dz_ref, w_ref, x_ref, g_ref, dr_ref, *rest):
        dx_ref, dg_ref = rest[len(after):]

        @pl.when(pl.program_id(0) == 0)
        def _():
            dg_ref[...] = jnp.zeros_like(dg_ref)

        dh = None
        for q in range(NB):
            cols = slice((q % per) * nb, (q % per + 1) * nb)
            part = mm(dz_ref[q // per, :, cols], w_ref[q])
            dh = part if dh is None else dh + part
        dx, dg = _rms_bwd(dh, x_ref[...], g_ref[...])
        dx_ref[...] = dr_ref[...] + dx
        dg_ref[...] += dg

    return pl.pallas_call(
        body, name=name, grid=(T // tm,),
        in_specs=[pl.BlockSpec((SB, tm, N), lambda i: (0, i, 0)),
                  pl.BlockSpec(wblk.shape, lambda i: (0, 0, 0)),
                  pl.BlockSpec((tm, D), lambda i: (i, 0)),
                  pl.BlockSpec((1, D), lambda i: (0, 0)),
                  pl.BlockSpec((tm, D), lambda i: (i, 0))] + [pl.BlockSpec(memory_space=pl.ANY)] * len(after),
        out_specs=[pl.BlockSpec((tm, D), lambda i: (i, 0)),
                   pl.BlockSpec((1, D), lambda i: (0, 0))],
        out_shape=[S((T, D), F32), S((1, D), F32)],
        compiler_params=_cp("arbitrary"),
    )(dz3, wblk, x, g, dres, *after)


def _loss_head(xv, tv, gv, D):
    err = xv * _rms_r(xv) * gv - tv
    e2 = jnp.sum(jnp.sum(err * err, axis=-1, keepdims=True), axis=0, keepdims=True)
    dx, dg = _rms_bwd(err * (1.0 / D), xv, gv)
    return (0.5 / D) * e2, dx, dg


def _mlp_fwd(x, g, wup, wdown, name, head=None):
    T, D = x.shape
    NB, _, fb = wup.shape
    tm = min(T, MLP_TM)
    n_head = 0 if head is None else 2

    def body(x_ref, g_ref, wu_ref, wd_ref, *rest):
        xo_ref, a_ref, h_ref = rest[n_head:n_head + 3]
        xv = x_ref[...]
        hb = (xv * _rms_r(xv) * g_ref[...]).astype(BF16)
        h_ref[...] = hb
        acc = xv
        for j in range(NB):
            a = _dot(hb, wu_ref[j])
            a_ref[:, j * fb:(j + 1) * fb] = a.astype(BF16)
            u = jnp.maximum(a, 0.0)
            acc = acc + _dot((u * u).astype(BF16), wd_ref[j * fb:(j + 1) * fb, :])

        if head is None:
            xo_ref[...] = acc
        else:
            t_ref, gf_ref = rest[:2]
            loss_ref, dgf_ref = rest[n_head + 3:n_head + 5]

            @pl.when(pl.program_id(0) == 0)
            def _():
                loss_ref[...] = jnp.zeros_like(loss_ref)
                dgf_ref[...] = jnp.zeros_like(dgf_ref)

            e2, dx, dg = _loss_head(acc, t_ref[...], gf_ref[...], D)
            xo_ref[...] = dx
            loss_ref[...] += e2
            dgf_ref[...] += dg

    row = pl.BlockSpec((tm, D), lambda i: (i, 0))
    vec = pl.BlockSpec((1, D), lambda i: (0, 0))
    once = pl.Buffered(1)
    in_specs = [row, vec, pl.BlockSpec((NB, D, fb), lambda i: (0, 0, 0), pipeline_mode=once),
                pl.BlockSpec((NB * fb, D), lambda i: (0, 0), pipeline_mode=once)]
    out_specs = [row, pl.BlockSpec((tm, NB * fb), lambda i: (i, 0)), row]
    out_shape = [S((T, D), F32), S((T, NB * fb), BF16), S((T, D), BF16)]
    if head is not None:
        in_specs += [row, vec]
        out_specs += [pl.BlockSpec((1, 128), lambda i: (0, 0)), vec]
        out_shape += [S((1, 128), F32), S((1, D), F32)]
    return pl.pallas_call(
        body, name=name, grid=(T // tm,),
        in_specs=in_specs, out_specs=out_specs, out_shape=out_shape,
        compiler_params=_cp("parallel" if head is None else "arbitrary"),
    )(x, g, wup, wdown, *(head or ()))


def _mlp_bwd_dx(x, dout, a, g, wup, wdown, name):
    T, D = x.shape
    NB, _, fb = wup.shape
    tm = min(T, MLP_TM)

    def body(x_ref, do_ref, a_ref, g_ref, wu_ref, wd_ref, dx_ref, da_ref, dob_ref, dg_ref):
        @pl.when(pl.program_id(0) == 0)
        def _():
            dg_ref[...] = jnp.zeros_like(dg_ref)

        dov = do_ref[...]
        dob = dov.astype(BF16)
        dob_ref[...] = dob
        dh = None
        for j in range(NB):
            cols = slice(j * fb, (j + 1) * fb)
            du2 = _dot_nt(dob, wd_ref[cols, :])
            u = jnp.maximum(a_ref[:, cols].astype(F32), 0.0)
            da = (du2 * (2.0 * u)).astype(BF16)
            da_ref[:, cols] = da
            part = _dot_nt(da, wu_ref[j])
            dh = part if dh is None else dh + part
        dx, dg = _rms_bwd(dh, x_ref[...], g_ref[...])
        dx_ref[...] = dov + dx
        dg_ref[...] += dg

    row = pl.BlockSpec((tm, D), lambda i: (i, 0))
    wide = pl.BlockSpec((tm, NB * fb), lambda i: (i, 0))
    vec = pl.BlockSpec((1, D), lambda i: (0, 0))
    once = pl.Buffered(1)
    return pl.pallas_call(
        body, name=name, grid=(T // tm,),
        in_specs=[row, row, wide, vec, pl.BlockSpec((NB, D, fb), lambda i: (0, 0, 0), pipeline_mode=once),
                  pl.BlockSpec((NB * fb, D), lambda i: (0, 0), pipeline_mode=once)],
        out_specs=[row, wide, row, vec],
        out_shape=[S((T, D), F32), S((T, NB * fb), BF16), S((T, D), BF16), S((1, D), F32)],
        compiler_params=_cp("arbitrary"),
    )(x, dout, a, g, wup, wdown)


def _mlp_bwd_dw(h, da, a, dob, fb, name):
    T, D = h.shape
    F = a.shape[1]
    NB = F // fb
    tk = min(T, 1024)
    nk = T // tk

    def body(h_ref, da_ref, a_ref, dob_ref, dwu_ref, dwd_ref, au_ref, ad_ref):
        k = pl.program_id(1)

        @pl.when(k == 0)
        def _():
            au_ref[...] = jnp.zeros_like(au_ref)
            ad_ref[...] = jnp.zeros_like(ad_ref)

        au_ref[...] += _dot_tn(h_ref[...], da_ref[...])
        u = jnp.maximum(a_ref[...].astype(F32), 0.0)
        ad_ref[...] += _dot_tn((u * u).astype(BF16), dob_ref[...])

        @pl.when(k == nk - 1)
        def _():
            dwu_ref[...] = au_ref[...].astype(BF16)
            dwd_ref[...] = ad_ref[...].astype(BF16)

    return pl.pallas_call(
        body, name=name, grid=(NB, nk),
        in_specs=[pl.BlockSpec((tk, D), lambda j, k: (k, 0)),
                  pl.BlockSpec((tk, fb), lambda j, k: (k, j)),
                  pl.BlockSpec((tk, fb), lambda j, k: (k, j)),
                  pl.BlockSpec((tk, D), lambda j, k: (k, 0))],
        out_specs=[pl.BlockSpec((None, D, fb), lambda j, k: (j, 0, 0)),
                   pl.BlockSpec((fb, D), lambda j, k: (j, 0))],
        out_shape=[S((NB, D, fb), BF16), S((F, D), BF16)],
        scratch_shapes=[pltpu.VMEM((D, fb), F32), pltpu.VMEM((fb, D), F32)],
        compiler_params=_cp("parallel", "arbitrary"),
    )(h, da, a, dob)


def _rope_tables(L):
    nf = HEAD_DIM // 4
    t = jnp.arange(L, dtype=jnp.int32)
    row = (t // GRID_W).astype(F32)
    col = (t % GRID_W).astype(F32)
    inv = ROPE_THETA ** (-jnp.arange(nf, dtype=F32) / nf)
    ar = row[:, None] * inv
    ac = col[:, None] * inv
    cos = jnp.concatenate([jnp.cos(ar), jnp.cos(ar), jnp.cos(ac), jnp.cos(ac)], axis=-1)
    sin = jnp.concatenate([-jnp.sin(ar), jnp.sin(ar), -jnp.sin(ac), jnp.sin(ac)], axis=-1)
    return cos, sin


def _swap32(x):
    lane = lax.broadcasted_iota(jnp.int32, x.shape, 1)
    up = pltpu.roll(x, HEAD_DIM - 32, 1)
    down = pltpu.roll(x, 32, 1)
    return jnp.where((lane % 64) < 32, up, down)


def _qk_prep(qkv, qg, kg, cos, sin, L, name):
    T, W = qkv.shape
    nh = W // HEAD_DIM - 2 * N_KV
    tm = min(L, 512)
    lb = L // tm

    def body(qkv_ref, qg_ref, kg_ref, cos_ref, sin_ref, q_ref, k_ref, v_ref):
        c = cos_ref[...]
        s = sin_ref[...]
        for h in range(nh + N_KV):
            xh = qkv_ref[:, h * HEAD_DIM:(h + 1) * HEAD_DIM]
            gv = qg_ref[...] if h < nh else kg_ref[...]
            y = xh * _rms_r(xh) * gv
            y = (y * c + _swap32(y) * s).astype(BF16)
            if h < nh:
                q_ref[:, h * HEAD_DIM:(h + 1) * HEAD_DIM] = y
            else:
                k_ref[:, (h - nh) * HEAD_DIM:(h - nh + 1) * HEAD_DIM] = y
        v_ref[...] = qkv_ref[:, (nh + N_KV) * HEAD_DIM:].astype(BF16)

    return pl.pallas_call(
        body, name=name, grid=(T // tm,),
        in_specs=[pl.BlockSpec((tm, W), lambda i: (i, 0)),
                  pl.BlockSpec((1, HEAD_DIM), lambda i: (0, 0)),
                  pl.BlockSpec((1, HEAD_DIM), lambda i: (0, 0)),
                  pl.BlockSpec((tm, HEAD_DIM), lambda i: (i % lb, 0)),
                  pl.BlockSpec((tm, HEAD_DIM), lambda i: (i % lb, 0))],
        out_specs=[pl.BlockSpec((tm, nh * HEAD_DIM), lambda i: (i, 0)),
                   pl.BlockSpec((tm, N_KV * HEAD_DIM), lambda i: (i, 0)),
                   pl.BlockSpec((tm, N_KV * HEAD_DIM), lambda i: (i, 0))],
        out_shape=[S((T, nh * HEAD_DIM), BF16), S((T, N_KV * HEAD_DIM), BF16), S((T, N_KV * HEAD_DIM), BF16)],
        compiler_params=_cp("parallel"),
    )(qkv, qg, kg, cos, sin)


def _qk_prep_bwd(qkv, dq, dk, dv, qg, kg, cos, sin, L, name):
    T, W = qkv.shape
    nh = W // HEAD_DIM - 2 * N_KV
    tm = min(L, 512)
    lb = L // tm

    def body(qkv_ref, dq_ref, dk_ref, dv_ref, qg_ref, kg_ref, cos_ref, sin_ref, dz_ref, dqg_ref, dkg_ref):
        @pl.when(pl.program_id(0) == 0)
        def _():
            dqg_ref[...] = jnp.zeros_like(dqg_ref)
            dkg_ref[...] = jnp.zeros_like(dkg_ref)

        c = cos_ref[...]
        s = sin_ref[...]
        for h in range(nh + N_KV):
            cols = slice(h * HEAD_DIM, (h + 1) * HEAD_DIM)
            if h < nh:
                dout, gv, dg_ref = dq_ref[:, cols], qg_ref[...], dqg_ref
            else:
                kc = slice((h - nh) * HEAD_DIM, (h - nh + 1) * HEAD_DIM)
                dout, gv, dg_ref = dk_ref[:, kc], kg_ref[...], dkg_ref
            dy = dout * c - _swap32(dout) * s
            dx, dg = _rms_bwd(dy, qkv_ref[:, cols], gv)
            dg_ref[...] += dg
            dz_ref[:, cols] = dx.astype(BF16)
        dz_ref[:, (nh + N_KV) * HEAD_DIM:] = dv_ref[...].astype(BF16)

    return pl.pallas_call(
        body, name=name, grid=(T // tm,),
        in_specs=[pl.BlockSpec((tm, W), lambda i: (i, 0)),
                  pl.BlockSpec((tm, nh * HEAD_DIM), lambda i: (i, 0)),
                  pl.BlockSpec((tm, N_KV * HEAD_DIM), lambda i: (i, 0)),
                  pl.BlockSpec((tm, N_KV * HEAD_DIM), lambda i: (i, 0)),
                  pl.BlockSpec((1, HEAD_DIM), lambda i: (0, 0)),
                  pl.BlockSpec((1, HEAD_DIM), lambda i: (0, 0)),
                  pl.BlockSpec((tm, HEAD_DIM), lambda i: (i % lb, 0)),
                  pl.BlockSpec((tm, HEAD_DIM), lambda i: (i % lb, 0))],
        out_specs=[pl.BlockSpec((tm, W), lambda i: (i, 0)),
                   pl.BlockSpec((1, HEAD_DIM), lambda i: (0, 0)),
                   pl.BlockSpec((1, HEAD_DIM), lambda i: (0, 0))],
        out_shape=[S((T, W), BF16), S((1, HEAD_DIM), F32), S((1, HEAD_DIM), F32)],
        compiler_params=_cp("arbitrary"),
    )(qkv, dq, dk, dv, qg, kg, cos, sin)


EXP2_SCALE = SOFTMAX_SCALE * math.log2(math.e)
ATTN_SUB = 256
ATTN_TQ = 1024


def _softmax_rows(q, k):
    s = _dot_nt(q, k)
    e = jnp.exp2((s - jnp.max(s, axis=-1, keepdims=True)) * EXP2_SCALE)
    return e, jnp.sum(e, axis=-1, keepdims=True)


def _attn_fwd(q, k, v, L, name):
    T = q.shape[0]
    nh = q.shape[1] // HEAD_DIM
    G = nh // N_KV
    B = T // L
    tq = min(L, ATTN_TQ)
    nq = L // tq
    sub = min(tq, ATTN_SUB)

    def body(q_ref, k_ref, v_ref, o_ref):
        for h in range(tq // sub):
            rows = slice(h * sub, (h + 1) * sub)
            e, l = _softmax_rows(q_ref[rows, :], k_ref[...])
            o_ref[rows, :] = (_dot(e.astype(BF16), v_ref[...]) / l).astype(BF16)

    qspec = pl.BlockSpec((tq, HEAD_DIM), lambda b, kv, g, qi: (b * nq + qi, kv * G + g))
    kspec = pl.BlockSpec((L, HEAD_DIM), lambda b, kv, g, qi: (b, kv))
    return pl.pallas_call(
        body, name=name, grid=(B, N_KV, G, nq),
        in_specs=[qspec, kspec, kspec],
        out_specs=qspec,
        out_shape=S((T, nh * HEAD_DIM), BF16),
        compiler_params=_cp("parallel", "parallel", "parallel", "parallel"),
    )(q, k, v)


def _attn_bwd(q, k, v, do, o, L, name):
    T = q.shape[0]
    nh = q.shape[1] // HEAD_DIM
    G = nh // N_KV
    B = T // L
    tq = min(L, ATTN_TQ)
    nq = L // tq

    sub = min(tq, ATTN_SUB)

    def body(q_ref, k_ref, v_ref, do_ref, o_ref, dq_ref, dk_ref, dv_ref, ds_scr, p_scr):
        first = (pl.program_id(2) == 0) & (pl.program_id(3) == 0)
        last = (pl.program_id(2) == G - 1) & (pl.program_id(3) == nq - 1)

        @pl.when(first)
        def _():
            dk_ref[...] = jnp.zeros_like(dk_ref)
            dv_ref[...] = jnp.zeros_like(dv_ref)

        for h in range(tq // sub):
            rows = slice(h * sub, (h + 1) * sub)
            dov = do_ref[rows, :]
            e, l = _softmax_rows(q_ref[rows, :], k_ref[...])
            p = e * (1.0 / l)
            dsum = jnp.sum(dov.astype(F32) * o_ref[rows, :].astype(F32), axis=-1, keepdims=True)
            ds_scr[rows, :] = (p * (_dot_nt(dov, v_ref[...]) - dsum)).astype(BF16)
            p_scr[rows, :] = p.astype(BF16)
        ds = ds_scr[...]
        dq_ref[...] = _dot(ds, k_ref[...]) * SOFTMAX_SCALE
        dk_ref[...] += _dot_tn(ds, q_ref[...])
        dv_ref[...] += _dot_tn(p_scr[...], do_ref[...])

        @pl.when(last)
        def _():
            dk_ref[...] = dk_ref[...] * SOFTMAX_SCALE

    qspec = pl.BlockSpec((tq, HEAD_DIM), lambda b, kv, g, qi: (b * nq + qi, kv * G + g))
    kspec = pl.BlockSpec((L, HEAD_DIM), lambda b, kv, g, qi: (b, kv))
    return pl.pallas_call(
        body, name=name, grid=(B, N_KV, G, nq),
        in_specs=[qspec, kspec, kspec, qspec, qspec],
        out_specs=[qspec, kspec, kspec],
        out_shape=[S((T, nh * HEAD_DIM), F32), S((T, N_KV * HEAD_DIM), F32), S((T, N_KV * HEAD_DIM), F32)],
        scratch_shapes=[pltpu.VMEM((tq, L), BF16), pltpu.VMEM((tq, L), BF16)],
        compiler_params=_cp("parallel", "parallel", "arbitrary", "arbitrary"),
    )(q, k, v, do, o)


PV_CONV_W = 0
PV_B_A = 8
PV_B_X = 16
PV_LAM = 24
PV_CONV_B = 32
PV_ROWS = 40


def _shift_rows(x, k):
    if k == 0:
        return x
    L = x.shape[0]
    n = N_SEG * abs(k)
    seg = lax.broadcasted_iota(jnp.int32, (n, x.shape[1]), 0) % N_SEG
    if k > 0:
        edge = jnp.where(seg == 0, 0.0, pltpu.roll(x[L - n:], 1, 0))
        return jnp.concatenate([edge, x[:L - n]], axis=0)
    edge = jnp.where(seg == N_SEG - 1, 0.0, pltpu.roll(x[:n], n - 1, 0))
    return jnp.concatenate([x[n:], edge], axis=0)


def _conv_taps(rec, pv):
    c = pv[PV_CONV_B:PV_CONV_B + 1]
    for j in range(CONV_W):
        c = c + pv[PV_CONV_W + j:PV_CONV_W + j + 1] * _shift_rows(rec, 2 - j)
    return c


def _sigmoid(x):
    return 0.5 * jnp.tanh(0.5 * x) + 0.5


EXPM1_SERIES_BELOW = 0.03


def _rg_gates(c, cbf, wa, wx, ba, bx, lam):
    r = _sigmoid(_dot(cbf, wa) + ba)
    i = _sigmoid(_dot(cbf, wx) + bx)
    sp = jnp.maximum(-lam, 0.0) + jnp.log1p(jnp.exp(-jnp.abs(lam)))
    la = r * ((-RG_C) * sp)
    a = jnp.exp(la)
    a2 = a * a
    x = la + la
    series = -(x * ((x * (1.0 / 6.0) + 0.5) * x + 1.0))
    om = jnp.where(x > -EXPM1_SERIES_BELOW, series, 1.0 - a2)
    rm = lax.rsqrt(om)
    return r, i, a, om * rm, rm, a2, sp


def _gelu(x):
    t = jnp.tanh(GELU_K * (x + GELU_C * x * x * x))
    return 0.5 * x * (1.0 + t), t


def _scan_pair(af_ref, uf_ref, ab_ref, ub_ref, hf_ref, hb_ref, pf_ref, pb_ref, L):
    ls = L // N_SEG
    zero = jnp.zeros((N_SEG, LRU_BW), F32)
    one = jnp.ones((N_SEG, LRU_BW), F32)
    tile = lambda t: pl.ds(pl.multiple_of(t * N_SEG, N_SEG), N_SEG)

    def steps(tc, carry):
        hf, pf, hb, pb = carry
        for q in range(SCAN_UNROLL):
            t = tc * SCAN_UNROLL + q
            rf, rb = tile(t), tile(ls - 1 - t)
            af = af_ref[rf, :]
            hf = af * hf + uf_ref[rf, :]
            pf = pf * af
            hf_ref[rf, :] = hf
            pf_ref[rf, :] = pf
            ab = ab_ref[rb, :]
            hb = ab * hb + ub_ref[rb, :]
            pb = pb * ab
            hb_ref[rb, :] = hb
            pb_ref[rb, :] = pb
        return hf, pf, hb, pb

    hf_e, pf_e, hb_e, pb_e = lax.fori_loop(0, ls // SCAN_UNROLL, steps, (zero, one, zero, one))

    rows, cin = [], jnp.zeros((1, LRU_BW), F32)
    for s in range(N_SEG):
        rows.append(cin)
        cin = hf_e[s:s + 1] + pf_e[s:s + 1] * cin
    cf = jnp.concatenate(rows, axis=0)
    rows, cin = [], jnp.zeros((1, LRU_BW), F32)
    for s in reversed(range(N_SEG)):
        rows.append(cin)
        cin = hb_e[s:s + 1] + pb_e[s:s + 1] * cin
    cb = jnp.concatenate(rows[::-1], axis=0)

    def fix(tc, _):
        for q in range(SCAN_UNROLL):
            r = tile(tc * SCAN_UNROLL + q)
            hf_ref[r, :] = hf_ref[r, :] + pf_ref[r, :] * cf
            hb_ref[r, :] = hb_ref[r, :] + pb_ref[r, :] * cb
        return 0

    lax.fori_loop(0, ls // SCAN_UNROLL, fix, 0)


def _rg_specs(L, D, nblk):
    slab = lambda off: pl.BlockSpec((L, LRU_BW), lambda cb, b: (b, off + cb))
    wspec = pl.BlockSpec((2, None, LRU_BW, LRU_BW), lambda cb, b: (0, cb, 0, 0))
    pvspec = pl.BlockSpec((PV_ROWS, LRU_BW), lambda cb, b: (0, cb))
    return slab, wspec, pvspec


def _rg_fwd(z, pvec, wa, wx, L, name):
    T, C2 = z.shape
    C = C2 // 2
    nblk = C // LRU_BW
    B = T // L
    slab, wspec, pvspec = _rg_specs(L, C, nblk)

    def body(gp_ref, rec_ref, pv_ref, wa_ref, wx_ref, yg_ref, hf_ref, hb_ref, a_scr, u_scr, p_scr):
        pv = pv_ref[...]
        c = _conv_taps(rec_ref[...], pv)
        cbf = c.astype(BF16)
        for d in range(2):
            _, i, a, m, _, _, _ = _rg_gates(c, cbf, wa_ref[d], wx_ref[d], pv[PV_B_A + d:PV_B_A + d + 1],
                                      pv[PV_B_X + d:PV_B_X + d + 1], pv[PV_LAM + d:PV_LAM + d + 1])
            a_scr[d] = a
            u_scr[d] = m * (i * c)
        _scan_pair(a_scr.at[0], u_scr.at[0], a_scr.at[1], u_scr.at[1], hf_ref, hb_ref, p_scr.at[0], p_scr.at[1], L)
        gate, _ = _gelu(gp_ref[...])
        yg_ref[...] = ((hf_ref[...] + hb_ref[...]) * gate).astype(BF16)

    return pl.pallas_call(
        body, name=name, grid=(nblk, B),
        in_specs=[slab(0), slab(nblk), pvspec, wspec, wspec],
        out_specs=[slab(0), slab(0), slab(0)],
        out_shape=[S((T, C), BF16), S((T, C), F32), S((T, C), F32)],
        scratch_shapes=[pltpu.VMEM((2, L, LRU_BW), F32)] * 3,
        compiler_params=_cp("parallel", "parallel"),
    )(z, z, pvec, wa, wx)


def _rg_bwd(z, hf, hb, dyg, pvec, wa, wx, L, name):
    T, C2 = z.shape
    C = C2 // 2
    nblk = C // LRU_BW
    B = T // L
    slab, wspec, pvspec = _rg_specs(L, C, nblk)

    def body(gp_ref, rec_ref, hf_ref, hb_ref, dyg_ref, pv_ref, wa_ref, wx_ref,
             dz_ref, dwa_ref, dwx_ref, dpv_ref, a_scr, u_scr, d_scr, p_scr):
        @pl.when(pl.program_id(1) == 0)
        def _():
            dwa_ref[...] = jnp.zeros_like(dwa_ref)
            dwx_ref[...] = jnp.zeros_like(dwx_ref)
            dpv_ref[...] = jnp.zeros_like(dpv_ref)

        pv = pv_ref[...]
        rec = rec_ref[...]
        c = _conv_taps(rec, pv)
        cbf = c.astype(BF16)
        gp = gp_ref[...]
        gate, th = _gelu(gp)
        dgelu = 0.5 * (1.0 + th) + 0.5 * gp * (1.0 - th * th) * GELU_K * (1.0 + 3.0 * GELU_C * gp * gp)
        dyg = dyg_ref[...]
        dz_ref[0] = (dyg * (hf_ref[...] + hb_ref[...]) * dgelu).astype(BF16)
        dy = dyg * gate

        gates = []
        for d in range(2):
            gates.append(_rg_gates(c, cbf, wa_ref[d], wx_ref[d], pv[PV_B_A + d:PV_B_A + d + 1],
                                   pv[PV_B_X + d:PV_B_X + d + 1], pv[PV_LAM + d:PV_LAM + d + 1]))
        a_scr[0] = _shift_rows(gates[1][2], 1)
        a_scr[1] = _shift_rows(gates[0][2], -1)
        u_scr[...] = dy
        _scan_pair(a_scr.at[0], u_scr, a_scr.at[1], u_scr, d_scr.at[1], d_scr.at[0], p_scr.at[0], p_scr.at[1], L)

        dc = jnp.zeros_like(c)
        rows = []
        for d in range(2):
            r, i, a, m, rm, a2, sp = gates[d]
            delta = d_scr[d]
            hnb = _shift_rows(hf_ref[...], 1) if d == 0 else _shift_rows(hb_ref[...], -1)
            da = delta * hnb
            dm = delta * (i * c)
            di = delta * (m * c)
            dc = dc + delta * (m * i)
            dla = da * a - dm * (a2 * rm)
            dpa = (dla * ((-RG_C) * sp)) * (r * (1.0 - r))
            dpx = di * (i * (1.0 - i))
            dsp = (-RG_C) * jnp.sum(dla * r, axis=0, keepdims=True)
            lam = pv[PV_LAM + d:PV_LAM + d + 1]
            rows.append((jnp.sum(dpa, axis=0, keepdims=True), jnp.sum(dpx, axis=0, keepdims=True),
                         -dsp * _sigmoid(-lam)))
            dpab = dpa.astype(BF16)
            dpxb = dpx.astype(BF16)
            dwa_ref[d] += _dot_tn(cbf, dpab)
            dwx_ref[d] += _dot_tn(cbf, dpxb)
            dc = dc + _dot_nt(dpab, wa_ref[d]) + _dot_nt(dpxb, wx_ref[d])

        drec = jnp.zeros_like(c)
        dcw = []
        for j in range(CONV_W):
            drec = drec + pv[PV_CONV_W + j:PV_CONV_W + j + 1] * _shift_rows(dc, j - 2)
            dcw.append(jnp.sum(dc * _shift_rows(rec, 2 - j), axis=0, keepdims=True))
        dz_ref[1] = drec.astype(BF16)
        for j in range(CONV_W):
            dpv_ref[PV_CONV_W + j:PV_CONV_W + j + 1, :] += dcw[j]
        for d in range(2):
            dpv_ref[PV_B_A + d:PV_B_A + d + 1, :] += rows[d][0]
            dpv_ref[PV_B_X + d:PV_B_X + d + 1, :] += rows[d][1]
            dpv_ref[PV_LAM + d:PV_LAM + d + 1, :] += rows[d][2]
        dpv_ref[PV_CONV_B:PV_CONV_B + 1, :] += jnp.sum(dc, axis=0, keepdims=True)

    return pl.pallas_call(
        body, name=name, grid=(nblk, B),
        in_specs=[slab(0), slab(nblk), slab(0), slab(0), slab(0), pvspec, wspec, wspec],
        out_specs=[pl.BlockSpec((2, L, LRU_BW), lambda cb, b: (0, b, cb)), wspec, wspec, pvspec],
        out_shape=[S((2, T, C), BF16), S((2, nblk, LRU_BW, LRU_BW), F32), S((2, nblk, LRU_BW, LRU_BW), F32),
                   S((PV_ROWS, C), F32)],
        scratch_shapes=[pltpu.VMEM((2, L, LRU_BW), F32), pltpu.VMEM((L, LRU_BW), F32),
                        pltpu.VMEM((2, L, LRU_BW), F32), pltpu.VMEM((2, L, LRU_BW), F32)],
        compiler_params=_cp("parallel", "arbitrary"),
    )(z, z, hf, hb, dyg, pvec, wa, wx)


QKV_NB = 512


def _interleave(a):
    *lead, L, D = a.shape
    return a.reshape(*lead, N_SEG, L // N_SEG, D).swapaxes(-3, -2).reshape(*lead, L, D)


def _deinterleave(a):
    *lead, L, D = a.shape
    return a.reshape(*lead, L // N_SEG, N_SEG, D).swapaxes(-3, -2).reshape(*lead, L, D)


def _local_step(x3, tgt3, w, fetch, send):
    Bl, L, D = x3.shape
    T = Bl * L
    x = _interleave(x3).reshape(T, D)
    tgt = _interleave(tgt3).reshape(T, D)
    gm = [w["g_mix"][i:i + 1] for i in range(2)]
    gl = [w["g_mlp"][i:i + 1] for i in range(2)]

    w0 = fetch(0, ())
    nb_in = w0["w_in"].shape[-1]
    z, h0 = _norm_matmul(x, gm[0], w0["w_in"], "rg_in")
    yg, hf, hb = _rg_fwd(z, w0["pvec"], w["wa"], w["wx"], L, "rg_fwd")
    w1 = fetch(1, (yg,))
    x1 = _matmul_res(yg, w1["w_out"], x, "rg_out")
    w1.update(fetch(4, (x1,)))
    fb = w1["w_up0"].shape[-1]
    x2, a0, hm0 = _mlp_fwd(x1, gl[0], w1["w_up0"], w1["w_down0"], "mlp0_fwd")
    w2 = fetch(2, (x2,))
    qkv, h1 = _norm_matmul(x2, gm[1], w2["w_qkv"], "at_qkv", transposed=True)
    cos, sin = [_interleave(t) for t in _rope_tables(L)]
    qn, kn, vb = _qk_prep(qkv, w["qg"], w["kg"], cos, sin, L, "at_prep")
    o = _attn_fwd(qn, kn, vb, L, "at_fwd")
    x3_ = _matmul_res(o, w2["w_o"], x2, "at_out")
    w3 = fetch(3, (x3_,))
    dx4, a1, hm1, loss, dgf = _mlp_fwd(x3_, gl[1], w3["w_up1"], w3["w_down1"], "mlp1_fwd", head=(tgt, w["g_fin"]))

    dx3, da1, dob1, dgl1 = _mlp_bwd_dx(x3_, dx4, a1, gl[1], w3["w_up1"], w3["w_down1"], "mlp1_bwd_dx")
    dwu1, dwd1 = _mlp_bwd_dw(hm1, da1, a1, dob1, fb, "mlp1_bwd_dw")
    sent = send(3, dict(w_up1=dwu1, w_down1=dwd1))
    do, dx3b = _matmul_nt(dx3, w2["w_o"], "at_out_bwd", BF16, after=sent)
    dwo = _matmul_tn(o, dx3b[None], QKV_NB, "at_out_dw", blocked=False)
    dq, dk, dv = _attn_bwd(qn, kn, vb, do, o, L, "at_bwd")
    dqkv, dqg, dkg = _qk_prep_bwd(qkv, dq, dk, dv, w["qg"], w["kg"], cos, sin, L, "at_prep_bwd")
    dwqkv = _matmul_tn(dqkv, h1[None], QKV_NB, "at_qkv_dw", blocked=False)
    sent = send(2, dict(w_qkv=dwqkv, w_o=dwo))
    dx2, dgm1 = _nt_normbwd(dqkv[None], w2["w_qkv"], x2, gm[1], dx3, "at_qkv_bwd", after=sent, transposed=True)
    dx1, da0, dob0, dgl0 = _mlp_bwd_dx(x1, dx2, a0, gl[0], w1["w_up0"], w1["w_down0"], "mlp0_bwd_dx")
    dwu0, dwd0 = _mlp_bwd_dw(hm0, da0, a0, dob0, fb, "mlp0_bwd_dw")
    sent = send(1, dict(w_up0=dwu0, w_down0=dwd0))
    dyg, dx1b = _matmul_nt(dx1, w1["w_out"], "rg_out_bwd", F32, after=sent)
    dwout = _matmul_tn(yg, dx1b[None], QKV_NB, "rg_out_dw", blocked=False)
    dz, dwa, dwx, dpv = _rg_bwd(z, hf, hb, dyg, w0["pvec"], w["wa"], w["wx"], L, "rg_bwd")
    sent = send(4, dict(w_out=dwout, pvec=dpv, wa=dwa, wx=dwx))
    dwin = _matmul_tn(h0, dz, nb_in, "rg_in_dw", blocked=True, after=sent)
    sent = send(0, dict(w_in=dwin))
    dx0, dgm0 = _nt_normbwd(dz, w0["w_in"], x, gm[0], dx1, "rg_in_bwd", after=sent)
    send(-1, dict(g_mix=[dgm0, dgm1], g_mlp=[dgl0, dgl1], g_fin=dgf, conv_b=dpv[PV_CONV_B:PV_CONV_B + 1], qg=dqg, kg=dkg, loss=loss))
    return _deinterleave(dx0.reshape(Bl, L, D))


MESH = pl.DeviceIdType.MESH
ANY = pl.BlockSpec(memory_space=pl.ANY)
N_PEERS = N_DEV - 1


def _my_place():
    return lax.axis_index("x"), lax.axis_index("y"), lax.axis_index("c")


def _flat(px, py, pc):
    return 4 * px + 2 * py + pc


def _all_gather(shards, name):
    n = len(shards)

    def body(*refs):
        ins, outs = refs[:n], refs[n:2 * n]
        send_sems, recv_sems, local_sems = refs[2 * n:]
        x, y, c = _my_place()
        me, sibling = (x, y, c), (x, y, 1 - c)
        chips = [(1 - x, y), (x, 1 - y), (1 - x, 1 - y)]

        def copy(a, k, block, to, src=None):
            dst = outs[a].at[_flat(*block)]
            return pltpu.make_async_remote_copy(
                src_ref=dst if src is None else src, dst_ref=dst,
                send_sem=send_sems.at[a, k], recv_sem=recv_sems.at[a, k],
                device_id=to, device_id_type=MESH)

        mine = [pltpu.make_async_copy(ins[a], outs[a].at[_flat(*me)], local_sems.at[a]) for a in range(n)]
        for cp in mine:
            cp.start()
        first = []
        for a in range(n):
            first.append(copy(a, 0, me, sibling, src=ins[a]))
            first += [copy(a, 1 + j, me, (*chip, c), src=ins[a]) for j, chip in enumerate(chips)]
        for cp in first:
            cp.start()
        passed = []
        for j, chip in enumerate(chips):
            for a in range(n):
                copy(a, 1 + j, (*chip, c), me).wait_recv()
                fwd = copy(a, 4 + j, (*chip, c), sibling)
                fwd.start()
                passed.append(fwd)
        for a in range(n):
            copy(a, 0, sibling, me).wait_recv()
            for j, chip in enumerate(chips):
                copy(a, 4 + j, (*chip, 1 - c), me).wait_recv()
        for cp in first + passed:
            cp.wait_send()
        for cp in mine:
            cp.wait()

    return pl.pallas_call(
        body, name=name,
        in_specs=[ANY] * n, out_specs=[ANY] * n,
        out_shape=[S((N_DEV,) + s.shape, s.dtype) for s in shards],
        scratch_shapes=[pltpu.SemaphoreType.DMA((n, N_PEERS)), pltpu.SemaphoreType.DMA((n, N_PEERS)),
                        pltpu.SemaphoreType.DMA((n,))],
    )(*shards)


HBM = pl.BlockSpec(memory_space=pltpu.HBM)
SEM = pl.BlockSpec(memory_space=pltpu.SEMAPHORE)
SIDE_EFFECT = pltpu.SideEffectType.DATAFLOW_SIDE_EFFECTING
SEMS_PER_GROUP = 3


NEAR_PEERS = (1, 2, 4, 6)
FAR_CHIPS = (2, 4, 6)


def _exchange_copies(srcs, lands, sems, mode):
    send_sems, recv_sems, local_sems = sems
    scatter = mode == "scatter"
    x, y, c = _my_place()
    me = _flat(x, y, c)
    remote, local = [], []
    for a in range(len(srcs)):
        for r in (NEAR_PEERS if mode == "near" else range(1, N_DEV)):
            peer = (1 - x if r & 4 else x, 1 - y if r & 2 else y, 1 - c if r & 1 else c)
            remote.append(pltpu.make_async_remote_copy(
                src_ref=srcs[a].at[_flat(*peer)] if scatter else srcs[a], dst_ref=lands[a].at[me],
                send_sem=send_sems.at[a * N_PEERS + r - 1], recv_sem=recv_sems.at[a * N_PEERS + r - 1],
                device_id=peer, device_id_type=MESH))
        local.append(pltpu.make_async_copy(srcs[a].at[me] if scatter else srcs[a], lands[a].at[me], local_sems.at[a]))
    return remote, local


def _exchange_start(groups, modes, name):
    sizes = [len(g) for g in groups]
    srcs = [pltpu.with_memory_space_constraint(a, pltpu.HBM) for g in groups for a in g]
    n = len(srcs)
    scatter_of = [m == "scatter" for g, m in zip(groups, modes) for _ in g]
    lands = [pltpu.with_memory_space_constraint(lax.empty(a.shape if sc else (N_DEV,) + a.shape, a.dtype), pltpu.HBM)
             for a, sc in zip(srcs, scatter_of)]
    n_sem = SEMS_PER_GROUP * len(groups)

    def body(*refs):
        src_refs, land_refs, sem_refs, token = refs[:n], refs[n:2 * n], refs[2 * n:2 * n + n_sem], refs[-1]
        off = 0
        for gi, k in enumerate(sizes):
            remote, local = _exchange_copies(src_refs[off:off + k], land_refs[off:off + k],
                                             sem_refs[SEMS_PER_GROUP * gi:SEMS_PER_GROUP * (gi + 1)], modes[gi])
            for cp in local + remote:
                cp.start()
            off += k
        token[...] = jnp.zeros_like(token)

    sem_shapes = []
    for k in sizes:
        sem_shapes += [pltpu.SemaphoreType.DMA((k * N_PEERS,)), pltpu.SemaphoreType.DMA((k * N_PEERS,)),
                       pltpu.SemaphoreType.DMA((k,))]
    outs = pl.pallas_call(
        body, name=name,
        out_shape=sem_shapes + [pltpu.HBM(a.shape, a.dtype) for a in srcs + lands] + [S((8, 128), F32)],
        in_specs=[HBM] * (2 * n),
        out_specs=[SEM] * n_sem + [HBM] * (2 * n) + [pl.BlockSpec(memory_space=pltpu.VMEM)],
        input_output_aliases={i: n_sem + i for i in range(2 * n)},
        compiler_params=pltpu.CompilerParams(has_side_effects=SIDE_EFFECT),
    )(*srcs, *lands)
    sems, thru, token = outs[:n_sem], outs[n_sem:n_sem + 2 * n], outs[-1]
    per_group, off = [], 0
    for gi, k in enumerate(sizes):
        per_group.append((sems[SEMS_PER_GROUP * gi:SEMS_PER_GROUP * (gi + 1)], thru[off:off + k], thru[n + off:n + off + k]))
        off += k
    return per_group, token


def _exchange_wait(group, after, mode, name):
    sems, srcs, lands = group
    k = len(srcs)

    def body(*refs):
        remote, local = _exchange_copies(refs[:k], refs[k:2 * k], refs[2 * k:2 * k + SEMS_PER_GROUP], mode)
        for cp in remote:
            cp.wait_send()
            cp.wait_recv()
        for cp in local:
            cp.wait()

    outs = pl.pallas_call(
        body, name=name,
        out_shape=[pltpu.HBM(a.shape, a.dtype) for a in list(srcs) + list(lands)],
        in_specs=[HBM] * (2 * k) + [SEM] * SEMS_PER_GROUP + [ANY] * len(after),
        out_specs=[HBM] * (2 * k),
        input_output_aliases={i: i for i in range(2 * k)},
        compiler_params=pltpu.CompilerParams(has_side_effects=SIDE_EFFECT),
    )(*srcs, *lands, *sems, *after)
    return outs[k:]


def _forward_copies(lands, sems):
    send_sems, recv_sems = sems
    x, y, c = _my_place()
    mine, theirs = [], []
    for a in range(len(lands)):
        for k, r in enumerate(FAR_CHIPS):
            px, py = (1 - x if r & 4 else x), (1 - y if r & 2 else y)
            for out, core in ((mine, c), (theirs, 1 - c)):
                blk = lands[a].at[_flat(px, py, core)]
                out.append(pltpu.make_async_remote_copy(
                    src_ref=blk, dst_ref=blk, send_sem=send_sems.at[a * len(FAR_CHIPS) + k],
                    recv_sem=recv_sems.at[a * len(FAR_CHIPS) + k], device_id=(x, y, 1 - c), device_id_type=MESH))
    return mine, theirs


def _forward_start(groups, name):
    sizes = [len(g) for g in groups]
    lands = [a for g in groups for a in g]
    n = len(lands)
    n_sem = 2 * len(groups)

    def body(*refs):
        land_refs, sem_refs, token = refs[:n], refs[n:n + n_sem], refs[-1]
        off = 0
        for gi, k in enumerate(sizes):
            mine, _ = _forward_copies(land_refs[off:off + k], sem_refs[2 * gi:2 * gi + 2])
            for cp in mine:
                cp.start()
            off += k
        token[...] = jnp.zeros_like(token)

    sem_shapes = []
    for k in sizes:
        sem_shapes += [pltpu.SemaphoreType.DMA((k * len(FAR_CHIPS),))] * 2
    outs = pl.pallas_call(
        body, name=name,
        out_shape=sem_shapes + [pltpu.HBM(a.shape, a.dtype) for a in lands] + [S((8, 128), F32)],
        in_specs=[HBM] * n,
        out_specs=[SEM] * n_sem + [HBM] * n + [pl.BlockSpec(memory_space=pltpu.VMEM)],
        input_output_aliases={i: n_sem + i for i in range(n)},
        compiler_params=pltpu.CompilerParams(has_side_effects=SIDE_EFFECT),
    )(*lands)
    per_group, off = [], 0
    for gi, k in enumerate(sizes):
        per_group.append((outs[2 * gi:2 * gi + 2], outs[n_sem + off:n_sem + off + k]))
        off += k
    return per_group


def _forward_wait(group, after, name):
    sems, lands = group
    k = len(lands)

    def body(*refs):
        mine, theirs = _forward_copies(refs[:k], refs[k:k + 2])
        for cp in mine:
            cp.wait_send()
        for cp in theirs:
            cp.wait_recv()

    return pl.pallas_call(
        body, name=name,
        out_shape=[pltpu.HBM(a.shape, a.dtype) for a in lands],
        in_specs=[HBM] * k + [SEM] * 2 + [ANY] * len(after),
        out_specs=[HBM] * k,
        input_output_aliases={i: i for i in range(k)},
        compiler_params=pltpu.CompilerParams(has_side_effects=SIDE_EFFECT),
    )(*lands, *sems, *after)


def _row_tile(rows, cols):
    want = max(16, (128 * 1024) // cols)
    if rows <= want:
        return rows
    t = want - want % 16
    while rows % t:
        t -= 16
    return t


def _sum_parts(parts, name, after=()):
    P, R, C = parts.shape
    tr = _row_tile(R, C)

    def body(p_ref, *rest):
        o_ref = rest[-1]
        g = p_ref[0].astype(F32)
        for i in range(1, P):
            g = g + p_ref[i].astype(F32)
        o_ref[...] = g

    return pl.pallas_call(
        body, name=name, grid=(R // tr,),
        in_specs=[pl.BlockSpec((P, tr, C), lambda i: (0, i, 0))] + [ANY] * len(after),
        out_specs=pl.BlockSpec((tr, C), lambda i: (i, 0)),
        out_shape=S((R, C), F32),
        compiler_params=_cp("parallel"),
    )(parts, *after)


def _adamw(parts, w, m, v, name, after=()):
    P, R, C = parts.shape
    tr = _row_tile(R, C)
    c1 = 1.0 - ADAM_B1 ** ADAM_STEP
    c2 = 1.0 - ADAM_B2 ** ADAM_STEP

    def body(p_ref, w_ref, m_ref, v_ref, *rest):
        g_ref, d_ref, mo_ref, vo_ref = rest[len(after):]
        g = p_ref[0].astype(F32)
        for i in range(1, P):
            g = g + p_ref[i].astype(F32)
        mn = ADAM_B1 * m_ref[...] + (1.0 - ADAM_B1) * g
        vn = ADAM_B2 * v_ref[...] + (1.0 - ADAM_B2) * (g * g)
        g_ref[...] = g
        mo_ref[...] = mn
        vo_ref[...] = vn
        d_ref[...] = (-ADAM_LR) * ((mn / c1) / (jnp.sqrt(vn / c2) + ADAM_EPS) + ADAM_WD * w_ref[...])

    blk = pl.BlockSpec((tr, C), lambda i: (i, 0))
    return pl.pallas_call(
        body, name=name, grid=(R // tr,),
        in_specs=[pl.BlockSpec((P, tr, C), lambda i: (0, i, 0)), blk, blk, blk] + [ANY] * len(after),
        out_specs=[blk, blk, blk, blk],
        out_shape=[S((R, C), F32)] * 4,
        compiler_params=_cp("parallel"),
    )(parts, w, m, v, *after)


def _adamw_layer(parts, w3, m3, v3, layer, prev, name, after=()):
    P, R, C = parts.shape
    NL = w3.shape[0]
    tr = _row_tile(R, C)
    c1 = 1.0 - ADAM_B1 ** ADAM_STEP
    c2 = 1.0 - ADAM_B2 ** ADAM_STEP
    n_prev = 0 if prev is None else len(prev)

    def body(p_ref, w_ref, m_ref, v_ref, *rest):
        g_ref, d_ref, mo_ref, vo_ref = rest[n_prev + len(after):]
        g = p_ref[0].astype(F32)
        for i in range(1, P):
            g = g + p_ref[i].astype(F32)
        mn = ADAM_B1 * m_ref[...] + (1.0 - ADAM_B1) * g
        vn = ADAM_B2 * v_ref[...] + (1.0 - ADAM_B2) * (g * g)
        g_ref[...] = g
        mo_ref[...] = mn
        vo_ref[...] = vn
        d_ref[...] = (-ADAM_LR) * ((mn / c1) / (jnp.sqrt(vn / c2) + ADAM_EPS) + ADAM_WD * w_ref[...])

    blk = pl.BlockSpec((None, tr, C), lambda i: (layer, i, 0))
    return pl.pallas_call(
        body, name=name, grid=(R // tr,),
        in_specs=[pl.BlockSpec((P, tr, C), lambda i: (0, i, 0)), blk, blk, blk] + [ANY] * (n_prev + len(after)),
        out_specs=[blk, blk, blk, blk],
        out_shape=[S((NL, R, C), F32)] * 4,
        input_output_aliases={4 + k: k for k in range(n_prev)},
        compiler_params=_cp("parallel"),
    )(parts, w3, m3, v3, *(prev or ()), *after)


VMEM_WHOLE = pl.BlockSpec(memory_space=pltpu.VMEM)


def _pack_vectors(vectors, starts, rows, name):
    def body(*refs):
        o_ref = refs[-1]
        o_ref[...] = jnp.zeros_like(o_ref)
        for v_ref, r0 in zip(refs[:-1], starts):
            for j in range(v_ref.shape[1] // 128):
                o_ref[r0 + j:r0 + j + 1, :] = v_ref[:, j * 128:(j + 1) * 128]

    return pl.pallas_call(body, name=name, in_specs=[VMEM_WHOLE] * len(vectors), out_specs=VMEM_WHOLE,
                          out_shape=S((rows, 128), F32))(*vectors)


def _adamw_vectors(g_pack, params, keep_rows, name, after=()):
    n = len(params)
    P = g_pack.shape[0]
    c1 = 1.0 - ADAM_B1 ** ADAM_STEP
    c2 = 1.0 - ADAM_B2 ** ADAM_STEP

    def body(g_ref, *refs):
        ins, outs = refs[:3 * n], refs[3 * n + len(after):]
        gs = g_ref[0]
        for i in range(1, P):
            gs = gs + g_ref[i]
        for pi, (_, _, _, slots) in enumerate(params):
            w_ref, m_ref, v_ref = ins[3 * pi:3 * pi + 3]
            g_out, d_out, m_out, v_out = outs[4 * pi:4 * pi + 4]
            for idx, row in slots:
                g = gs[row:row + 1, :]
                mn = ADAM_B1 * m_ref[idx] + (1.0 - ADAM_B1) * g
                vn = ADAM_B2 * v_ref[idx] + (1.0 - ADAM_B2) * (g * g)
                g_out[idx] = g
                m_out[idx] = mn
                v_out[idx] = vn
                d_out[idx] = (-ADAM_LR) * ((mn / c1) / (jnp.sqrt(vn / c2) + ADAM_EPS) + ADAM_WD * w_ref[idx])
        outs[-1][...] = jnp.concatenate([gs[r:r + 1, :] for r in keep_rows], axis=0)

    flat = [a for w, m, v, _ in params for a in (w, m, v)]
    out_shape = [S(w.shape, F32) for w, _, _, _ in params for _ in range(4)] + [S((len(keep_rows), 128), F32)]
    outs = pl.pallas_call(
        body, name=name,
        in_specs=[VMEM_WHOLE] * (1 + len(flat)) + [ANY] * len(after),
        out_specs=[VMEM_WHOLE] * len(out_shape), out_shape=out_shape,
    )(g_pack, *flat, *after)
    return [outs[4 * i:4 * i + 4] for i in range(n)], outs[-1]


def _adamw_nd(parts, w, m, v, name, after=()):
    shp = w.shape
    C = shp[-1]
    outs = _adamw(parts.reshape(parts.shape[0], -1, C), w.reshape(-1, C), m.reshape(-1, C), v.reshape(-1, C), name, after)
    return [o.reshape(shp) for o in outs]


TILE_ROWS = 8


REP_SMALL_ROWS = 128
REP_GRAD_STARTS = (0, 8, 16, 24, 32, 40, 48, 56, 64)
REP_SMALL_STARTS = (0, 16, 32, 40, 48, 56)
REP_LOSS_ROW = 64


def _small_pack(cw, ba, bx, lam):
    pad8 = lambda a: jnp.pad(a, ((0, TILE_ROWS - a.shape[0]), (0, 0)))
    return jnp.concatenate([pad8(cw[0, :, 0, :]), pad8(ba[0]), pad8(bx[0]), pad8(lam[0]),
                            jnp.zeros((PV_ROWS - PV_CONV_B, LRU_BW), F32)], axis=0)


def kernel(x, norm_mix_g, norm_mlp_g, rg_w_in, rg_conv_w, rg_conv_b, rg_w_a, rg_b_a, rg_w_x, rg_b_x, rg_lam, rg_w_out, at_w_qkv, at_q_g, at_k_g, at_w_o, mlp_w_up, mlp_w_down, final_g, loss_target, m_norm_mix_g, m_norm_mlp_g, m_rg_w_in, m_rg_conv_w, m_rg_conv_b, m_rg_w_a, m_rg_b_a, m_rg_w_x, m_rg_b_x, m_rg_lam, m_rg_w_out, m_at_w_qkv, m_at_q_g, m_at_k_g, m_at_w_o, m_mlp_w_up, m_mlp_w_down, m_final_g, v_norm_mix_g, v_norm_mlp_g, v_rg_w_in, v_rg_conv_w, v_rg_conv_b, v_rg_w_a, v_rg_b_a, v_rg_w_x, v_rg_b_x, v_rg_lam, v_rg_w_out, v_at_w_qkv, v_at_q_g, v_at_k_g, v_at_w_o, v_mlp_w_up, v_mlp_w_down, v_final_g):
    D = x.shape[-1]
    bf = lambda a: a.astype(BF16)

    sp_w = _small_pack(rg_conv_w, rg_b_a, rg_b_x, rg_lam)
    started, _ = _exchange_start(
        [[bf(rg_w_in[0]), sp_w], [bf(rg_w_out[0])], [bf(mlp_w_up[0]), bf(mlp_w_down[0])],
         [bf(at_w_qkv[0]).T, bf(at_w_o[0])], [bf(mlp_w_up[1]), bf(mlp_w_down[1])]],
        ["near", "near", "near", "gather", "gather"], "gather_start")
    gathers = dict(zip((0, 1, 4, 2, 3), started))
    forwards = {}

    def fetch(stage, after):
        after = tuple(after)
        if stage == 0:
            got = _exchange_wait(gathers[0], after, "near", "gather_wait0")
            g_in, g_sp = _forward_wait(_forward_start([got], "forward_start0")[0], (), "forward_wait0")
            pvec = g_sp.transpose(1, 0, 2).reshape(PV_ROWS, D)
            pvec = jnp.concatenate([pvec[:PV_CONV_B], jnp.broadcast_to(rg_conv_b, (PV_ROWS - PV_CONV_B, D))], axis=0)
            return dict(w_in=g_in, pvec=pvec)
        if stage == 1:
            near = [_exchange_wait(gathers[s], after, "near", "gather_wait%d" % s) for s in (1, 4)]
            f_out, forwards[4] = _forward_start(near, "forward_start1")
            g_out, = _forward_wait(f_out, (), "forward_wait1")
            return dict(w_out=g_out.reshape(D, D))
        if stage == 4:
            g_up0, g_dn0 = _forward_wait(forwards[4], after, "forward_wait4")
            return dict(w_up0=g_up0, w_down0=g_dn0.reshape(-1, D))
        got = _exchange_wait(gathers[stage], after, "gather", "gather_wait%d" % stage)
        if stage == 2:
            return dict(w_qkv=got[0].reshape(-1, QKV_NB, D), w_o=got[1].reshape(D, D))
        return dict(w_up1=got[0], w_down1=got[1].reshape(-1, D))

    scatters = {}

    def send(stage, g):
        if stage == 3:
            arrs = [g["w_up1"], g["w_down1"].reshape(N_DEV, -1, D)]
        elif stage == 2:
            arrs = [g["w_qkv"].reshape(N_DEV, -1, D), g["w_o"].reshape(N_DEV, -1, D)]
        elif stage == 1:
            arrs = [g["w_up0"], g["w_down0"].reshape(N_DEV, -1, D)]
        elif stage == 4:
            arrs = [g["w_out"].reshape(N_DEV, -1, D), g["pvec"].reshape(PV_ROWS, N_DEV, LRU_BW).transpose(1, 0, 2),
                    bf(g["wa"]).reshape(N_DEV, -1, 128), bf(g["wx"]).reshape(N_DEV, -1, 128)]
        elif stage == 0:
            arrs = [g["w_in"]]
        else:
            small = _pack_vectors(g["g_mix"] + g["g_mlp"] + [g["g_fin"], g["conv_b"], g["qg"], g["kg"], g["loss"]],
                                  REP_GRAD_STARTS, REP_SMALL_ROWS, "pack_rep_small")
            arrs = [small.reshape(N_DEV, -1, 128)]
        (group,), token = _exchange_start([arrs], ["scatter"], "scatter_start%d" % (stage % 6))
        scatters[stage] = (group, token)
        return (token,)

    w = dict(g_mix=norm_mix_g, g_mlp=norm_mlp_g, g_fin=final_g[None], qg=at_q_g, kg=at_k_g,
             wa=bf(rg_w_a[0]), wx=bf(rg_w_x[0]))
    grad_x = _local_step(x, loss_target, w, fetch, send)

    res = {}
    r_up1, r_dn1 = _exchange_wait(scatters[3][0], (scatters[-1][1],), "scatter", "scatter_wait3")
    r_out, r_sp, r_wa, r_wx = _exchange_wait(scatters[4][0], (r_up1,), "scatter", "scatter_wait4")
    wa_part = _sum_parts(r_wa, "reduce_w_a")
    wx_part = _sum_parts(r_wx, "reduce_w_x", (wa_part,))
    (rep_gather,), rep_token = _exchange_start([[wa_part, wx_part]], ["gather"], "rep_gather_start")
    up = _adamw_layer(r_up1, mlp_w_up, m_mlp_w_up, v_mlp_w_up, 1, None, "adam_mlp_w_up1", after=(rep_token,))
    dn = _adamw_layer(r_dn1, mlp_w_down, m_mlp_w_down, v_mlp_w_down, 1, None, "adam_mlp_w_down1", after=(up[1],))
    r_qkv, r_o = _exchange_wait(scatters[2][0], (dn[1],), "scatter", "scatter_wait2")
    tr = lambda a: a[0].T
    qkv_t = _adamw_nd(r_qkv, tr(at_w_qkv), tr(m_at_w_qkv), tr(v_at_w_qkv), "adam_at_w_qkv")
    res["at_w_qkv"] = [o.T[None] for o in qkv_t]
    res["at_w_o"] = _adamw_nd(r_o[:, None], at_w_o, m_at_w_o, v_at_w_o, "adam_at_w_o", (qkv_t[1],))
    r_up0, r_dn0 = _exchange_wait(scatters[1][0], (res["at_w_o"][1],), "scatter", "scatter_wait1")
    res["mlp_w_up"] = _adamw_layer(r_up0, mlp_w_up, m_mlp_w_up, v_mlp_w_up, 0, up, "adam_mlp_w_up0")
    res["mlp_w_down"] = _adamw_layer(r_dn0, mlp_w_down, m_mlp_w_down, v_mlp_w_down, 0, dn, "adam_mlp_w_down0",
                                     after=(res["mlp_w_up"][1],))
    r_in, = _exchange_wait(scatters[0][0], (res["mlp_w_down"][1],), "scatter", "scatter_wait0")
    res["rg_w_in"] = _adamw_nd(r_in[:, None], rg_w_in, m_rg_w_in, v_rg_w_in, "adam_rg_w_in")
    res["rg_w_out"] = _adamw_nd(r_out[:, None], rg_w_out, m_rg_w_out, v_rg_w_out, "adam_rg_w_out", (res["rg_w_in"][1],))
    whole, lane = slice(None), slice(0, 1)
    two_rows = lambda r0: [((0, slice(d, d + 1), whole), r0 + d) for d in range(2)]
    (res["rg_conv_w"], res["rg_b_a"], res["rg_b_x"], res["rg_lam"]), _ = _adamw_vectors(
        r_sp, [(rg_conv_w, m_rg_conv_w, v_rg_conv_w, [((0, t, lane, whole), PV_CONV_W + t) for t in range(CONV_W)]),
               (rg_b_a, m_rg_b_a, v_rg_b_a, two_rows(PV_B_A)), (rg_b_x, m_rg_b_x, v_rg_b_x, two_rows(PV_B_X)),
               (rg_lam, m_rg_lam, v_rg_lam, two_rows(PV_LAM))], [0], "adam_small", (res["rg_w_out"][1],))

    r_small, = _exchange_wait(scatters[-1][0], (res["rg_lam"][1],), "scatter", "scatter_wait5")
    small_sum, = _all_gather([_sum_parts(r_small, "reduce_rep_small")], "gather_replicated")
    wa_sum, wx_sum = _exchange_wait(rep_gather, (small_sum,), "gather", "rep_gather_wait")
    rows = lambda a: a.reshape(-1, 128)
    wa_res = _adamw(wa_sum.reshape(1, -1, 128), rows(rg_w_a), rows(m_rg_w_a), rows(v_rg_w_a), "adam_rg_w_a")
    wx_res = _adamw(wx_sum.reshape(1, -1, 128), rows(rg_w_x), rows(m_rg_w_x), rows(v_rg_w_x), "adam_rg_w_x", (wa_res[1],))
    res["rg_w_a"] = [o.reshape(rg_w_a.shape) for o in wa_res]
    res["rg_w_x"] = [o.reshape(rg_w_x.shape) for o in wx_res]
    def vec_slots(a, r0):
        per = a.shape[1] // 128
        return [((slice(l, l + 1), slice(128 * j, 128 * (j + 1))), r0 + l * per + j) for l in range(a.shape[0]) for j in range(per)]

    fin = [final_g[None], m_final_g[None], v_final_g[None]]
    vecs = [(norm_mix_g, m_norm_mix_g, v_norm_mix_g), (norm_mlp_g, m_norm_mlp_g, v_norm_mlp_g), fin,
            (rg_conv_b, m_rg_conv_b, v_rg_conv_b), (at_q_g, m_at_q_g, v_at_q_g), (at_k_g, m_at_k_g, v_at_k_g)]
    outs, kept = _adamw_vectors(
        small_sum.reshape(1, -1, 128),
        [(wv, mv, vv, vec_slots(wv, r0)) for (wv, mv, vv), r0 in zip(vecs, REP_SMALL_STARTS)], [REP_LOSS_ROW], "adam_rep_small", (wx_res[1],))
    for nm, o in zip(["norm_mix_g", "norm_mlp_g", "final_g", "rg_conv_b", "at_q_g", "at_k_g"], outs):
        res[nm] = [a[0] for a in o] if nm == "final_g" else o
    loss = kept[0, 0]

    order = ["norm_mix_g", "norm_mlp_g", "rg_w_in", "rg_conv_w", "rg_conv_b", "rg_w_a", "rg_b_a", "rg_w_x", "rg_b_x",
             "rg_lam", "rg_w_out", "at_w_qkv", "at_q_g", "at_k_g", "at_w_o", "mlp_w_up", "mlp_w_down", "final_g"]
    return (loss, grad_x, *[res[nm][k] for k in range(4) for nm in order])
```

```python
import functools
import math

import jax
import jax.numpy as jnp
from jax import lax
from jax.experimental import pallas as pl
from jax.experimental.pallas import tpu as pltpu

F32 = jnp.float32
BF16 = jnp.bfloat16
S = jax.ShapeDtypeStruct

EPS = 1e-6
HEAD_DIM = 128
N_KV = 2
GRID_W = 64
ROPE_THETA = 10000.0
LRU_BW = 128
RG_C = 8.0
CONV_W = 4
N_DEV = 8
N_SEG = 8
SCAN_UNROLL = 8
TN_STEP_COLS = 512
PROJ_TM = 1024
MLP_TM = 512
VMEM_LIMIT_V7X = 56 * 1024 * 1024
SOFTMAX_SCALE = 1.0 / math.sqrt(HEAD_DIM)
GELU_K = math.sqrt(2.0 / math.pi)
GELU_C = 0.044715

ADAM_LR = 0.001
ADAM_B1 = 0.9
ADAM_B2 = 0.999
ADAM_EPS = 1e-08
ADAM_WD = 0.01
ADAM_STEP = 10

NT = (((1,), (1,)), ((), ()))
TN = (((0,), (0,)), ((), ()))


def _cp(*sem):
    return pltpu.CompilerParams(dimension_semantics=sem, vmem_limit_bytes=VMEM_LIMIT_V7X)


def _rms_r(xv):
    return lax.rsqrt(jnp.mean(xv * xv, axis=-1, keepdims=True) + EPS)


def _rms_bwd(dh, xv, g):
    r = _rms_r(xv)
    xh = xv * r
    dg = jnp.sum(dh * xh, axis=0, keepdims=True)
    dxh = dh * g
    dx = r * (dxh - xh * jnp.mean(dxh * xh, axis=-1, keepdims=True))
    return dx, dg


def _dot(a, b):
    return jnp.dot(a, b, preferred_element_type=F32)


def _dot_nt(a, b):
    return lax.dot_general(a, b, NT, preferred_element_type=F32)


def _dot_tn(a, b):
    return lax.dot_general(a, b, TN, preferred_element_type=F32)


def _norm_matmul(x, g, wblk, name, out_dtype=F32, transposed=False):
    T, D = x.shape
    NB, nb = wblk.shape[0], wblk.shape[1 if transposed else 2]
    mm = _dot_nt if transposed else _dot
    tm = min(T, PROJ_TM)

    def body(x_ref, g_ref, w_ref, o_ref, h_ref):
        xv = x_ref[...]
        hb = (xv * _rms_r(xv) * g_ref[...]).astype(BF16)
        h_ref[...] = hb
        for q in range(NB):
            o_ref[:, q * nb:(q + 1) * nb] = mm(hb, w_ref[q]).astype(o_ref.dtype)

    return pl.pallas_call(
        body, name=name, grid=(T // tm,),
        in_specs=[pl.BlockSpec((tm, D), lambda i: (i, 0)),
                  pl.BlockSpec((1, D), lambda i: (0, 0)),
                  pl.BlockSpec(wblk.shape, lambda i: (0, 0, 0))],
        out_specs=[pl.BlockSpec((tm, NB * nb), lambda i: (i, 0)),
                   pl.BlockSpec((tm, D), lambda i: (i, 0))],
        out_shape=[S((T, NB * nb), out_dtype), S((T, D), BF16)],
        compiler_params=_cp("parallel"),
    )(x, g, wblk)


def _matmul_res(a, w, res, name):
    T, K = a.shape
    N = w.shape[1]
    tm = min(T, PROJ_TM)

    def body(a_ref, w_ref, r_ref, o_ref):
        o_ref[...] = r_ref[...] + _dot(a_ref[...], w_ref[...])

    return pl.pallas_call(
        body, name=name, grid=(T // tm,),
        in_specs=[pl.BlockSpec((tm, K), lambda i: (i, 0)),
                  pl.BlockSpec((K, N), lambda i: (0, 0)),
                  pl.BlockSpec((tm, N), lambda i: (i, 0))],
        out_specs=pl.BlockSpec((tm, N), lambda i: (i, 0)),
        out_shape=S((T, N), F32),
        compiler_params=_cp("parallel"),
    )(a, w, res)


def _matmul_nt(a, w, name, out_dtype, after=()):
    T, N = a.shape
    K = w.shape[0]
    tm = min(T, PROJ_TM)

    def body(a_ref, w_ref, *rest):
        o_ref, ab_ref = rest[len(after):]
        ab = a_ref[...].astype(BF16)
        ab_ref[...] = ab
        o_ref[...] = _dot_nt(ab, w_ref[...]).astype(o_ref.dtype)

    return pl.pallas_call(
        body, name=name, grid=(T // tm,),
        in_specs=[pl.BlockSpec((tm, N), lambda i: (i, 0)),
                  pl.BlockSpec((K, N), lambda i: (0, 0))] + [pl.BlockSpec(memory_space=pl.ANY)] * len(after),
        out_specs=[pl.BlockSpec((tm, K), lambda i: (i, 0)),
                   pl.BlockSpec((tm, N), lambda i: (i, 0))],
        out_shape=[S((T, K), out_dtype), S((T, N), BF16)],
        compiler_params=_cp("parallel"),
    )(a, w, *after)


def _matmul_tn(a, b3, nb, name, blocked, after=()):
    T, M = a.shape
    SB, _, N = b3.shape
    per = N // nb
    NB = SB * per
    tk = min(T, 1024)
    nk = T // tk
    jb = max(1, TN_STEP_COLS // nb) if blocked else 1
    assert per % jb == 0
    if blocked:
        out_spec, out_shape = pl.BlockSpec((jb, M, nb), lambda j: (j, 0, 0)), S((NB, M, nb), BF16)
    else:
        assert SB == 1
        out_spec, out_shape = pl.BlockSpec((M, nb), lambda j: (0, j)), S((M, N), BF16)

    def body(a_ref, b_ref, *rest):
        o_ref = rest[-1]
        for q in range(jb):
            acc = None
            for k in range(nk):
                rows = slice(k * tk, (k + 1) * tk)
                part = _dot_tn(a_ref[rows, :], b_ref[rows, q * nb:(q + 1) * nb])
                acc = part if acc is None else acc + part
            if blocked:
                o_ref[q] = acc.astype(BF16)
            else:
                o_ref[...] = acc.astype(BF16)

    return pl.pallas_call(
        body, name=name, grid=(NB // jb,),
        in_specs=[pl.BlockSpec((T, M), lambda j: (0, 0), pipeline_mode=pl.Buffered(1)),
                  pl.BlockSpec((None, T, jb * nb), lambda j: ((j * jb) // per, 0, ((j * jb) % per) // jb))]
        + [pl.BlockSpec(memory_space=pl.ANY)] * len(after),
        out_specs=out_spec,
        out_shape=out_shape,
        compiler_params=_cp("parallel"),
    )(a, b3, *after)


def _nt_normbwd(dz3, wblk, x, g, dres, name, after=(), transposed=False):
    T, D = x.shape
    NB, nb = wblk.shape[0], wblk.shape[1 if transposed else 2]
    mm = _dot if transposed else _dot_nt
    SB, _, N = dz3.shape
    per = N // nb
    tm = min(T, PROJ_TM)

    def body(dz_ref, w_ref, x_ref, g_ref, dr_ref, *rest):
        dx_ref, dg_ref = rest[len(after):]

        @pl.when(pl.program_id(0) == 0)
        def _():
            dg_ref[...] = jnp.zeros_like(dg_ref)

        dh = None
        for q in range(NB):
            cols = slice((q % per) * nb, (q % per + 1) * nb)
            part = mm(dz_ref[q // per, :, cols], w_ref[q])
            dh = part if dh is None else dh + part
        dx, dg = _rms_bwd(dh, x_ref[...], g_ref[...])
        dx_ref[...] = dr_ref[...] + dx
        dg_ref[...] += dg

    return pl.pallas_call(
        body, name=name, grid=(T // tm,),
        in_specs=[pl.BlockSpec((SB, tm, N), lambda i: (0, i, 0)),
                  pl.BlockSpec(wblk.shape, lambda i: (0, 0, 0)),
                  pl.BlockSpec((tm, D), lambda i: (i, 0)),
                  pl.BlockSpec((1, D), lambda i: (0, 0)),
                  pl.BlockSpec((tm, D), lambda i: (i, 0))] + [pl.BlockSpec(memory_space=pl.ANY)] * len(after),
        out_specs=[pl.BlockSpec((tm, D), lambda i: (i, 0)),
                   pl.BlockSpec((1, D), lambda i: (0, 0))],
        out_shape=[S((T, D), F32), S((1, D), F32)],
        compiler_params=_cp("arbitrary"),
    )(dz3, wblk, x, g, dres, *after)


def _loss_head(xv, tv, gv, D):
    err = xv * _rms_r(xv) * gv - tv
    e2 = jnp.sum(jnp.sum(err * err, axis=-1, keepdims=True), axis=0, keepdims=True)
    dx, dg = _rms_bwd(err * (1.0 / D), xv, gv)
    return (0.5 / D) * e2, dx, dg


def _mlp_fwd(x, g, wup, wdown, name, head=None):
    T, D = x.shape
    NB, _, fb = wup.shape
    tm = min(T, MLP_TM)
    n_head = 0 if head is None else 2

    def body(x_ref, g_ref, wu_ref, wd_ref, *rest):
        xo_ref, a_ref, h_ref = rest[n_head:n_head + 3]
        xv = x_ref[...]
        hb = (xv * _rms_r(xv) * g_ref[...]).astype(BF16)
        h_ref[...] = hb
        acc = xv
        for j in range(NB):
            a = _dot(hb, wu_ref[j])
            a_ref[:, j * fb:(j + 1) * fb] = a.astype(BF16)
            u = jnp.maximum(a, 0.0)
            acc = acc + _dot((u * u).astype(BF16), wd_ref[j * fb:(j + 1) * fb, :])

        if head is None:
            xo_ref[...] = acc
        else:
            t_ref, gf_ref = rest[:2]
            loss_ref, dgf_ref = rest[n_head + 3:n_head + 5]

            @pl.when(pl.program_id(0) == 0)
            def _():
                loss_ref[...] = jnp.zeros_like(loss_ref)
                dgf_ref[...] = jnp.zeros_like(dgf_ref)

            e2, dx, dg = _loss_head(acc, t_ref[...], gf_ref[...], D)
            xo_ref[...] = dx
            loss_ref[...] += e2
            dgf_ref[...] += dg

    row = pl.BlockSpec((tm, D), lambda i: (i, 0))
    vec = pl.BlockSpec((1, D), lambda i: (0, 0))
    once = pl.Buffered(1)
    in_specs = [row, vec, pl.BlockSpec((NB, D, fb), lambda i: (0, 0, 0), pipeline_mode=once),
                pl.BlockSpec((NB * fb, D), lambda i: (0, 0), pipeline_mode=once)]
    out_specs = [row, pl.BlockSpec((tm, NB * fb), lambda i: (i, 0)), row]
    out_shape = [S((T, D), F32), S((T, NB * fb), BF16), S((T, D), BF16)]
    if head is not None:
        in_specs += [row, vec]
        out_specs += [pl.BlockSpec((1, 128), lambda i: (0, 0)), vec]
        out_shape += [S((1, 128), F32), S((1, D), F32)]
    return pl.pallas_call(
        body, name=name, grid=(T // tm,),
        in_specs=in_specs, out_specs=out_specs, out_shape=out_shape,
        compiler_params=_cp("parallel" if head is None else "arbitrary"),
    )(x, g, wup, wdown, *(head or ()))


def _mlp_bwd_dx(x, dout, a, g, wup, wdown, name):
    T, D = x.shape
    NB, _, fb = wup.shape
    tm = min(T, MLP_TM)

    def body(x_ref, do_ref, a_ref, g_ref, wu_ref, wd_ref, dx_ref, da_ref, dob_ref, dg_ref):
        @pl.when(pl.program_id(0) == 0)
        def _():
            dg_ref[...] = jnp.zeros_like(dg_ref)

        dov = do_ref[...]
        dob = dov.astype(BF16)
        dob_ref[...] = dob
        dh = None
        for j in range(NB):
            cols = slice(j * fb, (j + 1) * fb)
            du2 = _dot_nt(dob, wd_ref[cols, :])
            u = jnp.maximum(a_ref[:, cols].astype(F32), 0.0)
            da = (du2 * (2.0 * u)).astype(BF16)
            da_ref[:, cols] = da
            part = _dot_nt(da, wu_ref[j])
            dh = part if dh is None else dh + part
        dx, dg = _rms_bwd(dh, x_ref[...], g_ref[...])
        dx_ref[...] = dov + dx
        dg_ref[...] += dg

    row = pl.BlockSpec((tm, D), lambda i: (i, 0))
    wide = pl.BlockSpec((tm, NB * fb), lambda i: (i, 0))
    vec = pl.BlockSpec((1, D), lambda i: (0, 0))
    once = pl.Buffered(1)
    return pl.pallas_call(
        body, name=name, grid=(T // tm,),
        in_specs=[row, row, wide, vec, pl.BlockSpec((NB, D, fb), lambda i: (0, 0, 0), pipeline_mode=once),
                  pl.BlockSpec((NB * fb, D), lambda i: (0, 0), pipeline_mode=once)],
        out_specs=[row, wide, row, vec],
        out_shape=[S((T, D), F32), S((T, NB * fb), BF16), S((T, D), BF16), S((1, D), F32)],
        compiler_params=_cp("arbitrary"),
    )(x, dout, a, g, wup, wdown)


def _mlp_bwd_dw(h, da, a, dob, fb, name):
    T, D = h.shape
    F = a.shape[1]
    NB = F // fb
    tk = min(T, 1024)
    nk = T // tk

    def body(h_ref, da_ref, a_ref, dob_ref, dwu_ref, dwd_ref):
        au = ad = None
        for k in range(nk):
            rows = slice(k * tk, (k + 1) * tk)
            pu = _dot_tn(h_ref[rows, :], da_ref[rows, :])
            u = jnp.maximum(a_ref[rows, :].astype(F32), 0.0)
            pd = _dot_tn((u * u).astype(BF16), dob_ref[rows, :])
            au, ad = (pu, pd) if au is None else (au + pu, ad + pd)
        dwu_ref[...] = au.astype(BF16)
        dwd_ref[...] = ad.astype(BF16)

    once = pl.Buffered(1)
    return pl.pallas_call(
        body, name=name, grid=(NB,),
        in_specs=[pl.BlockSpec((T, D), lambda j: (0, 0), pipeline_mode=once),
                  pl.BlockSpec((T, fb), lambda j: (0, j)),
                  pl.BlockSpec((T, fb), lambda j: (0, j)),
                  pl.BlockSpec((T, D), lambda j: (0, 0), pipeline_mode=once)],
        out_specs=[pl.BlockSpec((None, D, fb), lambda j: (j, 0, 0)),
                   pl.BlockSpec((fb, D), lambda j: (j, 0))],
        out_shape=[S((NB, D, fb), BF16), S((F, D), BF16)],
        compiler_params=_cp("parallel"),
    )(h, da, a, dob)


def _rope_tables(L):
    nf = HEAD_DIM // 4
    t = jnp.arange(L, dtype=jnp.int32)
    row = (t // GRID_W).astype(F32)
    col = (t % GRID_W).astype(F32)
    inv = ROPE_THETA ** (-jnp.arange(nf, dtype=F32) / nf)
    ar = row[:, None] * inv
    ac = col[:, None] * inv
    cos = jnp.concatenate([jnp.cos(ar), jnp.cos(ar), jnp.cos(ac), jnp.cos(ac)], axis=-1)
    sin = jnp.concatenate([-jnp.sin(ar), jnp.sin(ar), -jnp.sin(ac), jnp.sin(ac)], axis=-1)
    return cos, sin


def _swap32(x):
    lane = lax.broadcasted_iota(jnp.int32, x.shape, 1)
    up = pltpu.roll(x, HEAD_DIM - 32, 1)
    down = pltpu.roll(x, 32, 1)
    return jnp.where((lane % 64) < 32, up, down)


def _qk_prep(qkv, qg, kg, cos, sin, L, name):
    T, W = qkv.shape
    nh = W // HEAD_DIM - 2 * N_KV
    tm = min(L, 512)
    lb = L // tm

    def body(qkv_ref, qg_ref, kg_ref, cos_ref, sin_ref, q_ref, k_ref, v_ref):
        c = cos_ref[...]
        s = sin_ref[...]
        for h in range(nh + N_KV):
            xh = qkv_ref[:, h * HEAD_DIM:(h + 1) * HEAD_DIM]
            gv = qg_ref[...] if h < nh else kg_ref[...]
            y = xh * _rms_r(xh) * gv
            y = (y * c + _swap32(y) * s).astype(BF16)
            if h < nh:
                q_ref[:, h * HEAD_DIM:(h + 1) * HEAD_DIM] = y
            else:
                k_ref[:, (h - nh) * HEAD_DIM:(h - nh + 1) * HEAD_DIM] = y
        v_ref[...] = qkv_ref[:, (nh + N_KV) * HEAD_DIM:].astype(BF16)

    return pl.pallas_call(
        body, name=name, grid=(T // tm,),
        in_specs=[pl.BlockSpec((tm, W), lambda i: (i, 0)),
                  pl.BlockSpec((1, HEAD_DIM), lambda i: (0, 0)),
                  pl.BlockSpec((1, HEAD_DIM), lambda i: (0, 0)),
                  pl.BlockSpec((tm, HEAD_DIM), lambda i: (i % lb, 0)),
                  pl.BlockSpec((tm, HEAD_DIM), lambda i: (i % lb, 0))],
        out_specs=[pl.BlockSpec((tm, nh * HEAD_DIM), lambda i: (i, 0)),
                   pl.BlockSpec((tm, N_KV * HEAD_DIM), lambda i: (i, 0)),
                   pl.BlockSpec((tm, N_KV * HEAD_DIM), lambda i: (i, 0))],
        out_shape=[S((T, nh * HEAD_DIM), BF16), S((T, N_KV * HEAD_DIM), BF16), S((T, N_KV * HEAD_DIM), BF16)],
        compiler_params=_cp("parallel"),
    )(qkv, qg, kg, cos, sin)


def _qk_prep_bwd(qkv, dq, dk, dv, qg, kg, cos, sin, L, name):
    T, W = qkv.shape
    nh = W // HEAD_DIM - 2 * N_KV
    tm = min(L, 512)
    lb = L // tm

    def body(qkv_ref, dq_ref, dk_ref, dv_ref, qg_ref, kg_ref, cos_ref, sin_ref, dz_ref, dqg_ref, dkg_ref):
        @pl.when(pl.program_id(0) == 0)
        def _():
            dqg_ref[...] = jnp.zeros_like(dqg_ref)
            dkg_ref[...] = jnp.zeros_like(dkg_ref)

        c = cos_ref[...]
        s = sin_ref[...]
        for h in range(nh + N_KV):
            cols = slice(h * HEAD_DIM, (h + 1) * HEAD_DIM)
            if h < nh:
                dout, gv, dg_ref = dq_ref[:, cols], qg_ref[...], dqg_ref
            else:
                kc = slice((h - nh) * HEAD_DIM, (h - nh + 1) * HEAD_DIM)
                dout, gv, dg_ref = dk_ref[:, kc], kg_ref[...], dkg_ref
            dy = dout * c - _swap32(dout) * s
            dx, dg = _rms_bwd(dy, qkv_ref[:, cols], gv)
            dg_ref[...] += dg
            dz_ref[:, cols] = dx.astype(BF16)
        dz_ref[:, (nh + N_KV) * HEAD_DIM:] = dv_ref[...].astype(BF16)

    return pl.pallas_call(
        body, name=name, grid=(T // tm,),
        in_specs=[pl.BlockSpec((tm, W), lambda i: (i, 0)),
                  pl.BlockSpec((tm, nh * HEAD_DIM), lambda i: (i, 0)),
                  pl.BlockSpec((tm, N_KV * HEAD_DIM), lambda i: (i, 0)),
                  pl.BlockSpec((tm, N_KV * HEAD_DIM), lambda i: (i, 0)),
                  pl.BlockSpec((1, HEAD_DIM), lambda i: (0, 0)),
                  pl.BlockSpec((1, HEAD_DIM), lambda i: (0, 0)),
                  pl.BlockSpec((tm, HEAD_DIM), lambda i: (i % lb, 0)),
                  pl.BlockSpec((tm, HEAD_DIM), lambda i: (i % lb, 0))],
        out_specs=[pl.BlockSpec((tm, W), lambda i: (i, 0)),
                   pl.BlockSpec((1, HEAD_DIM), lambda i: (0, 0)),
                   pl.BlockSpec((1, HEAD_DIM), lambda i: (0, 0))],
        out_shape=[S((T, W), BF16), S((1, HEAD_DIM), F32), S((1, HEAD_DIM), F32)],
        compiler_params=_cp("arbitrary"),
    )(qkv, dq, dk, dv, qg, kg, cos, sin)


EXP2_SCALE = SOFTMAX_SCALE * math.log2(math.e)
ATTN_SUB = 256
ATTN_TQ = 1024


def _softmax_rows(q, k):
    s = _dot_nt(q, k)
    e = jnp.exp2((s - jnp.max(s, axis=-1, keepdims=True)) * EXP2_SCALE)
    return e, jnp.sum(e, axis=-1, keepdims=True)


def _attn_fwd(q, k, v, L, name):
    T = q.shape[0]
    nh = q.shape[1] // HEAD_DIM
    G = nh // N_KV
    B = T // L
    tq = min(L, ATTN_TQ)
    nq = L // tq
    sub = min(tq, ATTN_SUB)

    def body(q_ref, k_ref, v_ref, o_ref):
        for h in range(tq // sub):
            rows = slice(h * sub, (h + 1) * sub)
            e, l = _softmax_rows(q_ref[rows, :], k_ref[...])
            o_ref[rows, :] = (_dot(e.astype(BF16), v_ref[...]) / l).astype(BF16)

    qspec = pl.BlockSpec((tq, HEAD_DIM), lambda b, kv, g, qi: (b * nq + qi, kv * G + g))
    kspec = pl.BlockSpec((L, HEAD_DIM), lambda b, kv, g, qi: (b, kv))
    return pl.pallas_call(
        body, name=name, grid=(B, N_KV, G, nq),
        in_specs=[qspec, kspec, kspec],
        out_specs=qspec,
        out_shape=S((T, nh * HEAD_DIM), BF16),
        compiler_params=_cp("parallel", "parallel", "parallel", "parallel"),
    )(q, k, v)


def _attn_bwd(q, k, v, do, o, L, name):
    T = q.shape[0]
    nh = q.shape[1] // HEAD_DIM
    G = nh // N_KV
    B = T // L
    tq = min(L, ATTN_TQ)
    nq = L // tq

    sub = min(tq, ATTN_SUB)

    def body(q_ref, k_ref, v_ref, do_ref, o_ref, dq_ref, dk_ref, dv_ref, ds_scr, p_scr):
        first = (pl.program_id(2) == 0) & (pl.program_id(3) == 0)
        last = (pl.program_id(2) == G - 1) & (pl.program_id(3) == nq - 1)

        @pl.when(first)
        def _():
            dk_ref[...] = jnp.zeros_like(dk_ref)
            dv_ref[...] = jnp.zeros_like(dv_ref)

        for h in range(tq // sub):
            rows = slice(h * sub, (h + 1) * sub)
            dov = do_ref[rows, :]
            e, l = _softmax_rows(q_ref[rows, :], k_ref[...])
            p = e * (1.0 / l)
            dsum = jnp.sum(dov.astype(F32) * o_ref[rows, :].astype(F32), axis=-1, keepdims=True)
            ds_scr[rows, :] = (p * (_dot_nt(dov, v_ref[...]) - dsum)).astype(BF16)
            p_scr[rows, :] = p.astype(BF16)
        ds = ds_scr[...]
        dq_ref[...] = _dot(ds, k_ref[...]) * SOFTMAX_SCALE
        dk_ref[...] += _dot_tn(ds, q_ref[...])
        dv_ref[...] += _dot_tn(p_scr[...], do_ref[...])

        @pl.when(last)
        def _():
            dk_ref[...] = dk_ref[...] * SOFTMAX_SCALE

    qspec = pl.BlockSpec((tq, HEAD_DIM), lambda b, kv, g, qi: (b * nq + qi, kv * G + g))
    kspec = pl.BlockSpec((L, HEAD_DIM), lambda b, kv, g, qi: (b, kv))
    return pl.pallas_call(
        body, name=name, grid=(B, N_KV, G, nq),
        in_specs=[qspec, kspec, kspec, qspec, qspec],
        out_specs=[qspec, kspec, kspec],
        out_shape=[S((T, nh * HEAD_DIM), F32), S((T, N_KV * HEAD_DIM), F32), S((T, N_KV * HEAD_DIM), F32)],
        scratch_shapes=[pltpu.VMEM((tq, L), BF16), pltpu.VMEM((tq, L), BF16)],
        compiler_params=_cp("parallel", "parallel", "arbitrary", "arbitrary"),
    )(q, k, v, do, o)


PV_CONV_W = 0
PV_B_A = 8
PV_B_X = 16
PV_LAM = 24
PV_CONV_B = 32
PV_ROWS = 40


def _shift_rows(x, k):
    if k == 0:
        return x
    L = x.shape[0]
    n = N_SEG * abs(k)
    seg = lax.broadcasted_iota(jnp.int32, (n, x.shape[1]), 0) % N_SEG
    if k > 0:
        edge = jnp.where(seg == 0, 0.0, pltpu.roll(x[L - n:], 1, 0))
        return jnp.concatenate([edge, x[:L - n]], axis=0)
    edge = jnp.where(seg == N_SEG - 1, 0.0, pltpu.roll(x[:n], n - 1, 0))
    return jnp.concatenate([x[n:], edge], axis=0)


def _conv_taps(rec, pv):
    c = pv[PV_CONV_B:PV_CONV_B + 1]
    for j in range(CONV_W):
        c = c + pv[PV_CONV_W + j:PV_CONV_W + j + 1] * _shift_rows(rec, 2 - j)
    return c


def _sigmoid(x):
    return 0.5 * jnp.tanh(0.5 * x) + 0.5


EXPM1_SERIES_BELOW = 0.03


def _rg_gates(c, cbf, wa, wx, ba, bx, lam):
    r = _sigmoid(_dot(cbf, wa) + ba)
    i = _sigmoid(_dot(cbf, wx) + bx)
    sp = jnp.maximum(-lam, 0.0) + jnp.log1p(jnp.exp(-jnp.abs(lam)))
    la = r * ((-RG_C) * sp)
    a = jnp.exp(la)
    a2 = a * a
    x = la + la
    series = -(x * ((x * (1.0 / 6.0) + 0.5) * x + 1.0))
    om = jnp.where(x > -EXPM1_SERIES_BELOW, series, 1.0 - a2)
    rm = lax.rsqrt(om)
    return r, i, a, om * rm, rm, a2, sp


def _gelu(x):
    t = jnp.tanh(GELU_K * (x + GELU_C * x * x * x))
    return 0.5 * x * (1.0 + t), t


def _scan_pair(af_ref, uf_ref, ab_ref, ub_ref, hf_ref, hb_ref, pf_ref, pb_ref, L):
    ls = L // N_SEG
    zero = jnp.zeros((N_SEG, LRU_BW), F32)
    one = jnp.ones((N_SEG, LRU_BW), F32)
    tile = lambda t: pl.ds(pl.multiple_of(t * N_SEG, N_SEG), N_SEG)

    def steps(tc, carry):
        hf, pf, hb, pb = carry
        for q in range(SCAN_UNROLL):
            t = tc * SCAN_UNROLL + q
            rf, rb = tile(t), tile(ls - 1 - t)
            af = af_ref[rf, :]
            hf = af * hf + uf_ref[rf, :]
            pf = pf * af
            hf_ref[rf, :] = hf
            pf_ref[rf, :] = pf
            ab = ab_ref[rb, :]
            hb = ab * hb + ub_ref[rb, :]
            pb = pb * ab
            hb_ref[rb, :] = hb
            pb_ref[rb, :] = pb
        return hf, pf, hb, pb

    hf_e, pf_e, hb_e, pb_e = lax.fori_loop(0, ls // SCAN_UNROLL, steps, (zero, one, zero, one))

    rows, cin = [], jnp.zeros((1, LRU_BW), F32)
    for s in range(N_SEG):
        rows.append(cin)
        cin = hf_e[s:s + 1] + pf_e[s:s + 1] * cin
    cf = jnp.concatenate(rows, axis=0)
    rows, cin = [], jnp.zeros((1, LRU_BW), F32)
    for s in reversed(range(N_SEG)):
        rows.append(cin)
        cin = hb_e[s:s + 1] + pb_e[s:s + 1] * cin
    cb = jnp.concatenate(rows[::-1], axis=0)

    def fix(tc, _):
        for q in range(SCAN_UNROLL):
            r = tile(tc * SCAN_UNROLL + q)
            hf_ref[r, :] = hf_ref[r, :] + pf_ref[r, :] * cf
            hb_ref[r, :] = hb_ref[r, :] + pb_ref[r, :] * cb
        return 0

    lax.fori_loop(0, ls // SCAN_UNROLL, fix, 0)


def _rg_specs(L, D, nblk):
    slab = lambda off: pl.BlockSpec((L, LRU_BW), lambda cb, b: (b, off + cb))
    wspec = pl.BlockSpec((2, None, LRU_BW, LRU_BW), lambda cb, b: (0, cb, 0, 0))
    pvspec = pl.BlockSpec((PV_ROWS, LRU_BW), lambda cb, b: (0, cb))
    return slab, wspec, pvspec


def _rg_fwd(z, pvec, wa, wx, L, name):
    T, C2 = z.shape
    C = C2 // 2
    nblk = C // LRU_BW
    B = T // L
    slab, wspec, pvspec = _rg_specs(L, C, nblk)

    def body(gp_ref, rec_ref, pv_ref, wa_ref, wx_ref, yg_ref, hf_ref, hb_ref, a_scr, u_scr, p_scr):
        pv = pv_ref[...]
        c = _conv_taps(rec_ref[...], pv)
        cbf = c.astype(BF16)
        for d in range(2):
            _, i, a, m, _, _, _ = _rg_gates(c, cbf, wa_ref[d], wx_ref[d], pv[PV_B_A + d:PV_B_A + d + 1],
                                      pv[PV_B_X + d:PV_B_X + d + 1], pv[PV_LAM + d:PV_LAM + d + 1])
            a_scr[d] = a
            u_scr[d] = m * (i * c)
        _scan_pair(a_scr.at[0], u_scr.at[0], a_scr.at[1], u_scr.at[1], hf_ref, hb_ref, p_scr.at[0], p_scr.at[1], L)
        gate, _ = _gelu(gp_ref[...])
        yg_ref[...] = ((hf_ref[...] + hb_ref[...]) * gate).astype(BF16)

    return pl.pallas_call(
        body, name=name, grid=(nblk, B),
        in_specs=[slab(0), slab(nblk), pvspec, wspec, wspec],
        out_specs=[slab(0), slab(0), slab(0)],
        out_shape=[S((T, C), BF16), S((T, C), F32), S((T, C), F32)],
        scratch_shapes=[pltpu.VMEM((2, L, LRU_BW), F32)] * 3,
        compiler_params=_cp("parallel", "parallel"),
    )(z, z, pvec, wa, wx)


def _rg_bwd(z, hf, hb, dyg, pvec, wa, wx, L, name):
    T, C2 = z.shape
    C = C2 // 2
    nblk = C // LRU_BW
    B = T // L
    slab, wspec, pvspec = _rg_specs(L, C, nblk)

    def body(gp_ref, rec_ref, hf_ref, hb_ref, dyg_ref, pv_ref, wa_ref, wx_ref,
             dz_ref, dwa_ref, dwx_ref, dpv_ref, a_scr, u_scr, d_scr, p_scr):
        @pl.when(pl.program_id(1) == 0)
        def _():
            dwa_ref[...] = jnp.zeros_like(dwa_ref)
            dwx_ref[...] = jnp.zeros_like(dwx_ref)
            dpv_ref[...] = jnp.zeros_like(dpv_ref)

        pv = pv_ref[...]
        rec = rec_ref[...]
        c = _conv_taps(rec, pv)
        cbf = c.astype(BF16)
        gp = gp_ref[...]
        gate, th = _gelu(gp)
        dgelu = 0.5 * (1.0 + th) + 0.5 * gp * (1.0 - th * th) * GELU_K * (1.0 + 3.0 * GELU_C * gp * gp)
        dyg = dyg_ref[...]
        dz_ref[0] = (dyg * (hf_ref[...] + hb_ref[...]) * dgelu).astype(BF16)
        dy = dyg * gate

        gates = []
        for d in range(2):
            gates.append(_rg_gates(c, cbf, wa_ref[d], wx_ref[d], pv[PV_B_A + d:PV_B_A + d + 1],
                                   pv[PV_B_X + d:PV_B_X + d + 1], pv[PV_LAM + d:PV_LAM + d + 1]))
        a_scr[0] = _shift_rows(gates[1][2], 1)
        a_scr[1] = _shift_rows(gates[0][2], -1)
        u_scr[...] = dy
        _scan_pair(a_scr.at[0], u_scr, a_scr.at[1], u_scr, d_scr.at[1], d_scr.at[0], p_scr.at[0], p_scr.at[1], L)

        dc = jnp.zeros_like(c)
        rows = []
        for d in range(2):
            r, i, a, m, rm, a2, sp = gates[d]
            delta = d_scr[d]
            hnb = _shift_rows(hf_ref[...], 1) if d == 0 else _shift_rows(hb_ref[...], -1)
            da = delta * hnb
            dm = delta * (i * c)
            di = delta * (m * c)
            dc = dc + delta * (m * i)
            dla = da * a - dm * (a2 * rm)
            dpa = (dla * ((-RG_C) * sp)) * (r * (1.0 - r))
            dpx = di * (i * (1.0 - i))
            dsp = (-RG_C) * jnp.sum(dla * r, axis=0, keepdims=True)
            lam = pv[PV_LAM + d:PV_LAM + d + 1]
            rows.append((jnp.sum(dpa, axis=0, keepdims=True), jnp.sum(dpx, axis=0, keepdims=True),
                         -dsp * _sigmoid(-lam)))
            dpab = dpa.astype(BF16)
            dpxb = dpx.astype(BF16)
            dwa_ref[d] += _dot_tn(cbf, dpab)
            dwx_ref[d] += _dot_tn(cbf, dpxb)
            dc = dc + _dot_nt(dpab, wa_ref[d]) + _dot_nt(dpxb, wx_ref[d])

        drec = jnp.zeros_like(c)
        dcw = []
        for j in range(CONV_W):
            drec = drec + pv[PV_CONV_W + j:PV_CONV_W + j + 1] * _shift_rows(dc, j - 2)
            dcw.append(jnp.sum(dc * _shift_rows(rec, 2 - j), axis=0, keepdims=True))
        dz_ref[1] = drec.astype(BF16)
        for j in range(CONV_W):
            dpv_ref[PV_CONV_W + j:PV_CONV_W + j + 1, :] += dcw[j]
        for d in range(2):
            dpv_ref[PV_B_A + d:PV_B_A + d + 1, :] += rows[d][0]
            dpv_ref[PV_B_X + d:PV_B_X + d + 1, :] += rows[d][1]
            dpv_ref[PV_LAM + d:PV_LAM + d + 1, :] += rows[d][2]
        dpv_ref[PV_CONV_B:PV_CONV_B + 1, :] += jnp.sum(dc, axis=0, keepdims=True)

    return pl.pallas_call(
        body, name=name, grid=(nblk, B),
        in_specs=[slab(0), slab(nblk), slab(0), slab(0), slab(0), pvspec, wspec, wspec],
        out_specs=[pl.BlockSpec((2, L, LRU_BW), lambda cb, b: (0, b, cb)), wspec, wspec, pvspec],
        out_shape=[S((2, T, C), BF16), S((2, nblk, LRU_BW, LRU_BW), F32), S((2, nblk, LRU_BW, LRU_BW), F32),
                   S((PV_ROWS, C), F32)],
        scratch_shapes=[pltpu.VMEM((2, L, LRU_BW), F32), pltpu.VMEM((L, LRU_BW), F32),
                        pltpu.VMEM((2, L, LRU_BW), F32), pltpu.VMEM((2, L, LRU_BW), F32)],
        compiler_params=_cp("parallel", "arbitrary"),
    )(z, z, hf, hb, dyg, pvec, wa, wx)


QKV_NB = 512


def _interleave(a):
    *lead, L, D = a.shape
    return a.reshape(*lead, N_SEG, L // N_SEG, D).swapaxes(-3, -2).reshape(*lead, L, D)


def _deinterleave(a):
    *lead, L, D = a.shape
    return a.reshape(*lead, L // N_SEG, N_SEG, D).swapaxes(-3, -2).reshape(*lead, L, D)


def _local_step(x3, tgt3, w, fetch, send):
    Bl, L, D = x3.shape
    T = Bl * L
    x = _interleave(x3).reshape(T, D)
    tgt = _interleave(tgt3).reshape(T, D)
    gm = [w["g_mix"][i:i + 1] for i in range(2)]
    gl = [w["g_mlp"][i:i + 1] for i in range(2)]

    w0 = fetch(0, ())
    nb_in = w0["w_in"].shape[-1]
    z, h0 = _norm_matmul(x, gm[0], w0["w_in"], "rg_in")
    yg, hf, hb = _rg_fwd(z, w0["pvec"], w["wa"], w["wx"], L, "rg_fwd")
    w1 = fetch(1, (yg,))
    x1 = _matmul_res(yg, w1["w_out"], x, "rg_out")
    w1.update(fetch(4, (x1,)))
    fb = w1["w_up0"].shape[-1]
    x2, a0, hm0 = _mlp_fwd(x1, gl[0], w1["w_up0"], w1["w_down0"], "mlp0_fwd")
    w2 = fetch(2, (x2,))
    qkv, h1 = _norm_matmul(x2, gm[1], w2["w_qkv"], "at_qkv", transposed=True)
    cos, sin = [_interleave(t) for t in _rope_tables(L)]
    qn, kn, vb = _qk_prep(qkv, w["qg"], w["kg"], cos, sin, L, "at_prep")
    o = _attn_fwd(qn, kn, vb, L, "at_fwd")
    x3_ = _matmul_res(o, w2["w_o"], x2, "at_out")
    w3 = fetch(3, (x3_,))
    dx4, a1, hm1, loss, dgf = _mlp_fwd(x3_, gl[1], w3["w_up1"], w3["w_down1"], "mlp1_fwd", head=(tgt, w["g_fin"]))

    dx3, da1, dob1, dgl1 = _mlp_bwd_dx(x3_, dx4, a1, gl[1], w3["w_up1"], w3["w_down1"], "mlp1_bwd_dx")
    dwu1, dwd1 = _mlp_bwd_dw(hm1, da1, a1, dob1, fb, "mlp1_bwd_dw")
    sent = send(3, dict(w_up1=dwu1, w_down1=dwd1))
    do, dx3b = _matmul_nt(dx3, w2["w_o"], "at_out_bwd", BF16, after=sent)
    dwo = _matmul_tn(o, dx3b[None], QKV_NB, "at_out_dw", blocked=False)
    dq, dk, dv = _attn_bwd(qn, kn, vb, do, o, L, "at_bwd")
    dqkv, dqg, dkg = _qk_prep_bwd(qkv, dq, dk, dv, w["qg"], w["kg"], cos, sin, L, "at_prep_bwd")
    dwqkv = _matmul_tn(dqkv, h1[None], QKV_NB, "at_qkv_dw", blocked=False)
    sent = send(2, dict(w_qkv=dwqkv, w_o=dwo))
    dx2, dgm1 = _nt_normbwd(dqkv[None], w2["w_qkv"], x2, gm[1], dx3, "at_qkv_bwd", after=sent, transposed=True)
    dx1, da0, dob0, dgl0 = _mlp_bwd_dx(x1, dx2, a0, gl[0], w1["w_up0"], w1["w_down0"], "mlp0_bwd_dx")
    dwu0, dwd0 = _mlp_bwd_dw(hm0, da0, a0, dob0, fb, "mlp0_bwd_dw")
    sent = send(1, dict(w_up0=dwu0, w_down0=dwd0))
    dyg, dx1b = _matmul_nt(dx1, w1["w_out"], "rg_out_bwd", F32, after=sent)
    dwout = _matmul_tn(yg, dx1b[None], QKV_NB, "rg_out_dw", blocked=False)
    dz, dwa, dwx, dpv = _rg_bwd(z, hf, hb, dyg, w0["pvec"], w["wa"], w["wx"], L, "rg_bwd")
    sent = send(4, dict(w_out=dwout, pvec=dpv, wa=dwa, wx=dwx))
    dwin = _matmul_tn(h0, dz, nb_in, "rg_in_dw", blocked=True, after=sent)
    sent = send(0, dict(w_in=dwin))
    dx0, dgm0 = _nt_normbwd(dz, w0["w_in"], x, gm[0], dx1, "rg_in_bwd", after=sent)
    send(-1, dict(g_mix=[dgm0, dgm1], g_mlp=[dgl0, dgl1], g_fin=dgf, conv_b=dpv[PV_CONV_B:PV_CONV_B + 1], qg=dqg, kg=dkg, loss=loss))
    return _deinterleave(dx0.reshape(Bl, L, D))


MESH = pl.DeviceIdType.MESH
ANY = pl.BlockSpec(memory_space=pl.ANY)
N_PEERS = N_DEV - 1


def _my_place():
    return lax.axis_index("x"), lax.axis_index("y"), lax.axis_index("c")


def _flat(px, py, pc):
    return 4 * px + 2 * py + pc


def _all_gather(shards, name):
    n = len(shards)

    def body(*refs):
        ins, outs = refs[:n], refs[n:2 * n]
        send_sems, recv_sems, local_sems = refs[2 * n:]
        x, y, c = _my_place()
        me, sibling = (x, y, c), (x, y, 1 - c)
        chips = [(1 - x, y), (x, 1 - y), (1 - x, 1 - y)]

        def copy(a, k, block, to, src=None):
            dst = outs[a].at[_flat(*block)]
            return pltpu.make_async_remote_copy(
                src_ref=dst if src is None else src, dst_ref=dst,
                send_sem=send_sems.at[a, k], recv_sem=recv_sems.at[a, k],
                device_id=to, device_id_type=MESH)

        mine = [pltpu.make_async_copy(ins[a], outs[a].at[_flat(*me)], local_sems.at[a]) for a in range(n)]
        for cp in mine:
            cp.start()
        first = []
        for a in range(n):
            first.append(copy(a, 0, me, sibling, src=ins[a]))
            first += [copy(a, 1 + j, me, (*chip, c), src=ins[a]) for j, chip in enumerate(chips)]
        for cp in first:
            cp.start()
        passed = []
        for j, chip in enumerate(chips):
            for a in range(n):
                copy(a, 1 + j, (*chip, c), me).wait_recv()
                fwd = copy(a, 4 + j, (*chip, c), sibling)
                fwd.start()
                passed.append(fwd)
        for a in range(n):
            copy(a, 0, sibling, me).wait_recv()
            for j, chip in enumerate(chips):
                copy(a, 4 + j, (*chip, 1 - c), me).wait_recv()
        for cp in first + passed:
            cp.wait_send()
        for cp in mine:
            cp.wait()

    return pl.pallas_call(
        body, name=name,
        in_specs=[ANY] * n, out_specs=[ANY] * n,
        out_shape=[S((N_DEV,) + s.shape, s.dtype) for s in shards],
        scratch_shapes=[pltpu.SemaphoreType.DMA((n, N_PEERS)), pltpu.SemaphoreType.DMA((n, N_PEERS)),
                        pltpu.SemaphoreType.DMA((n,))],
    )(*shards)


HBM = pl.BlockSpec(memory_space=pltpu.HBM)
SEM = pl.BlockSpec(memory_space=pltpu.SEMAPHORE)
SIDE_EFFECT = pltpu.SideEffectType.DATAFLOW_SIDE_EFFECTING
SEMS_PER_GROUP = 3


NEAR_PEERS = (1, 2, 4, 6)
FAR_CHIPS = (2, 4, 6)


def _exchange_copies(srcs, lands, sems, mode):
    send_sems, recv_sems, local_sems = sems
    scatter = mode == "scatter"
    x, y, c = _my_place()
    me = _flat(x, y, c)
    remote, local = [], []
    for a in range(len(srcs)):
        for r in (NEAR_PEERS if mode == "near" else range(1, N_DEV)):
            peer = (1 - x if r & 4 else x, 1 - y if r & 2 else y, 1 - c if r & 1 else c)
            remote.append(pltpu.make_async_remote_copy(
                src_ref=srcs[a].at[_flat(*peer)] if scatter else srcs[a], dst_ref=lands[a].at[me],
                send_sem=send_sems.at[a * N_PEERS + r - 1], recv_sem=recv_sems.at[a * N_PEERS + r - 1],
                device_id=peer, device_id_type=MESH))
        local.append(pltpu.make_async_copy(srcs[a].at[me] if scatter else srcs[a], lands[a].at[me], local_sems.at[a]))
    return remote, local


def _exchange_start(groups, modes, name):
    sizes = [len(g) for g in groups]
    srcs = [pltpu.with_memory_space_constraint(a, pltpu.HBM) for g in groups for a in g]
    n = len(srcs)
    scatter_of = [m == "scatter" for g, m in zip(groups, modes) for _ in g]
    lands = [pltpu.with_memory_space_constraint(lax.empty(a.shape if sc else (N_DEV,) + a.shape, a.dtype), pltpu.HBM)
             for a, sc in zip(srcs, scatter_of)]
    n_sem = SEMS_PER_GROUP * len(groups)

    def body(*refs):
        src_refs, land_refs, sem_refs, token = refs[:n], refs[n:2 * n], refs[2 * n:2 * n + n_sem], refs[-1]
        off = 0
        for gi, k in enumerate(sizes):
            remote, local = _exchange_copies(src_refs[off:off + k], land_refs[off:off + k],
                                             sem_refs[SEMS_PER_GROUP * gi:SEMS_PER_GROUP * (gi + 1)], modes[gi])
            for cp in local + remote:
                cp.start()
            off += k
        token[...] = jnp.zeros_like(token)

    sem_shapes = []
    for k in sizes:
        sem_shapes += [pltpu.SemaphoreType.DMA((k * N_PEERS,)), pltpu.SemaphoreType.DMA((k * N_PEERS,)),
                       pltpu.SemaphoreType.DMA((k,))]
    outs = pl.pallas_call(
        body, name=name,
        out_shape=sem_shapes + [pltpu.HBM(a.shape, a.dtype) for a in srcs + lands] + [S((8, 128), F32)],
        in_specs=[HBM] * (2 * n),
        out_specs=[SEM] * n_sem + [HBM] * (2 * n) + [pl.BlockSpec(memory_space=pltpu.VMEM)],
        input_output_aliases={i: n_sem + i for i in range(2 * n)},
        compiler_params=pltpu.CompilerParams(has_side_effects=SIDE_EFFECT),
    )(*srcs, *lands)
    sems, thru, token = outs[:n_sem], outs[n_sem:n_sem + 2 * n], outs[-1]
    per_group, off = [], 0
    for gi, k in enumerate(sizes):
        per_group.append((sems[SEMS_PER_GROUP * gi:SEMS_PER_GROUP * (gi + 1)], thru[off:off + k], thru[n + off:n + off + k]))
        off += k
    return per_group, token


def _exchange_wait(group, after, mode, name):
    sems, srcs, lands = group
    k = len(srcs)

    def body(*refs):
        remote, local = _exchange_copies(refs[:k], refs[k:2 * k], refs[2 * k:2 * k + SEMS_PER_GROUP], mode)
        for cp in remote:
            cp.wait_send()
            cp.wait_recv()
        for cp in local:
            cp.wait()

    outs = pl.pallas_call(
        body, name=name,
        out_shape=[pltpu.HBM(a.shape, a.dtype) for a in list(srcs) + list(lands)],
        in_specs=[HBM] * (2 * k) + [SEM] * SEMS_PER_GROUP + [ANY] * len(after),
        out_specs=[HBM] * (2 * k),
        input_output_aliases={i: i for i in range(2 * k)},
        compiler_params=pltpu.CompilerParams(has_side_effects=SIDE_EFFECT),
    )(*srcs, *lands, *sems, *after)
    return outs[k:]


def _forward_copies(lands, sems):
    send_sems, recv_sems = sems
    x, y, c = _my_place()
    mine, theirs = [], []
    for a in range(len(lands)):
        for k, r in enumerate(FAR_CHIPS):
            px, py = (1 - x if r & 4 else x), (1 - y if r & 2 else y)
            for out, core in ((mine, c), (theirs, 1 - c)):
                blk = lands[a].at[_flat(px, py, core)]
                out.append(pltpu.make_async_remote_copy(
                    src_ref=blk, dst_ref=blk, send_sem=send_sems.at[a * len(FAR_CHIPS) + k],
                    recv_sem=recv_sems.at[a * len(FAR_CHIPS) + k], device_id=(x, y, 1 - c), device_id_type=MESH))
    return mine, theirs


def _forward_start(groups, name):
    sizes = [len(g) for g in groups]
    lands = [a for g in groups for a in g]
    n = len(lands)
    n_sem = 2 * len(groups)

    def body(*refs):
        land_refs, sem_refs, token = refs[:n], refs[n:n + n_sem], refs[-1]
        off = 0
        for gi, k in enumerate(sizes):
            mine, _ = _forward_copies(land_refs[off:off + k], sem_refs[2 * gi:2 * gi + 2])
            for cp in mine:
                cp.start()
            off += k
        token[...] = jnp.zeros_like(token)

    sem_shapes = []
    for k in sizes:
        sem_shapes += [pltpu.SemaphoreType.DMA((k * len(FAR_CHIPS),))] * 2
    outs = pl.pallas_call(
        body, name=name,
        out_shape=sem_shapes + [pltpu.HBM(a.shape, a.dtype) for a in lands] + [S((8, 128), F32)],
        in_specs=[HBM] * n,
        out_specs=[SEM] * n_sem + [HBM] * n + [pl.BlockSpec(memory_space=pltpu.VMEM)],
        input_output_aliases={i: n_sem + i for i in range(n)},
        compiler_params=pltpu.CompilerParams(has_side_effects=SIDE_EFFECT),
    )(*lands)
    per_group, off = [], 0
    for gi, k in enumerate(sizes):
        per_group.append((outs[2 * gi:2 * gi + 2], outs[n_sem + off:n_sem + off + k]))
        off += k
    return per_group


def _forward_wait(group, after, name):
    sems, lands = group
    k = len(lands)

    def body(*refs):
        mine, theirs = _forward_copies(refs[:k], refs[k:k + 2])
        for cp in mine:
            cp.wait_send()
        for cp in theirs:
            cp.wait_recv()

    return pl.pallas_call(
        body, name=name,
        out_shape=[pltpu.HBM(a.shape, a.dtype) for a in lands],
        in_specs=[HBM] * k + [SEM] * 2 + [ANY] * len(after),
        out_specs=[HBM] * k,
        input_output_aliases={i: i for i in range(k)},
        compiler_params=pltpu.CompilerParams(has_side_effects=SIDE_EFFECT),
    )(*lands, *sems, *after)


def _row_tile(rows, cols):
    want = max(16, (128 * 1024) // cols)
    if rows <= want:
        return rows
    t = want - want % 16
    while rows % t:
        t -= 16
    return t


def _sum_parts(parts, name, after=()):
    P, R, C = parts.shape
    tr = _row_tile(R, C)

    def body(p_ref, *rest):
        o_ref = rest[-1]
        g = p_ref[0].astype(F32)
        for i in range(1, P):
            g = g + p_ref[i].astype(F32)
        o_ref[...] = g

    return pl.pallas_call(
        body, name=name, grid=(R // tr,),
        in_specs=[pl.BlockSpec((P, tr, C), lambda i: (0, i, 0))] + [ANY] * len(after),
        out_specs=pl.BlockSpec((tr, C), lambda i: (i, 0)),
        out_shape=S((R, C), F32),
        compiler_params=_cp("parallel"),
    )(parts, *after)


def _adamw(parts, w, m, v, name, after=()):
    P, R, C = parts.shape
    tr = _row_tile(R, C)
    c1 = 1.0 - ADAM_B1 ** ADAM_STEP
    c2 = 1.0 - ADAM_B2 ** ADAM_STEP

    def body(p_ref, w_ref, m_ref, v_ref, *rest):
        g_ref, d_ref, mo_ref, vo_ref = rest[len(after):]
        g = p_ref[0].astype(F32)
        for i in range(1, P):
            g = g + p_ref[i].astype(F32)
        mn = ADAM_B1 * m_ref[...] + (1.0 - ADAM_B1) * g
        vn = ADAM_B2 * v_ref[...] + (1.0 - ADAM_B2) * (g * g)
        g_ref[...] = g
        mo_ref[...] = mn
        vo_ref[...] = vn
        d_ref[...] = (-ADAM_LR) * ((mn / c1) / (jnp.sqrt(vn / c2) + ADAM_EPS) + ADAM_WD * w_ref[...])

    blk = pl.BlockSpec((tr, C), lambda i: (i, 0))
    return pl.pallas_call(
        body, name=name, grid=(R // tr,),
        in_specs=[pl.BlockSpec((P, tr, C), lambda i: (0, i, 0)), blk, blk, blk] + [ANY] * len(after),
        out_specs=[blk, blk, blk, blk],
        out_shape=[S((R, C), F32)] * 4,
        compiler_params=_cp("parallel"),
    )(parts, w, m, v, *after)


def _adamw_layer(parts, w3, m3, v3, layer, prev, name, after=()):
    P, R, C = parts.shape
    NL = w3.shape[0]
    tr = _row_tile(R, C)
    c1 = 1.0 - ADAM_B1 ** ADAM_STEP
    c2 = 1.0 - ADAM_B2 ** ADAM_STEP
    n_prev = 0 if prev is None else len(prev)

    def body(p_ref, w_ref, m_ref, v_ref, *rest):
        g_ref, d_ref, mo_ref, vo_ref = rest[n_prev + len(after):]
        g = p_ref[0].astype(F32)
        for i in range(1, P):
            g = g + p_ref[i].astype(F32)
        mn = ADAM_B1 * m_ref[...] + (1.0 - ADAM_B1) * g
        vn = ADAM_B2 * v_ref[...] + (1.0 - ADAM_B2) * (g * g)
        g_ref[...] = g
        mo_ref[...] = mn
        vo_ref[...] = vn
        d_ref[...] = (-ADAM_LR) * ((mn / c1) / (jnp.sqrt(vn / c2) + ADAM_EPS) + ADAM_WD * w_ref[...])

    blk = pl.BlockSpec((None, tr, C), lambda i: (layer, i, 0))
    return pl.pallas_call(
        body, name=name, grid=(R // tr,),
        in_specs=[pl.BlockSpec((P, tr, C), lambda i: (0, i, 0)), blk, blk, blk] + [ANY] * (n_prev + len(after)),
        out_specs=[blk, blk, blk, blk],
        out_shape=[S((NL, R, C), F32)] * 4,
        input_output_aliases={4 + k: k for k in range(n_prev)},
        compiler_params=_cp("parallel"),
    )(parts, w3, m3, v3, *(prev or ()), *after)


VMEM_WHOLE = pl.BlockSpec(memory_space=pltpu.VMEM)


def _pack_vectors(vectors, starts, rows, name):
    def body(*refs):
        o_ref = refs[-1]
        o_ref[...] = jnp.zeros_like(o_ref)
        for v_ref, r0 in zip(refs[:-1], starts):
            for j in range(v_ref.shape[1] // 128):
                o_ref[r0 + j:r0 + j + 1, :] = v_ref[:, j * 128:(j + 1) * 128]

    return pl.pallas_call(body, name=name, in_specs=[VMEM_WHOLE] * len(vectors), out_specs=VMEM_WHOLE,
                          out_shape=S((rows, 128), F32))(*vectors)


def _adamw_vectors(g_pack, params, keep_rows, name, after=()):
    n = len(params)
    P = g_pack.shape[0]
    c1 = 1.0 - ADAM_B1 ** ADAM_STEP
    c2 = 1.0 - ADAM_B2 ** ADAM_STEP

    def body(g_ref, *refs):
        ins, outs = refs[:3 * n], refs[3 * n + len(after):]
        gs = g_ref[0]
        for i in range(1, P):
            gs = gs + g_ref[i]
        for pi, (_, _, _, slots) in enumerate(params):
            w_ref, m_ref, v_ref = ins[3 * pi:3 * pi + 3]
            g_out, d_out, m_out, v_out = outs[4 * pi:4 * pi + 4]
            for idx, row in slots:
                g = gs[row:row + 1, :]
                mn = ADAM_B1 * m_ref[idx] + (1.0 - ADAM_B1) * g
                vn = ADAM_B2 * v_ref[idx] + (1.0 - ADAM_B2) * (g * g)
                g_out[idx] = g
                m_out[idx] = mn
                v_out[idx] = vn
                d_out[idx] = (-ADAM_LR) * ((mn / c1) / (jnp.sqrt(vn / c2) + ADAM_EPS) + ADAM_WD * w_ref[idx])
        outs[-1][...] = jnp.concatenate([gs[r:r + 1, :] for r in keep_rows], axis=0)

    flat = [a for w, m, v, _ in params for a in (w, m, v)]
    out_shape = [S(w.shape, F32) for w, _, _, _ in params for _ in range(4)] + [S((len(keep_rows), 128), F32)]
    outs = pl.pallas_call(
        body, name=name,
        in_specs=[VMEM_WHOLE] * (1 + len(flat)) + [ANY] * len(after),
        out_specs=[VMEM_WHOLE] * len(out_shape), out_shape=out_shape,
    )(g_pack, *flat, *after)
    return [outs[4 * i:4 * i + 4] for i in range(n)], outs[-1]


def _adamw_nd(parts, w, m, v, name, after=()):
    shp = w.shape
    C = shp[-1]
    outs = _adamw(parts.reshape(parts.shape[0], -1, C), w.reshape(-1, C), m.reshape(-1, C), v.reshape(-1, C), name, after)
    return [o.reshape(shp) for o in outs]


TILE_ROWS = 8


REP_SMALL_ROWS = 128
REP_GRAD_STARTS = (0, 8, 16, 24, 32, 40, 48, 56, 64)
REP_SMALL_STARTS = (0, 16, 32, 40, 48, 56)
REP_LOSS_ROW = 64


def _small_pack(cw, ba, bx, lam):
    pad8 = lambda a: jnp.pad(a, ((0, TILE_ROWS - a.shape[0]), (0, 0)))
    return jnp.concatenate([pad8(cw[0, :, 0, :]), pad8(ba[0]), pad8(bx[0]), pad8(lam[0]),
                            jnp.zeros((PV_ROWS - PV_CONV_B, LRU_BW), F32)], axis=0)


def kernel(x, norm_mix_g, norm_mlp_g, rg_w_in, rg_conv_w, rg_conv_b, rg_w_a, rg_b_a, rg_w_x, rg_b_x, rg_lam, rg_w_out, at_w_qkv, at_q_g, at_k_g, at_w_o, mlp_w_up, mlp_w_down, final_g, loss_target, m_norm_mix_g, m_norm_mlp_g, m_rg_w_in, m_rg_conv_w, m_rg_conv_b, m_rg_w_a, m_rg_b_a, m_rg_w_x, m_rg_b_x, m_rg_lam, m_rg_w_out, m_at_w_qkv, m_at_q_g, m_at_k_g, m_at_w_o, m_mlp_w_up, m_mlp_w_down, m_final_g, v_norm_mix_g, v_norm_mlp_g, v_rg_w_in, v_rg_conv_w, v_rg_conv_b, v_rg_w_a, v_rg_b_a, v_rg_w_x, v_rg_b_x, v_rg_lam, v_rg_w_out, v_at_w_qkv, v_at_q_g, v_at_k_g, v_at_w_o, v_mlp_w_up, v_mlp_w_down, v_final_g):
    D = x.shape[-1]
    bf = lambda a: a.astype(BF16)

    sp_w = _small_pack(rg_conv_w, rg_b_a, rg_b_x, rg_lam)
    started, _ = _exchange_start(
        [[bf(rg_w_in[0]), sp_w], [bf(rg_w_out[0])], [bf(mlp_w_up[0]), bf(mlp_w_down[0])],
         [bf(at_w_qkv[0]).T, bf(at_w_o[0])], [bf(mlp_w_up[1]), bf(mlp_w_down[1])]],
        ["near", "near", "near", "gather", "gather"], "gather_start")
    gathers = dict(zip((0, 1, 4, 2, 3), started))
    forwards = {}

    def fetch(stage, after):
        after = tuple(after)
        if stage == 0:
            got = _exchange_wait(gathers[0], after, "near", "gather_wait0")
            g_in, g_sp = _forward_wait(_forward_start([got], "forward_start0")[0], (), "forward_wait0")
            pvec = g_sp.transpose(1, 0, 2).reshape(PV_ROWS, D)
            pvec = jnp.concatenate([pvec[:PV_CONV_B], jnp.broadcast_to(rg_conv_b, (PV_ROWS - PV_CONV_B, D))], axis=0)
            return dict(w_in=g_in, pvec=pvec)
        if stage == 1:
            near = [_exchange_wait(gathers[s], after, "near", "gather_wait%d" % s) for s in (1, 4)]
            f_out, forwards[4] = _forward_start(near, "forward_start1")
            g_out, = _forward_wait(f_out, (), "forward_wait1")
            return dict(w_out=g_out.reshape(D, D))
        if stage == 4:
            g_up0, g_dn0 = _forward_wait(forwards[4], after, "forward_wait4")
            return dict(w_up0=g_up0, w_down0=g_dn0.reshape(-1, D))
        got = _exchange_wait(gathers[stage], after, "gather", "gather_wait%d" % stage)
        if stage == 2:
            return dict(w_qkv=got[0].reshape(-1, QKV_NB, D), w_o=got[1].reshape(D, D))
        return dict(w_up1=got[0], w_down1=got[1].reshape(-1, D))

    scatters = {}

    def send(stage, g):
        if stage == 3:
            arrs = [g["w_up1"], g["w_down1"].reshape(N_DEV, -1, D)]
        elif stage == 2:
            arrs = [g["w_qkv"].reshape(N_DEV, -1, D), g["w_o"].reshape(N_DEV, -1, D)]
        elif stage == 1:
            arrs = [g["w_up0"], g["w_down0"].reshape(N_DEV, -1, D)]
        elif stage == 4:
            arrs = [g["w_out"].reshape(N_DEV, -1, D), g["pvec"].reshape(PV_ROWS, N_DEV, LRU_BW).transpose(1, 0, 2),
                    bf(g["wa"]).reshape(N_DEV, -1, 128), bf(g["wx"]).reshape(N_DEV, -1, 128)]
        elif stage == 0:
            arrs = [g["w_in"]]
        else:
            small = _pack_vectors(g["g_mix"] + g["g_mlp"] + [g["g_fin"], g["conv_b"], g["qg"], g["kg"], g["loss"]],
                                  REP_GRAD_STARTS, REP_SMALL_ROWS, "pack_rep_small")
            arrs = [small.reshape(N_DEV, -1, 128)]
        (group,), token = _exchange_start([arrs], ["scatter"], "scatter_start%d" % (stage % 6))
        scatters[stage] = (group, token)
        return (token,)

    w = dict(g_mix=norm_mix_g, g_mlp=norm_mlp_g, g_fin=final_g[None], qg=at_q_g, kg=at_k_g,
             wa=bf(rg_w_a[0]), wx=bf(rg_w_x[0]))
    grad_x = _local_step(x, loss_target, w, fetch, send)

    res = {}
    r_up1, r_dn1 = _exchange_wait(scatters[3][0], (scatters[-1][1],), "scatter", "scatter_wait3")
    r_out, r_sp, r_wa, r_wx = _exchange_wait(scatters[4][0], (r_up1,), "scatter", "scatter_wait4")
    wa_part = _sum_parts(r_wa, "reduce_w_a")
    wx_part = _sum_parts(r_wx, "reduce_w_x", (wa_part,))
    (rep_gather,), rep_token = _exchange_start([[wa_part, wx_part]], ["gather"], "rep_gather_start")
    up = _adamw_layer(r_up1, mlp_w_up, m_mlp_w_up, v_mlp_w_up, 1, None, "adam_mlp_w_up1", after=(rep_token,))
    dn = _adamw_layer(r_dn1, mlp_w_down, m_mlp_w_down, v_mlp_w_down, 1, None, "adam_mlp_w_down1", after=(up[1],))
    r_qkv, r_o = _exchange_wait(scatters[2][0], (dn[1],), "scatter", "scatter_wait2")
    tr = lambda a: a[0].T
    qkv_t = _adamw_nd(r_qkv, tr(at_w_qkv), tr(m_at_w_qkv), tr(v_at_w_qkv), "adam_at_w_qkv")
    res["at_w_qkv"] = [o.T[None] for o in qkv_t]
    res["at_w_o"] = _adamw_nd(r_o[:, None], at_w_o, m_at_w_o, v_at_w_o, "adam_at_w_o", (qkv_t[1],))
    r_up0, r_dn0 = _exchange_wait(scatters[1][0], (res["at_w_o"][1],), "scatter", "scatter_wait1")
    res["mlp_w_up"] = _adamw_layer(r_up0, mlp_w_up, m_mlp_w_up, v_mlp_w_up, 0, up, "adam_mlp_w_up0")
    res["mlp_w_down"] = _adamw_layer(r_dn0, mlp_w_down, m_mlp_w_down, v_mlp_w_down, 0, dn, "adam_mlp_w_down0",
                                     after=(res["mlp_w_up"][1],))
    r_in, = _exchange_wait(scatters[0][0], (res["mlp_w_down"][1],), "scatter", "scatter_wait0")
    res["rg_w_in"] = _adamw_nd(r_in[:, None], rg_w_in, m_rg_w_in, v_rg_w_in, "adam_rg_w_in")
    res["rg_w_out"] = _adamw_nd(r_out[:, None], rg_w_out, m_rg_w_out, v_rg_w_out, "adam_rg_w_out", (res["rg_w_in"][1],))
    whole, lane = slice(None), slice(0, 1)
    two_rows = lambda r0: [((0, slice(d, d + 1), whole), r0 + d) for d in range(2)]
    (res["rg_conv_w"], res["rg_b_a"], res["rg_b_x"], res["rg_lam"]), _ = _adamw_vectors(
        r_sp, [(rg_conv_w, m_rg_conv_w, v_rg_conv_w, [((0, t, lane, whole), PV_CONV_W + t) for t in range(CONV_W)]),
               (rg_b_a, m_rg_b_a, v_rg_b_a, two_rows(PV_B_A)), (rg_b_x, m_rg_b_x, v_rg_b_x, two_rows(PV_B_X)),
               (rg_lam, m_rg_lam, v_rg_lam, two_rows(PV_LAM))], [0], "adam_small", (res["rg_w_out"][1],))

    r_small, = _exchange_wait(scatters[-1][0], (res["rg_lam"][1],), "scatter", "scatter_wait5")
    small_sum, = _all_gather([_sum_parts(r_small, "reduce_rep_small")], "gather_replicated")
    wa_sum, wx_sum = _exchange_wait(rep_gather, (small_sum,), "gather", "rep_gather_wait")
    rows = lambda a: a.reshape(-1, 128)
    wa_res = _adamw(wa_sum.reshape(1, -1, 128), rows(rg_w_a), rows(m_rg_w_a), rows(v_rg_w_a), "adam_rg_w_a")
    wx_res = _adamw(wx_sum.reshape(1, -1, 128), rows(rg_w_x), rows(m_rg_w_x), rows(v_rg_w_x), "adam_rg_w_x", (wa_res[1],))
    res["rg_w_a"] = [o.reshape(rg_w_a.shape) for o in wa_res]
    res["rg_w_x"] = [o.reshape(rg_w_x.shape) for o in wx_res]
    def vec_slots(a, r0):
        per = a.shape[1] // 128
        return [((slice(l, l + 1), slice(128 * j, 128 * (j + 1))), r0 + l * per + j) for l in range(a.shape[0]) for j in range(per)]

    fin = [final_g[None], m_final_g[None], v_final_g[None]]
    vecs = [(norm_mix_g, m_norm_mix_g, v_norm_mix_g), (norm_mlp_g, m_norm_mlp_g, v_norm_mlp_g), fin,
            (rg_conv_b, m_rg_conv_b, v_rg_conv_b), (at_q_g, m_at_q_g, v_at_q_g), (at_k_g, m_at_k_g, v_at_k_g)]
    outs, kept = _adamw_vectors(
        small_sum.reshape(1, -1, 128),
        [(wv, mv, vv, vec_slots(wv, r0)) for (wv, mv, vv), r0 in zip(vecs, REP_SMALL_STARTS)], [REP_LOSS_ROW], "adam_rep_small", (wx_res[1],))
    for nm, o in zip(["norm_mix_g", "norm_mlp_g", "final_g", "rg_conv_b", "at_q_g", "at_k_g"], outs):
        res[nm] = [a[0] for a in o] if nm == "final_g" else o
    loss = kept[0, 0]

    order = ["norm_mix_g", "norm_mlp_g", "rg_w_in", "rg_conv_w", "rg_conv_b", "rg_w_a", "rg_b_a", "rg_w_x", "rg_b_x",
             "rg_lam", "rg_w_out", "at_w_qkv", "at_q_g", "at_k_g", "at_w_o", "mlp_w_up", "mlp_w_down", "final_g"]
    return (loss, grad_x, *[res[nm][k] for k in range(4) for nm in order])
```

```python
import functools
import math

import jax
import jax.numpy as jnp
from jax import lax
from jax.experimental import pallas as pl
from jax.experimental.pallas import tpu as pltpu

F32 = jnp.float32
BF16 = jnp.bfloat16
S = jax.ShapeDtypeStruct

EPS = 1e-6
HEAD_DIM = 128
N_KV = 2
GRID_W = 64
ROPE_THETA = 10000.0
LRU_BW = 128
RG_C = 8.0
CONV_W = 4
N_DEV = 8
N_SEG = 8
SCAN_UNROLL = 8
TN_STEP_COLS = 512
PROJ_TM = 1024
PROJ_CHAINS = 2
MLP_TM = 512
VMEM_LIMIT_V7X = 56 * 1024 * 1024
SOFTMAX_SCALE = 1.0 / math.sqrt(HEAD_DIM)
GELU_K = math.sqrt(2.0 / math.pi)
GELU_C = 0.044715

ADAM_LR = 0.001
ADAM_B1 = 0.9
ADAM_B2 = 0.999
ADAM_EPS = 1e-08
ADAM_WD = 0.01
ADAM_STEP = 10

NT = (((1,), (1,)), ((), ()))
TN = (((0,), (0,)), ((), ()))


def _cp(*sem):
    return pltpu.CompilerParams(dimension_semantics=sem, vmem_limit_bytes=VMEM_LIMIT_V7X)


def _rms_r(xv):
    return lax.rsqrt(jnp.mean(xv * xv, axis=-1, keepdims=True) + EPS)


def _rms_bwd(dh, xv, g):
    r = _rms_r(xv)
    xh = xv * r
    dg = jnp.sum(dh * xh, axis=0, keepdims=True)
    dxh = dh * g
    dx = r * (dxh - xh * jnp.mean(dxh * xh, axis=-1, keepdims=True))
    return dx, dg


def _dot(a, b):
    return jnp.dot(a, b, preferred_element_type=F32)


def _dot_nt(a, b):
    return lax.dot_general(a, b, NT, preferred_element_type=F32)


def _dot_tn(a, b):
    return lax.dot_general(a, b, TN, preferred_element_type=F32)


def _norm_matmul(x, g, wblk, name, out_dtype=F32, transposed=False):
    T, D = x.shape
    NB, nb = wblk.shape[0], wblk.shape[1 if transposed else 2]
    mm = _dot_nt if transposed else _dot
    tm = min(T, PROJ_TM)

    def body(x_ref, g_ref, w_ref, o_ref, h_ref):
        for c in range(PROJ_CHAINS):
            rows = slice(c * tm // PROJ_CHAINS, (c + 1) * tm // PROJ_CHAINS)
            xv = x_ref[rows, :]
            hb = (xv * _rms_r(xv) * g_ref[...]).astype(BF16)
            h_ref[rows, :] = hb
            for q in range(NB):
                o_ref[rows, q * nb:(q + 1) * nb] = mm(hb, w_ref[q]).astype(o_ref.dtype)

    return pl.pallas_call(
        body, name=name, grid=(T // tm,),
        in_specs=[pl.BlockSpec((tm, D), lambda i: (i, 0)),
                  pl.BlockSpec((1, D), lambda i: (0, 0)),
                  pl.BlockSpec(wblk.shape, lambda i: (0, 0, 0))],
        out_specs=[pl.BlockSpec((tm, NB * nb), lambda i: (i, 0)),
                   pl.BlockSpec((tm, D), lambda i: (i, 0))],
        out_shape=[S((T, NB * nb), out_dtype), S((T, D), BF16)],
        compiler_params=_cp("parallel"),
    )(x, g, wblk)


def _matmul_res(a, w, res, name):
    T, K = a.shape
    N = w.shape[1]
    tm = min(T, PROJ_TM)

    def body(a_ref, w_ref, r_ref, o_ref):
        o_ref[...] = r_ref[...] + _dot(a_ref[...], w_ref[...])

    return pl.pallas_call(
        body, name=name, grid=(T // tm,),
        in_specs=[pl.BlockSpec((tm, K), lambda i: (i, 0)),
                  pl.BlockSpec((K, N), lambda i: (0, 0)),
                  pl.BlockSpec((tm, N), lambda i: (i, 0))],
        out_specs=pl.BlockSpec((tm, N), lambda i: (i, 0)),
        out_shape=S((T, N), F32),
        compiler_params=_cp("parallel"),
    )(a, w, res)


def _matmul_nt(a, w, name, out_dtype, after=()):
    T, N = a.shape
    K = w.shape[0]
    tm = min(T, PROJ_TM)

    def body(a_ref, w_ref, *rest):
        o_ref, ab_ref = rest[len(after):]
        ab = a_ref[...].astype(BF16)
        ab_ref[...] = ab
        o_ref[...] = _dot_nt(ab, w_ref[...]).astype(o_ref.dtype)

    return pl.pallas_call(
        body, name=name, grid=(T // tm,),
        in_specs=[pl.BlockSpec((tm, N), lambda i: (i, 0)),
                  pl.BlockSpec((K, N), lambda i: (0, 0))] + [pl.BlockSpec(memory_space=pl.ANY)] * len(after),
        out_specs=[pl.BlockSpec((tm, K), lambda i: (i, 0)),
                   pl.BlockSpec((tm, N), lambda i: (i, 0))],
        out_shape=[S((T, K), out_dtype), S((T, N), BF16)],
        compiler_params=_cp("parallel"),
    )(a, w, *after)


def _matmul_tn(a, b3, nb, name, blocked, after=()):
    T, M = a.shape
    SB, _, N = b3.shape
    per = N // nb
    NB = SB * per
    tk = min(T, 1024)
    nk = T // tk
    jb = max(1, TN_STEP_COLS // nb) if blocked else 1
    assert per % jb == 0
    if blocked:
        out_spec, out_shape = pl.BlockSpec((jb, M, nb), lambda j: (j, 0, 0)), S((NB, M, nb), BF16)
    else:
        assert SB == 1
        out_spec, out_shape = pl.BlockSpec((M, nb), lambda j: (0, j)), S((M, N), BF16)

    def body(a_ref, b_ref, *rest):
        o_ref = rest[-1]
        for q in range(jb):
            acc = None
            for k in range(nk):
                rows = slice(k * tk, (k + 1) * tk)
                part = _dot_tn(a_ref[rows, :], b_ref[rows, q * nb:(q + 1) * nb])
                acc = part if acc is None else acc + part
            if blocked:
                o_ref[q] = acc.astype(BF16)
            else:
                o_ref[...] = acc.astype(BF16)

    return pl.pallas_call(
        body, name=name, grid=(NB // jb,),
        in_specs=[pl.BlockSpec((T, M), lambda j: (0, 0), pipeline_mode=pl.Buffered(1)),
                  pl.BlockSpec((None, T, jb * nb), lambda j: ((j * jb) // per, 0, ((j * jb) % per) // jb))]
        + [pl.BlockSpec(memory_space=pl.ANY)] * len(after),
        out_specs=out_spec,
        out_shape=out_shape,
        compiler_params=_cp("parallel"),
    )(a, b3, *after)


def _nt_normbwd(dz3, wblk, x, g, dres, name, after=(), transposed=False):
    T, D = x.shape
    NB, nb = wblk.shape[0], wblk.shape[1 if transposed else 2]
    mm = _dot if transposed else _dot_nt
    SB, _, N = dz3.shape
    per = N // nb
    tm = min(T, PROJ_TM)

    def body(dz_ref, w_ref, x_ref, g_ref, dr_ref, *rest):
        dx_ref, dg_ref = rest[len(after):]

        @pl.when(pl.program_id(0) == 0)
        def _():
            dg_ref[...] = jnp.zeros_like(dg_ref)

        for c in range(PROJ_CHAINS):
            rows = slice(c * tm // PROJ_CHAINS, (c + 1) * tm // PROJ_CHAINS)
            dh = None
            for q in range(NB):
                cols = slice((q % per) * nb, (q % per + 1) * nb)
                part = mm(dz_ref[q // per, rows, cols], w_ref[q])
                dh = part if dh is None else dh + part
            dx, dg = _rms_bwd(dh, x_ref[rows, :], g_ref[...])
            dx_ref[rows, :] = dr_ref[rows, :] + dx
            dg_ref[...] += dg

    return pl.pallas_call(
        body, name=name, grid=(T // tm,),
        in_specs=[pl.BlockSpec((SB, tm, N), lambda i: (0, i, 0)),
                  pl.BlockSpec(wblk.shape, lambda i: (0, 0, 0)),
                  pl.BlockSpec((tm, D), lambda i: (i, 0)),
                  pl.BlockSpec((1, D), lambda i: (0, 0)),
                  pl.BlockSpec((tm, D), lambda i: (i, 0))] + [pl.BlockSpec(memory_space=pl.ANY)] * len(after),
        out_specs=[pl.BlockSpec((tm, D), lambda i: (i, 0)),
                   pl.BlockSpec((1, D), lambda i: (0, 0))],
        out_shape=[S((T, D), F32), S((1, D), F32)],
        compiler_params=_cp("arbitrary"),
    )(dz3, wblk, x, g, dres, *after)


def _loss_head(xv, tv, gv, D):
    err = xv * _rms_r(xv) * gv - tv
    e2 = jnp.sum(jnp.sum(err * err, axis=-1, keepdims=True), axis=0, keepdims=True)
    dx, dg = _rms_bwd(err * (1.0 / D), xv, gv)
    return (0.5 / D) * e2, dx, dg


def _mlp_fwd(x, g, wup, wdown, name, head=None):
    T, D = x.shape
    NB, _, fb = wup.shape
    tm = min(T, MLP_TM)
    n_head = 0 if head is None else 2

    def body(x_ref, g_ref, wu_ref, wd_ref, *rest):
        xo_ref, a_ref, h_ref = rest[n_head:n_head + 3]
        xv = x_ref[...]
        hb = (xv * _rms_r(xv) * g_ref[...]).astype(BF16)
        h_ref[...] = hb
        acc = xv
        for j in range(NB):
            a = _dot(hb, wu_ref[j])
            a_ref[:, j * fb:(j + 1) * fb] = a.astype(BF16)
            u = jnp.maximum(a, 0.0)
            acc = acc + _dot((u * u).astype(BF16), wd_ref[j * fb:(j + 1) * fb, :])

        if head is None:
            xo_ref[...] = acc
        else:
            t_ref, gf_ref = rest[:2]
            loss_ref, dgf_ref = rest[n_head + 3:n_head + 5]

            @pl.when(pl.program_id(0) == 0)
            def _():
                loss_ref[...] = jnp.zeros_like(loss_ref)
                dgf_ref[...] = jnp.zeros_like(dgf_ref)

            e2, dx, dg = _loss_head(acc, t_ref[...], gf_ref[...], D)
            xo_ref[...] = dx
            loss_ref[...] += e2
            dgf_ref[...] += dg

    row = pl.BlockSpec((tm, D), lambda i: (i, 0))
    vec = pl.BlockSpec((1, D), lambda i: (0, 0))
    once = pl.Buffered(1)
    in_specs = [row, vec, pl.BlockSpec((NB, D, fb), lambda i: (0, 0, 0), pipeline_mode=once),
                pl.BlockSpec((NB * fb, D), lambda i: (0, 0), pipeline_mode=once)]
    out_specs = [row, pl.BlockSpec((tm, NB * fb), lambda i: (i, 0)), row]
    out_shape = [S((T, D), F32), S((T, NB * fb), BF16), S((T, D), BF16)]
    if head is not None:
        in_specs += [row, vec]
        out_specs += [pl.BlockSpec((1, 128), lambda i: (0, 0)), vec]
        out_shape += [S((1, 128), F32), S((1, D), F32)]
    return pl.pallas_call(
        body, name=name, grid=(T // tm,),
        in_specs=in_specs, out_specs=out_specs, out_shape=out_shape,
        compiler_params=_cp("parallel" if head is None else "arbitrary"),
    )(x, g, wup, wdown, *(head or ()))


def _mlp_bwd_dx(x, dout, a, g, wup, wdown, name):
    T, D = x.shape
    NB, _, fb = wup.shape
    tm = min(T, MLP_TM)

    def body(x_ref, do_ref, a_ref, g_ref, wu_ref, wd_ref, dx_ref, da_ref, dob_ref, dg_ref):
        @pl.when(pl.program_id(0) == 0)
        def _():
            dg_ref[...] = jnp.zeros_like(dg_ref)

        dov = do_ref[...]
        dob = dov.astype(BF16)
        dob_ref[...] = dob
        dh = None
        for j in range(NB):
            cols = slice(j * fb, (j + 1) * fb)
            du2 = _dot_nt(dob, wd_ref[cols, :])
            u = jnp.maximum(a_ref[:, cols].astype(F32), 0.0)
            da = (du2 * (2.0 * u)).astype(BF16)
            da_ref[:, cols] = da
            part = _dot_nt(da, wu_ref[j])
            dh = part if dh is None else dh + part
        dx, dg = _rms_bwd(dh, x_ref[...], g_ref[...])
        dx_ref[...] = dov + dx
        dg_ref[...] += dg

    row = pl.BlockSpec((tm, D), lambda i: (i, 0))
    wide = pl.BlockSpec((tm, NB * fb), lambda i: (i, 0))
    vec = pl.BlockSpec((1, D), lambda i: (0, 0))
    once = pl.Buffered(1)
    return pl.pallas_call(
        body, name=name, grid=(T // tm,),
        in_specs=[row, row, wide, vec, pl.BlockSpec((NB, D, fb), lambda i: (0, 0, 0), pipeline_mode=once),
                  pl.BlockSpec((NB * fb, D), lambda i: (0, 0), pipeline_mode=once)],
        out_specs=[row, wide, row, vec],
        out_shape=[S((T, D), F32), S((T, NB * fb), BF16), S((T, D), BF16), S((1, D), F32)],
        compiler_params=_cp("arbitrary"),
    )(x, dout, a, g, wup, wdown)


def _mlp_bwd_dw(h, da, a, dob, fb, name):
    T, D = h.shape
    F = a.shape[1]
    NB = F // fb
    tk = min(T, 1024)
    nk = T // tk

    def body(h_ref, da_ref, a_ref, dob_ref, dwu_ref, dwd_ref):
        au = ad = None
        for k in range(nk):
            rows = slice(k * tk, (k + 1) * tk)
            pu = _dot_tn(h_ref[rows, :], da_ref[rows, :])
            u = jnp.maximum(a_ref[rows, :].astype(F32), 0.0)
            pd = _dot_tn((u * u).astype(BF16), dob_ref[rows, :])
            au, ad = (pu, pd) if au is None else (au + pu, ad + pd)
        dwu_ref[...] = au.astype(BF16)
        dwd_ref[...] = ad.astype(BF16)

    once = pl.Buffered(1)
    return pl.pallas_call(
        body, name=name, grid=(NB,),
        in_specs=[pl.BlockSpec((T, D), lambda j: (0, 0), pipeline_mode=once),
                  pl.BlockSpec((T, fb), lambda j: (0, j)),
                  pl.BlockSpec((T, fb), lambda j: (0, j)),
                  pl.BlockSpec((T, D), lambda j: (0, 0), pipeline_mode=once)],
        out_specs=[pl.BlockSpec((None, D, fb), lambda j: (j, 0, 0)),
                   pl.BlockSpec((fb, D), lambda j: (j, 0))],
        out_shape=[S((NB, D, fb), BF16), S((F, D), BF16)],
        compiler_params=_cp("parallel"),
    )(h, da, a, dob)


def _rope_tables(L):
    nf = HEAD_DIM // 4
    t = jnp.arange(L, dtype=jnp.int32)
    row = (t // GRID_W).astype(F32)
    col = (t % GRID_W).astype(F32)
    inv = ROPE_THETA ** (-jnp.arange(nf, dtype=F32) / nf)
    ar = row[:, None] * inv
    ac = col[:, None] * inv
    cos = jnp.concatenate([jnp.cos(ar), jnp.cos(ar), jnp.cos(ac), jnp.cos(ac)], axis=-1)
    sin = jnp.concatenate([-jnp.sin(ar), jnp.sin(ar), -jnp.sin(ac), jnp.sin(ac)], axis=-1)
    return cos, sin


def _swap32(x):
    lane = lax.broadcasted_iota(jnp.int32, x.shape, 1)
    up = pltpu.roll(x, HEAD_DIM - 32, 1)
    down = pltpu.roll(x, 32, 1)
    return jnp.where((lane % 64) < 32, up, down)


def _qk_prep(qkv, qg, kg, cos, sin, L, name):
    T, W = qkv.shape
    nh = W // HEAD_DIM - 2 * N_KV
    tm = min(L, 512)
    lb = L // tm

    def body(qkv_ref, qg_ref, kg_ref, cos_ref, sin_ref, q_ref, k_ref, v_ref):
        c = cos_ref[...]
        s = sin_ref[...]
        for h in range(nh + N_KV):
            xh = qkv_ref[:, h * HEAD_DIM:(h + 1) * HEAD_DIM]
            gv = qg_ref[...] if h < nh else kg_ref[...]
            y = xh * _rms_r(xh) * gv
            y = (y * c + _swap32(y) * s).astype(BF16)
            if h < nh:
                q_ref[:, h * HEAD_DIM:(h + 1) * HEAD_DIM] = y
            else:
                k_ref[:, (h - nh) * HEAD_DIM:(h - nh + 1) * HEAD_DIM] = y
        v_ref[...] = qkv_ref[:, (nh + N_KV) * HEAD_DIM:].astype(BF16)

    return pl.pallas_call(
        body, name=name, grid=(T // tm,),
        in_specs=[pl.BlockSpec((tm, W), lambda i: (i, 0)),
                  pl.BlockSpec((1, HEAD_DIM), lambda i: (0, 0)),
                  pl.BlockSpec((1, HEAD_DIM), lambda i: (0, 0)),
                  pl.BlockSpec((tm, HEAD_DIM), lambda i: (i % lb, 0)),
                  pl.BlockSpec((tm, HEAD_DIM), lambda i: (i % lb, 0))],
        out_specs=[pl.BlockSpec((tm, nh * HEAD_DIM), lambda i: (i, 0)),
                   pl.BlockSpec((tm, N_KV * HEAD_DIM), lambda i: (i, 0)),
                   pl.BlockSpec((tm, N_KV * HEAD_DIM), lambda i: (i, 0))],
        out_shape=[S((T, nh * HEAD_DIM), BF16), S((T, N_KV * HEAD_DIM), BF16), S((T, N_KV * HEAD_DIM), BF16)],
        compiler_params=_cp("parallel"),
    )(qkv, qg, kg, cos, sin)


def _qk_prep_bwd(qkv, dq, dk, dv, qg, kg, cos, sin, L, name):
    T, W = qkv.shape
    nh = W // HEAD_DIM - 2 * N_KV
    tm = min(L, 512)
    lb = L // tm

    def body(qkv_ref, dq_ref, dk_ref, dv_ref, qg_ref, kg_ref, cos_ref, sin_ref, dz_ref, dqg_ref, dkg_ref):
        @pl.when(pl.program_id(0) == 0)
        def _():
            dqg_ref[...] = jnp.zeros_like(dqg_ref)
            dkg_ref[...] = jnp.zeros_like(dkg_ref)

        c = cos_ref[...]
        s = sin_ref[...]
        for h in range(nh + N_KV):
            cols = slice(h * HEAD_DIM, (h + 1) * HEAD_DIM)
            if h < nh:
                dout, gv, dg_ref = dq_ref[:, cols], qg_ref[...], dqg_ref
            else:
                kc = slice((h - nh) * HEAD_DIM, (h - nh + 1) * HEAD_DIM)
                dout, gv, dg_ref = dk_ref[:, kc], kg_ref[...], dkg_ref
            dy = dout * c - _swap32(dout) * s
            dx, dg = _rms_bwd(dy, qkv_ref[:, cols], gv)
            dg_ref[...] += dg
            dz_ref[:, cols] = dx.astype(BF16)
        dz_ref[:, (nh + N_KV) * HEAD_DIM:] = dv_ref[...].astype(BF16)

    return pl.pallas_call(
        body, name=name, grid=(T // tm,),
        in_specs=[pl.BlockSpec((tm, W), lambda i: (i, 0)),
                  pl.BlockSpec((tm, nh * HEAD_DIM), lambda i: (i, 0)),
                  pl.BlockSpec((tm, N_KV * HEAD_DIM), lambda i: (i, 0)),
                  pl.BlockSpec((tm, N_KV * HEAD_DIM), lambda i: (i, 0)),
                  pl.BlockSpec((1, HEAD_DIM), lambda i: (0, 0)),
                  pl.BlockSpec((1, HEAD_DIM), lambda i: (0, 0)),
                  pl.BlockSpec((tm, HEAD_DIM), lambda i: (i % lb, 0)),
                  pl.BlockSpec((tm, HEAD_DIM), lambda i: (i % lb, 0))],
        out_specs=[pl.BlockSpec((tm, W), lambda i: (i, 0)),
                   pl.BlockSpec((1, HEAD_DIM), lambda i: (0, 0)),
                   pl.BlockSpec((1, HEAD_DIM), lambda i: (0, 0))],
        out_shape=[S((T, W), BF16), S((1, HEAD_DIM), F32), S((1, HEAD_DIM), F32)],
        compiler_params=_cp("arbitrary"),
    )(qkv, dq, dk, dv, qg, kg, cos, sin)


EXP2_SCALE = SOFTMAX_SCALE * math.log2(math.e)
ATTN_SUB = 256
ATTN_TQ = 1024


def _softmax_rows(q, k):
    s = _dot_nt(q, k)
    e = jnp.exp2((s - jnp.max(s, axis=-1, keepdims=True)) * EXP2_SCALE)
    return e, jnp.sum(e, axis=-1, keepdims=True)


def _attn_fwd(q, k, v, L, name):
    T = q.shape[0]
    nh = q.shape[1] // HEAD_DIM
    G = nh // N_KV
    B = T // L
    tq = min(L, ATTN_TQ)
    nq = L // tq
    sub = min(tq, ATTN_SUB)

    def body(q_ref, k_ref, v_ref, o_ref):
        for h in range(tq // sub):
            rows = slice(h * sub, (h + 1) * sub)
            e, l = _softmax_rows(q_ref[rows, :], k_ref[...])
            o_ref[rows, :] = (_dot(e.astype(BF16), v_ref[...]) / l).astype(BF16)

    qspec = pl.BlockSpec((tq, HEAD_DIM), lambda b, kv, g, qi: (b * nq + qi, kv * G + g))
    kspec = pl.BlockSpec((L, HEAD_DIM), lambda b, kv, g, qi: (b, kv))
    return pl.pallas_call(
        body, name=name, grid=(B, N_KV, G, nq),
        in_specs=[qspec, kspec, kspec],
        out_specs=qspec,
        out_shape=S((T, nh * HEAD_DIM), BF16),
        compiler_params=_cp("parallel", "parallel", "parallel", "parallel"),
    )(q, k, v)


def _attn_bwd(q, k, v, do, o, L, name):
    T = q.shape[0]
    nh = q.shape[1] // HEAD_DIM
    G = nh // N_KV
    B = T // L
    tq = min(L, ATTN_TQ)
    nq = L // tq

    sub = min(tq, ATTN_SUB)

    def body(q_ref, k_ref, v_ref, do_ref, o_ref, dq_ref, dk_ref, dv_ref, ds_scr, p_scr):
        first = (pl.program_id(2) == 0) & (pl.program_id(3) == 0)
        last = (pl.program_id(2) == G - 1) & (pl.program_id(3) == nq - 1)

        @pl.when(first)
        def _():
            dk_ref[...] = jnp.zeros_like(dk_ref)
            dv_ref[...] = jnp.zeros_like(dv_ref)

        for h in range(tq // sub):
            rows = slice(h * sub, (h + 1) * sub)
            dov = do_ref[rows, :]
            e, l = _softmax_rows(q_ref[rows, :], k_ref[...])
            p = e * (1.0 / l)
            dsum = jnp.sum(dov.astype(F32) * o_ref[rows, :].astype(F32), axis=-1, keepdims=True)
            ds_scr[rows, :] = (p * (_dot_nt(dov, v_ref[...]) - dsum)).astype(BF16)
            p_scr[rows, :] = p.astype(BF16)
        ds = ds_scr[...]
        dq_ref[...] = _dot(ds, k_ref[...]) * SOFTMAX_SCALE
        dk_ref[...] += _dot_tn(ds, q_ref[...])
        dv_ref[...] += _dot_tn(p_scr[...], do_ref[...])

        @pl.when(last)
        def _():
            dk_ref[...] = dk_ref[...] * SOFTMAX_SCALE

    qspec = pl.BlockSpec((tq, HEAD_DIM), lambda b, kv, g, qi: (b * nq + qi, kv * G + g))
    kspec = pl.BlockSpec((L, HEAD_DIM), lambda b, kv, g, qi: (b, kv))
    return pl.pallas_call(
        body, name=name, grid=(B, N_KV, G, nq),
        in_specs=[qspec, kspec, kspec, qspec, qspec],
        out_specs=[qspec, kspec, kspec],
        out_shape=[S((T, nh * HEAD_DIM), F32), S((T, N_KV * HEAD_DIM), F32), S((T, N_KV * HEAD_DIM), F32)],
        scratch_shapes=[pltpu.VMEM((tq, L), BF16), pltpu.VMEM((tq, L), BF16)],
        compiler_params=_cp("parallel", "parallel", "arbitrary", "arbitrary"),
    )(q, k, v, do, o)


PV_CONV_W = 0
PV_B_A = 8
PV_B_X = 16
PV_LAM = 24
PV_CONV_B = 32
PV_ROWS = 40


def _shift_rows(x, k):
    if k == 0:
        return x
    L = x.shape[0]
    n = N_SEG * abs(k)
    seg = lax.broadcasted_iota(jnp.int32, (n, x.shape[1]), 0) % N_SEG
    if k > 0:
        edge = jnp.where(seg == 0, 0.0, pltpu.roll(x[L - n:], 1, 0))
        return jnp.concatenate([edge, x[:L - n]], axis=0)
    edge = jnp.where(seg == N_SEG - 1, 0.0, pltpu.roll(x[:n], n - 1, 0))
    return jnp.concatenate([x[n:], edge], axis=0)


def _conv_taps(rec, pv):
    c = pv[PV_CONV_B:PV_CONV_B + 1]
    for j in range(CONV_W):
        c = c + pv[PV_CONV_W + j:PV_CONV_W + j + 1] * _shift_rows(rec, 2 - j)
    return c


def _sigmoid(x):
    return 0.5 * jnp.tanh(0.5 * x) + 0.5


EXPM1_SERIES_BELOW = 0.03


def _rg_gates(c, cbf, wa, wx, ba, bx, lam):
    r = _sigmoid(_dot(cbf, wa) + ba)
    i = _sigmoid(_dot(cbf, wx) + bx)
    sp = jnp.maximum(-lam, 0.0) + jnp.log1p(jnp.exp(-jnp.abs(lam)))
    la = r * ((-RG_C) * sp)
    a = jnp.exp(la)
    a2 = a * a
    x = la + la
    series = -(x * ((x * (1.0 / 6.0) + 0.5) * x + 1.0))
    om = jnp.where(x > -EXPM1_SERIES_BELOW, series, 1.0 - a2)
    rm = lax.rsqrt(om)
    return r, i, a, om * rm, rm, a2, sp


def _gelu(x):
    t = jnp.tanh(GELU_K * (x + GELU_C * x * x * x))
    return 0.5 * x * (1.0 + t), t


def _scan_pair(af_ref, uf_ref, ab_ref, ub_ref, hf_ref, hb_ref, pf_ref, pb_ref, L):
    ls = L // N_SEG
    zero = jnp.zeros((N_SEG, LRU_BW), F32)
    one = jnp.ones((N_SEG, LRU_BW), F32)
    tile = lambda t: pl.ds(pl.multiple_of(t * N_SEG, N_SEG), N_SEG)

    def steps(tc, carry):
        hf, pf, hb, pb = carry
        for q in range(SCAN_UNROLL):
            t = tc * SCAN_UNROLL + q
            rf, rb = tile(t), tile(ls - 1 - t)
            af = af_ref[rf, :]
            hf = af * hf + uf_ref[rf, :]
            pf = pf * af
            hf_ref[rf, :] = hf
            pf_ref[rf, :] = pf
            ab = ab_ref[rb, :]
            hb = ab * hb + ub_ref[rb, :]
            pb = pb * ab
            hb_ref[rb, :] = hb
            pb_ref[rb, :] = pb
        return hf, pf, hb, pb

    hf_e, pf_e, hb_e, pb_e = lax.fori_loop(0, ls // SCAN_UNROLL, steps, (zero, one, zero, one))

    rows, cin = [], jnp.zeros((1, LRU_BW), F32)
    for s in range(N_SEG):
        rows.append(cin)
        cin = hf_e[s:s + 1] + pf_e[s:s + 1] * cin
    cf = jnp.concatenate(rows, axis=0)
    rows, cin = [], jnp.zeros((1, LRU_BW), F32)
    for s in reversed(range(N_SEG)):
        rows.append(cin)
        cin = hb_e[s:s + 1] + pb_e[s:s + 1] * cin
    cb = jnp.concatenate(rows[::-1], axis=0)

    def fix(tc, _):
        for q in range(SCAN_UNROLL):
            r = tile(tc * SCAN_UNROLL + q)
            hf_ref[r, :] = hf_ref[r, :] + pf_ref[r, :] * cf
            hb_ref[r, :] = hb_ref[r, :] + pb_ref[r, :] * cb
        return 0

    lax.fori_loop(0, ls // SCAN_UNROLL, fix, 0)


def _rg_specs(L, D, nblk):
    slab = lambda off: pl.BlockSpec((L, LRU_BW), lambda cb, b: (b, off + cb))
    wspec = pl.BlockSpec((2, None, LRU_BW, LRU_BW), lambda cb, b: (0, cb, 0, 0))
    pvspec = pl.BlockSpec((PV_ROWS, LRU_BW), lambda cb, b: (0, cb))
    return slab, wspec, pvspec


def _rg_fwd(z, pvec, wa, wx, L, name):
    T, C2 = z.shape
    C = C2 // 2
    nblk = C // LRU_BW
    B = T // L
    slab, wspec, pvspec = _rg_specs(L, C, nblk)

    def body(gp_ref, rec_ref, pv_ref, wa_ref, wx_ref, yg_ref, hf_ref, hb_ref, a_scr, u_scr, p_scr):
        pv = pv_ref[...]
        c = _conv_taps(rec_ref[...], pv)
        cbf = c.astype(BF16)
        for d in range(2):
            _, i, a, m, _, _, _ = _rg_gates(c, cbf, wa_ref[d], wx_ref[d], pv[PV_B_A + d:PV_B_A + d + 1],
                                      pv[PV_B_X + d:PV_B_X + d + 1], pv[PV_LAM + d:PV_LAM + d + 1])
            a_scr[d] = a
            u_scr[d] = m * (i * c)
        _scan_pair(a_scr.at[0], u_scr.at[0], a_scr.at[1], u_scr.at[1], hf_ref, hb_ref, p_scr.at[0], p_scr.at[1], L)
        gate, _ = _gelu(gp_ref[...])
        yg_ref[...] = ((hf_ref[...] + hb_ref[...]) * gate).astype(BF16)

    return pl.pallas_call(
        body, name=name, grid=(nblk, B),
        in_specs=[slab(0), slab(nblk), pvspec, wspec, wspec],
        out_specs=[slab(0), slab(0), slab(0)],
        out_shape=[S((T, C), BF16), S((T, C), F32), S((T, C), F32)],
        scratch_shapes=[pltpu.VMEM((2, L, LRU_BW), F32)] * 3,
        compiler_params=_cp("parallel", "parallel"),
    )(z, z, pvec, wa, wx)


def _rg_bwd(z, hf, hb, dyg, pvec, wa, wx, L, name):
    T, C2 = z.shape
    C = C2 // 2
    nblk = C // LRU_BW
    B = T // L
    slab, wspec, pvspec = _rg_specs(L, C, nblk)

    def body(gp_ref, rec_ref, hf_ref, hb_ref, dyg_ref, pv_ref, wa_ref, wx_ref,
             dz_ref, dwa_ref, dwx_ref, dpv_ref, a_scr, u_scr, d_scr, p_scr):
        @pl.when(pl.program_id(1) == 0)
        def _():
            dwa_ref[...] = jnp.zeros_like(dwa_ref)
            dwx_ref[...] = jnp.zeros_like(dwx_ref)
            dpv_ref[...] = jnp.zeros_like(dpv_ref)

        pv = pv_ref[...]
        rec = rec_ref[...]
        c = _conv_taps(rec, pv)
        cbf = c.astype(BF16)
        gp = gp_ref[...]
        gate, th = _gelu(gp)
        dgelu = 0.5 * (1.0 + th) + 0.5 * gp * (1.0 - th * th) * GELU_K * (1.0 + 3.0 * GELU_C * gp * gp)
        dyg = dyg_ref[...]
        dz_ref[0] = (dyg * (hf_ref[...] + hb_ref[...]) * dgelu).astype(BF16)
        dy = dyg * gate

        gates = []
        for d in range(2):
            gates.append(_rg_gates(c, cbf, wa_ref[d], wx_ref[d], pv[PV_B_A + d:PV_B_A + d + 1],
                                   pv[PV_B_X + d:PV_B_X + d + 1], pv[PV_LAM + d:PV_LAM + d + 1]))
        a_scr[0] = _shift_rows(gates[1][2], 1)
        a_scr[1] = _shift_rows(gates[0][2], -1)
        u_scr[...] = dy
        _scan_pair(a_scr.at[0], u_scr, a_scr.at[1], u_scr, d_scr.at[1], d_scr.at[0], p_scr.at[0], p_scr.at[1], L)

        dc = jnp.zeros_like(c)
        rows = []
        for d in range(2):
            r, i, a, m, rm, a2, sp = gates[d]
            delta = d_scr[d]
            hnb = _shift_rows(hf_ref[...], 1) if d == 0 else _shift_rows(hb_ref[...], -1)
            da = delta * hnb
            dm = delta * (i * c)
            di = delta * (m * c)
            dc = dc + delta * (m * i)
            dla = da * a - dm * (a2 * rm)
            dpa = (dla * ((-RG_C) * sp)) * (r * (1.0 - r))
            dpx = di * (i * (1.0 - i))
            dsp = (-RG_C) * jnp.sum(dla * r, axis=0, keepdims=True)
            lam = pv[PV_LAM + d:PV_LAM + d + 1]
            rows.append((jnp.sum(dpa, axis=0, keepdims=True), jnp.sum(dpx, axis=0, keepdims=True),
                         -dsp * _sigmoid(-lam)))
            dpab = dpa.astype(BF16)
            dpxb = dpx.astype(BF16)
            dwa_ref[d] += _dot_tn(cbf, dpab)
            dwx_ref[d] += _dot_tn(cbf, dpxb)
            dc = dc + _dot_nt(dpab, wa_ref[d]) + _dot_nt(dpxb, wx_ref[d])

        drec = jnp.zeros_like(c)
        dcw = []
        for j in range(CONV_W):
            drec = drec + pv[PV_CONV_W + j:PV_CONV_W + j + 1] * _shift_rows(dc, j - 2)
            dcw.append(jnp.sum(dc * _shift_rows(rec, 2 - j), axis=0, keepdims=True))
        dz_ref[1] = drec.astype(BF16)
        for j in range(CONV_W):
            dpv_ref[PV_CONV_W + j:PV_CONV_W + j + 1, :] += dcw[j]
        for d in range(2):
            dpv_ref[PV_B_A + d:PV_B_A + d + 1, :] += rows[d][0]
            dpv_ref[PV_B_X + d:PV_B_X + d + 1, :] += rows[d][1]
            dpv_ref[PV_LAM + d:PV_LAM + d + 1, :] += rows[d][2]
        dpv_ref[PV_CONV_B:PV_CONV_B + 1, :] += jnp.sum(dc, axis=0, keepdims=True)

    return pl.pallas_call(
        body, name=name, grid=(nblk, B),
        in_specs=[slab(0), slab(nblk), slab(0), slab(0), slab(0), pvspec, wspec, wspec],
        out_specs=[pl.BlockSpec((2, L, LRU_BW), lambda cb, b: (0, b, cb)), wspec, wspec, pvspec],
        out_shape=[S((2, T, C), BF16), S((2, nblk, LRU_BW, LRU_BW), F32), S((2, nblk, LRU_BW, LRU_BW), F32),
                   S((PV_ROWS, C), F32)],
        scratch_shapes=[pltpu.VMEM((2, L, LRU_BW), F32), pltpu.VMEM((L, LRU_BW), F32),
                        pltpu.VMEM((2, L, LRU_BW), F32), pltpu.VMEM((2, L, LRU_BW), F32)],
        compiler_params=_cp("parallel", "arbitrary"),
    )(z, z, hf, hb, dyg, pvec, wa, wx)


QKV_NB = 512


def _interleave(a):
    *lead, L, D = a.shape
    return a.reshape(*lead, N_SEG, L // N_SEG, D).swapaxes(-3, -2).reshape(*lead, L, D)


def _deinterleave(a):
    *lead, L, D = a.shape
    return a.reshape(*lead, L // N_SEG, N_SEG, D).swapaxes(-3, -2).reshape(*lead, L, D)


def _local_step(x3, tgt3, w, fetch, send):
    Bl, L, D = x3.shape
    T = Bl * L
    x = _interleave(x3).reshape(T, D)
    tgt = _interleave(tgt3).reshape(T, D)
    gm = [w["g_mix"][i:i + 1] for i in range(2)]
    gl = [w["g_mlp"][i:i + 1] for i in range(2)]

    w0 = fetch(0, ())
    nb_in = w0["w_in"].shape[-1]
    z, h0 = _norm_matmul(x, gm[0], w0["w_in"], "rg_in")
    yg, hf, hb = _rg_fwd(z, w0["pvec"], w["wa"], w["wx"], L, "rg_fwd")
    w1 = fetch(1, (yg,))
    x1 = _matmul_res(yg, w1["w_out"], x, "rg_out")
    w1.update(fetch(4, (x1,)))
    fb = w1["w_up0"].shape[-1]
    x2, a0, hm0 = _mlp_fwd(x1, gl[0], w1["w_up0"], w1["w_down0"], "mlp0_fwd")
    w2 = fetch(2, (x2,))
    qkv, h1 = _norm_matmul(x2, gm[1], w2["w_qkv"], "at_qkv", transposed=True)
    cos, sin = [_interleave(t) for t in _rope_tables(L)]
    qn, kn, vb = _qk_prep(qkv, w["qg"], w["kg"], cos, sin, L, "at_prep")
    o = _attn_fwd(qn, kn, vb, L, "at_fwd")
    x3_ = _matmul_res(o, w2["w_o"], x2, "at_out")
    w3 = fetch(3, (x3_,))
    dx4, a1, hm1, loss, dgf = _mlp_fwd(x3_, gl[1], w3["w_up1"], w3["w_down1"], "mlp1_fwd", head=(tgt, w["g_fin"]))

    dx3, da1, dob1, dgl1 = _mlp_bwd_dx(x3_, dx4, a1, gl[1], w3["w_up1"], w3["w_down1"], "mlp1_bwd_dx")
    dwu1, dwd1 = _mlp_bwd_dw(hm1, da1, a1, dob1, fb, "mlp1_bwd_dw")
    sent = send(3, dict(w_up1=dwu1, w_down1=dwd1))
    do, dx3b = _matmul_nt(dx3, w2["w_o"], "at_out_bwd", BF16, after=sent)
    dwo = _matmul_tn(o, dx3b[None], QKV_NB, "at_out_dw", blocked=False)
    dq, dk, dv = _attn_bwd(qn, kn, vb, do, o, L, "at_bwd")
    dqkv, dqg, dkg = _qk_prep_bwd(qkv, dq, dk, dv, w["qg"], w["kg"], cos, sin, L, "at_prep_bwd")
    dwqkv = _matmul_tn(dqkv, h1[None], QKV_NB, "at_qkv_dw", blocked=False)
    sent = send(2, dict(w_qkv=dwqkv, w_o=dwo))
    dx2, dgm1 = _nt_normbwd(dqkv[None], w2["w_qkv"], x2, gm[1], dx3, "at_qkv_bwd", after=sent, transposed=True)
    dx1, da0, dob0, dgl0 = _mlp_bwd_dx(x1, dx2, a0, gl[0], w1["w_up0"], w1["w_down0"], "mlp0_bwd_dx")
    dwu0, dwd0 = _mlp_bwd_dw(hm0, da0, a0, dob0, fb, "mlp0_bwd_dw")
    sent = send(1, dict(w_up0=dwu0, w_down0=dwd0))
    dyg, dx1b = _matmul_nt(dx1, w1["w_out"], "rg_out_bwd", F32, after=sent)
    dwout = _matmul_tn(yg, dx1b[None], QKV_NB, "rg_out_dw", blocked=False)
    dz, dwa, dwx, dpv = _rg_bwd(z, hf, hb, dyg, w0["pvec"], w["wa"], w["wx"], L, "rg_bwd")
    sent = send(4, dict(w_out=dwout, pvec=dpv, wa=dwa, wx=dwx))
    dwin = _matmul_tn(h0, dz, nb_in, "rg_in_dw", blocked=True, after=sent)
    sent = send(0, dict(w_in=dwin))
    dx0, dgm0 = _nt_normbwd(dz, w0["w_in"], x, gm[0], dx1, "rg_in_bwd", after=sent)
    send(-1, dict(g_mix=[dgm0, dgm1], g_mlp=[dgl0, dgl1], g_fin=dgf, conv_b=dpv[PV_CONV_B:PV_CONV_B + 1], qg=dqg, kg=dkg, loss=loss))
    return _deinterleave(dx0.reshape(Bl, L, D))


MESH = pl.DeviceIdType.MESH
ANY = pl.BlockSpec(memory_space=pl.ANY)
N_PEERS = N_DEV - 1


def _my_place():
    return lax.axis_index("x"), lax.axis_index("y"), lax.axis_index("c")


def _flat(px, py, pc):
    return 4 * px + 2 * py + pc


def _all_gather(shards, name):
    n = len(shards)

    def body(*refs):
        ins, outs = refs[:n], refs[n:2 * n]
        send_sems, recv_sems, local_sems = refs[2 * n:]
        x, y, c = _my_place()
        me, sibling = (x, y, c), (x, y, 1 - c)
        chips = [(1 - x, y), (x, 1 - y), (1 - x, 1 - y)]

        def copy(a, k, block, to, src=None):
            dst = outs[a].at[_flat(*block)]
            return pltpu.make_async_remote_copy(
                src_ref=dst if src is None else src, dst_ref=dst,
                send_sem=send_sems.at[a, k], recv_sem=recv_sems.at[a, k],
                device_id=to, device_id_type=MESH)

        mine = [pltpu.make_async_copy(ins[a], outs[a].at[_flat(*me)], local_sems.at[a]) for a in range(n)]
        for cp in mine:
            cp.start()
        first = []
        for a in range(n):
            first.append(copy(a, 0, me, sibling, src=ins[a]))
            first += [copy(a, 1 + j, me, (*chip, c), src=ins[a]) for j, chip in enumerate(chips)]
        for cp in first:
            cp.start()
        passed = []
        for j, chip in enumerate(chips):
            for a in range(n):
                copy(a, 1 + j, (*chip, c), me).wait_recv()
                fwd = copy(a, 4 + j, (*chip, c), sibling)
                fwd.start()
                passed.append(fwd)
        for a in range(n):
            copy(a, 0, sibling, me).wait_recv()
            for j, chip in enumerate(chips):
                copy(a, 4 + j, (*chip, 1 - c), me).wait_recv()
        for cp in first + passed:
            cp.wait_send()
        for cp in mine:
            cp.wait()

    return pl.pallas_call(
        body, name=name,
        in_specs=[ANY] * n, out_specs=[ANY] * n,
        out_shape=[S((N_DEV,) + s.shape, s.dtype) for s in shards],
        scratch_shapes=[pltpu.SemaphoreType.DMA((n, N_PEERS)), pltpu.SemaphoreType.DMA((n, N_PEERS)),
                        pltpu.SemaphoreType.DMA((n,))],
    )(*shards)


HBM = pl.BlockSpec(memory_space=pltpu.HBM)
SEM = pl.BlockSpec(memory_space=pltpu.SEMAPHORE)
SIDE_EFFECT = pltpu.SideEffectType.DATAFLOW_SIDE_EFFECTING
SEMS_PER_GROUP = 3


NEAR_PEERS = (1, 2, 4, 6)
FAR_CHIPS = (2, 4, 6)


def _exchange_copies(srcs, lands, sems, mode):
    send_sems, recv_sems, local_sems = sems
    scatter = mode == "scatter"
    x, y, c = _my_place()
    me = _flat(x, y, c)
    remote, local = [], []
    for a in range(len(srcs)):
        for r in (NEAR_PEERS if mode == "near" else range(1, N_DEV)):
            peer = (1 - x if r & 4 else x, 1 - y if r & 2 else y, 1 - c if r & 1 else c)
            remote.append(pltpu.make_async_remote_copy(
                src_ref=srcs[a].at[_flat(*peer)] if scatter else srcs[a], dst_ref=lands[a].at[me],
                send_sem=send_sems.at[a * N_PEERS + r - 1], recv_sem=recv_sems.at[a * N_PEERS + r - 1],
                device_id=peer, device_id_type=MESH))
        local.append(pltpu.make_async_copy(srcs[a].at[me] if scatter else srcs[a], lands[a].at[me], local_sems.at[a]))
    return remote, local


def _exchange_start(groups, modes, name):
    sizes = [len(g) for g in groups]
    srcs = [pltpu.with_memory_space_constraint(a, pltpu.HBM) for g in groups for a in g]
    n = len(srcs)
    scatter_of = [m == "scatter" for g, m in zip(groups, modes) for _ in g]
    lands = [pltpu.with_memory_space_constraint(lax.empty(a.shape if sc else (N_DEV,) + a.shape, a.dtype), pltpu.HBM)
             for a, sc in zip(srcs, scatter_of)]
    n_sem = SEMS_PER_GROUP * len(groups)

    def body(*refs):
        src_refs, land_refs, sem_refs, token = refs[:n], refs[n:2 * n], refs[2 * n:2 * n + n_sem], refs[-1]
        off = 0
        for gi, k in enumerate(sizes):
            remote, local = _exchange_copies(src_refs[off:off + k], land_refs[off:off + k],
                                             sem_refs[SEMS_PER_GROUP * gi:SEMS_PER_GROUP * (gi + 1)], modes[gi])
            for cp in local + remote:
                cp.start()
            off += k
        token[...] = jnp.zeros_like(token)

    sem_shapes = []
    for k in sizes:
        sem_shapes += [pltpu.SemaphoreType.DMA((k * N_PEERS,)), pltpu.SemaphoreType.DMA((k * N_PEERS,)),
                       pltpu.SemaphoreType.DMA((k,))]
    outs = pl.pallas_call(
        body, name=name,
        out_shape=sem_shapes + [pltpu.HBM(a.shape, a.dtype) for a in srcs + lands] + [S((8, 128), F32)],
        in_specs=[HBM] * (2 * n),
        out_specs=[SEM] * n_sem + [HBM] * (2 * n) + [pl.BlockSpec(memory_space=pltpu.VMEM)],
        input_output_aliases={i: n_sem + i for i in range(2 * n)},
        compiler_params=pltpu.CompilerParams(has_side_effects=SIDE_EFFECT),
    )(*srcs, *lands)
    sems, thru, token = outs[:n_sem], outs[n_sem:n_sem + 2 * n], outs[-1]
    per_group, off = [], 0
    for gi, k in enumerate(sizes):
        per_group.append((sems[SEMS_PER_GROUP * gi:SEMS_PER_GROUP * (gi + 1)], thru[off:off + k], thru[n + off:n + off + k]))
        off += k
    return per_group, token


def _exchange_wait(group, after, mode, name):
    sems, srcs, lands = group
    k = len(srcs)

    def body(*refs):
        remote, local = _exchange_copies(refs[:k], refs[k:2 * k], refs[2 * k:2 * k + SEMS_PER_GROUP], mode)
        for cp in remote:
            cp.wait_send()
            cp.wait_recv()
        for cp in local:
            cp.wait()

    outs = pl.pallas_call(
        body, name=name,
        out_shape=[pltpu.HBM(a.shape, a.dtype) for a in list(srcs) + list(lands)],
        in_specs=[HBM] * (2 * k) + [SEM] * SEMS_PER_GROUP + [ANY] * len(after),
        out_specs=[HBM] * (2 * k),
        input_output_aliases={i: i for i in range(2 * k)},
        compiler_params=pltpu.CompilerParams(has_side_effects=SIDE_EFFECT),
    )(*srcs, *lands, *sems, *after)
    return outs[k:]


def _forward_copies(lands, sems):
    send_sems, recv_sems = sems
    x, y, c = _my_place()
    mine, theirs = [], []
    for a in range(len(lands)):
        for k, r in enumerate(FAR_CHIPS):
            px, py = (1 - x if r & 4 else x), (1 - y if r & 2 else y)
            for out, core in ((mine, c), (theirs, 1 - c)):
                blk = lands[a].at[_flat(px, py, core)]
                out.append(pltpu.make_async_remote_copy(
                    src_ref=blk, dst_ref=blk, send_sem=send_sems.at[a * len(FAR_CHIPS) + k],
                    recv_sem=recv_sems.at[a * len(FAR_CHIPS) + k], device_id=(x, y, 1 - c), device_id_type=MESH))
    return mine, theirs


def _forward_start(groups, name):
    sizes = [len(g) for g in groups]
    lands = [a for g in groups for a in g]
    n = len(lands)
    n_sem = 2 * len(groups)

    def body(*refs):
        land_refs, sem_refs, token = refs[:n], refs[n:n + n_sem], refs[-1]
        off = 0
        for gi, k in enumerate(sizes):
            mine, _ = _forward_copies(land_refs[off:off + k], sem_refs[2 * gi:2 * gi + 2])
            for cp in mine:
                cp.start()
            off += k
        token[...] = jnp.zeros_like(token)

    sem_shapes = []
    for k in sizes:
        sem_shapes += [pltpu.SemaphoreType.DMA((k * len(FAR_CHIPS),))] * 2
    outs = pl.pallas_call(
        body, name=name,
        out_shape=sem_shapes + [pltpu.HBM(a.shape, a.dtype) for a in lands] + [S((8, 128), F32)],
        in_specs=[HBM] * n,
        out_specs=[SEM] * n_sem + [HBM] * n + [pl.BlockSpec(memory_space=pltpu.VMEM)],
        input_output_aliases={i: n_sem + i for i in range(n)},
        compiler_params=pltpu.CompilerParams(has_side_effects=SIDE_EFFECT),
    )(*lands)
    per_group, off = [], 0
    for gi, k in enumerate(sizes):
        per_group.append((outs[2 * gi:2 * gi + 2], outs[n_sem + off:n_sem + off + k]))
        off += k
    return per_group


def _forward_wait(group, after, name):
    sems, lands = group
    k = len(lands)

    def body(*refs):
        mine, theirs = _forward_copies(refs[:k], refs[k:k + 2])
        for cp in mine:
            cp.wait_send()
        for cp in theirs:
            cp.wait_recv()

    return pl.pallas_call(
        body, name=name,
        out_shape=[pltpu.HBM(a.shape, a.dtype) for a in lands],
        in_specs=[HBM] * k + [SEM] * 2 + [ANY] * len(after),
        out_specs=[HBM] * k,
        input_output_aliases={i: i for i in range(k)},
        compiler_params=pltpu.CompilerParams(has_side_effects=SIDE_EFFECT),
    )(*lands, *sems, *after)


def _row_tile(rows, cols):
    want = max(16, (128 * 1024) // cols)
    if rows <= want:
        return rows
    t = want - want % 16
    while rows % t:
        t -= 16
    return t


def _sum_parts(parts, name, after=()):
    P, R, C = parts.shape
    tr = _row_tile(R, C)

    def body(p_ref, *rest):
        o_ref = rest[-1]
        g = p_ref[0].astype(F32)
        for i in range(1, P):
            g = g + p_ref[i].astype(F32)
        o_ref[...] = g

    return pl.pallas_call(
        body, name=name, grid=(R // tr,),
        in_specs=[pl.BlockSpec((P, tr, C), lambda i: (0, i, 0))] + [ANY] * len(after),
        out_specs=pl.BlockSpec((tr, C), lambda i: (i, 0)),
        out_shape=S((R, C), F32),
        compiler_params=_cp("parallel"),
    )(parts, *after)


def _adamw(parts, w, m, v, name, after=()):
    P, R, C = parts.shape
    tr = _row_tile(R, C)
    c1 = 1.0 - ADAM_B1 ** ADAM_STEP
    c2 = 1.0 - ADAM_B2 ** ADAM_STEP

    def body(p_ref, w_ref, m_ref, v_ref, *rest):
        g_ref, d_ref, mo_ref, vo_ref = rest[len(after):]
        g = p_ref[0].astype(F32)
        for i in range(1, P):
            g = g + p_ref[i].astype(F32)
        mn = ADAM_B1 * m_ref[...] + (1.0 - ADAM_B1) * g
        vn = ADAM_B2 * v_ref[...] + (1.0 - ADAM_B2) * (g * g)
        g_ref[...] = g
        mo_ref[...] = mn
        vo_ref[...] = vn
        d_ref[...] = (-ADAM_LR) * ((mn / c1) / (jnp.sqrt(vn / c2) + ADAM_EPS) + ADAM_WD * w_ref[...])

    blk = pl.BlockSpec((tr, C), lambda i: (i, 0))
    return pl.pallas_call(
        body, name=name, grid=(R // tr,),
        in_specs=[pl.BlockSpec((P, tr, C), lambda i: (0, i, 0)), blk, blk, blk] + [ANY] * len(after),
        out_specs=[blk, blk, blk, blk],
        out_shape=[S((R, C), F32)] * 4,
        compiler_params=_cp("parallel"),
    )(parts, w, m, v, *after)


def _adamw_layer(parts, w3, m3, v3, layer, prev, name, after=()):
    P, R, C = parts.shape
    NL = w3.shape[0]
    tr = _row_tile(R, C)
    c1 = 1.0 - ADAM_B1 ** ADAM_STEP
    c2 = 1.0 - ADAM_B2 ** ADAM_STEP
    n_prev = 0 if prev is None else len(prev)

    def body(p_ref, w_ref, m_ref, v_ref, *rest):
        g_ref, d_ref, mo_ref, vo_ref = rest[n_prev + len(after):]
        g = p_ref[0].astype(F32)
        for i in range(1, P):
            g = g + p_ref[i].astype(F32)
        mn = ADAM_B1 * m_ref[...] + (1.0 - ADAM_B1) * g
        vn = ADAM_B2 * v_ref[...] + (1.0 - ADAM_B2) * (g * g)
        g_ref[...] = g
        mo_ref[...] = mn
        vo_ref[...] = vn
        d_ref[...] = (-ADAM_LR) * ((mn / c1) / (jnp.sqrt(vn / c2) + ADAM_EPS) + ADAM_WD * w_ref[...])

    blk = pl.BlockSpec((None, tr, C), lambda i: (layer, i, 0))
    return pl.pallas_call(
        body, name=name, grid=(R // tr,),
        in_specs=[pl.BlockSpec((P, tr, C), lambda i: (0, i, 0)), blk, blk, blk] + [ANY] * (n_prev + len(after)),
        out_specs=[blk, blk, blk, blk],
        out_shape=[S((NL, R, C), F32)] * 4,
        input_output_aliases={4 + k: k for k in range(n_prev)},
        compiler_params=_cp("parallel"),
    )(parts, w3, m3, v3, *(prev or ()), *after)


VMEM_WHOLE = pl.BlockSpec(memory_space=pltpu.VMEM)


def _pack_vectors(vectors, starts, rows, name):
    def body(*refs):
        o_ref = refs[-1]
        o_ref[...] = jnp.zeros_like(o_ref)
        for v_ref, r0 in zip(refs[:-1], starts):
            for j in range(v_ref.shape[1] // 128):
                o_ref[r0 + j:r0 + j + 1, :] = v_ref[:, j * 128:(j + 1) * 128]

    return pl.pallas_call(body, name=name, in_specs=[VMEM_WHOLE] * len(vectors), out_specs=VMEM_WHOLE,
                          out_shape=S((rows, 128), F32))(*vectors)


def _adamw_vectors(g_pack, params, keep_rows, name, after=()):
    n = len(params)
    P = g_pack.shape[0]
    c1 = 1.0 - ADAM_B1 ** ADAM_STEP
    c2 = 1.0 - ADAM_B2 ** ADAM_STEP

    def body(g_ref, *refs):
        ins, outs = refs[:3 * n], refs[3 * n + len(after):]
        gs = g_ref[0]
        for i in range(1, P):
            gs = gs + g_ref[i]
        for pi, (_, _, _, slots) in enumerate(params):
            w_ref, m_ref, v_ref = ins[3 * pi:3 * pi + 3]
            g_out, d_out, m_out, v_out = outs[4 * pi:4 * pi + 4]
            for idx, row in slots:
                g = gs[row:row + 1, :]
                mn = ADAM_B1 * m_ref[idx] + (1.0 - ADAM_B1) * g
                vn = ADAM_B2 * v_ref[idx] + (1.0 - ADAM_B2) * (g * g)
                g_out[idx] = g
                m_out[idx] = mn
                v_out[idx] = vn
                d_out[idx] = (-ADAM_LR) * ((mn / c1) / (jnp.sqrt(vn / c2) + ADAM_EPS) + ADAM_WD * w_ref[idx])
        outs[-1][...] = jnp.concatenate([gs[r:r + 1, :] for r in keep_rows], axis=0)

    flat = [a for w, m, v, _ in params for a in (w, m, v)]
    out_shape = [S(w.shape, F32) for w, _, _, _ in params for _ in range(4)] + [S((len(keep_rows), 128), F32)]
    outs = pl.pallas_call(
        body, name=name,
        in_specs=[VMEM_WHOLE] * (1 + len(flat)) + [ANY] * len(after),
        out_specs=[VMEM_WHOLE] * len(out_shape), out_shape=out_shape,
    )(g_pack, *flat, *after)
    return [outs[4 * i:4 * i + 4] for i in range(n)], outs[-1]


def _adamw_nd(parts, w, m, v, name, after=()):
    shp = w.shape
    C = shp[-1]
    outs = _adamw(parts.reshape(parts.shape[0], -1, C), w.reshape(-1, C), m.reshape(-1, C), v.reshape(-1, C), name, after)
    return [o.reshape(shp) for o in outs]


TILE_ROWS = 8


REP_SMALL_ROWS = 128
REP_GRAD_STARTS = (0, 8, 16, 24, 32, 40, 48, 56, 64)
REP_SMALL_STARTS = (0, 16, 32, 40, 48, 56)
REP_LOSS_ROW = 64


def _small_pack(cw, ba, bx, lam):
    pad8 = lambda a: jnp.pad(a, ((0, TILE_ROWS - a.shape[0]), (0, 0)))
    return jnp.concatenate([pad8(cw[0, :, 0, :]), pad8(ba[0]), pad8(bx[0]), pad8(lam[0]),
                            jnp.zeros((PV_ROWS - PV_CONV_B, LRU_BW), F32)], axis=0)


def kernel(x, norm_mix_g, norm_mlp_g, rg_w_in, rg_conv_w, rg_conv_b, rg_w_a, rg_b_a, rg_w_x, rg_b_x, rg_lam, rg_w_out, at_w_qkv, at_q_g, at_k_g, at_w_o, mlp_w_up, mlp_w_down, final_g, loss_target, m_norm_mix_g, m_norm_mlp_g, m_rg_w_in, m_rg_conv_w, m_rg_conv_b, m_rg_w_a, m_rg_b_a, m_rg_w_x, m_rg_b_x, m_rg_lam, m_rg_w_out, m_at_w_qkv, m_at_q_g, m_at_k_g, m_at_w_o, m_mlp_w_up, m_mlp_w_down, m_final_g, v_norm_mix_g, v_norm_mlp_g, v_rg_w_in, v_rg_conv_w, v_rg_conv_b, v_rg_w_a, v_rg_b_a, v_rg_w_x, v_rg_b_x, v_rg_lam, v_rg_w_out, v_at_w_qkv, v_at_q_g, v_at_k_g, v_at_w_o, v_mlp_w_up, v_mlp_w_down, v_final_g):
    D = x.shape[-1]
    bf = lambda a: a.astype(BF16)

    sp_w = _small_pack(rg_conv_w, rg_b_a, rg_b_x, rg_lam)
    started, _ = _exchange_start(
        [[bf(rg_w_in[0]), sp_w], [bf(rg_w_out[0])], [bf(mlp_w_up[0]), bf(mlp_w_down[0])],
         [bf(at_w_qkv[0]).T, bf(at_w_o[0])], [bf(mlp_w_up[1]), bf(mlp_w_down[1])]],
        ["near", "near", "near", "gather", "gather"], "gather_start")
    gathers = dict(zip((0, 1, 4, 2, 3), started))
    forwards = {}

    def fetch(stage, after):
        after = tuple(after)
        if stage == 0:
            got = _exchange_wait(gathers[0], after, "near", "gather_wait0")
            g_in, g_sp = _forward_wait(_forward_start([got], "forward_start0")[0], (), "forward_wait0")
            pvec = g_sp.transpose(1, 0, 2).reshape(PV_ROWS, D)
            pvec = jnp.concatenate([pvec[:PV_CONV_B], jnp.broadcast_to(rg_conv_b, (PV_ROWS - PV_CONV_B, D))], axis=0)
            return dict(w_in=g_in, pvec=pvec)
        if stage == 1:
            near = [_exchange_wait(gathers[s], after, "near", "gather_wait%d" % s) for s in (1, 4)]
            f_out, forwards[4] = _forward_start(near, "forward_start1")
            g_out, = _forward_wait(f_out, (), "forward_wait1")
            return dict(w_out=g_out.reshape(D, D))
        if stage == 4:
            g_up0, g_dn0 = _forward_wait(forwards[4], after, "forward_wait4")
            return dict(w_up0=g_up0, w_down0=g_dn0.reshape(-1, D))
        got = _exchange_wait(gathers[stage], after, "gather", "gather_wait%d" % stage)
        if stage == 2:
            return dict(w_qkv=got[0].reshape(-1, QKV_NB, D), w_o=got[1].reshape(D, D))
        return dict(w_up1=got[0], w_down1=got[1].reshape(-1, D))

    scatters = {}

    def send(stage, g):
        if stage == 3:
            arrs = [g["w_up1"], g["w_down1"].reshape(N_DEV, -1, D)]
        elif stage == 2:
            arrs = [g["w_qkv"].reshape(N_DEV, -1, D), g["w_o"].reshape(N_DEV, -1, D)]
        elif stage == 1:
            arrs = [g["w_up0"], g["w_down0"].reshape(N_DEV, -1, D)]
        elif stage == 4:
            arrs = [g["w_out"].reshape(N_DEV, -1, D), g["pvec"].reshape(PV_ROWS, N_DEV, LRU_BW).transpose(1, 0, 2),
                    bf(g["wa"]).reshape(N_DEV, -1, 128), bf(g["wx"]).reshape(N_DEV, -1, 128)]
        elif stage == 0:
            arrs = [g["w_in"]]
        else:
            small = _pack_vectors(g["g_mix"] + g["g_mlp"] + [g["g_fin"], g["conv_b"], g["qg"], g["kg"], g["loss"]],
                                  REP_GRAD_STARTS, REP_SMALL_ROWS, "pack_rep_small")
            arrs = [small.reshape(N_DEV, -1, 128)]
        (group,), token = _exchange_start([arrs], ["scatter"], "scatter_start%d" % (stage % 6))
        scatters[stage] = (group, token)
        return (token,)

    w = dict(g_mix=norm_mix_g, g_mlp=norm_mlp_g, g_fin=final_g[None], qg=at_q_g, kg=at_k_g,
             wa=bf(rg_w_a[0]), wx=bf(rg_w_x[0]))
    grad_x = _local_step(x, loss_target, w, fetch, send)

    res = {}
    r_up1, r_dn1 = _exchange_wait(scatters[3][0], (scatters[-1][1],), "scatter", "scatter_wait3")
    r_out, r_sp, r_wa, r_wx = _exchange_wait(scatters[4][0], (r_up1,), "scatter", "scatter_wait4")
    wa_part = _sum_parts(r_wa, "reduce_w_a")
    wx_part = _sum_parts(r_wx, "reduce_w_x", (wa_part,))
    (rep_gather,), rep_token = _exchange_start([[wa_part, wx_part]], ["gather"], "rep_gather_start")
    up = _adamw_layer(r_up1, mlp_w_up, m_mlp_w_up, v_mlp_w_up, 1, None, "adam_mlp_w_up1", after=(rep_token,))
    dn = _adamw_layer(r_dn1, mlp_w_down, m_mlp_w_down, v_mlp_w_down, 1, None, "adam_mlp_w_down1", after=(up[1],))
    r_qkv, r_o = _exchange_wait(scatters[2][0], (dn[1],), "scatter", "scatter_wait2")
    tr = lambda a: a[0].T
    qkv_t = _adamw_nd(r_qkv, tr(at_w_qkv), tr(m_at_w_qkv), tr(v_at_w_qkv), "adam_at_w_qkv")
    res["at_w_qkv"] = [o.T[None] for o in qkv_t]
    res["at_w_o"] = _adamw_nd(r_o[:, None], at_w_o, m_at_w_o, v_at_w_o, "adam_at_w_o", (qkv_t[1],))
    r_up0, r_dn0 = _exchange_wait(scatters[1][0], (res["at_w_o"][1],), "scatter", "scatter_wait1")
    res["mlp_w_up"] = _adamw_layer(r_up0, mlp_w_up, m_mlp_w_up, v_mlp_w_up, 0, up, "adam_mlp_w_up0")
    res["mlp_w_down"] = _adamw_layer(r_dn0, mlp_w_down, m_mlp_w_down, v_mlp_w_down, 0, dn, "adam_mlp_w_down0",
                                     after=(res["mlp_w_up"][1],))
    r_in, = _exchange_wait(scatters[0][0], (res["mlp_w_down"][1],), "scatter", "scatter_wait0")
    res["rg_w_in"] = _adamw_nd(r_in[:, None], rg_w_in, m_rg_w_in, v_rg_w_in, "adam_rg_w_in")
    res["rg_w_out"] = _adamw_nd(r_out[:, None], rg_w_out, m_rg_w_out, v_rg_w_out, "adam_rg_w_out", (res["rg_w_in"][1],))
    whole, lane = slice(None), slice(0, 1)
    two_rows = lambda r0: [((0, slice(d, d + 1), whole), r0 + d) for d in range(2)]
    (res["rg_conv_w"], res["rg_b_a"], res["rg_b_x"], res["rg_lam"]), _ = _adamw_vectors(
        r_sp, [(rg_conv_w, m_rg_conv_w, v_rg_conv_w, [((0, t, lane, whole), PV_CONV_W + t) for t in range(CONV_W)]),
               (rg_b_a, m_rg_b_a, v_rg_b_a, two_rows(PV_B_A)), (rg_b_x, m_rg_b_x, v_rg_b_x, two_rows(PV_B_X)),
               (rg_lam, m_rg_lam, v_rg_lam, two_rows(PV_LAM))], [0], "adam_small", (res["rg_w_out"][1],))

    r_small, = _exchange_wait(scatters[-1][0], (res["rg_lam"][1],), "scatter", "scatter_wait5")
    small_sum, = _all_gather([_sum_parts(r_small, "reduce_rep_small")], "gather_replicated")
    wa_sum, wx_sum = _exchange_wait(rep_gather, (small_sum,), "gather", "rep_gather_wait")
    rows = lambda a: a.reshape(-1, 128)
    wa_res = _adamw(wa_sum.reshape(1, -1, 128), rows(rg_w_a), rows(m_rg_w_a), rows(v_rg_w_a), "adam_rg_w_a")
    wx_res = _adamw(wx_sum.reshape(1, -1, 128), rows(rg_w_x), rows(m_rg_w_x), rows(v_rg_w_x), "adam_rg_w_x", (wa_res[1],))
    res["rg_w_a"] = [o.reshape(rg_w_a.shape) for o in wa_res]
    res["rg_w_x"] = [o.reshape(rg_w_x.shape) for o in wx_res]
    def vec_slots(a, r0):
        per = a.shape[1] // 128
        return [((slice(l, l + 1), slice(128 * j, 128 * (j + 1))), r0 + l * per + j) for l in range(a.shape[0]) for j in range(per)]

    fin = [final_g[None], m_final_g[None], v_final_g[None]]
    vecs = [(norm_mix_g, m_norm_mix_g, v_norm_mix_g), (norm_mlp_g, m_norm_mlp_g, v_norm_mlp_g), fin,
            (rg_conv_b, m_rg_conv_b, v_rg_conv_b), (at_q_g, m_at_q_g, v_at_q_g), (at_k_g, m_at_k_g, v_at_k_g)]
    outs, kept = _adamw_vectors(
        small_sum.reshape(1, -1, 128),
        [(wv, mv, vv, vec_slots(wv, r0)) for (wv, mv, vv), r0 in zip(vecs, REP_SMALL_STARTS)], [REP_LOSS_ROW], "adam_rep_small", (wx_res[1],))
    for nm, o in zip(["norm_mix_g", "norm_mlp_g", "final_g", "rg_conv_b", "at_q_g", "at_k_g"], outs):
        res[nm] = [a[0] for a in o] if nm == "final_g" else o
    loss = kept[0, 0]

    order = ["norm_mix_g", "norm_mlp_g", "rg_w_in", "rg_conv_w", "rg_conv_b", "rg_w_a", "rg_b_a", "rg_w_x", "rg_b_x",
             "rg_lam", "rg_w_out", "at_w_qkv", "at_q_g", "at_k_g", "at_w_o", "mlp_w_up", "mlp_w_down", "final_g"]
    return (loss, grad_x, *[res[nm][k] for k in range(4) for nm in order])
```

```python
import functools
import math

import jax
import jax.numpy as jnp
from jax import lax
from jax.experimental import pallas as pl
from jax.experimental.pallas import tpu as pltpu

F32 = jnp.float32
BF16 = jnp.bfloat16
S = jax.ShapeDtypeStruct

EPS = 1e-6
HEAD_DIM = 128
N_KV = 2
GRID_W = 64
ROPE_THETA = 10000.0
LRU_BW = 128
RG_C = 8.0
CONV_W = 4
N_DEV = 8
N_SEG = 8
SCAN_UNROLL = 8
TN_STEP_COLS = 512
PROJ_TM = 1024
PROJ_CHAINS = 2
MLP_TM = 512
VMEM_LIMIT_V7X = 56 * 1024 * 1024
SOFTMAX_SCALE = 1.0 / math.sqrt(HEAD_DIM)
GELU_K = math.sqrt(2.0 / math.pi)
GELU_C = 0.044715

ADAM_LR = 0.001
ADAM_B1 = 0.9
ADAM_B2 = 0.999
ADAM_EPS = 1e-08
ADAM_WD = 0.01
ADAM_STEP = 10

NT = (((1,), (1,)), ((), ()))
TN = (((0,), (0,)), ((), ()))


def _cp(*sem):
    return pltpu.CompilerParams(dimension_semantics=sem, vmem_limit_bytes=VMEM_LIMIT_V7X)


def _rms_r(xv):
    return lax.rsqrt(jnp.mean(xv * xv, axis=-1, keepdims=True) + EPS)


def _rms_bwd(dh, xv, g):
    r = _rms_r(xv)
    xh = xv * r
    dg = jnp.sum(dh * xh, axis=0, keepdims=True)
    dxh = dh * g
    dx = r * (dxh - xh * jnp.mean(dxh * xh, axis=-1, keepdims=True))
    return dx, dg


def _dot(a, b):
    return jnp.dot(a, b, preferred_element_type=F32)


def _dot_nt(a, b):
    return lax.dot_general(a, b, NT, preferred_element_type=F32)


def _dot_tn(a, b):
    return lax.dot_general(a, b, TN, preferred_element_type=F32)


def _norm_matmul(x, g, wblk, name, out_dtype=F32):
    T, D = x.shape
    NB, _, nb = wblk.shape
    mm = _dot
    tm = min(T, PROJ_TM)

    def body(x_ref, g_ref, w_ref, o_ref, h_ref):
        for c in range(PROJ_CHAINS):
            rows = slice(c * tm // PROJ_CHAINS, (c + 1) * tm // PROJ_CHAINS)
            xv = x_ref[rows, :]
            hb = (xv * _rms_r(xv) * g_ref[...]).astype(BF16)
            h_ref[rows, :] = hb
            for q in range(NB):
                o_ref[rows, q * nb:(q + 1) * nb] = mm(hb, w_ref[q]).astype(o_ref.dtype)

    return pl.pallas_call(
        body, name=name, grid=(T // tm,),
        in_specs=[pl.BlockSpec((tm, D), lambda i: (i, 0)),
                  pl.BlockSpec((1, D), lambda i: (0, 0)),
                  pl.BlockSpec(wblk.shape, lambda i: (0, 0, 0))],
        out_specs=[pl.BlockSpec((tm, NB * nb), lambda i: (i, 0)),
                   pl.BlockSpec((tm, D), lambda i: (i, 0))],
        out_shape=[S((T, NB * nb), out_dtype), S((T, D), BF16)],
        compiler_params=_cp("parallel"),
    )(x, g, wblk)


def _matmul_res(a, w, res, name):
    T, K = a.shape
    N = w.shape[1]
    tm = min(T, PROJ_TM)

    def body(a_ref, w_ref, r_ref, o_ref):
        o_ref[...] = r_ref[...] + _dot(a_ref[...], w_ref[...])

    return pl.pallas_call(
        body, name=name, grid=(T // tm,),
        in_specs=[pl.BlockSpec((tm, K), lambda i: (i, 0)),
                  pl.BlockSpec((K, N), lambda i: (0, 0)),
                  pl.BlockSpec((tm, N), lambda i: (i, 0))],
        out_specs=pl.BlockSpec((tm, N), lambda i: (i, 0)),
        out_shape=S((T, N), F32),
        compiler_params=_cp("parallel"),
    )(a, w, res)


def _matmul_nt(a, w, name, out_dtype, after=()):
    T, N = a.shape
    K = w.shape[0]
    tm = min(T, PROJ_TM)

    def body(a_ref, w_ref, *rest):
        o_ref, ab_ref = rest[len(after):]
        ab = a_ref[...].astype(BF16)
        ab_ref[...] = ab
        o_ref[...] = _dot_nt(ab, w_ref[...]).astype(o_ref.dtype)

    return pl.pallas_call(
        body, name=name, grid=(T // tm,),
        in_specs=[pl.BlockSpec((tm, N), lambda i: (i, 0)),
                  pl.BlockSpec((K, N), lambda i: (0, 0))] + [pl.BlockSpec(memory_space=pl.ANY)] * len(after),
        out_specs=[pl.BlockSpec((tm, K), lambda i: (i, 0)),
                   pl.BlockSpec((tm, N), lambda i: (i, 0))],
        out_shape=[S((T, K), out_dtype), S((T, N), BF16)],
        compiler_params=_cp("parallel"),
    )(a, w, *after)


def _matmul_tn(a, b3, nb, name, blocked, after=()):
    T, M = a.shape
    SB, _, N = b3.shape
    per = N // nb
    NB = SB * per
    tk = min(T, 1024)
    nk = T // tk
    jb = max(1, TN_STEP_COLS // nb) if blocked else 1
    assert per % jb == 0
    if blocked:
        out_spec, out_shape = pl.BlockSpec((jb, M, nb), lambda j: (j, 0, 0)), S((NB, M, nb), BF16)
    else:
        assert SB == 1
        out_spec, out_shape = pl.BlockSpec((M, nb), lambda j: (0, j)), S((M, N), BF16)

    def body(a_ref, b_ref, *rest):
        o_ref = rest[-1]
        for q in range(jb):
            acc = None
            for k in range(nk):
                rows = slice(k * tk, (k + 1) * tk)
                part = _dot_tn(a_ref[rows, :], b_ref[rows, q * nb:(q + 1) * nb])
                acc = part if acc is None else acc + part
            if blocked:
                o_ref[q] = acc.astype(BF16)
            else:
                o_ref[...] = acc.astype(BF16)

    return pl.pallas_call(
        body, name=name, grid=(NB // jb,),
        in_specs=[pl.BlockSpec((T, M), lambda j: (0, 0), pipeline_mode=pl.Buffered(1)),
                  pl.BlockSpec((None, T, jb * nb), lambda j: ((j * jb) // per, 0, ((j * jb) % per) // jb))]
        + [pl.BlockSpec(memory_space=pl.ANY)] * len(after),
        out_specs=out_spec,
        out_shape=out_shape,
        compiler_params=_cp("parallel"),
    )(a, b3, *after)


def _nt_normbwd(dz3, wblk, x, g, dres, name, after=(), transposed=False):
    T, D = x.shape
    NB, nb = wblk.shape[0], wblk.shape[1 if transposed else 2]
    mm = _dot if transposed else _dot_nt
    SB, _, N = dz3.shape
    per = N // nb
    tm = min(T, PROJ_TM)

    def body(dz_ref, w_ref, x_ref, g_ref, dr_ref, *rest):
        dx_ref, dg_ref = rest[len(after):]

        @pl.when(pl.program_id(0) == 0)
        def _():
            dg_ref[...] = jnp.zeros_like(dg_ref)

        for c in range(PROJ_CHAINS):
            rows = slice(c * tm // PROJ_CHAINS, (c + 1) * tm // PROJ_CHAINS)
            dh = None
            for q in range(NB):
                cols = slice((q % per) * nb, (q % per + 1) * nb)
                part = mm(dz_ref[q // per, rows, cols], w_ref[q])
                dh = part if dh is None else dh + part
            dx, dg = _rms_bwd(dh, x_ref[rows, :], g_ref[...])
            dx_ref[rows, :] = dr_ref[rows, :] + dx
            dg_ref[...] += dg

    return pl.pallas_call(
        body, name=name, grid=(T // tm,),
        in_specs=[pl.BlockSpec((SB, tm, N), lambda i: (0, i, 0)),
                  pl.BlockSpec(wblk.shape, lambda i: (0, 0, 0)),
                  pl.BlockSpec((tm, D), lambda i: (i, 0)),
                  pl.BlockSpec((1, D), lambda i: (0, 0)),
                  pl.BlockSpec((tm, D), lambda i: (i, 0))] + [pl.BlockSpec(memory_space=pl.ANY)] * len(after),
        out_specs=[pl.BlockSpec((tm, D), lambda i: (i, 0)),
                   pl.BlockSpec((1, D), lambda i: (0, 0))],
        out_shape=[S((T, D), F32), S((1, D), F32)],
        compiler_params=_cp("arbitrary"),
    )(dz3, wblk, x, g, dres, *after)


def _loss_head(xv, tv, gv, D):
    err = xv * _rms_r(xv) * gv - tv
    e2 = jnp.sum(jnp.sum(err * err, axis=-1, keepdims=True), axis=0, keepdims=True)
    dx, dg = _rms_bwd(err * (1.0 / D), xv, gv)
    return (0.5 / D) * e2, dx, dg


def _mlp_fwd(x, g, wup, wdown, name, head=None):
    T, D = x.shape
    NB, _, fb = wup.shape
    tm = min(T, MLP_TM)
    n_head = 0 if head is None else 2

    def body(x_ref, g_ref, wu_ref, wd_ref, *rest):
        xo_ref, a_ref, h_ref = rest[n_head:n_head + 3]
        xv = x_ref[...]
        hb = (xv * _rms_r(xv) * g_ref[...]).astype(BF16)
        h_ref[...] = hb
        acc = xv
        for j in range(NB):
            a = _dot(hb, wu_ref[j])
            a_ref[:, j * fb:(j + 1) * fb] = a.astype(BF16)
            u = jnp.maximum(a, 0.0)
            acc = acc + _dot((u * u).astype(BF16), wd_ref[j * fb:(j + 1) * fb, :])

        if head is None:
            xo_ref[...] = acc
        else:
            t_ref, gf_ref = rest[:2]
            loss_ref, dgf_ref = rest[n_head + 3:n_head + 5]

            @pl.when(pl.program_id(0) == 0)
            def _():
                loss_ref[...] = jnp.zeros_like(loss_ref)
                dgf_ref[...] = jnp.zeros_like(dgf_ref)

            e2, dx, dg = _loss_head(acc, t_ref[...], gf_ref[...], D)
            xo_ref[...] = dx
            loss_ref[...] += e2
            dgf_ref[...] += dg

    row = pl.BlockSpec((tm, D), lambda i: (i, 0))
    vec = pl.BlockSpec((1, D), lambda i: (0, 0))
    once = pl.Buffered(1)
    in_specs = [row, vec, pl.BlockSpec((NB, D, fb), lambda i: (0, 0, 0), pipeline_mode=once),
                pl.BlockSpec((NB * fb, D), lambda i: (0, 0), pipeline_mode=once)]
    out_specs = [row, pl.BlockSpec((tm, NB * fb), lambda i: (i, 0)), row]
    out_shape = [S((T, D), F32), S((T, NB * fb), BF16), S((T, D), BF16)]
    if head is not None:
        in_specs += [row, vec]
        out_specs += [pl.BlockSpec((1, 128), lambda i: (0, 0)), vec]
        out_shape += [S((1, 128), F32), S((1, D), F32)]
    return pl.pallas_call(
        body, name=name, grid=(T // tm,),
        in_specs=in_specs, out_specs=out_specs, out_shape=out_shape,
        compiler_params=_cp("parallel" if head is None else "arbitrary"),
    )(x, g, wup, wdown, *(head or ()))


def _mlp_bwd_dx(x, dout, a, g, wup, wdown, name):
    T, D = x.shape
    NB, _, fb = wup.shape
    tm = min(T, MLP_TM)

    def body(x_ref, do_ref, a_ref, g_ref, wu_ref, wd_ref, dx_ref, da_ref, dob_ref, dg_ref):
        @pl.when(pl.program_id(0) == 0)
        def _():
            dg_ref[...] = jnp.zeros_like(dg_ref)

        dov = do_ref[...]
        dob = dov.astype(BF16)
        dob_ref[...] = dob
        dh = None
        for j in range(NB):
            cols = slice(j * fb, (j + 1) * fb)
            du2 = _dot_nt(dob, wd_ref[cols, :])
            u = jnp.maximum(a_ref[:, cols].astype(F32), 0.0)
            da = (du2 * (2.0 * u)).astype(BF16)
            da_ref[:, cols] = da
            part = _dot_nt(da, wu_ref[j])
            dh = part if dh is None else dh + part
        dx, dg = _rms_bwd(dh, x_ref[...], g_ref[...])
        dx_ref[...] = dov + dx
        dg_ref[...] += dg

    row = pl.BlockSpec((tm, D), lambda i: (i, 0))
    wide = pl.BlockSpec((tm, NB * fb), lambda i: (i, 0))
    vec = pl.BlockSpec((1, D), lambda i: (0, 0))
    once = pl.Buffered(1)
    return pl.pallas_call(
        body, name=name, grid=(T // tm,),
        in_specs=[row, row, wide, vec, pl.BlockSpec((NB, D, fb), lambda i: (0, 0, 0), pipeline_mode=once),
                  pl.BlockSpec((NB * fb, D), lambda i: (0, 0), pipeline_mode=once)],
        out_specs=[row, wide, row, vec],
        out_shape=[S((T, D), F32), S((T, NB * fb), BF16), S((T, D), BF16), S((1, D), F32)],
        compiler_params=_cp("arbitrary"),
    )(x, dout, a, g, wup, wdown)


def _mlp_bwd_dw(h, da, a, dob, fb, name):
    T, D = h.shape
    F = a.shape[1]
    NB = F // fb
    tk = min(T, 1024)
    nk = T // tk

    def body(h_ref, da_ref, a_ref, dob_ref, dwu_ref, dwd_ref):
        au = ad = None
        for k in range(nk):
            rows = slice(k * tk, (k + 1) * tk)
            pu = _dot_tn(h_ref[rows, :], da_ref[rows, :])
            u = jnp.maximum(a_ref[rows, :].astype(F32), 0.0)
            pd = _dot_tn((u * u).astype(BF16), dob_ref[rows, :])
            au, ad = (pu, pd) if au is None else (au + pu, ad + pd)
        dwu_ref[...] = au.astype(BF16)
        dwd_ref[...] = ad.astype(BF16)

    once = pl.Buffered(1)
    return pl.pallas_call(
        body, name=name, grid=(NB,),
        in_specs=[pl.BlockSpec((T, D), lambda j: (0, 0), pipeline_mode=once),
                  pl.BlockSpec((T, fb), lambda j: (0, j)),
                  pl.BlockSpec((T, fb), lambda j: (0, j)),
                  pl.BlockSpec((T, D), lambda j: (0, 0), pipeline_mode=once)],
        out_specs=[pl.BlockSpec((None, D, fb), lambda j: (j, 0, 0)),
                   pl.BlockSpec((fb, D), lambda j: (j, 0))],
        out_shape=[S((NB, D, fb), BF16), S((F, D), BF16)],
        compiler_params=_cp("parallel"),
    )(h, da, a, dob)


def _rope_tables(L):
    nf = HEAD_DIM // 4
    t = jnp.arange(L, dtype=jnp.int32)
    row = (t // GRID_W).astype(F32)
    col = (t % GRID_W).astype(F32)
    inv = ROPE_THETA ** (-jnp.arange(nf, dtype=F32) / nf)
    ar = row[:, None] * inv
    ac = col[:, None] * inv
    cos = jnp.concatenate([jnp.cos(ar), jnp.cos(ar), jnp.cos(ac), jnp.cos(ac)], axis=-1)
    sin = jnp.concatenate([-jnp.sin(ar), jnp.sin(ar), -jnp.sin(ac), jnp.sin(ac)], axis=-1)
    return cos, sin


def _swap32(x):
    lane = lax.broadcasted_iota(jnp.int32, x.shape, 1)
    up = pltpu.roll(x, HEAD_DIM - 32, 1)
    down = pltpu.roll(x, 32, 1)
    return jnp.where((lane % 64) < 32, up, down)


def _qkv_proj(x, g, wt_blk, qg, kg, cos, sin, L, name):
    T, D = x.shape
    NB, nb, _ = wt_blk.shape
    W = NB * nb
    nh = W // HEAD_DIM - 2 * N_KV
    tm = min(L, 512)
    lb = L // tm

    def body(x_ref, g_ref, w_ref, qg_ref, kg_ref, cos_ref, sin_ref, qkv_ref, h_ref, q_ref, k_ref, v_ref):
        xv = x_ref[...]
        hb = (xv * _rms_r(xv) * g_ref[...]).astype(BF16)
        h_ref[...] = hb
        for j in range(NB):
            qkv_ref[:, j * nb:(j + 1) * nb] = _dot_nt(hb, w_ref[j])
        c = cos_ref[...]
        s = sin_ref[...]
        for h in range(nh + N_KV):
            xh = qkv_ref[:, h * HEAD_DIM:(h + 1) * HEAD_DIM]
            gv = qg_ref[...] if h < nh else kg_ref[...]
            y = xh * _rms_r(xh) * gv
            y = (y * c + _swap32(y) * s).astype(BF16)
            if h < nh:
                q_ref[:, h * HEAD_DIM:(h + 1) * HEAD_DIM] = y
            else:
                k_ref[:, (h - nh) * HEAD_DIM:(h - nh + 1) * HEAD_DIM] = y
        v_ref[...] = qkv_ref[:, (nh + N_KV) * HEAD_DIM:].astype(BF16)

    row = lambda cols: pl.BlockSpec((tm, cols), lambda i: (i, 0))
    vec = lambda cols: pl.BlockSpec((1, cols), lambda i: (0, 0))
    table = pl.BlockSpec((tm, HEAD_DIM), lambda i: (i % lb, 0))
    return pl.pallas_call(
        body, name=name, grid=(T // tm,),
        in_specs=[row(D), vec(D), pl.BlockSpec(wt_blk.shape, lambda i: (0, 0, 0)), vec(HEAD_DIM), vec(HEAD_DIM), table, table],
        out_specs=[row(W), row(D), row(nh * HEAD_DIM), row(N_KV * HEAD_DIM), row(N_KV * HEAD_DIM)],
        out_shape=[S((T, W), F32), S((T, D), BF16), S((T, nh * HEAD_DIM), BF16), S((T, N_KV * HEAD_DIM), BF16),
                   S((T, N_KV * HEAD_DIM), BF16)],
        compiler_params=_cp("parallel"),
    )(x, g, wt_blk, qg, kg, cos, sin)


def _qk_prep_bwd(qkv, dq, dk, dv, qg, kg, cos, sin, L, name):
    T, W = qkv.shape
    nh = W // HEAD_DIM - 2 * N_KV
    tm = min(L, 512)
    lb = L // tm

    def body(qkv_ref, dq_ref, dk_ref, dv_ref, qg_ref, kg_ref, cos_ref, sin_ref, dz_ref, dqg_ref, dkg_ref):
        @pl.when(pl.program_id(0) == 0)
        def _():
            dqg_ref[...] = jnp.zeros_like(dqg_ref)
            dkg_ref[...] = jnp.zeros_like(dkg_ref)

        c = cos_ref[...]
        s = sin_ref[...]
        for h in range(nh + N_KV):
            cols = slice(h * HEAD_DIM, (h + 1) * HEAD_DIM)
            if h < nh:
                dout, gv, dg_ref = dq_ref[:, cols], qg_ref[...], dqg_ref
            else:
                kc = slice((h - nh) * HEAD_DIM, (h - nh + 1) * HEAD_DIM)
                dout, gv, dg_ref = dk_ref[:, kc], kg_ref[...], dkg_ref
            dy = dout * c - _swap32(dout) * s
            dx, dg = _rms_bwd(dy, qkv_ref[:, cols], gv)
            dg_ref[...] += dg
            dz_ref[:, cols] = dx.astype(BF16)
        dz_ref[:, (nh + N_KV) * HEAD_DIM:] = dv_ref[...].astype(BF16)

    return pl.pallas_call(
        body, name=name, grid=(T // tm,),
        in_specs=[pl.BlockSpec((tm, W), lambda i: (i, 0)),
                  pl.BlockSpec((tm, nh * HEAD_DIM), lambda i: (i, 0)),
                  pl.BlockSpec((tm, N_KV * HEAD_DIM), lambda i: (i, 0)),
                  pl.BlockSpec((tm, N_KV * HEAD_DIM), lambda i: (i, 0)),
                  pl.BlockSpec((1, HEAD_DIM), lambda i: (0, 0)),
                  pl.BlockSpec((1, HEAD_DIM), lambda i: (0, 0)),
                  pl.BlockSpec((tm, HEAD_DIM), lambda i: (i % lb, 0)),
                  pl.BlockSpec((tm, HEAD_DIM), lambda i: (i % lb, 0))],
        out_specs=[pl.BlockSpec((tm, W), lambda i: (i, 0)),
                   pl.BlockSpec((1, HEAD_DIM), lambda i: (0, 0)),
                   pl.BlockSpec((1, HEAD_DIM), lambda i: (0, 0))],
        out_shape=[S((T, W), BF16), S((1, HEAD_DIM), F32), S((1, HEAD_DIM), F32)],
        compiler_params=_cp("arbitrary"),
    )(qkv, dq, dk, dv, qg, kg, cos, sin)


EXP2_SCALE = SOFTMAX_SCALE * math.log2(math.e)
ATTN_SUB = 256
ATTN_TQ = 1024


def _softmax_rows(q, k):
    s = _dot_nt(q, k)
    e = jnp.exp2((s - jnp.max(s, axis=-1, keepdims=True)) * EXP2_SCALE)
    return e, jnp.sum(e, axis=-1, keepdims=True)


def _attn_fwd(q, k, v, L, name):
    T = q.shape[0]
    nh = q.shape[1] // HEAD_DIM
    G = nh // N_KV
    B = T // L
    tq = min(L, ATTN_TQ)
    nq = L // tq
    sub = min(tq, ATTN_SUB)

    def body(q_ref, k_ref, v_ref, o_ref):
        for h in range(tq // sub):
            rows = slice(h * sub, (h + 1) * sub)
            e, l = _softmax_rows(q_ref[rows, :], k_ref[...])
            o_ref[rows, :] = (_dot(e.astype(BF16), v_ref[...]) / l).astype(BF16)

    qspec = pl.BlockSpec((tq, HEAD_DIM), lambda b, kv, g, qi: (b * nq + qi, kv * G + g))
    kspec = pl.BlockSpec((L, HEAD_DIM), lambda b, kv, g, qi: (b, kv))
    return pl.pallas_call(
        body, name=name, grid=(B, N_KV, G, nq),
        in_specs=[qspec, kspec, kspec],
        out_specs=qspec,
        out_shape=S((T, nh * HEAD_DIM), BF16),
        compiler_params=_cp("parallel", "parallel", "parallel", "parallel"),
    )(q, k, v)


def _attn_bwd(q, k, v, do, o, L, name):
    T = q.shape[0]
    nh = q.shape[1] // HEAD_DIM
    G = nh // N_KV
    B = T // L
    tq = min(L, ATTN_TQ)
    nq = L // tq

    sub = min(tq, ATTN_SUB)

    def body(q_ref, k_ref, v_ref, do_ref, o_ref, dq_ref, dk_ref, dv_ref, ds_scr, p_scr):
        first = (pl.program_id(2) == 0) & (pl.program_id(3) == 0)
        last = (pl.program_id(2) == G - 1) & (pl.program_id(3) == nq - 1)

        @pl.when(first)
        def _():
            dk_ref[...] = jnp.zeros_like(dk_ref)
            dv_ref[...] = jnp.zeros_like(dv_ref)

        for h in range(tq // sub):
            rows = slice(h * sub, (h + 1) * sub)
            dov = do_ref[rows, :]
            e, l = _softmax_rows(q_ref[rows, :], k_ref[...])
            p = e * (1.0 / l)
            dsum = jnp.sum(dov.astype(F32) * o_ref[rows, :].astype(F32), axis=-1, keepdims=True)
            ds_scr[rows, :] = (p * (_dot_nt(dov, v_ref[...]) - dsum)).astype(BF16)
            p_scr[rows, :] = p.astype(BF16)
        ds = ds_scr[...]
        dq_ref[...] = _dot(ds, k_ref[...]) * SOFTMAX_SCALE
        dk_ref[...] += _dot_tn(ds, q_ref[...])
        dv_ref[...] += _dot_tn(p_scr[...], do_ref[...])

        @pl.when(last)
        def _():
            dk_ref[...] = dk_ref[...] * SOFTMAX_SCALE

    qspec = pl.BlockSpec((tq, HEAD_DIM), lambda b, kv, g, qi: (b * nq + qi, kv * G + g))
    kspec = pl.BlockSpec((L, HEAD_DIM), lambda b, kv, g, qi: (b, kv))
    return pl.pallas_call(
        body, name=name, grid=(B, N_KV, G, nq),
        in_specs=[qspec, kspec, kspec, qspec, qspec],
        out_specs=[qspec, kspec, kspec],
        out_shape=[S((T, nh * HEAD_DIM), F32), S((T, N_KV * HEAD_DIM), F32), S((T, N_KV * HEAD_DIM), F32)],
        scratch_shapes=[pltpu.VMEM((tq, L), BF16), pltpu.VMEM((tq, L), BF16)],
        compiler_params=_cp("parallel", "parallel", "arbitrary", "arbitrary"),
    )(q, k, v, do, o)


PV_CONV_W = 0
PV_B_A = 8
PV_B_X = 16
PV_LAM = 24
PV_CONV_B = 32
PV_ROWS = 40


def _shift_rows(x, k):
    if k == 0:
        return x
    L = x.shape[0]
    n = N_SEG * abs(k)
    seg = lax.broadcasted_iota(jnp.int32, (n, x.shape[1]), 0) % N_SEG
    if k > 0:
        edge = jnp.where(seg == 0, 0.0, pltpu.roll(x[L - n:], 1, 0))
        return jnp.concatenate([edge, x[:L - n]], axis=0)
    edge = jnp.where(seg == N_SEG - 1, 0.0, pltpu.roll(x[:n], n - 1, 0))
    return jnp.concatenate([x[n:], edge], axis=0)


def _conv_taps(rec, pv):
    c = pv[PV_CONV_B:PV_CONV_B + 1]
    for j in range(CONV_W):
        c = c + pv[PV_CONV_W + j:PV_CONV_W + j + 1] * _shift_rows(rec, 2 - j)
    return c


def _sigmoid(x):
    return 0.5 * jnp.tanh(0.5 * x) + 0.5


EXPM1_SERIES_BELOW = 0.03


def _rg_gates(c, cbf, wa, wx, ba, bx, lam):
    r = _sigmoid(_dot(cbf, wa) + ba)
    i = _sigmoid(_dot(cbf, wx) + bx)
    sp = jnp.maximum(-lam, 0.0) + jnp.log1p(jnp.exp(-jnp.abs(lam)))
    la = r * ((-RG_C) * sp)
    a = jnp.exp(la)
    a2 = a * a
    x = la + la
    series = -(x * ((x * (1.0 / 6.0) + 0.5) * x + 1.0))
    om = jnp.where(x > -EXPM1_SERIES_BELOW, series, 1.0 - a2)
    rm = lax.rsqrt(om)
    return r, i, a, om * rm, rm, a2, sp


def _gelu(x):
    t = jnp.tanh(GELU_K * (x + GELU_C * x * x * x))
    return 0.5 * x * (1.0 + t), t


def _scan_pair(af_ref, uf_ref, ab_ref, ub_ref, hf_ref, hb_ref, pf_ref, pb_ref, L):
    ls = L // N_SEG
    zero = jnp.zeros((N_SEG, LRU_BW), F32)
    one = jnp.ones((N_SEG, LRU_BW), F32)
    tile = lambda t: pl.ds(pl.multiple_of(t * N_SEG, N_SEG), N_SEG)

    def steps(tc, carry):
        hf, pf, hb, pb = carry
        for q in range(SCAN_UNROLL):
            t = tc * SCAN_UNROLL + q
            rf, rb = tile(t), tile(ls - 1 - t)
            af = af_ref[rf, :]
            hf = af * hf + uf_ref[rf, :]
            pf = pf * af
            hf_ref[rf, :] = hf
            pf_ref[rf, :] = pf
            ab = ab_ref[rb, :]
            hb = ab * hb + ub_ref[rb, :]
            pb = pb * ab
            hb_ref[rb, :] = hb
            pb_ref[rb, :] = pb
        return hf, pf, hb, pb

    hf_e, pf_e, hb_e, pb_e = lax.fori_loop(0, ls // SCAN_UNROLL, steps, (zero, one, zero, one))

    rows, cin = [], jnp.zeros((1, LRU_BW), F32)
    for s in range(N_SEG):
        rows.append(cin)
        cin = hf_e[s:s + 1] + pf_e[s:s + 1] * cin
    cf = jnp.concatenate(rows, axis=0)
    rows, cin = [], jnp.zeros((1, LRU_BW), F32)
    for s in reversed(range(N_SEG)):
        rows.append(cin)
        cin = hb_e[s:s + 1] + pb_e[s:s + 1] * cin
    cb = jnp.concatenate(rows[::-1], axis=0)

    def fix(tc, _):
        for q in range(SCAN_UNROLL):
            r = tile(tc * SCAN_UNROLL + q)
            hf_ref[r, :] = hf_ref[r, :] + pf_ref[r, :] * cf
            hb_ref[r, :] = hb_ref[r, :] + pb_ref[r, :] * cb
        return 0

    lax.fori_loop(0, ls // SCAN_UNROLL, fix, 0)


def _rg_specs(L, D, nblk):
    slab = lambda off: pl.BlockSpec((L, LRU_BW), lambda cb, b: (b, off + cb))
    wspec = pl.BlockSpec((2, None, LRU_BW, LRU_BW), lambda cb, b: (0, cb, 0, 0))
    pvspec = pl.BlockSpec((PV_ROWS, LRU_BW), lambda cb, b: (0, cb))
    return slab, wspec, pvspec


def _rg_fwd(z, pvec, wa, wx, L, name):
    T, C2 = z.shape
    C = C2 // 2
    nblk = C // LRU_BW
    B = T // L
    slab, wspec, pvspec = _rg_specs(L, C, nblk)

    def body(gp_ref, rec_ref, pv_ref, wa_ref, wx_ref, yg_ref, hf_ref, hb_ref, a_scr, u_scr, p_scr):
        pv = pv_ref[...]
        c = _conv_taps(rec_ref[...], pv)
        cbf = c.astype(BF16)
        for d in range(2):
            _, i, a, m, _, _, _ = _rg_gates(c, cbf, wa_ref[d], wx_ref[d], pv[PV_B_A + d:PV_B_A + d + 1],
                                      pv[PV_B_X + d:PV_B_X + d + 1], pv[PV_LAM + d:PV_LAM + d + 1])
            a_scr[d] = a
            u_scr[d] = m * (i * c)
        _scan_pair(a_scr.at[0], u_scr.at[0], a_scr.at[1], u_scr.at[1], hf_ref, hb_ref, p_scr.at[0], p_scr.at[1], L)
        gate, _ = _gelu(gp_ref[...])
        yg_ref[...] = ((hf_ref[...] + hb_ref[...]) * gate).astype(BF16)

    return pl.pallas_call(
        body, name=name, grid=(nblk, B),
        in_specs=[slab(0), slab(nblk), pvspec, wspec, wspec],
        out_specs=[slab(0), slab(0), slab(0)],
        out_shape=[S((T, C), BF16), S((T, C), F32), S((T, C), F32)],
        scratch_shapes=[pltpu.VMEM((2, L, LRU_BW), F32)] * 3,
        compiler_params=_cp("parallel", "parallel"),
    )(z, z, pvec, wa, wx)


def _rg_bwd(z, hf, hb, dyg, pvec, wa, wx, L, name):
    T, C2 = z.shape
    C = C2 // 2
    nblk = C // LRU_BW
    B = T // L
    slab, wspec, pvspec = _rg_specs(L, C, nblk)

    def body(gp_ref, rec_ref, hf_ref, hb_ref, dyg_ref, pv_ref, wa_ref, wx_ref,
             dz_ref, dwa_ref, dwx_ref, dpv_ref, a_scr, u_scr, d_scr, p_scr):
        @pl.when(pl.program_id(1) == 0)
        def _():
            dwa_ref[...] = jnp.zeros_like(dwa_ref)
            dwx_ref[...] = jnp.zeros_like(dwx_ref)
            dpv_ref[...] = jnp.zeros_like(dpv_ref)

        pv = pv_ref[...]
        rec = rec_ref[...]
        c = _conv_taps(rec, pv)
        cbf = c.astype(BF16)
        gp = gp_ref[...]
        gate, th = _gelu(gp)
        dgelu = 0.5 * (1.0 + th) + 0.5 * gp * (1.0 - th * th) * GELU_K * (1.0 + 3.0 * GELU_C * gp * gp)
        dyg = dyg_ref[...]
        dz_ref[0] = (dyg * (hf_ref[...] + hb_ref[...]) * dgelu).astype(BF16)
        dy = dyg * gate

        gates = []
        for d in range(2):
            gates.append(_rg_gates(c, cbf, wa_ref[d], wx_ref[d], pv[PV_B_A + d:PV_B_A + d + 1],
                                   pv[PV_B_X + d:PV_B_X + d + 1], pv[PV_LAM + d:PV_LAM + d + 1]))
        a_scr[0] = _shift_rows(gates[1][2], 1)
        a_scr[1] = _shift_rows(gates[0][2], -1)
        u_scr[...] = dy
        _scan_pair(a_scr.at[0], u_scr, a_scr.at[1], u_scr, d_scr.at[1], d_scr.at[0], p_scr.at[0], p_scr.at[1], L)

        dc = jnp.zeros_like(c)
        rows = []
        for d in range(2):
            r, i, a, m, rm, a2, sp = gates[d]
            delta = d_scr[d]
            hnb = _shift_rows(hf_ref[...], 1) if d == 0 else _shift_rows(hb_ref[...], -1)
            da = delta * hnb
            dm = delta * (i * c)
            di = delta * (m * c)
            dc = dc + delta * (m * i)
            dla = da * a - dm * (a2 * rm)
            dpa = (dla * ((-RG_C) * sp)) * (r * (1.0 - r))
            dpx = di * (i * (1.0 - i))
            dsp = (-RG_C) * jnp.sum(dla * r, axis=0, keepdims=True)
            lam = pv[PV_LAM + d:PV_LAM + d + 1]
            rows.append((jnp.sum(dpa, axis=0, keepdims=True), jnp.sum(dpx, axis=0, keepdims=True),
                         -dsp * _sigmoid(-lam)))
            dpab = dpa.astype(BF16)
            dpxb = dpx.astype(BF16)
            dwa_ref[d] += _dot_tn(cbf, dpab)
            dwx_ref[d] += _dot_tn(cbf, dpxb)
            dc = dc + _dot_nt(dpab, wa_ref[d]) + _dot_nt(dpxb, wx_ref[d])

        drec = jnp.zeros_like(c)
        dcw = []
        for j in range(CONV_W):
            drec = drec + pv[PV_CONV_W + j:PV_CONV_W + j + 1] * _shift_rows(dc, j - 2)
            dcw.append(jnp.sum(dc * _shift_rows(rec, 2 - j), axis=0, keepdims=True))
        dz_ref[1] = drec.astype(BF16)
        for j in range(CONV_W):
            dpv_ref[PV_CONV_W + j:PV_CONV_W + j + 1, :] += dcw[j]
        for d in range(2):
            dpv_ref[PV_B_A + d:PV_B_A + d + 1, :] += rows[d][0]
            dpv_ref[PV_B_X + d:PV_B_X + d + 1, :] += rows[d][1]
            dpv_ref[PV_LAM + d:PV_LAM + d + 1, :] += rows[d][2]
        dpv_ref[PV_CONV_B:PV_CONV_B + 1, :] += jnp.sum(dc, axis=0, keepdims=True)

    return pl.pallas_call(
        body, name=name, grid=(nblk, B),
        in_specs=[slab(0), slab(nblk), slab(0), slab(0), slab(0), pvspec, wspec, wspec],
        out_specs=[pl.BlockSpec((2, L, LRU_BW), lambda cb, b: (0, b, cb)), wspec, wspec, pvspec],
        out_shape=[S((2, T, C), BF16), S((2, nblk, LRU_BW, LRU_BW), F32), S((2, nblk, LRU_BW, LRU_BW), F32),
                   S((PV_ROWS, C), F32)],
        scratch_shapes=[pltpu.VMEM((2, L, LRU_BW), F32), pltpu.VMEM((L, LRU_BW), F32),
                        pltpu.VMEM((2, L, LRU_BW), F32), pltpu.VMEM((2, L, LRU_BW), F32)],
        compiler_params=_cp("parallel", "arbitrary"),
    )(z, z, hf, hb, dyg, pvec, wa, wx)


QKV_NB = 512


def _interleave(a):
    *lead, L, D = a.shape
    return a.reshape(*lead, N_SEG, L // N_SEG, D).swapaxes(-3, -2).reshape(*lead, L, D)


def _deinterleave(a):
    *lead, L, D = a.shape
    return a.reshape(*lead, L // N_SEG, N_SEG, D).swapaxes(-3, -2).reshape(*lead, L, D)


def _local_step(x3, tgt3, w, fetch, send):
    Bl, L, D = x3.shape
    T = Bl * L
    x = _interleave(x3).reshape(T, D)
    tgt = _interleave(tgt3).reshape(T, D)
    gm = [w["g_mix"][i:i + 1] for i in range(2)]
    gl = [w["g_mlp"][i:i + 1] for i in range(2)]

    w0 = fetch(0, ())
    nb_in = w0["w_in"].shape[-1]
    z, h0 = _norm_matmul(x, gm[0], w0["w_in"], "rg_in")
    yg, hf, hb = _rg_fwd(z, w0["pvec"], w["wa"], w["wx"], L, "rg_fwd")
    w1 = fetch(1, (yg,))
    x1 = _matmul_res(yg, w1["w_out"], x, "rg_out")
    w1.update(fetch(4, (x1,)))
    fb = w1["w_up0"].shape[-1]
    x2, a0, hm0 = _mlp_fwd(x1, gl[0], w1["w_up0"], w1["w_down0"], "mlp0_fwd")
    w2 = fetch(2, (x2,))
    cos, sin = [_interleave(t) for t in _rope_tables(L)]
    qkv, h1, qn, kn, vb = _qkv_proj(x2, gm[1], w2["w_qkv"], w["qg"], w["kg"], cos, sin, L, "at_qkv")
    o = _attn_fwd(qn, kn, vb, L, "at_fwd")
    x3_ = _matmul_res(o, w2["w_o"], x2, "at_out")
    w3 = fetch(3, (x3_,))
    dx4, a1, hm1, loss, dgf = _mlp_fwd(x3_, gl[1], w3["w_up1"], w3["w_down1"], "mlp1_fwd", head=(tgt, w["g_fin"]))

    dx3, da1, dob1, dgl1 = _mlp_bwd_dx(x3_, dx4, a1, gl[1], w3["w_up1"], w3["w_down1"], "mlp1_bwd_dx")
    dwu1, dwd1 = _mlp_bwd_dw(hm1, da1, a1, dob1, fb, "mlp1_bwd_dw")
    sent = send(3, dict(w_up1=dwu1, w_down1=dwd1))
    do, dx3b = _matmul_nt(dx3, w2["w_o"], "at_out_bwd", BF16, after=sent)
    dwo = _matmul_tn(o, dx3b[None], QKV_NB, "at_out_dw", blocked=False)
    dq, dk, dv = _attn_bwd(qn, kn, vb, do, o, L, "at_bwd")
    dqkv, dqg, dkg = _qk_prep_bwd(qkv, dq, dk, dv, w["qg"], w["kg"], cos, sin, L, "at_prep_bwd")
    dwqkv = _matmul_tn(dqkv, h1[None], QKV_NB, "at_qkv_dw", blocked=False)
    dx2, dgm1 = _nt_normbwd(dqkv[None], w2["w_qkv"], x2, gm[1], dx3, "at_qkv_bwd", transposed=True)
    dx1, da0, dob0, dgl0 = _mlp_bwd_dx(x1, dx2, a0, gl[0], w1["w_up0"], w1["w_down0"], "mlp0_bwd_dx")
    dwu0, dwd0 = _mlp_bwd_dw(hm0, da0, a0, dob0, fb, "mlp0_bwd_dw")
    sent = send(1, dict(w_up0=dwu0, w_down0=dwd0, w_qkv=dwqkv, w_o=dwo))
    dyg, dx1b = _matmul_nt(dx1, w1["w_out"], "rg_out_bwd", F32, after=sent)
    dwout = _matmul_tn(yg, dx1b[None], QKV_NB, "rg_out_dw", blocked=False)
    dz, dwa, dwx, dpv = _rg_bwd(z, hf, hb, dyg, w0["pvec"], w["wa"], w["wx"], L, "rg_bwd")
    sent = send(4, dict(w_out=dwout, pvec=dpv, wa=dwa, wx=dwx))
    dwin = _matmul_tn(h0, dz, nb_in, "rg_in_dw", blocked=True, after=sent)
    sent = send(0, dict(w_in=dwin))
    dx0, dgm0 = _nt_normbwd(dz, w0["w_in"], x, gm[0], dx1, "rg_in_bwd", after=sent)
    send(-1, dict(g_mix=[dgm0, dgm1], g_mlp=[dgl0, dgl1], g_fin=dgf, conv_b=dpv[PV_CONV_B:PV_CONV_B + 1], qg=dqg, kg=dkg, loss=loss))
    return _deinterleave(dx0.reshape(Bl, L, D))


MESH = pl.DeviceIdType.MESH
ANY = pl.BlockSpec(memory_space=pl.ANY)
N_PEERS = N_DEV - 1


def _my_place():
    return lax.axis_index("x"), lax.axis_index("y"), lax.axis_index("c")


def _flat(px, py, pc):
    return 4 * px + 2 * py + pc


def _all_gather(shards, name):
    n = len(shards)

    def body(*refs):
        ins, outs = refs[:n], refs[n:2 * n]
        send_sems, recv_sems, local_sems = refs[2 * n:]
        x, y, c = _my_place()
        me, sibling = (x, y, c), (x, y, 1 - c)
        chips = [(1 - x, y), (x, 1 - y), (1 - x, 1 - y)]

        def copy(a, k, block, to, src=None):
            dst = outs[a].at[_flat(*block)]
            return pltpu.make_async_remote_copy(
                src_ref=dst if src is None else src, dst_ref=dst,
                send_sem=send_sems.at[a, k], recv_sem=recv_sems.at[a, k],
                device_id=to, device_id_type=MESH)

        mine = [pltpu.make_async_copy(ins[a], outs[a].at[_flat(*me)], local_sems.at[a]) for a in range(n)]
        for cp in mine:
            cp.start()
        first = []
        for a in range(n):
            first.append(copy(a, 0, me, sibling, src=ins[a]))
            first += [copy(a, 1 + j, me, (*chip, c), src=ins[a]) for j, chip in enumerate(chips)]
        for cp in first:
            cp.start()
        passed = []
        for j, chip in enumerate(chips):
            for a in range(n):
                copy(a, 1 + j, (*chip, c), me).wait_recv()
                fwd = copy(a, 4 + j, (*chip, c), sibling)
                fwd.start()
                passed.append(fwd)
        for a in range(n):
            copy(a, 0, sibling, me).wait_recv()
            for j, chip in enumerate(chips):
                copy(a, 4 + j, (*chip, 1 - c), me).wait_recv()
        for cp in first + passed:
            cp.wait_send()
        for cp in mine:
            cp.wait()

    return pl.pallas_call(
        body, name=name,
        in_specs=[ANY] * n, out_specs=[ANY] * n,
        out_shape=[S((N_DEV,) + s.shape, s.dtype) for s in shards],
        scratch_shapes=[pltpu.SemaphoreType.DMA((n, N_PEERS)), pltpu.SemaphoreType.DMA((n, N_PEERS)),
                        pltpu.SemaphoreType.DMA((n,))],
    )(*shards)


HBM = pl.BlockSpec(memory_space=pltpu.HBM)
SEM = pl.BlockSpec(memory_space=pltpu.SEMAPHORE)
SIDE_EFFECT = pltpu.SideEffectType.DATAFLOW_SIDE_EFFECTING
SEMS_PER_GROUP = 3


NEAR_PEERS = (1, 2, 4, 6)
FAR_CHIPS = (2, 4, 6)


def _exchange_copies(srcs, lands, sems, mode):
    send_sems, recv_sems, local_sems = sems
    scatter = mode == "scatter"
    x, y, c = _my_place()
    me = _flat(x, y, c)
    remote, local = [], []
    for a in range(len(srcs)):
        for r in (NEAR_PEERS if mode == "near" else range(1, N_DEV)):
            peer = (1 - x if r & 4 else x, 1 - y if r & 2 else y, 1 - c if r & 1 else c)
            remote.append(pltpu.make_async_remote_copy(
                src_ref=srcs[a].at[_flat(*peer)] if scatter else srcs[a], dst_ref=lands[a].at[me],
                send_sem=send_sems.at[a * N_PEERS + r - 1], recv_sem=recv_sems.at[a * N_PEERS + r - 1],
                device_id=peer, device_id_type=MESH))
        local.append(pltpu.make_async_copy(srcs[a].at[me] if scatter else srcs[a], lands[a].at[me], local_sems.at[a]))
    return remote, local


def _exchange_start(groups, modes, name):
    sizes = [len(g) for g in groups]
    srcs = [pltpu.with_memory_space_constraint(a, pltpu.HBM) for g in groups for a in g]
    n = len(srcs)
    scatter_of = [m == "scatter" for g, m in zip(groups, modes) for _ in g]
    lands = [pltpu.with_memory_space_constraint(lax.empty(a.shape if sc else (N_DEV,) + a.shape, a.dtype), pltpu.HBM)
             for a, sc in zip(srcs, scatter_of)]
    n_sem = SEMS_PER_GROUP * len(groups)

    def body(*refs):
        src_refs, land_refs, sem_refs, token = refs[:n], refs[n:2 * n], refs[2 * n:2 * n + n_sem], refs[-1]
        off = 0
        for gi, k in enumerate(sizes):
            remote, local = _exchange_copies(src_refs[off:off + k], land_refs[off:off + k],
                                             sem_refs[SEMS_PER_GROUP * gi:SEMS_PER_GROUP * (gi + 1)], modes[gi])
            for cp in local + remote:
                cp.start()
            off += k
        token[...] = jnp.zeros_like(token)

    sem_shapes = []
    for k in sizes:
        sem_shapes += [pltpu.SemaphoreType.DMA((k * N_PEERS,)), pltpu.SemaphoreType.DMA((k * N_PEERS,)),
                       pltpu.SemaphoreType.DMA((k,))]
    outs = pl.pallas_call(
        body, name=name,
        out_shape=sem_shapes + [pltpu.HBM(a.shape, a.dtype) for a in srcs + lands] + [S((8, 128), F32)],
        in_specs=[HBM] * (2 * n),
        out_specs=[SEM] * n_sem + [HBM] * (2 * n) + [pl.BlockSpec(memory_space=pltpu.VMEM)],
        input_output_aliases={i: n_sem + i for i in range(2 * n)},
        compiler_params=pltpu.CompilerParams(has_side_effects=SIDE_EFFECT),
    )(*srcs, *lands)
    sems, thru, token = outs[:n_sem], outs[n_sem:n_sem + 2 * n], outs[-1]
    per_group, off = [], 0
    for gi, k in enumerate(sizes):
        per_group.append((sems[SEMS_PER_GROUP * gi:SEMS_PER_GROUP * (gi + 1)], thru[off:off + k], thru[n + off:n + off + k]))
        off += k
    return per_group, token


def _exchange_wait(group, after, mode, name):
    sems, srcs, lands = group
    k = len(srcs)

    def body(*refs):
        remote, local = _exchange_copies(refs[:k], refs[k:2 * k], refs[2 * k:2 * k + SEMS_PER_GROUP], mode)
        for cp in remote:
            cp.wait_send()
            cp.wait_recv()
        for cp in local:
            cp.wait()

    outs = pl.pallas_call(
        body, name=name,
        out_shape=[pltpu.HBM(a.shape, a.dtype) for a in list(srcs) + list(lands)],
        in_specs=[HBM] * (2 * k) + [SEM] * SEMS_PER_GROUP + [ANY] * len(after),
        out_specs=[HBM] * (2 * k),
        input_output_aliases={i: i for i in range(2 * k)},
        compiler_params=pltpu.CompilerParams(has_side_effects=SIDE_EFFECT),
    )(*srcs, *lands, *sems, *after)
    return outs[k:]


def _forward_copies(lands, sems):
    send_sems, recv_sems = sems
    x, y, c = _my_place()
    mine, theirs = [], []
    for a in range(len(lands)):
        for k, r in enumerate(FAR_CHIPS):
            px, py = (1 - x if r & 4 else x), (1 - y if r & 2 else y)
            for out, core in ((mine, c), (theirs, 1 - c)):
                blk = lands[a].at[_flat(px, py, core)]
                out.append(pltpu.make_async_remote_copy(
                    src_ref=blk, dst_ref=blk, send_sem=send_sems.at[a * len(FAR_CHIPS) + k],
                    recv_sem=recv_sems.at[a * len(FAR_CHIPS) + k], device_id=(x, y, 1 - c), device_id_type=MESH))
    return mine, theirs


def _forward_start(groups, name):
    sizes = [len(g) for g in groups]
    lands = [a for g in groups for a in g]
    n = len(lands)
    n_sem = 2 * len(groups)

    def body(*refs):
        land_refs, sem_refs, token = refs[:n], refs[n:n + n_sem], refs[-1]
        off = 0
        for gi, k in enumerate(sizes):
            mine, _ = _forward_copies(land_refs[off:off + k], sem_refs[2 * gi:2 * gi + 2])
            for cp in mine:
                cp.start()
            off += k
        token[...] = jnp.zeros_like(token)

    sem_shapes = []
    for k in sizes:
        sem_shapes += [pltpu.SemaphoreType.DMA((k * len(FAR_CHIPS),))] * 2
    outs = pl.pallas_call(
        body, name=name,
        out_shape=sem_shapes + [pltpu.HBM(a.shape, a.dtype) for a in lands] + [S((8, 128), F32)],
        in_specs=[HBM] * n,
        out_specs=[SEM] * n_sem + [HBM] * n + [pl.BlockSpec(memory_space=pltpu.VMEM)],
        input_output_aliases={i: n_sem + i for i in range(n)},
        compiler_params=pltpu.CompilerParams(has_side_effects=SIDE_EFFECT),
    )(*lands)
    per_group, off = [], 0
    for gi, k in enumerate(sizes):
        per_group.append((outs[2 * gi:2 * gi + 2], outs[n_sem + off:n_sem + off + k]))
        off += k
    return per_group


def _forward_wait(group, after, name):
    sems, lands = group
    k = len(lands)

    def body(*refs):
        mine, theirs = _forward_copies(refs[:k], refs[k:k + 2])
        for cp in mine:
            cp.wait_send()
        for cp in theirs:
            cp.wait_recv()

    return pl.pallas_call(
        body, name=name,
        out_shape=[pltpu.HBM(a.shape, a.dtype) for a in lands],
        in_specs=[HBM] * k + [SEM] * 2 + [ANY] * len(after),
        out_specs=[HBM] * k,
        input_output_aliases={i: i for i in range(k)},
        compiler_params=pltpu.CompilerParams(has_side_effects=SIDE_EFFECT),
    )(*lands, *sems, *after)


def _row_tile(rows, cols):
    want = max(16, (128 * 1024) // cols)
    if rows <= want:
        return rows
    t = want - want % 16
    while rows % t:
        t -= 16
    return t


def _sum_parts(parts, name, after=()):
    P, R, C = parts.shape
    tr = _row_tile(R, C)

    def body(p_ref, *rest):
        o_ref = rest[-1]
        g = p_ref[0].astype(F32)
        for i in range(1, P):
            g = g + p_ref[i].astype(F32)
        o_ref[...] = g

    return pl.pallas_call(
        body, name=name, grid=(R // tr,),
        in_specs=[pl.BlockSpec((P, tr, C), lambda i: (0, i, 0))] + [ANY] * len(after),
        out_specs=pl.BlockSpec((tr, C), lambda i: (i, 0)),
        out_shape=S((R, C), F32),
        compiler_params=_cp("parallel"),
    )(parts, *after)


def _adamw(parts, w, m, v, name, after=()):
    P, R, C = parts.shape
    tr = _row_tile(R, C)
    c1 = 1.0 - ADAM_B1 ** ADAM_STEP
    c2 = 1.0 - ADAM_B2 ** ADAM_STEP

    def body(p_ref, w_ref, m_ref, v_ref, *rest):
        g_ref, d_ref, mo_ref, vo_ref = rest[len(after):]
        g = p_ref[0].astype(F32)
        for i in range(1, P):
            g = g + p_ref[i].astype(F32)
        mn = ADAM_B1 * m_ref[...] + (1.0 - ADAM_B1) * g
        vn = ADAM_B2 * v_ref[...] + (1.0 - ADAM_B2) * (g * g)
        g_ref[...] = g
        mo_ref[...] = mn
        vo_ref[...] = vn
        d_ref[...] = (-ADAM_LR) * ((mn / c1) / (jnp.sqrt(vn / c2) + ADAM_EPS) + ADAM_WD * w_ref[...])

    blk = pl.BlockSpec((tr, C), lambda i: (i, 0))
    return pl.pallas_call(
        body, name=name, grid=(R // tr,),
        in_specs=[pl.BlockSpec((P, tr, C), lambda i: (0, i, 0)), blk, blk, blk] + [ANY] * len(after),
        out_specs=[blk, blk, blk, blk],
        out_shape=[S((R, C), F32)] * 4,
        compiler_params=_cp("parallel"),
    )(parts, w, m, v, *after)


def _adamw_layer(parts, w3, m3, v3, layer, prev, name, after=()):
    P, R, C = parts.shape
    NL = w3.shape[0]
    tr = _row_tile(R, C)
    c1 = 1.0 - ADAM_B1 ** ADAM_STEP
    c2 = 1.0 - ADAM_B2 ** ADAM_STEP
    n_prev = 0 if prev is None else len(prev)

    def body(p_ref, w_ref, m_ref, v_ref, *rest):
        g_ref, d_ref, mo_ref, vo_ref = rest[n_prev + len(after):]
        g = p_ref[0].astype(F32)
        for i in range(1, P):
            g = g + p_ref[i].astype(F32)
        mn = ADAM_B1 * m_ref[...] + (1.0 - ADAM_B1) * g
        vn = ADAM_B2 * v_ref[...] + (1.0 - ADAM_B2) * (g * g)
        g_ref[...] = g
        mo_ref[...] = mn
        vo_ref[...] = vn
        d_ref[...] = (-ADAM_LR) * ((mn / c1) / (jnp.sqrt(vn / c2) + ADAM_EPS) + ADAM_WD * w_ref[...])

    blk = pl.BlockSpec((None, tr, C), lambda i: (layer, i, 0))
    return pl.pallas_call(
        body, name=name, grid=(R // tr,),
        in_specs=[pl.BlockSpec((P, tr, C), lambda i: (0, i, 0)), blk, blk, blk] + [ANY] * (n_prev + len(after)),
        out_specs=[blk, blk, blk, blk],
        out_shape=[S((NL, R, C), F32)] * 4,
        input_output_aliases={4 + k: k for k in range(n_prev)},
        compiler_params=_cp("parallel"),
    )(parts, w3, m3, v3, *(prev or ()), *after)


VMEM_WHOLE = pl.BlockSpec(memory_space=pltpu.VMEM)


def _pack_vectors(vectors, starts, rows, name):
    def body(*refs):
        o_ref = refs[-1]
        o_ref[...] = jnp.zeros_like(o_ref)
        for v_ref, r0 in zip(refs[:-1], starts):
            for j in range(v_ref.shape[1] // 128):
                o_ref[r0 + j:r0 + j + 1, :] = v_ref[:, j * 128:(j + 1) * 128]

    return pl.pallas_call(body, name=name, in_specs=[VMEM_WHOLE] * len(vectors), out_specs=VMEM_WHOLE,
                          out_shape=S((rows, 128), F32))(*vectors)


def _adamw_vectors(g_pack, params, keep_rows, name, after=()):
    n = len(params)
    P = g_pack.shape[0]
    c1 = 1.0 - ADAM_B1 ** ADAM_STEP
    c2 = 1.0 - ADAM_B2 ** ADAM_STEP

    def body(g_ref, *refs):
        ins, outs = refs[:3 * n], refs[3 * n + len(after):]
        gs = g_ref[0]
        for i in range(1, P):
            gs = gs + g_ref[i]
        for pi, (_, _, _, slots) in enumerate(params):
            w_ref, m_ref, v_ref = ins[3 * pi:3 * pi + 3]
            g_out, d_out, m_out, v_out = outs[4 * pi:4 * pi + 4]
            for idx, row in slots:
                g = gs[row:row + 1, :]
                mn = ADAM_B1 * m_ref[idx] + (1.0 - ADAM_B1) * g
                vn = ADAM_B2 * v_ref[idx] + (1.0 - ADAM_B2) * (g * g)
                g_out[idx] = g
                m_out[idx] = mn
                v_out[idx] = vn
                d_out[idx] = (-ADAM_LR) * ((mn / c1) / (jnp.sqrt(vn / c2) + ADAM_EPS) + ADAM_WD * w_ref[idx])
        outs[-1][...] = jnp.concatenate([gs[r:r + 1, :] for r in keep_rows], axis=0)

    flat = [a for w, m, v, _ in params for a in (w, m, v)]
    out_shape = [S(w.shape, F32) for w, _, _, _ in params for _ in range(4)] + [S((len(keep_rows), 128), F32)]
    outs = pl.pallas_call(
        body, name=name,
        in_specs=[VMEM_WHOLE] * (1 + len(flat)) + [ANY] * len(after),
        out_specs=[VMEM_WHOLE] * len(out_shape), out_shape=out_shape,
    )(g_pack, *flat, *after)
    return [outs[4 * i:4 * i + 4] for i in range(n)], outs[-1]


def _adamw_nd(parts, w, m, v, name, after=()):
    shp = w.shape
    C = shp[-1]
    outs = _adamw(parts.reshape(parts.shape[0], -1, C), w.reshape(-1, C), m.reshape(-1, C), v.reshape(-1, C), name, after)
    return [o.reshape(shp) for o in outs]


TILE_ROWS = 8


REP_SMALL_ROWS = 128
REP_GRAD_STARTS = (0, 8, 16, 24, 32, 40, 48, 56, 64)
REP_SMALL_STARTS = (0, 16, 32, 40, 48, 56)
REP_LOSS_ROW = 64


def _small_pack(cw, ba, bx, lam):
    pad8 = lambda a: jnp.pad(a, ((0, TILE_ROWS - a.shape[0]), (0, 0)))
    return jnp.concatenate([pad8(cw[0, :, 0, :]), pad8(ba[0]), pad8(bx[0]), pad8(lam[0]),
                            jnp.zeros((PV_ROWS - PV_CONV_B, LRU_BW), F32)], axis=0)


def kernel(x, norm_mix_g, norm_mlp_g, rg_w_in, rg_conv_w, rg_conv_b, rg_w_a, rg_b_a, rg_w_x, rg_b_x, rg_lam, rg_w_out, at_w_qkv, at_q_g, at_k_g, at_w_o, mlp_w_up, mlp_w_down, final_g, loss_target, m_norm_mix_g, m_norm_mlp_g, m_rg_w_in, m_rg_conv_w, m_rg_conv_b, m_rg_w_a, m_rg_b_a, m_rg_w_x, m_rg_b_x, m_rg_lam, m_rg_w_out, m_at_w_qkv, m_at_q_g, m_at_k_g, m_at_w_o, m_mlp_w_up, m_mlp_w_down, m_final_g, v_norm_mix_g, v_norm_mlp_g, v_rg_w_in, v_rg_conv_w, v_rg_conv_b, v_rg_w_a, v_rg_b_a, v_rg_w_x, v_rg_b_x, v_rg_lam, v_rg_w_out, v_at_w_qkv, v_at_q_g, v_at_k_g, v_at_w_o, v_mlp_w_up, v_mlp_w_down, v_final_g):
    D = x.shape[-1]
    bf = lambda a: a.astype(BF16)

    sp_w = _small_pack(rg_conv_w, rg_b_a, rg_b_x, rg_lam)
    started, _ = _exchange_start(
        [[bf(rg_w_in[0]), sp_w], [bf(rg_w_out[0])], [bf(mlp_w_up[0]), bf(mlp_w_down[0])],
         [bf(at_w_qkv[0]).T, bf(at_w_o[0])], [bf(mlp_w_up[1]), bf(mlp_w_down[1])]],
        ["near", "near", "near", "gather", "gather"], "gather_start")
    gathers = dict(zip((0, 1, 4, 2, 3), started))
    forwards = {}

    def fetch(stage, after):
        after = tuple(after)
        if stage == 0:
            got = _exchange_wait(gathers[0], after, "near", "gather_wait0")
            g_in, g_sp = _forward_wait(_forward_start([got], "forward_start0")[0], (), "forward_wait0")
            pvec = g_sp.transpose(1, 0, 2).reshape(PV_ROWS, D)
            pvec = jnp.concatenate([pvec[:PV_CONV_B], jnp.broadcast_to(rg_conv_b, (PV_ROWS - PV_CONV_B, D))], axis=0)
            return dict(w_in=g_in, pvec=pvec)
        if stage == 1:
            near = [_exchange_wait(gathers[s], after, "near", "gather_wait%d" % s) for s in (1, 4)]
            f_out, forwards[4] = _forward_start(near, "forward_start1")
            g_out, = _forward_wait(f_out, (), "forward_wait1")
            return dict(w_out=g_out.reshape(D, D))
        if stage == 4:
            g_up0, g_dn0 = _forward_wait(forwards[4], after, "forward_wait4")
            return dict(w_up0=g_up0, w_down0=g_dn0.reshape(-1, D))
        got = _exchange_wait(gathers[stage], after, "gather", "gather_wait%d" % stage)
        if stage == 2:
            return dict(w_qkv=got[0].reshape(-1, QKV_NB, D), w_o=got[1].reshape(D, D))
        return dict(w_up1=got[0], w_down1=got[1].reshape(-1, D))

    scatters = {}

    def send(stage, g):
        if stage == 3:
            arrs = [g["w_up1"], g["w_down1"].reshape(N_DEV, -1, D)]
        elif stage == 1:
            arrs = [g["w_up0"], g["w_down0"].reshape(N_DEV, -1, D),
                    g["w_qkv"].reshape(N_DEV, -1, D), g["w_o"].reshape(N_DEV, -1, D)]
        elif stage == 4:
            arrs = [g["w_out"].reshape(N_DEV, -1, D), g["pvec"].reshape(PV_ROWS, N_DEV, LRU_BW).transpose(1, 0, 2),
                    bf(g["wa"]).reshape(N_DEV, -1, 128), bf(g["wx"]).reshape(N_DEV, -1, 128)]
        elif stage == 0:
            arrs = [g["w_in"]]
        else:
            small = _pack_vectors(g["g_mix"] + g["g_mlp"] + [g["g_fin"], g["conv_b"], g["qg"], g["kg"], g["loss"]],
                                  REP_GRAD_STARTS, REP_SMALL_ROWS, "pack_rep_small")
            arrs = [small.reshape(N_DEV, -1, 128)]
        (group,), token = _exchange_start([arrs], ["scatter"], "scatter_start%d" % (stage % 6))
        scatters[stage] = (group, token)
        return (token,)

    w = dict(g_mix=norm_mix_g, g_mlp=norm_mlp_g, g_fin=final_g[None], qg=at_q_g, kg=at_k_g,
             wa=bf(rg_w_a[0]), wx=bf(rg_w_x[0]))
    grad_x = _local_step(x, loss_target, w, fetch, send)

    res = {}
    r_up1, r_dn1 = _exchange_wait(scatters[3][0], (scatters[-1][1],), "scatter", "scatter_wait3")
    r_out, r_sp, r_wa, r_wx = _exchange_wait(scatters[4][0], (r_up1,), "scatter", "scatter_wait4")
    wa_part = _sum_parts(r_wa, "reduce_w_a")
    wx_part = _sum_parts(r_wx, "reduce_w_x", (wa_part,))
    (rep_gather,), rep_token = _exchange_start([[wa_part, wx_part]], ["gather"], "rep_gather_start")
    up = _adamw_layer(r_up1, mlp_w_up, m_mlp_w_up, v_mlp_w_up, 1, None, "adam_mlp_w_up1", after=(rep_token,))
    dn = _adamw_layer(r_dn1, mlp_w_down, m_mlp_w_down, v_mlp_w_down, 1, None, "adam_mlp_w_down1", after=(up[1],))
    r_up0, r_dn0, r_qkv, r_o = _exchange_wait(scatters[1][0], (dn[1],), "scatter", "scatter_wait1")
    tr = lambda a: a[0].T
    qkv_t = _adamw_nd(r_qkv, tr(at_w_qkv), tr(m_at_w_qkv), tr(v_at_w_qkv), "adam_at_w_qkv")
    res["at_w_qkv"] = [o.T[None] for o in qkv_t]
    res["at_w_o"] = _adamw_nd(r_o[:, None], at_w_o, m_at_w_o, v_at_w_o, "adam_at_w_o", (qkv_t[1],))
    res["mlp_w_up"] = _adamw_layer(r_up0, mlp_w_up, m_mlp_w_up, v_mlp_w_up, 0, up, "adam_mlp_w_up0", after=(res["at_w_o"][1],))
    res["mlp_w_down"] = _adamw_layer(r_dn0, mlp_w_down, m_mlp_w_down, v_mlp_w_down, 0, dn, "adam_mlp_w_down0",
                                     after=(res["mlp_w_up"][1],))
    r_in, = _exchange_wait(scatters[0][0], (res["mlp_w_down"][1],), "scatter", "scatter_wait0")
    res["rg_w_in"] = _adamw_nd(r_in[:, None], rg_w_in, m_rg_w_in, v_rg_w_in, "adam_rg_w_in")
    res["rg_w_out"] = _adamw_nd(r_out[:, None], rg_w_out, m_rg_w_out, v_rg_w_out, "adam_rg_w_out", (res["rg_w_in"][1],))
    whole, lane = slice(None), slice(0, 1)
    two_rows = lambda r0: [((0, slice(d, d + 1), whole), r0 + d) for d in range(2)]
    (res["rg_conv_w"], res["rg_b_a"], res["rg_b_x"], res["rg_lam"]), _ = _adamw_vectors(
        r_sp, [(rg_conv_w, m_rg_conv_w, v_rg_conv_w, [((0, t, lane, whole), PV_CONV_W + t) for t in range(CONV_W)]),
               (rg_b_a, m_rg_b_a, v_rg_b_a, two_rows(PV_B_A)), (rg_b_x, m_rg_b_x, v_rg_b_x, two_rows(PV_B_X)),
               (rg_lam, m_rg_lam, v_rg_lam, two_rows(PV_LAM))], [0], "adam_small", (res["rg_w_out"][1],))

    r_small, = _exchange_wait(scatters[-1][0], (res["rg_lam"][1],), "scatter", "scatter_wait5")
    small_sum, = _all_gather([_sum_parts(r_small, "reduce_rep_small")], "gather_replicated")
    wa_sum, wx_sum = _exchange_wait(rep_gather, (small_sum,), "gather", "rep_gather_wait")
    rows = lambda a: a.reshape(-1, 128)
    wa_res = _adamw(wa_sum.reshape(1, -1, 128), rows(rg_w_a), rows(m_rg_w_a), rows(v_rg_w_a), "adam_rg_w_a")
    wx_res = _adamw(wx_sum.reshape(1, -1, 128), rows(rg_w_x), rows(m_rg_w_x), rows(v_rg_w_x), "adam_rg_w_x", (wa_res[1],))
    res["rg_w_a"] = [o.reshape(rg_w_a.shape) for o in wa_res]
    res["rg_w_x"] = [o.reshape(rg_w_x.shape) for o in wx_res]
    def vec_slots(a, r0):
        per = a.shape[1] // 128
        return [((slice(l, l + 1), slice(128 * j, 128 * (j + 1))), r0 + l * per + j) for l in range(a.shape[0]) for j in range(per)]

    fin = [final_g[None], m_final_g[None], v_final_g[None]]
    vecs = [(norm_mix_g, m_norm_mix_g, v_norm_mix_g), (norm_mlp_g, m_norm_mlp_g, v_norm_mlp_g), fin,
            (rg_conv_b, m_rg_conv_b, v_rg_conv_b), (at_q_g, m_at_q_g, v_at_q_g), (at_k_g, m_at_k_g, v_at_k_g)]
    outs, kept = _adamw_vectors(
        small_sum.reshape(1, -1, 128),
        [(wv, mv, vv, vec_slots(wv, r0)) for (wv, mv, vv), r0 in zip(vecs, REP_SMALL_STARTS)], [REP_LOSS_ROW], "adam_rep_small", (wx_res[1],))
    for nm, o in zip(["norm_mix_g", "norm_mlp_g", "final_g", "rg_conv_b", "at_q_g", "at_k_g"], outs):
        res[nm] = [a[0] for a in o] if nm == "final_g" else o
    loss = kept[0, 0]

    order = ["norm_mix_g", "norm_mlp_g", "rg_w_in", "rg_conv_w", "rg_conv_b", "rg_w_a", "rg_b_a", "rg_w_x", "rg_b_x",
             "rg_lam", "rg_w_out", "at_w_qkv", "at_q_g", "at_k_g", "at_w_o", "mlp_w_up", "mlp_w_down", "final_g"]
    return (loss, grad_x, *[res[nm][k] for k in range(4) for nm in order])
```

```python
import functools
import math

import jax
import jax.numpy as jnp
from jax import lax
from jax.experimental import pallas as pl
from jax.experimental.pallas import tpu as pltpu

F32 = jnp.float32
BF16 = jnp.bfloat16
S = jax.ShapeDtypeStruct

EPS = 1e-6
HEAD_DIM = 128
N_KV = 2
GRID_W = 64
ROPE_THETA = 10000.0
LRU_BW = 128
RG_C = 8.0
CONV_W = 4
N_DEV = 8
N_SEG = 8
SCAN_UNROLL = 8
TN_STEP_COLS = 512
PROJ_TM = 1024
PROJ_CHAINS = 2
MLP_TM = 512
VMEM_LIMIT_V7X = 56 * 1024 * 1024
SOFTMAX_SCALE = 1.0 / math.sqrt(HEAD_DIM)
GELU_K = math.sqrt(2.0 / math.pi)
GELU_C = 0.044715

ADAM_LR = 0.001
ADAM_B1 = 0.9
ADAM_B2 = 0.999
ADAM_EPS = 1e-08
ADAM_WD = 0.01
ADAM_STEP = 10

NT = (((1,), (1,)), ((), ()))
TN = (((0,), (0,)), ((), ()))


def _cp(*sem):
    return pltpu.CompilerParams(dimension_semantics=sem, vmem_limit_bytes=VMEM_LIMIT_V7X)


def _rms_r(xv):
    return lax.rsqrt(jnp.mean(xv * xv, axis=-1, keepdims=True) + EPS)


def _rms_bwd(dh, xv, g):
    r = _rms_r(xv)
    xh = xv * r
    dg = jnp.sum(dh * xh, axis=0, keepdims=True)
    dxh = dh * g
    dx = r * (dxh - xh * jnp.mean(dxh * xh, axis=-1, keepdims=True))
    return dx, dg


def _dot(a, b):
    return jnp.dot(a, b, preferred_element_type=F32)


def _dot_nt(a, b):
    return lax.dot_general(a, b, NT, preferred_element_type=F32)


def _dot_tn(a, b):
    return lax.dot_general(a, b, TN, preferred_element_type=F32)


def _norm_matmul(x, g, wblk, name, out_dtype=F32):
    T, D = x.shape
    NB, _, nb = wblk.shape
    mm = _dot
    tm = min(T, PROJ_TM)

    def body(x_ref, g_ref, w_ref, o_ref, h_ref):
        for c in range(PROJ_CHAINS):
            rows = slice(c * tm // PROJ_CHAINS, (c + 1) * tm // PROJ_CHAINS)
            xv = x_ref[rows, :]
            hb = (xv * _rms_r(xv) * g_ref[...]).astype(BF16)
            h_ref[rows, :] = hb
            for q in range(NB):
                o_ref[rows, q * nb:(q + 1) * nb] = mm(hb, w_ref[q]).astype(o_ref.dtype)

    return pl.pallas_call(
        body, name=name, grid=(T // tm,),
        in_specs=[pl.BlockSpec((tm, D), lambda i: (i, 0)),
                  pl.BlockSpec((1, D), lambda i: (0, 0)),
                  pl.BlockSpec(wblk.shape, lambda i: (0, 0, 0))],
        out_specs=[pl.BlockSpec((tm, NB * nb), lambda i: (i, 0)),
                   pl.BlockSpec((tm, D), lambda i: (i, 0))],
        out_shape=[S((T, NB * nb), out_dtype), S((T, D), BF16)],
        compiler_params=_cp("parallel"),
    )(x, g, wblk)


def _matmul_res(a, w, res, name):
    T, K = a.shape
    N = w.shape[1]
    tm = min(T, PROJ_TM)

    def body(a_ref, w_ref, r_ref, o_ref):
        o_ref[...] = r_ref[...] + _dot(a_ref[...], w_ref[...])

    return pl.pallas_call(
        body, name=name, grid=(T // tm,),
        in_specs=[pl.BlockSpec((tm, K), lambda i: (i, 0)),
                  pl.BlockSpec((K, N), lambda i: (0, 0)),
                  pl.BlockSpec((tm, N), lambda i: (i, 0))],
        out_specs=pl.BlockSpec((tm, N), lambda i: (i, 0)),
        out_shape=S((T, N), F32),
        compiler_params=_cp("parallel"),
    )(a, w, res)


def _matmul_nt(a, w, name, out_dtype, after=()):
    T, N = a.shape
    K = w.shape[0]
    tm = min(T, PROJ_TM)

    def body(a_ref, w_ref, *rest):
        o_ref, ab_ref = rest[len(after):]
        ab = a_ref[...].astype(BF16)
        ab_ref[...] = ab
        o_ref[...] = _dot_nt(ab, w_ref[...]).astype(o_ref.dtype)

    return pl.pallas_call(
        body, name=name, grid=(T // tm,),
        in_specs=[pl.BlockSpec((tm, N), lambda i: (i, 0)),
                  pl.BlockSpec((K, N), lambda i: (0, 0))] + [pl.BlockSpec(memory_space=pl.ANY)] * len(after),
        out_specs=[pl.BlockSpec((tm, K), lambda i: (i, 0)),
                   pl.BlockSpec((tm, N), lambda i: (i, 0))],
        out_shape=[S((T, K), out_dtype), S((T, N), BF16)],
        compiler_params=_cp("parallel"),
    )(a, w, *after)


def _matmul_tn(a, b3, nb, name, blocked, after=()):
    T, M = a.shape
    SB, _, N = b3.shape
    per = N // nb
    NB = SB * per
    tk = min(T, 1024)
    nk = T // tk
    jb = max(1, TN_STEP_COLS // nb) if blocked else 1
    assert per % jb == 0
    if blocked:
        out_spec, out_shape = pl.BlockSpec((jb, M, nb), lambda j: (j, 0, 0)), S((NB, M, nb), BF16)
    else:
        assert SB == 1
        out_spec, out_shape = pl.BlockSpec((M, nb), lambda j: (0, j)), S((M, N), BF16)

    def body(a_ref, b_ref, *rest):
        o_ref = rest[-1]
        for q in range(jb):
            acc = None
            for k in range(nk):
                rows = slice(k * tk, (k + 1) * tk)
                part = _dot_tn(a_ref[rows, :], b_ref[rows, q * nb:(q + 1) * nb])
                acc = part if acc is None else acc + part
            if blocked:
                o_ref[q] = acc.astype(BF16)
            else:
                o_ref[...] = acc.astype(BF16)

    return pl.pallas_call(
        body, name=name, grid=(NB // jb,),
        in_specs=[pl.BlockSpec((T, M), lambda j: (0, 0), pipeline_mode=pl.Buffered(1)),
                  pl.BlockSpec((None, T, jb * nb), lambda j: ((j * jb) // per, 0, ((j * jb) % per) // jb))]
        + [pl.BlockSpec(memory_space=pl.ANY)] * len(after),
        out_specs=out_spec,
        out_shape=out_shape,
        compiler_params=_cp("parallel"),
    )(a, b3, *after)


def _nt_normbwd(dz3, wblk, x, g, dres, name, after=(), transposed=False):
    T, D = x.shape
    NB, nb = wblk.shape[0], wblk.shape[1 if transposed else 2]
    mm = _dot if transposed else _dot_nt
    SB, _, N = dz3.shape
    per = N // nb
    tm = min(T, PROJ_TM)

    def body(dz_ref, w_ref, x_ref, g_ref, dr_ref, *rest):
        dx_ref, dg_ref = rest[len(after):]

        @pl.when(pl.program_id(0) == 0)
        def _():
            dg_ref[...] = jnp.zeros_like(dg_ref)

        for c in range(PROJ_CHAINS):
            rows = slice(c * tm // PROJ_CHAINS, (c + 1) * tm // PROJ_CHAINS)
            dh = None
            for q in range(NB):
                cols = slice((q % per) * nb, (q % per + 1) * nb)
                part = mm(dz_ref[q // per, rows, cols], w_ref[q])
                dh = part if dh is None else dh + part
            dx, dg = _rms_bwd(dh, x_ref[rows, :], g_ref[...])
            dx_ref[rows, :] = dr_ref[rows, :] + dx
            dg_ref[...] += dg

    return pl.pallas_call(
        body, name=name, grid=(T // tm,),
        in_specs=[pl.BlockSpec((SB, tm, N), lambda i: (0, i, 0)),
                  pl.BlockSpec(wblk.shape, lambda i: (0, 0, 0)),
                  pl.BlockSpec((tm, D), lambda i: (i, 0)),
                  pl.BlockSpec((1, D), lambda i: (0, 0)),
                  pl.BlockSpec((tm, D), lambda i: (i, 0))] + [pl.BlockSpec(memory_space=pl.ANY)] * len(after),
        out_specs=[pl.BlockSpec((tm, D), lambda i: (i, 0)),
                   pl.BlockSpec((1, D), lambda i: (0, 0))],
        out_shape=[S((T, D), F32), S((1, D), F32)],
        compiler_params=_cp("arbitrary"),
    )(dz3, wblk, x, g, dres, *after)


def _loss_head(xv, tv, gv, D):
    err = xv * _rms_r(xv) * gv - tv
    e2 = jnp.sum(jnp.sum(err * err, axis=-1, keepdims=True), axis=0, keepdims=True)
    dx, dg = _rms_bwd(err * (1.0 / D), xv, gv)
    return (0.5 / D) * e2, dx, dg


def _mlp_fwd(x, g, wup, wdown, name, head=None):
    T, D = x.shape
    NB, _, fb = wup.shape
    tm = min(T, MLP_TM)
    n_head = 0 if head is None else 2

    def body(x_ref, g_ref, wu_ref, wd_ref, *rest):
        xo_ref, a_ref, h_ref = rest[n_head:n_head + 3]
        xv = x_ref[...]
        hb = (xv * _rms_r(xv) * g_ref[...]).astype(BF16)
        h_ref[...] = hb
        acc = xv
        for j in range(NB):
            a = _dot(hb, wu_ref[j])
            a_ref[:, j * fb:(j + 1) * fb] = a.astype(BF16)
            u = jnp.maximum(a, 0.0)
            acc = acc + _dot((u * u).astype(BF16), wd_ref[j * fb:(j + 1) * fb, :])

        if head is None:
            xo_ref[...] = acc
        else:
            t_ref, gf_ref = rest[:2]
            loss_ref, dgf_ref = rest[n_head + 3:n_head + 5]

            @pl.when(pl.program_id(0) == 0)
            def _():
                loss_ref[...] = jnp.zeros_like(loss_ref)
                dgf_ref[...] = jnp.zeros_like(dgf_ref)

            e2, dx, dg = _loss_head(acc, t_ref[...], gf_ref[...], D)
            xo_ref[...] = dx
            loss_ref[...] += e2
            dgf_ref[...] += dg

    row = pl.BlockSpec((tm, D), lambda i: (i, 0))
    vec = pl.BlockSpec((1, D), lambda i: (0, 0))
    once = pl.Buffered(1)
    in_specs = [row, vec, pl.BlockSpec((NB, D, fb), lambda i: (0, 0, 0), pipeline_mode=once),
                pl.BlockSpec((NB * fb, D), lambda i: (0, 0), pipeline_mode=once)]
    out_specs = [row, pl.BlockSpec((tm, NB * fb), lambda i: (i, 0)), row]
    out_shape = [S((T, D), F32), S((T, NB * fb), BF16), S((T, D), BF16)]
    if head is not None:
        in_specs += [row, vec]
        out_specs += [pl.BlockSpec((1, 128), lambda i: (0, 0)), vec]
        out_shape += [S((1, 128), F32), S((1, D), F32)]
    return pl.pallas_call(
        body, name=name, grid=(T // tm,),
        in_specs=in_specs, out_specs=out_specs, out_shape=out_shape,
        compiler_params=_cp("parallel" if head is None else "arbitrary"),
    )(x, g, wup, wdown, *(head or ()))


def _mlp_bwd_dx(x, dout, a, g, wup, wdown, name):
    T, D = x.shape
    NB, _, fb = wup.shape
    tm = min(T, MLP_TM)

    def body(x_ref, do_ref, a_ref, g_ref, wu_ref, wd_ref, dx_ref, da_ref, dob_ref, dg_ref):
        @pl.when(pl.program_id(0) == 0)
        def _():
            dg_ref[...] = jnp.zeros_like(dg_ref)

        dov = do_ref[...]
        dob = dov.astype(BF16)
        dob_ref[...] = dob
        dh = None
        for j in range(NB):
            cols = slice(j * fb, (j + 1) * fb)
            du2 = _dot_nt(dob, wd_ref[cols, :])
            u = jnp.maximum(a_ref[:, cols].astype(F32), 0.0)
            da = (du2 * (2.0 * u)).astype(BF16)
            da_ref[:, cols] = da
            part = _dot_nt(da, wu_ref[j])
            dh = part if dh is None else dh + part
        dx, dg = _rms_bwd(dh, x_ref[...], g_ref[...])
        dx_ref[...] = dov + dx
        dg_ref[...] += dg

    row = pl.BlockSpec((tm, D), lambda i: (i, 0))
    wide = pl.BlockSpec((tm, NB * fb), lambda i: (i, 0))
    vec = pl.BlockSpec((1, D), lambda i: (0, 0))
    once = pl.Buffered(1)
    return pl.pallas_call(
        body, name=name, grid=(T // tm,),
        in_specs=[row, row, wide, vec, pl.BlockSpec((NB, D, fb), lambda i: (0, 0, 0), pipeline_mode=once),
                  pl.BlockSpec((NB * fb, D), lambda i: (0, 0), pipeline_mode=once)],
        out_specs=[row, wide, row, vec],
        out_shape=[S((T, D), F32), S((T, NB * fb), BF16), S((T, D), BF16), S((1, D), F32)],
        compiler_params=_cp("arbitrary"),
    )(x, dout, a, g, wup, wdown)


def _mlp_bwd_dw(h, da, a, dob, fb, name):
    T, D = h.shape
    F = a.shape[1]
    NB = F // fb
    tk = min(T, 1024)
    nk = T // tk

    def body(h_ref, da_ref, a_ref, dob_ref, dwu_ref, dwd_ref):
        au = ad = None
        for k in range(nk):
            rows = slice(k * tk, (k + 1) * tk)
            pu = _dot_tn(h_ref[rows, :], da_ref[rows, :])
            u = jnp.maximum(a_ref[rows, :].astype(F32), 0.0)
            pd = _dot_tn((u * u).astype(BF16), dob_ref[rows, :])
            au, ad = (pu, pd) if au is None else (au + pu, ad + pd)
        dwu_ref[...] = au.astype(BF16)
        dwd_ref[...] = ad.astype(BF16)

    once = pl.Buffered(1)
    return pl.pallas_call(
        body, name=name, grid=(NB,),
        in_specs=[pl.BlockSpec((T, D), lambda j: (0, 0), pipeline_mode=once),
                  pl.BlockSpec((T, fb), lambda j: (0, j)),
                  pl.BlockSpec((T, fb), lambda j: (0, j)),
                  pl.BlockSpec((T, D), lambda j: (0, 0), pipeline_mode=once)],
        out_specs=[pl.BlockSpec((None, D, fb), lambda j: (j, 0, 0)),
                   pl.BlockSpec((fb, D), lambda j: (j, 0))],
        out_shape=[S((NB, D, fb), BF16), S((F, D), BF16)],
        compiler_params=_cp("parallel"),
    )(h, da, a, dob)


def _rope_tables(L):
    nf = HEAD_DIM // 4
    t = jnp.arange(L, dtype=jnp.int32)
    row = (t // GRID_W).astype(F32)
    col = (t % GRID_W).astype(F32)
    inv = ROPE_THETA ** (-jnp.arange(nf, dtype=F32) / nf)
    ar = row[:, None] * inv
    ac = col[:, None] * inv
    cos = jnp.concatenate([jnp.cos(ar), jnp.cos(ar), jnp.cos(ac), jnp.cos(ac)], axis=-1)
    sin = jnp.concatenate([-jnp.sin(ar), jnp.sin(ar), -jnp.sin(ac), jnp.sin(ac)], axis=-1)
    return cos, sin


def _swap32(x):
    lane = lax.broadcasted_iota(jnp.int32, x.shape, 1)
    up = pltpu.roll(x, HEAD_DIM - 32, 1)
    down = pltpu.roll(x, 32, 1)
    return jnp.where((lane % 64) < 32, up, down)


def _qkv_proj(x, g, wt_blk, qg, kg, cos, sin, L, name):
    T, D = x.shape
    NB, nb, _ = wt_blk.shape
    W = NB * nb
    nh = W // HEAD_DIM - 2 * N_KV
    tm = min(L, 512)
    lb = L // tm

    def body(x_ref, g_ref, w_ref, qg_ref, kg_ref, cos_ref, sin_ref, qkv_ref, h_ref, q_ref, k_ref, v_ref):
        for ch in range(PROJ_CHAINS):
            rows = slice(ch * tm // PROJ_CHAINS, (ch + 1) * tm // PROJ_CHAINS)
            xv = x_ref[rows, :]
            hb = (xv * _rms_r(xv) * g_ref[...]).astype(BF16)
            h_ref[rows, :] = hb
            for j in range(NB):
                qkv_ref[rows, j * nb:(j + 1) * nb] = _dot_nt(hb, w_ref[j])
            c = cos_ref[rows, :]
            s = sin_ref[rows, :]
            for h in range(nh + N_KV):
                xh = qkv_ref[rows, h * HEAD_DIM:(h + 1) * HEAD_DIM]
                gv = qg_ref[...] if h < nh else kg_ref[...]
                y = xh * _rms_r(xh) * gv
                y = (y * c + _swap32(y) * s).astype(BF16)
                if h < nh:
                    q_ref[rows, h * HEAD_DIM:(h + 1) * HEAD_DIM] = y
                else:
                    k_ref[rows, (h - nh) * HEAD_DIM:(h - nh + 1) * HEAD_DIM] = y
            v_ref[rows, :] = qkv_ref[rows, (nh + N_KV) * HEAD_DIM:].astype(BF16)

    row = lambda cols: pl.BlockSpec((tm, cols), lambda i: (i, 0))
    vec = lambda cols: pl.BlockSpec((1, cols), lambda i: (0, 0))
    table = pl.BlockSpec((tm, HEAD_DIM), lambda i: (i % lb, 0))
    return pl.pallas_call(
        body, name=name, grid=(T // tm,),
        in_specs=[row(D), vec(D), pl.BlockSpec(wt_blk.shape, lambda i: (0, 0, 0)), vec(HEAD_DIM), vec(HEAD_DIM), table, table],
        out_specs=[row(W), row(D), row(nh * HEAD_DIM), row(N_KV * HEAD_DIM), row(N_KV * HEAD_DIM)],
        out_shape=[S((T, W), F32), S((T, D), BF16), S((T, nh * HEAD_DIM), BF16), S((T, N_KV * HEAD_DIM), BF16),
                   S((T, N_KV * HEAD_DIM), BF16)],
        compiler_params=_cp("parallel"),
    )(x, g, wt_blk, qg, kg, cos, sin)


def _qk_prep_bwd(qkv, dq, dk, dv, qg, kg, cos, sin, L, name):
    T, W = qkv.shape
    nh = W // HEAD_DIM - 2 * N_KV
    tm = min(L, 512)
    lb = L // tm

    def body(qkv_ref, dq_ref, dk_ref, dv_ref, qg_ref, kg_ref, cos_ref, sin_ref, dz_ref, dqg_ref, dkg_ref):
        @pl.when(pl.program_id(0) == 0)
        def _():
            dqg_ref[...] = jnp.zeros_like(dqg_ref)
            dkg_ref[...] = jnp.zeros_like(dkg_ref)

        c = cos_ref[...]
        s = sin_ref[...]
        for h in range(nh + N_KV):
            cols = slice(h * HEAD_DIM, (h + 1) * HEAD_DIM)
            if h < nh:
                dout, gv, dg_ref = dq_ref[:, cols], qg_ref[...], dqg_ref
            else:
                kc = slice((h - nh) * HEAD_DIM, (h - nh + 1) * HEAD_DIM)
                dout, gv, dg_ref = dk_ref[:, kc], kg_ref[...], dkg_ref
            dy = dout * c - _swap32(dout) * s
            dx, dg = _rms_bwd(dy, qkv_ref[:, cols], gv)
            dg_ref[...] += dg
            dz_ref[:, cols] = dx.astype(BF16)
        dz_ref[:, (nh + N_KV) * HEAD_DIM:] = dv_ref[...].astype(BF16)

    return pl.pallas_call(
        body, name=name, grid=(T // tm,),
        in_specs=[pl.BlockSpec((tm, W), lambda i: (i, 0)),
                  pl.BlockSpec((tm, nh * HEAD_DIM), lambda i: (i, 0)),
                  pl.BlockSpec((tm, N_KV * HEAD_DIM), lambda i: (i, 0)),
                  pl.BlockSpec((tm, N_KV * HEAD_DIM), lambda i: (i, 0)),
                  pl.BlockSpec((1, HEAD_DIM), lambda i: (0, 0)),
                  pl.BlockSpec((1, HEAD_DIM), lambda i: (0, 0)),
                  pl.BlockSpec((tm, HEAD_DIM), lambda i: (i % lb, 0)),
                  pl.BlockSpec((tm, HEAD_DIM), lambda i: (i % lb, 0))],
        out_specs=[pl.BlockSpec((tm, W), lambda i: (i, 0)),
                   pl.BlockSpec((1, HEAD_DIM), lambda i: (0, 0)),
                   pl.BlockSpec((1, HEAD_DIM), lambda i: (0, 0))],
        out_shape=[S((T, W), BF16), S((1, HEAD_DIM), F32), S((1, HEAD_DIM), F32)],
        compiler_params=_cp("arbitrary"),
    )(qkv, dq, dk, dv, qg, kg, cos, sin)


EXP2_SCALE = SOFTMAX_SCALE * math.log2(math.e)
ATTN_SUB = 256
ATTN_TQ = 1024


def _softmax_rows(q, k):
    s = _dot_nt(q, k)
    e = jnp.exp2((s - jnp.max(s, axis=-1, keepdims=True)) * EXP2_SCALE)
    return e, jnp.sum(e, axis=-1, keepdims=True)


def _attn_fwd(q, k, v, L, name):
    T = q.shape[0]
    nh = q.shape[1] // HEAD_DIM
    G = nh // N_KV
    B = T // L
    tq = min(L, ATTN_TQ)
    nq = L // tq
    sub = min(tq, ATTN_SUB)

    def body(q_ref, k_ref, v_ref, o_ref):
        for h in range(tq // sub):
            rows = slice(h * sub, (h + 1) * sub)
            e, l = _softmax_rows(q_ref[rows, :], k_ref[...])
            o_ref[rows, :] = (_dot(e.astype(BF16), v_ref[...]) / l).astype(BF16)

    qspec = pl.BlockSpec((tq, HEAD_DIM), lambda b, kv, g, qi: (b * nq + qi, kv * G + g))
    kspec = pl.BlockSpec((L, HEAD_DIM), lambda b, kv, g, qi: (b, kv))
    return pl.pallas_call(
        body, name=name, grid=(B, N_KV, G, nq),
        in_specs=[qspec, kspec, kspec],
        out_specs=qspec,
        out_shape=S((T, nh * HEAD_DIM), BF16),
        compiler_params=_cp("parallel", "parallel", "parallel", "parallel"),
    )(q, k, v)


def _attn_bwd(q, k, v, do, o, L, name):
    T = q.shape[0]
    nh = q.shape[1] // HEAD_DIM
    G = nh // N_KV
    B = T // L
    tq = min(L, ATTN_TQ)
    nq = L // tq

    sub = min(tq, ATTN_SUB)

    def body(q_ref, k_ref, v_ref, do_ref, o_ref, dq_ref, dk_ref, dv_ref, ds_scr, p_scr):
        first = (pl.program_id(2) == 0) & (pl.program_id(3) == 0)
        last = (pl.program_id(2) == G - 1) & (pl.program_id(3) == nq - 1)

        @pl.when(first)
        def _():
            dk_ref[...] = jnp.zeros_like(dk_ref)
            dv_ref[...] = jnp.zeros_like(dv_ref)

        for h in range(tq // sub):
            rows = slice(h * sub, (h + 1) * sub)
            dov = do_ref[rows, :]
            e, l = _softmax_rows(q_ref[rows, :], k_ref[...])
            p = e * (1.0 / l)
            dsum = jnp.sum(dov.astype(F32) * o_ref[rows, :].astype(F32), axis=-1, keepdims=True)
            ds_scr[rows, :] = (p * (_dot_nt(dov, v_ref[...]) - dsum)).astype(BF16)
            p_scr[rows, :] = p.astype(BF16)
        ds = ds_scr[...]
        dq_ref[...] = _dot(ds, k_ref[...]) * SOFTMAX_SCALE
        dk_ref[...] += _dot_tn(ds, q_ref[...])
        dv_ref[...] += _dot_tn(p_scr[...], do_ref[...])

        @pl.when(last)
        def _():
            dk_ref[...] = dk_ref[...] * SOFTMAX_SCALE

    qspec = pl.BlockSpec((tq, HEAD_DIM), lambda b, kv, g, qi: (b * nq + qi, kv * G + g))
    kspec = pl.BlockSpec((L, HEAD_DIM), lambda b, kv, g, qi: (b, kv))
    return pl.pallas_call(
        body, name=name, grid=(B, N_KV, G, nq),
        in_specs=[qspec, kspec, kspec, qspec, qspec],
        out_specs=[qspec, kspec, kspec],
        out_shape=[S((T, nh * HEAD_DIM), F32), S((T, N_KV * HEAD_DIM), F32), S((T, N_KV * HEAD_DIM), F32)],
        scratch_shapes=[pltpu.VMEM((tq, L), BF16), pltpu.VMEM((tq, L), BF16)],
        compiler_params=_cp("parallel", "parallel", "arbitrary", "arbitrary"),
    )(q, k, v, do, o)


PV_CONV_W = 0
PV_B_A = 8
PV_B_X = 16
PV_LAM = 24
PV_CONV_B = 32
PV_ROWS = 40


def _shift_rows(x, k):
    if k == 0:
        return x
    L = x.shape[0]
    n = N_SEG * abs(k)
    seg = lax.broadcasted_iota(jnp.int32, (n, x.shape[1]), 0) % N_SEG
    if k > 0:
        edge = jnp.where(seg == 0, 0.0, pltpu.roll(x[L - n:], 1, 0))
        return jnp.concatenate([edge, x[:L - n]], axis=0)
    edge = jnp.where(seg == N_SEG - 1, 0.0, pltpu.roll(x[:n], n - 1, 0))
    return jnp.concatenate([x[n:], edge], axis=0)


def _conv_taps(rec, pv):
    c = pv[PV_CONV_B:PV_CONV_B + 1]
    for j in range(CONV_W):
        c = c + pv[PV_CONV_W + j:PV_CONV_W + j + 1] * _shift_rows(rec, 2 - j)
    return c


def _sigmoid(x):
    return 0.5 * jnp.tanh(0.5 * x) + 0.5


EXPM1_SERIES_BELOW = 0.03


def _rg_gates(c, cbf, wa, wx, ba, bx, lam):
    r = _sigmoid(_dot(cbf, wa) + ba)
    i = _sigmoid(_dot(cbf, wx) + bx)
    sp = jnp.maximum(-lam, 0.0) + jnp.log1p(jnp.exp(-jnp.abs(lam)))
    la = r * ((-RG_C) * sp)
    a = jnp.exp(la)
    a2 = a * a
    x = la + la
    series = -(x * ((x * (1.0 / 6.0) + 0.5) * x + 1.0))
    om = jnp.where(x > -EXPM1_SERIES_BELOW, series, 1.0 - a2)
    rm = lax.rsqrt(om)
    return r, i, a, om * rm, rm, a2, sp


def _gelu(x):
    t = jnp.tanh(GELU_K * (x + GELU_C * x * x * x))
    return 0.5 * x * (1.0 + t), t


def _scan_pair(af_ref, uf_ref, ab_ref, ub_ref, hf_ref, hb_ref, pf_ref, pb_ref, L):
    ls = L // N_SEG
    zero = jnp.zeros((N_SEG, LRU_BW), F32)
    one = jnp.ones((N_SEG, LRU_BW), F32)
    tile = lambda t: pl.ds(pl.multiple_of(t * N_SEG, N_SEG), N_SEG)

    def steps(tc, carry):
        hf, pf, hb, pb = carry
        for q in range(SCAN_UNROLL):
            t = tc * SCAN_UNROLL + q
            rf, rb = tile(t), tile(ls - 1 - t)
            af = af_ref[rf, :]
            hf = af * hf + uf_ref[rf, :]
            pf = pf * af
            hf_ref[rf, :] = hf
            pf_ref[rf, :] = pf
            ab = ab_ref[rb, :]
            hb = ab * hb + ub_ref[rb, :]
            pb = pb * ab
            hb_ref[rb, :] = hb
            pb_ref[rb, :] = pb
        return hf, pf, hb, pb

    hf_e, pf_e, hb_e, pb_e = lax.fori_loop(0, ls // SCAN_UNROLL, steps, (zero, one, zero, one))

    rows, cin = [], jnp.zeros((1, LRU_BW), F32)
    for s in range(N_SEG):
        rows.append(cin)
        cin = hf_e[s:s + 1] + pf_e[s:s + 1] * cin
    cf = jnp.concatenate(rows, axis=0)
    rows, cin = [], jnp.zeros((1, LRU_BW), F32)
    for s in reversed(range(N_SEG)):
        rows.append(cin)
        cin = hb_e[s:s + 1] + pb_e[s:s + 1] * cin
    cb = jnp.concatenate(rows[::-1], axis=0)

    def fix(tc, _):
        for q in range(SCAN_UNROLL):
            r = tile(tc * SCAN_UNROLL + q)
            hf_ref[r, :] = hf_ref[r, :] + pf_ref[r, :] * cf
            hb_ref[r, :] = hb_ref[r, :] + pb_ref[r, :] * cb
        return 0

    lax.fori_loop(0, ls // SCAN_UNROLL, fix, 0)


def _rg_specs(L, D, nblk):
    slab = lambda off: pl.BlockSpec((L, LRU_BW), lambda cb, b: (b, off + cb))
    wspec = pl.BlockSpec((2, None, LRU_BW, LRU_BW), lambda cb, b: (0, cb, 0, 0))
    pvspec = pl.BlockSpec((PV_ROWS, LRU_BW), lambda cb, b: (0, cb))
    return slab, wspec, pvspec


def _rg_fwd(z, pvec, wa, wx, L, name):
    T, C2 = z.shape
    C = C2 // 2
    nblk = C // LRU_BW
    B = T // L
    slab, wspec, pvspec = _rg_specs(L, C, nblk)

    def body(gp_ref, rec_ref, pv_ref, wa_ref, wx_ref, yg_ref, hf_ref, hb_ref, a_scr, u_scr, p_scr):
        pv = pv_ref[...]
        c = _conv_taps(rec_ref[...], pv)
        cbf = c.astype(BF16)
        for d in range(2):
            _, i, a, m, _, _, _ = _rg_gates(c, cbf, wa_ref[d], wx_ref[d], pv[PV_B_A + d:PV_B_A + d + 1],
                                      pv[PV_B_X + d:PV_B_X + d + 1], pv[PV_LAM + d:PV_LAM + d + 1])
            a_scr[d] = a
            u_scr[d] = m * (i * c)
        _scan_pair(a_scr.at[0], u_scr.at[0], a_scr.at[1], u_scr.at[1], hf_ref, hb_ref, p_scr.at[0], p_scr.at[1], L)
        gate, _ = _gelu(gp_ref[...])
        yg_ref[...] = ((hf_ref[...] + hb_ref[...]) * gate).astype(BF16)

    return pl.pallas_call(
        body, name=name, grid=(nblk, B),
        in_specs=[slab(0), slab(nblk), pvspec, wspec, wspec],
        out_specs=[slab(0), slab(0), slab(0)],
        out_shape=[S((T, C), BF16), S((T, C), F32), S((T, C), F32)],
        scratch_shapes=[pltpu.VMEM((2, L, LRU_BW), F32)] * 3,
        compiler_params=_cp("parallel", "parallel"),
    )(z, z, pvec, wa, wx)


def _rg_bwd(z, hf, hb, dyg, pvec, wa, wx, L, name):
    T, C2 = z.shape
    C = C2 // 2
    nblk = C // LRU_BW
    B = T // L
    slab, wspec, pvspec = _rg_specs(L, C, nblk)

    def body(gp_ref, rec_ref, hf_ref, hb_ref, dyg_ref, pv_ref, wa_ref, wx_ref,
             dz_ref, dwa_ref, dwx_ref, dpv_ref, a_scr, u_scr, d_scr, p_scr):
        @pl.when(pl.program_id(1) == 0)
        def _():
            dwa_ref[...] = jnp.zeros_like(dwa_ref)
            dwx_ref[...] = jnp.zeros_like(dwx_ref)
            dpv_ref[...] = jnp.zeros_like(dpv_ref)

        pv = pv_ref[...]
        rec = rec_ref[...]
        c = _conv_taps(rec, pv)
        cbf = c.astype(BF16)
        gp = gp_ref[...]
        gate, th = _gelu(gp)
        dgelu = 0.5 * (1.0 + th) + 0.5 * gp * (1.0 - th * th) * GELU_K * (1.0 + 3.0 * GELU_C * gp * gp)
        dyg = dyg_ref[...]
        dz_ref[0] = (dyg * (hf_ref[...] + hb_ref[...]) * dgelu).astype(BF16)
        dy = dyg * gate

        gates = []
        for d in range(2):
            gates.append(_rg_gates(c, cbf, wa_ref[d], wx_ref[d], pv[PV_B_A + d:PV_B_A + d + 1],
                                   pv[PV_B_X + d:PV_B_X + d + 1], pv[PV_LAM + d:PV_LAM + d + 1]))
        a_scr[0] = _shift_rows(gates[1][2], 1)
        a_scr[1] = _shift_rows(gates[0][2], -1)
        u_scr[...] = dy
        _scan_pair(a_scr.at[0], u_scr, a_scr.at[1], u_scr, d_scr.at[1], d_scr.at[0], p_scr.at[0], p_scr.at[1], L)

        dc = jnp.zeros_like(c)
        rows = []
        for d in range(2):
            r, i, a, m, rm, a2, sp = gates[d]
            delta = d_scr[d]
            hnb = _shift_rows(hf_ref[...], 1) if d == 0 else _shift_rows(hb_ref[...], -1)
            da = delta * hnb
            dm = delta * (i * c)
            di = delta * (m * c)
            dc = dc + delta * (m * i)
            dla = da * a - dm * (a2 * rm)
            dpa = (dla * ((-RG_C) * sp)) * (r * (1.0 - r))
            dpx = di * (i * (1.0 - i))
            dsp = (-RG_C) * jnp.sum(dla * r, axis=0, keepdims=True)
            lam = pv[PV_LAM + d:PV_LAM + d + 1]
            rows.append((jnp.sum(dpa, axis=0, keepdims=True), jnp.sum(dpx, axis=0, keepdims=True),
                         -dsp * _sigmoid(-lam)))
            dpab = dpa.astype(BF16)
            dpxb = dpx.astype(BF16)
            dwa_ref[d] += _dot_tn(cbf, dpab)
            dwx_ref[d] += _dot_tn(cbf, dpxb)
            dc = dc + _dot_nt(dpab, wa_ref[d]) + _dot_nt(dpxb, wx_ref[d])

        drec = jnp.zeros_like(c)
        dcw = []
        for j in range(CONV_W):
            drec = drec + pv[PV_CONV_W + j:PV_CONV_W + j + 1] * _shift_rows(dc, j - 2)
            dcw.append(jnp.sum(dc * _shift_rows(rec, 2 - j), axis=0, keepdims=True))
        dz_ref[1] = drec.astype(BF16)
        for j in range(CONV_W):
            dpv_ref[PV_CONV_W + j:PV_CONV_W + j + 1, :] += dcw[j]
        for d in range(2):
            dpv_ref[PV_B_A + d:PV_B_A + d + 1, :] += rows[d][0]
            dpv_ref[PV_B_X + d:PV_B_X + d + 1, :] += rows[d][1]
            dpv_ref[PV_LAM + d:PV_LAM + d + 1, :] += rows[d][2]
        dpv_ref[PV_CONV_B:PV_CONV_B + 1, :] += jnp.sum(dc, axis=0, keepdims=True)

    return pl.pallas_call(
        body, name=name, grid=(nblk, B),
        in_specs=[slab(0), slab(nblk), slab(0), slab(0), slab(0), pvspec, wspec, wspec],
        out_specs=[pl.BlockSpec((2, L, LRU_BW), lambda cb, b: (0, b, cb)), wspec, wspec, pvspec],
        out_shape=[S((2, T, C), BF16), S((2, nblk, LRU_BW, LRU_BW), F32), S((2, nblk, LRU_BW, LRU_BW), F32),
                   S((PV_ROWS, C), F32)],
        scratch_shapes=[pltpu.VMEM((2, L, LRU_BW), F32), pltpu.VMEM((L, LRU_BW), F32),
                        pltpu.VMEM((2, L, LRU_BW), F32), pltpu.VMEM((2, L, LRU_BW), F32)],
        compiler_params=_cp("parallel", "arbitrary"),
    )(z, z, hf, hb, dyg, pvec, wa, wx)


QKV_NB = 512


def _interleave(a):
    *lead, L, D = a.shape
    return a.reshape(*lead, N_SEG, L // N_SEG, D).swapaxes(-3, -2).reshape(*lead, L, D)


def _deinterleave(a):
    *lead, L, D = a.shape
    return a.reshape(*lead, L // N_SEG, N_SEG, D).swapaxes(-3, -2).reshape(*lead, L, D)


def _local_step(x3, tgt3, w, fetch, send):
    Bl, L, D = x3.shape
    T = Bl * L
    x = _interleave(x3).reshape(T, D)
    tgt = _interleave(tgt3).reshape(T, D)
    gm = [w["g_mix"][i:i + 1] for i in range(2)]
    gl = [w["g_mlp"][i:i + 1] for i in range(2)]

    w0 = fetch(0, ())
    nb_in = w0["w_in"].shape[-1]
    z, h0 = _norm_matmul(x, gm[0], w0["w_in"], "rg_in")
    yg, hf, hb = _rg_fwd(z, w0["pvec"], w["wa"], w["wx"], L, "rg_fwd")
    w1 = fetch(1, (yg,))
    x1 = _matmul_res(yg, w1["w_out"], x, "rg_out")
    w1.update(fetch(4, (x1,)))
    fb = w1["w_up0"].shape[-1]
    x2, a0, hm0 = _mlp_fwd(x1, gl[0], w1["w_up0"], w1["w_down0"], "mlp0_fwd")
    w2 = fetch(2, (x2,))
    cos, sin = [_interleave(t) for t in _rope_tables(L)]
    qkv, h1, qn, kn, vb = _qkv_proj(x2, gm[1], w2["w_qkv"], w["qg"], w["kg"], cos, sin, L, "at_qkv")
    o = _attn_fwd(qn, kn, vb, L, "at_fwd")
    x3_ = _matmul_res(o, w2["w_o"], x2, "at_out")
    w3 = fetch(3, (x3_,))
    dx4, a1, hm1, loss, dgf = _mlp_fwd(x3_, gl[1], w3["w_up1"], w3["w_down1"], "mlp1_fwd", head=(tgt, w["g_fin"]))

    dx3, da1, dob1, dgl1 = _mlp_bwd_dx(x3_, dx4, a1, gl[1], w3["w_up1"], w3["w_down1"], "mlp1_bwd_dx")
    dwu1, dwd1 = _mlp_bwd_dw(hm1, da1, a1, dob1, fb, "mlp1_bwd_dw")
    sent = send(3, dict(w_up1=dwu1, w_down1=dwd1))
    do, dx3b = _matmul_nt(dx3, w2["w_o"], "at_out_bwd", BF16, after=sent)
    dwo = _matmul_tn(o, dx3b[None], QKV_NB, "at_out_dw", blocked=False)
    dq, dk, dv = _attn_bwd(qn, kn, vb, do, o, L, "at_bwd")
    dqkv, dqg, dkg = _qk_prep_bwd(qkv, dq, dk, dv, w["qg"], w["kg"], cos, sin, L, "at_prep_bwd")
    dwqkv = _matmul_tn(dqkv, h1[None], QKV_NB, "at_qkv_dw", blocked=False)
    sent = send(2, dict(w_qkv=dwqkv, w_o=dwo))
    dx2, dgm1 = _nt_normbwd(dqkv[None], w2["w_qkv"], x2, gm[1], dx3, "at_qkv_bwd", after=sent, transposed=True)
    dx1, da0, dob0, dgl0 = _mlp_bwd_dx(x1, dx2, a0, gl[0], w1["w_up0"], w1["w_down0"], "mlp0_bwd_dx")
    dwu0, dwd0 = _mlp_bwd_dw(hm0, da0, a0, dob0, fb, "mlp0_bwd_dw")
    sent = send(1, dict(w_up0=dwu0, w_down0=dwd0))
    dyg, dx1b = _matmul_nt(dx1, w1["w_out"], "rg_out_bwd", F32, after=sent)
    dwout = _matmul_tn(yg, dx1b[None], QKV_NB, "rg_out_dw", blocked=False)
    dz, dwa, dwx, dpv = _rg_bwd(z, hf, hb, dyg, w0["pvec"], w["wa"], w["wx"], L, "rg_bwd")
    sent = send(4, dict(w_out=dwout, pvec=dpv, wa=dwa, wx=dwx))
    dwin = _matmul_tn(h0, dz, nb_in, "rg_in_dw", blocked=True, after=sent)
    sent = send(0, dict(w_in=dwin))
    dx0, dgm0 = _nt_normbwd(dz, w0["w_in"], x, gm[0], dx1, "rg_in_bwd", after=sent)
    send(-1, dict(g_mix=[dgm0, dgm1], g_mlp=[dgl0, dgl1], g_fin=dgf, conv_b=dpv[PV_CONV_B:PV_CONV_B + 1], qg=dqg, kg=dkg, loss=loss))
    return _deinterleave(dx0.reshape(Bl, L, D))


MESH = pl.DeviceIdType.MESH
ANY = pl.BlockSpec(memory_space=pl.ANY)
N_PEERS = N_DEV - 1


def _my_place():
    return lax.axis_index("x"), lax.axis_index("y"), lax.axis_index("c")


def _flat(px, py, pc):
    return 4 * px + 2 * py + pc


def _all_gather(shards, name):
    n = len(shards)

    def body(*refs):
        ins, outs = refs[:n], refs[n:2 * n]
        send_sems, recv_sems, local_sems = refs[2 * n:]
        x, y, c = _my_place()
        me, sibling = (x, y, c), (x, y, 1 - c)
        chips = [(1 - x, y), (x, 1 - y), (1 - x, 1 - y)]

        def copy(a, k, block, to, src=None):
            dst = outs[a].at[_flat(*block)]
            return pltpu.make_async_remote_copy(
                src_ref=dst if src is None else src, dst_ref=dst,
                send_sem=send_sems.at[a, k], recv_sem=recv_sems.at[a, k],
                device_id=to, device_id_type=MESH)

        mine = [pltpu.make_async_copy(ins[a], outs[a].at[_flat(*me)], local_sems.at[a]) for a in range(n)]
        for cp in mine:
            cp.start()
        first = []
        for a in range(n):
            first.append(copy(a, 0, me, sibling, src=ins[a]))
            first += [copy(a, 1 + j, me, (*chip, c), src=ins[a]) for j, chip in enumerate(chips)]
        for cp in first:
            cp.start()
        passed = []
        for j, chip in enumerate(chips):
            for a in range(n):
                copy(a, 1 + j, (*chip, c), me).wait_recv()
                fwd = copy(a, 4 + j, (*chip, c), sibling)
                fwd.start()
                passed.append(fwd)
        for a in range(n):
            copy(a, 0, sibling, me).wait_recv()
            for j, chip in enumerate(chips):
                copy(a, 4 + j, (*chip, 1 - c), me).wait_recv()
        for cp in first + passed:
            cp.wait_send()
        for cp in mine:
            cp.wait()

    return pl.pallas_call(
        body, name=name,
        in_specs=[ANY] * n, out_specs=[ANY] * n,
        out_shape=[S((N_DEV,) + s.shape, s.dtype) for s in shards],
        scratch_shapes=[pltpu.SemaphoreType.DMA((n, N_PEERS)), pltpu.SemaphoreType.DMA((n, N_PEERS)),
                        pltpu.SemaphoreType.DMA((n,))],
    )(*shards)


HBM = pl.BlockSpec(memory_space=pltpu.HBM)
SEM = pl.BlockSpec(memory_space=pltpu.SEMAPHORE)
SIDE_EFFECT = pltpu.SideEffectType.DATAFLOW_SIDE_EFFECTING
SEMS_PER_GROUP = 3


NEAR_PEERS = (1, 2, 4, 6)
FAR_CHIPS = (2, 4, 6)


def _exchange_copies(srcs, lands, sems, mode):
    send_sems, recv_sems, local_sems = sems
    scatter = mode == "scatter"
    x, y, c = _my_place()
    me = _flat(x, y, c)
    remote, local = [], []
    for a in range(len(srcs)):
        for r in (NEAR_PEERS if mode == "near" else range(1, N_DEV)):
            peer = (1 - x if r & 4 else x, 1 - y if r & 2 else y, 1 - c if r & 1 else c)
            remote.append(pltpu.make_async_remote_copy(
                src_ref=srcs[a].at[_flat(*peer)] if scatter else srcs[a], dst_ref=lands[a].at[me],
                send_sem=send_sems.at[a * N_PEERS + r - 1], recv_sem=recv_sems.at[a * N_PEERS + r - 1],
                device_id=peer, device_id_type=MESH))
        local.append(pltpu.make_async_copy(srcs[a].at[me] if scatter else srcs[a], lands[a].at[me], local_sems.at[a]))
    return remote, local


def _exchange_start(groups, modes, name):
    sizes = [len(g) for g in groups]
    srcs = [pltpu.with_memory_space_constraint(a, pltpu.HBM) for g in groups for a in g]
    n = len(srcs)
    scatter_of = [m == "scatter" for g, m in zip(groups, modes) for _ in g]
    lands = [pltpu.with_memory_space_constraint(lax.empty(a.shape if sc else (N_DEV,) + a.shape, a.dtype), pltpu.HBM)
             for a, sc in zip(srcs, scatter_of)]
    n_sem = SEMS_PER_GROUP * len(groups)

    def body(*refs):
        src_refs, land_refs, sem_refs, token = refs[:n], refs[n:2 * n], refs[2 * n:2 * n + n_sem], refs[-1]
        off = 0
        for gi, k in enumerate(sizes):
            remote, local = _exchange_copies(src_refs[off:off + k], land_refs[off:off + k],
                                             sem_refs[SEMS_PER_GROUP * gi:SEMS_PER_GROUP * (gi + 1)], modes[gi])
            for cp in local + remote:
                cp.start()
            off += k
        token[...] = jnp.zeros_like(token)

    sem_shapes = []
    for k in sizes:
        sem_shapes += [pltpu.SemaphoreType.DMA((k * N_PEERS,)), pltpu.SemaphoreType.DMA((k * N_PEERS,)),
                       pltpu.SemaphoreType.DMA((k,))]
    outs = pl.pallas_call(
        body, name=name,
        out_shape=sem_shapes + [pltpu.HBM(a.shape, a.dtype) for a in srcs + lands] + [S((8, 128), F32)],
        in_specs=[HBM] * (2 * n),
        out_specs=[SEM] * n_sem + [HBM] * (2 * n) + [pl.BlockSpec(memory_space=pltpu.VMEM)],
        input_output_aliases={i: n_sem + i for i in range(2 * n)},
        compiler_params=pltpu.CompilerParams(has_side_effects=SIDE_EFFECT),
    )(*srcs, *lands)
    sems, thru, token = outs[:n_sem], outs[n_sem:n_sem + 2 * n], outs[-1]
    per_group, off = [], 0
    for gi, k in enumerate(sizes):
        per_group.append((sems[SEMS_PER_GROUP * gi:SEMS_PER_GROUP * (gi + 1)], thru[off:off + k], thru[n + off:n + off + k]))
        off += k
    return per_group, token


def _exchange_wait(group, after, mode, name):
    sems, srcs, lands = group
    k = len(srcs)

    def body(*refs):
        remote, local = _exchange_copies(refs[:k], refs[k:2 * k], refs[2 * k:2 * k + SEMS_PER_GROUP], mode)
        for cp in remote:
            cp.wait_send()
            cp.wait_recv()
        for cp in local:
            cp.wait()

    outs = pl.pallas_call(
        body, name=name,
        out_shape=[pltpu.HBM(a.shape, a.dtype) for a in list(srcs) + list(lands)],
        in_specs=[HBM] * (2 * k) + [SEM] * SEMS_PER_GROUP + [ANY] * len(after),
        out_specs=[HBM] * (2 * k),
        input_output_aliases={i: i for i in range(2 * k)},
        compiler_params=pltpu.CompilerParams(has_side_effects=SIDE_EFFECT),
    )(*srcs, *lands, *sems, *after)
    return outs[k:]


def _forward_copies(lands, sems):
    send_sems, recv_sems = sems
    x, y, c = _my_place()
    mine, theirs = [], []
    for a in range(len(lands)):
        for k, r in enumerate(FAR_CHIPS):
            px, py = (1 - x if r & 4 else x), (1 - y if r & 2 else y)
            for out, core in ((mine, c), (theirs, 1 - c)):
                blk = lands[a].at[_flat(px, py, core)]
                out.append(pltpu.make_async_remote_copy(
                    src_ref=blk, dst_ref=blk, send_sem=send_sems.at[a * len(FAR_CHIPS) + k],
                    recv_sem=recv_sems.at[a * len(FAR_CHIPS) + k], device_id=(x, y, 1 - c), device_id_type=MESH))
    return mine, theirs


def _forward_start(groups, name):
    sizes = [len(g) for g in groups]
    lands = [a for g in groups for a in g]
    n = len(lands)
    n_sem = 2 * len(groups)

    def body(*refs):
        land_refs, sem_refs, token = refs[:n], refs[n:n + n_sem], refs[-1]
        off = 0
        for gi, k in enumerate(sizes):
            mine, _ = _forward_copies(land_refs[off:off + k], sem_refs[2 * gi:2 * gi + 2])
            for cp in mine:
                cp.start()
            off += k
        token[...] = jnp.zeros_like(token)

    sem_shapes = []
    for k in sizes:
        sem_shapes += [pltpu.SemaphoreType.DMA((k * len(FAR_CHIPS),))] * 2
    outs = pl.pallas_call(
        body, name=name,
        out_shape=sem_shapes + [pltpu.HBM(a.shape, a.dtype) for a in lands] + [S((8, 128), F32)],
        in_specs=[HBM] * n,
        out_specs=[SEM] * n_sem + [HBM] * n + [pl.BlockSpec(memory_space=pltpu.VMEM)],
        input_output_aliases={i: n_sem + i for i in range(n)},
        compiler_params=pltpu.CompilerParams(has_side_effects=SIDE_EFFECT),
    )(*lands)
    per_group, off = [], 0
    for gi, k in enumerate(sizes):
        per_group.append((outs[2 * gi:2 * gi + 2], outs[n_sem + off:n_sem + off + k]))
        off += k
    return per_group


def _forward_wait(group, after, name):
    sems, lands = group
    k = len(lands)

    def body(*refs):
        mine, theirs = _forward_copies(refs[:k], refs[k:k + 2])
        for cp in mine:
            cp.wait_send()
        for cp in theirs:
            cp.wait_recv()

    return pl.pallas_call(
        body, name=name,
        out_shape=[pltpu.HBM(a.shape, a.dtype) for a in lands],
        in_specs=[HBM] * k + [SEM] * 2 + [ANY] * len(after),
        out_specs=[HBM] * k,
        input_output_aliases={i: i for i in range(k)},
        compiler_params=pltpu.CompilerParams(has_side_effects=SIDE_EFFECT),
    )(*lands, *sems, *after)


def _row_tile(rows, cols):
    want = max(16, (128 * 1024) // cols)
    if rows <= want:
        return rows
    t = want - want % 16
    while rows % t:
        t -= 16
    return t


def _sum_parts(parts, name, after=()):
    P, R, C = parts.shape
    tr = _row_tile(R, C)

    def body(p_ref, *rest):
        o_ref = rest[-1]
        g = p_ref[0].astype(F32)
        for i in range(1, P):
            g = g + p_ref[i].astype(F32)
        o_ref[...] = g

    return pl.pallas_call(
        body, name=name, grid=(R // tr,),
        in_specs=[pl.BlockSpec((P, tr, C), lambda i: (0, i, 0))] + [ANY] * len(after),
        out_specs=pl.BlockSpec((tr, C), lambda i: (i, 0)),
        out_shape=S((R, C), F32),
        compiler_params=_cp("parallel"),
    )(parts, *after)


def _adamw(parts, w, m, v, name, after=()):
    P, R, C = parts.shape
    tr = _row_tile(R, C)
    c1 = 1.0 - ADAM_B1 ** ADAM_STEP
    c2 = 1.0 - ADAM_B2 ** ADAM_STEP

    def body(p_ref, w_ref, m_ref, v_ref, *rest):
        g_ref, d_ref, mo_ref, vo_ref = rest[len(after):]
        g = p_ref[0].astype(F32)
        for i in range(1, P):
            g = g + p_ref[i].astype(F32)
        mn = ADAM_B1 * m_ref[...] + (1.0 - ADAM_B1) * g
        vn = ADAM_B2 * v_ref[...] + (1.0 - ADAM_B2) * (g * g)
        g_ref[...] = g
        mo_ref[...] = mn
        vo_ref[...] = vn
        d_ref[...] = (-ADAM_LR) * ((mn / c1) / (jnp.sqrt(vn / c2) + ADAM_EPS) + ADAM_WD * w_ref[...])

    blk = pl.BlockSpec((tr, C), lambda i: (i, 0))
    return pl.pallas_call(
        body, name=name, grid=(R // tr,),
        in_specs=[pl.BlockSpec((P, tr, C), lambda i: (0, i, 0)), blk, blk, blk] + [ANY] * len(after),
        out_specs=[blk, blk, blk, blk],
        out_shape=[S((R, C), F32)] * 4,
        compiler_params=_cp("parallel"),
    )(parts, w, m, v, *after)


def _adamw_layer(parts, w3, m3, v3, layer, prev, name, after=()):
    P, R, C = parts.shape
    NL = w3.shape[0]
    tr = _row_tile(R, C)
    c1 = 1.0 - ADAM_B1 ** ADAM_STEP
    c2 = 1.0 - ADAM_B2 ** ADAM_STEP
    n_prev = 0 if prev is None else len(prev)

    def body(p_ref, w_ref, m_ref, v_ref, *rest):
        g_ref, d_ref, mo_ref, vo_ref = rest[n_prev + len(after):]
        g = p_ref[0].astype(F32)
        for i in range(1, P):
            g = g + p_ref[i].astype(F32)
        mn = ADAM_B1 * m_ref[...] + (1.0 - ADAM_B1) * g
        vn = ADAM_B2 * v_ref[...] + (1.0 - ADAM_B2) * (g * g)
        g_ref[...] = g
        mo_ref[...] = mn
        vo_ref[...] = vn
        d_ref[...] = (-ADAM_LR) * ((mn / c1) / (jnp.sqrt(vn / c2) + ADAM_EPS) + ADAM_WD * w_ref[...])

    blk = pl.BlockSpec((None, tr, C), lambda i: (layer, i, 0))
    return pl.pallas_call(
        body, name=name, grid=(R // tr,),
        in_specs=[pl.BlockSpec((P, tr, C), lambda i: (0, i, 0)), blk, blk, blk] + [ANY] * (n_prev + len(after)),
        out_specs=[blk, blk, blk, blk],
        out_shape=[S((NL, R, C), F32)] * 4,
        input_output_aliases={4 + k: k for k in range(n_prev)},
        compiler_params=_cp("parallel"),
    )(parts, w3, m3, v3, *(prev or ()), *after)


VMEM_WHOLE = pl.BlockSpec(memory_space=pltpu.VMEM)


def _pack_vectors(vectors, starts, rows, name):
    def body(*refs):
        o_ref = refs[-1]
        o_ref[...] = jnp.zeros_like(o_ref)
        for v_ref, r0 in zip(refs[:-1], starts):
            for j in range(v_ref.shape[1] // 128):
                o_ref[r0 + j:r0 + j + 1, :] = v_ref[:, j * 128:(j + 1) * 128]

    return pl.pallas_call(body, name=name, in_specs=[VMEM_WHOLE] * len(vectors), out_specs=VMEM_WHOLE,
                          out_shape=S((rows, 128), F32))(*vectors)


def _adamw_vectors(g_pack, params, keep_rows, name, after=()):
    n = len(params)
    P = g_pack.shape[0]
    c1 = 1.0 - ADAM_B1 ** ADAM_STEP
    c2 = 1.0 - ADAM_B2 ** ADAM_STEP

    def body(g_ref, *refs):
        ins, outs = refs[:3 * n], refs[3 * n + len(after):]
        gs = g_ref[0]
        for i in range(1, P):
            gs = gs + g_ref[i]
        for pi, (_, _, _, slots) in enumerate(params):
            w_ref, m_ref, v_ref = ins[3 * pi:3 * pi + 3]
            g_out, d_out, m_out, v_out = outs[4 * pi:4 * pi + 4]
            for idx, row in slots:
                g = gs[row:row + 1, :]
                mn = ADAM_B1 * m_ref[idx] + (1.0 - ADAM_B1) * g
                vn = ADAM_B2 * v_ref[idx] + (1.0 - ADAM_B2) * (g * g)
                g_out[idx] = g
                m_out[idx] = mn
                v_out[idx] = vn
                d_out[idx] = (-ADAM_LR) * ((mn / c1) / (jnp.sqrt(vn / c2) + ADAM_EPS) + ADAM_WD * w_ref[idx])
        outs[-1][...] = jnp.concatenate([gs[r:r + 1, :] for r in keep_rows], axis=0)

    flat = [a for w, m, v, _ in params for a in (w, m, v)]
    out_shape = [S(w.shape, F32) for w, _, _, _ in params for _ in range(4)] + [S((len(keep_rows), 128), F32)]
    outs = pl.pallas_call(
        body, name=name,
        in_specs=[VMEM_WHOLE] * (1 + len(flat)) + [ANY] * len(after),
        out_specs=[VMEM_WHOLE] * len(out_shape), out_shape=out_shape,
    )(g_pack, *flat, *after)
    return [outs[4 * i:4 * i + 4] for i in range(n)], outs[-1]


def _adamw_nd(parts, w, m, v, name, after=()):
    shp = w.shape
    C = shp[-1]
    outs = _adamw(parts.reshape(parts.shape[0], -1, C), w.reshape(-1, C), m.reshape(-1, C), v.reshape(-1, C), name, after)
    return [o.reshape(shp) for o in outs]


TILE_ROWS = 8


REP_SMALL_ROWS = 128
REP_GRAD_STARTS = (0, 8, 16, 24, 32, 40, 48, 56, 64)
REP_SMALL_STARTS = (0, 16, 32, 40, 48, 56)
REP_LOSS_ROW = 64


def _small_pack(cw, ba, bx, lam):
    pad8 = lambda a: jnp.pad(a, ((0, TILE_ROWS - a.shape[0]), (0, 0)))
    return jnp.concatenate([pad8(cw[0, :, 0, :]), pad8(ba[0]), pad8(bx[0]), pad8(lam[0]),
                            jnp.zeros((PV_ROWS - PV_CONV_B, LRU_BW), F32)], axis=0)


def kernel(x, norm_mix_g, norm_mlp_g, rg_w_in, rg_conv_w, rg_conv_b, rg_w_a, rg_b_a, rg_w_x, rg_b_x, rg_lam, rg_w_out, at_w_qkv, at_q_g, at_k_g, at_w_o, mlp_w_up, mlp_w_down, final_g, loss_target, m_norm_mix_g, m_norm_mlp_g, m_rg_w_in, m_rg_conv_w, m_rg_conv_b, m_rg_w_a, m_rg_b_a, m_rg_w_x, m_rg_b_x, m_rg_lam, m_rg_w_out, m_at_w_qkv, m_at_q_g, m_at_k_g, m_at_w_o, m_mlp_w_up, m_mlp_w_down, m_final_g, v_norm_mix_g, v_norm_mlp_g, v_rg_w_in, v_rg_conv_w, v_rg_conv_b, v_rg_w_a, v_rg_b_a, v_rg_w_x, v_rg_b_x, v_rg_lam, v_rg_w_out, v_at_w_qkv, v_at_q_g, v_at_k_g, v_at_w_o, v_mlp_w_up, v_mlp_w_down, v_final_g):
    D = x.shape[-1]
    bf = lambda a: a.astype(BF16)

    sp_w = _small_pack(rg_conv_w, rg_b_a, rg_b_x, rg_lam)
    started, _ = _exchange_start(
        [[bf(rg_w_in[0]), sp_w], [bf(rg_w_out[0])], [bf(mlp_w_up[0]), bf(mlp_w_down[0])],
         [bf(at_w_qkv[0]).T, bf(at_w_o[0])], [bf(mlp_w_up[1]), bf(mlp_w_down[1])]],
        ["near", "near", "near", "gather", "gather"], "gather_start")
    gathers = dict(zip((0, 1, 4, 2, 3), started))
    forwards = {}

    def fetch(stage, after):
        after = tuple(after)
        if stage == 0:
            got = _exchange_wait(gathers[0], after, "near", "gather_wait0")
            g_in, g_sp = _forward_wait(_forward_start([got], "forward_start0")[0], (), "forward_wait0")
            pvec = g_sp.transpose(1, 0, 2).reshape(PV_ROWS, D)
            pvec = jnp.concatenate([pvec[:PV_CONV_B], jnp.broadcast_to(rg_conv_b, (PV_ROWS - PV_CONV_B, D))], axis=0)
            return dict(w_in=g_in, pvec=pvec)
        if stage == 1:
            near = [_exchange_wait(gathers[s], after, "near", "gather_wait%d" % s) for s in (1, 4)]
            f_out, forwards[4] = _forward_start(near, "forward_start1")
            g_out, = _forward_wait(f_out, (), "forward_wait1")
            return dict(w_out=g_out.reshape(D, D))
        if stage == 4:
            g_up0, g_dn0 = _forward_wait(forwards[4], after, "forward_wait4")
            return dict(w_up0=g_up0, w_down0=g_dn0.reshape(-1, D))
        got = _exchange_wait(gathers[stage], after, "gather", "gather_wait%d" % stage)
        if stage == 2:
            return dict(w_qkv=got[0].reshape(-1, QKV_NB, D), w_o=got[1].reshape(D, D))
        return dict(w_up1=got[0], w_down1=got[1].reshape(-1, D))

    scatters = {}

    def send(stage, g):
        if stage == 3:
            arrs = [g["w_up1"], g["w_down1"].reshape(N_DEV, -1, D)]
        elif stage == 2:
            arrs = [g["w_qkv"].reshape(N_DEV, -1, D), g["w_o"].reshape(N_DEV, -1, D)]
        elif stage == 1:
            arrs = [g["w_up0"], g["w_down0"].reshape(N_DEV, -1, D)]
        elif stage == 4:
            arrs = [g["w_out"].reshape(N_DEV, -1, D), g["pvec"].reshape(PV_ROWS, N_DEV, LRU_BW).transpose(1, 0, 2),
                    bf(g["wa"]).reshape(N_DEV, -1, 128), bf(g["wx"]).reshape(N_DEV, -1, 128)]
        elif stage == 0:
            arrs = [g["w_in"]]
        else:
            small = _pack_vectors(g["g_mix"] + g["g_mlp"] + [g["g_fin"], g["conv_b"], g["qg"], g["kg"], g["loss"]],
                                  REP_GRAD_STARTS, REP_SMALL_ROWS, "pack_rep_small")
            arrs = [small.reshape(N_DEV, -1, 128)]
        (group,), token = _exchange_start([arrs], ["scatter"], "scatter_start%d" % (stage % 6))
        scatters[stage] = (group, token)
        return (token,)

    w = dict(g_mix=norm_mix_g, g_mlp=norm_mlp_g, g_fin=final_g[None], qg=at_q_g, kg=at_k_g,
             wa=bf(rg_w_a[0]), wx=bf(rg_w_x[0]))
    grad_x = _local_step(x, loss_target, w, fetch, send)

    res = {}
    r_up1, r_dn1 = _exchange_wait(scatters[3][0], (scatters[-1][1],), "scatter", "scatter_wait3")
    r_out, r_sp, r_wa, r_wx = _exchange_wait(scatters[4][0], (r_up1,), "scatter", "scatter_wait4")
    wa_part = _sum_parts(r_wa, "reduce_w_a")
    wx_part = _sum_parts(r_wx, "reduce_w_x", (wa_part,))
    (rep_gather,), rep_token = _exchange_start([[wa_part, wx_part]], ["gather"], "rep_gather_start")
    up = _adamw_layer(r_up1, mlp_w_up, m_mlp_w_up, v_mlp_w_up, 1, None, "adam_mlp_w_up1", after=(rep_token,))
    dn = _adamw_layer(r_dn1, mlp_w_down, m_mlp_w_down, v_mlp_w_down, 1, None, "adam_mlp_w_down1", after=(up[1],))
    r_qkv, r_o = _exchange_wait(scatters[2][0], (dn[1],), "scatter", "scatter_wait2")
    tr = lambda a: a[0].T
    qkv_t = _adamw_nd(r_qkv, tr(at_w_qkv), tr(m_at_w_qkv), tr(v_at_w_qkv), "adam_at_w_qkv")
    res["at_w_qkv"] = [o.T[None] for o in qkv_t]
    res["at_w_o"] = _adamw_nd(r_o[:, None], at_w_o, m_at_w_o, v_at_w_o, "adam_at_w_o", (qkv_t[1],))
    r_up0, r_dn0 = _exchange_wait(scatters[1][0], (res["at_w_o"][1],), "scatter", "scatter_wait1")
    res["mlp_w_up"] = _adamw_layer(r_up0, mlp_w_up, m_mlp_w_up, v_mlp_w_up, 0, up, "adam_mlp_w_up0")
    res["mlp_w_down"] = _adamw_layer(r_dn0, mlp_w_down, m_mlp_w_down, v_mlp_w_down, 0, dn, "adam_mlp_w_down0",
                                     after=(res["mlp_w_up"][1],))
    r_in, = _exchange_wait(scatters[0][0], (res["mlp_w_down"][1],), "scatter", "scatter_wait0")
    res["rg_w_in"] = _adamw_nd(r_in[:, None], rg_w_in, m_rg_w_in, v_rg_w_in, "adam_rg_w_in")
    res["rg_w_out"] = _adamw_nd(r_out[:, None], rg_w_out, m_rg_w_out, v_rg_w_out, "adam_rg_w_out", (res["rg_w_in"][1],))
    whole, lane = slice(None), slice(0, 1)
    two_rows = lambda r0: [((0, slice(d, d + 1), whole), r0 + d) for d in range(2)]
    (res["rg_conv_w"], res["rg_b_a"], res["rg_b_x"], res["rg_lam"]), _ = _adamw_vectors(
        r_sp, [(rg_conv_w, m_rg_conv_w, v_rg_conv_w, [((0, t, lane, whole), PV_CONV_W + t) for t in range(CONV_W)]),
               (rg_b_a, m_rg_b_a, v_rg_b_a, two_rows(PV_B_A)), (rg_b_x, m_rg_b_x, v_rg_b_x, two_rows(PV_B_X)),
               (rg_lam, m_rg_lam, v_rg_lam, two_rows(PV_LAM))], [0], "adam_small", (res["rg_w_out"][1],))

    r_small, = _exchange_wait(scatters[-1][0], (res["rg_lam"][1],), "scatter", "scatter_wait5")
    small_sum, = _all_gather([_sum_parts(r_small, "reduce_rep_small")], "gather_replicated")
    wa_sum, wx_sum = _exchange_wait(rep_gather, (small_sum,), "gather", "rep_gather_wait")
    rows = lambda a: a.reshape(-1, 128)
    wa_res = _adamw(wa_sum.reshape(1, -1, 128), rows(rg_w_a), rows(m_rg_w_a), rows(v_rg_w_a), "adam_rg_w_a")
    wx_res = _adamw(wx_sum.reshape(1, -1, 128), rows(rg_w_x), rows(m_rg_w_x), rows(v_rg_w_x), "adam_rg_w_x", (wa_res[1],))
    res["rg_w_a"] = [o.reshape(rg_w_a.shape) for o in wa_res]
    res["rg_w_x"] = [o.reshape(rg_w_x.shape) for o in wx_res]
    def vec_slots(a, r0):
        per = a.shape[1] // 128
        return [((slice(l, l + 1), slice(128 * j, 128 * (j + 1))), r0 + l * per + j) for l in range(a.shape[0]) for j in range(per)]

    fin = [final_g[None], m_final_g[None], v_final_g[None]]
    vecs = [(norm_mix_g, m_norm_mix_g, v_norm_mix_g), (norm_mlp_g, m_norm_mlp_g, v_norm_mlp_g), fin,
            (rg_conv_b, m_rg_conv_b, v_rg_conv_b), (at_q_g, m_at_q_g, v_at_q_g), (at_k_g, m_at_k_g, v_at_k_g)]
    outs, kept = _adamw_vectors(
        small_sum.reshape(1, -1, 128),
        [(wv, mv, vv, vec_slots(wv, r0)) for (wv, mv, vv), r0 in zip(vecs, REP_SMALL_STARTS)], [REP_LOSS_ROW], "adam_rep_small", (wx_res[1],))
    for nm, o in zip(["norm_mix_g", "norm_mlp_g", "final_g", "rg_conv_b", "at_q_g", "at_k_g"], outs):
        res[nm] = [a[0] for a in o] if nm == "final_g" else o
    loss = kept[0, 0]

    order = ["norm_mix_g", "norm_mlp_g", "rg_w_in", "rg_conv_w", "rg_conv_b", "rg_w_a", "rg_b_a", "rg_w_x", "rg_b_x",
             "rg_lam", "rg_w_out", "at_w_qkv", "at_q_g", "at_k_g", "at_w_o", "mlp_w_up", "mlp_w_down", "final_g"]
    return (loss, grad_x, *[res[nm][k] for k in range(4) for nm in order])
```

```python
import math

import jax
import jax.numpy as jnp
from jax import lax
from jax.experimental import pallas as pl
from jax.experimental.pallas import tpu as pltpu

F32 = jnp.float32
BF16 = jnp.bfloat16
S = jax.ShapeDtypeStruct

EPS = 1e-6
HEAD_DIM = 128
N_KV = 2
GRID_W = 64
ROPE_THETA = 10000.0
LRU_BW = 128
RG_C = 8.0
CONV_W = 4
N_DEV = 8
N_SEG = 8
SCAN_UNROLL = 8
TN_STEP_COLS = 512
PROJ_TM = 1024
PROJ_CHAINS = 2
MLP_TM = 512
VMEM_LIMIT_V7X = 56 * 1024 * 1024
SOFTMAX_SCALE = 1.0 / math.sqrt(HEAD_DIM)
GELU_K = math.sqrt(2.0 / math.pi)
GELU_C = 0.044715

ADAM_LR = 0.001
ADAM_B1 = 0.9
ADAM_B2 = 0.999
ADAM_EPS = 1e-08
ADAM_WD = 0.01
ADAM_STEP = 10

NT = (((1,), (1,)), ((), ()))
TN = (((0,), (0,)), ((), ()))


def _cp(*sem):
    return pltpu.CompilerParams(dimension_semantics=sem, vmem_limit_bytes=VMEM_LIMIT_V7X)


def _rms_r(xv):
    return lax.rsqrt(jnp.mean(xv * xv, axis=-1, keepdims=True) + EPS)


def _rms_bwd(dh, xv, g):
    r = _rms_r(xv)
    xh = xv * r
    dg = jnp.sum(dh * xh, axis=0, keepdims=True)
    dxh = dh * g
    dx = r * (dxh - xh * jnp.mean(dxh * xh, axis=-1, keepdims=True))
    return dx, dg


def _dot(a, b):
    return jnp.dot(a, b, preferred_element_type=F32)


def _dot_nt(a, b):
    return lax.dot_general(a, b, NT, preferred_element_type=F32)


def _dot_tn(a, b):
    return lax.dot_general(a, b, TN, preferred_element_type=F32)


def _norm_matmul(x, g, wblk, name, out_dtype=F32):
    T, D = x.shape
    NB, _, nb = wblk.shape
    tm = min(T, PROJ_TM)

    def body(x_ref, g_ref, w_ref, o_ref, h_ref):
        for c in range(PROJ_CHAINS):
            rows = slice(c * tm // PROJ_CHAINS, (c + 1) * tm // PROJ_CHAINS)
            xv = x_ref[rows, :]
            hb = (xv * _rms_r(xv) * g_ref[...]).astype(BF16)
            h_ref[rows, :] = hb
            for q in range(NB):
                o_ref[rows, q * nb:(q + 1) * nb] = _dot(hb, w_ref[q]).astype(o_ref.dtype)

    return pl.pallas_call(
        body, name=name, grid=(T // tm,),
        in_specs=[pl.BlockSpec((tm, D), lambda i: (i, 0)),
                  pl.BlockSpec((1, D), lambda i: (0, 0)),
                  pl.BlockSpec(wblk.shape, lambda i: (0, 0, 0))],
        out_specs=[pl.BlockSpec((tm, NB * nb), lambda i: (i, 0)),
                   pl.BlockSpec((tm, D), lambda i: (i, 0))],
        out_shape=[S((T, NB * nb), out_dtype), S((T, D), BF16)],
        compiler_params=_cp("parallel"),
    )(x, g, wblk)


def _matmul_res(a, w, res, name):
    T, K = a.shape
    N = w.shape[1]
    tm = min(T, PROJ_TM)

    def body(a_ref, w_ref, r_ref, o_ref):
        o_ref[...] = r_ref[...] + _dot(a_ref[...], w_ref[...])

    return pl.pallas_call(
        body, name=name, grid=(T // tm,),
        in_specs=[pl.BlockSpec((tm, K), lambda i: (i, 0)),
                  pl.BlockSpec((K, N), lambda i: (0, 0)),
                  pl.BlockSpec((tm, N), lambda i: (i, 0))],
        out_specs=pl.BlockSpec((tm, N), lambda i: (i, 0)),
        out_shape=S((T, N), F32),
        compiler_params=_cp("parallel"),
    )(a, w, res)


def _matmul_nt(a, w, name, out_dtype, after=()):
    T, N = a.shape
    K = w.shape[0]
    tm = min(T, PROJ_TM)

    def body(a_ref, w_ref, *rest):
        o_ref, ab_ref = rest[len(after):]
        ab = a_ref[...].astype(BF16)
        ab_ref[...] = ab
        o_ref[...] = _dot_nt(ab, w_ref[...]).astype(o_ref.dtype)

    return pl.pallas_call(
        body, name=name, grid=(T // tm,),
        in_specs=[pl.BlockSpec((tm, N), lambda i: (i, 0)),
                  pl.BlockSpec((K, N), lambda i: (0, 0))] + [pl.BlockSpec(memory_space=pl.ANY)] * len(after),
        out_specs=[pl.BlockSpec((tm, K), lambda i: (i, 0)),
                   pl.BlockSpec((tm, N), lambda i: (i, 0))],
        out_shape=[S((T, K), out_dtype), S((T, N), BF16)],
        compiler_params=_cp("parallel"),
    )(a, w, *after)


def _matmul_tn(a, b3, nb, name, blocked, after=()):
    T, M = a.shape
    SB, _, N = b3.shape
    per = N // nb
    NB = SB * per
    tk = min(T, 1024)
    nk = T // tk
    jb = max(1, TN_STEP_COLS // nb) if blocked else 1
    assert per % jb == 0
    if blocked:
        out_spec, out_shape = pl.BlockSpec((jb, M, nb), lambda j: (j, 0, 0)), S((NB, M, nb), BF16)
    else:
        assert SB == 1
        out_spec, out_shape = pl.BlockSpec((M, nb), lambda j: (0, j)), S((M, N), BF16)

    def body(a_ref, b_ref, *rest):
        o_ref = rest[-1]
        for q in range(jb):
            acc = None
            for k in range(nk):
                rows = slice(k * tk, (k + 1) * tk)
                part = _dot_tn(a_ref[rows, :], b_ref[rows, q * nb:(q + 1) * nb])
                acc = part if acc is None else acc + part
            if blocked:
                o_ref[q] = acc.astype(BF16)
            else:
                o_ref[...] = acc.astype(BF16)

    return pl.pallas_call(
        body, name=name, grid=(NB // jb,),
        in_specs=[pl.BlockSpec((T, M), lambda j: (0, 0), pipeline_mode=pl.Buffered(1)),
                  pl.BlockSpec((None, T, jb * nb), lambda j: ((j * jb) // per, 0, ((j * jb) % per) // jb))]
        + [pl.BlockSpec(memory_space=pl.ANY)] * len(after),
        out_specs=out_spec,
        out_shape=out_shape,
        compiler_params=_cp("parallel"),
    )(a, b3, *after)


def _nt_normbwd(dz3, wblk, x, g, dres, name, after=(), transposed=False):
    T, D = x.shape
    NB, nb = wblk.shape[0], wblk.shape[1 if transposed else 2]
    mm = _dot if transposed else _dot_nt
    SB, _, N = dz3.shape
    per = N // nb
    tm = min(T, PROJ_TM)

    def body(dz_ref, w_ref, x_ref, g_ref, dr_ref, *rest):
        dx_ref, dg_ref = rest[len(after):]

        @pl.when(pl.program_id(0) == 0)
        def _():
            dg_ref[...] = jnp.zeros_like(dg_ref)

        for c in range(PROJ_CHAINS):
            rows = slice(c * tm // PROJ_CHAINS, (c + 1) * tm // PROJ_CHAINS)
            dh = None
            for q in range(NB):
                cols = slice((q % per) * nb, (q % per + 1) * nb)
                part = mm(dz_ref[q // per, rows, cols], w_ref[q])
                dh = part if dh is None else dh + part
            dx, dg = _rms_bwd(dh, x_ref[rows, :], g_ref[...])
            dx_ref[rows, :] = dr_ref[rows, :] + dx
            dg_ref[...] += dg

    return pl.pallas_call(
        body, name=name, grid=(T // tm,),
        in_specs=[pl.BlockSpec((SB, tm, N), lambda i: (0, i, 0)),
                  pl.BlockSpec(wblk.shape, lambda i: (0, 0, 0)),
                  pl.BlockSpec((tm, D), lambda i: (i, 0)),
                  pl.BlockSpec((1, D), lambda i: (0, 0)),
                  pl.BlockSpec((tm, D), lambda i: (i, 0))] + [pl.BlockSpec(memory_space=pl.ANY)] * len(after),
        out_specs=[pl.BlockSpec((tm, D), lambda i: (i, 0)),
                   pl.BlockSpec((1, D), lambda i: (0, 0))],
        out_shape=[S((T, D), F32), S((1, D), F32)],
        compiler_params=_cp("arbitrary"),
    )(dz3, wblk, x, g, dres, *after)


def _loss_head(xv, tv, gv, D):
    err = xv * _rms_r(xv) * gv - tv
    e2 = jnp.sum(jnp.sum(err * err, axis=-1, keepdims=True), axis=0, keepdims=True)
    dx, dg = _rms_bwd(err * (1.0 / D), xv, gv)
    return (0.5 / D) * e2, dx, dg


def _mlp_fwd(x, g, wup, wdown, name, head=None):
    T, D = x.shape
    NB, _, fb = wup.shape
    tm = min(T, MLP_TM)
    n_head = 0 if head is None else 2

    def body(x_ref, g_ref, wu_ref, wd_ref, *rest):
        xo_ref, a_ref, h_ref = rest[n_head:n_head + 3]
        xv = x_ref[...]
        hb = (xv * _rms_r(xv) * g_ref[...]).astype(BF16)
        h_ref[...] = hb
        acc = xv
        for j in range(NB):
            a = _dot(hb, wu_ref[j])
            a_ref[:, j * fb:(j + 1) * fb] = a.astype(BF16)
            u = jnp.maximum(a, 0.0)
            acc = acc + _dot((u * u).astype(BF16), wd_ref[j * fb:(j + 1) * fb, :])

        if head is None:
            xo_ref[...] = acc
        else:
            t_ref, gf_ref = rest[:2]
            loss_ref, dgf_ref = rest[n_head + 3:n_head + 5]

            @pl.when(pl.program_id(0) == 0)
            def _():
                loss_ref[...] = jnp.zeros_like(loss_ref)
                dgf_ref[...] = jnp.zeros_like(dgf_ref)

            e2, dx, dg = _loss_head(acc, t_ref[...], gf_ref[...], D)
            xo_ref[...] = dx
            loss_ref[...] += e2
            dgf_ref[...] += dg

    row = pl.BlockSpec((tm, D), lambda i: (i, 0))
    vec = pl.BlockSpec((1, D), lambda i: (0, 0))
    once = pl.Buffered(1)
    in_specs = [row, vec, pl.BlockSpec((NB, D, fb), lambda i: (0, 0, 0), pipeline_mode=once),
                pl.BlockSpec((NB * fb, D), lambda i: (0, 0), pipeline_mode=once)]
    out_specs = [row, pl.BlockSpec((tm, NB * fb), lambda i: (i, 0)), row]
    out_shape = [S((T, D), F32), S((T, NB * fb), BF16), S((T, D), BF16)]
    if head is not None:
        in_specs += [row, vec]
        out_specs += [pl.BlockSpec((1, 128), lambda i: (0, 0)), vec]
        out_shape += [S((1, 128), F32), S((1, D), F32)]
    return pl.pallas_call(
        body, name=name, grid=(T // tm,),
        in_specs=in_specs, out_specs=out_specs, out_shape=out_shape,
        compiler_params=_cp("parallel" if head is None else "arbitrary"),
    )(x, g, wup, wdown, *(head or ()))


def _mlp_bwd_dx(x, dout, a, g, wup, wdown, name):
    T, D = x.shape
    NB, _, fb = wup.shape
    tm = min(T, MLP_TM)

    def body(x_ref, do_ref, a_ref, g_ref, wu_ref, wd_ref, dx_ref, da_ref, dob_ref, dg_ref):
        @pl.when(pl.program_id(0) == 0)
        def _():
            dg_ref[...] = jnp.zeros_like(dg_ref)

        dov = do_ref[...]
        dob = dov.astype(BF16)
        dob_ref[...] = dob
        dh = None
        for j in range(NB):
            cols = slice(j * fb, (j + 1) * fb)
            du2 = _dot_nt(dob, wd_ref[cols, :])
            u = jnp.maximum(a_ref[:, cols].astype(F32), 0.0)
            da = (du2 * (2.0 * u)).astype(BF16)
            da_ref[:, cols] = da
            part = _dot_nt(da, wu_ref[j])
            dh = part if dh is None else dh + part
        dx, dg = _rms_bwd(dh, x_ref[...], g_ref[...])
        dx_ref[...] = dov + dx
        dg_ref[...] += dg

    row = pl.BlockSpec((tm, D), lambda i: (i, 0))
    wide = pl.BlockSpec((tm, NB * fb), lambda i: (i, 0))
    vec = pl.BlockSpec((1, D), lambda i: (0, 0))
    once = pl.Buffered(1)
    return pl.pallas_call(
        body, name=name, grid=(T // tm,),
        in_specs=[row, row, wide, vec, pl.BlockSpec((NB, D, fb), lambda i: (0, 0, 0), pipeline_mode=once),
                  pl.BlockSpec((NB * fb, D), lambda i: (0, 0), pipeline_mode=once)],
        out_specs=[row, wide, row, vec],
        out_shape=[S((T, D), F32), S((T, NB * fb), BF16), S((T, D), BF16), S((1, D), F32)],
        compiler_params=_cp("arbitrary"),
    )(x, dout, a, g, wup, wdown)


def _mlp_bwd_dw(h, da, a, dob, fb, name):
    T, D = h.shape
    F = a.shape[1]
    NB = F // fb
    tk = min(T, 1024)
    nk = T // tk

    def body(h_ref, da_ref, a_ref, dob_ref, dwu_ref, dwd_ref):
        au = ad = None
        for k in range(nk):
            rows = slice(k * tk, (k + 1) * tk)
            pu = _dot_tn(h_ref[rows, :], da_ref[rows, :])
            u = jnp.maximum(a_ref[rows, :].astype(F32), 0.0)
            pd = _dot_tn((u * u).astype(BF16), dob_ref[rows, :])
            au, ad = (pu, pd) if au is None else (au + pu, ad + pd)
        dwu_ref[...] = au.astype(BF16)
        dwd_ref[...] = ad.astype(BF16)

    once = pl.Buffered(1)
    return pl.pallas_call(
        body, name=name, grid=(NB,),
        in_specs=[pl.BlockSpec((T, D), lambda j: (0, 0), pipeline_mode=once),
                  pl.BlockSpec((T, fb), lambda j: (0, j)),
                  pl.BlockSpec((T, fb), lambda j: (0, j)),
                  pl.BlockSpec((T, D), lambda j: (0, 0), pipeline_mode=once)],
        out_specs=[pl.BlockSpec((None, D, fb), lambda j: (j, 0, 0)),
                   pl.BlockSpec((fb, D), lambda j: (j, 0))],
        out_shape=[S((NB, D, fb), BF16), S((F, D), BF16)],
        compiler_params=_cp("parallel"),
    )(h, da, a, dob)


def _rope_tables(L):
    nf = HEAD_DIM // 4
    t = jnp.arange(L, dtype=jnp.int32)
    row = (t // GRID_W).astype(F32)
    col = (t % GRID_W).astype(F32)
    inv = ROPE_THETA ** (-jnp.arange(nf, dtype=F32) / nf)
    ar = row[:, None] * inv
    ac = col[:, None] * inv
    cos = jnp.concatenate([jnp.cos(ar), jnp.cos(ar), jnp.cos(ac), jnp.cos(ac)], axis=-1)
    sin = jnp.concatenate([-jnp.sin(ar), jnp.sin(ar), -jnp.sin(ac), jnp.sin(ac)], axis=-1)
    return cos, sin


def _swap32(x):
    lane = lax.broadcasted_iota(jnp.int32, x.shape, 1)
    up = pltpu.roll(x, HEAD_DIM - 32, 1)
    down = pltpu.roll(x, 32, 1)
    return jnp.where((lane % 64) < 32, up, down)


def _qkv_proj(x, g, wt_blk, qg, kg, cos, sin, L, name):
    T, D = x.shape
    NB, nb, _ = wt_blk.shape
    W = NB * nb
    nh = W // HEAD_DIM - 2 * N_KV
    tm = min(L, 512)
    lb = L // tm

    def body(x_ref, g_ref, w_ref, qg_ref, kg_ref, cos_ref, sin_ref, qkv_ref, h_ref, q_ref, k_ref, v_ref):
        for ch in range(PROJ_CHAINS):
            rows = slice(ch * tm // PROJ_CHAINS, (ch + 1) * tm // PROJ_CHAINS)
            xv = x_ref[rows, :]
            hb = (xv * _rms_r(xv) * g_ref[...]).astype(BF16)
            h_ref[rows, :] = hb
            for j in range(NB):
                qkv_ref[rows, j * nb:(j + 1) * nb] = _dot_nt(hb, w_ref[j])
            c = cos_ref[rows, :]
            s = sin_ref[rows, :]
            for h in range(nh + N_KV):
                xh = qkv_ref[rows, h * HEAD_DIM:(h + 1) * HEAD_DIM]
                gv = qg_ref[...] if h < nh else kg_ref[...]
                y = xh * _rms_r(xh) * gv
                y = (y * c + _swap32(y) * s).astype(BF16)
                if h < nh:
                    q_ref[rows, h * HEAD_DIM:(h + 1) * HEAD_DIM] = y
                else:
                    k_ref[rows, (h - nh) * HEAD_DIM:(h - nh + 1) * HEAD_DIM] = y
            v_ref[rows, :] = qkv_ref[rows, (nh + N_KV) * HEAD_DIM:].astype(BF16)

    row = lambda cols: pl.BlockSpec((tm, cols), lambda i: (i, 0))
    vec = lambda cols: pl.BlockSpec((1, cols), lambda i: (0, 0))
    table = pl.BlockSpec((tm, HEAD_DIM), lambda i: (i % lb, 0))
    return pl.pallas_call(
        body, name=name, grid=(T // tm,),
        in_specs=[row(D), vec(D), pl.BlockSpec(wt_blk.shape, lambda i: (0, 0, 0)), vec(HEAD_DIM), vec(HEAD_DIM), table, table],
        out_specs=[row(W), row(D), row(nh * HEAD_DIM), row(N_KV * HEAD_DIM), row(N_KV * HEAD_DIM)],
        out_shape=[S((T, W), F32), S((T, D), BF16), S((T, nh * HEAD_DIM), BF16), S((T, N_KV * HEAD_DIM), BF16),
                   S((T, N_KV * HEAD_DIM), BF16)],
        compiler_params=_cp("parallel"),
    )(x, g, wt_blk, qg, kg, cos, sin)


def _qk_prep_bwd(qkv, dq, dk, dv, qg, kg, cos, sin, L, name):
    T, W = qkv.shape
    nh = W // HEAD_DIM - 2 * N_KV
    tm = min(L, 512)
    lb = L // tm

    def body(qkv_ref, dq_ref, dk_ref, dv_ref, qg_ref, kg_ref, cos_ref, sin_ref, dz_ref, dqg_ref, dkg_ref):
        @pl.when(pl.program_id(0) == 0)
        def _():
            dqg_ref[...] = jnp.zeros_like(dqg_ref)
            dkg_ref[...] = jnp.zeros_like(dkg_ref)

        c = cos_ref[...]
        s = sin_ref[...]
        for h in range(nh + N_KV):
            cols = slice(h * HEAD_DIM, (h + 1) * HEAD_DIM)
            if h < nh:
                dout, gv, dg_ref = dq_ref[:, cols], qg_ref[...], dqg_ref
            else:
                kc = slice((h - nh) * HEAD_DIM, (h - nh + 1) * HEAD_DIM)
                dout, gv, dg_ref = dk_ref[:, kc], kg_ref[...], dkg_ref
            dy = dout * c - _swap32(dout) * s
            dx, dg = _rms_bwd(dy, qkv_ref[:, cols], gv)
            dg_ref[...] += dg
            dz_ref[:, cols] = dx.astype(BF16)
        dz_ref[:, (nh + N_KV) * HEAD_DIM:] = dv_ref[...].astype(BF16)

    return pl.pallas_call(
        body, name=name, grid=(T // tm,),
        in_specs=[pl.BlockSpec((tm, W), lambda i: (i, 0)),
                  pl.BlockSpec((tm, nh * HEAD_DIM), lambda i: (i, 0)),
                  pl.BlockSpec((tm, N_KV * HEAD_DIM), lambda i: (i, 0)),
                  pl.BlockSpec((tm, N_KV * HEAD_DIM), lambda i: (i, 0)),
                  pl.BlockSpec((1, HEAD_DIM), lambda i: (0, 0)),
                  pl.BlockSpec((1, HEAD_DIM), lambda i: (0, 0)),
                  pl.BlockSpec((tm, HEAD_DIM), lambda i: (i % lb, 0)),
                  pl.BlockSpec((tm, HEAD_DIM), lambda i: (i % lb, 0))],
        out_specs=[pl.BlockSpec((tm, W), lambda i: (i, 0)),
                   pl.BlockSpec((1, HEAD_DIM), lambda i: (0, 0)),
                   pl.BlockSpec((1, HEAD_DIM), lambda i: (0, 0))],
        out_shape=[S((T, W), BF16), S((1, HEAD_DIM), F32), S((1, HEAD_DIM), F32)],
        compiler_params=_cp("arbitrary"),
    )(qkv, dq, dk, dv, qg, kg, cos, sin)


EXP2_SCALE = SOFTMAX_SCALE * math.log2(math.e)
ATTN_SUB = 256
ATTN_TQ = 1024
ATTN_BWD_TQ = 2048


def _softmax_rows(q, k):
    s = _dot_nt(q, k)
    e = jnp.exp2((s - jnp.max(s, axis=-1, keepdims=True)) * EXP2_SCALE)
    return e, jnp.sum(e, axis=-1, keepdims=True)


def _attn_fwd(q, k, v, L, name):
    T = q.shape[0]
    nh = q.shape[1] // HEAD_DIM
    G = nh // N_KV
    B = T // L
    tq = min(L, ATTN_TQ)
    nq = L // tq
    sub = min(tq, ATTN_SUB)

    def body(q_ref, k_ref, v_ref, o_ref):
        for h in range(tq // sub):
            rows = slice(h * sub, (h + 1) * sub)
            e, l = _softmax_rows(q_ref[rows, :], k_ref[...])
            o_ref[rows, :] = (_dot(e.astype(BF16), v_ref[...]) / l).astype(BF16)

    qspec = pl.BlockSpec((tq, HEAD_DIM), lambda b, kv, g, qi: (b * nq + qi, kv * G + g))
    kspec = pl.BlockSpec((L, HEAD_DIM), lambda b, kv, g, qi: (b, kv))
    return pl.pallas_call(
        body, name=name, grid=(B, N_KV, G, nq),
        in_specs=[qspec, kspec, kspec],
        out_specs=qspec,
        out_shape=S((T, nh * HEAD_DIM), BF16),
        compiler_params=_cp("parallel", "parallel", "parallel", "parallel"),
    )(q, k, v)


def _attn_bwd(q, k, v, do, o, L, name):
    T = q.shape[0]
    nh = q.shape[1] // HEAD_DIM
    G = nh // N_KV
    B = T // L
    tq = min(L, ATTN_BWD_TQ)
    nq = L // tq
    sub = min(tq, ATTN_SUB)

    def body(q_ref, k_ref, v_ref, do_ref, o_ref, dq_ref, dk_ref, dv_ref, ds_scr, p_scr):
        first = (pl.program_id(2) == 0) & (pl.program_id(3) == 0)
        last = (pl.program_id(2) == G - 1) & (pl.program_id(3) == nq - 1)

        @pl.when(first)
        def _():
            dk_ref[...] = jnp.zeros_like(dk_ref)
            dv_ref[...] = jnp.zeros_like(dv_ref)

        for h in range(tq // sub):
            rows = slice(h * sub, (h + 1) * sub)
            dov = do_ref[rows, :]
            e, l = _softmax_rows(q_ref[rows, :], k_ref[...])
            p = e * (1.0 / l)
            dsum = jnp.sum(dov.astype(F32) * o_ref[rows, :].astype(F32), axis=-1, keepdims=True)
            ds_scr[rows, :] = (p * (_dot_nt(dov, v_ref[...]) - dsum)).astype(BF16)
            p_scr[rows, :] = p.astype(BF16)
        ds = ds_scr[...]
        dq_ref[...] = _dot(ds, k_ref[...]) * SOFTMAX_SCALE
        dk_ref[...] += _dot_tn(ds, q_ref[...])
        dv_ref[...] += _dot_tn(p_scr[...], do_ref[...])

        @pl.when(last)
        def _():
            dk_ref[...] = dk_ref[...] * SOFTMAX_SCALE

    qspec = pl.BlockSpec((tq, HEAD_DIM), lambda b, kv, g, qi: (b * nq + qi, kv * G + g))
    kspec = pl.BlockSpec((L, HEAD_DIM), lambda b, kv, g, qi: (b, kv))
    return pl.pallas_call(
        body, name=name, grid=(B, N_KV, G, nq),
        in_specs=[qspec, kspec, kspec, qspec, qspec],
        out_specs=[qspec, kspec, kspec],
        out_shape=[S((T, nh * HEAD_DIM), F32), S((T, N_KV * HEAD_DIM), F32), S((T, N_KV * HEAD_DIM), F32)],
        scratch_shapes=[pltpu.VMEM((tq, L), BF16), pltpu.VMEM((tq, L), BF16)],
        compiler_params=_cp("parallel", "parallel", "arbitrary", "arbitrary"),
    )(q, k, v, do, o)


PV_CONV_W = 0
PV_B_A = 8
PV_B_X = 16
PV_LAM = 24
PV_CONV_B = 32
PV_ROWS = 40


def _shift_rows(x, k):
    if k == 0:
        return x
    L = x.shape[0]
    n = N_SEG * abs(k)
    seg = lax.broadcasted_iota(jnp.int32, (n, x.shape[1]), 0) % N_SEG
    if k > 0:
        edge = jnp.where(seg == 0, 0.0, pltpu.roll(x[L - n:], 1, 0))
        return jnp.concatenate([edge, x[:L - n]], axis=0)
    edge = jnp.where(seg == N_SEG - 1, 0.0, pltpu.roll(x[:n], n - 1, 0))
    return jnp.concatenate([x[n:], edge], axis=0)


def _conv_taps(rec, pv):
    c = pv[PV_CONV_B:PV_CONV_B + 1]
    for j in range(CONV_W):
        c = c + pv[PV_CONV_W + j:PV_CONV_W + j + 1] * _shift_rows(rec, 2 - j)
    return c


def _sigmoid(x):
    return 0.5 * jnp.tanh(0.5 * x) + 0.5


EXPM1_SERIES_BELOW = 0.03


def _rg_gates(c, cbf, wa, wx, ba, bx, lam):
    r = _sigmoid(_dot(cbf, wa) + ba)
    i = _sigmoid(_dot(cbf, wx) + bx)
    sp = jnp.maximum(-lam, 0.0) + jnp.log1p(jnp.exp(-jnp.abs(lam)))
    la = r * ((-RG_C) * sp)
    a = jnp.exp(la)
    a2 = a * a
    x = la + la
    series = -(x * ((x * (1.0 / 6.0) + 0.5) * x + 1.0))
    om = jnp.where(x > -EXPM1_SERIES_BELOW, series, 1.0 - a2)
    rm = lax.rsqrt(om)
    return r, i, a, om * rm, rm, a2, sp


def _gelu(x):
    t = jnp.tanh(GELU_K * (x + GELU_C * x * x * x))
    return 0.5 * x * (1.0 + t), t


def _scan_pair(af_ref, uf_ref, ab_ref, ub_ref, hf_ref, hb_ref, pf_ref, pb_ref, L):
    ls = L // N_SEG
    zero = jnp.zeros((N_SEG, LRU_BW), F32)
    one = jnp.ones((N_SEG, LRU_BW), F32)
    tile = lambda t: pl.ds(pl.multiple_of(t * N_SEG, N_SEG), N_SEG)

    def steps(tc, carry):
        hf, pf, hb, pb = carry
        for q in range(SCAN_UNROLL):
            t = tc * SCAN_UNROLL + q
            rf, rb = tile(t), tile(ls - 1 - t)
            af = af_ref[rf, :]
            hf = af * hf + uf_ref[rf, :]
            pf = pf * af
            hf_ref[rf, :] = hf
            pf_ref[rf, :] = pf
            ab = ab_ref[rb, :]
            hb = ab * hb + ub_ref[rb, :]
            pb = pb * ab
            hb_ref[rb, :] = hb
            pb_ref[rb, :] = pb
        return hf, pf, hb, pb

    hf_e, pf_e, hb_e, pb_e = lax.fori_loop(0, ls // SCAN_UNROLL, steps, (zero, one, zero, one))

    rows, cin = [], jnp.zeros((1, LRU_BW), F32)
    for s in range(N_SEG):
        rows.append(cin)
        cin = hf_e[s:s + 1] + pf_e[s:s + 1] * cin
    cf = jnp.concatenate(rows, axis=0)
    rows, cin = [], jnp.zeros((1, LRU_BW), F32)
    for s in reversed(range(N_SEG)):
        rows.append(cin)
        cin = hb_e[s:s + 1] + pb_e[s:s + 1] * cin
    cb = jnp.concatenate(rows[::-1], axis=0)

    def fix(tc, _):
        for q in range(SCAN_UNROLL):
            r = tile(tc * SCAN_UNROLL + q)
            hf_ref[r, :] = hf_ref[r, :] + pf_ref[r, :] * cf
            hb_ref[r, :] = hb_ref[r, :] + pb_ref[r, :] * cb
        return 0

    lax.fori_loop(0, ls // SCAN_UNROLL, fix, 0)


def _rg_specs(L, D, nblk):
    slab = lambda off: pl.BlockSpec((L, LRU_BW), lambda cb, b: (b, off + cb))
    wspec = pl.BlockSpec((2, None, LRU_BW, LRU_BW), lambda cb, b: (0, cb, 0, 0))
    pvspec = pl.BlockSpec((PV_ROWS, LRU_BW), lambda cb, b: (0, cb))
    return slab, wspec, pvspec


def _rg_fwd(z, pvec, wa, wx, L, name):
    T, C2 = z.shape
    C = C2 // 2
    nblk = C // LRU_BW
    B = T // L
    slab, wspec, pvspec = _rg_specs(L, C, nblk)

    def body(gp_ref, rec_ref, pv_ref, wa_ref, wx_ref, yg_ref, hf_ref, hb_ref, a_scr, u_scr, p_scr):
        pv = pv_ref[...]
        c = _conv_taps(rec_ref[...], pv)
        cbf = c.astype(BF16)
        for d in range(2):
            _, i, a, m, _, _, _ = _rg_gates(c, cbf, wa_ref[d], wx_ref[d], pv[PV_B_A + d:PV_B_A + d + 1],
                                      pv[PV_B_X + d:PV_B_X + d + 1], pv[PV_LAM + d:PV_LAM + d + 1])
            a_scr[d] = a
            u_scr[d] = m * (i * c)
        _scan_pair(a_scr.at[0], u_scr.at[0], a_scr.at[1], u_scr.at[1], hf_ref, hb_ref, p_scr.at[0], p_scr.at[1], L)
        gate, _ = _gelu(gp_ref[...])
        yg_ref[...] = ((hf_ref[...] + hb_ref[...]) * gate).astype(BF16)

    return pl.pallas_call(
        body, name=name, grid=(nblk, B),
        in_specs=[slab(0), slab(nblk), pvspec, wspec, wspec],
        out_specs=[slab(0), slab(0), slab(0)],
        out_shape=[S((T, C), BF16), S((T, C), F32), S((T, C), F32)],
        scratch_shapes=[pltpu.VMEM((2, L, LRU_BW), F32)] * 3,
        compiler_params=_cp("parallel", "parallel"),
    )(z, z, pvec, wa, wx)


def _rg_bwd(z, hf, hb, dyg, pvec, wa, wx, L, name):
    T, C2 = z.shape
    C = C2 // 2
    nblk = C // LRU_BW
    B = T // L
    slab, wspec, pvspec = _rg_specs(L, C, nblk)

    def body(gp_ref, rec_ref, hf_ref, hb_ref, dyg_ref, pv_ref, wa_ref, wx_ref,
             dz_ref, dwa_ref, dwx_ref, dpv_ref, a_scr, u_scr, d_scr, p_scr):
        @pl.when(pl.program_id(1) == 0)
        def _():
            dwa_ref[...] = jnp.zeros_like(dwa_ref)
            dwx_ref[...] = jnp.zeros_like(dwx_ref)
            dpv_ref[...] = jnp.zeros_like(dpv_ref)

        pv = pv_ref[...]
        rec = rec_ref[...]
        c = _conv_taps(rec, pv)
        cbf = c.astype(BF16)
        gp = gp_ref[...]
        gate, th = _gelu(gp)
        dgelu = 0.5 * (1.0 + th) + 0.5 * gp * (1.0 - th * th) * GELU_K * (1.0 + 3.0 * GELU_C * gp * gp)
        dyg = dyg_ref[...]
        dz_ref[0] = (dyg * (hf_ref[...] + hb_ref[...]) * dgelu).astype(BF16)
        dy = dyg * gate

        gates = []
        for d in range(2):
            gates.append(_rg_gates(c, cbf, wa_ref[d], wx_ref[d], pv[PV_B_A + d:PV_B_A + d + 1],
                                   pv[PV_B_X + d:PV_B_X + d + 1], pv[PV_LAM + d:PV_LAM + d + 1]))
        a_scr[0] = _shift_rows(gates[1][2], 1)
        a_scr[1] = _shift_rows(gates[0][2], -1)
        u_scr[...] = dy
        _scan_pair(a_scr.at[0], u_scr, a_scr.at[1], u_scr, d_scr.at[1], d_scr.at[0], p_scr.at[0], p_scr.at[1], L)

        dc = jnp.zeros_like(c)
        rows = []
        for d in range(2):
            r, i, a, m, rm, a2, sp = gates[d]
            delta = d_scr[d]
            hnb = _shift_rows(hf_ref[...], 1) if d == 0 else _shift_rows(hb_ref[...], -1)
            da = delta * hnb
            dm = delta * (i * c)
            di = delta * (m * c)
            dc = dc + delta * (m * i)
            dla = da * a - dm * (a2 * rm)
            dpa = (dla * ((-RG_C) * sp)) * (r * (1.0 - r))
            dpx = di * (i * (1.0 - i))
            dsp = (-RG_C) * jnp.sum(dla * r, axis=0, keepdims=True)
            lam = pv[PV_LAM + d:PV_LAM + d + 1]
            rows.append((jnp.sum(dpa, axis=0, keepdims=True), jnp.sum(dpx, axis=0, keepdims=True),
                         -dsp * _sigmoid(-lam)))
            dpab = dpa.astype(BF16)
            dpxb = dpx.astype(BF16)
            dwa_ref[d] += _dot_tn(cbf, dpab)
            dwx_ref[d] += _dot_tn(cbf, dpxb)
            dc = dc + _dot_nt(dpab, wa_ref[d]) + _dot_nt(dpxb, wx_ref[d])

        drec = jnp.zeros_like(c)
        dcw = []
        for j in range(CONV_W):
            drec = drec + pv[PV_CONV_W + j:PV_CONV_W + j + 1] * _shift_rows(dc, j - 2)
            dcw.append(jnp.sum(dc * _shift_rows(rec, 2 - j), axis=0, keepdims=True))
        dz_ref[1] = drec.astype(BF16)
        for j in range(CONV_W):
            dpv_ref[PV_CONV_W + j:PV_CONV_W + j + 1, :] += dcw[j]
        for d in range(2):
            dpv_ref[PV_B_A + d:PV_B_A + d + 1, :] += rows[d][0]
            dpv_ref[PV_B_X + d:PV_B_X + d + 1, :] += rows[d][1]
            dpv_ref[PV_LAM + d:PV_LAM + d + 1, :] += rows[d][2]
        dpv_ref[PV_CONV_B:PV_CONV_B + 1, :] += jnp.sum(dc, axis=0, keepdims=True)

    return pl.pallas_call(
        body, name=name, grid=(nblk, B),
        in_specs=[slab(0), slab(nblk), slab(0), slab(0), slab(0), pvspec, wspec, wspec],
        out_specs=[pl.BlockSpec((2, L, LRU_BW), lambda cb, b: (0, b, cb)), wspec, wspec, pvspec],
        out_shape=[S((2, T, C), BF16), S((2, nblk, LRU_BW, LRU_BW), F32), S((2, nblk, LRU_BW, LRU_BW), F32),
                   S((PV_ROWS, C), F32)],
        scratch_shapes=[pltpu.VMEM((2, L, LRU_BW), F32), pltpu.VMEM((L, LRU_BW), F32),
                        pltpu.VMEM((2, L, LRU_BW), F32), pltpu.VMEM((2, L, LRU_BW), F32)],
        compiler_params=_cp("parallel", "arbitrary"),
    )(z, z, hf, hb, dyg, pvec, wa, wx)


QKV_NB = 512


def _interleave(a):
    *lead, L, D = a.shape
    return a.reshape(*lead, N_SEG, L // N_SEG, D).swapaxes(-3, -2).reshape(*lead, L, D)


def _deinterleave(a):
    *lead, L, D = a.shape
    return a.reshape(*lead, L // N_SEG, N_SEG, D).swapaxes(-3, -2).reshape(*lead, L, D)


def _local_step(x3, tgt3, w, fetch, send):
    Bl, L, D = x3.shape
    T = Bl * L
    x = _interleave(x3).reshape(T, D)
    tgt = _interleave(tgt3).reshape(T, D)
    gm = [w["g_mix"][i:i + 1] for i in range(2)]
    gl = [w["g_mlp"][i:i + 1] for i in range(2)]

    w0 = fetch(0, ())
    nb_in = w0["w_in"].shape[-1]
    z, h0 = _norm_matmul(x, gm[0], w0["w_in"], "rg_in")
    yg, hf, hb = _rg_fwd(z, w0["pvec"], w["wa"], w["wx"], L, "rg_fwd")
    w1 = fetch(1, (yg,))
    x1 = _matmul_res(yg, w1["w_out"], x, "rg_out")
    w1.update(fetch(4, (x1,)))
    fb = w1["w_up0"].shape[-1]
    x2, a0, hm0 = _mlp_fwd(x1, gl[0], w1["w_up0"], w1["w_down0"], "mlp0_fwd")
    w2 = fetch(2, (x2,))
    cos, sin = [_interleave(t) for t in _rope_tables(L)]
    qkv, h1, qn, kn, vb = _qkv_proj(x2, gm[1], w2["w_qkv"], w["qg"], w["kg"], cos, sin, L, "at_qkv")
    o = _attn_fwd(qn, kn, vb, L, "at_fwd")
    x3_ = _matmul_res(o, w2["w_o"], x2, "at_out")
    w3 = fetch(3, (x3_,))
    dx4, a1, hm1, loss, dgf = _mlp_fwd(x3_, gl[1], w3["w_up1"], w3["w_down1"], "mlp1_fwd", head=(tgt, w["g_fin"]))

    dx3, da1, dob1, dgl1 = _mlp_bwd_dx(x3_, dx4, a1, gl[1], w3["w_up1"], w3["w_down1"], "mlp1_bwd_dx")
    dwu1, dwd1 = _mlp_bwd_dw(hm1, da1, a1, dob1, fb, "mlp1_bwd_dw")
    sent = send(3, dict(w_up1=dwu1, w_down1=dwd1))
    do, dx3b = _matmul_nt(dx3, w2["w_o"], "at_out_bwd", BF16, after=sent)
    dwo = _matmul_tn(o, dx3b[None], QKV_NB, "at_out_dw", blocked=False)
    dq, dk, dv = _attn_bwd(qn, kn, vb, do, o, L, "at_bwd")
    dqkv, dqg, dkg = _qk_prep_bwd(qkv, dq, dk, dv, w["qg"], w["kg"], cos, sin, L, "at_prep_bwd")
    dwqkv = _matmul_tn(dqkv, h1[None], QKV_NB, "at_qkv_dw", blocked=False)
    sent = send(2, dict(w_qkv=dwqkv, w_o=dwo))
    dx2, dgm1 = _nt_normbwd(dqkv[None], w2["w_qkv"], x2, gm[1], dx3, "at_qkv_bwd", after=sent, transposed=True)
    dx1, da0, dob0, dgl0 = _mlp_bwd_dx(x1, dx2, a0, gl[0], w1["w_up0"], w1["w_down0"], "mlp0_bwd_dx")
    dwu0, dwd0 = _mlp_bwd_dw(hm0, da0, a0, dob0, fb, "mlp0_bwd_dw")
    sent = send(1, dict(w_up0=dwu0, w_down0=dwd0))
    dyg, dx1b = _matmul_nt(dx1, w1["w_out"], "rg_out_bwd", F32, after=sent)
    dwout = _matmul_tn(yg, dx1b[None], QKV_NB, "rg_out_dw", blocked=False)
    dz, dwa, dwx, dpv = _rg_bwd(z, hf, hb, dyg, w0["pvec"], w["wa"], w["wx"], L, "rg_bwd")
    sent = send(4, dict(w_out=dwout, pvec=dpv, wa=dwa, wx=dwx))
    dwin = _matmul_tn(h0, dz, nb_in, "rg_in_dw", blocked=True, after=sent)
    sent = send(0, dict(w_in=dwin))
    dx0, dgm0 = _nt_normbwd(dz, w0["w_in"], x, gm[0], dx1, "rg_in_bwd", after=sent)
    send(-1, dict(g_mix=[dgm0, dgm1], g_mlp=[dgl0, dgl1], g_fin=dgf, conv_b=dpv[PV_CONV_B:PV_CONV_B + 1], qg=dqg, kg=dkg, loss=loss))
    return _deinterleave(dx0.reshape(Bl, L, D))


MESH = pl.DeviceIdType.MESH
ANY = pl.BlockSpec(memory_space=pl.ANY)
N_PEERS = N_DEV - 1


def _my_place():
    return lax.axis_index("x"), lax.axis_index("y"), lax.axis_index("c")


def _flat(px, py, pc):
    return 4 * px + 2 * py + pc


def _all_gather(shards, name):
    n = len(shards)

    def body(*refs):
        ins, outs = refs[:n], refs[n:2 * n]
        send_sems, recv_sems, local_sems = refs[2 * n:]
        x, y, c = _my_place()
        me, sibling = (x, y, c), (x, y, 1 - c)
        chips = [(1 - x, y), (x, 1 - y), (1 - x, 1 - y)]

        def copy(a, k, block, to, src=None):
            dst = outs[a].at[_flat(*block)]
            return pltpu.make_async_remote_copy(
                src_ref=dst if src is None else src, dst_ref=dst,
                send_sem=send_sems.at[a, k], recv_sem=recv_sems.at[a, k],
                device_id=to, device_id_type=MESH)

        mine = [pltpu.make_async_copy(ins[a], outs[a].at[_flat(*me)], local_sems.at[a]) for a in range(n)]
        for cp in mine:
            cp.start()
        first = []
        for a in range(n):
            first.append(copy(a, 0, me, sibling, src=ins[a]))
            first += [copy(a, 1 + j, me, (*chip, c), src=ins[a]) for j, chip in enumerate(chips)]
        for cp in first:
            cp.start()
        passed = []
        for j, chip in enumerate(chips):
            for a in range(n):
                copy(a, 1 + j, (*chip, c), me).wait_recv()
                fwd = copy(a, 4 + j, (*chip, c), sibling)
                fwd.start()
                passed.append(fwd)
        for a in range(n):
            copy(a, 0, sibling, me).wait_recv()
            for j, chip in enumerate(chips):
                copy(a, 4 + j, (*chip, 1 - c), me).wait_recv()
        for cp in first + passed:
            cp.wait_send()
        for cp in mine:
            cp.wait()

    return pl.pallas_call(
        body, name=name,
        in_specs=[ANY] * n, out_specs=[ANY] * n,
        out_shape=[S((N_DEV,) + s.shape, s.dtype) for s in shards],
        scratch_shapes=[pltpu.SemaphoreType.DMA((n, N_PEERS)), pltpu.SemaphoreType.DMA((n, N_PEERS)),
                        pltpu.SemaphoreType.DMA((n,))],
    )(*shards)


HBM = pl.BlockSpec(memory_space=pltpu.HBM)
SEM = pl.BlockSpec(memory_space=pltpu.SEMAPHORE)
SIDE_EFFECT = pltpu.SideEffectType.DATAFLOW_SIDE_EFFECTING
SEMS_PER_GROUP = 3


NEAR_PEERS = (1, 2, 4, 6)
FAR_CHIPS = (2, 4, 6)


def _exchange_copies(srcs, lands, sems, mode):
    send_sems, recv_sems, local_sems = sems
    scatter = mode == "scatter"
    x, y, c = _my_place()
    me = _flat(x, y, c)
    remote, local = [], []
    for a in range(len(srcs)):
        for r in (NEAR_PEERS if mode == "near" else range(1, N_DEV)):
            peer = (1 - x if r & 4 else x, 1 - y if r & 2 else y, 1 - c if r & 1 else c)
            remote.append(pltpu.make_async_remote_copy(
                src_ref=srcs[a].at[_flat(*peer)] if scatter else srcs[a], dst_ref=lands[a].at[me],
                send_sem=send_sems.at[a * N_PEERS + r - 1], recv_sem=recv_sems.at[a * N_PEERS + r - 1],
                device_id=peer, device_id_type=MESH))
        local.append(pltpu.make_async_copy(srcs[a].at[me] if scatter else srcs[a], lands[a].at[me], local_sems.at[a]))
    return remote, local


def _exchange_start(groups, modes, name):
    sizes = [len(g) for g in groups]
    srcs = [pltpu.with_memory_space_constraint(a, pltpu.HBM) for g in groups for a in g]
    n = len(srcs)
    scatter_of = [m == "scatter" for g, m in zip(groups, modes) for _ in g]
    lands = [pltpu.with_memory_space_constraint(lax.empty(a.shape if sc else (N_DEV,) + a.shape, a.dtype), pltpu.HBM)
             for a, sc in zip(srcs, scatter_of)]
    n_sem = SEMS_PER_GROUP * len(groups)

    def body(*refs):
        src_refs, land_refs, sem_refs, token = refs[:n], refs[n:2 * n], refs[2 * n:2 * n + n_sem], refs[-1]
        off = 0
        for gi, k in enumerate(sizes):
            remote, local = _exchange_copies(src_refs[off:off + k], land_refs[off:off + k],
                                             sem_refs[SEMS_PER_GROUP * gi:SEMS_PER_GROUP * (gi + 1)], modes[gi])
            for cp in local + remote:
                cp.start()
            off += k
        token[...] = jnp.zeros_like(token)

    sem_shapes = []
    for k in sizes:
        sem_shapes += [pltpu.SemaphoreType.DMA((k * N_PEERS,)), pltpu.SemaphoreType.DMA((k * N_PEERS,)),
                       pltpu.SemaphoreType.DMA((k,))]
    outs = pl.pallas_call(
        body, name=name,
        out_shape=sem_shapes + [pltpu.HBM(a.shape, a.dtype) for a in srcs + lands] + [S((8, 128), F32)],
        in_specs=[HBM] * (2 * n),
        out_specs=[SEM] * n_sem + [HBM] * (2 * n) + [pl.BlockSpec(memory_space=pltpu.VMEM)],
        input_output_aliases={i: n_sem + i for i in range(2 * n)},
        compiler_params=pltpu.CompilerParams(has_side_effects=SIDE_EFFECT),
    )(*srcs, *lands)
    sems, thru, token = outs[:n_sem], outs[n_sem:n_sem + 2 * n], outs[-1]
    per_group, off = [], 0
    for gi, k in enumerate(sizes):
        per_group.append((sems[SEMS_PER_GROUP * gi:SEMS_PER_GROUP * (gi + 1)], thru[off:off + k], thru[n + off:n + off + k]))
        off += k
    return per_group, token


def _exchange_wait(group, after, mode, name):
    sems, srcs, lands = group
    k = len(srcs)

    def body(*refs):
        remote, local = _exchange_copies(refs[:k], refs[k:2 * k], refs[2 * k:2 * k + SEMS_PER_GROUP], mode)
        for cp in remote:
            cp.wait_send()
            cp.wait_recv()
        for cp in local:
            cp.wait()

    outs = pl.pallas_call(
        body, name=name,
        out_shape=[pltpu.HBM(a.shape, a.dtype) for a in list(srcs) + list(lands)],
        in_specs=[HBM] * (2 * k) + [SEM] * SEMS_PER_GROUP + [ANY] * len(after),
        out_specs=[HBM] * (2 * k),
        input_output_aliases={i: i for i in range(2 * k)},
        compiler_params=pltpu.CompilerParams(has_side_effects=SIDE_EFFECT),
    )(*srcs, *lands, *sems, *after)
    return outs[k:]


def _forward_copies(lands, sems):
    send_sems, recv_sems = sems
    x, y, c = _my_place()
    mine, theirs = [], []
    for a in range(len(lands)):
        for k, r in enumerate(FAR_CHIPS):
            px, py = (1 - x if r & 4 else x), (1 - y if r & 2 else y)
            for out, core in ((mine, c), (theirs, 1 - c)):
                blk = lands[a].at[_flat(px, py, core)]
                out.append(pltpu.make_async_remote_copy(
                    src_ref=blk, dst_ref=blk, send_sem=send_sems.at[a * len(FAR_CHIPS) + k],
                    recv_sem=recv_sems.at[a * len(FAR_CHIPS) + k], device_id=(x, y, 1 - c), device_id_type=MESH))
    return mine, theirs


def _forward_start(groups, name):
    sizes = [len(g) for g in groups]
    lands = [a for g in groups for a in g]
    n = len(lands)
    n_sem = 2 * len(groups)

    def body(*refs):
        land_refs, sem_refs, token = refs[:n], refs[n:n + n_sem], refs[-1]
        off = 0
        for gi, k in enumerate(sizes):
            mine, _ = _forward_copies(land_refs[off:off + k], sem_refs[2 * gi:2 * gi + 2])
            for cp in mine:
                cp.start()
            off += k
        token[...] = jnp.zeros_like(token)

    sem_shapes = []
    for k in sizes:
        sem_shapes += [pltpu.SemaphoreType.DMA((k * len(FAR_CHIPS),))] * 2
    outs = pl.pallas_call(
        body, name=name,
        out_shape=sem_shapes + [pltpu.HBM(a.shape, a.dtype) for a in lands] + [S((8, 128), F32)],
        in_specs=[HBM] * n,
        out_specs=[SEM] * n_sem + [HBM] * n + [pl.BlockSpec(memory_space=pltpu.VMEM)],
        input_output_aliases={i: n_sem + i for i in range(n)},
        compiler_params=pltpu.CompilerParams(has_side_effects=SIDE_EFFECT),
    )(*lands)
    per_group, off = [], 0
    for gi, k in enumerate(sizes):
        per_group.append((outs[2 * gi:2 * gi + 2], outs[n_sem + off:n_sem + off + k]))
        off += k
    return per_group


def _forward_wait(group, after, name):
    sems, lands = group
    k = len(lands)

    def body(*refs):
        mine, theirs = _forward_copies(refs[:k], refs[k:k + 2])
        for cp in mine:
            cp.wait_send()
        for cp in theirs:
            cp.wait_recv()

    return pl.pallas_call(
        body, name=name,
        out_shape=[pltpu.HBM(a.shape, a.dtype) for a in lands],
        in_specs=[HBM] * k + [SEM] * 2 + [ANY] * len(after),
        out_specs=[HBM] * k,
        input_output_aliases={i: i for i in range(k)},
        compiler_params=pltpu.CompilerParams(has_side_effects=SIDE_EFFECT),
    )(*lands, *sems, *after)


def _row_tile(rows, cols):
    want = max(16, (128 * 1024) // cols)
    if rows <= want:
        return rows
    t = want - want % 16
    while rows % t:
        t -= 16
    return t


def _sum_parts(parts, name, after=()):
    P, R, C = parts.shape
    tr = _row_tile(R, C)

    def body(p_ref, *rest):
        o_ref = rest[-1]
        g = p_ref[0].astype(F32)
        for i in range(1, P):
            g = g + p_ref[i].astype(F32)
        o_ref[...] = g

    return pl.pallas_call(
        body, name=name, grid=(R // tr,),
        in_specs=[pl.BlockSpec((P, tr, C), lambda i: (0, i, 0))] + [ANY] * len(after),
        out_specs=pl.BlockSpec((tr, C), lambda i: (i, 0)),
        out_shape=S((R, C), F32),
        compiler_params=_cp("parallel"),
    )(parts, *after)


def _adamw(parts, w, m, v, name, after=()):
    P, R, C = parts.shape
    tr = _row_tile(R, C)
    c1 = 1.0 - ADAM_B1 ** ADAM_STEP
    c2 = 1.0 - ADAM_B2 ** ADAM_STEP

    def body(p_ref, w_ref, m_ref, v_ref, *rest):
        g_ref, d_ref, mo_ref, vo_ref = rest[len(after):]
        g = p_ref[0].astype(F32)
        for i in range(1, P):
            g = g + p_ref[i].astype(F32)
        mn = ADAM_B1 * m_ref[...] + (1.0 - ADAM_B1) * g
        vn = ADAM_B2 * v_ref[...] + (1.0 - ADAM_B2) * (g * g)
        g_ref[...] = g
        mo_ref[...] = mn
        vo_ref[...] = vn
        d_ref[...] = (-ADAM_LR) * ((mn / c1) / (jnp.sqrt(vn / c2) + ADAM_EPS) + ADAM_WD * w_ref[...])

    blk = pl.BlockSpec((tr, C), lambda i: (i, 0))
    return pl.pallas_call(
        body, name=name, grid=(R // tr,),
        in_specs=[pl.BlockSpec((P, tr, C), lambda i: (0, i, 0)), blk, blk, blk] + [ANY] * len(after),
        out_specs=[blk, blk, blk, blk],
        out_shape=[S((R, C), F32)] * 4,
        compiler_params=_cp("parallel"),
    )(parts, w, m, v, *after)


def _adamw_layer(parts, w3, m3, v3, layer, prev, name, after=()):
    P, R, C = parts.shape
    NL = w3.shape[0]
    tr = _row_tile(R, C)
    c1 = 1.0 - ADAM_B1 ** ADAM_STEP
    c2 = 1.0 - ADAM_B2 ** ADAM_STEP
    n_prev = 0 if prev is None else len(prev)

    def body(p_ref, w_ref, m_ref, v_ref, *rest):
        g_ref, d_ref, mo_ref, vo_ref = rest[n_prev + len(after):]
        g = p_ref[0].astype(F32)
        for i in range(1, P):
            g = g + p_ref[i].astype(F32)
        mn = ADAM_B1 * m_ref[...] + (1.0 - ADAM_B1) * g
        vn = ADAM_B2 * v_ref[...] + (1.0 - ADAM_B2) * (g * g)
        g_ref[...] = g
        mo_ref[...] = mn
        vo_ref[...] = vn
        d_ref[...] = (-ADAM_LR) * ((mn / c1) / (jnp.sqrt(vn / c2) + ADAM_EPS) + ADAM_WD * w_ref[...])

    blk = pl.BlockSpec((None, tr, C), lambda i: (layer, i, 0))
    return pl.pallas_call(
        body, name=name, grid=(R // tr,),
        in_specs=[pl.BlockSpec((P, tr, C), lambda i: (0, i, 0)), blk, blk, blk] + [ANY] * (n_prev + len(after)),
        out_specs=[blk, blk, blk, blk],
        out_shape=[S((NL, R, C), F32)] * 4,
        input_output_aliases={4 + k: k for k in range(n_prev)},
        compiler_params=_cp("parallel"),
    )(parts, w3, m3, v3, *(prev or ()), *after)


VMEM_WHOLE = pl.BlockSpec(memory_space=pltpu.VMEM)
LANES = 128


def _pack_vectors(vectors, starts, rows, name):
    def body(*refs):
        o_ref = refs[-1]
        o_ref[...] = jnp.zeros_like(o_ref)
        for v_ref, r0 in zip(refs[:-1], starts):
            for j in range(v_ref.shape[1] // LANES):
                o_ref[r0 + j:r0 + j + 1, :] = v_ref[:, j * LANES:(j + 1) * LANES]

    return pl.pallas_call(body, name=name, in_specs=[VMEM_WHOLE] * len(vectors), out_specs=VMEM_WHOLE,
                          out_shape=S((rows, LANES), F32))(*vectors)


def _adamw_vectors(g_pack, params, keep_rows, name, after=()):
    n = len(params)
    P = g_pack.shape[0]
    c1 = 1.0 - ADAM_B1 ** ADAM_STEP
    c2 = 1.0 - ADAM_B2 ** ADAM_STEP

    def body(g_ref, *refs):
        ins, outs = refs[:3 * n], refs[3 * n + len(after):]
        gs = g_ref[0]
        for i in range(1, P):
            gs = gs + g_ref[i]
        for pi, (_, _, _, slots) in enumerate(params):
            w_ref, m_ref, v_ref = ins[3 * pi:3 * pi + 3]
            g_out, d_out, m_out, v_out = outs[4 * pi:4 * pi + 4]
            for idx, row in slots:
                g = gs[row:row + 1, :]
                mn = ADAM_B1 * m_ref[idx] + (1.0 - ADAM_B1) * g
                vn = ADAM_B2 * v_ref[idx] + (1.0 - ADAM_B2) * (g * g)
                g_out[idx] = g
                m_out[idx] = mn
                v_out[idx] = vn
                d_out[idx] = (-ADAM_LR) * ((mn / c1) / (jnp.sqrt(vn / c2) + ADAM_EPS) + ADAM_WD * w_ref[idx])
        outs[-1][...] = jnp.concatenate([gs[r:r + 1, :] for r in keep_rows], axis=0)

    flat = [a for w, m, v, _ in params for a in (w, m, v)]
    out_shape = [S(w.shape, F32) for w, _, _, _ in params for _ in range(4)] + [S((len(keep_rows), LANES), F32)]
    outs = pl.pallas_call(
        body, name=name,
        in_specs=[VMEM_WHOLE] * (1 + len(flat)) + [ANY] * len(after),
        out_specs=[VMEM_WHOLE] * len(out_shape), out_shape=out_shape,
    )(g_pack, *flat, *after)
    return [outs[4 * i:4 * i + 4] for i in range(n)], outs[-1]


def _adamw_nd(parts, w, m, v, name, after=()):
    shp = w.shape
    C = shp[-1]
    outs = _adamw(parts.reshape(parts.shape[0], -1, C), w.reshape(-1, C), m.reshape(-1, C), v.reshape(-1, C), name, after)
    return [o.reshape(shp) for o in outs]


TILE_ROWS = 8


REP_SMALL_ROWS = 128
REP_GRAD_STARTS = (0, 8, 16, 24, 32, 40, 48, 56, 64)
REP_SMALL_STARTS = (0, 16, 32, 40, 48, 56)
REP_LOSS_ROW = 64


def _small_pack(cw, ba, bx, lam):
    pad8 = lambda a: jnp.pad(a, ((0, TILE_ROWS - a.shape[0]), (0, 0)))
    return jnp.concatenate([pad8(cw[0, :, 0, :]), pad8(ba[0]), pad8(bx[0]), pad8(lam[0]),
                            jnp.zeros((PV_ROWS - PV_CONV_B, LRU_BW), F32)], axis=0)


def kernel(x, norm_mix_g, norm_mlp_g, rg_w_in, rg_conv_w, rg_conv_b, rg_w_a, rg_b_a, rg_w_x, rg_b_x, rg_lam, rg_w_out, at_w_qkv, at_q_g, at_k_g, at_w_o, mlp_w_up, mlp_w_down, final_g, loss_target, m_norm_mix_g, m_norm_mlp_g, m_rg_w_in, m_rg_conv_w, m_rg_conv_b, m_rg_w_a, m_rg_b_a, m_rg_w_x, m_rg_b_x, m_rg_lam, m_rg_w_out, m_at_w_qkv, m_at_q_g, m_at_k_g, m_at_w_o, m_mlp_w_up, m_mlp_w_down, m_final_g, v_norm_mix_g, v_norm_mlp_g, v_rg_w_in, v_rg_conv_w, v_rg_conv_b, v_rg_w_a, v_rg_b_a, v_rg_w_x, v_rg_b_x, v_rg_lam, v_rg_w_out, v_at_w_qkv, v_at_q_g, v_at_k_g, v_at_w_o, v_mlp_w_up, v_mlp_w_down, v_final_g):
    D = x.shape[-1]
    bf = lambda a: a.astype(BF16)

    sp_w = _small_pack(rg_conv_w, rg_b_a, rg_b_x, rg_lam)
    started, _ = _exchange_start(
        [[bf(rg_w_in[0]), sp_w], [bf(rg_w_out[0])], [bf(mlp_w_up[0]), bf(mlp_w_down[0])],
         [bf(at_w_qkv[0]).T, bf(at_w_o[0])], [bf(mlp_w_up[1]), bf(mlp_w_down[1])]],
        ["near", "near", "near", "gather", "gather"], "gather_start")
    gathers = dict(zip((0, 1, 4, 2, 3), started))
    forwards = {}

    def fetch(stage, after):
        after = tuple(after)
        if stage == 0:
            got = _exchange_wait(gathers[0], after, "near", "gather_wait0")
            g_in, g_sp = _forward_wait(_forward_start([got], "forward_start0")[0], (), "forward_wait0")
            pvec = g_sp.transpose(1, 0, 2).reshape(PV_ROWS, D)
            pvec = jnp.concatenate([pvec[:PV_CONV_B], jnp.broadcast_to(rg_conv_b, (PV_ROWS - PV_CONV_B, D))], axis=0)
            return dict(w_in=g_in, pvec=pvec)
        if stage == 1:
            near = [_exchange_wait(gathers[s], after, "near", "gather_wait%d" % s) for s in (1, 4)]
            f_out, forwards[4] = _forward_start(near, "forward_start1")
            g_out, = _forward_wait(f_out, (), "forward_wait1")
            return dict(w_out=g_out.reshape(D, D))
        if stage == 4:
            g_up0, g_dn0 = _forward_wait(forwards[4], after, "forward_wait4")
            return dict(w_up0=g_up0, w_down0=g_dn0.reshape(-1, D))
        got = _exchange_wait(gathers[stage], after, "gather", "gather_wait%d" % stage)
        if stage == 2:
            return dict(w_qkv=got[0].reshape(-1, QKV_NB, D), w_o=got[1].reshape(D, D))
        return dict(w_up1=got[0], w_down1=got[1].reshape(-1, D))

    scatters = {}

    def send(stage, g):
        if stage == 3:
            arrs = [g["w_up1"], g["w_down1"].reshape(N_DEV, -1, D)]
        elif stage == 2:
            arrs = [g["w_qkv"].reshape(N_DEV, -1, D), g["w_o"].reshape(N_DEV, -1, D)]
        elif stage == 1:
            arrs = [g["w_up0"], g["w_down0"].reshape(N_DEV, -1, D)]
        elif stage == 4:
            arrs = [g["w_out"].reshape(N_DEV, -1, D), g["pvec"].reshape(PV_ROWS, N_DEV, LRU_BW).transpose(1, 0, 2),
                    bf(g["wa"]).reshape(N_DEV, -1, LANES), bf(g["wx"]).reshape(N_DEV, -1, LANES)]
        elif stage == 0:
            arrs = [g["w_in"]]
        else:
            small = _pack_vectors(g["g_mix"] + g["g_mlp"] + [g["g_fin"], g["conv_b"], g["qg"], g["kg"], g["loss"]],
                                  REP_GRAD_STARTS, REP_SMALL_ROWS, "pack_rep_small")
            arrs = [small.reshape(N_DEV, -1, LANES)]
        (group,), token = _exchange_start([arrs], ["scatter"], "scatter_start%d" % (stage % 6))
        scatters[stage] = (group, token)
        return (token,)

    w = dict(g_mix=norm_mix_g, g_mlp=norm_mlp_g, g_fin=final_g[None], qg=at_q_g, kg=at_k_g,
             wa=bf(rg_w_a[0]), wx=bf(rg_w_x[0]))
    grad_x = _local_step(x, loss_target, w, fetch, send)

    res = {}
    r_up1, r_dn1 = _exchange_wait(scatters[3][0], (scatters[-1][1],), "scatter", "scatter_wait3")
    r_out, r_sp, r_wa, r_wx = _exchange_wait(scatters[4][0], (r_up1,), "scatter", "scatter_wait4")
    wa_part = _sum_parts(r_wa, "reduce_w_a")
    wx_part = _sum_parts(r_wx, "reduce_w_x", (wa_part,))
    (rep_gather,), rep_token = _exchange_start([[wa_part, wx_part]], ["gather"], "rep_gather_start")
    up = _adamw_layer(r_up1, mlp_w_up, m_mlp_w_up, v_mlp_w_up, 1, None, "adam_mlp_w_up1", after=(rep_token,))
    dn = _adamw_layer(r_dn1, mlp_w_down, m_mlp_w_down, v_mlp_w_down, 1, None, "adam_mlp_w_down1", after=(up[1],))
    r_qkv, r_o = _exchange_wait(scatters[2][0], (dn[1],), "scatter", "scatter_wait2")
    tr = lambda a: a[0].T
    qkv_t = _adamw_nd(r_qkv, tr(at_w_qkv), tr(m_at_w_qkv), tr(v_at_w_qkv), "adam_at_w_qkv")
    res["at_w_qkv"] = [o.T[None] for o in qkv_t]
    res["at_w_o"] = _adamw_nd(r_o[:, None], at_w_o, m_at_w_o, v_at_w_o, "adam_at_w_o", (qkv_t[1],))
    r_up0, r_dn0 = _exchange_wait(scatters[1][0], (res["at_w_o"][1],), "scatter", "scatter_wait1")
    res["mlp_w_up"] = _adamw_layer(r_up0, mlp_w_up, m_mlp_w_up, v_mlp_w_up, 0, up, "adam_mlp_w_up0")
    res["mlp_w_down"] = _adamw_layer(r_dn0, mlp_w_down, m_mlp_w_down, v_mlp_w_down, 0, dn, "adam_mlp_w_down0",
                                     after=(res["mlp_w_up"][1],))
    r_in, = _exchange_wait(scatters[0][0], (res["mlp_w_down"][1],), "scatter", "scatter_wait0")
    res["rg_w_in"] = _adamw_nd(r_in[:, None], rg_w_in, m_rg_w_in, v_rg_w_in, "adam_rg_w_in")
    res["rg_w_out"] = _adamw_nd(r_out[:, None], rg_w_out, m_rg_w_out, v_rg_w_out, "adam_rg_w_out", (res["rg_w_in"][1],))
    whole, lane = slice(None), slice(0, 1)
    two_rows = lambda r0: [((0, slice(d, d + 1), whole), r0 + d) for d in range(2)]
    (res["rg_conv_w"], res["rg_b_a"], res["rg_b_x"], res["rg_lam"]), _ = _adamw_vectors(
        r_sp, [(rg_conv_w, m_rg_conv_w, v_rg_conv_w, [((0, t, lane, whole), PV_CONV_W + t) for t in range(CONV_W)]),
               (rg_b_a, m_rg_b_a, v_rg_b_a, two_rows(PV_B_A)), (rg_b_x, m_rg_b_x, v_rg_b_x, two_rows(PV_B_X)),
               (rg_lam, m_rg_lam, v_rg_lam, two_rows(PV_LAM))], [0], "adam_small", (res["rg_w_out"][1],))

    r_small, = _exchange_wait(scatters[-1][0], (res["rg_lam"][1],), "scatter", "scatter_wait5")
    small_sum, = _all_gather([_sum_parts(r_small, "reduce_rep_small")], "gather_replicated")
    wa_sum, wx_sum = _exchange_wait(rep_gather, (small_sum,), "gather", "rep_gather_wait")
    rows = lambda a: a.reshape(-1, LANES)
    wa_res = _adamw(wa_sum.reshape(1, -1, LANES), rows(rg_w_a), rows(m_rg_w_a), rows(v_rg_w_a), "adam_rg_w_a")
    wx_res = _adamw(wx_sum.reshape(1, -1, LANES), rows(rg_w_x), rows(m_rg_w_x), rows(v_rg_w_x), "adam_rg_w_x", (wa_res[1],))
    res["rg_w_a"] = [o.reshape(rg_w_a.shape) for o in wa_res]
    res["rg_w_x"] = [o.reshape(rg_w_x.shape) for o in wx_res]

    def vec_slots(a, r0):
        per = a.shape[1] // LANES
        return [((slice(l, l + 1), slice(LANES * j, LANES * (j + 1))), r0 + l * per + j)
                for l in range(a.shape[0]) for j in range(per)]

    fin = [final_g[None], m_final_g[None], v_final_g[None]]
    vecs = [(norm_mix_g, m_norm_mix_g, v_norm_mix_g), (norm_mlp_g, m_norm_mlp_g, v_norm_mlp_g), fin,
            (rg_conv_b, m_rg_conv_b, v_rg_conv_b), (at_q_g, m_at_q_g, v_at_q_g), (at_k_g, m_at_k_g, v_at_k_g)]
    outs, kept = _adamw_vectors(
        small_sum.reshape(1, -1, LANES),
        [(wv, mv, vv, vec_slots(wv, r0)) for (wv, mv, vv), r0 in zip(vecs, REP_SMALL_STARTS)], [REP_LOSS_ROW],
        "adam_rep_small", (wx_res[1],))
    for nm, o in zip(["norm_mix_g", "norm_mlp_g", "final_g", "rg_conv_b", "at_q_g", "at_k_g"], outs):
        res[nm] = [a[0] for a in o] if nm == "final_g" else o
    loss = kept[0, 0]

    order = ["norm_mix_g", "norm_mlp_g", "rg_w_in", "rg_conv_w", "rg_conv_b", "rg_w_a", "rg_b_a", "rg_w_x", "rg_b_x",
             "rg_lam", "rg_w_out", "at_w_qkv", "at_q_g", "at_k_g", "at_w_o", "mlp_w_up", "mlp_w_down", "final_g"]
    return (loss, grad_x, *[res[nm][k] for k in range(4) for nm in order])
```

```python
import math

import jax
import jax.numpy as jnp
from jax import lax
from jax.experimental import pallas as pl
from jax.experimental.pallas import tpu as pltpu

F32 = jnp.float32
BF16 = jnp.bfloat16
S = jax.ShapeDtypeStruct

EPS = 1e-6
HEAD_DIM = 128
N_KV = 2
GRID_W = 64
ROPE_THETA = 10000.0
LRU_BW = 128
RG_C = 8.0
CONV_W = 4
N_DEV = 8
N_SEG = 8
SCAN_UNROLL = 8
TN_STEP_COLS = 512
PROJ_TM = 1024
PROJ_CHAINS = 2
MLP_TM = 512
VMEM_LIMIT_V7X = 56 * 1024 * 1024
SOFTMAX_SCALE = 1.0 / math.sqrt(HEAD_DIM)
GELU_K = math.sqrt(2.0 / math.pi)
GELU_C = 0.044715

ADAM_LR = 0.001
ADAM_B1 = 0.9
ADAM_B2 = 0.999
ADAM_EPS = 1e-08
ADAM_WD = 0.01
ADAM_STEP = 10

NT = (((1,), (1,)), ((), ()))
TN = (((0,), (0,)), ((), ()))


def _cp(*sem):
    return pltpu.CompilerParams(dimension_semantics=sem, vmem_limit_bytes=VMEM_LIMIT_V7X)


def _rms_r(xv):
    return lax.rsqrt(jnp.mean(xv * xv, axis=-1, keepdims=True) + EPS)


def _rms_bwd(dh, xv, g):
    r = _rms_r(xv)
    xh = xv * r
    dg = jnp.sum(dh * xh, axis=0, keepdims=True)
    dxh = dh * g
    dx = r * (dxh - xh * jnp.mean(dxh * xh, axis=-1, keepdims=True))
    return dx, dg


def _dot(a, b):
    return jnp.dot(a, b, preferred_element_type=F32)


def _dot_nt(a, b):
    return lax.dot_general(a, b, NT, preferred_element_type=F32)


def _dot_tn(a, b):
    return lax.dot_general(a, b, TN, preferred_element_type=F32)


def _norm_matmul(x, g, wblk, name, out_dtype=F32):
    T, D = x.shape
    NB, _, nb = wblk.shape
    tm = min(T, PROJ_TM)

    def body(x_ref, g_ref, w_ref, o_ref, h_ref):
        for c in range(PROJ_CHAINS):
            rows = slice(c * tm // PROJ_CHAINS, (c + 1) * tm // PROJ_CHAINS)
            xv = x_ref[rows, :]
            hb = (xv * _rms_r(xv) * g_ref[...]).astype(BF16)
            h_ref[rows, :] = hb
            for q in range(NB):
                o_ref[rows, q * nb:(q + 1) * nb] = _dot(hb, w_ref[q]).astype(o_ref.dtype)

    return pl.pallas_call(
        body, name=name, grid=(T // tm,),
        in_specs=[pl.BlockSpec((tm, D), lambda i: (i, 0)),
                  pl.BlockSpec((1, D), lambda i: (0, 0)),
                  pl.BlockSpec(wblk.shape, lambda i: (0, 0, 0))],
        out_specs=[pl.BlockSpec((tm, NB * nb), lambda i: (i, 0)),
                   pl.BlockSpec((tm, D), lambda i: (i, 0))],
        out_shape=[S((T, NB * nb), out_dtype), S((T, D), BF16)],
        compiler_params=_cp("parallel"),
    )(x, g, wblk)


def _matmul_res(a, w, res, name):
    T, K = a.shape
    N = w.shape[1]
    tm = min(T, PROJ_TM)

    def body(a_ref, w_ref, r_ref, o_ref):
        o_ref[...] = r_ref[...] + _dot(a_ref[...], w_ref[...])

    return pl.pallas_call(
        body, name=name, grid=(T // tm,),
        in_specs=[pl.BlockSpec((tm, K), lambda i: (i, 0)),
                  pl.BlockSpec((K, N), lambda i: (0, 0)),
                  pl.BlockSpec((tm, N), lambda i: (i, 0))],
        out_specs=pl.BlockSpec((tm, N), lambda i: (i, 0)),
        out_shape=S((T, N), F32),
        compiler_params=_cp("parallel"),
    )(a, w, res)


def _matmul_nt(a, w, name, out_dtype, after=()):
    T, N = a.shape
    K = w.shape[0]
    tm = min(T, PROJ_TM)

    def body(a_ref, w_ref, *rest):
        o_ref, ab_ref = rest[len(after):]
        ab = a_ref[...].astype(BF16)
        ab_ref[...] = ab
        o_ref[...] = _dot_nt(ab, w_ref[...]).astype(o_ref.dtype)

    return pl.pallas_call(
        body, name=name, grid=(T // tm,),
        in_specs=[pl.BlockSpec((tm, N), lambda i: (i, 0)),
                  pl.BlockSpec((K, N), lambda i: (0, 0))] + [pl.BlockSpec(memory_space=pl.ANY)] * len(after),
        out_specs=[pl.BlockSpec((tm, K), lambda i: (i, 0)),
                   pl.BlockSpec((tm, N), lambda i: (i, 0))],
        out_shape=[S((T, K), out_dtype), S((T, N), BF16)],
        compiler_params=_cp("parallel"),
    )(a, w, *after)


def _matmul_tn(a, b3, nb, name, blocked, after=()):
    T, M = a.shape
    SB, _, N = b3.shape
    per = N // nb
    NB = SB * per
    tk = min(T, 1024)
    nk = T // tk
    jb = max(1, TN_STEP_COLS // nb) if blocked else 1
    assert per % jb == 0
    if blocked:
        out_spec, out_shape = pl.BlockSpec((jb, M, nb), lambda j: (j, 0, 0)), S((NB, M, nb), BF16)
    else:
        assert SB == 1
        out_spec, out_shape = pl.BlockSpec((M, nb), lambda j: (0, j)), S((M, N), BF16)

    def body(a_ref, b_ref, *rest):
        o_ref = rest[-1]
        for q in range(jb):
            acc = None
            for k in range(nk):
                rows = slice(k * tk, (k + 1) * tk)
                part = _dot_tn(a_ref[rows, :], b_ref[rows, q * nb:(q + 1) * nb])
                acc = part if acc is None else acc + part
            if blocked:
                o_ref[q] = acc.astype(BF16)
            else:
                o_ref[...] = acc.astype(BF16)

    return pl.pallas_call(
        body, name=name, grid=(NB // jb,),
        in_specs=[pl.BlockSpec((T, M), lambda j: (0, 0), pipeline_mode=pl.Buffered(1)),
                  pl.BlockSpec((None, T, jb * nb), lambda j: ((j * jb) // per, 0, ((j * jb) % per) // jb))]
        + [pl.BlockSpec(memory_space=pl.ANY)] * len(after),
        out_specs=out_spec,
        out_shape=out_shape,
        compiler_params=_cp("parallel"),
    )(a, b3, *after)


def _nt_normbwd(dz3, wblk, x, g, dres, name, after=(), transposed=False):
    T, D = x.shape
    NB, nb = wblk.shape[0], wblk.shape[1 if transposed else 2]
    mm = _dot if transposed else _dot_nt
    SB, _, N = dz3.shape
    per = N // nb
    tm = min(T, PROJ_TM)

    def body(dz_ref, w_ref, x_ref, g_ref, dr_ref, *rest):
        dx_ref, dg_ref = rest[len(after):]

        @pl.when(pl.program_id(0) == 0)
        def _():
            dg_ref[...] = jnp.zeros_like(dg_ref)

        for c in range(PROJ_CHAINS):
            rows = slice(c * tm // PROJ_CHAINS, (c + 1) * tm // PROJ_CHAINS)
            dh = None
            for q in range(NB):
                cols = slice((q % per) * nb, (q % per + 1) * nb)
                part = mm(dz_ref[q // per, rows, cols], w_ref[q])
                dh = part if dh is None else dh + part
            dx, dg = _rms_bwd(dh, x_ref[rows, :], g_ref[...])
            dx_ref[rows, :] = dr_ref[rows, :] + dx
            dg_ref[...] += dg

    return pl.pallas_call(
        body, name=name, grid=(T // tm,),
        in_specs=[pl.BlockSpec((SB, tm, N), lambda i: (0, i, 0)),
                  pl.BlockSpec(wblk.shape, lambda i: (0, 0, 0)),
                  pl.BlockSpec((tm, D), lambda i: (i, 0)),
                  pl.BlockSpec((1, D), lambda i: (0, 0)),
                  pl.BlockSpec((tm, D), lambda i: (i, 0))] + [pl.BlockSpec(memory_space=pl.ANY)] * len(after),
        out_specs=[pl.BlockSpec((tm, D), lambda i: (i, 0)),
                   pl.BlockSpec((1, D), lambda i: (0, 0))],
        out_shape=[S((T, D), F32), S((1, D), F32)],
        compiler_params=_cp("arbitrary"),
    )(dz3, wblk, x, g, dres, *after)


def _loss_head(xv, tv, gv, D):
    err = xv * _rms_r(xv) * gv - tv
    e2 = jnp.sum(jnp.sum(err * err, axis=-1, keepdims=True), axis=0, keepdims=True)
    dx, dg = _rms_bwd(err * (1.0 / D), xv, gv)
    return (0.5 / D) * e2, dx, dg


def _mlp_fwd(x, g, wup, wdown, name, head=None):
    T, D = x.shape
    NB, _, fb = wup.shape
    tm = min(T, MLP_TM)
    n_head = 0 if head is None else 2

    def body(x_ref, g_ref, wu_ref, wd_ref, *rest):
        xo_ref, a_ref, h_ref = rest[n_head:n_head + 3]
        xv = x_ref[...]
        hb = (xv * _rms_r(xv) * g_ref[...]).astype(BF16)
        h_ref[...] = hb
        acc = xv
        for j in range(NB):
            a = _dot(hb, wu_ref[j])
            a_ref[:, j * fb:(j + 1) * fb] = a.astype(BF16)
            u = jnp.maximum(a, 0.0)
            acc = acc + _dot((u * u).astype(BF16), wd_ref[j * fb:(j + 1) * fb, :])

        if head is None:
            xo_ref[...] = acc
        else:
            t_ref, gf_ref = rest[:2]
            loss_ref, dgf_ref = rest[n_head + 3:n_head + 5]

            @pl.when(pl.program_id(0) == 0)
            def _():
                loss_ref[...] = jnp.zeros_like(loss_ref)
                dgf_ref[...] = jnp.zeros_like(dgf_ref)

            e2, dx, dg = _loss_head(acc, t_ref[...], gf_ref[...], D)
            xo_ref[...] = dx
            loss_ref[...] += e2
            dgf_ref[...] += dg

    row = pl.BlockSpec((tm, D), lambda i: (i, 0))
    vec = pl.BlockSpec((1, D), lambda i: (0, 0))
    once = pl.Buffered(1)
    in_specs = [row, vec, pl.BlockSpec((NB, D, fb), lambda i: (0, 0, 0), pipeline_mode=once),
                pl.BlockSpec((NB * fb, D), lambda i: (0, 0), pipeline_mode=once)]
    out_specs = [row, pl.BlockSpec((tm, NB * fb), lambda i: (i, 0)), row]
    out_shape = [S((T, D), F32), S((T, NB * fb), BF16), S((T, D), BF16)]
    if head is not None:
        in_specs += [row, vec]
        out_specs += [pl.BlockSpec((1, 128), lambda i: (0, 0)), vec]
        out_shape += [S((1, 128), F32), S((1, D), F32)]
    return pl.pallas_call(
        body, name=name, grid=(T // tm,),
        in_specs=in_specs, out_specs=out_specs, out_shape=out_shape,
        compiler_params=_cp("parallel" if head is None else "arbitrary"),
    )(x, g, wup, wdown, *(head or ()))


def _mlp_bwd_dx(x, dout, a, g, wup, wdown, name):
    T, D = x.shape
    NB, _, fb = wup.shape
    tm = min(T, MLP_TM)

    def body(x_ref, do_ref, a_ref, g_ref, wu_ref, wd_ref, dx_ref, da_ref, dob_ref, dg_ref):
        @pl.when(pl.program_id(0) == 0)
        def _():
            dg_ref[...] = jnp.zeros_like(dg_ref)

        dov = do_ref[...]
        dob = dov.astype(BF16)
        dob_ref[...] = dob
        dh = None
        for j in range(NB):
            cols = slice(j * fb, (j + 1) * fb)
            du2 = _dot_nt(dob, wd_ref[cols, :])
            u = jnp.maximum(a_ref[:, cols].astype(F32), 0.0)
            da = (du2 * (2.0 * u)).astype(BF16)
            da_ref[:, cols] = da
            part = _dot_nt(da, wu_ref[j])
            dh = part if dh is None else dh + part
        dx, dg = _rms_bwd(dh, x_ref[...], g_ref[...])
        dx_ref[...] = dov + dx
        dg_ref[...] += dg

    row = pl.BlockSpec((tm, D), lambda i: (i, 0))
    wide = pl.BlockSpec((tm, NB * fb), lambda i: (i, 0))
    vec = pl.BlockSpec((1, D), lambda i: (0, 0))
    once = pl.Buffered(1)
    return pl.pallas_call(
        body, name=name, grid=(T // tm,),
        in_specs=[row, row, wide, vec, pl.BlockSpec((NB, D, fb), lambda i: (0, 0, 0), pipeline_mode=once),
                  pl.BlockSpec((NB * fb, D), lambda i: (0, 0), pipeline_mode=once)],
        out_specs=[row, wide, row, vec],
        out_shape=[S((T, D), F32), S((T, NB * fb), BF16), S((T, D), BF16), S((1, D), F32)],
        compiler_params=_cp("arbitrary"),
    )(x, dout, a, g, wup, wdown)


def _adam_update(g, w, m, v):
    mn = ADAM_B1 * m + (1.0 - ADAM_B1) * g
    vn = ADAM_B2 * v + (1.0 - ADAM_B2) * (g * g)
    c1 = 1.0 - ADAM_B1 ** ADAM_STEP
    c2 = 1.0 - ADAM_B2 ** ADAM_STEP
    return (-ADAM_LR) * ((mn / c1) / (jnp.sqrt(vn / c2) + ADAM_EPS) + ADAM_WD * w), mn, vn


def _mlp_bwd_dw(h, da, a, dob, fb, name, adam_jobs=()):
    T, D = h.shape
    F = a.shape[1]
    NB = F // fb
    tk = min(T, 1024)
    nk = T // tk
    nj = len(adam_jobs)

    def body(h_ref, da_ref, a_ref, dob_ref, *rest):
        dwu_ref, dwd_ref = rest[4 * nj:4 * nj + 2]
        au = ad = None
        for k in range(nk):
            rows = slice(k * tk, (k + 1) * tk)
            pu = _dot_tn(h_ref[rows, :], da_ref[rows, :])
            u = jnp.maximum(a_ref[rows, :].astype(F32), 0.0)
            pd = _dot_tn((u * u).astype(BF16), dob_ref[rows, :])
            au, ad = (pu, pd) if au is None else (au + pu, ad + pd)
        dwu_ref[...] = au.astype(BF16)
        dwd_ref[...] = ad.astype(BF16)
        for q in range(nj):
            p_ref, w_ref, m_ref, v_ref = rest[4 * q:4 * q + 4]
            g_out, d_out, m_out, v_out = rest[4 * nj + 2 + 4 * q:4 * nj + 6 + 4 * q]
            g = p_ref[0].astype(F32)
            for i in range(1, p_ref.shape[0]):
                g = g + p_ref[i].astype(F32)
            d, mn, vn = _adam_update(g, w_ref[...], m_ref[...], v_ref[...])
            g_out[...], d_out[...], m_out[...], v_out[...] = g, d, mn, vn

    once = pl.Buffered(1)
    in_specs = [pl.BlockSpec((T, D), lambda j: (0, 0), pipeline_mode=once),
                pl.BlockSpec((T, fb), lambda j: (0, j)),
                pl.BlockSpec((T, fb), lambda j: (0, j)),
                pl.BlockSpec((T, D), lambda j: (0, 0), pipeline_mode=once)]
    out_specs = [pl.BlockSpec((None, D, fb), lambda j: (j, 0, 0)), pl.BlockSpec((fb, D), lambda j: (j, 0))]
    out_shape = [S((NB, D, fb), BF16), S((F, D), BF16)]
    operands = [h, da, a, dob]
    for parts, w3, m3, v3, layer in adam_jobs:
        P, R, C = parts.shape
        blk = pl.BlockSpec((None, R // NB, C), lambda j, layer=layer: (layer, j, 0))
        in_specs += [pl.BlockSpec((P, R // NB, C), lambda j: (0, j, 0)), blk, blk, blk]
        out_specs += [blk] * 4
        out_shape += [S(w3.shape, F32)] * 4
        operands += [parts, w3, m3, v3]
    outs = pl.pallas_call(
        body, name=name, grid=(NB,),
        in_specs=in_specs, out_specs=out_specs, out_shape=out_shape,
        compiler_params=_cp("parallel"),
    )(*operands)
    return outs[0], outs[1], [outs[2 + 4 * q:6 + 4 * q] for q in range(nj)]


def _rope_tables(L):
    nf = HEAD_DIM // 4
    t = jnp.arange(L, dtype=jnp.int32)
    row = (t // GRID_W).astype(F32)
    col = (t % GRID_W).astype(F32)
    inv = ROPE_THETA ** (-jnp.arange(nf, dtype=F32) / nf)
    ar = row[:, None] * inv
    ac = col[:, None] * inv
    cos = jnp.concatenate([jnp.cos(ar), jnp.cos(ar), jnp.cos(ac), jnp.cos(ac)], axis=-1)
    sin = jnp.concatenate([-jnp.sin(ar), jnp.sin(ar), -jnp.sin(ac), jnp.sin(ac)], axis=-1)
    return cos, sin


def _swap32(x):
    lane = lax.broadcasted_iota(jnp.int32, x.shape, 1)
    up = pltpu.roll(x, HEAD_DIM - 32, 1)
    down = pltpu.roll(x, 32, 1)
    return jnp.where((lane % 64) < 32, up, down)


def _qkv_proj(x, g, wt_blk, qg, kg, cos, sin, L, name):
    T, D = x.shape
    NB, nb, _ = wt_blk.shape
    W = NB * nb
    nh = W // HEAD_DIM - 2 * N_KV
    tm = min(L, 512)
    lb = L // tm

    def body(x_ref, g_ref, w_ref, qg_ref, kg_ref, cos_ref, sin_ref, qkv_ref, h_ref, q_ref, k_ref, v_ref):
        for ch in range(PROJ_CHAINS):
            rows = slice(ch * tm // PROJ_CHAINS, (ch + 1) * tm // PROJ_CHAINS)
            xv = x_ref[rows, :]
            hb = (xv * _rms_r(xv) * g_ref[...]).astype(BF16)
            h_ref[rows, :] = hb
            for j in range(NB):
                qkv_ref[rows, j * nb:(j + 1) * nb] = _dot_nt(hb, w_ref[j])
            c = cos_ref[rows, :]
            s = sin_ref[rows, :]
            for h in range(nh + N_KV):
                xh = qkv_ref[rows, h * HEAD_DIM:(h + 1) * HEAD_DIM]
                gv = qg_ref[...] if h < nh else kg_ref[...]
                y = xh * _rms_r(xh) * gv
                y = (y * c + _swap32(y) * s).astype(BF16)
                if h < nh:
                    q_ref[rows, h * HEAD_DIM:(h + 1) * HEAD_DIM] = y
                else:
                    k_ref[rows, (h - nh) * HEAD_DIM:(h - nh + 1) * HEAD_DIM] = y
            v_ref[rows, :] = qkv_ref[rows, (nh + N_KV) * HEAD_DIM:].astype(BF16)

    row = lambda cols: pl.BlockSpec((tm, cols), lambda i: (i, 0))
    vec = lambda cols: pl.BlockSpec((1, cols), lambda i: (0, 0))
    table = pl.BlockSpec((tm, HEAD_DIM), lambda i: (i % lb, 0))
    return pl.pallas_call(
        body, name=name, grid=(T // tm,),
        in_specs=[row(D), vec(D), pl.BlockSpec(wt_blk.shape, lambda i: (0, 0, 0)), vec(HEAD_DIM), vec(HEAD_DIM), table, table],
        out_specs=[row(W), row(D), row(nh * HEAD_DIM), row(N_KV * HEAD_DIM), row(N_KV * HEAD_DIM)],
        out_shape=[S((T, W), F32), S((T, D), BF16), S((T, nh * HEAD_DIM), BF16), S((T, N_KV * HEAD_DIM), BF16),
                   S((T, N_KV * HEAD_DIM), BF16)],
        compiler_params=_cp("parallel"),
    )(x, g, wt_blk, qg, kg, cos, sin)


def _qk_prep_bwd(qkv, dq, dk, dv, qg, kg, cos, sin, L, name):
    T, W = qkv.shape
    nh = W // HEAD_DIM - 2 * N_KV
    tm = min(L, 512)
    lb = L // tm

    def body(qkv_ref, dq_ref, dk_ref, dv_ref, qg_ref, kg_ref, cos_ref, sin_ref, dz_ref, dqg_ref, dkg_ref):
        @pl.when(pl.program_id(0) == 0)
        def _():
            dqg_ref[...] = jnp.zeros_like(dqg_ref)
            dkg_ref[...] = jnp.zeros_like(dkg_ref)

        c = cos_ref[...]
        s = sin_ref[...]
        for h in range(nh + N_KV):
            cols = slice(h * HEAD_DIM, (h + 1) * HEAD_DIM)
            if h < nh:
                dout, gv, dg_ref = dq_ref[:, cols], qg_ref[...], dqg_ref
            else:
                kc = slice((h - nh) * HEAD_DIM, (h - nh + 1) * HEAD_DIM)
                dout, gv, dg_ref = dk_ref[:, kc], kg_ref[...], dkg_ref
            dy = dout * c - _swap32(dout) * s
            dx, dg = _rms_bwd(dy, qkv_ref[:, cols], gv)
            dg_ref[...] += dg
            dz_ref[:, cols] = dx.astype(BF16)
        dz_ref[:, (nh + N_KV) * HEAD_DIM:] = dv_ref[...].astype(BF16)

    return pl.pallas_call(
        body, name=name, grid=(T // tm,),
        in_specs=[pl.BlockSpec((tm, W), lambda i: (i, 0)),
                  pl.BlockSpec((tm, nh * HEAD_DIM), lambda i: (i, 0)),
                  pl.BlockSpec((tm, N_KV * HEAD_DIM), lambda i: (i, 0)),
                  pl.BlockSpec((tm, N_KV * HEAD_DIM), lambda i: (i, 0)),
                  pl.BlockSpec((1, HEAD_DIM), lambda i: (0, 0)),
                  pl.BlockSpec((1, HEAD_DIM), lambda i: (0, 0)),
                  pl.BlockSpec((tm, HEAD_DIM), lambda i: (i % lb, 0)),
                  pl.BlockSpec((tm, HEAD_DIM), lambda i: (i % lb, 0))],
        out_specs=[pl.BlockSpec((tm, W), lambda i: (i, 0)),
                   pl.BlockSpec((1, HEAD_DIM), lambda i: (0, 0)),
                   pl.BlockSpec((1, HEAD_DIM), lambda i: (0, 0))],
        out_shape=[S((T, W), BF16), S((1, HEAD_DIM), F32), S((1, HEAD_DIM), F32)],
        compiler_params=_cp("arbitrary"),
    )(qkv, dq, dk, dv, qg, kg, cos, sin)


EXP2_SCALE = SOFTMAX_SCALE * math.log2(math.e)
ATTN_SUB = 256
ATTN_TQ = 1024
ATTN_BWD_TQ = 2048


def _softmax_rows(q, k):
    s = _dot_nt(q, k)
    e = jnp.exp2((s - jnp.max(s, axis=-1, keepdims=True)) * EXP2_SCALE)
    return e, jnp.sum(e, axis=-1, keepdims=True)


def _attn_fwd(q, k, v, L, name):
    T = q.shape[0]
    nh = q.shape[1] // HEAD_DIM
    G = nh // N_KV
    B = T // L
    tq = min(L, ATTN_TQ)
    nq = L // tq
    sub = min(tq, ATTN_SUB)

    def body(q_ref, k_ref, v_ref, o_ref):
        for h in range(tq // sub):
            rows = slice(h * sub, (h + 1) * sub)
            e, l = _softmax_rows(q_ref[rows, :], k_ref[...])
            o_ref[rows, :] = (_dot(e.astype(BF16), v_ref[...]) / l).astype(BF16)

    qspec = pl.BlockSpec((tq, HEAD_DIM), lambda b, kv, g, qi: (b * nq + qi, kv * G + g))
    kspec = pl.BlockSpec((L, HEAD_DIM), lambda b, kv, g, qi: (b, kv))
    return pl.pallas_call(
        body, name=name, grid=(B, N_KV, G, nq),
        in_specs=[qspec, kspec, kspec],
        out_specs=qspec,
        out_shape=S((T, nh * HEAD_DIM), BF16),
        compiler_params=_cp("parallel", "parallel", "parallel", "parallel"),
    )(q, k, v)


def _attn_bwd(q, k, v, do, o, L, name):
    T = q.shape[0]
    nh = q.shape[1] // HEAD_DIM
    G = nh // N_KV
    B = T // L
    tq = min(L, ATTN_BWD_TQ)
    nq = L // tq
    sub = min(tq, ATTN_SUB)

    def body(q_ref, k_ref, v_ref, do_ref, o_ref, dq_ref, dk_ref, dv_ref, ds_scr, p_scr):
        first = (pl.program_id(2) == 0) & (pl.program_id(3) == 0)
        last = (pl.program_id(2) == G - 1) & (pl.program_id(3) == nq - 1)

        @pl.when(first)
        def _():
            dk_ref[...] = jnp.zeros_like(dk_ref)
            dv_ref[...] = jnp.zeros_like(dv_ref)

        for h in range(tq // sub):
            rows = slice(h * sub, (h + 1) * sub)
            dov = do_ref[rows, :]
            e, l = _softmax_rows(q_ref[rows, :], k_ref[...])
            p = e * (1.0 / l)
            dsum = jnp.sum(dov.astype(F32) * o_ref[rows, :].astype(F32), axis=-1, keepdims=True)
            ds_scr[rows, :] = (p * (_dot_nt(dov, v_ref[...]) - dsum)).astype(BF16)
            p_scr[rows, :] = p.astype(BF16)
        ds = ds_scr[...]
        dq_ref[...] = _dot(ds, k_ref[...]) * SOFTMAX_SCALE
        dk_ref[...] += _dot_tn(ds, q_ref[...])
        dv_ref[...] += _dot_tn(p_scr[...], do_ref[...])

        @pl.when(last)
        def _():
            dk_ref[...] = dk_ref[...] * SOFTMAX_SCALE

    qspec = pl.BlockSpec((tq, HEAD_DIM), lambda b, kv, g, qi: (b * nq + qi, kv * G + g))
    kspec = pl.BlockSpec((L, HEAD_DIM), lambda b, kv, g, qi: (b, kv))
    return pl.pallas_call(
        body, name=name, grid=(B, N_KV, G, nq),
        in_specs=[qspec, kspec, kspec, qspec, qspec],
        out_specs=[qspec, kspec, kspec],
        out_shape=[S((T, nh * HEAD_DIM), F32), S((T, N_KV * HEAD_DIM), F32), S((T, N_KV * HEAD_DIM), F32)],
        scratch_shapes=[pltpu.VMEM((tq, L), BF16), pltpu.VMEM((tq, L), BF16)],
        compiler_params=_cp("parallel", "parallel", "arbitrary", "arbitrary"),
    )(q, k, v, do, o)


PV_CONV_W = 0
PV_B_A = 8
PV_B_X = 16
PV_LAM = 24
PV_CONV_B = 32
PV_ROWS = 40


def _shift_rows(x, k):
    if k == 0:
        return x
    L = x.shape[0]
    n = N_SEG * abs(k)
    seg = lax.broadcasted_iota(jnp.int32, (n, x.shape[1]), 0) % N_SEG
    if k > 0:
        edge = jnp.where(seg == 0, 0.0, pltpu.roll(x[L - n:], 1, 0))
        return jnp.concatenate([edge, x[:L - n]], axis=0)
    edge = jnp.where(seg == N_SEG - 1, 0.0, pltpu.roll(x[:n], n - 1, 0))
    return jnp.concatenate([x[n:], edge], axis=0)


def _conv_taps(rec, pv):
    c = pv[PV_CONV_B:PV_CONV_B + 1]
    for j in range(CONV_W):
        c = c + pv[PV_CONV_W + j:PV_CONV_W + j + 1] * _shift_rows(rec, 2 - j)
    return c


def _sigmoid(x):
    return 0.5 * jnp.tanh(0.5 * x) + 0.5


EXPM1_SERIES_BELOW = 0.03


def _rg_gates(c, cbf, wa, wx, ba, bx, lam):
    r = _sigmoid(_dot(cbf, wa) + ba)
    i = _sigmoid(_dot(cbf, wx) + bx)
    sp = jnp.maximum(-lam, 0.0) + jnp.log1p(jnp.exp(-jnp.abs(lam)))
    la = r * ((-RG_C) * sp)
    a = jnp.exp(la)
    a2 = a * a
    x = la + la
    series = -(x * ((x * (1.0 / 6.0) + 0.5) * x + 1.0))
    om = jnp.where(x > -EXPM1_SERIES_BELOW, series, 1.0 - a2)
    rm = lax.rsqrt(om)
    return r, i, a, om * rm, rm, a2, sp


def _gelu(x):
    t = jnp.tanh(GELU_K * (x + GELU_C * x * x * x))
    return 0.5 * x * (1.0 + t), t


def _scan_pair(af_ref, uf_ref, ab_ref, ub_ref, hf_ref, hb_ref, pf_ref, pb_ref, L):
    ls = L // N_SEG
    zero = jnp.zeros((N_SEG, LRU_BW), F32)
    one = jnp.ones((N_SEG, LRU_BW), F32)
    tile = lambda t: pl.ds(pl.multiple_of(t * N_SEG, N_SEG), N_SEG)

    def steps(tc, carry):
        hf, pf, hb, pb = carry
        for q in range(SCAN_UNROLL):
            t = tc * SCAN_UNROLL + q
            rf, rb = tile(t), tile(ls - 1 - t)
            af = af_ref[rf, :]
            hf = af * hf + uf_ref[rf, :]
            pf = pf * af
            hf_ref[rf, :] = hf
            pf_ref[rf, :] = pf
            ab = ab_ref[rb, :]
            hb = ab * hb + ub_ref[rb, :]
            pb = pb * ab
            hb_ref[rb, :] = hb
            pb_ref[rb, :] = pb
        return hf, pf, hb, pb

    hf_e, pf_e, hb_e, pb_e = lax.fori_loop(0, ls // SCAN_UNROLL, steps, (zero, one, zero, one))

    rows, cin = [], jnp.zeros((1, LRU_BW), F32)
    for s in range(N_SEG):
        rows.append(cin)
        cin = hf_e[s:s + 1] + pf_e[s:s + 1] * cin
    cf = jnp.concatenate(rows, axis=0)
    rows, cin = [], jnp.zeros((1, LRU_BW), F32)
    for s in reversed(range(N_SEG)):
        rows.append(cin)
        cin = hb_e[s:s + 1] + pb_e[s:s + 1] * cin
    cb = jnp.concatenate(rows[::-1], axis=0)

    def fix(tc, _):
        for q in range(SCAN_UNROLL):
            r = tile(tc * SCAN_UNROLL + q)
            hf_ref[r, :] = hf_ref[r, :] + pf_ref[r, :] * cf
            hb_ref[r, :] = hb_ref[r, :] + pb_ref[r, :] * cb
        return 0

    lax.fori_loop(0, ls // SCAN_UNROLL, fix, 0)


def _rg_specs(L, D, nblk):
    slab = lambda off: pl.BlockSpec((L, LRU_BW), lambda cb, b: (b, off + cb))
    wspec = pl.BlockSpec((2, None, LRU_BW, LRU_BW), lambda cb, b: (0, cb, 0, 0))
    pvspec = pl.BlockSpec((PV_ROWS, LRU_BW), lambda cb, b: (0, cb))
    return slab, wspec, pvspec


def _rg_fwd(z, pvec, wa, wx, L, name):
    T, C2 = z.shape
    C = C2 // 2
    nblk = C // LRU_BW
    B = T // L
    slab, wspec, pvspec = _rg_specs(L, C, nblk)

    def body(gp_ref, rec_ref, pv_ref, wa_ref, wx_ref, yg_ref, hf_ref, hb_ref, a_scr, u_scr, p_scr):
        pv = pv_ref[...]
        c = _conv_taps(rec_ref[...], pv)
        cbf = c.astype(BF16)
        for d in range(2):
            _, i, a, m, _, _, _ = _rg_gates(c, cbf, wa_ref[d], wx_ref[d], pv[PV_B_A + d:PV_B_A + d + 1],
                                      pv[PV_B_X + d:PV_B_X + d + 1], pv[PV_LAM + d:PV_LAM + d + 1])
            a_scr[d] = a
            u_scr[d] = m * (i * c)
        _scan_pair(a_scr.at[0], u_scr.at[0], a_scr.at[1], u_scr.at[1], hf_ref, hb_ref, p_scr.at[0], p_scr.at[1], L)
        gate, _ = _gelu(gp_ref[...])
        yg_ref[...] = ((hf_ref[...] + hb_ref[...]) * gate).astype(BF16)

    return pl.pallas_call(
        body, name=name, grid=(nblk, B),
        in_specs=[slab(0), slab(nblk), pvspec, wspec, wspec],
        out_specs=[slab(0), slab(0), slab(0)],
        out_shape=[S((T, C), BF16), S((T, C), F32), S((T, C), F32)],
        scratch_shapes=[pltpu.VMEM((2, L, LRU_BW), F32)] * 3,
        compiler_params=_cp("parallel", "parallel"),
    )(z, z, pvec, wa, wx)


def _rg_bwd(z, hf, hb, dyg, pvec, wa, wx, L, name):
    T, C2 = z.shape
    C = C2 // 2
    nblk = C // LRU_BW
    B = T // L
    slab, wspec, pvspec = _rg_specs(L, C, nblk)

    def body(gp_ref, rec_ref, hf_ref, hb_ref, dyg_ref, pv_ref, wa_ref, wx_ref,
             dz_ref, dwa_ref, dwx_ref, dpv_ref, a_scr, u_scr, d_scr, p_scr):
        @pl.when(pl.program_id(1) == 0)
        def _():
            dwa_ref[...] = jnp.zeros_like(dwa_ref)
            dwx_ref[...] = jnp.zeros_like(dwx_ref)
            dpv_ref[...] = jnp.zeros_like(dpv_ref)

        pv = pv_ref[...]
        rec = rec_ref[...]
        c = _conv_taps(rec, pv)
        cbf = c.astype(BF16)
        gp = gp_ref[...]
        gate, th = _gelu(gp)
        dgelu = 0.5 * (1.0 + th) + 0.5 * gp * (1.0 - th * th) * GELU_K * (1.0 + 3.0 * GELU_C * gp * gp)
        dyg = dyg_ref[...]
        dz_ref[0] = (dyg * (hf_ref[...] + hb_ref[...]) * dgelu).astype(BF16)
        dy = dyg * gate

        gates = []
        for d in range(2):
            gates.append(_rg_gates(c, cbf, wa_ref[d], wx_ref[d], pv[PV_B_A + d:PV_B_A + d + 1],
                                   pv[PV_B_X + d:PV_B_X + d + 1], pv[PV_LAM + d:PV_LAM + d + 1]))
        a_scr[0] = _shift_rows(gates[1][2], 1)
        a_scr[1] = _shift_rows(gates[0][2], -1)
        u_scr[...] = dy
        _scan_pair(a_scr.at[0], u_scr, a_scr.at[1], u_scr, d_scr.at[1], d_scr.at[0], p_scr.at[0], p_scr.at[1], L)

        dc = jnp.zeros_like(c)
        rows = []
        for d in range(2):
            r, i, a, m, rm, a2, sp = gates[d]
            delta = d_scr[d]
            hnb = _shift_rows(hf_ref[...], 1) if d == 0 else _shift_rows(hb_ref[...], -1)
            da = delta * hnb
            dm = delta * (i * c)
            di = delta * (m * c)
            dc = dc + delta * (m * i)
            dla = da * a - dm * (a2 * rm)
            dpa = (dla * ((-RG_C) * sp)) * (r * (1.0 - r))
            dpx = di * (i * (1.0 - i))
            dsp = (-RG_C) * jnp.sum(dla * r, axis=0, keepdims=True)
            lam = pv[PV_LAM + d:PV_LAM + d + 1]
            rows.append((jnp.sum(dpa, axis=0, keepdims=True), jnp.sum(dpx, axis=0, keepdims=True),
                         -dsp * _sigmoid(-lam)))
            dpab = dpa.astype(BF16)
            dpxb = dpx.astype(BF16)
            dwa_ref[d] += _dot_tn(cbf, dpab)
            dwx_ref[d] += _dot_tn(cbf, dpxb)
            dc = dc + _dot_nt(dpab, wa_ref[d]) + _dot_nt(dpxb, wx_ref[d])

        drec = jnp.zeros_like(c)
        dcw = []
        for j in range(CONV_W):
            drec = drec + pv[PV_CONV_W + j:PV_CONV_W + j + 1] * _shift_rows(dc, j - 2)
            dcw.append(jnp.sum(dc * _shift_rows(rec, 2 - j), axis=0, keepdims=True))
        dz_ref[1] = drec.astype(BF16)
        for j in range(CONV_W):
            dpv_ref[PV_CONV_W + j:PV_CONV_W + j + 1, :] += dcw[j]
        for d in range(2):
            dpv_ref[PV_B_A + d:PV_B_A + d + 1, :] += rows[d][0]
            dpv_ref[PV_B_X + d:PV_B_X + d + 1, :] += rows[d][1]
            dpv_ref[PV_LAM + d:PV_LAM + d + 1, :] += rows[d][2]
        dpv_ref[PV_CONV_B:PV_CONV_B + 1, :] += jnp.sum(dc, axis=0, keepdims=True)

    return pl.pallas_call(
        body, name=name, grid=(nblk, B),
        in_specs=[slab(0), slab(nblk), slab(0), slab(0), slab(0), pvspec, wspec, wspec],
        out_specs=[pl.BlockSpec((2, L, LRU_BW), lambda cb, b: (0, b, cb)), wspec, wspec, pvspec],
        out_shape=[S((2, T, C), BF16), S((2, nblk, LRU_BW, LRU_BW), F32), S((2, nblk, LRU_BW, LRU_BW), F32),
                   S((PV_ROWS, C), F32)],
        scratch_shapes=[pltpu.VMEM((2, L, LRU_BW), F32), pltpu.VMEM((L, LRU_BW), F32),
                        pltpu.VMEM((2, L, LRU_BW), F32), pltpu.VMEM((2, L, LRU_BW), F32)],
        compiler_params=_cp("parallel", "arbitrary"),
    )(z, z, hf, hb, dyg, pvec, wa, wx)


QKV_NB = 512


def _interleave(a):
    *lead, L, D = a.shape
    return a.reshape(*lead, N_SEG, L // N_SEG, D).swapaxes(-3, -2).reshape(*lead, L, D)


def _deinterleave(a):
    *lead, L, D = a.shape
    return a.reshape(*lead, L // N_SEG, N_SEG, D).swapaxes(-3, -2).reshape(*lead, L, D)


def _local_step(x3, tgt3, w, fetch, send, early_adam):
    Bl, L, D = x3.shape
    T = Bl * L
    x = _interleave(x3).reshape(T, D)
    tgt = _interleave(tgt3).reshape(T, D)
    gm = [w["g_mix"][i:i + 1] for i in range(2)]
    gl = [w["g_mlp"][i:i + 1] for i in range(2)]

    w0 = fetch(0, ())
    nb_in = w0["w_in"].shape[-1]
    z, h0 = _norm_matmul(x, gm[0], w0["w_in"], "rg_in")
    yg, hf, hb = _rg_fwd(z, w0["pvec"], w["wa"], w["wx"], L, "rg_fwd")
    w1 = fetch(1, (yg,))
    x1 = _matmul_res(yg, w1["w_out"], x, "rg_out")
    w1.update(fetch(4, (x1,)))
    fb = w1["w_up0"].shape[-1]
    x2, a0, hm0 = _mlp_fwd(x1, gl[0], w1["w_up0"], w1["w_down0"], "mlp0_fwd")
    w2 = fetch(2, (x2,))
    cos, sin = [_interleave(t) for t in _rope_tables(L)]
    qkv, h1, qn, kn, vb = _qkv_proj(x2, gm[1], w2["w_qkv"], w["qg"], w["kg"], cos, sin, L, "at_qkv")
    o = _attn_fwd(qn, kn, vb, L, "at_fwd")
    x3_ = _matmul_res(o, w2["w_o"], x2, "at_out")
    w3 = fetch(3, (x3_,))
    dx4, a1, hm1, loss, dgf = _mlp_fwd(x3_, gl[1], w3["w_up1"], w3["w_down1"], "mlp1_fwd", head=(tgt, w["g_fin"]))

    dx3, da1, dob1, dgl1 = _mlp_bwd_dx(x3_, dx4, a1, gl[1], w3["w_up1"], w3["w_down1"], "mlp1_bwd_dx")
    dwu1, dwd1, _ = _mlp_bwd_dw(hm1, da1, a1, dob1, fb, "mlp1_bwd_dw")
    sent = send(3, dict(w_up1=dwu1, w_down1=dwd1))
    do, dx3b = _matmul_nt(dx3, w2["w_o"], "at_out_bwd", BF16, after=sent)
    dwo = _matmul_tn(o, dx3b[None], QKV_NB, "at_out_dw", blocked=False)
    dq, dk, dv = _attn_bwd(qn, kn, vb, do, o, L, "at_bwd")
    dqkv, dqg, dkg = _qk_prep_bwd(qkv, dq, dk, dv, w["qg"], w["kg"], cos, sin, L, "at_prep_bwd")
    dwqkv = _matmul_tn(dqkv, h1[None], QKV_NB, "at_qkv_dw", blocked=False)
    sent = send(2, dict(w_qkv=dwqkv, w_o=dwo))
    dx2, dgm1 = _nt_normbwd(dqkv[None], w2["w_qkv"], x2, gm[1], dx3, "at_qkv_bwd", after=sent, transposed=True)
    dx1, da0, dob0, dgl0 = _mlp_bwd_dx(x1, dx2, a0, gl[0], w1["w_up0"], w1["w_down0"], "mlp0_bwd_dx")
    jobs, sink = early_adam((dob0,))
    dwu0, dwd0, early = _mlp_bwd_dw(hm0, da0, a0, dob0, fb, "mlp0_bwd_dw", jobs)
    sink(early)
    sent = send(1, dict(w_up0=dwu0, w_down0=dwd0))
    dyg, dx1b = _matmul_nt(dx1, w1["w_out"], "rg_out_bwd", F32, after=sent)
    dwout = _matmul_tn(yg, dx1b[None], QKV_NB, "rg_out_dw", blocked=False)
    dz, dwa, dwx, dpv = _rg_bwd(z, hf, hb, dyg, w0["pvec"], w["wa"], w["wx"], L, "rg_bwd")
    sent = send(4, dict(w_out=dwout, pvec=dpv, wa=dwa, wx=dwx))
    dwin = _matmul_tn(h0, dz, nb_in, "rg_in_dw", blocked=True, after=sent)
    sent = send(0, dict(w_in=dwin))
    dx0, dgm0 = _nt_normbwd(dz, w0["w_in"], x, gm[0], dx1, "rg_in_bwd", after=sent)
    send(-1, dict(g_mix=[dgm0, dgm1], g_mlp=[dgl0, dgl1], g_fin=dgf, conv_b=dpv[PV_CONV_B:PV_CONV_B + 1], qg=dqg, kg=dkg, loss=loss))
    return _deinterleave(dx0.reshape(Bl, L, D))


MESH = pl.DeviceIdType.MESH
ANY = pl.BlockSpec(memory_space=pl.ANY)
N_PEERS = N_DEV - 1


def _my_place():
    return lax.axis_index("x"), lax.axis_index("y"), lax.axis_index("c")


def _flat(px, py, pc):
    return 4 * px + 2 * py + pc


def _all_gather(shards, name):
    n = len(shards)

    def body(*refs):
        ins, outs = refs[:n], refs[n:2 * n]
        send_sems, recv_sems, local_sems = refs[2 * n:]
        x, y, c = _my_place()
        me, sibling = (x, y, c), (x, y, 1 - c)
        chips = [(1 - x, y), (x, 1 - y), (1 - x, 1 - y)]

        def copy(a, k, block, to, src=None):
            dst = outs[a].at[_flat(*block)]
            return pltpu.make_async_remote_copy(
                src_ref=dst if src is None else src, dst_ref=dst,
                send_sem=send_sems.at[a, k], recv_sem=recv_sems.at[a, k],
                device_id=to, device_id_type=MESH)

        mine = [pltpu.make_async_copy(ins[a], outs[a].at[_flat(*me)], local_sems.at[a]) for a in range(n)]
        for cp in mine:
            cp.start()
        first = []
        for a in range(n):
            first.append(copy(a, 0, me, sibling, src=ins[a]))
            first += [copy(a, 1 + j, me, (*chip, c), src=ins[a]) for j, chip in enumerate(chips)]
        for cp in first:
            cp.start()
        passed = []
        for j, chip in enumerate(chips):
            for a in range(n):
                copy(a, 1 + j, (*chip, c), me).wait_recv()
                fwd = copy(a, 4 + j, (*chip, c), sibling)
                fwd.start()
                passed.append(fwd)
        for a in range(n):
            copy(a, 0, sibling, me).wait_recv()
            for j, chip in enumerate(chips):
                copy(a, 4 + j, (*chip, 1 - c), me).wait_recv()
        for cp in first + passed:
            cp.wait_send()
        for cp in mine:
            cp.wait()

    return pl.pallas_call(
        body, name=name,
        in_specs=[ANY] * n, out_specs=[ANY] * n,
        out_shape=[S((N_DEV,) + s.shape, s.dtype) for s in shards],
        scratch_shapes=[pltpu.SemaphoreType.DMA((n, N_PEERS)), pltpu.SemaphoreType.DMA((n, N_PEERS)),
                        pltpu.SemaphoreType.DMA((n,))],
    )(*shards)


HBM = pl.BlockSpec(memory_space=pltpu.HBM)
SEM = pl.BlockSpec(memory_space=pltpu.SEMAPHORE)
SIDE_EFFECT = pltpu.SideEffectType.DATAFLOW_SIDE_EFFECTING
SEMS_PER_GROUP = 3


NEAR_PEERS = (1, 2, 4, 6)
FAR_CHIPS = (2, 4, 6)


def _exchange_copies(srcs, lands, sems, mode):
    send_sems, recv_sems, local_sems = sems
    scatter = mode == "scatter"
    x, y, c = _my_place()
    me = _flat(x, y, c)
    remote, local = [], []
    for a in range(len(srcs)):
        for r in (NEAR_PEERS if mode == "near" else range(1, N_DEV)):
            peer = (1 - x if r & 4 else x, 1 - y if r & 2 else y, 1 - c if r & 1 else c)
            remote.append(pltpu.make_async_remote_copy(
                src_ref=srcs[a].at[_flat(*peer)] if scatter else srcs[a], dst_ref=lands[a].at[me],
                send_sem=send_sems.at[a * N_PEERS + r - 1], recv_sem=recv_sems.at[a * N_PEERS + r - 1],
                device_id=peer, device_id_type=MESH))
        local.append(pltpu.make_async_copy(srcs[a].at[me] if scatter else srcs[a], lands[a].at[me], local_sems.at[a]))
    return remote, local


def _exchange_start(groups, modes, name):
    sizes = [len(g) for g in groups]
    srcs = [pltpu.with_memory_space_constraint(a, pltpu.HBM) for g in groups for a in g]
    n = len(srcs)
    scatter_of = [m == "scatter" for g, m in zip(groups, modes) for _ in g]
    lands = [pltpu.with_memory_space_constraint(lax.empty(a.shape if sc else (N_DEV,) + a.shape, a.dtype), pltpu.HBM)
             for a, sc in zip(srcs, scatter_of)]
    n_sem = SEMS_PER_GROUP * len(groups)

    def body(*refs):
        src_refs, land_refs, sem_refs, token = refs[:n], refs[n:2 * n], refs[2 * n:2 * n + n_sem], refs[-1]
        off = 0
        for gi, k in enumerate(sizes):
            remote, local = _exchange_copies(src_refs[off:off + k], land_refs[off:off + k],
                                             sem_refs[SEMS_PER_GROUP * gi:SEMS_PER_GROUP * (gi + 1)], modes[gi])
            for cp in local + remote:
                cp.start()
            off += k
        token[...] = jnp.zeros_like(token)

    sem_shapes = []
    for k in sizes:
        sem_shapes += [pltpu.SemaphoreType.DMA((k * N_PEERS,)), pltpu.SemaphoreType.DMA((k * N_PEERS,)),
                       pltpu.SemaphoreType.DMA((k,))]
    outs = pl.pallas_call(
        body, name=name,
        out_shape=sem_shapes + [pltpu.HBM(a.shape, a.dtype) for a in srcs + lands] + [S((8, 128), F32)],
        in_specs=[HBM] * (2 * n),
        out_specs=[SEM] * n_sem + [HBM] * (2 * n) + [pl.BlockSpec(memory_space=pltpu.VMEM)],
        input_output_aliases={i: n_sem + i for i in range(2 * n)},
        compiler_params=pltpu.CompilerParams(has_side_effects=SIDE_EFFECT),
    )(*srcs, *lands)
    sems, thru, token = outs[:n_sem], outs[n_sem:n_sem + 2 * n], outs[-1]
    per_group, off = [], 0
    for gi, k in enumerate(sizes):
        per_group.append((sems[SEMS_PER_GROUP * gi:SEMS_PER_GROUP * (gi + 1)], thru[off:off + k], thru[n + off:n + off + k]))
        off += k
    return per_group, token


def _exchange_wait(group, after, mode, name):
    sems, srcs, lands = group
    k = len(srcs)

    def body(*refs):
        remote, local = _exchange_copies(refs[:k], refs[k:2 * k], refs[2 * k:2 * k + SEMS_PER_GROUP], mode)
        for cp in remote:
            cp.wait_send()
            cp.wait_recv()
        for cp in local:
            cp.wait()

    outs = pl.pallas_call(
        body, name=name,
        out_shape=[pltpu.HBM(a.shape, a.dtype) for a in list(srcs) + list(lands)],
        in_specs=[HBM] * (2 * k) + [SEM] * SEMS_PER_GROUP + [ANY] * len(after),
        out_specs=[HBM] * (2 * k),
        input_output_aliases={i: i for i in range(2 * k)},
        compiler_params=pltpu.CompilerParams(has_side_effects=SIDE_EFFECT),
    )(*srcs, *lands, *sems, *after)
    return outs[k:]


def _forward_copies(lands, sems):
    send_sems, recv_sems = sems
    x, y, c = _my_place()
    mine, theirs = [], []
    for a in range(len(lands)):
        for k, r in enumerate(FAR_CHIPS):
            px, py = (1 - x if r & 4 else x), (1 - y if r & 2 else y)
            for out, core in ((mine, c), (theirs, 1 - c)):
                blk = lands[a].at[_flat(px, py, core)]
                out.append(pltpu.make_async_remote_copy(
                    src_ref=blk, dst_ref=blk, send_sem=send_sems.at[a * len(FAR_CHIPS) + k],
                    recv_sem=recv_sems.at[a * len(FAR_CHIPS) + k], device_id=(x, y, 1 - c), device_id_type=MESH))
    return mine, theirs


def _forward_start(groups, name):
    sizes = [len(g) for g in groups]
    lands = [a for g in groups for a in g]
    n = len(lands)
    n_sem = 2 * len(groups)

    def body(*refs):
        land_refs, sem_refs, token = refs[:n], refs[n:n + n_sem], refs[-1]
        off = 0
        for gi, k in enumerate(sizes):
            mine, _ = _forward_copies(land_refs[off:off + k], sem_refs[2 * gi:2 * gi + 2])
            for cp in mine:
                cp.start()
            off += k
        token[...] = jnp.zeros_like(token)

    sem_shapes = []
    for k in sizes:
        sem_shapes += [pltpu.SemaphoreType.DMA((k * len(FAR_CHIPS),))] * 2
    outs = pl.pallas_call(
        body, name=name,
        out_shape=sem_shapes + [pltpu.HBM(a.shape, a.dtype) for a in lands] + [S((8, 128), F32)],
        in_specs=[HBM] * n,
        out_specs=[SEM] * n_sem + [HBM] * n + [pl.BlockSpec(memory_space=pltpu.VMEM)],
        input_output_aliases={i: n_sem + i for i in range(n)},
        compiler_params=pltpu.CompilerParams(has_side_effects=SIDE_EFFECT),
    )(*lands)
    per_group, off = [], 0
    for gi, k in enumerate(sizes):
        per_group.append((outs[2 * gi:2 * gi + 2], outs[n_sem + off:n_sem + off + k]))
        off += k
    return per_group


def _forward_wait(group, after, name):
    sems, lands = group
    k = len(lands)

    def body(*refs):
        mine, theirs = _forward_copies(refs[:k], refs[k:k + 2])
        for cp in mine:
            cp.wait_send()
        for cp in theirs:
            cp.wait_recv()

    return pl.pallas_call(
        body, name=name,
        out_shape=[pltpu.HBM(a.shape, a.dtype) for a in lands],
        in_specs=[HBM] * k + [SEM] * 2 + [ANY] * len(after),
        out_specs=[HBM] * k,
        input_output_aliases={i: i for i in range(k)},
        compiler_params=pltpu.CompilerParams(has_side_effects=SIDE_EFFECT),
    )(*lands, *sems, *after)


def _row_tile(rows, cols):
    want = max(16, (128 * 1024) // cols)
    if rows <= want:
        return rows
    t = want - want % 16
    while rows % t:
        t -= 16
    return t


def _sum_parts(parts, name, after=()):
    P, R, C = parts.shape
    tr = _row_tile(R, C)

    def body(p_ref, *rest):
        o_ref = rest[-1]
        g = p_ref[0].astype(F32)
        for i in range(1, P):
            g = g + p_ref[i].astype(F32)
        o_ref[...] = g

    return pl.pallas_call(
        body, name=name, grid=(R // tr,),
        in_specs=[pl.BlockSpec((P, tr, C), lambda i: (0, i, 0))] + [ANY] * len(after),
        out_specs=pl.BlockSpec((tr, C), lambda i: (i, 0)),
        out_shape=S((R, C), F32),
        compiler_params=_cp("parallel"),
    )(parts, *after)


def _adamw(parts, w, m, v, name, after=()):
    P, R, C = parts.shape
    tr = _row_tile(R, C)
    c1 = 1.0 - ADAM_B1 ** ADAM_STEP
    c2 = 1.0 - ADAM_B2 ** ADAM_STEP

    def body(p_ref, w_ref, m_ref, v_ref, *rest):
        g_ref, d_ref, mo_ref, vo_ref = rest[len(after):]
        g = p_ref[0].astype(F32)
        for i in range(1, P):
            g = g + p_ref[i].astype(F32)
        mn = ADAM_B1 * m_ref[...] + (1.0 - ADAM_B1) * g
        vn = ADAM_B2 * v_ref[...] + (1.0 - ADAM_B2) * (g * g)
        g_ref[...] = g
        mo_ref[...] = mn
        vo_ref[...] = vn
        d_ref[...] = (-ADAM_LR) * ((mn / c1) / (jnp.sqrt(vn / c2) + ADAM_EPS) + ADAM_WD * w_ref[...])

    blk = pl.BlockSpec((tr, C), lambda i: (i, 0))
    return pl.pallas_call(
        body, name=name, grid=(R // tr,),
        in_specs=[pl.BlockSpec((P, tr, C), lambda i: (0, i, 0)), blk, blk, blk] + [ANY] * len(after),
        out_specs=[blk, blk, blk, blk],
        out_shape=[S((R, C), F32)] * 4,
        compiler_params=_cp("parallel"),
    )(parts, w, m, v, *after)


def _adamw_layer(parts, w3, m3, v3, layer, prev, name, after=()):
    P, R, C = parts.shape
    NL = w3.shape[0]
    tr = _row_tile(R, C)
    c1 = 1.0 - ADAM_B1 ** ADAM_STEP
    c2 = 1.0 - ADAM_B2 ** ADAM_STEP
    n_prev = 0 if prev is None else len(prev)

    def body(p_ref, w_ref, m_ref, v_ref, *rest):
        g_ref, d_ref, mo_ref, vo_ref = rest[n_prev + len(after):]
        g = p_ref[0].astype(F32)
        for i in range(1, P):
            g = g + p_ref[i].astype(F32)
        mn = ADAM_B1 * m_ref[...] + (1.0 - ADAM_B1) * g
        vn = ADAM_B2 * v_ref[...] + (1.0 - ADAM_B2) * (g * g)
        g_ref[...] = g
        mo_ref[...] = mn
        vo_ref[...] = vn
        d_ref[...] = (-ADAM_LR) * ((mn / c1) / (jnp.sqrt(vn / c2) + ADAM_EPS) + ADAM_WD * w_ref[...])

    blk = pl.BlockSpec((None, tr, C), lambda i: (layer, i, 0))
    return pl.pallas_call(
        body, name=name, grid=(R // tr,),
        in_specs=[pl.BlockSpec((P, tr, C), lambda i: (0, i, 0)), blk, blk, blk] + [ANY] * (n_prev + len(after)),
        out_specs=[blk, blk, blk, blk],
        out_shape=[S((NL, R, C), F32)] * 4,
        input_output_aliases={4 + k: k for k in range(n_prev)},
        compiler_params=_cp("parallel"),
    )(parts, w3, m3, v3, *(prev or ()), *after)


VMEM_WHOLE = pl.BlockSpec(memory_space=pltpu.VMEM)
LANES = 128


def _pack_vectors(vectors, starts, rows, name):
    def body(*refs):
        o_ref = refs[-1]
        o_ref[...] = jnp.zeros_like(o_ref)
        for v_ref, r0 in zip(refs[:-1], starts):
            for j in range(v_ref.shape[1] // LANES):
                o_ref[r0 + j:r0 + j + 1, :] = v_ref[:, j * LANES:(j + 1) * LANES]

    return pl.pallas_call(body, name=name, in_specs=[VMEM_WHOLE] * len(vectors), out_specs=VMEM_WHOLE,
                          out_shape=S((rows, LANES), F32))(*vectors)


def _adamw_vectors(g_pack, params, keep_rows, name, after=()):
    n = len(params)
    P = g_pack.shape[0]
    c1 = 1.0 - ADAM_B1 ** ADAM_STEP
    c2 = 1.0 - ADAM_B2 ** ADAM_STEP

    def body(g_ref, *refs):
        ins, outs = refs[:3 * n], refs[3 * n + len(after):]
        gs = g_ref[0]
        for i in range(1, P):
            gs = gs + g_ref[i]
        for pi, (_, _, _, slots) in enumerate(params):
            w_ref, m_ref, v_ref = ins[3 * pi:3 * pi + 3]
            g_out, d_out, m_out, v_out = outs[4 * pi:4 * pi + 4]
            for idx, row in slots:
                g = gs[row:row + 1, :]
                mn = ADAM_B1 * m_ref[idx] + (1.0 - ADAM_B1) * g
                vn = ADAM_B2 * v_ref[idx] + (1.0 - ADAM_B2) * (g * g)
                g_out[idx] = g
                m_out[idx] = mn
                v_out[idx] = vn
                d_out[idx] = (-ADAM_LR) * ((mn / c1) / (jnp.sqrt(vn / c2) + ADAM_EPS) + ADAM_WD * w_ref[idx])
        outs[-1][...] = jnp.concatenate([gs[r:r + 1, :] for r in keep_rows], axis=0)

    flat = [a for w, m, v, _ in params for a in (w, m, v)]
    out_shape = [S(w.shape, F32) for w, _, _, _ in params for _ in range(4)] + [S((len(keep_rows), LANES), F32)]
    outs = pl.pallas_call(
        body, name=name,
        in_specs=[VMEM_WHOLE] * (1 + len(flat)) + [ANY] * len(after),
        out_specs=[VMEM_WHOLE] * len(out_shape), out_shape=out_shape,
    )(g_pack, *flat, *after)
    return [outs[4 * i:4 * i + 4] for i in range(n)], outs[-1]


def _adamw_nd(parts, w, m, v, name, after=()):
    shp = w.shape
    C = shp[-1]
    outs = _adamw(parts.reshape(parts.shape[0], -1, C), w.reshape(-1, C), m.reshape(-1, C), v.reshape(-1, C), name, after)
    return [o.reshape(shp) for o in outs]


TILE_ROWS = 8


REP_SMALL_ROWS = 128
REP_GRAD_STARTS = (0, 8, 16, 24, 32, 40, 48, 56, 64)
REP_SMALL_STARTS = (0, 16, 32, 40, 48, 56)
REP_LOSS_ROW = 64


def _small_pack(cw, ba, bx, lam):
    pad8 = lambda a: jnp.pad(a, ((0, TILE_ROWS - a.shape[0]), (0, 0)))
    return jnp.concatenate([pad8(cw[0, :, 0, :]), pad8(ba[0]), pad8(bx[0]), pad8(lam[0]),
                            jnp.zeros((PV_ROWS - PV_CONV_B, LRU_BW), F32)], axis=0)


def kernel(x, norm_mix_g, norm_mlp_g, rg_w_in, rg_conv_w, rg_conv_b, rg_w_a, rg_b_a, rg_w_x, rg_b_x, rg_lam, rg_w_out, at_w_qkv, at_q_g, at_k_g, at_w_o, mlp_w_up, mlp_w_down, final_g, loss_target, m_norm_mix_g, m_norm_mlp_g, m_rg_w_in, m_rg_conv_w, m_rg_conv_b, m_rg_w_a, m_rg_b_a, m_rg_w_x, m_rg_b_x, m_rg_lam, m_rg_w_out, m_at_w_qkv, m_at_q_g, m_at_k_g, m_at_w_o, m_mlp_w_up, m_mlp_w_down, m_final_g, v_norm_mix_g, v_norm_mlp_g, v_rg_w_in, v_rg_conv_w, v_rg_conv_b, v_rg_w_a, v_rg_b_a, v_rg_w_x, v_rg_b_x, v_rg_lam, v_rg_w_out, v_at_w_qkv, v_at_q_g, v_at_k_g, v_at_w_o, v_mlp_w_up, v_mlp_w_down, v_final_g):
    D = x.shape[-1]
    bf = lambda a: a.astype(BF16)

    sp_w = _small_pack(rg_conv_w, rg_b_a, rg_b_x, rg_lam)
    started, _ = _exchange_start(
        [[bf(rg_w_in[0]), sp_w], [bf(rg_w_out[0])], [bf(mlp_w_up[0]), bf(mlp_w_down[0])],
         [bf(at_w_qkv[0]).T, bf(at_w_o[0])], [bf(mlp_w_up[1]), bf(mlp_w_down[1])]],
        ["near", "near", "near", "gather", "gather"], "gather_start")
    gathers = dict(zip((0, 1, 4, 2, 3), started))
    forwards = {}

    def fetch(stage, after):
        after = tuple(after)
        if stage == 0:
            got = _exchange_wait(gathers[0], after, "near", "gather_wait0")
            g_in, g_sp = _forward_wait(_forward_start([got], "forward_start0")[0], (), "forward_wait0")
            pvec = g_sp.transpose(1, 0, 2).reshape(PV_ROWS, D)
            pvec = jnp.concatenate([pvec[:PV_CONV_B], jnp.broadcast_to(rg_conv_b, (PV_ROWS - PV_CONV_B, D))], axis=0)
            return dict(w_in=g_in, pvec=pvec)
        if stage == 1:
            near = [_exchange_wait(gathers[s], after, "near", "gather_wait%d" % s) for s in (1, 4)]
            f_out, forwards[4] = _forward_start(near, "forward_start1")
            g_out, = _forward_wait(f_out, (), "forward_wait1")
            return dict(w_out=g_out.reshape(D, D))
        if stage == 4:
            g_up0, g_dn0 = _forward_wait(forwards[4], after, "forward_wait4")
            return dict(w_up0=g_up0, w_down0=g_dn0.reshape(-1, D))
        got = _exchange_wait(gathers[stage], after, "gather", "gather_wait%d" % stage)
        if stage == 2:
            return dict(w_qkv=got[0].reshape(-1, QKV_NB, D), w_o=got[1].reshape(D, D))
        return dict(w_up1=got[0], w_down1=got[1].reshape(-1, D))

    scatters = {}

    def send(stage, g):
        if stage == 3:
            arrs = [g["w_up1"], g["w_down1"].reshape(N_DEV, -1, D)]
        elif stage == 2:
            arrs = [g["w_qkv"].reshape(N_DEV, -1, D), g["w_o"].reshape(N_DEV, -1, D)]
        elif stage == 1:
            arrs = [g["w_up0"], g["w_down0"].reshape(N_DEV, -1, D)]
        elif stage == 4:
            arrs = [g["w_out"].reshape(N_DEV, -1, D), g["pvec"].reshape(PV_ROWS, N_DEV, LRU_BW).transpose(1, 0, 2),
                    bf(g["wa"]).reshape(N_DEV, -1, LANES), bf(g["wx"]).reshape(N_DEV, -1, LANES)]
        elif stage == 0:
            arrs = [g["w_in"]]
        else:
            small = _pack_vectors(g["g_mix"] + g["g_mlp"] + [g["g_fin"], g["conv_b"], g["qg"], g["kg"], g["loss"]],
                                  REP_GRAD_STARTS, REP_SMALL_ROWS, "pack_rep_small")
            arrs = [small.reshape(N_DEV, -1, LANES)]
        (group,), token = _exchange_start([arrs], ["scatter"], "scatter_start%d" % (stage % 6))
        scatters[stage] = (group, token)
        return (token,)

    w = dict(g_mix=norm_mix_g, g_mlp=norm_mlp_g, g_fin=final_g[None], qg=at_q_g, kg=at_k_g,
             wa=bf(rg_w_a[0]), wx=bf(rg_w_x[0]))
    early = {}

    def early_adam(after):
        r_up1, r_dn1 = _exchange_wait(scatters[3][0], tuple(after), "scatter", "scatter_wait3")
        jobs = [(r_up1, mlp_w_up, m_mlp_w_up, v_mlp_w_up, 1), (r_dn1, mlp_w_down, m_mlp_w_down, v_mlp_w_down, 1)]
        return jobs, lambda outs: early.update(up=outs[0], dn=outs[1])

    grad_x = _local_step(x, loss_target, w, fetch, send, early_adam)

    res = {}
    up, dn = early["up"], early["dn"]
    r_out, r_sp, r_wa, r_wx = _exchange_wait(scatters[4][0], (scatters[-1][1],), "scatter", "scatter_wait4")
    wa_part = _sum_parts(r_wa, "reduce_w_a")
    wx_part = _sum_parts(r_wx, "reduce_w_x", (wa_part,))
    (rep_gather,), rep_token = _exchange_start([[wa_part, wx_part]], ["gather"], "rep_gather_start")
    r_qkv, r_o = _exchange_wait(scatters[2][0], (rep_token,), "scatter", "scatter_wait2")
    tr = lambda a: a[0].T
    qkv_t = _adamw_nd(r_qkv, tr(at_w_qkv), tr(m_at_w_qkv), tr(v_at_w_qkv), "adam_at_w_qkv")
    res["at_w_qkv"] = [o.T[None] for o in qkv_t]
    res["at_w_o"] = _adamw_nd(r_o[:, None], at_w_o, m_at_w_o, v_at_w_o, "adam_at_w_o", (qkv_t[1],))
    r_up0, r_dn0 = _exchange_wait(scatters[1][0], (res["at_w_o"][1],), "scatter", "scatter_wait1")
    res["mlp_w_up"] = _adamw_layer(r_up0, mlp_w_up, m_mlp_w_up, v_mlp_w_up, 0, up, "adam_mlp_w_up0")
    res["mlp_w_down"] = _adamw_layer(r_dn0, mlp_w_down, m_mlp_w_down, v_mlp_w_down, 0, dn, "adam_mlp_w_down0",
                                     after=(res["mlp_w_up"][1],))
    r_in, = _exchange_wait(scatters[0][0], (res["mlp_w_down"][1],), "scatter", "scatter_wait0")
    res["rg_w_in"] = _adamw_nd(r_in[:, None], rg_w_in, m_rg_w_in, v_rg_w_in, "adam_rg_w_in")
    res["rg_w_out"] = _adamw_nd(r_out[:, None], rg_w_out, m_rg_w_out, v_rg_w_out, "adam_rg_w_out", (res["rg_w_in"][1],))
    whole, lane = slice(None), slice(0, 1)
    two_rows = lambda r0: [((0, slice(d, d + 1), whole), r0 + d) for d in range(2)]
    (res["rg_conv_w"], res["rg_b_a"], res["rg_b_x"], res["rg_lam"]), _ = _adamw_vectors(
        r_sp, [(rg_conv_w, m_rg_conv_w, v_rg_conv_w, [((0, t, lane, whole), PV_CONV_W + t) for t in range(CONV_W)]),
               (rg_b_a, m_rg_b_a, v_rg_b_a, two_rows(PV_B_A)), (rg_b_x, m_rg_b_x, v_rg_b_x, two_rows(PV_B_X)),
               (rg_lam, m_rg_lam, v_rg_lam, two_rows(PV_LAM))], [0], "adam_small", (res["rg_w_out"][1],))

    r_small, = _exchange_wait(scatters[-1][0], (res["rg_lam"][1],), "scatter", "scatter_wait5")
    small_sum, = _all_gather([_sum_parts(r_small, "reduce_rep_small")], "gather_replicated")
    wa_sum, wx_sum = _exchange_wait(rep_gather, (small_sum,), "gather", "rep_gather_wait")
    rows = lambda a: a.reshape(-1, LANES)
    wa_res = _adamw(wa_sum.reshape(1, -1, LANES), rows(rg_w_a), rows(m_rg_w_a), rows(v_rg_w_a), "adam_rg_w_a")
    wx_res = _adamw(wx_sum.reshape(1, -1, LANES), rows(rg_w_x), rows(m_rg_w_x), rows(v_rg_w_x), "adam_rg_w_x", (wa_res[1],))
    res["rg_w_a"] = [o.reshape(rg_w_a.shape) for o in wa_res]
    res["rg_w_x"] = [o.reshape(rg_w_x.shape) for o in wx_res]

    def vec_slots(a, r0):
        per = a.shape[1] // LANES
        return [((slice(l, l + 1), slice(LANES * j, LANES * (j + 1))), r0 + l * per + j)
                for l in range(a.shape[0]) for j in range(per)]

    fin = [final_g[None], m_final_g[None], v_final_g[None]]
    vecs = [(norm_mix_g, m_norm_mix_g, v_norm_mix_g), (norm_mlp_g, m_norm_mlp_g, v_norm_mlp_g), fin,
            (rg_conv_b, m_rg_conv_b, v_rg_conv_b), (at_q_g, m_at_q_g, v_at_q_g), (at_k_g, m_at_k_g, v_at_k_g)]
    outs, kept = _adamw_vectors(
        small_sum.reshape(1, -1, LANES),
        [(wv, mv, vv, vec_slots(wv, r0)) for (wv, mv, vv), r0 in zip(vecs, REP_SMALL_STARTS)], [REP_LOSS_ROW],
        "adam_rep_small", (wx_res[1],))
    for nm, o in zip(["norm_mix_g", "norm_mlp_g", "final_g", "rg_conv_b", "at_q_g", "at_k_g"], outs):
        res[nm] = [a[0] for a in o] if nm == "final_g" else o
    loss = kept[0, 0]

    order = ["norm_mix_g", "norm_mlp_g", "rg_w_in", "rg_conv_w", "rg_conv_b", "rg_w_a", "rg_b_a", "rg_w_x", "rg_b_x",
             "rg_lam", "rg_w_out", "at_w_qkv", "at_q_g", "at_k_g", "at_w_o", "mlp_w_up", "mlp_w_down", "final_g"]
    return (loss, grad_x, *[res[nm][k] for k in range(4) for nm in order])
```

```python
import math

import jax
import jax.numpy as jnp
from jax import lax
from jax.experimental import pallas as pl
from jax.experimental.pallas import tpu as pltpu

F32 = jnp.float32
BF16 = jnp.bfloat16
S = jax.ShapeDtypeStruct

EPS = 1e-6
HEAD_DIM = 128
N_KV = 2
GRID_W = 64
ROPE_THETA = 10000.0
LRU_BW = 128
RG_C = 8.0
CONV_W = 4
N_DEV = 8
N_SEG = 8
SCAN_UNROLL = 8
TN_STEP_COLS = 512
PROJ_TM = 1024
PROJ_CHAINS = 2
MLP_TM = 512
VMEM_LIMIT_V7X = 56 * 1024 * 1024
SOFTMAX_SCALE = 1.0 / math.sqrt(HEAD_DIM)
GELU_K = math.sqrt(2.0 / math.pi)
GELU_C = 0.044715

ADAM_LR = 0.001
ADAM_B1 = 0.9
ADAM_B2 = 0.999
ADAM_EPS = 1e-08
ADAM_WD = 0.01
ADAM_STEP = 10

NT = (((1,), (1,)), ((), ()))
TN = (((0,), (0,)), ((), ()))


def _cp(*sem):
    return pltpu.CompilerParams(dimension_semantics=sem, vmem_limit_bytes=VMEM_LIMIT_V7X)


def _rms_r(xv):
    return lax.rsqrt(jnp.mean(xv * xv, axis=-1, keepdims=True) + EPS)


def _rms_bwd(dh, xv, g):
    r = _rms_r(xv)
    xh = xv * r
    dg = jnp.sum(dh * xh, axis=0, keepdims=True)
    dxh = dh * g
    dx = r * (dxh - xh * jnp.mean(dxh * xh, axis=-1, keepdims=True))
    return dx, dg


def _dot(a, b):
    return jnp.dot(a, b, preferred_element_type=F32)


def _dot_nt(a, b):
    return lax.dot_general(a, b, NT, preferred_element_type=F32)


def _dot_tn(a, b):
    return lax.dot_general(a, b, TN, preferred_element_type=F32)


def _norm_matmul(x, g, wblk, name, out_dtype=F32):
    T, D = x.shape
    NB, _, nb = wblk.shape
    tm = min(T, PROJ_TM)

    def body(x_ref, g_ref, w_ref, o_ref, h_ref):
        for c in range(PROJ_CHAINS):
            rows = slice(c * tm // PROJ_CHAINS, (c + 1) * tm // PROJ_CHAINS)
            xv = x_ref[rows, :]
            hb = (xv * _rms_r(xv) * g_ref[...]).astype(BF16)
            h_ref[rows, :] = hb
            for q in range(NB):
                o_ref[rows, q * nb:(q + 1) * nb] = _dot(hb, w_ref[q]).astype(o_ref.dtype)

    return pl.pallas_call(
        body, name=name, grid=(T // tm,),
        in_specs=[pl.BlockSpec((tm, D), lambda i: (i, 0)),
                  pl.BlockSpec((1, D), lambda i: (0, 0)),
                  pl.BlockSpec(wblk.shape, lambda i: (0, 0, 0))],
        out_specs=[pl.BlockSpec((tm, NB * nb), lambda i: (i, 0)),
                   pl.BlockSpec((tm, D), lambda i: (i, 0))],
        out_shape=[S((T, NB * nb), out_dtype), S((T, D), BF16)],
        compiler_params=_cp("parallel"),
    )(x, g, wblk)


def _matmul_res(a, w, res, name):
    T, K = a.shape
    N = w.shape[1]
    tm = min(T, PROJ_TM)

    def body(a_ref, w_ref, r_ref, o_ref):
        o_ref[...] = r_ref[...] + _dot(a_ref[...], w_ref[...])

    return pl.pallas_call(
        body, name=name, grid=(T // tm,),
        in_specs=[pl.BlockSpec((tm, K), lambda i: (i, 0)),
                  pl.BlockSpec((K, N), lambda i: (0, 0)),
                  pl.BlockSpec((tm, N), lambda i: (i, 0))],
        out_specs=pl.BlockSpec((tm, N), lambda i: (i, 0)),
        out_shape=S((T, N), F32),
        compiler_params=_cp("parallel"),
    )(a, w, res)


def _matmul_nt(a, w, name, out_dtype, after=()):
    T, N = a.shape
    K = w.shape[0]
    tm = min(T, PROJ_TM)

    def body(a_ref, w_ref, *rest):
        o_ref, ab_ref = rest[len(after):]
        ab = a_ref[...].astype(BF16)
        ab_ref[...] = ab
        o_ref[...] = _dot_nt(ab, w_ref[...]).astype(o_ref.dtype)

    return pl.pallas_call(
        body, name=name, grid=(T // tm,),
        in_specs=[pl.BlockSpec((tm, N), lambda i: (i, 0)),
                  pl.BlockSpec((K, N), lambda i: (0, 0))] + [pl.BlockSpec(memory_space=pl.ANY)] * len(after),
        out_specs=[pl.BlockSpec((tm, K), lambda i: (i, 0)),
                   pl.BlockSpec((tm, N), lambda i: (i, 0))],
        out_shape=[S((T, K), out_dtype), S((T, N), BF16)],
        compiler_params=_cp("parallel"),
    )(a, w, *after)


def _matmul_tn(a, b3, nb, name, blocked, after=()):
    T, M = a.shape
    SB, _, N = b3.shape
    per = N // nb
    NB = SB * per
    tk = min(T, 1024)
    nk = T // tk
    jb = max(1, TN_STEP_COLS // nb) if blocked else 1
    assert per % jb == 0
    if blocked:
        out_spec, out_shape = pl.BlockSpec((jb, M, nb), lambda j: (j, 0, 0)), S((NB, M, nb), BF16)
    else:
        assert SB == 1
        out_spec, out_shape = pl.BlockSpec((M, nb), lambda j: (0, j)), S((M, N), BF16)

    def body(a_ref, b_ref, *rest):
        o_ref = rest[-1]
        for q in range(jb):
            acc = None
            for k in range(nk):
                rows = slice(k * tk, (k + 1) * tk)
                part = _dot_tn(a_ref[rows, :], b_ref[rows, q * nb:(q + 1) * nb])
                acc = part if acc is None else acc + part
            if blocked:
                o_ref[q] = acc.astype(BF16)
            else:
                o_ref[...] = acc.astype(BF16)

    return pl.pallas_call(
        body, name=name, grid=(NB // jb,),
        in_specs=[pl.BlockSpec((T, M), lambda j: (0, 0), pipeline_mode=pl.Buffered(1)),
                  pl.BlockSpec((None, T, jb * nb), lambda j: ((j * jb) // per, 0, ((j * jb) % per) // jb))]
        + [pl.BlockSpec(memory_space=pl.ANY)] * len(after),
        out_specs=out_spec,
        out_shape=out_shape,
        compiler_params=_cp("parallel"),
    )(a, b3, *after)


def _nt_normbwd(dz3, wblk, x, g, dres, name, after=(), transposed=False):
    T, D = x.shape
    NB, nb = wblk.shape[0], wblk.shape[1 if transposed else 2]
    mm = _dot if transposed else _dot_nt
    SB, _, N = dz3.shape
    per = N // nb
    tm = min(T, PROJ_TM)

    def body(dz_ref, w_ref, x_ref, g_ref, dr_ref, *rest):
        dx_ref, dg_ref = rest[len(after):]

        @pl.when(pl.program_id(0) == 0)
        def _():
            dg_ref[...] = jnp.zeros_like(dg_ref)

        for c in range(PROJ_CHAINS):
            rows = slice(c * tm // PROJ_CHAINS, (c + 1) * tm // PROJ_CHAINS)
            dh = None
            for q in range(NB):
                cols = slice((q % per) * nb, (q % per + 1) * nb)
                part = mm(dz_ref[q // per, rows, cols], w_ref[q])
                dh = part if dh is None else dh + part
            dx, dg = _rms_bwd(dh, x_ref[rows, :], g_ref[...])
            dx_ref[rows, :] = dr_ref[rows, :] + dx
            dg_ref[...] += dg

    return pl.pallas_call(
        body, name=name, grid=(T // tm,),
        in_specs=[pl.BlockSpec((SB, tm, N), lambda i: (0, i, 0)),
                  pl.BlockSpec(wblk.shape, lambda i: (0, 0, 0)),
                  pl.BlockSpec((tm, D), lambda i: (i, 0)),
                  pl.BlockSpec((1, D), lambda i: (0, 0)),
                  pl.BlockSpec((tm, D), lambda i: (i, 0))] + [pl.BlockSpec(memory_space=pl.ANY)] * len(after),
        out_specs=[pl.BlockSpec((tm, D), lambda i: (i, 0)),
                   pl.BlockSpec((1, D), lambda i: (0, 0))],
        out_shape=[S((T, D), F32), S((1, D), F32)],
        compiler_params=_cp("arbitrary"),
    )(dz3, wblk, x, g, dres, *after)


def _loss_head(xv, tv, gv, D):
    err = xv * _rms_r(xv) * gv - tv
    e2 = jnp.sum(jnp.sum(err * err, axis=-1, keepdims=True), axis=0, keepdims=True)
    dx, dg = _rms_bwd(err * (1.0 / D), xv, gv)
    return (0.5 / D) * e2, dx, dg


def _mlp_fwd(x, g, wup, wdown, name, head=None):
    T, D = x.shape
    NB, _, fb = wup.shape
    tm = min(T, MLP_TM)
    n_head = 0 if head is None else 2

    def body(x_ref, g_ref, wu_ref, wd_ref, *rest):
        xo_ref, a_ref, h_ref = rest[n_head:n_head + 3]
        xv = x_ref[...]
        hb = (xv * _rms_r(xv) * g_ref[...]).astype(BF16)
        h_ref[...] = hb
        acc = xv
        for j in range(NB):
            a = _dot(hb, wu_ref[j])
            a_ref[:, j * fb:(j + 1) * fb] = a.astype(BF16)
            u = jnp.maximum(a, 0.0)
            acc = acc + _dot((u * u).astype(BF16), wd_ref[j * fb:(j + 1) * fb, :])

        if head is None:
            xo_ref[...] = acc
        else:
            t_ref, gf_ref = rest[:2]
            loss_ref, dgf_ref = rest[n_head + 3:n_head + 5]

            @pl.when(pl.program_id(0) == 0)
            def _():
                loss_ref[...] = jnp.zeros_like(loss_ref)
                dgf_ref[...] = jnp.zeros_like(dgf_ref)

            e2, dx, dg = _loss_head(acc, t_ref[...], gf_ref[...], D)
            xo_ref[...] = dx
            loss_ref[...] += e2
            dgf_ref[...] += dg

    row = pl.BlockSpec((tm, D), lambda i: (i, 0))
    vec = pl.BlockSpec((1, D), lambda i: (0, 0))
    once = pl.Buffered(1)
    in_specs = [row, vec, pl.BlockSpec((NB, D, fb), lambda i: (0, 0, 0), pipeline_mode=once),
                pl.BlockSpec((NB * fb, D), lambda i: (0, 0), pipeline_mode=once)]
    out_specs = [row, pl.BlockSpec((tm, NB * fb), lambda i: (i, 0)), row]
    out_shape = [S((T, D), F32), S((T, NB * fb), BF16), S((T, D), BF16)]
    if head is not None:
        in_specs += [row, vec]
        out_specs += [pl.BlockSpec((1, 128), lambda i: (0, 0)), vec]
        out_shape += [S((1, 128), F32), S((1, D), F32)]
    return pl.pallas_call(
        body, name=name, grid=(T // tm,),
        in_specs=in_specs, out_specs=out_specs, out_shape=out_shape,
        compiler_params=_cp("parallel" if head is None else "arbitrary"),
    )(x, g, wup, wdown, *(head or ()))


def _mlp_bwd_dx(x, dout, a, g, wup, wdown, name):
    T, D = x.shape
    NB, _, fb = wup.shape
    tm = min(T, MLP_TM)

    def body(x_ref, do_ref, a_ref, g_ref, wu_ref, wd_ref, dx_ref, da_ref, dob_ref, dg_ref):
        @pl.when(pl.program_id(0) == 0)
        def _():
            dg_ref[...] = jnp.zeros_like(dg_ref)

        dov = do_ref[...]
        dob = dov.astype(BF16)
        dob_ref[...] = dob
        dh = None
        for j in range(NB):
            cols = slice(j * fb, (j + 1) * fb)
            du2 = _dot_nt(dob, wd_ref[cols, :])
            u = jnp.maximum(a_ref[:, cols].astype(F32), 0.0)
            da = (du2 * (2.0 * u)).astype(BF16)
            da_ref[:, cols] = da
            part = _dot_nt(da, wu_ref[j])
            dh = part if dh is None else dh + part
        dx, dg = _rms_bwd(dh, x_ref[...], g_ref[...])
        dx_ref[...] = dov + dx
        dg_ref[...] += dg

    row = pl.BlockSpec((tm, D), lambda i: (i, 0))
    wide = pl.BlockSpec((tm, NB * fb), lambda i: (i, 0))
    vec = pl.BlockSpec((1, D), lambda i: (0, 0))
    once = pl.Buffered(1)
    return pl.pallas_call(
        body, name=name, grid=(T // tm,),
        in_specs=[row, row, wide, vec, pl.BlockSpec((NB, D, fb), lambda i: (0, 0, 0), pipeline_mode=once),
                  pl.BlockSpec((NB * fb, D), lambda i: (0, 0), pipeline_mode=once)],
        out_specs=[row, wide, row, vec],
        out_shape=[S((T, D), F32), S((T, NB * fb), BF16), S((T, D), BF16), S((1, D), F32)],
        compiler_params=_cp("arbitrary"),
    )(x, dout, a, g, wup, wdown)


def _adam_update(g, w, m, v):
    mn = ADAM_B1 * m + (1.0 - ADAM_B1) * g
    vn = ADAM_B2 * v + (1.0 - ADAM_B2) * (g * g)
    c1 = 1.0 - ADAM_B1 ** ADAM_STEP
    c2 = 1.0 - ADAM_B2 ** ADAM_STEP
    return (-ADAM_LR) * ((mn / c1) / (jnp.sqrt(vn / c2) + ADAM_EPS) + ADAM_WD * w), mn, vn


def _mlp_bwd_dw(h, da, a, dob, fb, name, adam_jobs=()):
    T, D = h.shape
    F = a.shape[1]
    NB = F // fb
    tk = min(T, 1024)
    nk = T // tk
    nj = len(adam_jobs)

    def body(h_ref, da_ref, a_ref, dob_ref, *rest):
        dwu_ref, dwd_ref = rest[4 * nj:4 * nj + 2]
        au = ad = None
        for k in range(nk):
            rows = slice(k * tk, (k + 1) * tk)
            pu = _dot_tn(h_ref[rows, :], da_ref[rows, :])
            u = jnp.maximum(a_ref[rows, :].astype(F32), 0.0)
            pd = _dot_tn((u * u).astype(BF16), dob_ref[rows, :])
            au, ad = (pu, pd) if au is None else (au + pu, ad + pd)
        dwu_ref[...] = au.astype(BF16)
        dwd_ref[...] = ad.astype(BF16)
        for q in range(nj):
            p_ref, w_ref, m_ref, v_ref = rest[4 * q:4 * q + 4]
            g_out, d_out, m_out, v_out = rest[4 * nj + 2 + 4 * q:4 * nj + 6 + 4 * q]
            g = p_ref[0].astype(F32)
            for i in range(1, p_ref.shape[0]):
                g = g + p_ref[i].astype(F32)
            d, mn, vn = _adam_update(g, w_ref[...], m_ref[...], v_ref[...])
            g_out[...], d_out[...], m_out[...], v_out[...] = g, d, mn, vn

    once = pl.Buffered(1)
    in_specs = [pl.BlockSpec((T, D), lambda j: (0, 0), pipeline_mode=once),
                pl.BlockSpec((T, fb), lambda j: (0, j)),
                pl.BlockSpec((T, fb), lambda j: (0, j)),
                pl.BlockSpec((T, D), lambda j: (0, 0), pipeline_mode=once)]
    out_specs = [pl.BlockSpec((None, D, fb), lambda j: (j, 0, 0)), pl.BlockSpec((fb, D), lambda j: (j, 0))]
    out_shape = [S((NB, D, fb), BF16), S((F, D), BF16)]
    operands = [h, da, a, dob]
    for parts, w3, m3, v3, layer in adam_jobs:
        P, R, C = parts.shape
        blk = pl.BlockSpec((None, R // NB, C), lambda j, layer=layer: (layer, j, 0))
        in_specs += [pl.BlockSpec((P, R // NB, C), lambda j: (0, j, 0)), blk, blk, blk]
        out_specs += [blk] * 4
        out_shape += [S(w3.shape, F32)] * 4
        operands += [parts, w3, m3, v3]
    outs = pl.pallas_call(
        body, name=name, grid=(NB,),
        in_specs=in_specs, out_specs=out_specs, out_shape=out_shape,
        compiler_params=_cp("parallel"),
    )(*operands)
    return outs[0], outs[1], [outs[2 + 4 * q:6 + 4 * q] for q in range(nj)]


def _rope_tables(L):
    nf = HEAD_DIM // 4
    t = jnp.arange(L, dtype=jnp.int32)
    row = (t // GRID_W).astype(F32)
    col = (t % GRID_W).astype(F32)
    inv = ROPE_THETA ** (-jnp.arange(nf, dtype=F32) / nf)
    ar = row[:, None] * inv
    ac = col[:, None] * inv
    cos = jnp.concatenate([jnp.cos(ar), jnp.cos(ar), jnp.cos(ac), jnp.cos(ac)], axis=-1)
    sin = jnp.concatenate([-jnp.sin(ar), jnp.sin(ar), -jnp.sin(ac), jnp.sin(ac)], axis=-1)
    return cos, sin


def _swap32(x):
    lane = lax.broadcasted_iota(jnp.int32, x.shape, 1)
    up = pltpu.roll(x, HEAD_DIM - 32, 1)
    down = pltpu.roll(x, 32, 1)
    return jnp.where((lane % 64) < 32, up, down)


def _qkv_proj(x, g, wt_blk, qg, kg, cos, sin, L, name):
    T, D = x.shape
    NB, nb, _ = wt_blk.shape
    W = NB * nb
    nh = W // HEAD_DIM - 2 * N_KV
    tm = min(L, 512)
    lb = L // tm

    def body(x_ref, g_ref, w_ref, qg_ref, kg_ref, cos_ref, sin_ref, qkv_ref, h_ref, q_ref, k_ref, v_ref):
        for ch in range(PROJ_CHAINS):
            rows = slice(ch * tm // PROJ_CHAINS, (ch + 1) * tm // PROJ_CHAINS)
            xv = x_ref[rows, :]
            hb = (xv * _rms_r(xv) * g_ref[...]).astype(BF16)
            h_ref[rows, :] = hb
            for j in range(NB):
                qkv_ref[rows, j * nb:(j + 1) * nb] = _dot_nt(hb, w_ref[j])
            c = cos_ref[rows, :]
            s = sin_ref[rows, :]
            for h in range(nh + N_KV):
                xh = qkv_ref[rows, h * HEAD_DIM:(h + 1) * HEAD_DIM]
                gv = qg_ref[...] if h < nh else kg_ref[...]
                y = xh * _rms_r(xh) * gv
                y = (y * c + _swap32(y) * s).astype(BF16)
                if h < nh:
                    q_ref[rows, h * HEAD_DIM:(h + 1) * HEAD_DIM] = y
                else:
                    k_ref[rows, (h - nh) * HEAD_DIM:(h - nh + 1) * HEAD_DIM] = y
            v_ref[rows, :] = qkv_ref[rows, (nh + N_KV) * HEAD_DIM:].astype(BF16)

    row = lambda cols: pl.BlockSpec((tm, cols), lambda i: (i, 0))
    vec = lambda cols: pl.BlockSpec((1, cols), lambda i: (0, 0))
    table = pl.BlockSpec((tm, HEAD_DIM), lambda i: (i % lb, 0))
    return pl.pallas_call(
        body, name=name, grid=(T // tm,),
        in_specs=[row(D), vec(D), pl.BlockSpec(wt_blk.shape, lambda i: (0, 0, 0)), vec(HEAD_DIM), vec(HEAD_DIM), table, table],
        out_specs=[row(W), row(D), row(nh * HEAD_DIM), row(N_KV * HEAD_DIM), row(N_KV * HEAD_DIM)],
        out_shape=[S((T, W), F32), S((T, D), BF16), S((T, nh * HEAD_DIM), BF16), S((T, N_KV * HEAD_DIM), BF16),
                   S((T, N_KV * HEAD_DIM), BF16)],
        compiler_params=_cp("parallel"),
    )(x, g, wt_blk, qg, kg, cos, sin)


def _qk_prep_bwd(qkv, dq, dk, dv, qg, kg, cos, sin, L, name):
    T, W = qkv.shape
    nh = W // HEAD_DIM - 2 * N_KV
    tm = min(L, 512)
    lb = L // tm

    def body(qkv_ref, dq_ref, dk_ref, dv_ref, qg_ref, kg_ref, cos_ref, sin_ref, dz_ref, dqg_ref, dkg_ref):
        @pl.when(pl.program_id(0) == 0)
        def _():
            dqg_ref[...] = jnp.zeros_like(dqg_ref)
            dkg_ref[...] = jnp.zeros_like(dkg_ref)

        c = cos_ref[...]
        s = sin_ref[...]
        for h in range(nh + N_KV):
            cols = slice(h * HEAD_DIM, (h + 1) * HEAD_DIM)
            if h < nh:
                dout, gv, dg_ref = dq_ref[:, cols], qg_ref[...], dqg_ref
            else:
                kc = slice((h - nh) * HEAD_DIM, (h - nh + 1) * HEAD_DIM)
                dout, gv, dg_ref = dk_ref[:, kc], kg_ref[...], dkg_ref
            dy = dout * c - _swap32(dout) * s
            dx, dg = _rms_bwd(dy, qkv_ref[:, cols], gv)
            dg_ref[...] += dg
            dz_ref[:, cols] = dx.astype(BF16)
        dz_ref[:, (nh + N_KV) * HEAD_DIM:] = dv_ref[...].astype(BF16)

    return pl.pallas_call(
        body, name=name, grid=(T // tm,),
        in_specs=[pl.BlockSpec((tm, W), lambda i: (i, 0)),
                  pl.BlockSpec((tm, nh * HEAD_DIM), lambda i: (i, 0)),
                  pl.BlockSpec((tm, N_KV * HEAD_DIM), lambda i: (i, 0)),
                  pl.BlockSpec((tm, N_KV * HEAD_DIM), lambda i: (i, 0)),
                  pl.BlockSpec((1, HEAD_DIM), lambda i: (0, 0)),
                  pl.BlockSpec((1, HEAD_DIM), lambda i: (0, 0)),
                  pl.BlockSpec((tm, HEAD_DIM), lambda i: (i % lb, 0)),
                  pl.BlockSpec((tm, HEAD_DIM), lambda i: (i % lb, 0))],
        out_specs=[pl.BlockSpec((tm, W), lambda i: (i, 0)),
                   pl.BlockSpec((1, HEAD_DIM), lambda i: (0, 0)),
                   pl.BlockSpec((1, HEAD_DIM), lambda i: (0, 0))],
        out_shape=[S((T, W), BF16), S((1, HEAD_DIM), F32), S((1, HEAD_DIM), F32)],
        compiler_params=_cp("arbitrary"),
    )(qkv, dq, dk, dv, qg, kg, cos, sin)


EXP2_SCALE = SOFTMAX_SCALE * math.log2(math.e)
ATTN_SUB = 256
ATTN_TQ = 2048


def _softmax_rows(q, k):
    s = _dot_nt(q, k)
    e = jnp.exp2((s - jnp.max(s, axis=-1, keepdims=True)) * EXP2_SCALE)
    return e, jnp.sum(e, axis=-1, keepdims=True)


def _attn_fwd(q, k, v, L, name):
    T = q.shape[0]
    nh = q.shape[1] // HEAD_DIM
    G = nh // N_KV
    B = T // L
    tq = min(L, ATTN_TQ)
    nq = L // tq
    sub = min(tq, ATTN_SUB)

    def body(q_ref, k_ref, v_ref, o_ref):
        for h in range(tq // sub):
            rows = slice(h * sub, (h + 1) * sub)
            e, l = _softmax_rows(q_ref[rows, :], k_ref[...])
            o_ref[rows, :] = (_dot(e.astype(BF16), v_ref[...]) / l).astype(BF16)

    qspec = pl.BlockSpec((tq, HEAD_DIM), lambda b, kv, g, qi: (b * nq + qi, kv * G + g))
    kspec = pl.BlockSpec((L, HEAD_DIM), lambda b, kv, g, qi: (b, kv))
    return pl.pallas_call(
        body, name=name, grid=(B, N_KV, G, nq),
        in_specs=[qspec, kspec, kspec],
        out_specs=qspec,
        out_shape=S((T, nh * HEAD_DIM), BF16),
        compiler_params=_cp("parallel", "parallel", "parallel", "parallel"),
    )(q, k, v)


def _attn_bwd(q, k, v, do, o, L, name):
    T = q.shape[0]
    nh = q.shape[1] // HEAD_DIM
    G = nh // N_KV
    B = T // L
    tq = min(L, ATTN_TQ)
    nq = L // tq
    sub = min(tq, ATTN_SUB)

    def body(q_ref, k_ref, v_ref, do_ref, o_ref, dq_ref, dk_ref, dv_ref, ds_scr, p_scr):
        first = (pl.program_id(2) == 0) & (pl.program_id(3) == 0)
        last = (pl.program_id(2) == G - 1) & (pl.program_id(3) == nq - 1)

        @pl.when(first)
        def _():
            dk_ref[...] = jnp.zeros_like(dk_ref)
            dv_ref[...] = jnp.zeros_like(dv_ref)

        for h in range(tq // sub):
            rows = slice(h * sub, (h + 1) * sub)
            dov = do_ref[rows, :]
            e, l = _softmax_rows(q_ref[rows, :], k_ref[...])
            p = e * (1.0 / l)
            dsum = jnp.sum(dov.astype(F32) * o_ref[rows, :].astype(F32), axis=-1, keepdims=True)
            ds_scr[rows, :] = (p * (_dot_nt(dov, v_ref[...]) - dsum)).astype(BF16)
            p_scr[rows, :] = p.astype(BF16)
        ds = ds_scr[...]
        dq_ref[...] = _dot(ds, k_ref[...]) * SOFTMAX_SCALE
        dk_ref[...] += _dot_tn(ds, q_ref[...])
        dv_ref[...] += _dot_tn(p_scr[...], do_ref[...])

        @pl.when(last)
        def _():
            dk_ref[...] = dk_ref[...] * SOFTMAX_SCALE

    qspec = pl.BlockSpec((tq, HEAD_DIM), lambda b, kv, g, qi: (b * nq + qi, kv * G + g))
    kspec = pl.BlockSpec((L, HEAD_DIM), lambda b, kv, g, qi: (b, kv))
    return pl.pallas_call(
        body, name=name, grid=(B, N_KV, G, nq),
        in_specs=[qspec, kspec, kspec, qspec, qspec],
        out_specs=[qspec, kspec, kspec],
        out_shape=[S((T, nh * HEAD_DIM), F32), S((T, N_KV * HEAD_DIM), F32), S((T, N_KV * HEAD_DIM), F32)],
        scratch_shapes=[pltpu.VMEM((tq, L), BF16), pltpu.VMEM((tq, L), BF16)],
        compiler_params=_cp("parallel", "parallel", "arbitrary", "arbitrary"),
    )(q, k, v, do, o)


PV_CONV_W = 0
PV_B_A = 8
PV_B_X = 16
PV_LAM = 24
PV_CONV_B = 32
PV_ROWS = 40


def _shift_rows(x, k):
    if k == 0:
        return x
    L = x.shape[0]
    n = N_SEG * abs(k)
    seg = lax.broadcasted_iota(jnp.int32, (n, x.shape[1]), 0) % N_SEG
    if k > 0:
        edge = jnp.where(seg == 0, 0.0, pltpu.roll(x[L - n:], 1, 0))
        return jnp.concatenate([edge, x[:L - n]], axis=0)
    edge = jnp.where(seg == N_SEG - 1, 0.0, pltpu.roll(x[:n], n - 1, 0))
    return jnp.concatenate([x[n:], edge], axis=0)


def _conv_taps(rec, pv):
    c = pv[PV_CONV_B:PV_CONV_B + 1]
    for j in range(CONV_W):
        c = c + pv[PV_CONV_W + j:PV_CONV_W + j + 1] * _shift_rows(rec, 2 - j)
    return c


def _sigmoid(x):
    return 0.5 * jnp.tanh(0.5 * x) + 0.5


EXPM1_SERIES_BELOW = 0.03


def _rg_gates(c, cbf, wa, wx, ba, bx, lam):
    r = _sigmoid(_dot(cbf, wa) + ba)
    i = _sigmoid(_dot(cbf, wx) + bx)
    sp = jnp.maximum(-lam, 0.0) + jnp.log1p(jnp.exp(-jnp.abs(lam)))
    la = r * ((-RG_C) * sp)
    a = jnp.exp(la)
    a2 = a * a
    x = la + la
    series = -(x * ((x * (1.0 / 6.0) + 0.5) * x + 1.0))
    om = jnp.where(x > -EXPM1_SERIES_BELOW, series, 1.0 - a2)
    rm = lax.rsqrt(om)
    return r, i, a, om * rm, rm, a2, sp


def _gelu(x):
    t = jnp.tanh(GELU_K * (x + GELU_C * x * x * x))
    return 0.5 * x * (1.0 + t), t


def _scan_pair(af_ref, uf_ref, ab_ref, ub_ref, hf_ref, hb_ref, pf_ref, pb_ref, L):
    ls = L // N_SEG
    zero = jnp.zeros((N_SEG, LRU_BW), F32)
    one = jnp.ones((N_SEG, LRU_BW), F32)
    tile = lambda t: pl.ds(pl.multiple_of(t * N_SEG, N_SEG), N_SEG)

    def steps(tc, carry):
        hf, pf, hb, pb = carry
        for q in range(SCAN_UNROLL):
            t = tc * SCAN_UNROLL + q
            rf, rb = tile(t), tile(ls - 1 - t)
            af = af_ref[rf, :]
            hf = af * hf + uf_ref[rf, :]
            pf = pf * af
            hf_ref[rf, :] = hf
            pf_ref[rf, :] = pf
            ab = ab_ref[rb, :]
            hb = ab * hb + ub_ref[rb, :]
            pb = pb * ab
            hb_ref[rb, :] = hb
            pb_ref[rb, :] = pb
        return hf, pf, hb, pb

    hf_e, pf_e, hb_e, pb_e = lax.fori_loop(0, ls // SCAN_UNROLL, steps, (zero, one, zero, one))

    rows, cin = [], jnp.zeros((1, LRU_BW), F32)
    for s in range(N_SEG):
        rows.append(cin)
        cin = hf_e[s:s + 1] + pf_e[s:s + 1] * cin
    cf = jnp.concatenate(rows, axis=0)
    rows, cin = [], jnp.zeros((1, LRU_BW), F32)
    for s in reversed(range(N_SEG)):
        rows.append(cin)
        cin = hb_e[s:s + 1] + pb_e[s:s + 1] * cin
    cb = jnp.concatenate(rows[::-1], axis=0)

    def fix(tc, _):
        for q in range(SCAN_UNROLL):
            r = tile(tc * SCAN_UNROLL + q)
            hf_ref[r, :] = hf_ref[r, :] + pf_ref[r, :] * cf
            hb_ref[r, :] = hb_ref[r, :] + pb_ref[r, :] * cb
        return 0

    lax.fori_loop(0, ls // SCAN_UNROLL, fix, 0)


def _rg_specs(L, D, nblk):
    slab = lambda off: pl.BlockSpec((L, LRU_BW), lambda cb, b: (b, off + cb))
    wspec = pl.BlockSpec((2, None, LRU_BW, LRU_BW), lambda cb, b: (0, cb, 0, 0))
    pvspec = pl.BlockSpec((PV_ROWS, LRU_BW), lambda cb, b: (0, cb))
    return slab, wspec, pvspec


def _rg_fwd(z, pvec, wa, wx, L, name):
    T, C2 = z.shape
    C = C2 // 2
    nblk = C // LRU_BW
    B = T // L
    slab, wspec, pvspec = _rg_specs(L, C, nblk)

    def body(gp_ref, rec_ref, pv_ref, wa_ref, wx_ref, yg_ref, hf_ref, hb_ref, a_scr, u_scr, p_scr):
        pv = pv_ref[...]
        c = _conv_taps(rec_ref[...], pv)
        cbf = c.astype(BF16)
        for d in range(2):
            _, i, a, m, _, _, _ = _rg_gates(c, cbf, wa_ref[d], wx_ref[d], pv[PV_B_A + d:PV_B_A + d + 1],
                                      pv[PV_B_X + d:PV_B_X + d + 1], pv[PV_LAM + d:PV_LAM + d + 1])
            a_scr[d] = a
            u_scr[d] = m * (i * c)
        _scan_pair(a_scr.at[0], u_scr.at[0], a_scr.at[1], u_scr.at[1], hf_ref, hb_ref, p_scr.at[0], p_scr.at[1], L)
        gate, _ = _gelu(gp_ref[...])
        yg_ref[...] = ((hf_ref[...] + hb_ref[...]) * gate).astype(BF16)

    return pl.pallas_call(
        body, name=name, grid=(nblk, B),
        in_specs=[slab(0), slab(nblk), pvspec, wspec, wspec],
        out_specs=[slab(0), slab(0), slab(0)],
        out_shape=[S((T, C), BF16), S((T, C), F32), S((T, C), F32)],
        scratch_shapes=[pltpu.VMEM((2, L, LRU_BW), F32)] * 3,
        compiler_params=_cp("parallel", "parallel"),
    )(z, z, pvec, wa, wx)


def _rg_bwd(z, hf, hb, dyg, pvec, wa, wx, L, name):
    T, C2 = z.shape
    C = C2 // 2
    nblk = C // LRU_BW
    B = T // L
    slab, wspec, pvspec = _rg_specs(L, C, nblk)

    def body(gp_ref, rec_ref, hf_ref, hb_ref, dyg_ref, pv_ref, wa_ref, wx_ref,
             dz_ref, dwa_ref, dwx_ref, dpv_ref, a_scr, u_scr, d_scr, p_scr):
        @pl.when(pl.program_id(1) == 0)
        def _():
            dwa_ref[...] = jnp.zeros_like(dwa_ref)
            dwx_ref[...] = jnp.zeros_like(dwx_ref)
            dpv_ref[...] = jnp.zeros_like(dpv_ref)

        pv = pv_ref[...]
        rec = rec_ref[...]
        c = _conv_taps(rec, pv)
        cbf = c.astype(BF16)
        gp = gp_ref[...]
        gate, th = _gelu(gp)
        dgelu = 0.5 * (1.0 + th) + 0.5 * gp * (1.0 - th * th) * GELU_K * (1.0 + 3.0 * GELU_C * gp * gp)
        dyg = dyg_ref[...]
        dz_ref[0] = (dyg * (hf_ref[...] + hb_ref[...]) * dgelu).astype(BF16)
        dy = dyg * gate

        gates = []
        for d in range(2):
            gates.append(_rg_gates(c, cbf, wa_ref[d], wx_ref[d], pv[PV_B_A + d:PV_B_A + d + 1],
                                   pv[PV_B_X + d:PV_B_X + d + 1], pv[PV_LAM + d:PV_LAM + d + 1]))
        a_scr[0] = _shift_rows(gates[1][2], 1)
        a_scr[1] = _shift_rows(gates[0][2], -1)
        u_scr[...] = dy
        _scan_pair(a_scr.at[0], u_scr, a_scr.at[1], u_scr, d_scr.at[1], d_scr.at[0], p_scr.at[0], p_scr.at[1], L)

        dc = jnp.zeros_like(c)
        rows = []
        for d in range(2):
            r, i, a, m, rm, a2, sp = gates[d]
            delta = d_scr[d]
            hnb = _shift_rows(hf_ref[...], 1) if d == 0 else _shift_rows(hb_ref[...], -1)
            da = delta * hnb
            dm = delta * (i * c)
            di = delta * (m * c)
            dc = dc + delta * (m * i)
            dla = da * a - dm * (a2 * rm)
            dpa = (dla * ((-RG_C) * sp)) * (r * (1.0 - r))
            dpx = di * (i * (1.0 - i))
            dsp = (-RG_C) * jnp.sum(dla * r, axis=0, keepdims=True)
            lam = pv[PV_LAM + d:PV_LAM + d + 1]
            rows.append((jnp.sum(dpa, axis=0, keepdims=True), jnp.sum(dpx, axis=0, keepdims=True),
                         -dsp * _sigmoid(-lam)))
            dpab = dpa.astype(BF16)
            dpxb = dpx.astype(BF16)
            dwa_ref[d] += _dot_tn(cbf, dpab)
            dwx_ref[d] += _dot_tn(cbf, dpxb)
            dc = dc + _dot_nt(dpab, wa_ref[d]) + _dot_nt(dpxb, wx_ref[d])

        drec = jnp.zeros_like(c)
        dcw = []
        for j in range(CONV_W):
            drec = drec + pv[PV_CONV_W + j:PV_CONV_W + j + 1] * _shift_rows(dc, j - 2)
            dcw.append(jnp.sum(dc * _shift_rows(rec, 2 - j), axis=0, keepdims=True))
        dz_ref[1] = drec.astype(BF16)
        for j in range(CONV_W):
            dpv_ref[PV_CONV_W + j:PV_CONV_W + j + 1, :] += dcw[j]
        for d in range(2):
            dpv_ref[PV_B_A + d:PV_B_A + d + 1, :] += rows[d][0]
            dpv_ref[PV_B_X + d:PV_B_X + d + 1, :] += rows[d][1]
            dpv_ref[PV_LAM + d:PV_LAM + d + 1, :] += rows[d][2]
        dpv_ref[PV_CONV_B:PV_CONV_B + 1, :] += jnp.sum(dc, axis=0, keepdims=True)

    return pl.pallas_call(
        body, name=name, grid=(nblk, B),
        in_specs=[slab(0), slab(nblk), slab(0), slab(0), slab(0), pvspec, wspec, wspec],
        out_specs=[pl.BlockSpec((2, L, LRU_BW), lambda cb, b: (0, b, cb)), wspec, wspec, pvspec],
        out_shape=[S((2, T, C), BF16), S((2, nblk, LRU_BW, LRU_BW), F32), S((2, nblk, LRU_BW, LRU_BW), F32),
                   S((PV_ROWS, C), F32)],
        scratch_shapes=[pltpu.VMEM((2, L, LRU_BW), F32), pltpu.VMEM((L, LRU_BW), F32),
                        pltpu.VMEM((2, L, LRU_BW), F32), pltpu.VMEM((2, L, LRU_BW), F32)],
        compiler_params=_cp("parallel", "arbitrary"),
    )(z, z, hf, hb, dyg, pvec, wa, wx)


QKV_NB = 512


def _interleave(a):
    *lead, L, D = a.shape
    return a.reshape(*lead, N_SEG, L // N_SEG, D).swapaxes(-3, -2).reshape(*lead, L, D)


def _deinterleave(a):
    *lead, L, D = a.shape
    return a.reshape(*lead, L // N_SEG, N_SEG, D).swapaxes(-3, -2).reshape(*lead, L, D)


def _local_step(x3, tgt3, w, fetch, send, early_adam):
    Bl, L, D = x3.shape
    T = Bl * L
    x = _interleave(x3).reshape(T, D)
    tgt = _interleave(tgt3).reshape(T, D)
    gm = [w["g_mix"][i:i + 1] for i in range(2)]
    gl = [w["g_mlp"][i:i + 1] for i in range(2)]

    w0 = fetch(0, ())
    nb_in = w0["w_in"].shape[-1]
    z, h0 = _norm_matmul(x, gm[0], w0["w_in"], "rg_in")
    yg, hf, hb = _rg_fwd(z, w0["pvec"], w["wa"], w["wx"], L, "rg_fwd")
    w1 = fetch(1, (yg,))
    x1 = _matmul_res(yg, w1["w_out"], x, "rg_out")
    w1.update(fetch(4, (x1,)))
    fb = w1["w_up0"].shape[-1]
    x2, a0, hm0 = _mlp_fwd(x1, gl[0], w1["w_up0"], w1["w_down0"], "mlp0_fwd")
    w2 = fetch(2, (x2,))
    cos, sin = [_interleave(t) for t in _rope_tables(L)]
    qkv, h1, qn, kn, vb = _qkv_proj(x2, gm[1], w2["w_qkv"], w["qg"], w["kg"], cos, sin, L, "at_qkv")
    o = _attn_fwd(qn, kn, vb, L, "at_fwd")
    x3_ = _matmul_res(o, w2["w_o"], x2, "at_out")
    w3 = fetch(3, (x3_,))
    dx4, a1, hm1, loss, dgf = _mlp_fwd(x3_, gl[1], w3["w_up1"], w3["w_down1"], "mlp1_fwd", head=(tgt, w["g_fin"]))

    dx3, da1, dob1, dgl1 = _mlp_bwd_dx(x3_, dx4, a1, gl[1], w3["w_up1"], w3["w_down1"], "mlp1_bwd_dx")
    dwu1, dwd1, _ = _mlp_bwd_dw(hm1, da1, a1, dob1, fb, "mlp1_bwd_dw")
    sent = send(3, dict(w_up1=dwu1, w_down1=dwd1))
    do, dx3b = _matmul_nt(dx3, w2["w_o"], "at_out_bwd", BF16, after=sent)
    dwo = _matmul_tn(o, dx3b[None], QKV_NB, "at_out_dw", blocked=False)
    dq, dk, dv = _attn_bwd(qn, kn, vb, do, o, L, "at_bwd")
    dqkv, dqg, dkg = _qk_prep_bwd(qkv, dq, dk, dv, w["qg"], w["kg"], cos, sin, L, "at_prep_bwd")
    dwqkv = _matmul_tn(dqkv, h1[None], QKV_NB, "at_qkv_dw", blocked=False)
    sent = send(2, dict(w_qkv=dwqkv, w_o=dwo))
    dx2, dgm1 = _nt_normbwd(dqkv[None], w2["w_qkv"], x2, gm[1], dx3, "at_qkv_bwd", after=sent, transposed=True)
    dx1, da0, dob0, dgl0 = _mlp_bwd_dx(x1, dx2, a0, gl[0], w1["w_up0"], w1["w_down0"], "mlp0_bwd_dx")
    jobs, sink = early_adam((dob0,))
    dwu0, dwd0, early = _mlp_bwd_dw(hm0, da0, a0, dob0, fb, "mlp0_bwd_dw", jobs)
    sink(early)
    sent = send(1, dict(w_up0=dwu0, w_down0=dwd0))
    dyg, dx1b = _matmul_nt(dx1, w1["w_out"], "rg_out_bwd", F32, after=sent)
    dwout = _matmul_tn(yg, dx1b[None], QKV_NB, "rg_out_dw", blocked=False)
    dz, dwa, dwx, dpv = _rg_bwd(z, hf, hb, dyg, w0["pvec"], w["wa"], w["wx"], L, "rg_bwd")
    sent = send(4, dict(w_out=dwout, pvec=dpv, wa=dwa, wx=dwx))
    dwin = _matmul_tn(h0, dz, nb_in, "rg_in_dw", blocked=True, after=sent)
    sent = send(0, dict(w_in=dwin))
    dx0, dgm0 = _nt_normbwd(dz, w0["w_in"], x, gm[0], dx1, "rg_in_bwd", after=sent)
    send(-1, dict(g_mix=[dgm0, dgm1], g_mlp=[dgl0, dgl1], g_fin=dgf, conv_b=dpv[PV_CONV_B:PV_CONV_B + 1], qg=dqg, kg=dkg, loss=loss))
    return _deinterleave(dx0.reshape(Bl, L, D))


MESH = pl.DeviceIdType.MESH
ANY = pl.BlockSpec(memory_space=pl.ANY)
N_PEERS = N_DEV - 1


def _my_place():
    return lax.axis_index("x"), lax.axis_index("y"), lax.axis_index("c")


def _flat(px, py, pc):
    return 4 * px + 2 * py + pc


def _all_gather(shards, name):
    n = len(shards)

    def body(*refs):
        ins, outs = refs[:n], refs[n:2 * n]
        send_sems, recv_sems, local_sems = refs[2 * n:]
        x, y, c = _my_place()
        me, sibling = (x, y, c), (x, y, 1 - c)
        chips = [(1 - x, y), (x, 1 - y), (1 - x, 1 - y)]

        def copy(a, k, block, to, src=None):
            dst = outs[a].at[_flat(*block)]
            return pltpu.make_async_remote_copy(
                src_ref=dst if src is None else src, dst_ref=dst,
                send_sem=send_sems.at[a, k], recv_sem=recv_sems.at[a, k],
                device_id=to, device_id_type=MESH)

        mine = [pltpu.make_async_copy(ins[a], outs[a].at[_flat(*me)], local_sems.at[a]) for a in range(n)]
        for cp in mine:
            cp.start()
        first = []
        for a in range(n):
            first.append(copy(a, 0, me, sibling, src=ins[a]))
            first += [copy(a, 1 + j, me, (*chip, c), src=ins[a]) for j, chip in enumerate(chips)]
        for cp in first:
            cp.start()
        passed = []
        for j, chip in enumerate(chips):
            for a in range(n):
                copy(a, 1 + j, (*chip, c), me).wait_recv()
                fwd = copy(a, 4 + j, (*chip, c), sibling)
                fwd.start()
                passed.append(fwd)
        for a in range(n):
            copy(a, 0, sibling, me).wait_recv()
            for j, chip in enumerate(chips):
                copy(a, 4 + j, (*chip, 1 - c), me).wait_recv()
        for cp in first + passed:
            cp.wait_send()
        for cp in mine:
            cp.wait()

    return pl.pallas_call(
        body, name=name,
        in_specs=[ANY] * n, out_specs=[ANY] * n,
        out_shape=[S((N_DEV,) + s.shape, s.dtype) for s in shards],
        scratch_shapes=[pltpu.SemaphoreType.DMA((n, N_PEERS)), pltpu.SemaphoreType.DMA((n, N_PEERS)),
                        pltpu.SemaphoreType.DMA((n,))],
    )(*shards)


HBM = pl.BlockSpec(memory_space=pltpu.HBM)
SEM = pl.BlockSpec(memory_space=pltpu.SEMAPHORE)
SIDE_EFFECT = pltpu.SideEffectType.DATAFLOW_SIDE_EFFECTING
SEMS_PER_GROUP = 3


NEAR_PEERS = (1, 2, 4, 6)
FAR_CHIPS = (2, 4, 6)


def _exchange_copies(srcs, lands, sems, mode):
    send_sems, recv_sems, local_sems = sems
    scatter = mode == "scatter"
    x, y, c = _my_place()
    me = _flat(x, y, c)
    remote, local = [], []
    for a in range(len(srcs)):
        for r in (NEAR_PEERS if mode == "near" else range(1, N_DEV)):
            peer = (1 - x if r & 4 else x, 1 - y if r & 2 else y, 1 - c if r & 1 else c)
            remote.append(pltpu.make_async_remote_copy(
                src_ref=srcs[a].at[_flat(*peer)] if scatter else srcs[a], dst_ref=lands[a].at[me],
                send_sem=send_sems.at[a * N_PEERS + r - 1], recv_sem=recv_sems.at[a * N_PEERS + r - 1],
                device_id=peer, device_id_type=MESH))
        local.append(pltpu.make_async_copy(srcs[a].at[me] if scatter else srcs[a], lands[a].at[me], local_sems.at[a]))
    return remote, local


def _exchange_start(groups, modes, name):
    sizes = [len(g) for g in groups]
    srcs = [pltpu.with_memory_space_constraint(a, pltpu.HBM) for g in groups for a in g]
    n = len(srcs)
    scatter_of = [m == "scatter" for g, m in zip(groups, modes) for _ in g]
    lands = [pltpu.with_memory_space_constraint(lax.empty(a.shape if sc else (N_DEV,) + a.shape, a.dtype), pltpu.HBM)
             for a, sc in zip(srcs, scatter_of)]
    n_sem = SEMS_PER_GROUP * len(groups)

    def body(*refs):
        src_refs, land_refs, sem_refs, token = refs[:n], refs[n:2 * n], refs[2 * n:2 * n + n_sem], refs[-1]
        off = 0
        for gi, k in enumerate(sizes):
            remote, local = _exchange_copies(src_refs[off:off + k], land_refs[off:off + k],
                                             sem_refs[SEMS_PER_GROUP * gi:SEMS_PER_GROUP * (gi + 1)], modes[gi])
            for cp in local + remote:
                cp.start()
            off += k
        token[...] = jnp.zeros_like(token)

    sem_shapes = []
    for k in sizes:
        sem_shapes += [pltpu.SemaphoreType.DMA((k * N_PEERS,)), pltpu.SemaphoreType.DMA((k * N_PEERS,)),
                       pltpu.SemaphoreType.DMA((k,))]
    outs = pl.pallas_call(
        body, name=name,
        out_shape=sem_shapes + [pltpu.HBM(a.shape, a.dtype) for a in srcs + lands] + [S((8, 128), F32)],
        in_specs=[HBM] * (2 * n),
        out_specs=[SEM] * n_sem + [HBM] * (2 * n) + [pl.BlockSpec(memory_space=pltpu.VMEM)],
        input_output_aliases={i: n_sem + i for i in range(2 * n)},
        compiler_params=pltpu.CompilerParams(has_side_effects=SIDE_EFFECT),
    )(*srcs, *lands)
    sems, thru, token = outs[:n_sem], outs[n_sem:n_sem + 2 * n], outs[-1]
    per_group, off = [], 0
    for gi, k in enumerate(sizes):
        per_group.append((sems[SEMS_PER_GROUP * gi:SEMS_PER_GROUP * (gi + 1)], thru[off:off + k], thru[n + off:n + off + k]))
        off += k
    return per_group, token


def _exchange_wait(group, after, mode, name):
    sems, srcs, lands = group
    k = len(srcs)

    def body(*refs):
        remote, local = _exchange_copies(refs[:k], refs[k:2 * k], refs[2 * k:2 * k + SEMS_PER_GROUP], mode)
        for cp in remote:
            cp.wait_send()
            cp.wait_recv()
        for cp in local:
            cp.wait()

    outs = pl.pallas_call(
        body, name=name,
        out_shape=[pltpu.HBM(a.shape, a.dtype) for a in list(srcs) + list(lands)],
        in_specs=[HBM] * (2 * k) + [SEM] * SEMS_PER_GROUP + [ANY] * len(after),
        out_specs=[HBM] * (2 * k),
        input_output_aliases={i: i for i in range(2 * k)},
        compiler_params=pltpu.CompilerParams(has_side_effects=SIDE_EFFECT),
    )(*srcs, *lands, *sems, *after)
    return outs[k:]


def _forward_copies(lands, sems):
    send_sems, recv_sems = sems
    x, y, c = _my_place()
    mine, theirs = [], []
    for a in range(len(lands)):
        for k, r in enumerate(FAR_CHIPS):
            px, py = (1 - x if r & 4 else x), (1 - y if r & 2 else y)
            for out, core in ((mine, c), (theirs, 1 - c)):
                blk = lands[a].at[_flat(px, py, core)]
                out.append(pltpu.make_async_remote_copy(
                    src_ref=blk, dst_ref=blk, send_sem=send_sems.at[a * len(FAR_CHIPS) + k],
                    recv_sem=recv_sems.at[a * len(FAR_CHIPS) + k], device_id=(x, y, 1 - c), device_id_type=MESH))
    return mine, theirs


def _forward_start(groups, name):
    sizes = [len(g) for g in groups]
    lands = [a for g in groups for a in g]
    n = len(lands)
    n_sem = 2 * len(groups)

    def body(*refs):
        land_refs, sem_refs, token = refs[:n], refs[n:n + n_sem], refs[-1]
        off = 0
        for gi, k in enumerate(sizes):
            mine, _ = _forward_copies(land_refs[off:off + k], sem_refs[2 * gi:2 * gi + 2])
            for cp in mine:
                cp.start()
            off += k
        token[...] = jnp.zeros_like(token)

    sem_shapes = []
    for k in sizes:
        sem_shapes += [pltpu.SemaphoreType.DMA((k * len(FAR_CHIPS),))] * 2
    outs = pl.pallas_call(
        body, name=name,
        out_shape=sem_shapes + [pltpu.HBM(a.shape, a.dtype) for a in lands] + [S((8, 128), F32)],
        in_specs=[HBM] * n,
        out_specs=[SEM] * n_sem + [HBM] * n + [pl.BlockSpec(memory_space=pltpu.VMEM)],
        input_output_aliases={i: n_sem + i for i in range(n)},
        compiler_params=pltpu.CompilerParams(has_side_effects=SIDE_EFFECT),
    )(*lands)
    per_group, off = [], 0
    for gi, k in enumerate(sizes):
        per_group.append((outs[2 * gi:2 * gi + 2], outs[n_sem + off:n_sem + off + k]))
        off += k
    return per_group


def _forward_wait(group, after, name):
    sems, lands = group
    k = len(lands)

    def body(*refs):
        mine, theirs = _forward_copies(refs[:k], refs[k:k + 2])
        for cp in mine:
            cp.wait_send()
        for cp in theirs:
            cp.wait_recv()

    return pl.pallas_call(
        body, name=name,
        out_shape=[pltpu.HBM(a.shape, a.dtype) for a in lands],
        in_specs=[HBM] * k + [SEM] * 2 + [ANY] * len(after),
        out_specs=[HBM] * k,
        input_output_aliases={i: i for i in range(k)},
        compiler_params=pltpu.CompilerParams(has_side_effects=SIDE_EFFECT),
    )(*lands, *sems, *after)


def _row_tile(rows, cols):
    want = max(16, (128 * 1024) // cols)
    if rows <= want:
        return rows
    t = want - want % 16
    while rows % t:
        t -= 16
    return t


def _sum_parts(parts, name, after=()):
    P, R, C = parts.shape
    tr = _row_tile(R, C)

    def body(p_ref, *rest):
        o_ref = rest[-1]
        g = p_ref[0].astype(F32)
        for i in range(1, P):
            g = g + p_ref[i].astype(F32)
        o_ref[...] = g

    return pl.pallas_call(
        body, name=name, grid=(R // tr,),
        in_specs=[pl.BlockSpec((P, tr, C), lambda i: (0, i, 0))] + [ANY] * len(after),
        out_specs=pl.BlockSpec((tr, C), lambda i: (i, 0)),
        out_shape=S((R, C), F32),
        compiler_params=_cp("parallel"),
    )(parts, *after)


def _adamw(parts, w, m, v, name, after=()):
    P, R, C = parts.shape
    tr = _row_tile(R, C)
    c1 = 1.0 - ADAM_B1 ** ADAM_STEP
    c2 = 1.0 - ADAM_B2 ** ADAM_STEP

    def body(p_ref, w_ref, m_ref, v_ref, *rest):
        g_ref, d_ref, mo_ref, vo_ref = rest[len(after):]
        g = p_ref[0].astype(F32)
        for i in range(1, P):
            g = g + p_ref[i].astype(F32)
        mn = ADAM_B1 * m_ref[...] + (1.0 - ADAM_B1) * g
        vn = ADAM_B2 * v_ref[...] + (1.0 - ADAM_B2) * (g * g)
        g_ref[...] = g
        mo_ref[...] = mn
        vo_ref[...] = vn
        d_ref[...] = (-ADAM_LR) * ((mn / c1) / (jnp.sqrt(vn / c2) + ADAM_EPS) + ADAM_WD * w_ref[...])

    blk = pl.BlockSpec((tr, C), lambda i: (i, 0))
    return pl.pallas_call(
        body, name=name, grid=(R // tr,),
        in_specs=[pl.BlockSpec((P, tr, C), lambda i: (0, i, 0)), blk, blk, blk] + [ANY] * len(after),
        out_specs=[blk, blk, blk, blk],
        out_shape=[S((R, C), F32)] * 4,
        compiler_params=_cp("parallel"),
    )(parts, w, m, v, *after)


def _adamw_layer(parts, w3, m3, v3, layer, prev, name, after=()):
    P, R, C = parts.shape
    NL = w3.shape[0]
    tr = _row_tile(R, C)
    c1 = 1.0 - ADAM_B1 ** ADAM_STEP
    c2 = 1.0 - ADAM_B2 ** ADAM_STEP
    n_prev = 0 if prev is None else len(prev)

    def body(p_ref, w_ref, m_ref, v_ref, *rest):
        g_ref, d_ref, mo_ref, vo_ref = rest[n_prev + len(after):]
        g = p_ref[0].astype(F32)
        for i in range(1, P):
            g = g + p_ref[i].astype(F32)
        mn = ADAM_B1 * m_ref[...] + (1.0 - ADAM_B1) * g
        vn = ADAM_B2 * v_ref[...] + (1.0 - ADAM_B2) * (g * g)
        g_ref[...] = g
        mo_ref[...] = mn
        vo_ref[...] = vn
        d_ref[...] = (-ADAM_LR) * ((mn / c1) / (jnp.sqrt(vn / c2) + ADAM_EPS) + ADAM_WD * w_ref[...])

    blk = pl.BlockSpec((None, tr, C), lambda i: (layer, i, 0))
    return pl.pallas_call(
        body, name=name, grid=(R // tr,),
        in_specs=[pl.BlockSpec((P, tr, C), lambda i: (0, i, 0)), blk, blk, blk] + [ANY] * (n_prev + len(after)),
        out_specs=[blk, blk, blk, blk],
        out_shape=[S((NL, R, C), F32)] * 4,
        input_output_aliases={4 + k: k for k in range(n_prev)},
        compiler_params=_cp("parallel"),
    )(parts, w3, m3, v3, *(prev or ()), *after)


VMEM_WHOLE = pl.BlockSpec(memory_space=pltpu.VMEM)
LANES = 128


def _pack_vectors(vectors, starts, rows, name):
    def body(*refs):
        o_ref = refs[-1]
        o_ref[...] = jnp.zeros_like(o_ref)
        for v_ref, r0 in zip(refs[:-1], starts):
            for j in range(v_ref.shape[1] // LANES):
                o_ref[r0 + j:r0 + j + 1, :] = v_ref[:, j * LANES:(j + 1) * LANES]

    return pl.pallas_call(body, name=name, in_specs=[VMEM_WHOLE] * len(vectors), out_specs=VMEM_WHOLE,
                          out_shape=S((rows, LANES), F32))(*vectors)


def _adamw_vectors(g_pack, params, keep_rows, name, after=()):
    n = len(params)
    P = g_pack.shape[0]
    c1 = 1.0 - ADAM_B1 ** ADAM_STEP
    c2 = 1.0 - ADAM_B2 ** ADAM_STEP

    def body(g_ref, *refs):
        ins, outs = refs[:3 * n], refs[3 * n + len(after):]
        gs = g_ref[0]
        for i in range(1, P):
            gs = gs + g_ref[i]
        for pi, (_, _, _, slots) in enumerate(params):
            w_ref, m_ref, v_ref = ins[3 * pi:3 * pi + 3]
            g_out, d_out, m_out, v_out = outs[4 * pi:4 * pi + 4]
            for idx, row in slots:
                g = gs[row:row + 1, :]
                mn = ADAM_B1 * m_ref[idx] + (1.0 - ADAM_B1) * g
                vn = ADAM_B2 * v_ref[idx] + (1.0 - ADAM_B2) * (g * g)
                g_out[idx] = g
                m_out[idx] = mn
                v_out[idx] = vn
                d_out[idx] = (-ADAM_LR) * ((mn / c1) / (jnp.sqrt(vn / c2) + ADAM_EPS) + ADAM_WD * w_ref[idx])
        outs[-1][...] = jnp.concatenate([gs[r:r + 1, :] for r in keep_rows], axis=0)

    flat = [a for w, m, v, _ in params for a in (w, m, v)]
    out_shape = [S(w.shape, F32) for w, _, _, _ in params for _ in range(4)] + [S((len(keep_rows), LANES), F32)]
    outs = pl.pallas_call(
        body, name=name,
        in_specs=[VMEM_WHOLE] * (1 + len(flat)) + [ANY] * len(after),
        out_specs=[VMEM_WHOLE] * len(out_shape), out_shape=out_shape,
    )(g_pack, *flat, *after)
    return [outs[4 * i:4 * i + 4] for i in range(n)], outs[-1]


def _adamw_nd(parts, w, m, v, name, after=()):
    shp = w.shape
    C = shp[-1]
    outs = _adamw(parts.reshape(parts.shape[0], -1, C), w.reshape(-1, C), m.reshape(-1, C), v.reshape(-1, C), name, after)
    return [o.reshape(shp) for o in outs]


TILE_ROWS = 8


REP_SMALL_ROWS = 128
REP_GRAD_STARTS = (0, 8, 16, 24, 32, 40, 48, 56, 64)
REP_SMALL_STARTS = (0, 16, 32, 40, 48, 56)
REP_LOSS_ROW = 64


def _small_pack(cw, ba, bx, lam):
    pad8 = lambda a: jnp.pad(a, ((0, TILE_ROWS - a.shape[0]), (0, 0)))
    return jnp.concatenate([pad8(cw[0, :, 0, :]), pad8(ba[0]), pad8(bx[0]), pad8(lam[0]),
                            jnp.zeros((PV_ROWS - PV_CONV_B, LRU_BW), F32)], axis=0)


def kernel(x, norm_mix_g, norm_mlp_g, rg_w_in, rg_conv_w, rg_conv_b, rg_w_a, rg_b_a, rg_w_x, rg_b_x, rg_lam, rg_w_out, at_w_qkv, at_q_g, at_k_g, at_w_o, mlp_w_up, mlp_w_down, final_g, loss_target, m_norm_mix_g, m_norm_mlp_g, m_rg_w_in, m_rg_conv_w, m_rg_conv_b, m_rg_w_a, m_rg_b_a, m_rg_w_x, m_rg_b_x, m_rg_lam, m_rg_w_out, m_at_w_qkv, m_at_q_g, m_at_k_g, m_at_w_o, m_mlp_w_up, m_mlp_w_down, m_final_g, v_norm_mix_g, v_norm_mlp_g, v_rg_w_in, v_rg_conv_w, v_rg_conv_b, v_rg_w_a, v_rg_b_a, v_rg_w_x, v_rg_b_x, v_rg_lam, v_rg_w_out, v_at_w_qkv, v_at_q_g, v_at_k_g, v_at_w_o, v_mlp_w_up, v_mlp_w_down, v_final_g):
    D = x.shape[-1]
    bf = lambda a: a.astype(BF16)

    sp_w = _small_pack(rg_conv_w, rg_b_a, rg_b_x, rg_lam)
    started, _ = _exchange_start(
        [[bf(rg_w_in[0]), sp_w], [bf(rg_w_out[0])], [bf(mlp_w_up[0]), bf(mlp_w_down[0])],
         [bf(at_w_qkv[0]).T, bf(at_w_o[0])], [bf(mlp_w_up[1]), bf(mlp_w_down[1])]],
        ["near", "near", "near", "gather", "gather"], "gather_start")
    gathers = dict(zip((0, 1, 4, 2, 3), started))
    forwards = {}

    def fetch(stage, after):
        after = tuple(after)
        if stage == 0:
            got = _exchange_wait(gathers[0], after, "near", "gather_wait0")
            g_in, g_sp = _forward_wait(_forward_start([got], "forward_start0")[0], (), "forward_wait0")
            pvec = g_sp.transpose(1, 0, 2).reshape(PV_ROWS, D)
            pvec = jnp.concatenate([pvec[:PV_CONV_B], jnp.broadcast_to(rg_conv_b, (PV_ROWS - PV_CONV_B, D))], axis=0)
            return dict(w_in=g_in, pvec=pvec)
        if stage == 1:
            near = [_exchange_wait(gathers[s], after, "near", "gather_wait%d" % s) for s in (1, 4)]
            f_out, forwards[4] = _forward_start(near, "forward_start1")
            g_out, = _forward_wait(f_out, (), "forward_wait1")
            return dict(w_out=g_out.reshape(D, D))
        if stage == 4:
            g_up0, g_dn0 = _forward_wait(forwards[4], after, "forward_wait4")
            return dict(w_up0=g_up0, w_down0=g_dn0.reshape(-1, D))
        got = _exchange_wait(gathers[stage], after, "gather", "gather_wait%d" % stage)
        if stage == 2:
            return dict(w_qkv=got[0].reshape(-1, QKV_NB, D), w_o=got[1].reshape(D, D))
        return dict(w_up1=got[0], w_down1=got[1].reshape(-1, D))

    scatters = {}

    def send(stage, g):
        if stage == 3:
            arrs = [g["w_up1"], g["w_down1"].reshape(N_DEV, -1, D)]
        elif stage == 2:
            arrs = [g["w_qkv"].reshape(N_DEV, -1, D), g["w_o"].reshape(N_DEV, -1, D)]
        elif stage == 1:
            arrs = [g["w_up0"], g["w_down0"].reshape(N_DEV, -1, D)]
        elif stage == 4:
            arrs = [g["w_out"].reshape(N_DEV, -1, D), g["pvec"].reshape(PV_ROWS, N_DEV, LRU_BW).transpose(1, 0, 2),
                    bf(g["wa"]).reshape(N_DEV, -1, LANES), bf(g["wx"]).reshape(N_DEV, -1, LANES)]
        elif stage == 0:
            arrs = [g["w_in"]]
        else:
            small = _pack_vectors(g["g_mix"] + g["g_mlp"] + [g["g_fin"], g["conv_b"], g["qg"], g["kg"], g["loss"]],
                                  REP_GRAD_STARTS, REP_SMALL_ROWS, "pack_rep_small")
            arrs = [small.reshape(N_DEV, -1, LANES)]
        (group,), token = _exchange_start([arrs], ["scatter"], "scatter_start%d" % (stage % 6))
        scatters[stage] = (group, token)
        return (token,)

    w = dict(g_mix=norm_mix_g, g_mlp=norm_mlp_g, g_fin=final_g[None], qg=at_q_g, kg=at_k_g,
             wa=bf(rg_w_a[0]), wx=bf(rg_w_x[0]))
    early = {}

    def early_adam(after):
        r_up1, r_dn1 = _exchange_wait(scatters[3][0], tuple(after), "scatter", "scatter_wait3")
        r_qkv, r_o = _exchange_wait(scatters[2][0], (r_up1,), "scatter", "scatter_wait2")
        jobs = [(r_up1, mlp_w_up, m_mlp_w_up, v_mlp_w_up, 1), (r_dn1, mlp_w_down, m_mlp_w_down, v_mlp_w_down, 1),
                (r_o, at_w_o, m_at_w_o, v_at_w_o, 0)]
        return jobs, lambda outs: early.update(up=outs[0], dn=outs[1], o=outs[2], r_qkv=r_qkv)

    grad_x = _local_step(x, loss_target, w, fetch, send, early_adam)

    res = {}
    up, dn = early["up"], early["dn"]
    r_out, r_sp, r_wa, r_wx = _exchange_wait(scatters[4][0], (scatters[-1][1],), "scatter", "scatter_wait4")
    wa_part = _sum_parts(r_wa, "reduce_w_a")
    wx_part = _sum_parts(r_wx, "reduce_w_x", (wa_part,))
    (rep_gather,), rep_token = _exchange_start([[wa_part, wx_part]], ["gather"], "rep_gather_start")
    tr = lambda a: a[0].T
    qkv_t = _adamw_nd(early["r_qkv"], tr(at_w_qkv), tr(m_at_w_qkv), tr(v_at_w_qkv), "adam_at_w_qkv", (rep_token,))
    res["at_w_qkv"] = [o.T[None] for o in qkv_t]
    res["at_w_o"] = early["o"]
    r_up0, r_dn0 = _exchange_wait(scatters[1][0], (qkv_t[1],), "scatter", "scatter_wait1")
    res["mlp_w_up"] = _adamw_layer(r_up0, mlp_w_up, m_mlp_w_up, v_mlp_w_up, 0, up, "adam_mlp_w_up0")
    res["mlp_w_down"] = _adamw_layer(r_dn0, mlp_w_down, m_mlp_w_down, v_mlp_w_down, 0, dn, "adam_mlp_w_down0",
                                     after=(res["mlp_w_up"][1],))
    r_in, = _exchange_wait(scatters[0][0], (res["mlp_w_down"][1],), "scatter", "scatter_wait0")
    res["rg_w_in"] = _adamw_nd(r_in[:, None], rg_w_in, m_rg_w_in, v_rg_w_in, "adam_rg_w_in")
    res["rg_w_out"] = _adamw_nd(r_out[:, None], rg_w_out, m_rg_w_out, v_rg_w_out, "adam_rg_w_out", (res["rg_w_in"][1],))
    whole, lane = slice(None), slice(0, 1)
    two_rows = lambda r0: [((0, slice(d, d + 1), whole), r0 + d) for d in range(2)]
    (res["rg_conv_w"], res["rg_b_a"], res["rg_b_x"], res["rg_lam"]), _ = _adamw_vectors(
        r_sp, [(rg_conv_w, m_rg_conv_w, v_rg_conv_w, [((0, t, lane, whole), PV_CONV_W + t) for t in range(CONV_W)]),
               (rg_b_a, m_rg_b_a, v_rg_b_a, two_rows(PV_B_A)), (rg_b_x, m_rg_b_x, v_rg_b_x, two_rows(PV_B_X)),
               (rg_lam, m_rg_lam, v_rg_lam, two_rows(PV_LAM))], [0], "adam_small", (res["rg_w_out"][1],))

    r_small, = _exchange_wait(scatters[-1][0], (res["rg_lam"][1],), "scatter", "scatter_wait5")
    small_sum, = _all_gather([_sum_parts(r_small, "reduce_rep_small")], "gather_replicated")
    wa_sum, wx_sum = _exchange_wait(rep_gather, (small_sum,), "gather", "rep_gather_wait")
    rows = lambda a: a.reshape(-1, LANES)
    wa_res = _adamw(wa_sum.reshape(1, -1, LANES), rows(rg_w_a), rows(m_rg_w_a), rows(v_rg_w_a), "adam_rg_w_a")
    wx_res = _adamw(wx_sum.reshape(1, -1, LANES), rows(rg_w_x), rows(m_rg_w_x), rows(v_rg_w_x), "adam_rg_w_x", (wa_res[1],))
    res["rg_w_a"] = [o.reshape(rg_w_a.shape) for o in wa_res]
    res["rg_w_x"] = [o.reshape(rg_w_x.shape) for o in wx_res]

    def vec_slots(a, r0):
        per = a.shape[1] // LANES
        return [((slice(l, l + 1), slice(LANES * j, LANES * (j + 1))), r0 + l * per + j)
                for l in range(a.shape[0]) for j in range(per)]

    fin = [final_g[None], m_final_g[None], v_final_g[None]]
    vecs = [(norm_mix_g, m_norm_mix_g, v_norm_mix_g), (norm_mlp_g, m_norm_mlp_g, v_norm_mlp_g), fin,
            (rg_conv_b, m_rg_conv_b, v_rg_conv_b), (at_q_g, m_at_q_g, v_at_q_g), (at_k_g, m_at_k_g, v_at_k_g)]
    outs, kept = _adamw_vectors(
        small_sum.reshape(1, -1, LANES),
        [(wv, mv, vv, vec_slots(wv, r0)) for (wv, mv, vv), r0 in zip(vecs, REP_SMALL_STARTS)], [REP_LOSS_ROW],
        "adam_rep_small", (wx_res[1],))
    for nm, o in zip(["norm_mix_g", "norm_mlp_g", "final_g", "rg_conv_b", "at_q_g", "at_k_g"], outs):
        res[nm] = [a[0] for a in o] if nm == "final_g" else o
    loss = kept[0, 0]

    order = ["norm_mix_g", "norm_mlp_g", "rg_w_in", "rg_conv_w", "rg_conv_b", "rg_w_a", "rg_b_a", "rg_w_x", "rg_b_x",
             "rg_lam", "rg_w_out", "at_w_qkv", "at_q_g", "at_k_g", "at_w_o", "mlp_w_up", "mlp_w_down", "final_g"]
    return (loss, grad_x, *[res[nm][k] for k in range(4) for nm in order])
```

```python
import math

import jax
import jax.numpy as jnp
from jax import lax
from jax.experimental import pallas as pl
from jax.experimental.pallas import tpu as pltpu

F32 = jnp.float32
BF16 = jnp.bfloat16
S = jax.ShapeDtypeStruct

EPS = 1e-6
HEAD_DIM = 128
N_KV = 2
GRID_W = 64
ROPE_THETA = 10000.0
LRU_BW = 128
RG_C = 8.0
CONV_W = 4
N_DEV = 8
N_SEG = 8
SCAN_UNROLL = 8
TN_STEP_COLS = 512
PROJ_TM = 1024
PROJ_CHAINS = 2
MLP_TM = 512
VMEM_LIMIT_V7X = 56 * 1024 * 1024
SOFTMAX_SCALE = 1.0 / math.sqrt(HEAD_DIM)
GELU_K = math.sqrt(2.0 / math.pi)
GELU_C = 0.044715

ADAM_LR = 0.001
ADAM_B1 = 0.9
ADAM_B2 = 0.999
ADAM_EPS = 1e-08
ADAM_WD = 0.01
ADAM_STEP = 10

NT = (((1,), (1,)), ((), ()))
TN = (((0,), (0,)), ((), ()))


def _cp(*sem):
    return pltpu.CompilerParams(dimension_semantics=sem, vmem_limit_bytes=VMEM_LIMIT_V7X)


def _rms_r(xv):
    return lax.rsqrt(jnp.mean(xv * xv, axis=-1, keepdims=True) + EPS)


def _rms_bwd(dh, xv, g):
    r = _rms_r(xv)
    xh = xv * r
    dg = jnp.sum(dh * xh, axis=0, keepdims=True)
    dxh = dh * g
    dx = r * (dxh - xh * jnp.mean(dxh * xh, axis=-1, keepdims=True))
    return dx, dg


def _dot(a, b):
    return jnp.dot(a, b, preferred_element_type=F32)


def _dot_nt(a, b):
    return lax.dot_general(a, b, NT, preferred_element_type=F32)


def _dot_tn(a, b):
    return lax.dot_general(a, b, TN, preferred_element_type=F32)


def _norm_matmul(x, g, wblk, name, out_dtype=F32):
    T, D = x.shape
    NB, _, nb = wblk.shape
    tm = min(T, PROJ_TM)

    def body(x_ref, g_ref, w_ref, o_ref, h_ref):
        for c in range(PROJ_CHAINS):
            rows = slice(c * tm // PROJ_CHAINS, (c + 1) * tm // PROJ_CHAINS)
            xv = x_ref[rows, :]
            hb = (xv * _rms_r(xv) * g_ref[...]).astype(BF16)
            h_ref[rows, :] = hb
            for q in range(NB):
                o_ref[rows, q * nb:(q + 1) * nb] = _dot(hb, w_ref[q]).astype(o_ref.dtype)

    return pl.pallas_call(
        body, name=name, grid=(T // tm,),
        in_specs=[pl.BlockSpec((tm, D), lambda i: (i, 0)),
                  pl.BlockSpec((1, D), lambda i: (0, 0)),
                  pl.BlockSpec(wblk.shape, lambda i: (0, 0, 0))],
        out_specs=[pl.BlockSpec((tm, NB * nb), lambda i: (i, 0)),
                   pl.BlockSpec((tm, D), lambda i: (i, 0))],
        out_shape=[S((T, NB * nb), out_dtype), S((T, D), BF16)],
        compiler_params=_cp("parallel"),
    )(x, g, wblk)


def _matmul_res(a, w, res, name):
    T, K = a.shape
    N = w.shape[1]
    tm = min(T, PROJ_TM)

    def body(a_ref, w_ref, r_ref, o_ref):
        o_ref[...] = r_ref[...] + _dot(a_ref[...], w_ref[...])

    return pl.pallas_call(
        body, name=name, grid=(T // tm,),
        in_specs=[pl.BlockSpec((tm, K), lambda i: (i, 0)),
                  pl.BlockSpec((K, N), lambda i: (0, 0)),
                  pl.BlockSpec((tm, N), lambda i: (i, 0))],
        out_specs=pl.BlockSpec((tm, N), lambda i: (i, 0)),
        out_shape=S((T, N), F32),
        compiler_params=_cp("parallel"),
    )(a, w, res)


def _matmul_nt(a, w, name, out_dtype, after=()):
    T, N = a.shape
    K = w.shape[0]
    tm = min(T, PROJ_TM)

    def body(a_ref, w_ref, *rest):
        o_ref, ab_ref = rest[len(after):]
        ab = a_ref[...].astype(BF16)
        ab_ref[...] = ab
        o_ref[...] = _dot_nt(ab, w_ref[...]).astype(o_ref.dtype)

    return pl.pallas_call(
        body, name=name, grid=(T // tm,),
        in_specs=[pl.BlockSpec((tm, N), lambda i: (i, 0)),
                  pl.BlockSpec((K, N), lambda i: (0, 0))] + [pl.BlockSpec(memory_space=pl.ANY)] * len(after),
        out_specs=[pl.BlockSpec((tm, K), lambda i: (i, 0)),
                   pl.BlockSpec((tm, N), lambda i: (i, 0))],
        out_shape=[S((T, K), out_dtype), S((T, N), BF16)],
        compiler_params=_cp("parallel"),
    )(a, w, *after)


def _matmul_tn(a, b3, nb, name, blocked, after=()):
    T, M = a.shape
    SB, _, N = b3.shape
    per = N // nb
    NB = SB * per
    tk = min(T, 1024)
    nk = T // tk
    jb = max(1, TN_STEP_COLS // nb) if blocked else 1
    assert per % jb == 0
    if blocked:
        out_spec, out_shape = pl.BlockSpec((jb, M, nb), lambda j: (j, 0, 0)), S((NB, M, nb), BF16)
    else:
        assert SB == 1
        out_spec, out_shape = pl.BlockSpec((M, nb), lambda j: (0, j)), S((M, N), BF16)

    def body(a_ref, b_ref, *rest):
        o_ref = rest[-1]
        for q in range(jb):
            acc = None
            for k in range(nk):
                rows = slice(k * tk, (k + 1) * tk)
                part = _dot_tn(a_ref[rows, :], b_ref[rows, q * nb:(q + 1) * nb])
                acc = part if acc is None else acc + part
            if blocked:
                o_ref[q] = acc.astype(BF16)
            else:
                o_ref[...] = acc.astype(BF16)

    return pl.pallas_call(
        body, name=name, grid=(NB // jb,),
        in_specs=[pl.BlockSpec((T, M), lambda j: (0, 0), pipeline_mode=pl.Buffered(1)),
                  pl.BlockSpec((None, T, jb * nb), lambda j: ((j * jb) // per, 0, ((j * jb) % per) // jb))]
        + [pl.BlockSpec(memory_space=pl.ANY)] * len(after),
        out_specs=out_spec,
        out_shape=out_shape,
        compiler_params=_cp("parallel"),
    )(a, b3, *after)


def _nt_normbwd(dz3, wblk, x, g, dres, name, after=(), transposed=False):
    T, D = x.shape
    NB, nb = wblk.shape[0], wblk.shape[1 if transposed else 2]
    mm = _dot if transposed else _dot_nt
    SB, _, N = dz3.shape
    per = N // nb
    tm = min(T, PROJ_TM)

    def body(dz_ref, w_ref, x_ref, g_ref, dr_ref, *rest):
        dx_ref, dg_ref = rest[len(after):]

        @pl.when(pl.program_id(0) == 0)
        def _():
            dg_ref[...] = jnp.zeros_like(dg_ref)

        for c in range(PROJ_CHAINS):
            rows = slice(c * tm // PROJ_CHAINS, (c + 1) * tm // PROJ_CHAINS)
            dh = None
            for q in range(NB):
                cols = slice((q % per) * nb, (q % per + 1) * nb)
                part = mm(dz_ref[q // per, rows, cols], w_ref[q])
                dh = part if dh is None else dh + part
            dx, dg = _rms_bwd(dh, x_ref[rows, :], g_ref[...])
            dx_ref[rows, :] = dr_ref[rows, :] + dx
            dg_ref[...] += dg

    return pl.pallas_call(
        body, name=name, grid=(T // tm,),
        in_specs=[pl.BlockSpec((SB, tm, N), lambda i: (0, i, 0)),
                  pl.BlockSpec(wblk.shape, lambda i: (0, 0, 0)),
                  pl.BlockSpec((tm, D), lambda i: (i, 0)),
                  pl.BlockSpec((1, D), lambda i: (0, 0)),
                  pl.BlockSpec((tm, D), lambda i: (i, 0))] + [pl.BlockSpec(memory_space=pl.ANY)] * len(after),
        out_specs=[pl.BlockSpec((tm, D), lambda i: (i, 0)),
                   pl.BlockSpec((1, D), lambda i: (0, 0))],
        out_shape=[S((T, D), F32), S((1, D), F32)],
        compiler_params=_cp("arbitrary"),
    )(dz3, wblk, x, g, dres, *after)


def _loss_head(xv, tv, gv, D):
    err = xv * _rms_r(xv) * gv - tv
    e2 = jnp.sum(jnp.sum(err * err, axis=-1, keepdims=True), axis=0, keepdims=True)
    dx, dg = _rms_bwd(err * (1.0 / D), xv, gv)
    return (0.5 / D) * e2, dx, dg


def _mlp_fwd(x, g, wup, wdown, name, head=None):
    T, D = x.shape
    NB, _, fb = wup.shape
    tm = min(T, MLP_TM)
    n_head = 0 if head is None else 2

    def body(x_ref, g_ref, wu_ref, wd_ref, *rest):
        xo_ref, a_ref, h_ref = rest[n_head:n_head + 3]
        xv = x_ref[...]
        hb = (xv * _rms_r(xv) * g_ref[...]).astype(BF16)
        h_ref[...] = hb
        acc = xv
        for j in range(NB):
            a = _dot(hb, wu_ref[j])
            a_ref[:, j * fb:(j + 1) * fb] = a.astype(BF16)
            u = jnp.maximum(a, 0.0)
            acc = acc + _dot((u * u).astype(BF16), wd_ref[j * fb:(j + 1) * fb, :])

        if head is None:
            xo_ref[...] = acc
        else:
            t_ref, gf_ref = rest[:2]
            loss_ref, dgf_ref = rest[n_head + 3:n_head + 5]

            @pl.when(pl.program_id(0) == 0)
            def _():
                loss_ref[...] = jnp.zeros_like(loss_ref)
                dgf_ref[...] = jnp.zeros_like(dgf_ref)

            e2, dx, dg = _loss_head(acc, t_ref[...], gf_ref[...], D)
            xo_ref[...] = dx
            loss_ref[...] += e2
            dgf_ref[...] += dg

    row = pl.BlockSpec((tm, D), lambda i: (i, 0))
    vec = pl.BlockSpec((1, D), lambda i: (0, 0))
    once = pl.Buffered(1)
    in_specs = [row, vec, pl.BlockSpec((NB, D, fb), lambda i: (0, 0, 0), pipeline_mode=once),
                pl.BlockSpec((NB * fb, D), lambda i: (0, 0), pipeline_mode=once)]
    out_specs = [row, pl.BlockSpec((tm, NB * fb), lambda i: (i, 0)), row]
    out_shape = [S((T, D), F32), S((T, NB * fb), BF16), S((T, D), BF16)]
    if head is not None:
        in_specs += [row, vec]
        out_specs += [pl.BlockSpec((1, 128), lambda i: (0, 0)), vec]
        out_shape += [S((1, 128), F32), S((1, D), F32)]
    return pl.pallas_call(
        body, name=name, grid=(T // tm,),
        in_specs=in_specs, out_specs=out_specs, out_shape=out_shape,
        compiler_params=_cp("parallel" if head is None else "arbitrary"),
    )(x, g, wup, wdown, *(head or ()))


def _mlp_bwd_dx(x, dout, a, g, wup, wdown, name):
    T, D = x.shape
    NB, _, fb = wup.shape
    tm = min(T, MLP_TM)

    def body(x_ref, do_ref, a_ref, g_ref, wu_ref, wd_ref, dx_ref, da_ref, dob_ref, dg_ref):
        @pl.when(pl.program_id(0) == 0)
        def _():
            dg_ref[...] = jnp.zeros_like(dg_ref)

        dov = do_ref[...]
        dob = dov.astype(BF16)
        dob_ref[...] = dob
        dh = None
        for j in range(NB):
            cols = slice(j * fb, (j + 1) * fb)
            du2 = _dot_nt(dob, wd_ref[cols, :])
            u = jnp.maximum(a_ref[:, cols].astype(F32), 0.0)
            da = (du2 * (2.0 * u)).astype(BF16)
            da_ref[:, cols] = da
            part = _dot_nt(da, wu_ref[j])
            dh = part if dh is None else dh + part
        dx, dg = _rms_bwd(dh, x_ref[...], g_ref[...])
        dx_ref[...] = dov + dx
        dg_ref[...] += dg

    row = pl.BlockSpec((tm, D), lambda i: (i, 0))
    wide = pl.BlockSpec((tm, NB * fb), lambda i: (i, 0))
    vec = pl.BlockSpec((1, D), lambda i: (0, 0))
    once = pl.Buffered(1)
    return pl.pallas_call(
        body, name=name, grid=(T // tm,),
        in_specs=[row, row, wide, vec, pl.BlockSpec((NB, D, fb), lambda i: (0, 0, 0), pipeline_mode=once),
                  pl.BlockSpec((NB * fb, D), lambda i: (0, 0), pipeline_mode=once)],
        out_specs=[row, wide, row, vec],
        out_shape=[S((T, D), F32), S((T, NB * fb), BF16), S((T, D), BF16), S((1, D), F32)],
        compiler_params=_cp("arbitrary"),
    )(x, dout, a, g, wup, wdown)


def _adam_update(g, w, m, v):
    mn = ADAM_B1 * m + (1.0 - ADAM_B1) * g
    vn = ADAM_B2 * v + (1.0 - ADAM_B2) * (g * g)
    c1 = 1.0 - ADAM_B1 ** ADAM_STEP
    c2 = 1.0 - ADAM_B2 ** ADAM_STEP
    return (-ADAM_LR) * ((mn / c1) / (jnp.sqrt(vn / c2) + ADAM_EPS) + ADAM_WD * w), mn, vn


def _mlp_bwd_dw(h, da, a, dob, fb, name, adam_jobs=()):
    T, D = h.shape
    F = a.shape[1]
    NB = F // fb
    tk = min(T, 1024)
    nk = T // tk
    nj = len(adam_jobs)

    def body(h_ref, da_ref, a_ref, dob_ref, *rest):
        dwu_ref, dwd_ref = rest[4 * nj:4 * nj + 2]
        au = ad = None
        for k in range(nk):
            rows = slice(k * tk, (k + 1) * tk)
            pu = _dot_tn(h_ref[rows, :], da_ref[rows, :])
            u = jnp.maximum(a_ref[rows, :].astype(F32), 0.0)
            pd = _dot_tn((u * u).astype(BF16), dob_ref[rows, :])
            au, ad = (pu, pd) if au is None else (au + pu, ad + pd)
        dwu_ref[...] = au.astype(BF16)
        dwd_ref[...] = ad.astype(BF16)
        for q in range(nj):
            p_ref, w_ref, m_ref, v_ref = rest[4 * q:4 * q + 4]
            g_out, d_out, m_out, v_out = rest[4 * nj + 2 + 4 * q:4 * nj + 6 + 4 * q]
            g = p_ref[0].astype(F32)
            for i in range(1, p_ref.shape[0]):
                g = g + p_ref[i].astype(F32)
            d, mn, vn = _adam_update(g, w_ref[...], m_ref[...], v_ref[...])
            g_out[...], d_out[...], m_out[...], v_out[...] = g, d, mn, vn

    once = pl.Buffered(1)
    in_specs = [pl.BlockSpec((T, D), lambda j: (0, 0), pipeline_mode=once),
                pl.BlockSpec((T, fb), lambda j: (0, j)),
                pl.BlockSpec((T, fb), lambda j: (0, j)),
                pl.BlockSpec((T, D), lambda j: (0, 0), pipeline_mode=once)]
    out_specs = [pl.BlockSpec((None, D, fb), lambda j: (j, 0, 0)), pl.BlockSpec((fb, D), lambda j: (j, 0))]
    out_shape = [S((NB, D, fb), BF16), S((F, D), BF16)]
    operands = [h, da, a, dob]
    for parts, w3, m3, v3, layer in adam_jobs:
        P, R, C = parts.shape
        blk = pl.BlockSpec((None, R // NB, C), lambda j, layer=layer: (layer, j, 0))
        in_specs += [pl.BlockSpec((P, R // NB, C), lambda j: (0, j, 0)), blk, blk, blk]
        out_specs += [blk] * 4
        out_shape += [S(w3.shape, F32)] * 4
        operands += [parts, w3, m3, v3]
    outs = pl.pallas_call(
        body, name=name, grid=(NB,),
        in_specs=in_specs, out_specs=out_specs, out_shape=out_shape,
        compiler_params=_cp("parallel"),
    )(*operands)
    return outs[0], outs[1], [outs[2 + 4 * q:6 + 4 * q] for q in range(nj)]


def _rope_tables(L):
    nf = HEAD_DIM // 4
    t = jnp.arange(L, dtype=jnp.int32)
    row = (t // GRID_W).astype(F32)
    col = (t % GRID_W).astype(F32)
    inv = ROPE_THETA ** (-jnp.arange(nf, dtype=F32) / nf)
    ar = row[:, None] * inv
    ac = col[:, None] * inv
    cos = jnp.concatenate([jnp.cos(ar), jnp.cos(ar), jnp.cos(ac), jnp.cos(ac)], axis=-1)
    sin = jnp.concatenate([-jnp.sin(ar), jnp.sin(ar), -jnp.sin(ac), jnp.sin(ac)], axis=-1)
    return cos, sin


def _swap32(x):
    lane = lax.broadcasted_iota(jnp.int32, x.shape, 1)
    up = pltpu.roll(x, HEAD_DIM - 32, 1)
    down = pltpu.roll(x, 32, 1)
    return jnp.where((lane % 64) < 32, up, down)


def _qkv_proj(x, g, wt_blk, qg, kg, cos, sin, L, name):
    T, D = x.shape
    NB, nb, _ = wt_blk.shape
    W = NB * nb
    nh = W // HEAD_DIM - 2 * N_KV
    tm = min(L, 1024)
    lb = L // tm

    def body(x_ref, g_ref, w_ref, qg_ref, kg_ref, cos_ref, sin_ref, qkv_ref, h_ref, q_ref, k_ref, v_ref):
        for ch in range(PROJ_CHAINS):
            rows = slice(ch * tm // PROJ_CHAINS, (ch + 1) * tm // PROJ_CHAINS)
            xv = x_ref[rows, :]
            hb = (xv * _rms_r(xv) * g_ref[...]).astype(BF16)
            h_ref[rows, :] = hb
            for j in range(NB):
                qkv_ref[rows, j * nb:(j + 1) * nb] = _dot_nt(hb, w_ref[j])
            c = cos_ref[rows, :]
            s = sin_ref[rows, :]
            for h in range(nh + N_KV):
                xh = qkv_ref[rows, h * HEAD_DIM:(h + 1) * HEAD_DIM]
                gv = qg_ref[...] if h < nh else kg_ref[...]
                y = xh * _rms_r(xh) * gv
                y = (y * c + _swap32(y) * s).astype(BF16)
                if h < nh:
                    q_ref[rows, h * HEAD_DIM:(h + 1) * HEAD_DIM] = y
                else:
                    k_ref[rows, (h - nh) * HEAD_DIM:(h - nh + 1) * HEAD_DIM] = y
            v_ref[rows, :] = qkv_ref[rows, (nh + N_KV) * HEAD_DIM:].astype(BF16)

    row = lambda cols: pl.BlockSpec((tm, cols), lambda i: (i, 0))
    vec = lambda cols: pl.BlockSpec((1, cols), lambda i: (0, 0))
    table = pl.BlockSpec((tm, HEAD_DIM), lambda i: (i % lb, 0))
    return pl.pallas_call(
        body, name=name, grid=(T // tm,),
        in_specs=[row(D), vec(D), pl.BlockSpec(wt_blk.shape, lambda i: (0, 0, 0)), vec(HEAD_DIM), vec(HEAD_DIM), table, table],
        out_specs=[row(W), row(D), row(nh * HEAD_DIM), row(N_KV * HEAD_DIM), row(N_KV * HEAD_DIM)],
        out_shape=[S((T, W), F32), S((T, D), BF16), S((T, nh * HEAD_DIM), BF16), S((T, N_KV * HEAD_DIM), BF16),
                   S((T, N_KV * HEAD_DIM), BF16)],
        compiler_params=_cp("parallel"),
    )(x, g, wt_blk, qg, kg, cos, sin)


def _qk_prep_bwd(qkv, dq, dk, dv, qg, kg, cos, sin, L, name):
    T, W = qkv.shape
    nh = W // HEAD_DIM - 2 * N_KV
    tm = min(L, 1024)
    lb = L // tm

    def body(qkv_ref, dq_ref, dk_ref, dv_ref, qg_ref, kg_ref, cos_ref, sin_ref, dz_ref, dqg_ref, dkg_ref):
        @pl.when(pl.program_id(0) == 0)
        def _():
            dqg_ref[...] = jnp.zeros_like(dqg_ref)
            dkg_ref[...] = jnp.zeros_like(dkg_ref)

        c = cos_ref[...]
        s = sin_ref[...]
        for h in range(nh + N_KV):
            cols = slice(h * HEAD_DIM, (h + 1) * HEAD_DIM)
            if h < nh:
                dout, gv, dg_ref = dq_ref[:, cols], qg_ref[...], dqg_ref
            else:
                kc = slice((h - nh) * HEAD_DIM, (h - nh + 1) * HEAD_DIM)
                dout, gv, dg_ref = dk_ref[:, kc], kg_ref[...], dkg_ref
            dy = dout * c - _swap32(dout) * s
            dx, dg = _rms_bwd(dy, qkv_ref[:, cols], gv)
            dg_ref[...] += dg
            dz_ref[:, cols] = dx.astype(BF16)
        dz_ref[:, (nh + N_KV) * HEAD_DIM:] = dv_ref[...].astype(BF16)

    return pl.pallas_call(
        body, name=name, grid=(T // tm,),
        in_specs=[pl.BlockSpec((tm, W), lambda i: (i, 0)),
                  pl.BlockSpec((tm, nh * HEAD_DIM), lambda i: (i, 0)),
                  pl.BlockSpec((tm, N_KV * HEAD_DIM), lambda i: (i, 0)),
                  pl.BlockSpec((tm, N_KV * HEAD_DIM), lambda i: (i, 0)),
                  pl.BlockSpec((1, HEAD_DIM), lambda i: (0, 0)),
                  pl.BlockSpec((1, HEAD_DIM), lambda i: (0, 0)),
                  pl.BlockSpec((tm, HEAD_DIM), lambda i: (i % lb, 0)),
                  pl.BlockSpec((tm, HEAD_DIM), lambda i: (i % lb, 0))],
        out_specs=[pl.BlockSpec((tm, W), lambda i: (i, 0)),
                   pl.BlockSpec((1, HEAD_DIM), lambda i: (0, 0)),
                   pl.BlockSpec((1, HEAD_DIM), lambda i: (0, 0))],
        out_shape=[S((T, W), BF16), S((1, HEAD_DIM), F32), S((1, HEAD_DIM), F32)],
        compiler_params=_cp("arbitrary"),
    )(qkv, dq, dk, dv, qg, kg, cos, sin)


EXP2_SCALE = SOFTMAX_SCALE * math.log2(math.e)
ATTN_SUB = 256
ATTN_TQ = 2048


def _softmax_rows(q, k):
    s = _dot_nt(q, k)
    e = jnp.exp2((s - jnp.max(s, axis=-1, keepdims=True)) * EXP2_SCALE)
    return e, jnp.sum(e, axis=-1, keepdims=True)


def _attn_fwd(q, k, v, L, name):
    T = q.shape[0]
    nh = q.shape[1] // HEAD_DIM
    G = nh // N_KV
    B = T // L
    tq = min(L, ATTN_TQ)
    nq = L // tq
    sub = min(tq, ATTN_SUB)

    def body(q_ref, k_ref, v_ref, o_ref):
        for h in range(tq // sub):
            rows = slice(h * sub, (h + 1) * sub)
            e, l = _softmax_rows(q_ref[rows, :], k_ref[...])
            o_ref[rows, :] = (_dot(e.astype(BF16), v_ref[...]) / l).astype(BF16)

    qspec = pl.BlockSpec((tq, HEAD_DIM), lambda b, kv, g, qi: (b * nq + qi, kv * G + g))
    kspec = pl.BlockSpec((L, HEAD_DIM), lambda b, kv, g, qi: (b, kv))
    return pl.pallas_call(
        body, name=name, grid=(B, N_KV, G, nq),
        in_specs=[qspec, kspec, kspec],
        out_specs=qspec,
        out_shape=S((T, nh * HEAD_DIM), BF16),
        compiler_params=_cp("parallel", "parallel", "parallel", "parallel"),
    )(q, k, v)


def _attn_bwd(q, k, v, do, o, L, name):
    T = q.shape[0]
    nh = q.shape[1] // HEAD_DIM
    G = nh // N_KV
    B = T // L
    tq = min(L, ATTN_TQ)
    nq = L // tq
    sub = min(tq, ATTN_SUB)

    def body(q_ref, k_ref, v_ref, do_ref, o_ref, dq_ref, dk_ref, dv_ref, ds_scr, p_scr):
        first = (pl.program_id(2) == 0) & (pl.program_id(3) == 0)
        last = (pl.program_id(2) == G - 1) & (pl.program_id(3) == nq - 1)

        @pl.when(first)
        def _():
            dk_ref[...] = jnp.zeros_like(dk_ref)
            dv_ref[...] = jnp.zeros_like(dv_ref)

        for h in range(tq // sub):
            rows = slice(h * sub, (h + 1) * sub)
            dov = do_ref[rows, :]
            e, l = _softmax_rows(q_ref[rows, :], k_ref[...])
            p = e * (1.0 / l)
            dsum = jnp.sum(dov.astype(F32) * o_ref[rows, :].astype(F32), axis=-1, keepdims=True)
            ds_scr[rows, :] = (p * (_dot_nt(dov, v_ref[...]) - dsum)).astype(BF16)
            p_scr[rows, :] = p.astype(BF16)
        ds = ds_scr[...]
        dq_ref[...] = _dot(ds, k_ref[...]) * SOFTMAX_SCALE
        dk_ref[...] += _dot_tn(ds, q_ref[...])
        dv_ref[...] += _dot_tn(p_scr[...], do_ref[...])

        @pl.when(last)
        def _():
            dk_ref[...] = dk_ref[...] * SOFTMAX_SCALE

    qspec = pl.BlockSpec((tq, HEAD_DIM), lambda b, kv, g, qi: (b * nq + qi, kv * G + g))
    kspec = pl.BlockSpec((L, HEAD_DIM), lambda b, kv, g, qi: (b, kv))
    return pl.pallas_call(
        body, name=name, grid=(B, N_KV, G, nq),
        in_specs=[qspec, kspec, kspec, qspec, qspec],
        out_specs=[qspec, kspec, kspec],
        out_shape=[S((T, nh * HEAD_DIM), F32), S((T, N_KV * HEAD_DIM), F32), S((T, N_KV * HEAD_DIM), F32)],
        scratch_shapes=[pltpu.VMEM((tq, L), BF16), pltpu.VMEM((tq, L), BF16)],
        compiler_params=_cp("parallel", "parallel", "arbitrary", "arbitrary"),
    )(q, k, v, do, o)


PV_CONV_W = 0
PV_B_A = 8
PV_B_X = 16
PV_LAM = 24
PV_CONV_B = 32
PV_ROWS = 40


def _shift_rows(x, k):
    if k == 0:
        return x
    L = x.shape[0]
    n = N_SEG * abs(k)
    seg = lax.broadcasted_iota(jnp.int32, (n, x.shape[1]), 0) % N_SEG
    if k > 0:
        edge = jnp.where(seg == 0, 0.0, pltpu.roll(x[L - n:], 1, 0))
        return jnp.concatenate([edge, x[:L - n]], axis=0)
    edge = jnp.where(seg == N_SEG - 1, 0.0, pltpu.roll(x[:n], n - 1, 0))
    return jnp.concatenate([x[n:], edge], axis=0)


def _conv_taps(rec, pv):
    c = pv[PV_CONV_B:PV_CONV_B + 1]
    for j in range(CONV_W):
        c = c + pv[PV_CONV_W + j:PV_CONV_W + j + 1] * _shift_rows(rec, 2 - j)
    return c


def _sigmoid(x):
    return 0.5 * jnp.tanh(0.5 * x) + 0.5


EXPM1_SERIES_BELOW = 0.03


def _rg_gates(c, cbf, wa, wx, ba, bx, lam):
    r = _sigmoid(_dot(cbf, wa) + ba)
    i = _sigmoid(_dot(cbf, wx) + bx)
    sp = jnp.maximum(-lam, 0.0) + jnp.log1p(jnp.exp(-jnp.abs(lam)))
    la = r * ((-RG_C) * sp)
    a = jnp.exp(la)
    a2 = a * a
    x = la + la
    series = -(x * ((x * (1.0 / 6.0) + 0.5) * x + 1.0))
    om = jnp.where(x > -EXPM1_SERIES_BELOW, series, 1.0 - a2)
    rm = lax.rsqrt(om)
    return r, i, a, om * rm, rm, a2, sp


def _gelu(x):
    t = jnp.tanh(GELU_K * (x + GELU_C * x * x * x))
    return 0.5 * x * (1.0 + t), t


def _scan_pair(af_ref, uf_ref, ab_ref, ub_ref, hf_ref, hb_ref, pf_ref, pb_ref, L):
    ls = L // N_SEG
    zero = jnp.zeros((N_SEG, LRU_BW), F32)
    one = jnp.ones((N_SEG, LRU_BW), F32)
    tile = lambda t: pl.ds(pl.multiple_of(t * N_SEG, N_SEG), N_SEG)

    def steps(tc, carry):
        hf, pf, hb, pb = carry
        for q in range(SCAN_UNROLL):
            t = tc * SCAN_UNROLL + q
            rf, rb = tile(t), tile(ls - 1 - t)
            af = af_ref[rf, :]
            hf = af * hf + uf_ref[rf, :]
            pf = pf * af
            hf_ref[rf, :] = hf
            pf_ref[rf, :] = pf
            ab = ab_ref[rb, :]
            hb = ab * hb + ub_ref[rb, :]
            pb = pb * ab
            hb_ref[rb, :] = hb
            pb_ref[rb, :] = pb
        return hf, pf, hb, pb

    hf_e, pf_e, hb_e, pb_e = lax.fori_loop(0, ls // SCAN_UNROLL, steps, (zero, one, zero, one))

    rows, cin = [], jnp.zeros((1, LRU_BW), F32)
    for s in range(N_SEG):
        rows.append(cin)
        cin = hf_e[s:s + 1] + pf_e[s:s + 1] * cin
    cf = jnp.concatenate(rows, axis=0)
    rows, cin = [], jnp.zeros((1, LRU_BW), F32)
    for s in reversed(range(N_SEG)):
        rows.append(cin)
        cin = hb_e[s:s + 1] + pb_e[s:s + 1] * cin
    cb = jnp.concatenate(rows[::-1], axis=0)

    def fix(tc, _):
        for q in range(SCAN_UNROLL):
            r = tile(tc * SCAN_UNROLL + q)
            hf_ref[r, :] = hf_ref[r, :] + pf_ref[r, :] * cf
            hb_ref[r, :] = hb_ref[r, :] + pb_ref[r, :] * cb
        return 0

    lax.fori_loop(0, ls // SCAN_UNROLL, fix, 0)


def _rg_specs(L, D, nblk):
    slab = lambda off: pl.BlockSpec((L, LRU_BW), lambda cb, b: (b, off + cb))
    wspec = pl.BlockSpec((2, None, LRU_BW, LRU_BW), lambda cb, b: (0, cb, 0, 0))
    pvspec = pl.BlockSpec((PV_ROWS, LRU_BW), lambda cb, b: (0, cb))
    return slab, wspec, pvspec


def _rg_fwd(z, pvec, wa, wx, L, name):
    T, C2 = z.shape
    C = C2 // 2
    nblk = C // LRU_BW
    B = T // L
    slab, wspec, pvspec = _rg_specs(L, C, nblk)

    def body(gp_ref, rec_ref, pv_ref, wa_ref, wx_ref, yg_ref, hf_ref, hb_ref, a_scr, u_scr, p_scr):
        pv = pv_ref[...]
        c = _conv_taps(rec_ref[...], pv)
        cbf = c.astype(BF16)
        for d in range(2):
            _, i, a, m, _, _, _ = _rg_gates(c, cbf, wa_ref[d], wx_ref[d], pv[PV_B_A + d:PV_B_A + d + 1],
                                      pv[PV_B_X + d:PV_B_X + d + 1], pv[PV_LAM + d:PV_LAM + d + 1])
            a_scr[d] = a
            u_scr[d] = m * (i * c)
        _scan_pair(a_scr.at[0], u_scr.at[0], a_scr.at[1], u_scr.at[1], hf_ref, hb_ref, p_scr.at[0], p_scr.at[1], L)
        gate, _ = _gelu(gp_ref[...])
        yg_ref[...] = ((hf_ref[...] + hb_ref[...]) * gate).astype(BF16)

    return pl.pallas_call(
        body, name=name, grid=(nblk, B),
        in_specs=[slab(0), slab(nblk), pvspec, wspec, wspec],
        out_specs=[slab(0), slab(0), slab(0)],
        out_shape=[S((T, C), BF16), S((T, C), F32), S((T, C), F32)],
        scratch_shapes=[pltpu.VMEM((2, L, LRU_BW), F32)] * 3,
        compiler_params=_cp("parallel", "parallel"),
    )(z, z, pvec, wa, wx)


def _rg_bwd(z, hf, hb, dyg, pvec, wa, wx, L, name):
    T, C2 = z.shape
    C = C2 // 2
    nblk = C // LRU_BW
    B = T // L
    slab, wspec, pvspec = _rg_specs(L, C, nblk)

    def body(gp_ref, rec_ref, hf_ref, hb_ref, dyg_ref, pv_ref, wa_ref, wx_ref,
             dz_ref, dwa_ref, dwx_ref, dpv_ref, a_scr, u_scr, d_scr, p_scr):
        @pl.when(pl.program_id(1) == 0)
        def _():
            dwa_ref[...] = jnp.zeros_like(dwa_ref)
            dwx_ref[...] = jnp.zeros_like(dwx_ref)
            dpv_ref[...] = jnp.zeros_like(dpv_ref)

        pv = pv_ref[...]
        rec = rec_ref[...]
        c = _conv_taps(rec, pv)
        cbf = c.astype(BF16)
        gp = gp_ref[...]
        gate, th = _gelu(gp)
        dgelu = 0.5 * (1.0 + th) + 0.5 * gp * (1.0 - th * th) * GELU_K * (1.0 + 3.0 * GELU_C * gp * gp)
        dyg = dyg_ref[...]
        dz_ref[0] = (dyg * (hf_ref[...] + hb_ref[...]) * dgelu).astype(BF16)
        dy = dyg * gate

        gates = []
        for d in range(2):
            gates.append(_rg_gates(c, cbf, wa_ref[d], wx_ref[d], pv[PV_B_A + d:PV_B_A + d + 1],
                                   pv[PV_B_X + d:PV_B_X + d + 1], pv[PV_LAM + d:PV_LAM + d + 1]))
        a_scr[0] = _shift_rows(gates[1][2], 1)
        a_scr[1] = _shift_rows(gates[0][2], -1)
        u_scr[...] = dy
        _scan_pair(a_scr.at[0], u_scr, a_scr.at[1], u_scr, d_scr.at[1], d_scr.at[0], p_scr.at[0], p_scr.at[1], L)

        dc = jnp.zeros_like(c)
        rows = []
        for d in range(2):
            r, i, a, m, rm, a2, sp = gates[d]
            delta = d_scr[d]
            hnb = _shift_rows(hf_ref[...], 1) if d == 0 else _shift_rows(hb_ref[...], -1)
            da = delta * hnb
            dm = delta * (i * c)
            di = delta * (m * c)
            dc = dc + delta * (m * i)
            dla = da * a - dm * (a2 * rm)
            dpa = (dla * ((-RG_C) * sp)) * (r * (1.0 - r))
            dpx = di * (i * (1.0 - i))
            dsp = (-RG_C) * jnp.sum(dla * r, axis=0, keepdims=True)
            lam = pv[PV_LAM + d:PV_LAM + d + 1]
            rows.append((jnp.sum(dpa, axis=0, keepdims=True), jnp.sum(dpx, axis=0, keepdims=True),
                         -dsp * _sigmoid(-lam)))
            dpab = dpa.astype(BF16)
            dpxb = dpx.astype(BF16)
            dwa_ref[d] += _dot_tn(cbf, dpab)
            dwx_ref[d] += _dot_tn(cbf, dpxb)
            dc = dc + _dot_nt(dpab, wa_ref[d]) + _dot_nt(dpxb, wx_ref[d])

        drec = jnp.zeros_like(c)
        dcw = []
        for j in range(CONV_W):
            drec = drec + pv[PV_CONV_W + j:PV_CONV_W + j + 1] * _shift_rows(dc, j - 2)
            dcw.append(jnp.sum(dc * _shift_rows(rec, 2 - j), axis=0, keepdims=True))
        dz_ref[1] = drec.astype(BF16)
        for j in range(CONV_W):
            dpv_ref[PV_CONV_W + j:PV_CONV_W + j + 1, :] += dcw[j]
        for d in range(2):
            dpv_ref[PV_B_A + d:PV_B_A + d + 1, :] += rows[d][0]
            dpv_ref[PV_B_X + d:PV_B_X + d + 1, :] += rows[d][1]
            dpv_ref[PV_LAM + d:PV_LAM + d + 1, :] += rows[d][2]
        dpv_ref[PV_CONV_B:PV_CONV_B + 1, :] += jnp.sum(dc, axis=0, keepdims=True)

    return pl.pallas_call(
        body, name=name, grid=(nblk, B),
        in_specs=[slab(0), slab(nblk), slab(0), slab(0), slab(0), pvspec, wspec, wspec],
        out_specs=[pl.BlockSpec((2, L, LRU_BW), lambda cb, b: (0, b, cb)), wspec, wspec, pvspec],
        out_shape=[S((2, T, C), BF16), S((2, nblk, LRU_BW, LRU_BW), F32), S((2, nblk, LRU_BW, LRU_BW), F32),
                   S((PV_ROWS, C), F32)],
        scratch_shapes=[pltpu.VMEM((2, L, LRU_BW), F32), pltpu.VMEM((L, LRU_BW), F32),
                        pltpu.VMEM((2, L, LRU_BW), F32), pltpu.VMEM((2, L, LRU_BW), F32)],
        compiler_params=_cp("parallel", "arbitrary"),
    )(z, z, hf, hb, dyg, pvec, wa, wx)


QKV_NB = 512


def _interleave(a):
    *lead, L, D = a.shape
    return a.reshape(*lead, N_SEG, L // N_SEG, D).swapaxes(-3, -2).reshape(*lead, L, D)


def _deinterleave(a):
    *lead, L, D = a.shape
    return a.reshape(*lead, L // N_SEG, N_SEG, D).swapaxes(-3, -2).reshape(*lead, L, D)


def _local_step(x3, tgt3, w, fetch, send, early_adam):
    Bl, L, D = x3.shape
    T = Bl * L
    x = _interleave(x3).reshape(T, D)
    tgt = _interleave(tgt3).reshape(T, D)
    gm = [w["g_mix"][i:i + 1] for i in range(2)]
    gl = [w["g_mlp"][i:i + 1] for i in range(2)]

    w0 = fetch(0, ())
    nb_in = w0["w_in"].shape[-1]
    z, h0 = _norm_matmul(x, gm[0], w0["w_in"], "rg_in")
    yg, hf, hb = _rg_fwd(z, w0["pvec"], w["wa"], w["wx"], L, "rg_fwd")
    w1 = fetch(1, (yg,))
    x1 = _matmul_res(yg, w1["w_out"], x, "rg_out")
    w1.update(fetch(4, (x1,)))
    fb = w1["w_up0"].shape[-1]
    x2, a0, hm0 = _mlp_fwd(x1, gl[0], w1["w_up0"], w1["w_down0"], "mlp0_fwd")
    w2 = fetch(2, (x2,))
    cos, sin = [_interleave(t) for t in _rope_tables(L)]
    qkv, h1, qn, kn, vb = _qkv_proj(x2, gm[1], w2["w_qkv"], w["qg"], w["kg"], cos, sin, L, "at_qkv")
    o = _attn_fwd(qn, kn, vb, L, "at_fwd")
    x3_ = _matmul_res(o, w2["w_o"], x2, "at_out")
    w3 = fetch(3, (x3_,))
    dx4, a1, hm1, loss, dgf = _mlp_fwd(x3_, gl[1], w3["w_up1"], w3["w_down1"], "mlp1_fwd", head=(tgt, w["g_fin"]))

    dx3, da1, dob1, dgl1 = _mlp_bwd_dx(x3_, dx4, a1, gl[1], w3["w_up1"], w3["w_down1"], "mlp1_bwd_dx")
    dwu1, dwd1, _ = _mlp_bwd_dw(hm1, da1, a1, dob1, fb, "mlp1_bwd_dw")
    sent = send(3, dict(w_up1=dwu1, w_down1=dwd1))
    do, dx3b = _matmul_nt(dx3, w2["w_o"], "at_out_bwd", BF16, after=sent)
    dwo = _matmul_tn(o, dx3b[None], QKV_NB, "at_out_dw", blocked=False)
    dq, dk, dv = _attn_bwd(qn, kn, vb, do, o, L, "at_bwd")
    dqkv, dqg, dkg = _qk_prep_bwd(qkv, dq, dk, dv, w["qg"], w["kg"], cos, sin, L, "at_prep_bwd")
    dwqkv = _matmul_tn(dqkv, h1[None], QKV_NB, "at_qkv_dw", blocked=False)
    sent = send(2, dict(w_qkv=dwqkv, w_o=dwo))
    dx2, dgm1 = _nt_normbwd(dqkv[None], w2["w_qkv"], x2, gm[1], dx3, "at_qkv_bwd", after=sent, transposed=True)
    dx1, da0, dob0, dgl0 = _mlp_bwd_dx(x1, dx2, a0, gl[0], w1["w_up0"], w1["w_down0"], "mlp0_bwd_dx")
    jobs, sink = early_adam((dob0,))
    dwu0, dwd0, early = _mlp_bwd_dw(hm0, da0, a0, dob0, fb, "mlp0_bwd_dw", jobs)
    sink(early)
    sent = send(1, dict(w_up0=dwu0, w_down0=dwd0))
    dyg, dx1b = _matmul_nt(dx1, w1["w_out"], "rg_out_bwd", F32, after=sent)
    dwout = _matmul_tn(yg, dx1b[None], QKV_NB, "rg_out_dw", blocked=False)
    dz, dwa, dwx, dpv = _rg_bwd(z, hf, hb, dyg, w0["pvec"], w["wa"], w["wx"], L, "rg_bwd")
    sent = send(4, dict(w_out=dwout, pvec=dpv, wa=dwa, wx=dwx))
    dwin = _matmul_tn(h0, dz, nb_in, "rg_in_dw", blocked=True, after=sent)
    sent = send(0, dict(w_in=dwin))
    dx0, dgm0 = _nt_normbwd(dz, w0["w_in"], x, gm[0], dx1, "rg_in_bwd", after=sent)
    send(-1, dict(g_mix=[dgm0, dgm1], g_mlp=[dgl0, dgl1], g_fin=dgf, conv_b=dpv[PV_CONV_B:PV_CONV_B + 1], qg=dqg, kg=dkg, loss=loss))
    return _deinterleave(dx0.reshape(Bl, L, D))


MESH = pl.DeviceIdType.MESH
ANY = pl.BlockSpec(memory_space=pl.ANY)
N_PEERS = N_DEV - 1


def _my_place():
    return lax.axis_index("x"), lax.axis_index("y"), lax.axis_index("c")


def _flat(px, py, pc):
    return 4 * px + 2 * py + pc


def _all_gather(shards, name):
    n = len(shards)

    def body(*refs):
        ins, outs = refs[:n], refs[n:2 * n]
        send_sems, recv_sems, local_sems = refs[2 * n:]
        x, y, c = _my_place()
        me, sibling = (x, y, c), (x, y, 1 - c)
        chips = [(1 - x, y), (x, 1 - y), (1 - x, 1 - y)]

        def copy(a, k, block, to, src=None):
            dst = outs[a].at[_flat(*block)]
            return pltpu.make_async_remote_copy(
                src_ref=dst if src is None else src, dst_ref=dst,
                send_sem=send_sems.at[a, k], recv_sem=recv_sems.at[a, k],
                device_id=to, device_id_type=MESH)

        mine = [pltpu.make_async_copy(ins[a], outs[a].at[_flat(*me)], local_sems.at[a]) for a in range(n)]
        for cp in mine:
            cp.start()
        first = []
        for a in range(n):
            first.append(copy(a, 0, me, sibling, src=ins[a]))
            first += [copy(a, 1 + j, me, (*chip, c), src=ins[a]) for j, chip in enumerate(chips)]
        for cp in first:
            cp.start()
        passed = []
        for j, chip in enumerate(chips):
            for a in range(n):
                copy(a, 1 + j, (*chip, c), me).wait_recv()
                fwd = copy(a, 4 + j, (*chip, c), sibling)
                fwd.start()
                passed.append(fwd)
        for a in range(n):
            copy(a, 0, sibling, me).wait_recv()
            for j, chip in enumerate(chips):
                copy(a, 4 + j, (*chip, 1 - c), me).wait_recv()
        for cp in first + passed:
            cp.wait_send()
        for cp in mine:
            cp.wait()

    return pl.pallas_call(
        body, name=name,
        in_specs=[ANY] * n, out_specs=[ANY] * n,
        out_shape=[S((N_DEV,) + s.shape, s.dtype) for s in shards],
        scratch_shapes=[pltpu.SemaphoreType.DMA((n, N_PEERS)), pltpu.SemaphoreType.DMA((n, N_PEERS)),
                        pltpu.SemaphoreType.DMA((n,))],
    )(*shards)


HBM = pl.BlockSpec(memory_space=pltpu.HBM)
SEM = pl.BlockSpec(memory_space=pltpu.SEMAPHORE)
SIDE_EFFECT = pltpu.SideEffectType.DATAFLOW_SIDE_EFFECTING
SEMS_PER_GROUP = 3


NEAR_PEERS = (1, 2, 4, 6)
FAR_CHIPS = (2, 4, 6)


def _exchange_copies(srcs, lands, sems, mode):
    send_sems, recv_sems, local_sems = sems
    scatter = mode == "scatter"
    x, y, c = _my_place()
    me = _flat(x, y, c)
    remote, local = [], []
    for a in range(len(srcs)):
        for r in (NEAR_PEERS if mode == "near" else range(1, N_DEV)):
            peer = (1 - x if r & 4 else x, 1 - y if r & 2 else y, 1 - c if r & 1 else c)
            remote.append(pltpu.make_async_remote_copy(
                src_ref=srcs[a].at[_flat(*peer)] if scatter else srcs[a], dst_ref=lands[a].at[me],
                send_sem=send_sems.at[a * N_PEERS + r - 1], recv_sem=recv_sems.at[a * N_PEERS + r - 1],
                device_id=peer, device_id_type=MESH))
        local.append(pltpu.make_async_copy(srcs[a].at[me] if scatter else srcs[a], lands[a].at[me], local_sems.at[a]))
    return remote, local


def _exchange_start(groups, modes, name):
    sizes = [len(g) for g in groups]
    srcs = [pltpu.with_memory_space_constraint(a, pltpu.HBM) for g in groups for a in g]
    n = len(srcs)
    scatter_of = [m == "scatter" for g, m in zip(groups, modes) for _ in g]
    lands = [pltpu.with_memory_space_constraint(lax.empty(a.shape if sc else (N_DEV,) + a.shape, a.dtype), pltpu.HBM)
             for a, sc in zip(srcs, scatter_of)]
    n_sem = SEMS_PER_GROUP * len(groups)

    def body(*refs):
        src_refs, land_refs, sem_refs, token = refs[:n], refs[n:2 * n], refs[2 * n:2 * n + n_sem], refs[-1]
        off = 0
        for gi, k in enumerate(sizes):
            remote, local = _exchange_copies(src_refs[off:off + k], land_refs[off:off + k],
                                             sem_refs[SEMS_PER_GROUP * gi:SEMS_PER_GROUP * (gi + 1)], modes[gi])
            for cp in local + remote:
                cp.start()
            off += k
        token[...] = jnp.zeros_like(token)

    sem_shapes = []
    for k in sizes:
        sem_shapes += [pltpu.SemaphoreType.DMA((k * N_PEERS,)), pltpu.SemaphoreType.DMA((k * N_PEERS,)),
                       pltpu.SemaphoreType.DMA((k,))]
    outs = pl.pallas_call(
        body, name=name,
        out_shape=sem_shapes + [pltpu.HBM(a.shape, a.dtype) for a in srcs + lands] + [S((8, 128), F32)],
        in_specs=[HBM] * (2 * n),
        out_specs=[SEM] * n_sem + [HBM] * (2 * n) + [pl.BlockSpec(memory_space=pltpu.VMEM)],
        input_output_aliases={i: n_sem + i for i in range(2 * n)},
        compiler_params=pltpu.CompilerParams(has_side_effects=SIDE_EFFECT),
    )(*srcs, *lands)
    sems, thru, token = outs[:n_sem], outs[n_sem:n_sem + 2 * n], outs[-1]
    per_group, off = [], 0
    for gi, k in enumerate(sizes):
        per_group.append((sems[SEMS_PER_GROUP * gi:SEMS_PER_GROUP * (gi + 1)], thru[off:off + k], thru[n + off:n + off + k]))
        off += k
    return per_group, token


def _exchange_wait(group, after, mode, name):
    sems, srcs, lands = group
    k = len(srcs)

    def body(*refs):
        remote, local = _exchange_copies(refs[:k], refs[k:2 * k], refs[2 * k:2 * k + SEMS_PER_GROUP], mode)
        for cp in remote:
            cp.wait_send()
            cp.wait_recv()
        for cp in local:
            cp.wait()

    outs = pl.pallas_call(
        body, name=name,
        out_shape=[pltpu.HBM(a.shape, a.dtype) for a in list(srcs) + list(lands)],
        in_specs=[HBM] * (2 * k) + [SEM] * SEMS_PER_GROUP + [ANY] * len(after),
        out_specs=[HBM] * (2 * k),
        input_output_aliases={i: i for i in range(2 * k)},
        compiler_params=pltpu.CompilerParams(has_side_effects=SIDE_EFFECT),
    )(*srcs, *lands, *sems, *after)
    return outs[k:]


def _forward_copies(lands, sems):
    send_sems, recv_sems = sems
    x, y, c = _my_place()
    mine, theirs = [], []
    for a in range(len(lands)):
        for k, r in enumerate(FAR_CHIPS):
            px, py = (1 - x if r & 4 else x), (1 - y if r & 2 else y)
            for out, core in ((mine, c), (theirs, 1 - c)):
                blk = lands[a].at[_flat(px, py, core)]
                out.append(pltpu.make_async_remote_copy(
                    src_ref=blk, dst_ref=blk, send_sem=send_sems.at[a * len(FAR_CHIPS) + k],
                    recv_sem=recv_sems.at[a * len(FAR_CHIPS) + k], device_id=(x, y, 1 - c), device_id_type=MESH))
    return mine, theirs


def _forward_start(groups, name):
    sizes = [len(g) for g in groups]
    lands = [a for g in groups for a in g]
    n = len(lands)
    n_sem = 2 * len(groups)

    def body(*refs):
        land_refs, sem_refs, token = refs[:n], refs[n:n + n_sem], refs[-1]
        off = 0
        for gi, k in enumerate(sizes):
            mine, _ = _forward_copies(land_refs[off:off + k], sem_refs[2 * gi:2 * gi + 2])
            for cp in mine:
                cp.start()
            off += k
        token[...] = jnp.zeros_like(token)

    sem_shapes = []
    for k in sizes:
        sem_shapes += [pltpu.SemaphoreType.DMA((k * len(FAR_CHIPS),))] * 2
    outs = pl.pallas_call(
        body, name=name,
        out_shape=sem_shapes + [pltpu.HBM(a.shape, a.dtype) for a in lands] + [S((8, 128), F32)],
        in_specs=[HBM] * n,
        out_specs=[SEM] * n_sem + [HBM] * n + [pl.BlockSpec(memory_space=pltpu.VMEM)],
        input_output_aliases={i: n_sem + i for i in range(n)},
        compiler_params=pltpu.CompilerParams(has_side_effects=SIDE_EFFECT),
    )(*lands)
    per_group, off = [], 0
    for gi, k in enumerate(sizes):
        per_group.append((outs[2 * gi:2 * gi + 2], outs[n_sem + off:n_sem + off + k]))
        off += k
    return per_group


def _forward_wait(group, after, name):
    sems, lands = group
    k = len(lands)

    def body(*refs):
        mine, theirs = _forward_copies(refs[:k], refs[k:k + 2])
        for cp in mine:
            cp.wait_send()
        for cp in theirs:
            cp.wait_recv()

    return pl.pallas_call(
        body, name=name,
        out_shape=[pltpu.HBM(a.shape, a.dtype) for a in lands],
        in_specs=[HBM] * k + [SEM] * 2 + [ANY] * len(after),
        out_specs=[HBM] * k,
        input_output_aliases={i: i for i in range(k)},
        compiler_params=pltpu.CompilerParams(has_side_effects=SIDE_EFFECT),
    )(*lands, *sems, *after)


def _row_tile(rows, cols):
    want = max(16, (128 * 1024) // cols)
    if rows <= want:
        return rows
    t = want - want % 16
    while rows % t:
        t -= 16
    return t


def _sum_parts(parts, name, after=()):
    P, R, C = parts.shape
    tr = _row_tile(R, C)

    def body(p_ref, *rest):
        o_ref = rest[-1]
        g = p_ref[0].astype(F32)
        for i in range(1, P):
            g = g + p_ref[i].astype(F32)
        o_ref[...] = g

    return pl.pallas_call(
        body, name=name, grid=(R // tr,),
        in_specs=[pl.BlockSpec((P, tr, C), lambda i: (0, i, 0))] + [ANY] * len(after),
        out_specs=pl.BlockSpec((tr, C), lambda i: (i, 0)),
        out_shape=S((R, C), F32),
        compiler_params=_cp("parallel"),
    )(parts, *after)


def _adamw(parts, w, m, v, name, after=()):
    P, R, C = parts.shape
    tr = _row_tile(R, C)
    c1 = 1.0 - ADAM_B1 ** ADAM_STEP
    c2 = 1.0 - ADAM_B2 ** ADAM_STEP

    def body(p_ref, w_ref, m_ref, v_ref, *rest):
        g_ref, d_ref, mo_ref, vo_ref = rest[len(after):]
        g = p_ref[0].astype(F32)
        for i in range(1, P):
            g = g + p_ref[i].astype(F32)
        mn = ADAM_B1 * m_ref[...] + (1.0 - ADAM_B1) * g
        vn = ADAM_B2 * v_ref[...] + (1.0 - ADAM_B2) * (g * g)
        g_ref[...] = g
        mo_ref[...] = mn
        vo_ref[...] = vn
        d_ref[...] = (-ADAM_LR) * ((mn / c1) / (jnp.sqrt(vn / c2) + ADAM_EPS) + ADAM_WD * w_ref[...])

    blk = pl.BlockSpec((tr, C), lambda i: (i, 0))
    return pl.pallas_call(
        body, name=name, grid=(R // tr,),
        in_specs=[pl.BlockSpec((P, tr, C), lambda i: (0, i, 0)), blk, blk, blk] + [ANY] * len(after),
        out_specs=[blk, blk, blk, blk],
        out_shape=[S((R, C), F32)] * 4,
        compiler_params=_cp("parallel"),
    )(parts, w, m, v, *after)


def _adamw_layer(parts, w3, m3, v3, layer, prev, name, after=()):
    P, R, C = parts.shape
    NL = w3.shape[0]
    tr = _row_tile(R, C)
    c1 = 1.0 - ADAM_B1 ** ADAM_STEP
    c2 = 1.0 - ADAM_B2 ** ADAM_STEP
    n_prev = 0 if prev is None else len(prev)

    def body(p_ref, w_ref, m_ref, v_ref, *rest):
        g_ref, d_ref, mo_ref, vo_ref = rest[n_prev + len(after):]
        g = p_ref[0].astype(F32)
        for i in range(1, P):
            g = g + p_ref[i].astype(F32)
        mn = ADAM_B1 * m_ref[...] + (1.0 - ADAM_B1) * g
        vn = ADAM_B2 * v_ref[...] + (1.0 - ADAM_B2) * (g * g)
        g_ref[...] = g
        mo_ref[...] = mn
        vo_ref[...] = vn
        d_ref[...] = (-ADAM_LR) * ((mn / c1) / (jnp.sqrt(vn / c2) + ADAM_EPS) + ADAM_WD * w_ref[...])

    blk = pl.BlockSpec((None, tr, C), lambda i: (layer, i, 0))
    return pl.pallas_call(
        body, name=name, grid=(R // tr,),
        in_specs=[pl.BlockSpec((P, tr, C), lambda i: (0, i, 0)), blk, blk, blk] + [ANY] * (n_prev + len(after)),
        out_specs=[blk, blk, blk, blk],
        out_shape=[S((NL, R, C), F32)] * 4,
        input_output_aliases={4 + k: k for k in range(n_prev)},
        compiler_params=_cp("parallel"),
    )(parts, w3, m3, v3, *(prev or ()), *after)


VMEM_WHOLE = pl.BlockSpec(memory_space=pltpu.VMEM)
LANES = 128


def _pack_vectors(vectors, starts, rows, name):
    def body(*refs):
        o_ref = refs[-1]
        o_ref[...] = jnp.zeros_like(o_ref)
        for v_ref, r0 in zip(refs[:-1], starts):
            for j in range(v_ref.shape[1] // LANES):
                o_ref[r0 + j:r0 + j + 1, :] = v_ref[:, j * LANES:(j + 1) * LANES]

    return pl.pallas_call(body, name=name, in_specs=[VMEM_WHOLE] * len(vectors), out_specs=VMEM_WHOLE,
                          out_shape=S((rows, LANES), F32))(*vectors)


def _adamw_vectors(g_pack, params, keep_rows, name, after=()):
    n = len(params)
    P = g_pack.shape[0]
    c1 = 1.0 - ADAM_B1 ** ADAM_STEP
    c2 = 1.0 - ADAM_B2 ** ADAM_STEP

    def body(g_ref, *refs):
        ins, outs = refs[:3 * n], refs[3 * n + len(after):]
        gs = g_ref[0]
        for i in range(1, P):
            gs = gs + g_ref[i]
        for pi, (_, _, _, slots) in enumerate(params):
            w_ref, m_ref, v_ref = ins[3 * pi:3 * pi + 3]
            g_out, d_out, m_out, v_out = outs[4 * pi:4 * pi + 4]
            for idx, row in slots:
                g = gs[row:row + 1, :]
                mn = ADAM_B1 * m_ref[idx] + (1.0 - ADAM_B1) * g
                vn = ADAM_B2 * v_ref[idx] + (1.0 - ADAM_B2) * (g * g)
                g_out[idx] = g
                m_out[idx] = mn
                v_out[idx] = vn
                d_out[idx] = (-ADAM_LR) * ((mn / c1) / (jnp.sqrt(vn / c2) + ADAM_EPS) + ADAM_WD * w_ref[idx])
        outs[-1][...] = jnp.concatenate([gs[r:r + 1, :] for r in keep_rows], axis=0)

    flat = [a for w, m, v, _ in params for a in (w, m, v)]
    out_shape = [S(w.shape, F32) for w, _, _, _ in params for _ in range(4)] + [S((len(keep_rows), LANES), F32)]
    outs = pl.pallas_call(
        body, name=name,
        in_specs=[VMEM_WHOLE] * (1 + len(flat)) + [ANY] * len(after),
        out_specs=[VMEM_WHOLE] * len(out_shape), out_shape=out_shape,
    )(g_pack, *flat, *after)
    return [outs[4 * i:4 * i + 4] for i in range(n)], outs[-1]


def _adamw_nd(parts, w, m, v, name, after=()):
    shp = w.shape
    C = shp[-1]
    outs = _adamw(parts.reshape(parts.shape[0], -1, C), w.reshape(-1, C), m.reshape(-1, C), v.reshape(-1, C), name, after)
    return [o.reshape(shp) for o in outs]


TILE_ROWS = 8


REP_SMALL_ROWS = 128
REP_GRAD_STARTS = (0, 8, 16, 24, 32, 40, 48, 56, 64)
REP_SMALL_STARTS = (0, 16, 32, 40, 48, 56)
REP_LOSS_ROW = 64


def _small_pack(cw, ba, bx, lam):
    pad8 = lambda a: jnp.pad(a, ((0, TILE_ROWS - a.shape[0]), (0, 0)))
    return jnp.concatenate([pad8(cw[0, :, 0, :]), pad8(ba[0]), pad8(bx[0]), pad8(lam[0]),
                            jnp.zeros((PV_ROWS - PV_CONV_B, LRU_BW), F32)], axis=0)


def kernel(x, norm_mix_g, norm_mlp_g, rg_w_in, rg_conv_w, rg_conv_b, rg_w_a, rg_b_a, rg_w_x, rg_b_x, rg_lam, rg_w_out, at_w_qkv, at_q_g, at_k_g, at_w_o, mlp_w_up, mlp_w_down, final_g, loss_target, m_norm_mix_g, m_norm_mlp_g, m_rg_w_in, m_rg_conv_w, m_rg_conv_b, m_rg_w_a, m_rg_b_a, m_rg_w_x, m_rg_b_x, m_rg_lam, m_rg_w_out, m_at_w_qkv, m_at_q_g, m_at_k_g, m_at_w_o, m_mlp_w_up, m_mlp_w_down, m_final_g, v_norm_mix_g, v_norm_mlp_g, v_rg_w_in, v_rg_conv_w, v_rg_conv_b, v_rg_w_a, v_rg_b_a, v_rg_w_x, v_rg_b_x, v_rg_lam, v_rg_w_out, v_at_w_qkv, v_at_q_g, v_at_k_g, v_at_w_o, v_mlp_w_up, v_mlp_w_down, v_final_g):
    D = x.shape[-1]
    bf = lambda a: a.astype(BF16)

    sp_w = _small_pack(rg_conv_w, rg_b_a, rg_b_x, rg_lam)
    started, _ = _exchange_start(
        [[bf(rg_w_in[0]), sp_w], [bf(rg_w_out[0])], [bf(mlp_w_up[0]), bf(mlp_w_down[0])],
         [bf(at_w_qkv[0]).T, bf(at_w_o[0])], [bf(mlp_w_up[1]), bf(mlp_w_down[1])]],
        ["near", "near", "near", "gather", "gather"], "gather_start")
    gathers = dict(zip((0, 1, 4, 2, 3), started))
    forwards = {}

    def fetch(stage, after):
        after = tuple(after)
        if stage == 0:
            got = _exchange_wait(gathers[0], after, "near", "gather_wait0")
            g_in, g_sp = _forward_wait(_forward_start([got], "forward_start0")[0], (), "forward_wait0")
            pvec = g_sp.transpose(1, 0, 2).reshape(PV_ROWS, D)
            pvec = jnp.concatenate([pvec[:PV_CONV_B], jnp.broadcast_to(rg_conv_b, (PV_ROWS - PV_CONV_B, D))], axis=0)
            return dict(w_in=g_in, pvec=pvec)
        if stage == 1:
            near = [_exchange_wait(gathers[s], after, "near", "gather_wait%d" % s) for s in (1, 4)]
            f_out, forwards[4] = _forward_start(near, "forward_start1")
            g_out, = _forward_wait(f_out, (), "forward_wait1")
            return dict(w_out=g_out.reshape(D, D))
        if stage == 4:
            g_up0, g_dn0 = _forward_wait(forwards[4], after, "forward_wait4")
            return dict(w_up0=g_up0, w_down0=g_dn0.reshape(-1, D))
        got = _exchange_wait(gathers[stage], after, "gather", "gather_wait%d" % stage)
        if stage == 2:
            return dict(w_qkv=got[0].reshape(-1, QKV_NB, D), w_o=got[1].reshape(D, D))
        return dict(w_up1=got[0], w_down1=got[1].reshape(-1, D))

    scatters = {}

    def send(stage, g):
        if stage == 3:
            arrs = [g["w_up1"], g["w_down1"].reshape(N_DEV, -1, D)]
        elif stage == 2:
            arrs = [g["w_qkv"].reshape(N_DEV, -1, D), g["w_o"].reshape(N_DEV, -1, D)]
        elif stage == 1:
            arrs = [g["w_up0"], g["w_down0"].reshape(N_DEV, -1, D)]
        elif stage == 4:
            arrs = [g["w_out"].reshape(N_DEV, -1, D), g["pvec"].reshape(PV_ROWS, N_DEV, LRU_BW).transpose(1, 0, 2),
                    bf(g["wa"]).reshape(N_DEV, -1, LANES), bf(g["wx"]).reshape(N_DEV, -1, LANES)]
        elif stage == 0:
            arrs = [g["w_in"]]
        else:
            small = _pack_vectors(g["g_mix"] + g["g_mlp"] + [g["g_fin"], g["conv_b"], g["qg"], g["kg"], g["loss"]],
                                  REP_GRAD_STARTS, REP_SMALL_ROWS, "pack_rep_small")
            arrs = [small.reshape(N_DEV, -1, LANES)]
        (group,), token = _exchange_start([arrs], ["scatter"], "scatter_start%d" % (stage % 6))
        scatters[stage] = (group, token)
        return (token,)

    w = dict(g_mix=norm_mix_g, g_mlp=norm_mlp_g, g_fin=final_g[None], qg=at_q_g, kg=at_k_g,
             wa=bf(rg_w_a[0]), wx=bf(rg_w_x[0]))
    early = {}

    def early_adam(after):
        r_up1, r_dn1 = _exchange_wait(scatters[3][0], tuple(after), "scatter", "scatter_wait3")
        r_qkv, r_o = _exchange_wait(scatters[2][0], (r_up1,), "scatter", "scatter_wait2")
        jobs = [(r_up1, mlp_w_up, m_mlp_w_up, v_mlp_w_up, 1), (r_dn1, mlp_w_down, m_mlp_w_down, v_mlp_w_down, 1),
                (r_o, at_w_o, m_at_w_o, v_at_w_o, 0)]
        return jobs, lambda outs: early.update(up=outs[0], dn=outs[1], o=outs[2], r_qkv=r_qkv)

    grad_x = _local_step(x, loss_target, w, fetch, send, early_adam)

    res = {}
    up, dn = early["up"], early["dn"]
    r_out, r_sp, r_wa, r_wx = _exchange_wait(scatters[4][0], (scatters[-1][1],), "scatter", "scatter_wait4")
    wa_part = _sum_parts(r_wa, "reduce_w_a")
    wx_part = _sum_parts(r_wx, "reduce_w_x", (wa_part,))
    (rep_gather,), rep_token = _exchange_start([[wa_part, wx_part]], ["gather"], "rep_gather_start")
    tr = lambda a: a[0].T
    qkv_t = _adamw_nd(early["r_qkv"], tr(at_w_qkv), tr(m_at_w_qkv), tr(v_at_w_qkv), "adam_at_w_qkv", (rep_token,))
    res["at_w_qkv"] = [o.T[None] for o in qkv_t]
    res["at_w_o"] = early["o"]
    r_up0, r_dn0 = _exchange_wait(scatters[1][0], (qkv_t[1],), "scatter", "scatter_wait1")
    res["mlp_w_up"] = _adamw_layer(r_up0, mlp_w_up, m_mlp_w_up, v_mlp_w_up, 0, up, "adam_mlp_w_up0")
    res["mlp_w_down"] = _adamw_layer(r_dn0, mlp_w_down, m_mlp_w_down, v_mlp_w_down, 0, dn, "adam_mlp_w_down0",
                                     after=(res["mlp_w_up"][1],))
    r_in, = _exchange_wait(scatters[0][0], (res["mlp_w_down"][1],), "scatter", "scatter_wait0")
    res["rg_w_in"] = _adamw_nd(r_in[:, None], rg_w_in, m_rg_w_in, v_rg_w_in, "adam_rg_w_in")
    res["rg_w_out"] = _adamw_nd(r_out[:, None], rg_w_out, m_rg_w_out, v_rg_w_out, "adam_rg_w_out", (res["rg_w_in"][1],))
    whole, lane = slice(None), slice(0, 1)
    two_rows = lambda r0: [((0, slice(d, d + 1), whole), r0 + d) for d in range(2)]
    (res["rg_conv_w"], res["rg_b_a"], res["rg_b_x"], res["rg_lam"]), _ = _adamw_vectors(
        r_sp, [(rg_conv_w, m_rg_conv_w, v_rg_conv_w, [((0, t, lane, whole), PV_CONV_W + t) for t in range(CONV_W)]),
               (rg_b_a, m_rg_b_a, v_rg_b_a, two_rows(PV_B_A)), (rg_b_x, m_rg_b_x, v_rg_b_x, two_rows(PV_B_X)),
               (rg_lam, m_rg_lam, v_rg_lam, two_rows(PV_LAM))], [0], "adam_small", (res["rg_w_out"][1],))

    r_small, = _exchange_wait(scatters[-1][0], (res["rg_lam"][1],), "scatter", "scatter_wait5")
    small_sum, = _all_gather([_sum_parts(r_small, "reduce_rep_small")], "gather_replicated")
    wa_sum, wx_sum = _exchange_wait(rep_gather, (small_sum,), "gather", "rep_gather_wait")
    rows = lambda a: a.reshape(-1, LANES)
    wa_res = _adamw(wa_sum.reshape(1, -1, LANES), rows(rg_w_a), rows(m_rg_w_a), rows(v_rg_w_a), "adam_rg_w_a")
    wx_res = _adamw(wx_sum.reshape(1, -1, LANES), rows(rg_w_x), rows(m_rg_w_x), rows(v_rg_w_x), "adam_rg_w_x", (wa_res[1],))
    res["rg_w_a"] = [o.reshape(rg_w_a.shape) for o in wa_res]
    res["rg_w_x"] = [o.reshape(rg_w_x.shape) for o in wx_res]

    def vec_slots(a, r0):
        per = a.shape[1] // LANES
        return [((slice(l, l + 1), slice(LANES * j, LANES * (j + 1))), r0 + l * per + j)
                for l in range(a.shape[0]) for j in range(per)]

    fin = [final_g[None], m_final_g[None], v_final_g[None]]
    vecs = [(norm_mix_g, m_norm_mix_g, v_norm_mix_g), (norm_mlp_g, m_norm_mlp_g, v_norm_mlp_g), fin,
            (rg_conv_b, m_rg_conv_b, v_rg_conv_b), (at_q_g, m_at_q_g, v_at_q_g), (at_k_g, m_at_k_g, v_at_k_g)]
    outs, kept = _adamw_vectors(
        small_sum.reshape(1, -1, LANES),
        [(wv, mv, vv, vec_slots(wv, r0)) for (wv, mv, vv), r0 in zip(vecs, REP_SMALL_STARTS)], [REP_LOSS_ROW],
        "adam_rep_small", (wx_res[1],))
    for nm, o in zip(["norm_mix_g", "norm_mlp_g", "final_g", "rg_conv_b", "at_q_g", "at_k_g"], outs):
        res[nm] = [a[0] for a in o] if nm == "final_g" else o
    loss = kept[0, 0]

    order = ["norm_mix_g", "norm_mlp_g", "rg_w_in", "rg_conv_w", "rg_conv_b", "rg_w_a", "rg_b_a", "rg_w_x", "rg_b_x",
             "rg_lam", "rg_w_out", "at_w_qkv", "at_q_g", "at_k_g", "at_w_o", "mlp_w_up", "mlp_w_down", "final_g"]
    return (loss, grad_x, *[res[nm][k] for k in range(4) for nm in order])
```

```python
import math

import jax
import jax.numpy as jnp
from jax import lax
from jax.experimental import pallas as pl
from jax.experimental.pallas import tpu as pltpu

F32 = jnp.float32
BF16 = jnp.bfloat16
S = jax.ShapeDtypeStruct

EPS = 1e-6
HEAD_DIM = 128
N_KV = 2
GRID_W = 64
ROPE_THETA = 10000.0
LRU_BW = 128
RG_C = 8.0
CONV_W = 4
N_DEV = 8
N_SEG = 8
SCAN_UNROLL = 8
TN_STEP_COLS = 512
PROJ_TM = 1024
STREAMED = pl.Buffered(2)
RES_TM = 512
RES_BUFFERS = 3
PROJ_CHAINS = 2
MLP_TM = 512
VMEM_LIMIT_V7X = 56 * 1024 * 1024
SOFTMAX_SCALE = 1.0 / math.sqrt(HEAD_DIM)
GELU_K = math.sqrt(2.0 / math.pi)
GELU_C = 0.044715

ADAM_LR = 0.001
ADAM_B1 = 0.9
ADAM_B2 = 0.999
ADAM_EPS = 1e-08
ADAM_WD = 0.01
ADAM_STEP = 10

NT = (((1,), (1,)), ((), ()))
TN = (((0,), (0,)), ((), ()))


def _cp(*sem):
    return pltpu.CompilerParams(dimension_semantics=sem, vmem_limit_bytes=VMEM_LIMIT_V7X)


def _rms_r(xv):
    return lax.rsqrt(jnp.mean(xv * xv, axis=-1, keepdims=True) + EPS)


def _rms_bwd(dh, xv, g):
    r = _rms_r(xv)
    xh = xv * r
    dg = jnp.sum(dh * xh, axis=0, keepdims=True)
    dxh = dh * g
    dx = r * (dxh - xh * jnp.mean(dxh * xh, axis=-1, keepdims=True))
    return dx, dg


def _dot(a, b):
    return jnp.dot(a, b, preferred_element_type=F32)


def _dot_nt(a, b):
    return lax.dot_general(a, b, NT, preferred_element_type=F32)


def _dot_tn(a, b):
    return lax.dot_general(a, b, TN, preferred_element_type=F32)


def _norm_matmul(x, g, wblk, name, out_dtype=F32):
    T, D = x.shape
    NB, _, nb = wblk.shape
    tm = min(T, PROJ_TM)

    def body(x_ref, g_ref, w_ref, o_ref, h_ref):
        for c in range(PROJ_CHAINS):
            rows = slice(c * tm // PROJ_CHAINS, (c + 1) * tm // PROJ_CHAINS)
            xv = x_ref[rows, :]
            hb = (xv * _rms_r(xv) * g_ref[...]).astype(BF16)
            h_ref[rows, :] = hb
            for q in range(NB):
                o_ref[rows, q * nb:(q + 1) * nb] = _dot(hb, w_ref[q]).astype(o_ref.dtype)

    return pl.pallas_call(
        body, name=name, grid=(T // tm,),
        in_specs=[pl.BlockSpec((tm, D), lambda i: (i, 0), pipeline_mode=STREAMED),
                  pl.BlockSpec((1, D), lambda i: (0, 0)),
                  pl.BlockSpec(wblk.shape, lambda i: (0, 0, 0))],
        out_specs=[pl.BlockSpec((tm, NB * nb), lambda i: (i, 0)),
                   pl.BlockSpec((tm, D), lambda i: (i, 0))],
        out_shape=[S((T, NB * nb), out_dtype), S((T, D), BF16)],
        compiler_params=_cp("parallel"),
    )(x, g, wblk)


def _matmul_res(a, w, res, name):
    T, K = a.shape
    N = w.shape[1]
    tm = min(T, RES_TM)
    n = T // tm
    nbuf = min(RES_BUFFERS, n)

    def body(a_hbm, w_ref, r_hbm, o_ref, a_buf, r_buf, sem):
        i = pl.program_id(0)

        def reads(step, slot):
            rows = pl.ds(pl.multiple_of(step * tm, tm), tm)
            return (pltpu.make_async_copy(a_hbm.at[rows, :], a_buf.at[slot], sem.at[0, slot]),
                    pltpu.make_async_copy(r_hbm.at[rows, :], r_buf.at[slot], sem.at[1, slot]))

        @pl.when(i == 0)
        def _():
            for s in range(nbuf):
                for cp in reads(s, s):
                    cp.start()

        slot = i % nbuf
        for cp in reads(i, slot):
            cp.wait()
        o_ref[...] = r_buf[slot] + _dot(a_buf[slot], w_ref[...])

        @pl.when(i + nbuf < n)
        def _():
            for cp in reads(i + nbuf, slot):
                cp.start()

    return pl.pallas_call(
        body, name=name, grid=(n,),
        in_specs=[ANY, pl.BlockSpec((K, N), lambda i: (0, 0)), ANY],
        out_specs=pl.BlockSpec((tm, N), lambda i: (i, 0)),
        out_shape=S((T, N), F32),
        scratch_shapes=[pltpu.VMEM((nbuf, tm, K), a.dtype), pltpu.VMEM((nbuf, tm, N), F32),
                        pltpu.SemaphoreType.DMA((2, nbuf))],
        compiler_params=_cp("arbitrary"),
    )(a, w, res)


def _matmul_nt(a, w, name, out_dtype, after=()):
    T, N = a.shape
    K = w.shape[0]
    tm = min(T, PROJ_TM)

    def body(a_ref, w_ref, *rest):
        o_ref, ab_ref = rest[len(after):]
        ab = a_ref[...].astype(BF16)
        ab_ref[...] = ab
        o_ref[...] = _dot_nt(ab, w_ref[...]).astype(o_ref.dtype)

    return pl.pallas_call(
        body, name=name, grid=(T // tm,),
        in_specs=[pl.BlockSpec((tm, N), lambda i: (i, 0), pipeline_mode=STREAMED),
                  pl.BlockSpec((K, N), lambda i: (0, 0))] + [pl.BlockSpec(memory_space=pl.ANY)] * len(after),
        out_specs=[pl.BlockSpec((tm, K), lambda i: (i, 0)),
                   pl.BlockSpec((tm, N), lambda i: (i, 0))],
        out_shape=[S((T, K), out_dtype), S((T, N), BF16)],
        compiler_params=_cp("parallel"),
    )(a, w, *after)


def _matmul_tn(a, b3, nb, name, blocked, after=()):
    T, M = a.shape
    SB, _, N = b3.shape
    per = N // nb
    NB = SB * per
    tk = min(T, 1024)
    nk = T // tk
    jb = max(1, TN_STEP_COLS // nb) if blocked else 1
    assert per % jb == 0
    if blocked:
        out_spec, out_shape = pl.BlockSpec((jb, M, nb), lambda j: (j, 0, 0)), S((NB, M, nb), BF16)
    else:
        assert SB == 1
        out_spec, out_shape = pl.BlockSpec((M, nb), lambda j: (0, j)), S((M, N), BF16)

    def body(a_ref, b_ref, *rest):
        o_ref = rest[-1]
        for q in range(jb):
            acc = None
            for k in range(nk):
                rows = slice(k * tk, (k + 1) * tk)
                part = _dot_tn(a_ref[rows, :], b_ref[rows, q * nb:(q + 1) * nb])
                acc = part if acc is None else acc + part
            if blocked:
                o_ref[q] = acc.astype(BF16)
            else:
                o_ref[...] = acc.astype(BF16)

    return pl.pallas_call(
        body, name=name, grid=(NB // jb,),
        in_specs=[pl.BlockSpec((T, M), lambda j: (0, 0), pipeline_mode=pl.Buffered(1)),
                  pl.BlockSpec((None, T, jb * nb), lambda j: ((j * jb) // per, 0, ((j * jb) % per) // jb))]
        + [pl.BlockSpec(memory_space=pl.ANY)] * len(after),
        out_specs=out_spec,
        out_shape=out_shape,
        compiler_params=_cp("parallel"),
    )(a, b3, *after)


def _nt_normbwd(dz3, wblk, x, g, dres, name, after=(), transposed=False):
    T, D = x.shape
    NB, nb = wblk.shape[0], wblk.shape[1 if transposed else 2]
    mm = _dot if transposed else _dot_nt
    SB, _, N = dz3.shape
    per = N // nb
    tm = min(T, PROJ_TM)

    def body(dz_ref, w_ref, x_ref, g_ref, dr_ref, *rest):
        dx_ref, dg_ref = rest[len(after):]

        @pl.when(pl.program_id(0) == 0)
        def _():
            dg_ref[...] = jnp.zeros_like(dg_ref)

        for c in range(PROJ_CHAINS):
            rows = slice(c * tm // PROJ_CHAINS, (c + 1) * tm // PROJ_CHAINS)
            dh = None
            for q in range(NB):
                cols = slice((q % per) * nb, (q % per + 1) * nb)
                part = mm(dz_ref[q // per, rows, cols], w_ref[q])
                dh = part if dh is None else dh + part
            dx, dg = _rms_bwd(dh, x_ref[rows, :], g_ref[...])
            dx_ref[rows, :] = dr_ref[rows, :] + dx
            dg_ref[...] += dg

    return pl.pallas_call(
        body, name=name, grid=(T // tm,),
        in_specs=[pl.BlockSpec((SB, tm, N), lambda i: (0, i, 0), pipeline_mode=STREAMED),
                  pl.BlockSpec(wblk.shape, lambda i: (0, 0, 0)),
                  pl.BlockSpec((tm, D), lambda i: (i, 0), pipeline_mode=STREAMED),
                  pl.BlockSpec((1, D), lambda i: (0, 0)),
                  pl.BlockSpec((tm, D), lambda i: (i, 0), pipeline_mode=STREAMED)] + [pl.BlockSpec(memory_space=pl.ANY)] * len(after),
        out_specs=[pl.BlockSpec((tm, D), lambda i: (i, 0)),
                   pl.BlockSpec((1, D), lambda i: (0, 0))],
        out_shape=[S((T, D), F32), S((1, D), F32)],
        compiler_params=_cp("arbitrary"),
    )(dz3, wblk, x, g, dres, *after)


def _loss_head(xv, tv, gv, D):
    err = xv * _rms_r(xv) * gv - tv
    e2 = jnp.sum(jnp.sum(err * err, axis=-1, keepdims=True), axis=0, keepdims=True)
    dx, dg = _rms_bwd(err * (1.0 / D), xv, gv)
    return (0.5 / D) * e2, dx, dg


def _mlp_fwd(x, g, wup, wdown, name, head=None):
    T, D = x.shape
    NB, _, fb = wup.shape
    tm = min(T, MLP_TM)
    n_head = 0 if head is None else 2

    def body(x_ref, g_ref, wu_ref, wd_ref, *rest):
        xo_ref, a_ref, h_ref = rest[n_head:n_head + 3]
        xv = x_ref[...]
        hb = (xv * _rms_r(xv) * g_ref[...]).astype(BF16)
        h_ref[...] = hb
        acc = xv
        for j in range(NB):
            a = _dot(hb, wu_ref[j])
            a_ref[:, j * fb:(j + 1) * fb] = a.astype(BF16)
            u = jnp.maximum(a, 0.0)
            acc = acc + _dot((u * u).astype(BF16), wd_ref[j * fb:(j + 1) * fb, :])

        if head is None:
            xo_ref[...] = acc
        else:
            t_ref, gf_ref = rest[:2]
            loss_ref, dgf_ref = rest[n_head + 3:n_head + 5]

            @pl.when(pl.program_id(0) == 0)
            def _():
                loss_ref[...] = jnp.zeros_like(loss_ref)
                dgf_ref[...] = jnp.zeros_like(dgf_ref)

            e2, dx, dg = _loss_head(acc, t_ref[...], gf_ref[...], D)
            xo_ref[...] = dx
            loss_ref[...] += e2
            dgf_ref[...] += dg

    row = pl.BlockSpec((tm, D), lambda i: (i, 0))
    vec = pl.BlockSpec((1, D), lambda i: (0, 0))
    once = pl.Buffered(1)
    in_specs = [row, vec, pl.BlockSpec((NB, D, fb), lambda i: (0, 0, 0), pipeline_mode=once),
                pl.BlockSpec((NB * fb, D), lambda i: (0, 0), pipeline_mode=once)]
    out_specs = [row, pl.BlockSpec((tm, NB * fb), lambda i: (i, 0)), row]
    out_shape = [S((T, D), F32), S((T, NB * fb), BF16), S((T, D), BF16)]
    if head is not None:
        in_specs += [row, vec]
        out_specs += [pl.BlockSpec((1, 128), lambda i: (0, 0)), vec]
        out_shape += [S((1, 128), F32), S((1, D), F32)]
    return pl.pallas_call(
        body, name=name, grid=(T // tm,),
        in_specs=in_specs, out_specs=out_specs, out_shape=out_shape,
        compiler_params=_cp("parallel" if head is None else "arbitrary"),
    )(x, g, wup, wdown, *(head or ()))


def _mlp_bwd_dx(x, dout, a, g, wup, wdown, name):
    T, D = x.shape
    NB, _, fb = wup.shape
    tm = min(T, MLP_TM)

    def body(x_ref, do_ref, a_ref, g_ref, wu_ref, wd_ref, dx_ref, da_ref, dob_ref, dg_ref):
        @pl.when(pl.program_id(0) == 0)
        def _():
            dg_ref[...] = jnp.zeros_like(dg_ref)

        dov = do_ref[...]
        dob = dov.astype(BF16)
        dob_ref[...] = dob
        dh = None
        for j in range(NB):
            cols = slice(j * fb, (j + 1) * fb)
            du2 = _dot_nt(dob, wd_ref[cols, :])
            u = jnp.maximum(a_ref[:, cols].astype(F32), 0.0)
            da = (du2 * (2.0 * u)).astype(BF16)
            da_ref[:, cols] = da
            part = _dot_nt(da, wu_ref[j])
            dh = part if dh is None else dh + part
        dx, dg = _rms_bwd(dh, x_ref[...], g_ref[...])
        dx_ref[...] = dov + dx
        dg_ref[...] += dg

    row = pl.BlockSpec((tm, D), lambda i: (i, 0))
    wide = pl.BlockSpec((tm, NB * fb), lambda i: (i, 0))
    vec = pl.BlockSpec((1, D), lambda i: (0, 0))
    once = pl.Buffered(1)
    return pl.pallas_call(
        body, name=name, grid=(T // tm,),
        in_specs=[row, row, wide, vec, pl.BlockSpec((NB, D, fb), lambda i: (0, 0, 0), pipeline_mode=once),
                  pl.BlockSpec((NB * fb, D), lambda i: (0, 0), pipeline_mode=once)],
        out_specs=[row, wide, row, vec],
        out_shape=[S((T, D), F32), S((T, NB * fb), BF16), S((T, D), BF16), S((1, D), F32)],
        compiler_params=_cp("arbitrary"),
    )(x, dout, a, g, wup, wdown)


def _adam_update(g, w, m, v):
    mn = ADAM_B1 * m + (1.0 - ADAM_B1) * g
    vn = ADAM_B2 * v + (1.0 - ADAM_B2) * (g * g)
    c1 = 1.0 - ADAM_B1 ** ADAM_STEP
    c2 = 1.0 - ADAM_B2 ** ADAM_STEP
    return (-ADAM_LR) * ((mn / c1) / (jnp.sqrt(vn / c2) + ADAM_EPS) + ADAM_WD * w), mn, vn


def _mlp_bwd_dw(h, da, a, dob, fb, name, adam_jobs=()):
    T, D = h.shape
    F = a.shape[1]
    NB = F // fb
    tk = min(T, 1024)
    nk = T // tk
    nj = len(adam_jobs)

    def body(h_ref, da_ref, a_ref, dob_ref, *rest):
        dwu_ref, dwd_ref = rest[4 * nj:4 * nj + 2]
        au = ad = None
        for k in range(nk):
            rows = slice(k * tk, (k + 1) * tk)
            pu = _dot_tn(h_ref[rows, :], da_ref[rows, :])
            u = jnp.maximum(a_ref[rows, :].astype(F32), 0.0)
            pd = _dot_tn((u * u).astype(BF16), dob_ref[rows, :])
            au, ad = (pu, pd) if au is None else (au + pu, ad + pd)
        dwu_ref[...] = au.astype(BF16)
        dwd_ref[...] = ad.astype(BF16)
        for q in range(nj):
            p_ref, w_ref, m_ref, v_ref = rest[4 * q:4 * q + 4]
            g_out, d_out, m_out, v_out = rest[4 * nj + 2 + 4 * q:4 * nj + 6 + 4 * q]
            g = p_ref[0].astype(F32)
            for i in range(1, p_ref.shape[0]):
                g = g + p_ref[i].astype(F32)
            d, mn, vn = _adam_update(g, w_ref[...], m_ref[...], v_ref[...])
            g_out[...], d_out[...], m_out[...], v_out[...] = g, d, mn, vn

    once = pl.Buffered(1)
    in_specs = [pl.BlockSpec((T, D), lambda j: (0, 0), pipeline_mode=once),
                pl.BlockSpec((T, fb), lambda j: (0, j)),
                pl.BlockSpec((T, fb), lambda j: (0, j)),
                pl.BlockSpec((T, D), lambda j: (0, 0), pipeline_mode=once)]
    out_specs = [pl.BlockSpec((None, D, fb), lambda j: (j, 0, 0)), pl.BlockSpec((fb, D), lambda j: (j, 0))]
    out_shape = [S((NB, D, fb), BF16), S((F, D), BF16)]
    operands = [h, da, a, dob]
    for parts, w3, m3, v3, layer in adam_jobs:
        P, R, C = parts.shape
        blk = pl.BlockSpec((None, R // NB, C), lambda j, layer=layer: (layer, j, 0))
        in_specs += [pl.BlockSpec((P, R // NB, C), lambda j: (0, j, 0)), blk, blk, blk]
        out_specs += [blk] * 4
        out_shape += [S(w3.shape, F32)] * 4
        operands += [parts, w3, m3, v3]
    outs = pl.pallas_call(
        body, name=name, grid=(NB,),
        in_specs=in_specs, out_specs=out_specs, out_shape=out_shape,
        compiler_params=_cp("parallel"),
    )(*operands)
    return outs[0], outs[1], [outs[2 + 4 * q:6 + 4 * q] for q in range(nj)]


def _rope_tables(L):
    nf = HEAD_DIM // 4
    t = jnp.arange(L, dtype=jnp.int32)
    row = (t // GRID_W).astype(F32)
    col = (t % GRID_W).astype(F32)
    inv = ROPE_THETA ** (-jnp.arange(nf, dtype=F32) / nf)
    ar = row[:, None] * inv
    ac = col[:, None] * inv
    cos = jnp.concatenate([jnp.cos(ar), jnp.cos(ar), jnp.cos(ac), jnp.cos(ac)], axis=-1)
    sin = jnp.concatenate([-jnp.sin(ar), jnp.sin(ar), -jnp.sin(ac), jnp.sin(ac)], axis=-1)
    return cos, sin


def _swap32(x):
    lane = lax.broadcasted_iota(jnp.int32, x.shape, 1)
    up = pltpu.roll(x, HEAD_DIM - 32, 1)
    down = pltpu.roll(x, 32, 1)
    return jnp.where((lane % 64) < 32, up, down)


def _qkv_proj(x, g, wt_blk, qg, kg, cos, sin, L, name):
    T, D = x.shape
    NB, nb, _ = wt_blk.shape
    W = NB * nb
    nh = W // HEAD_DIM - 2 * N_KV
    tm = min(L, 1024)
    lb = L // tm

    def body(x_ref, g_ref, w_ref, qg_ref, kg_ref, cos_ref, sin_ref, qkv_ref, h_ref, q_ref, k_ref, v_ref):
        for ch in range(PROJ_CHAINS):
            rows = slice(ch * tm // PROJ_CHAINS, (ch + 1) * tm // PROJ_CHAINS)
            xv = x_ref[rows, :]
            hb = (xv * _rms_r(xv) * g_ref[...]).astype(BF16)
            h_ref[rows, :] = hb
            for j in range(NB):
                qkv_ref[rows, j * nb:(j + 1) * nb] = _dot_nt(hb, w_ref[j])
            c = cos_ref[rows, :]
            s = sin_ref[rows, :]
            for h in range(nh + N_KV):
                xh = qkv_ref[rows, h * HEAD_DIM:(h + 1) * HEAD_DIM]
                gv = qg_ref[...] if h < nh else kg_ref[...]
                y = xh * _rms_r(xh) * gv
                y = (y * c + _swap32(y) * s).astype(BF16)
                if h < nh:
                    q_ref[rows, h * HEAD_DIM:(h + 1) * HEAD_DIM] = y
                else:
                    k_ref[rows, (h - nh) * HEAD_DIM:(h - nh + 1) * HEAD_DIM] = y
            v_ref[rows, :] = qkv_ref[rows, (nh + N_KV) * HEAD_DIM:].astype(BF16)

    row = lambda cols: pl.BlockSpec((tm, cols), lambda i: (i, 0))
    vec = lambda cols: pl.BlockSpec((1, cols), lambda i: (0, 0))
    table = pl.BlockSpec((tm, HEAD_DIM), lambda i: (i % lb, 0))
    return pl.pallas_call(
        body, name=name, grid=(T // tm,),
        in_specs=[row(D), vec(D), pl.BlockSpec(wt_blk.shape, lambda i: (0, 0, 0)), vec(HEAD_DIM), vec(HEAD_DIM), table, table],
        out_specs=[row(W), row(D), row(nh * HEAD_DIM), row(N_KV * HEAD_DIM), row(N_KV * HEAD_DIM)],
        out_shape=[S((T, W), F32), S((T, D), BF16), S((T, nh * HEAD_DIM), BF16), S((T, N_KV * HEAD_DIM), BF16),
                   S((T, N_KV * HEAD_DIM), BF16)],
        compiler_params=_cp("parallel"),
    )(x, g, wt_blk, qg, kg, cos, sin)


def _qk_prep_bwd(qkv, dq, dk, dv, qg, kg, cos, sin, L, name):
    T, W = qkv.shape
    nh = W // HEAD_DIM - 2 * N_KV
    tm = min(L, 1024)
    lb = L // tm

    def body(qkv_ref, dq_ref, dk_ref, dv_ref, qg_ref, kg_ref, cos_ref, sin_ref, dz_ref, dqg_ref, dkg_ref):
        @pl.when(pl.program_id(0) == 0)
        def _():
            dqg_ref[...] = jnp.zeros_like(dqg_ref)
            dkg_ref[...] = jnp.zeros_like(dkg_ref)

        c = cos_ref[...]
        s = sin_ref[...]
        for h in range(nh + N_KV):
            cols = slice(h * HEAD_DIM, (h + 1) * HEAD_DIM)
            if h < nh:
                dout, gv, dg_ref = dq_ref[:, cols], qg_ref[...], dqg_ref
            else:
                kc = slice((h - nh) * HEAD_DIM, (h - nh + 1) * HEAD_DIM)
                dout, gv, dg_ref = dk_ref[:, kc], kg_ref[...], dkg_ref
            dy = dout * c - _swap32(dout) * s
            dx, dg = _rms_bwd(dy, qkv_ref[:, cols], gv)
            dg_ref[...] += dg
            dz_ref[:, cols] = dx.astype(BF16)
        dz_ref[:, (nh + N_KV) * HEAD_DIM:] = dv_ref[...].astype(BF16)

    return pl.pallas_call(
        body, name=name, grid=(T // tm,),
        in_specs=[pl.BlockSpec((tm, W), lambda i: (i, 0)),
                  pl.BlockSpec((tm, nh * HEAD_DIM), lambda i: (i, 0)),
                  pl.BlockSpec((tm, N_KV * HEAD_DIM), lambda i: (i, 0)),
                  pl.BlockSpec((tm, N_KV * HEAD_DIM), lambda i: (i, 0)),
                  pl.BlockSpec((1, HEAD_DIM), lambda i: (0, 0)),
                  pl.BlockSpec((1, HEAD_DIM), lambda i: (0, 0)),
                  pl.BlockSpec((tm, HEAD_DIM), lambda i: (i % lb, 0)),
                  pl.BlockSpec((tm, HEAD_DIM), lambda i: (i % lb, 0))],
        out_specs=[pl.BlockSpec((tm, W), lambda i: (i, 0)),
                   pl.BlockSpec((1, HEAD_DIM), lambda i: (0, 0)),
                   pl.BlockSpec((1, HEAD_DIM), lambda i: (0, 0))],
        out_shape=[S((T, W), BF16), S((1, HEAD_DIM), F32), S((1, HEAD_DIM), F32)],
        compiler_params=_cp("arbitrary"),
    )(qkv, dq, dk, dv, qg, kg, cos, sin)


EXP2_SCALE = SOFTMAX_SCALE * math.log2(math.e)
ATTN_SUB = 256
ATTN_TQ = 2048


def _softmax_rows(q, k):
    s = _dot_nt(q, k)
    e = jnp.exp2((s - jnp.max(s, axis=-1, keepdims=True)) * EXP2_SCALE)
    return e, jnp.sum(e, axis=-1, keepdims=True)


def _attn_fwd(q, k, v, L, name):
    T = q.shape[0]
    nh = q.shape[1] // HEAD_DIM
    G = nh // N_KV
    B = T // L
    tq = min(L, ATTN_TQ)
    nq = L // tq
    sub = min(tq, ATTN_SUB)

    def body(q_ref, k_ref, v_ref, o_ref):
        for h in range(tq // sub):
            rows = slice(h * sub, (h + 1) * sub)
            e, l = _softmax_rows(q_ref[rows, :], k_ref[...])
            o_ref[rows, :] = (_dot(e.astype(BF16), v_ref[...]) / l).astype(BF16)

    qspec = pl.BlockSpec((tq, HEAD_DIM), lambda b, kv, g, qi: (b * nq + qi, kv * G + g))
    kspec = pl.BlockSpec((L, HEAD_DIM), lambda b, kv, g, qi: (b, kv))
    return pl.pallas_call(
        body, name=name, grid=(B, N_KV, G, nq),
        in_specs=[qspec, kspec, kspec],
        out_specs=qspec,
        out_shape=S((T, nh * HEAD_DIM), BF16),
        compiler_params=_cp("parallel", "parallel", "parallel", "parallel"),
    )(q, k, v)


def _attn_bwd(q, k, v, do, o, L, name):
    T = q.shape[0]
    nh = q.shape[1] // HEAD_DIM
    G = nh // N_KV
    B = T // L
    tq = min(L, ATTN_TQ)
    nq = L // tq
    sub = min(tq, ATTN_SUB)

    def body(q_ref, k_ref, v_ref, do_ref, o_ref, dq_ref, dk_ref, dv_ref, ds_scr, p_scr):
        first = (pl.program_id(2) == 0) & (pl.program_id(3) == 0)
        last = (pl.program_id(2) == G - 1) & (pl.program_id(3) == nq - 1)

        @pl.when(first)
        def _():
            dk_ref[...] = jnp.zeros_like(dk_ref)
            dv_ref[...] = jnp.zeros_like(dv_ref)

        for h in range(tq // sub):
            rows = slice(h * sub, (h + 1) * sub)
            dov = do_ref[rows, :]
            e, l = _softmax_rows(q_ref[rows, :], k_ref[...])
            p = e * (1.0 / l)
            dsum = jnp.sum(dov.astype(F32) * o_ref[rows, :].astype(F32), axis=-1, keepdims=True)
            ds_scr[rows, :] = (p * (_dot_nt(dov, v_ref[...]) - dsum)).astype(BF16)
            p_scr[rows, :] = p.astype(BF16)
        ds = ds_scr[...]
        dq_ref[...] = _dot(ds, k_ref[...]) * SOFTMAX_SCALE
        dk_ref[...] += _dot_tn(ds, q_ref[...])
        dv_ref[...] += _dot_tn(p_scr[...], do_ref[...])

        @pl.when(last)
        def _():
            dk_ref[...] = dk_ref[...] * SOFTMAX_SCALE

    qspec = pl.BlockSpec((tq, HEAD_DIM), lambda b, kv, g, qi: (b * nq + qi, kv * G + g))
    kspec = pl.BlockSpec((L, HEAD_DIM), lambda b, kv, g, qi: (b, kv))
    return pl.pallas_call(
        body, name=name, grid=(B, N_KV, G, nq),
        in_specs=[qspec, kspec, kspec, qspec, qspec],
        out_specs=[qspec, kspec, kspec],
        out_shape=[S((T, nh * HEAD_DIM), F32), S((T, N_KV * HEAD_DIM), F32), S((T, N_KV * HEAD_DIM), F32)],
        scratch_shapes=[pltpu.VMEM((tq, L), BF16), pltpu.VMEM((tq, L), BF16)],
        compiler_params=_cp("parallel", "parallel", "arbitrary", "arbitrary"),
    )(q, k, v, do, o)


PV_CONV_W = 0
PV_B_A = 8
PV_B_X = 16
PV_LAM = 24
PV_CONV_B = 32
PV_ROWS = 40


def _shift_rows(x, k):
    if k == 0:
        return x
    L = x.shape[0]
    n = N_SEG * abs(k)
    seg = lax.broadcasted_iota(jnp.int32, (n, x.shape[1]), 0) % N_SEG
    if k > 0:
        edge = jnp.where(seg == 0, 0.0, pltpu.roll(x[L - n:], 1, 0))
        return jnp.concatenate([edge, x[:L - n]], axis=0)
    edge = jnp.where(seg == N_SEG - 1, 0.0, pltpu.roll(x[:n], n - 1, 0))
    return jnp.concatenate([x[n:], edge], axis=0)


def _conv_taps(rec, pv):
    c = pv[PV_CONV_B:PV_CONV_B + 1]
    for j in range(CONV_W):
        c = c + pv[PV_CONV_W + j:PV_CONV_W + j + 1] * _shift_rows(rec, 2 - j)
    return c


def _sigmoid(x):
    return 0.5 * jnp.tanh(0.5 * x) + 0.5


EXPM1_SERIES_BELOW = 0.03


def _rg_gates(c, cbf, wa, wx, ba, bx, lam):
    r = _sigmoid(_dot(cbf, wa) + ba)
    i = _sigmoid(_dot(cbf, wx) + bx)
    sp = jnp.maximum(-lam, 0.0) + jnp.log1p(jnp.exp(-jnp.abs(lam)))
    la = r * ((-RG_C) * sp)
    a = jnp.exp(la)
    a2 = a * a
    x = la + la
    series = -(x * ((x * (1.0 / 6.0) + 0.5) * x + 1.0))
    om = jnp.where(x > -EXPM1_SERIES_BELOW, series, 1.0 - a2)
    rm = lax.rsqrt(om)
    return r, i, a, om * rm, rm, a2, sp


def _gelu(x):
    t = jnp.tanh(GELU_K * (x + GELU_C * x * x * x))
    return 0.5 * x * (1.0 + t), t


def _scan_pair(af_ref, uf_ref, ab_ref, ub_ref, hf_ref, hb_ref, pf_ref, pb_ref, L):
    ls = L // N_SEG
    zero = jnp.zeros((N_SEG, LRU_BW), F32)
    one = jnp.ones((N_SEG, LRU_BW), F32)
    tile = lambda t: pl.ds(pl.multiple_of(t * N_SEG, N_SEG), N_SEG)

    def steps(tc, carry):
        hf, pf, hb, pb = carry
        for q in range(SCAN_UNROLL):
            t = tc * SCAN_UNROLL + q
            rf, rb = tile(t), tile(ls - 1 - t)
            af = af_ref[rf, :]
            hf = af * hf + uf_ref[rf, :]
            pf = pf * af
            hf_ref[rf, :] = hf
            pf_ref[rf, :] = pf
            ab = ab_ref[rb, :]
            hb = ab * hb + ub_ref[rb, :]
            pb = pb * ab
            hb_ref[rb, :] = hb
            pb_ref[rb, :] = pb
        return hf, pf, hb, pb

    hf_e, pf_e, hb_e, pb_e = lax.fori_loop(0, ls // SCAN_UNROLL, steps, (zero, one, zero, one))

    rows, cin = [], jnp.zeros((1, LRU_BW), F32)
    for s in range(N_SEG):
        rows.append(cin)
        cin = hf_e[s:s + 1] + pf_e[s:s + 1] * cin
    cf = jnp.concatenate(rows, axis=0)
    rows, cin = [], jnp.zeros((1, LRU_BW), F32)
    for s in reversed(range(N_SEG)):
        rows.append(cin)
        cin = hb_e[s:s + 1] + pb_e[s:s + 1] * cin
    cb = jnp.concatenate(rows[::-1], axis=0)

    def fix(tc, _):
        for q in range(SCAN_UNROLL):
            r = tile(tc * SCAN_UNROLL + q)
            hf_ref[r, :] = hf_ref[r, :] + pf_ref[r, :] * cf
            hb_ref[r, :] = hb_ref[r, :] + pb_ref[r, :] * cb
        return 0

    lax.fori_loop(0, ls // SCAN_UNROLL, fix, 0)


def _rg_specs(L, D, nblk):
    slab = lambda off: pl.BlockSpec((L, LRU_BW), lambda cb, b: (b, off + cb))
    wspec = pl.BlockSpec((2, None, LRU_BW, LRU_BW), lambda cb, b: (0, cb, 0, 0))
    pvspec = pl.BlockSpec((PV_ROWS, LRU_BW), lambda cb, b: (0, cb))
    return slab, wspec, pvspec


def _rg_fwd(z, pvec, wa, wx, L, name):
    T, C2 = z.shape
    C = C2 // 2
    nblk = C // LRU_BW
    B = T // L
    slab, wspec, pvspec = _rg_specs(L, C, nblk)

    def body(gp_ref, rec_ref, pv_ref, wa_ref, wx_ref, yg_ref, hf_ref, hb_ref, a_scr, u_scr, p_scr):
        pv = pv_ref[...]
        c = _conv_taps(rec_ref[...], pv)
        cbf = c.astype(BF16)
        for d in range(2):
            _, i, a, m, _, _, _ = _rg_gates(c, cbf, wa_ref[d], wx_ref[d], pv[PV_B_A + d:PV_B_A + d + 1],
                                      pv[PV_B_X + d:PV_B_X + d + 1], pv[PV_LAM + d:PV_LAM + d + 1])
            a_scr[d] = a
            u_scr[d] = m * (i * c)
        _scan_pair(a_scr.at[0], u_scr.at[0], a_scr.at[1], u_scr.at[1], hf_ref, hb_ref, p_scr.at[0], p_scr.at[1], L)
        gate, _ = _gelu(gp_ref[...])
        yg_ref[...] = ((hf_ref[...] + hb_ref[...]) * gate).astype(BF16)

    return pl.pallas_call(
        body, name=name, grid=(nblk, B),
        in_specs=[slab(0), slab(nblk), pvspec, wspec, wspec],
        out_specs=[slab(0), slab(0), slab(0)],
        out_shape=[S((T, C), BF16), S((T, C), F32), S((T, C), F32)],
        scratch_shapes=[pltpu.VMEM((2, L, LRU_BW), F32)] * 3,
        compiler_params=_cp("parallel", "parallel"),
    )(z, z, pvec, wa, wx)


def _rg_bwd(z, hf, hb, dyg, pvec, wa, wx, L, name):
    T, C2 = z.shape
    C = C2 // 2
    nblk = C // LRU_BW
    B = T // L
    slab, wspec, pvspec = _rg_specs(L, C, nblk)

    def body(gp_ref, rec_ref, hf_ref, hb_ref, dyg_ref, pv_ref, wa_ref, wx_ref,
             dz_ref, dwa_ref, dwx_ref, dpv_ref, a_scr, u_scr, d_scr, p_scr):
        @pl.when(pl.program_id(1) == 0)
        def _():
            dwa_ref[...] = jnp.zeros_like(dwa_ref)
            dwx_ref[...] = jnp.zeros_like(dwx_ref)
            dpv_ref[...] = jnp.zeros_like(dpv_ref)

        pv = pv_ref[...]
        rec = rec_ref[...]
        c = _conv_taps(rec, pv)
        cbf = c.astype(BF16)
        gp = gp_ref[...]
        gate, th = _gelu(gp)
        dgelu = 0.5 * (1.0 + th) + 0.5 * gp * (1.0 - th * th) * GELU_K * (1.0 + 3.0 * GELU_C * gp * gp)
        dyg = dyg_ref[...]
        dz_ref[0] = (dyg * (hf_ref[...] + hb_ref[...]) * dgelu).astype(BF16)
        dy = dyg * gate

        gates = []
        for d in range(2):
            gates.append(_rg_gates(c, cbf, wa_ref[d], wx_ref[d], pv[PV_B_A + d:PV_B_A + d + 1],
                                   pv[PV_B_X + d:PV_B_X + d + 1], pv[PV_LAM + d:PV_LAM + d + 1]))
        a_scr[0] = _shift_rows(gates[1][2], 1)
        a_scr[1] = _shift_rows(gates[0][2], -1)
        u_scr[...] = dy
        _scan_pair(a_scr.at[0], u_scr, a_scr.at[1], u_scr, d_scr.at[1], d_scr.at[0], p_scr.at[0], p_scr.at[1], L)

        dc = jnp.zeros_like(c)
        rows = []
        for d in range(2):
            r, i, a, m, rm, a2, sp = gates[d]
            delta = d_scr[d]
            hnb = _shift_rows(hf_ref[...], 1) if d == 0 else _shift_rows(hb_ref[...], -1)
            da = delta * hnb
            dm = delta * (i * c)
            di = delta * (m * c)
            dc = dc + delta * (m * i)
            dla = da * a - dm * (a2 * rm)
            dpa = (dla * ((-RG_C) * sp)) * (r * (1.0 - r))
            dpx = di * (i * (1.0 - i))
            dsp = (-RG_C) * jnp.sum(dla * r, axis=0, keepdims=True)
            lam = pv[PV_LAM + d:PV_LAM + d + 1]
            rows.append((jnp.sum(dpa, axis=0, keepdims=True), jnp.sum(dpx, axis=0, keepdims=True),
                         -dsp * _sigmoid(-lam)))
            dpab = dpa.astype(BF16)
            dpxb = dpx.astype(BF16)
            dwa_ref[d] += _dot_tn(cbf, dpab)
            dwx_ref[d] += _dot_tn(cbf, dpxb)
            dc = dc + _dot_nt(dpab, wa_ref[d]) + _dot_nt(dpxb, wx_ref[d])

        drec = jnp.zeros_like(c)
        dcw = []
        for j in range(CONV_W):
            drec = drec + pv[PV_CONV_W + j:PV_CONV_W + j + 1] * _shift_rows(dc, j - 2)
            dcw.append(jnp.sum(dc * _shift_rows(rec, 2 - j), axis=0, keepdims=True))
        dz_ref[1] = drec.astype(BF16)
        for j in range(CONV_W):
            dpv_ref[PV_CONV_W + j:PV_CONV_W + j + 1, :] += dcw[j]
        for d in range(2):
            dpv_ref[PV_B_A + d:PV_B_A + d + 1, :] += rows[d][0]
            dpv_ref[PV_B_X + d:PV_B_X + d + 1, :] += rows[d][1]
            dpv_ref[PV_LAM + d:PV_LAM + d + 1, :] += rows[d][2]
        dpv_ref[PV_CONV_B:PV_CONV_B + 1, :] += jnp.sum(dc, axis=0, keepdims=True)

    return pl.pallas_call(
        body, name=name, grid=(nblk, B),
        in_specs=[slab(0), slab(nblk), slab(0), slab(0), slab(0), pvspec, wspec, wspec],
        out_specs=[pl.BlockSpec((2, L, LRU_BW), lambda cb, b: (0, b, cb)), wspec, wspec, pvspec],
        out_shape=[S((2, T, C), BF16), S((2, nblk, LRU_BW, LRU_BW), F32), S((2, nblk, LRU_BW, LRU_BW), F32),
                   S((PV_ROWS, C), F32)],
        scratch_shapes=[pltpu.VMEM((2, L, LRU_BW), F32), pltpu.VMEM((L, LRU_BW), F32),
                        pltpu.VMEM((2, L, LRU_BW), F32), pltpu.VMEM((2, L, LRU_BW), F32)],
        compiler_params=_cp("parallel", "arbitrary"),
    )(z, z, hf, hb, dyg, pvec, wa, wx)


QKV_NB = 512


def _interleave(a):
    *lead, L, D = a.shape
    return a.reshape(*lead, N_SEG, L // N_SEG, D).swapaxes(-3, -2).reshape(*lead, L, D)


def _deinterleave(a):
    *lead, L, D = a.shape
    return a.reshape(*lead, L // N_SEG, N_SEG, D).swapaxes(-3, -2).reshape(*lead, L, D)


def _local_step(x3, tgt3, w, fetch, send, early_adam):
    Bl, L, D = x3.shape
    T = Bl * L
    x = _interleave(x3).reshape(T, D)
    tgt = _interleave(tgt3).reshape(T, D)
    gm = [w["g_mix"][i:i + 1] for i in range(2)]
    gl = [w["g_mlp"][i:i + 1] for i in range(2)]

    w0 = fetch(0, ())
    nb_in = w0["w_in"].shape[-1]
    z, h0 = _norm_matmul(x, gm[0], w0["w_in"], "rg_in")
    yg, hf, hb = _rg_fwd(z, w0["pvec"], w["wa"], w["wx"], L, "rg_fwd")
    w1 = fetch(1, (yg,))
    x1 = _matmul_res(yg, w1["w_out"], x, "rg_out")
    w1.update(fetch(4, (x1,)))
    fb = w1["w_up0"].shape[-1]
    x2, a0, hm0 = _mlp_fwd(x1, gl[0], w1["w_up0"], w1["w_down0"], "mlp0_fwd")
    w2 = fetch(2, (x2,))
    cos, sin = [_interleave(t) for t in _rope_tables(L)]
    qkv, h1, qn, kn, vb = _qkv_proj(x2, gm[1], w2["w_qkv"], w["qg"], w["kg"], cos, sin, L, "at_qkv")
    o = _attn_fwd(qn, kn, vb, L, "at_fwd")
    x3_ = _matmul_res(o, w2["w_o"], x2, "at_out")
    w3 = fetch(3, (x3_,))
    dx4, a1, hm1, loss, dgf = _mlp_fwd(x3_, gl[1], w3["w_up1"], w3["w_down1"], "mlp1_fwd", head=(tgt, w["g_fin"]))

    dx3, da1, dob1, dgl1 = _mlp_bwd_dx(x3_, dx4, a1, gl[1], w3["w_up1"], w3["w_down1"], "mlp1_bwd_dx")
    dwu1, dwd1, _ = _mlp_bwd_dw(hm1, da1, a1, dob1, fb, "mlp1_bwd_dw")
    sent = send(3, dict(w_up1=dwu1, w_down1=dwd1))
    do, dx3b = _matmul_nt(dx3, w2["w_o"], "at_out_bwd", BF16, after=sent)
    dwo = _matmul_tn(o, dx3b[None], QKV_NB, "at_out_dw", blocked=False)
    dq, dk, dv = _attn_bwd(qn, kn, vb, do, o, L, "at_bwd")
    dqkv, dqg, dkg = _qk_prep_bwd(qkv, dq, dk, dv, w["qg"], w["kg"], cos, sin, L, "at_prep_bwd")
    dwqkv = _matmul_tn(dqkv, h1[None], QKV_NB, "at_qkv_dw", blocked=False)
    sent = send(2, dict(w_qkv=dwqkv, w_o=dwo))
    dx2, dgm1 = _nt_normbwd(dqkv[None], w2["w_qkv"], x2, gm[1], dx3, "at_qkv_bwd", after=sent, transposed=True)
    dx1, da0, dob0, dgl0 = _mlp_bwd_dx(x1, dx2, a0, gl[0], w1["w_up0"], w1["w_down0"], "mlp0_bwd_dx")
    jobs, sink = early_adam((dob0,))
    dwu0, dwd0, early = _mlp_bwd_dw(hm0, da0, a0, dob0, fb, "mlp0_bwd_dw", jobs)
    sink(early)
    sent = send(1, dict(w_up0=dwu0, w_down0=dwd0))
    dyg, dx1b = _matmul_nt(dx1, w1["w_out"], "rg_out_bwd", F32, after=sent)
    dwout = _matmul_tn(yg, dx1b[None], QKV_NB, "rg_out_dw", blocked=False)
    dz, dwa, dwx, dpv = _rg_bwd(z, hf, hb, dyg, w0["pvec"], w["wa"], w["wx"], L, "rg_bwd")
    sent = send(4, dict(w_out=dwout, pvec=dpv, wa=dwa, wx=dwx))
    dwin = _matmul_tn(h0, dz, nb_in, "rg_in_dw", blocked=True, after=sent)
    sent = send(0, dict(w_in=dwin))
    dx0, dgm0 = _nt_normbwd(dz, w0["w_in"], x, gm[0], dx1, "rg_in_bwd", after=sent)
    send(-1, dict(g_mix=[dgm0, dgm1], g_mlp=[dgl0, dgl1], g_fin=dgf, conv_b=dpv[PV_CONV_B:PV_CONV_B + 1], qg=dqg, kg=dkg, loss=loss))
    return _deinterleave(dx0.reshape(Bl, L, D))


MESH = pl.DeviceIdType.MESH
ANY = pl.BlockSpec(memory_space=pl.ANY)
N_PEERS = N_DEV - 1


def _my_place():
    return lax.axis_index("x"), lax.axis_index("y"), lax.axis_index("c")


def _flat(px, py, pc):
    return 4 * px + 2 * py + pc


def _all_gather(shards, name):
    n = len(shards)

    def body(*refs):
        ins, outs = refs[:n], refs[n:2 * n]
        send_sems, recv_sems, local_sems = refs[2 * n:]
        x, y, c = _my_place()
        me, sibling = (x, y, c), (x, y, 1 - c)
        chips = [(1 - x, y), (x, 1 - y), (1 - x, 1 - y)]

        def copy(a, k, block, to, src=None):
            dst = outs[a].at[_flat(*block)]
            return pltpu.make_async_remote_copy(
                src_ref=dst if src is None else src, dst_ref=dst,
                send_sem=send_sems.at[a, k], recv_sem=recv_sems.at[a, k],
                device_id=to, device_id_type=MESH)

        mine = [pltpu.make_async_copy(ins[a], outs[a].at[_flat(*me)], local_sems.at[a]) for a in range(n)]
        for cp in mine:
            cp.start()
        first = []
        for a in range(n):
            first.append(copy(a, 0, me, sibling, src=ins[a]))
            first += [copy(a, 1 + j, me, (*chip, c), src=ins[a]) for j, chip in enumerate(chips)]
        for cp in first:
            cp.start()
        passed = []
        for j, chip in enumerate(chips):
            for a in range(n):
                copy(a, 1 + j, (*chip, c), me).wait_recv()
                fwd = copy(a, 4 + j, (*chip, c), sibling)
                fwd.start()
                passed.append(fwd)
        for a in range(n):
            copy(a, 0, sibling, me).wait_recv()
            for j, chip in enumerate(chips):
                copy(a, 4 + j, (*chip, 1 - c), me).wait_recv()
        for cp in first + passed:
            cp.wait_send()
        for cp in mine:
            cp.wait()

    return pl.pallas_call(
        body, name=name,
        in_specs=[ANY] * n, out_specs=[ANY] * n,
        out_shape=[S((N_DEV,) + s.shape, s.dtype) for s in shards],
        scratch_shapes=[pltpu.SemaphoreType.DMA((n, N_PEERS)), pltpu.SemaphoreType.DMA((n, N_PEERS)),
                        pltpu.SemaphoreType.DMA((n,))],
    )(*shards)


HBM = pl.BlockSpec(memory_space=pltpu.HBM)
SEM = pl.BlockSpec(memory_space=pltpu.SEMAPHORE)
SIDE_EFFECT = pltpu.SideEffectType.DATAFLOW_SIDE_EFFECTING
SEMS_PER_GROUP = 3


NEAR_PEERS = (1, 2, 4, 6)
FAR_CHIPS = (2, 4, 6)


def _exchange_copies(srcs, lands, sems, mode):
    send_sems, recv_sems, local_sems = sems
    scatter = mode == "scatter"
    x, y, c = _my_place()
    me = _flat(x, y, c)
    remote, local = [], []
    for a in range(len(srcs)):
        for r in (NEAR_PEERS if mode == "near" else range(1, N_DEV)):
            peer = (1 - x if r & 4 else x, 1 - y if r & 2 else y, 1 - c if r & 1 else c)
            remote.append(pltpu.make_async_remote_copy(
                src_ref=srcs[a].at[_flat(*peer)] if scatter else srcs[a], dst_ref=lands[a].at[me],
                send_sem=send_sems.at[a * N_PEERS + r - 1], recv_sem=recv_sems.at[a * N_PEERS + r - 1],
                device_id=peer, device_id_type=MESH))
        local.append(pltpu.make_async_copy(srcs[a].at[me] if scatter else srcs[a], lands[a].at[me], local_sems.at[a]))
    return remote, local


def _exchange_start(groups, modes, name):
    sizes = [len(g) for g in groups]
    srcs = [pltpu.with_memory_space_constraint(a, pltpu.HBM) for g in groups for a in g]
    n = len(srcs)
    scatter_of = [m == "scatter" for g, m in zip(groups, modes) for _ in g]
    lands = [pltpu.with_memory_space_constraint(lax.empty(a.shape if sc else (N_DEV,) + a.shape, a.dtype), pltpu.HBM)
             for a, sc in zip(srcs, scatter_of)]
    n_sem = SEMS_PER_GROUP * len(groups)

    def body(*refs):
        src_refs, land_refs, sem_refs, token = refs[:n], refs[n:2 * n], refs[2 * n:2 * n + n_sem], refs[-1]
        off = 0
        for gi, k in enumerate(sizes):
            remote, local = _exchange_copies(src_refs[off:off + k], land_refs[off:off + k],
                                             sem_refs[SEMS_PER_GROUP * gi:SEMS_PER_GROUP * (gi + 1)], modes[gi])
            for cp in local + remote:
                cp.start()
            off += k
        token[...] = jnp.zeros_like(token)

    sem_shapes = []
    for k in sizes:
        sem_shapes += [pltpu.SemaphoreType.DMA((k * N_PEERS,)), pltpu.SemaphoreType.DMA((k * N_PEERS,)),
                       pltpu.SemaphoreType.DMA((k,))]
    outs = pl.pallas_call(
        body, name=name,
        out_shape=sem_shapes + [pltpu.HBM(a.shape, a.dtype) for a in srcs + lands] + [S((8, 128), F32)],
        in_specs=[HBM] * (2 * n),
        out_specs=[SEM] * n_sem + [HBM] * (2 * n) + [pl.BlockSpec(memory_space=pltpu.VMEM)],
        input_output_aliases={i: n_sem + i for i in range(2 * n)},
        compiler_params=pltpu.CompilerParams(has_side_effects=SIDE_EFFECT),
    )(*srcs, *lands)
    sems, thru, token = outs[:n_sem], outs[n_sem:n_sem + 2 * n], outs[-1]
    per_group, off = [], 0
    for gi, k in enumerate(sizes):
        per_group.append((sems[SEMS_PER_GROUP * gi:SEMS_PER_GROUP * (gi + 1)], thru[off:off + k], thru[n + off:n + off + k]))
        off += k
    return per_group, token


def _exchange_wait(group, after, mode, name):
    sems, srcs, lands = group
    k = len(srcs)

    def body(*refs):
        remote, local = _exchange_copies(refs[:k], refs[k:2 * k], refs[2 * k:2 * k + SEMS_PER_GROUP], mode)
        for cp in remote:
            cp.wait_send()
            cp.wait_recv()
        for cp in local:
            cp.wait()

    outs = pl.pallas_call(
        body, name=name,
        out_shape=[pltpu.HBM(a.shape, a.dtype) for a in list(srcs) + list(lands)],
        in_specs=[HBM] * (2 * k) + [SEM] * SEMS_PER_GROUP + [ANY] * len(after),
        out_specs=[HBM] * (2 * k),
        input_output_aliases={i: i for i in range(2 * k)},
        compiler_params=pltpu.CompilerParams(has_side_effects=SIDE_EFFECT),
    )(*srcs, *lands, *sems, *after)
    return outs[k:]


def _forward_copies(lands, sems):
    send_sems, recv_sems = sems
    x, y, c = _my_place()
    mine, theirs = [], []
    for a in range(len(lands)):
        for k, r in enumerate(FAR_CHIPS):
            px, py = (1 - x if r & 4 else x), (1 - y if r & 2 else y)
            for out, core in ((mine, c), (theirs, 1 - c)):
                blk = lands[a].at[_flat(px, py, core)]
                out.append(pltpu.make_async_remote_copy(
                    src_ref=blk, dst_ref=blk, send_sem=send_sems.at[a * len(FAR_CHIPS) + k],
                    recv_sem=recv_sems.at[a * len(FAR_CHIPS) + k], device_id=(x, y, 1 - c), device_id_type=MESH))
    return mine, theirs


def _forward_start(groups, name):
    sizes = [len(g) for g in groups]
    lands = [a for g in groups for a in g]
    n = len(lands)
    n_sem = 2 * len(groups)

    def body(*refs):
        land_refs, sem_refs, token = refs[:n], refs[n:n + n_sem], refs[-1]
        off = 0
        for gi, k in enumerate(sizes):
            mine, _ = _forward_copies(land_refs[off:off + k], sem_refs[2 * gi:2 * gi + 2])
            for cp in mine:
                cp.start()
            off += k
        token[...] = jnp.zeros_like(token)

    sem_shapes = []
    for k in sizes:
        sem_shapes += [pltpu.SemaphoreType.DMA((k * len(FAR_CHIPS),))] * 2
    outs = pl.pallas_call(
        body, name=name,
        out_shape=sem_shapes + [pltpu.HBM(a.shape, a.dtype) for a in lands] + [S((8, 128), F32)],
        in_specs=[HBM] * n,
        out_specs=[SEM] * n_sem + [HBM] * n + [pl.BlockSpec(memory_space=pltpu.VMEM)],
        input_output_aliases={i: n_sem + i for i in range(n)},
        compiler_params=pltpu.CompilerParams(has_side_effects=SIDE_EFFECT),
    )(*lands)
    per_group, off = [], 0
    for gi, k in enumerate(sizes):
        per_group.append((outs[2 * gi:2 * gi + 2], outs[n_sem + off:n_sem + off + k]))
        off += k
    return per_group


def _forward_wait(group, after, name):
    sems, lands = group
    k = len(lands)

    def body(*refs):
        mine, theirs = _forward_copies(refs[:k], refs[k:k + 2])
        for cp in mine:
            cp.wait_send()
        for cp in theirs:
            cp.wait_recv()

    return pl.pallas_call(
        body, name=name,
        out_shape=[pltpu.HBM(a.shape, a.dtype) for a in lands],
        in_specs=[HBM] * k + [SEM] * 2 + [ANY] * len(after),
        out_specs=[HBM] * k,
        input_output_aliases={i: i for i in range(k)},
        compiler_params=pltpu.CompilerParams(has_side_effects=SIDE_EFFECT),
    )(*lands, *sems, *after)


def _row_tile(rows, cols):
    want = max(16, (128 * 1024) // cols)
    if rows <= want:
        return rows
    t = want - want % 16
    while rows % t:
        t -= 16
    return t


def _sum_parts(parts, name, after=()):
    P, R, C = parts.shape
    tr = _row_tile(R, C)

    def body(p_ref, *rest):
        o_ref = rest[-1]
        g = p_ref[0].astype(F32)
        for i in range(1, P):
            g = g + p_ref[i].astype(F32)
        o_ref[...] = g

    return pl.pallas_call(
        body, name=name, grid=(R // tr,),
        in_specs=[pl.BlockSpec((P, tr, C), lambda i: (0, i, 0))] + [ANY] * len(after),
        out_specs=pl.BlockSpec((tr, C), lambda i: (i, 0)),
        out_shape=S((R, C), F32),
        compiler_params=_cp("parallel"),
    )(parts, *after)


def _adamw(parts, w, m, v, name, after=()):
    P, R, C = parts.shape
    tr = _row_tile(R, C)
    c1 = 1.0 - ADAM_B1 ** ADAM_STEP
    c2 = 1.0 - ADAM_B2 ** ADAM_STEP

    def body(p_ref, w_ref, m_ref, v_ref, *rest):
        g_ref, d_ref, mo_ref, vo_ref = rest[len(after):]
        g = p_ref[0].astype(F32)
        for i in range(1, P):
            g = g + p_ref[i].astype(F32)
        mn = ADAM_B1 * m_ref[...] + (1.0 - ADAM_B1) * g
        vn = ADAM_B2 * v_ref[...] + (1.0 - ADAM_B2) * (g * g)
        g_ref[...] = g
        mo_ref[...] = mn
        vo_ref[...] = vn
        d_ref[...] = (-ADAM_LR) * ((mn / c1) / (jnp.sqrt(vn / c2) + ADAM_EPS) + ADAM_WD * w_ref[...])

    blk = pl.BlockSpec((tr, C), lambda i: (i, 0))
    return pl.pallas_call(
        body, name=name, grid=(R // tr,),
        in_specs=[pl.BlockSpec((P, tr, C), lambda i: (0, i, 0)), blk, blk, blk] + [ANY] * len(after),
        out_specs=[blk, blk, blk, blk],
        out_shape=[S((R, C), F32)] * 4,
        compiler_params=_cp("parallel"),
    )(parts, w, m, v, *after)


def _adamw_layer(parts, w3, m3, v3, layer, prev, name, after=()):
    P, R, C = parts.shape
    NL = w3.shape[0]
    tr = _row_tile(R, C)
    c1 = 1.0 - ADAM_B1 ** ADAM_STEP
    c2 = 1.0 - ADAM_B2 ** ADAM_STEP
    n_prev = 0 if prev is None else len(prev)

    def body(p_ref, w_ref, m_ref, v_ref, *rest):
        g_ref, d_ref, mo_ref, vo_ref = rest[n_prev + len(after):]
        g = p_ref[0].astype(F32)
        for i in range(1, P):
            g = g + p_ref[i].astype(F32)
        mn = ADAM_B1 * m_ref[...] + (1.0 - ADAM_B1) * g
        vn = ADAM_B2 * v_ref[...] + (1.0 - ADAM_B2) * (g * g)
        g_ref[...] = g
        mo_ref[...] = mn
        vo_ref[...] = vn
        d_ref[...] = (-ADAM_LR) * ((mn / c1) / (jnp.sqrt(vn / c2) + ADAM_EPS) + ADAM_WD * w_ref[...])

    blk = pl.BlockSpec((None, tr, C), lambda i: (layer, i, 0))
    return pl.pallas_call(
        body, name=name, grid=(R // tr,),
        in_specs=[pl.BlockSpec((P, tr, C), lambda i: (0, i, 0)), blk, blk, blk] + [ANY] * (n_prev + len(after)),
        out_specs=[blk, blk, blk, blk],
        out_shape=[S((NL, R, C), F32)] * 4,
        input_output_aliases={4 + k: k for k in range(n_prev)},
        compiler_params=_cp("parallel"),
    )(parts, w3, m3, v3, *(prev or ()), *after)


VMEM_WHOLE = pl.BlockSpec(memory_space=pltpu.VMEM)
LANES = 128


def _pack_vectors(vectors, starts, rows, name):
    def body(*refs):
        o_ref = refs[-1]
        o_ref[...] = jnp.zeros_like(o_ref)
        for v_ref, r0 in zip(refs[:-1], starts):
            for j in range(v_ref.shape[1] // LANES):
                o_ref[r0 + j:r0 + j + 1, :] = v_ref[:, j * LANES:(j + 1) * LANES]

    return pl.pallas_call(body, name=name, in_specs=[VMEM_WHOLE] * len(vectors), out_specs=VMEM_WHOLE,
                          out_shape=S((rows, LANES), F32))(*vectors)


def _adamw_vectors(g_pack, params, keep_rows, name, after=()):
    n = len(params)
    P = g_pack.shape[0]
    c1 = 1.0 - ADAM_B1 ** ADAM_STEP
    c2 = 1.0 - ADAM_B2 ** ADAM_STEP

    def body(g_ref, *refs):
        ins, outs = refs[:3 * n], refs[3 * n + len(after):]
        gs = g_ref[0]
        for i in range(1, P):
            gs = gs + g_ref[i]
        for pi, (_, _, _, slots) in enumerate(params):
            w_ref, m_ref, v_ref = ins[3 * pi:3 * pi + 3]
            g_out, d_out, m_out, v_out = outs[4 * pi:4 * pi + 4]
            for idx, row in slots:
                g = gs[row:row + 1, :]
                mn = ADAM_B1 * m_ref[idx] + (1.0 - ADAM_B1) * g
                vn = ADAM_B2 * v_ref[idx] + (1.0 - ADAM_B2) * (g * g)
                g_out[idx] = g
                m_out[idx] = mn
                v_out[idx] = vn
                d_out[idx] = (-ADAM_LR) * ((mn / c1) / (jnp.sqrt(vn / c2) + ADAM_EPS) + ADAM_WD * w_ref[idx])
        outs[-1][...] = jnp.concatenate([gs[r:r + 1, :] for r in keep_rows], axis=0)

    flat = [a for w, m, v, _ in params for a in (w, m, v)]
    out_shape = [S(w.shape, F32) for w, _, _, _ in params for _ in range(4)] + [S((len(keep_rows), LANES), F32)]
    outs = pl.pallas_call(
        body, name=name,
        in_specs=[VMEM_WHOLE] * (1 + len(flat)) + [ANY] * len(after),
        out_specs=[VMEM_WHOLE] * len(out_shape), out_shape=out_shape,
    )(g_pack, *flat, *after)
    return [outs[4 * i:4 * i + 4] for i in range(n)], outs[-1]


def _adamw_nd(parts, w, m, v, name, after=()):
    shp = w.shape
    C = shp[-1]
    outs = _adamw(parts.reshape(parts.shape[0], -1, C), w.reshape(-1, C), m.reshape(-1, C), v.reshape(-1, C), name, after)
    return [o.reshape(shp) for o in outs]


TILE_ROWS = 8


REP_SMALL_ROWS = 128
REP_GRAD_STARTS = (0, 8, 16, 24, 32, 40, 48, 56, 64)
REP_SMALL_STARTS = (0, 16, 32, 40, 48, 56)
REP_LOSS_ROW = 64


def _small_pack(cw, ba, bx, lam):
    pad8 = lambda a: jnp.pad(a, ((0, TILE_ROWS - a.shape[0]), (0, 0)))
    return jnp.concatenate([pad8(cw[0, :, 0, :]), pad8(ba[0]), pad8(bx[0]), pad8(lam[0]),
                            jnp.zeros((PV_ROWS - PV_CONV_B, LRU_BW), F32)], axis=0)


def kernel(x, norm_mix_g, norm_mlp_g, rg_w_in, rg_conv_w, rg_conv_b, rg_w_a, rg_b_a, rg_w_x, rg_b_x, rg_lam, rg_w_out, at_w_qkv, at_q_g, at_k_g, at_w_o, mlp_w_up, mlp_w_down, final_g, loss_target, m_norm_mix_g, m_norm_mlp_g, m_rg_w_in, m_rg_conv_w, m_rg_conv_b, m_rg_w_a, m_rg_b_a, m_rg_w_x, m_rg_b_x, m_rg_lam, m_rg_w_out, m_at_w_qkv, m_at_q_g, m_at_k_g, m_at_w_o, m_mlp_w_up, m_mlp_w_down, m_final_g, v_norm_mix_g, v_norm_mlp_g, v_rg_w_in, v_rg_conv_w, v_rg_conv_b, v_rg_w_a, v_rg_b_a, v_rg_w_x, v_rg_b_x, v_rg_lam, v_rg_w_out, v_at_w_qkv, v_at_q_g, v_at_k_g, v_at_w_o, v_mlp_w_up, v_mlp_w_down, v_final_g):
    D = x.shape[-1]
    bf = lambda a: a.astype(BF16)

    sp_w = _small_pack(rg_conv_w, rg_b_a, rg_b_x, rg_lam)
    started, _ = _exchange_start(
        [[bf(rg_w_in[0]), sp_w], [bf(rg_w_out[0])], [bf(mlp_w_up[0]), bf(mlp_w_down[0])],
         [bf(at_w_qkv[0]).T, bf(at_w_o[0])], [bf(mlp_w_up[1]), bf(mlp_w_down[1])]],
        ["near", "near", "near", "gather", "gather"], "gather_start")
    gathers = dict(zip((0, 1, 4, 2, 3), started))
    forwards = {}

    def fetch(stage, after):
        after = tuple(after)
        if stage == 0:
            got = _exchange_wait(gathers[0], after, "near", "gather_wait0")
            g_in, g_sp = _forward_wait(_forward_start([got], "forward_start0")[0], (), "forward_wait0")
            pvec = g_sp.transpose(1, 0, 2).reshape(PV_ROWS, D)
            pvec = jnp.concatenate([pvec[:PV_CONV_B], jnp.broadcast_to(rg_conv_b, (PV_ROWS - PV_CONV_B, D))], axis=0)
            return dict(w_in=g_in, pvec=pvec)
        if stage == 1:
            near = [_exchange_wait(gathers[s], after, "near", "gather_wait%d" % s) for s in (1, 4)]
            f_out, forwards[4] = _forward_start(near, "forward_start1")
            g_out, = _forward_wait(f_out, (), "forward_wait1")
            return dict(w_out=g_out.reshape(D, D))
        if stage == 4:
            g_up0, g_dn0 = _forward_wait(forwards[4], after, "forward_wait4")
            return dict(w_up0=g_up0, w_down0=g_dn0.reshape(-1, D))
        got = _exchange_wait(gathers[stage], after, "gather", "gather_wait%d" % stage)
        if stage == 2:
            return dict(w_qkv=got[0].reshape(-1, QKV_NB, D), w_o=got[1].reshape(D, D))
        return dict(w_up1=got[0], w_down1=got[1].reshape(-1, D))

    scatters = {}

    def send(stage, g):
        if stage == 3:
            arrs = [g["w_up1"], g["w_down1"].reshape(N_DEV, -1, D)]
        elif stage == 2:
            arrs = [g["w_qkv"].reshape(N_DEV, -1, D), g["w_o"].reshape(N_DEV, -1, D)]
        elif stage == 1:
            arrs = [g["w_up0"], g["w_down0"].reshape(N_DEV, -1, D)]
        elif stage == 4:
            arrs = [g["w_out"].reshape(N_DEV, -1, D), g["pvec"].reshape(PV_ROWS, N_DEV, LRU_BW).transpose(1, 0, 2),
                    bf(g["wa"]).reshape(N_DEV, -1, LANES), bf(g["wx"]).reshape(N_DEV, -1, LANES)]
        elif stage == 0:
            arrs = [g["w_in"]]
        else:
            small = _pack_vectors(g["g_mix"] + g["g_mlp"] + [g["g_fin"], g["conv_b"], g["qg"], g["kg"], g["loss"]],
                                  REP_GRAD_STARTS, REP_SMALL_ROWS, "pack_rep_small")
            arrs = [small.reshape(N_DEV, -1, LANES)]
        (group,), token = _exchange_start([arrs], ["scatter"], "scatter_start%d" % (stage % 6))
        scatters[stage] = (group, token)
        return (token,)

    w = dict(g_mix=norm_mix_g, g_mlp=norm_mlp_g, g_fin=final_g[None], qg=at_q_g, kg=at_k_g,
             wa=bf(rg_w_a[0]), wx=bf(rg_w_x[0]))
    early = {}

    def early_adam(after):
        r_up1, r_dn1 = _exchange_wait(scatters[3][0], tuple(after), "scatter", "scatter_wait3")
        r_qkv, r_o = _exchange_wait(scatters[2][0], (r_up1,), "scatter", "scatter_wait2")
        jobs = [(r_up1, mlp_w_up, m_mlp_w_up, v_mlp_w_up, 1), (r_dn1, mlp_w_down, m_mlp_w_down, v_mlp_w_down, 1),
                (r_o, at_w_o, m_at_w_o, v_at_w_o, 0)]
        return jobs, lambda outs: early.update(up=outs[0], dn=outs[1], o=outs[2], r_qkv=r_qkv)

    grad_x = _local_step(x, loss_target, w, fetch, send, early_adam)

    res = {}
    up, dn = early["up"], early["dn"]
    r_out, r_sp, r_wa, r_wx = _exchange_wait(scatters[4][0], (scatters[-1][1],), "scatter", "scatter_wait4")
    wa_part = _sum_parts(r_wa, "reduce_w_a")
    wx_part = _sum_parts(r_wx, "reduce_w_x", (wa_part,))
    (rep_gather,), rep_token = _exchange_start([[wa_part, wx_part]], ["gather"], "rep_gather_start")
    tr = lambda a: a[0].T
    qkv_t = _adamw_nd(early["r_qkv"], tr(at_w_qkv), tr(m_at_w_qkv), tr(v_at_w_qkv), "adam_at_w_qkv", (rep_token,))
    res["at_w_qkv"] = [o.T[None] for o in qkv_t]
    res["at_w_o"] = early["o"]
    r_up0, r_dn0 = _exchange_wait(scatters[1][0], (qkv_t[1],), "scatter", "scatter_wait1")
    res["mlp_w_up"] = _adamw_layer(r_up0, mlp_w_up, m_mlp_w_up, v_mlp_w_up, 0, up, "adam_mlp_w_up0")
    res["mlp_w_down"] = _adamw_layer(r_dn0, mlp_w_down, m_mlp_w_down, v_mlp_w_down, 0, dn, "adam_mlp_w_down0",
                                     after=(res["mlp_w_up"][1],))
    r_in, = _exchange_wait(scatters[0][0], (res["mlp_w_down"][1],), "scatter", "scatter_wait0")
    res["rg_w_in"] = _adamw_nd(r_in[:, None], rg_w_in, m_rg_w_in, v_rg_w_in, "adam_rg_w_in")
    res["rg_w_out"] = _adamw_nd(r_out[:, None], rg_w_out, m_rg_w_out, v_rg_w_out, "adam_rg_w_out", (res["rg_w_in"][1],))
    whole, lane = slice(None), slice(0, 1)
    two_rows = lambda r0: [((0, slice(d, d + 1), whole), r0 + d) for d in range(2)]
    (res["rg_conv_w"], res["rg_b_a"], res["rg_b_x"], res["rg_lam"]), _ = _adamw_vectors(
        r_sp, [(rg_conv_w, m_rg_conv_w, v_rg_conv_w, [((0, t, lane, whole), PV_CONV_W + t) for t in range(CONV_W)]),
               (rg_b_a, m_rg_b_a, v_rg_b_a, two_rows(PV_B_A)), (rg_b_x, m_rg_b_x, v_rg_b_x, two_rows(PV_B_X)),
               (rg_lam, m_rg_lam, v_rg_lam, two_rows(PV_LAM))], [0], "adam_small", (res["rg_w_out"][1],))

    r_small, = _exchange_wait(scatters[-1][0], (res["rg_lam"][1],), "scatter", "scatter_wait5")
    small_sum, = _all_gather([_sum_parts(r_small, "reduce_rep_small")], "gather_replicated")
    wa_sum, wx_sum = _exchange_wait(rep_gather, (small_sum,), "gather", "rep_gather_wait")
    rows = lambda a: a.reshape(-1, LANES)
    wa_res = _adamw(wa_sum.reshape(1, -1, LANES), rows(rg_w_a), rows(m_rg_w_a), rows(v_rg_w_a), "adam_rg_w_a")
    wx_res = _adamw(wx_sum.reshape(1, -1, LANES), rows(rg_w_x), rows(m_rg_w_x), rows(v_rg_w_x), "adam_rg_w_x", (wa_res[1],))
    res["rg_w_a"] = [o.reshape(rg_w_a.shape) for o in wa_res]
    res["rg_w_x"] = [o.reshape(rg_w_x.shape) for o in wx_res]

    def vec_slots(a, r0):
        per = a.shape[1] // LANES
        return [((slice(l, l + 1), slice(LANES * j, LANES * (j + 1))), r0 + l * per + j)
                for l in range(a.shape[0]) for j in range(per)]

    fin = [final_g[None], m_final_g[None], v_final_g[None]]
    vecs = [(norm_mix_g, m_norm_mix_g, v_norm_mix_g), (norm_mlp_g, m_norm_mlp_g, v_norm_mlp_g), fin,
            (rg_conv_b, m_rg_conv_b, v_rg_conv_b), (at_q_g, m_at_q_g, v_at_q_g), (at_k_g, m_at_k_g, v_at_k_g)]
    outs, kept = _adamw_vectors(
        small_sum.reshape(1, -1, LANES),
        [(wv, mv, vv, vec_slots(wv, r0)) for (wv, mv, vv), r0 in zip(vecs, REP_SMALL_STARTS)], [REP_LOSS_ROW],
        "adam_rep_small", (wx_res[1],))
    for nm, o in zip(["norm_mix_g", "norm_mlp_g", "final_g", "rg_conv_b", "at_q_g", "at_k_g"], outs):
        res[nm] = [a[0] for a in o] if nm == "final_g" else o
    loss = kept[0, 0]

    order = ["norm_mix_g", "norm_mlp_g", "rg_w_in", "rg_conv_w", "rg_conv_b", "rg_w_a", "rg_b_a", "rg_w_x", "rg_b_x",
             "rg_lam", "rg_w_out", "at_w_qkv", "at_q_g", "at_k_g", "at_w_o", "mlp_w_up", "mlp_w_down", "final_g"]
    return (loss, grad_x, *[res[nm][k] for k in range(4) for nm in order])
```

```python
import math

import jax
import jax.numpy as jnp
from jax import lax
from jax.experimental import pallas as pl
from jax.experimental.pallas import tpu as pltpu

F32 = jnp.float32
BF16 = jnp.bfloat16
S = jax.ShapeDtypeStruct

EPS = 1e-6
HEAD_DIM = 128
N_KV = 2
GRID_W = 64
ROPE_THETA = 10000.0
LRU_BW = 128
RG_C = 8.0
CONV_W = 4
N_DEV = 8
N_SEG = 8
SCAN_UNROLL = 8
TN_STEP_COLS = 512
PROJ_TM = 1024
STREAMED = pl.Buffered(2)
RES_TM = 512
RES_BUFFERS = 3
PROJ_CHAINS = 2
MLP_TM = 512
VMEM_LIMIT_V7X = 56 * 1024 * 1024
SOFTMAX_SCALE = 1.0 / math.sqrt(HEAD_DIM)
GELU_K = math.sqrt(2.0 / math.pi)
GELU_C = 0.044715

ADAM_LR = 0.001
ADAM_B1 = 0.9
ADAM_B2 = 0.999
ADAM_EPS = 1e-08
ADAM_WD = 0.01
ADAM_STEP = 10

NT = (((1,), (1,)), ((), ()))
TN = (((0,), (0,)), ((), ()))


def _cp(*sem):
    return pltpu.CompilerParams(dimension_semantics=sem, vmem_limit_bytes=VMEM_LIMIT_V7X)


def _rms_r(xv):
    return lax.rsqrt(jnp.mean(xv * xv, axis=-1, keepdims=True) + EPS)


def _rms_bwd(dh, xv, g):
    r = _rms_r(xv)
    xh = xv * r
    dg = jnp.sum(dh * xh, axis=0, keepdims=True)
    dxh = dh * g
    dx = r * (dxh - xh * jnp.mean(dxh * xh, axis=-1, keepdims=True))
    return dx, dg


def _dot(a, b):
    return jnp.dot(a, b, preferred_element_type=F32)


def _dot_nt(a, b):
    return lax.dot_general(a, b, NT, preferred_element_type=F32)


def _dot_tn(a, b):
    return lax.dot_general(a, b, TN, preferred_element_type=F32)


def _norm_matmul(x, g, wblk, name, out_dtype=F32):
    T, D = x.shape
    NB, _, nb = wblk.shape
    tm = min(T, PROJ_TM)

    def body(x_ref, g_ref, w_ref, o_ref, h_ref):
        for c in range(PROJ_CHAINS):
            rows = slice(c * tm // PROJ_CHAINS, (c + 1) * tm // PROJ_CHAINS)
            xv = x_ref[rows, :]
            hb = (xv * _rms_r(xv) * g_ref[...]).astype(BF16)
            h_ref[rows, :] = hb
            for q in range(NB):
                o_ref[rows, q * nb:(q + 1) * nb] = _dot(hb, w_ref[q]).astype(o_ref.dtype)

    return pl.pallas_call(
        body, name=name, grid=(T // tm,),
        in_specs=[pl.BlockSpec((tm, D), lambda i: (i, 0), pipeline_mode=STREAMED),
                  pl.BlockSpec((1, D), lambda i: (0, 0)),
                  pl.BlockSpec(wblk.shape, lambda i: (0, 0, 0))],
        out_specs=[pl.BlockSpec((tm, NB * nb), lambda i: (i, 0)),
                   pl.BlockSpec((tm, D), lambda i: (i, 0))],
        out_shape=[S((T, NB * nb), out_dtype), S((T, D), BF16)],
        compiler_params=_cp("parallel"),
    )(x, g, wblk)


def _matmul_res(a, w, res, name):
    T, K = a.shape
    N = w.shape[1]
    tm = min(T, RES_TM)
    n = T // tm
    nbuf = min(RES_BUFFERS, n)

    def body(a_hbm, w_ref, r_hbm, o_ref, a_buf, r_buf, sem):
        i = pl.program_id(0)

        def reads(step, slot):
            rows = pl.ds(pl.multiple_of(step * tm, tm), tm)
            return (pltpu.make_async_copy(a_hbm.at[rows, :], a_buf.at[slot], sem.at[0, slot]),
                    pltpu.make_async_copy(r_hbm.at[rows, :], r_buf.at[slot], sem.at[1, slot]))

        @pl.when(i == 0)
        def _():
            for s in range(nbuf):
                for cp in reads(s, s):
                    cp.start()

        slot = i % nbuf
        for cp in reads(i, slot):
            cp.wait()
        o_ref[...] = r_buf[slot] + _dot(a_buf[slot], w_ref[...])

        @pl.when(i + nbuf < n)
        def _():
            for cp in reads(i + nbuf, slot):
                cp.start()

    return pl.pallas_call(
        body, name=name, grid=(n,),
        in_specs=[ANY, pl.BlockSpec((K, N), lambda i: (0, 0)), ANY],
        out_specs=pl.BlockSpec((tm, N), lambda i: (i, 0)),
        out_shape=S((T, N), F32),
        scratch_shapes=[pltpu.VMEM((nbuf, tm, K), a.dtype), pltpu.VMEM((nbuf, tm, N), F32),
                        pltpu.SemaphoreType.DMA((2, nbuf))],
        compiler_params=_cp("arbitrary"),
    )(a, w, res)


def _matmul_nt(a, w, name, out_dtype, after=()):
    T, N = a.shape
    K = w.shape[0]
    tm = min(T, RES_TM)
    n = T // tm
    nbuf = min(RES_BUFFERS, n)

    def body(a_hbm, w_ref, *rest):
        o_ref, ab_ref, a_buf, sem = rest[len(after):]
        i = pl.program_id(0)

        def read(step, slot):
            rows = pl.ds(pl.multiple_of(step * tm, tm), tm)
            return pltpu.make_async_copy(a_hbm.at[rows, :], a_buf.at[slot], sem.at[slot])

        @pl.when(i == 0)
        def _():
            for s in range(nbuf):
                read(s, s).start()

        slot = i % nbuf
        read(i, slot).wait()
        ab = a_buf[slot].astype(BF16)
        ab_ref[...] = ab
        o_ref[...] = _dot_nt(ab, w_ref[...]).astype(o_ref.dtype)

        @pl.when(i + nbuf < n)
        def _():
            read(i + nbuf, slot).start()

    return pl.pallas_call(
        body, name=name, grid=(n,),
        in_specs=[ANY, pl.BlockSpec((K, N), lambda i: (0, 0))] + [ANY] * len(after),
        out_specs=[pl.BlockSpec((tm, K), lambda i: (i, 0)),
                   pl.BlockSpec((tm, N), lambda i: (i, 0))],
        out_shape=[S((T, K), out_dtype), S((T, N), BF16)],
        scratch_shapes=[pltpu.VMEM((nbuf, tm, N), a.dtype), pltpu.SemaphoreType.DMA((nbuf,))],
        compiler_params=_cp("arbitrary"),
    )(a, w, *after)


def _matmul_tn(a, b3, nb, name, blocked, after=()):
    T, M = a.shape
    SB, _, N = b3.shape
    per = N // nb
    NB = SB * per
    tk = min(T, 1024)
    nk = T // tk
    jb = max(1, TN_STEP_COLS // nb) if blocked else 1
    assert per % jb == 0
    if blocked:
        out_spec, out_shape = pl.BlockSpec((jb, M, nb), lambda j: (j, 0, 0)), S((NB, M, nb), BF16)
    else:
        assert SB == 1
        out_spec, out_shape = pl.BlockSpec((M, nb), lambda j: (0, j)), S((M, N), BF16)

    def body(a_ref, b_ref, *rest):
        o_ref = rest[-1]
        for q in range(jb):
            acc = None
            for k in range(nk):
                rows = slice(k * tk, (k + 1) * tk)
                part = _dot_tn(a_ref[rows, :], b_ref[rows, q * nb:(q + 1) * nb])
                acc = part if acc is None else acc + part
            if blocked:
                o_ref[q] = acc.astype(BF16)
            else:
                o_ref[...] = acc.astype(BF16)

    return pl.pallas_call(
        body, name=name, grid=(NB // jb,),
        in_specs=[pl.BlockSpec((T, M), lambda j: (0, 0), pipeline_mode=pl.Buffered(1)),
                  pl.BlockSpec((None, T, jb * nb), lambda j: ((j * jb) // per, 0, ((j * jb) % per) // jb))]
        + [pl.BlockSpec(memory_space=pl.ANY)] * len(after),
        out_specs=out_spec,
        out_shape=out_shape,
        compiler_params=_cp("parallel"),
    )(a, b3, *after)


def _nt_normbwd(dz3, wblk, x, g, dres, name, after=(), transposed=False):
    T, D = x.shape
    NB, nb = wblk.shape[0], wblk.shape[1 if transposed else 2]
    mm = _dot if transposed else _dot_nt
    SB, _, N = dz3.shape
    per = N // nb
    tm = min(T, PROJ_TM)

    def body(dz_ref, w_ref, x_ref, g_ref, dr_ref, *rest):
        dx_ref, dg_ref = rest[len(after):]

        @pl.when(pl.program_id(0) == 0)
        def _():
            dg_ref[...] = jnp.zeros_like(dg_ref)

        for c in range(PROJ_CHAINS):
            rows = slice(c * tm // PROJ_CHAINS, (c + 1) * tm // PROJ_CHAINS)
            dh = None
            for q in range(NB):
                cols = slice((q % per) * nb, (q % per + 1) * nb)
                part = mm(dz_ref[q // per, rows, cols], w_ref[q])
                dh = part if dh is None else dh + part
            dx, dg = _rms_bwd(dh, x_ref[rows, :], g_ref[...])
            dx_ref[rows, :] = dr_ref[rows, :] + dx
            dg_ref[...] += dg

    return pl.pallas_call(
        body, name=name, grid=(T // tm,),
        in_specs=[pl.BlockSpec((SB, tm, N), lambda i: (0, i, 0), pipeline_mode=STREAMED),
                  pl.BlockSpec(wblk.shape, lambda i: (0, 0, 0)),
                  pl.BlockSpec((tm, D), lambda i: (i, 0), pipeline_mode=STREAMED),
                  pl.BlockSpec((1, D), lambda i: (0, 0)),
                  pl.BlockSpec((tm, D), lambda i: (i, 0), pipeline_mode=STREAMED)] + [pl.BlockSpec(memory_space=pl.ANY)] * len(after),
        out_specs=[pl.BlockSpec((tm, D), lambda i: (i, 0)),
                   pl.BlockSpec((1, D), lambda i: (0, 0))],
        out_shape=[S((T, D), F32), S((1, D), F32)],
        compiler_params=_cp("arbitrary"),
    )(dz3, wblk, x, g, dres, *after)


def _loss_head(xv, tv, gv, D):
    err = xv * _rms_r(xv) * gv - tv
    e2 = jnp.sum(jnp.sum(err * err, axis=-1, keepdims=True), axis=0, keepdims=True)
    dx, dg = _rms_bwd(err * (1.0 / D), xv, gv)
    return (0.5 / D) * e2, dx, dg


def _mlp_fwd(x, g, wup, wdown, name, head=None):
    T, D = x.shape
    NB, _, fb = wup.shape
    tm = min(T, MLP_TM)
    n_head = 0 if head is None else 2

    def body(x_ref, g_ref, wu_ref, wd_ref, *rest):
        xo_ref, a_ref, h_ref = rest[n_head:n_head + 3]
        xv = x_ref[...]
        hb = (xv * _rms_r(xv) * g_ref[...]).astype(BF16)
        h_ref[...] = hb
        acc = xv
        for j in range(NB):
            a = _dot(hb, wu_ref[j])
            a_ref[:, j * fb:(j + 1) * fb] = a.astype(BF16)
            u = jnp.maximum(a, 0.0)
            acc = acc + _dot((u * u).astype(BF16), wd_ref[j * fb:(j + 1) * fb, :])

        if head is None:
            xo_ref[...] = acc
        else:
            t_ref, gf_ref = rest[:2]
            loss_ref, dgf_ref = rest[n_head + 3:n_head + 5]

            @pl.when(pl.program_id(0) == 0)
            def _():
                loss_ref[...] = jnp.zeros_like(loss_ref)
                dgf_ref[...] = jnp.zeros_like(dgf_ref)

            e2, dx, dg = _loss_head(acc, t_ref[...], gf_ref[...], D)
            xo_ref[...] = dx
            loss_ref[...] += e2
            dgf_ref[...] += dg

    row = pl.BlockSpec((tm, D), lambda i: (i, 0))
    vec = pl.BlockSpec((1, D), lambda i: (0, 0))
    once = pl.Buffered(1)
    in_specs = [row, vec, pl.BlockSpec((NB, D, fb), lambda i: (0, 0, 0), pipeline_mode=once),
                pl.BlockSpec((NB * fb, D), lambda i: (0, 0), pipeline_mode=once)]
    out_specs = [row, pl.BlockSpec((tm, NB * fb), lambda i: (i, 0)), row]
    out_shape = [S((T, D), F32), S((T, NB * fb), BF16), S((T, D), BF16)]
    if head is not None:
        in_specs += [row, vec]
        out_specs += [pl.BlockSpec((1, 128), lambda i: (0, 0)), vec]
        out_shape += [S((1, 128), F32), S((1, D), F32)]
    return pl.pallas_call(
        body, name=name, grid=(T // tm,),
        in_specs=in_specs, out_specs=out_specs, out_shape=out_shape,
        compiler_params=_cp("parallel" if head is None else "arbitrary"),
    )(x, g, wup, wdown, *(head or ()))


def _mlp_bwd_dx(x, dout, a, g, wup, wdown, name):
    T, D = x.shape
    NB, _, fb = wup.shape
    tm = min(T, MLP_TM)

    def body(x_ref, do_ref, a_ref, g_ref, wu_ref, wd_ref, dx_ref, da_ref, dob_ref, dg_ref):
        @pl.when(pl.program_id(0) == 0)
        def _():
            dg_ref[...] = jnp.zeros_like(dg_ref)

        dov = do_ref[...]
        dob = dov.astype(BF16)
        dob_ref[...] = dob
        dh = None
        for j in range(NB):
            cols = slice(j * fb, (j + 1) * fb)
            du2 = _dot_nt(dob, wd_ref[cols, :])
            u = jnp.maximum(a_ref[:, cols].astype(F32), 0.0)
            da = (du2 * (2.0 * u)).astype(BF16)
            da_ref[:, cols] = da
            part = _dot_nt(da, wu_ref[j])
            dh = part if dh is None else dh + part
        dx, dg = _rms_bwd(dh, x_ref[...], g_ref[...])
        dx_ref[...] = dov + dx
        dg_ref[...] += dg

    row = pl.BlockSpec((tm, D), lambda i: (i, 0))
    wide = pl.BlockSpec((tm, NB * fb), lambda i: (i, 0))
    vec = pl.BlockSpec((1, D), lambda i: (0, 0))
    once = pl.Buffered(1)
    return pl.pallas_call(
        body, name=name, grid=(T // tm,),
        in_specs=[row, row, wide, vec, pl.BlockSpec((NB, D, fb), lambda i: (0, 0, 0), pipeline_mode=once),
                  pl.BlockSpec((NB * fb, D), lambda i: (0, 0), pipeline_mode=once)],
        out_specs=[row, wide, row, vec],
        out_shape=[S((T, D), F32), S((T, NB * fb), BF16), S((T, D), BF16), S((1, D), F32)],
        compiler_params=_cp("arbitrary"),
    )(x, dout, a, g, wup, wdown)


def _adam_update(g, w, m, v):
    mn = ADAM_B1 * m + (1.0 - ADAM_B1) * g
    vn = ADAM_B2 * v + (1.0 - ADAM_B2) * (g * g)
    c1 = 1.0 - ADAM_B1 ** ADAM_STEP
    c2 = 1.0 - ADAM_B2 ** ADAM_STEP
    return (-ADAM_LR) * ((mn / c1) / (jnp.sqrt(vn / c2) + ADAM_EPS) + ADAM_WD * w), mn, vn


def _mlp_bwd_dw(h, da, a, dob, fb, name, adam_jobs=()):
    T, D = h.shape
    F = a.shape[1]
    NB = F // fb
    tk = min(T, 1024)
    nk = T // tk
    nj = len(adam_jobs)

    def body(h_ref, da_ref, a_ref, dob_ref, *rest):
        dwu_ref, dwd_ref = rest[4 * nj:4 * nj + 2]
        au = ad = None
        for k in range(nk):
            rows = slice(k * tk, (k + 1) * tk)
            pu = _dot_tn(h_ref[rows, :], da_ref[rows, :])
            u = jnp.maximum(a_ref[rows, :].astype(F32), 0.0)
            pd = _dot_tn((u * u).astype(BF16), dob_ref[rows, :])
            au, ad = (pu, pd) if au is None else (au + pu, ad + pd)
        dwu_ref[...] = au.astype(BF16)
        dwd_ref[...] = ad.astype(BF16)
        for q in range(nj):
            p_ref, w_ref, m_ref, v_ref = rest[4 * q:4 * q + 4]
            g_out, d_out, m_out, v_out = rest[4 * nj + 2 + 4 * q:4 * nj + 6 + 4 * q]
            g = p_ref[0].astype(F32)
            for i in range(1, p_ref.shape[0]):
                g = g + p_ref[i].astype(F32)
            d, mn, vn = _adam_update(g, w_ref[...], m_ref[...], v_ref[...])
            g_out[...], d_out[...], m_out[...], v_out[...] = g, d, mn, vn

    once = pl.Buffered(1)
    in_specs = [pl.BlockSpec((T, D), lambda j: (0, 0), pipeline_mode=once),
                pl.BlockSpec((T, fb), lambda j: (0, j)),
                pl.BlockSpec((T, fb), lambda j: (0, j)),
                pl.BlockSpec((T, D), lambda j: (0, 0), pipeline_mode=once)]
    out_specs = [pl.BlockSpec((None, D, fb), lambda j: (j, 0, 0)), pl.BlockSpec((fb, D), lambda j: (j, 0))]
    out_shape = [S((NB, D, fb), BF16), S((F, D), BF16)]
    operands = [h, da, a, dob]
    for parts, w3, m3, v3, layer in adam_jobs:
        P, R, C = parts.shape
        blk = pl.BlockSpec((None, R // NB, C), lambda j, layer=layer: (layer, j, 0))
        in_specs += [pl.BlockSpec((P, R // NB, C), lambda j: (0, j, 0)), blk, blk, blk]
        out_specs += [blk] * 4
        out_shape += [S(w3.shape, F32)] * 4
        operands += [parts, w3, m3, v3]
    outs = pl.pallas_call(
        body, name=name, grid=(NB,),
        in_specs=in_specs, out_specs=out_specs, out_shape=out_shape,
        compiler_params=_cp("parallel"),
    )(*operands)
    return outs[0], outs[1], [outs[2 + 4 * q:6 + 4 * q] for q in range(nj)]


def _rope_tables(L):
    nf = HEAD_DIM // 4
    t = jnp.arange(L, dtype=jnp.int32)
    row = (t // GRID_W).astype(F32)
    col = (t % GRID_W).astype(F32)
    inv = ROPE_THETA ** (-jnp.arange(nf, dtype=F32) / nf)
    ar = row[:, None] * inv
    ac = col[:, None] * inv
    cos = jnp.concatenate([jnp.cos(ar), jnp.cos(ar), jnp.cos(ac), jnp.cos(ac)], axis=-1)
    sin = jnp.concatenate([-jnp.sin(ar), jnp.sin(ar), -jnp.sin(ac), jnp.sin(ac)], axis=-1)
    return cos, sin


def _swap32(x):
    lane = lax.broadcasted_iota(jnp.int32, x.shape, 1)
    up = pltpu.roll(x, HEAD_DIM - 32, 1)
    down = pltpu.roll(x, 32, 1)
    return jnp.where((lane % 64) < 32, up, down)


def _qkv_proj(x, g, wt_blk, qg, kg, cos, sin, L, name):
    T, D = x.shape
    NB, nb, _ = wt_blk.shape
    W = NB * nb
    nh = W // HEAD_DIM - 2 * N_KV
    tm = min(L, 1024)
    lb = L // tm

    def body(x_ref, g_ref, w_ref, qg_ref, kg_ref, cos_ref, sin_ref, qkv_ref, h_ref, q_ref, k_ref, v_ref):
        for ch in range(PROJ_CHAINS):
            rows = slice(ch * tm // PROJ_CHAINS, (ch + 1) * tm // PROJ_CHAINS)
            xv = x_ref[rows, :]
            hb = (xv * _rms_r(xv) * g_ref[...]).astype(BF16)
            h_ref[rows, :] = hb
            for j in range(NB):
                qkv_ref[rows, j * nb:(j + 1) * nb] = _dot_nt(hb, w_ref[j])
            c = cos_ref[rows, :]
            s = sin_ref[rows, :]
            for h in range(nh + N_KV):
                xh = qkv_ref[rows, h * HEAD_DIM:(h + 1) * HEAD_DIM]
                gv = qg_ref[...] if h < nh else kg_ref[...]
                y = xh * _rms_r(xh) * gv
                y = (y * c + _swap32(y) * s).astype(BF16)
                if h < nh:
                    q_ref[rows, h * HEAD_DIM:(h + 1) * HEAD_DIM] = y
                else:
                    k_ref[rows, (h - nh) * HEAD_DIM:(h - nh + 1) * HEAD_DIM] = y
            v_ref[rows, :] = qkv_ref[rows, (nh + N_KV) * HEAD_DIM:].astype(BF16)

    row = lambda cols: pl.BlockSpec((tm, cols), lambda i: (i, 0))
    vec = lambda cols: pl.BlockSpec((1, cols), lambda i: (0, 0))
    table = pl.BlockSpec((tm, HEAD_DIM), lambda i: (i % lb, 0))
    return pl.pallas_call(
        body, name=name, grid=(T // tm,),
        in_specs=[row(D), vec(D), pl.BlockSpec(wt_blk.shape, lambda i: (0, 0, 0)), vec(HEAD_DIM), vec(HEAD_DIM), table, table],
        out_specs=[row(W), row(D), row(nh * HEAD_DIM), row(N_KV * HEAD_DIM), row(N_KV * HEAD_DIM)],
        out_shape=[S((T, W), F32), S((T, D), BF16), S((T, nh * HEAD_DIM), BF16), S((T, N_KV * HEAD_DIM), BF16),
                   S((T, N_KV * HEAD_DIM), BF16)],
        compiler_params=_cp("parallel"),
    )(x, g, wt_blk, qg, kg, cos, sin)


def _qk_prep_bwd(qkv, dq, dk, dv, qg, kg, cos, sin, L, name):
    T, W = qkv.shape
    nh = W // HEAD_DIM - 2 * N_KV
    tm = min(L, 1024)
    lb = L // tm

    def body(qkv_ref, dq_ref, dk_ref, dv_ref, qg_ref, kg_ref, cos_ref, sin_ref, dz_ref, dqg_ref, dkg_ref):
        @pl.when(pl.program_id(0) == 0)
        def _():
            dqg_ref[...] = jnp.zeros_like(dqg_ref)
            dkg_ref[...] = jnp.zeros_like(dkg_ref)

        c = cos_ref[...]
        s = sin_ref[...]
        for h in range(nh + N_KV):
            cols = slice(h * HEAD_DIM, (h + 1) * HEAD_DIM)
            if h < nh:
                dout, gv, dg_ref = dq_ref[:, cols], qg_ref[...], dqg_ref
            else:
                kc = slice((h - nh) * HEAD_DIM, (h - nh + 1) * HEAD_DIM)
                dout, gv, dg_ref = dk_ref[:, kc], kg_ref[...], dkg_ref
            dy = dout * c - _swap32(dout) * s
            dx, dg = _rms_bwd(dy, qkv_ref[:, cols], gv)
            dg_ref[...] += dg
            dz_ref[:, cols] = dx.astype(BF16)
        dz_ref[:, (nh + N_KV) * HEAD_DIM:] = dv_ref[...].astype(BF16)

    return pl.pallas_call(
        body, name=name, grid=(T // tm,),
        in_specs=[pl.BlockSpec((tm, W), lambda i: (i, 0)),
                  pl.BlockSpec((tm, nh * HEAD_DIM), lambda i: (i, 0)),
                  pl.BlockSpec((tm, N_KV * HEAD_DIM), lambda i: (i, 0)),
                  pl.BlockSpec((tm, N_KV * HEAD_DIM), lambda i: (i, 0)),
                  pl.BlockSpec((1, HEAD_DIM), lambda i: (0, 0)),
                  pl.BlockSpec((1, HEAD_DIM), lambda i: (0, 0)),
                  pl.BlockSpec((tm, HEAD_DIM), lambda i: (i % lb, 0)),
                  pl.BlockSpec((tm, HEAD_DIM), lambda i: (i % lb, 0))],
        out_specs=[pl.BlockSpec((tm, W), lambda i: (i, 0)),
                   pl.BlockSpec((1, HEAD_DIM), lambda i: (0, 0)),
                   pl.BlockSpec((1, HEAD_DIM), lambda i: (0, 0))],
        out_shape=[S((T, W), BF16), S((1, HEAD_DIM), F32), S((1, HEAD_DIM), F32)],
        compiler_params=_cp("arbitrary"),
    )(qkv, dq, dk, dv, qg, kg, cos, sin)


EXP2_SCALE = SOFTMAX_SCALE * math.log2(math.e)
ATTN_SUB = 256
ATTN_TQ = 2048


def _softmax_rows(q, k):
    s = _dot_nt(q, k)
    e = jnp.exp2((s - jnp.max(s, axis=-1, keepdims=True)) * EXP2_SCALE)
    return e, jnp.sum(e, axis=-1, keepdims=True)


def _attn_fwd(q, k, v, L, name):
    T = q.shape[0]
    nh = q.shape[1] // HEAD_DIM
    G = nh // N_KV
    B = T // L
    tq = min(L, ATTN_TQ)
    nq = L // tq
    sub = min(tq, ATTN_SUB)

    def body(q_ref, k_ref, v_ref, o_ref):
        for h in range(tq // sub):
            rows = slice(h * sub, (h + 1) * sub)
            e, l = _softmax_rows(q_ref[rows, :], k_ref[...])
            o_ref[rows, :] = (_dot(e.astype(BF16), v_ref[...]) / l).astype(BF16)

    qspec = pl.BlockSpec((tq, HEAD_DIM), lambda b, kv, g, qi: (b * nq + qi, kv * G + g))
    kspec = pl.BlockSpec((L, HEAD_DIM), lambda b, kv, g, qi: (b, kv))
    return pl.pallas_call(
        body, name=name, grid=(B, N_KV, G, nq),
        in_specs=[qspec, kspec, kspec],
        out_specs=qspec,
        out_shape=S((T, nh * HEAD_DIM), BF16),
        compiler_params=_cp("parallel", "parallel", "parallel", "parallel"),
    )(q, k, v)


def _attn_bwd(q, k, v, do, o, L, name):
    T = q.shape[0]
    nh = q.shape[1] // HEAD_DIM
    G = nh // N_KV
    B = T // L
    tq = min(L, ATTN_TQ)
    nq = L // tq
    sub = min(tq, ATTN_SUB)

    def body(q_ref, k_ref, v_ref, do_ref, o_ref, dq_ref, dk_ref, dv_ref, ds_scr, p_scr):
        first = (pl.program_id(2) == 0) & (pl.program_id(3) == 0)
        last = (pl.program_id(2) == G - 1) & (pl.program_id(3) == nq - 1)

        @pl.when(first)
        def _():
            dk_ref[...] = jnp.zeros_like(dk_ref)
            dv_ref[...] = jnp.zeros_like(dv_ref)

        for h in range(tq // sub):
            rows = slice(h * sub, (h + 1) * sub)
            dov = do_ref[rows, :]
            e, l = _softmax_rows(q_ref[rows, :], k_ref[...])
            p = e * (1.0 / l)
            dsum = jnp.sum(dov.astype(F32) * o_ref[rows, :].astype(F32), axis=-1, keepdims=True)
            ds_scr[rows, :] = (p * (_dot_nt(dov, v_ref[...]) - dsum)).astype(BF16)
            p_scr[rows, :] = p.astype(BF16)
        ds = ds_scr[...]
        dq_ref[...] = _dot(ds, k_ref[...]) * SOFTMAX_SCALE
        dk_ref[...] += _dot_tn(ds, q_ref[...])
        dv_ref[...] += _dot_tn(p_scr[...], do_ref[...])

        @pl.when(last)
        def _():
            dk_ref[...] = dk_ref[...] * SOFTMAX_SCALE

    qspec = pl.BlockSpec((tq, HEAD_DIM), lambda b, kv, g, qi: (b * nq + qi, kv * G + g))
    kspec = pl.BlockSpec((L, HEAD_DIM), lambda b, kv, g, qi: (b, kv))
    return pl.pallas_call(
        body, name=name, grid=(B, N_KV, G, nq),
        in_specs=[qspec, kspec, kspec, qspec, qspec],
        out_specs=[qspec, kspec, kspec],
        out_shape=[S((T, nh * HEAD_DIM), F32), S((T, N_KV * HEAD_DIM), F32), S((T, N_KV * HEAD_DIM), F32)],
        scratch_shapes=[pltpu.VMEM((tq, L), BF16), pltpu.VMEM((tq, L), BF16)],
        compiler_params=_cp("parallel", "parallel", "arbitrary", "arbitrary"),
    )(q, k, v, do, o)


PV_CONV_W = 0
PV_B_A = 8
PV_B_X = 16
PV_LAM = 24
PV_CONV_B = 32
PV_ROWS = 40


def _shift_rows(x, k):
    if k == 0:
        return x
    L = x.shape[0]
    n = N_SEG * abs(k)
    seg = lax.broadcasted_iota(jnp.int32, (n, x.shape[1]), 0) % N_SEG
    if k > 0:
        edge = jnp.where(seg == 0, 0.0, pltpu.roll(x[L - n:], 1, 0))
        return jnp.concatenate([edge, x[:L - n]], axis=0)
    edge = jnp.where(seg == N_SEG - 1, 0.0, pltpu.roll(x[:n], n - 1, 0))
    return jnp.concatenate([x[n:], edge], axis=0)


def _conv_taps(rec, pv):
    c = pv[PV_CONV_B:PV_CONV_B + 1]
    for j in range(CONV_W):
        c = c + pv[PV_CONV_W + j:PV_CONV_W + j + 1] * _shift_rows(rec, 2 - j)
    return c


def _sigmoid(x):
    return 0.5 * jnp.tanh(0.5 * x) + 0.5


EXPM1_SERIES_BELOW = 0.03


def _rg_gates(c, cbf, wa, wx, ba, bx, lam):
    r = _sigmoid(_dot(cbf, wa) + ba)
    i = _sigmoid(_dot(cbf, wx) + bx)
    sp = jnp.maximum(-lam, 0.0) + jnp.log1p(jnp.exp(-jnp.abs(lam)))
    la = r * ((-RG_C) * sp)
    a = jnp.exp(la)
    a2 = a * a
    x = la + la
    series = -(x * ((x * (1.0 / 6.0) + 0.5) * x + 1.0))
    om = jnp.where(x > -EXPM1_SERIES_BELOW, series, 1.0 - a2)
    rm = lax.rsqrt(om)
    return r, i, a, om * rm, rm, a2, sp


def _gelu(x):
    t = jnp.tanh(GELU_K * (x + GELU_C * x * x * x))
    return 0.5 * x * (1.0 + t), t


def _scan_pair(af_ref, uf_ref, ab_ref, ub_ref, hf_ref, hb_ref, pf_ref, pb_ref, L):
    ls = L // N_SEG
    zero = jnp.zeros((N_SEG, LRU_BW), F32)
    one = jnp.ones((N_SEG, LRU_BW), F32)
    tile = lambda t: pl.ds(pl.multiple_of(t * N_SEG, N_SEG), N_SEG)

    def steps(tc, carry):
        hf, pf, hb, pb = carry
        for q in range(SCAN_UNROLL):
            t = tc * SCAN_UNROLL + q
            rf, rb = tile(t), tile(ls - 1 - t)
            af = af_ref[rf, :]
            hf = af * hf + uf_ref[rf, :]
            pf = pf * af
            hf_ref[rf, :] = hf
            pf_ref[rf, :] = pf
            ab = ab_ref[rb, :]
            hb = ab * hb + ub_ref[rb, :]
            pb = pb * ab
            hb_ref[rb, :] = hb
            pb_ref[rb, :] = pb
        return hf, pf, hb, pb

    hf_e, pf_e, hb_e, pb_e = lax.fori_loop(0, ls // SCAN_UNROLL, steps, (zero, one, zero, one))

    rows, cin = [], jnp.zeros((1, LRU_BW), F32)
    for s in range(N_SEG):
        rows.append(cin)
        cin = hf_e[s:s + 1] + pf_e[s:s + 1] * cin
    cf = jnp.concatenate(rows, axis=0)
    rows, cin = [], jnp.zeros((1, LRU_BW), F32)
    for s in reversed(range(N_SEG)):
        rows.append(cin)
        cin = hb_e[s:s + 1] + pb_e[s:s + 1] * cin
    cb = jnp.concatenate(rows[::-1], axis=0)

    def fix(tc, _):
        for q in range(SCAN_UNROLL):
            r = tile(tc * SCAN_UNROLL + q)
            hf_ref[r, :] = hf_ref[r, :] + pf_ref[r, :] * cf
            hb_ref[r, :] = hb_ref[r, :] + pb_ref[r, :] * cb
        return 0

    lax.fori_loop(0, ls // SCAN_UNROLL, fix, 0)


def _rg_specs(L, D, nblk):
    slab = lambda off: pl.BlockSpec((L, LRU_BW), lambda cb, b: (b, off + cb))
    wspec = pl.BlockSpec((2, None, LRU_BW, LRU_BW), lambda cb, b: (0, cb, 0, 0))
    pvspec = pl.BlockSpec((PV_ROWS, LRU_BW), lambda cb, b: (0, cb))
    return slab, wspec, pvspec


def _rg_fwd(z, pvec, wa, wx, L, name):
    T, C2 = z.shape
    C = C2 // 2
    nblk = C // LRU_BW
    B = T // L
    slab, wspec, pvspec = _rg_specs(L, C, nblk)

    def body(gp_ref, rec_ref, pv_ref, wa_ref, wx_ref, yg_ref, hf_ref, hb_ref, a_scr, u_scr, p_scr):
        pv = pv_ref[...]
        c = _conv_taps(rec_ref[...], pv)
        cbf = c.astype(BF16)
        for d in range(2):
            _, i, a, m, _, _, _ = _rg_gates(c, cbf, wa_ref[d], wx_ref[d], pv[PV_B_A + d:PV_B_A + d + 1],
                                      pv[PV_B_X + d:PV_B_X + d + 1], pv[PV_LAM + d:PV_LAM + d + 1])
            a_scr[d] = a
            u_scr[d] = m * (i * c)
        _scan_pair(a_scr.at[0], u_scr.at[0], a_scr.at[1], u_scr.at[1], hf_ref, hb_ref, p_scr.at[0], p_scr.at[1], L)
        gate, _ = _gelu(gp_ref[...])
        yg_ref[...] = ((hf_ref[...] + hb_ref[...]) * gate).astype(BF16)

    return pl.pallas_call(
        body, name=name, grid=(nblk, B),
        in_specs=[slab(0), slab(nblk), pvspec, wspec, wspec],
        out_specs=[slab(0), slab(0), slab(0)],
        out_shape=[S((T, C), BF16), S((T, C), F32), S((T, C), F32)],
        scratch_shapes=[pltpu.VMEM((2, L, LRU_BW), F32)] * 3,
        compiler_params=_cp("parallel", "parallel"),
    )(z, z, pvec, wa, wx)


def _rg_bwd(z, hf, hb, dyg, pvec, wa, wx, L, name):
    T, C2 = z.shape
    C = C2 // 2
    nblk = C // LRU_BW
    B = T // L
    slab, wspec, pvspec = _rg_specs(L, C, nblk)

    def body(gp_ref, rec_ref, hf_ref, hb_ref, dyg_ref, pv_ref, wa_ref, wx_ref,
             dz_ref, dwa_ref, dwx_ref, dpv_ref, a_scr, u_scr, d_scr, p_scr):
        @pl.when(pl.program_id(1) == 0)
        def _():
            dwa_ref[...] = jnp.zeros_like(dwa_ref)
            dwx_ref[...] = jnp.zeros_like(dwx_ref)
            dpv_ref[...] = jnp.zeros_like(dpv_ref)

        pv = pv_ref[...]
        rec = rec_ref[...]
        c = _conv_taps(rec, pv)
        cbf = c.astype(BF16)
        gp = gp_ref[...]
        gate, th = _gelu(gp)
        dgelu = 0.5 * (1.0 + th) + 0.5 * gp * (1.0 - th * th) * GELU_K * (1.0 + 3.0 * GELU_C * gp * gp)
        dyg = dyg_ref[...]
        dz_ref[0] = (dyg * (hf_ref[...] + hb_ref[...]) * dgelu).astype(BF16)
        dy = dyg * gate

        gates = []
        for d in range(2):
            gates.append(_rg_gates(c, cbf, wa_ref[d], wx_ref[d], pv[PV_B_A + d:PV_B_A + d + 1],
                                   pv[PV_B_X + d:PV_B_X + d + 1], pv[PV_LAM + d:PV_LAM + d + 1]))
        a_scr[0] = _shift_rows(gates[1][2], 1)
        a_scr[1] = _shift_rows(gates[0][2], -1)
        u_scr[...] = dy
        _scan_pair(a_scr.at[0], u_scr, a_scr.at[1], u_scr, d_scr.at[1], d_scr.at[0], p_scr.at[0], p_scr.at[1], L)

        dc = jnp.zeros_like(c)
        rows = []
        for d in range(2):
            r, i, a, m, rm, a2, sp = gates[d]
            delta = d_scr[d]
            hnb = _shift_rows(hf_ref[...], 1) if d == 0 else _shift_rows(hb_ref[...], -1)
            da = delta * hnb
            dm = delta * (i * c)
            di = delta * (m * c)
            dc = dc + delta * (m * i)
            dla = da * a - dm * (a2 * rm)
            dpa = (dla * ((-RG_C) * sp)) * (r * (1.0 - r))
            dpx = di * (i * (1.0 - i))
            dsp = (-RG_C) * jnp.sum(dla * r, axis=0, keepdims=True)
            lam = pv[PV_LAM + d:PV_LAM + d + 1]
            rows.append((jnp.sum(dpa, axis=0, keepdims=True), jnp.sum(dpx, axis=0, keepdims=True),
                         -dsp * _sigmoid(-lam)))
            dpab = dpa.astype(BF16)
            dpxb = dpx.astype(BF16)
            dwa_ref[d] += _dot_tn(cbf, dpab)
            dwx_ref[d] += _dot_tn(cbf, dpxb)
            dc = dc + _dot_nt(dpab, wa_ref[d]) + _dot_nt(dpxb, wx_ref[d])

        drec = jnp.zeros_like(c)
        dcw = []
        for j in range(CONV_W):
            drec = drec + pv[PV_CONV_W + j:PV_CONV_W + j + 1] * _shift_rows(dc, j - 2)
            dcw.append(jnp.sum(dc * _shift_rows(rec, 2 - j), axis=0, keepdims=True))
        dz_ref[1] = drec.astype(BF16)
        for j in range(CONV_W):
            dpv_ref[PV_CONV_W + j:PV_CONV_W + j + 1, :] += dcw[j]
        for d in range(2):
            dpv_ref[PV_B_A + d:PV_B_A + d + 1, :] += rows[d][0]
            dpv_ref[PV_B_X + d:PV_B_X + d + 1, :] += rows[d][1]
            dpv_ref[PV_LAM + d:PV_LAM + d + 1, :] += rows[d][2]
        dpv_ref[PV_CONV_B:PV_CONV_B + 1, :] += jnp.sum(dc, axis=0, keepdims=True)

    return pl.pallas_call(
        body, name=name, grid=(nblk, B),
        in_specs=[slab(0), slab(nblk), slab(0), slab(0), slab(0), pvspec, wspec, wspec],
        out_specs=[pl.BlockSpec((2, L, LRU_BW), lambda cb, b: (0, b, cb)), wspec, wspec, pvspec],
        out_shape=[S((2, T, C), BF16), S((2, nblk, LRU_BW, LRU_BW), F32), S((2, nblk, LRU_BW, LRU_BW), F32),
                   S((PV_ROWS, C), F32)],
        scratch_shapes=[pltpu.VMEM((2, L, LRU_BW), F32), pltpu.VMEM((L, LRU_BW), F32),
                        pltpu.VMEM((2, L, LRU_BW), F32), pltpu.VMEM((2, L, LRU_BW), F32)],
        compiler_params=_cp("parallel", "arbitrary"),
    )(z, z, hf, hb, dyg, pvec, wa, wx)


QKV_NB = 512


def _interleave(a):
    *lead, L, D = a.shape
    return a.reshape(*lead, N_SEG, L // N_SEG, D).swapaxes(-3, -2).reshape(*lead, L, D)


def _deinterleave(a):
    *lead, L, D = a.shape
    return a.reshape(*lead, L // N_SEG, N_SEG, D).swapaxes(-3, -2).reshape(*lead, L, D)


def _local_step(x3, tgt3, w, fetch, send, early_adam):
    Bl, L, D = x3.shape
    T = Bl * L
    x = _interleave(x3).reshape(T, D)
    tgt = _interleave(tgt3).reshape(T, D)
    gm = [w["g_mix"][i:i + 1] for i in range(2)]
    gl = [w["g_mlp"][i:i + 1] for i in range(2)]

    w0 = fetch(0, ())
    nb_in = w0["w_in"].shape[-1]
    z, h0 = _norm_matmul(x, gm[0], w0["w_in"], "rg_in")
    yg, hf, hb = _rg_fwd(z, w0["pvec"], w["wa"], w["wx"], L, "rg_fwd")
    w1 = fetch(1, (yg,))
    x1 = _matmul_res(yg, w1["w_out"], x, "rg_out")
    w1.update(fetch(4, (x1,)))
    fb = w1["w_up0"].shape[-1]
    x2, a0, hm0 = _mlp_fwd(x1, gl[0], w1["w_up0"], w1["w_down0"], "mlp0_fwd")
    w2 = fetch(2, (x2,))
    cos, sin = [_interleave(t) for t in _rope_tables(L)]
    qkv, h1, qn, kn, vb = _qkv_proj(x2, gm[1], w2["w_qkv"], w["qg"], w["kg"], cos, sin, L, "at_qkv")
    o = _attn_fwd(qn, kn, vb, L, "at_fwd")
    x3_ = _matmul_res(o, w2["w_o"], x2, "at_out")
    w3 = fetch(3, (x3_,))
    dx4, a1, hm1, loss, dgf = _mlp_fwd(x3_, gl[1], w3["w_up1"], w3["w_down1"], "mlp1_fwd", head=(tgt, w["g_fin"]))

    dx3, da1, dob1, dgl1 = _mlp_bwd_dx(x3_, dx4, a1, gl[1], w3["w_up1"], w3["w_down1"], "mlp1_bwd_dx")
    dwu1, dwd1, _ = _mlp_bwd_dw(hm1, da1, a1, dob1, fb, "mlp1_bwd_dw")
    sent = send(3, dict(w_up1=dwu1, w_down1=dwd1))
    do, dx3b = _matmul_nt(dx3, w2["w_o"], "at_out_bwd", BF16, after=sent)
    dwo = _matmul_tn(o, dx3b[None], QKV_NB, "at_out_dw", blocked=False)
    dq, dk, dv = _attn_bwd(qn, kn, vb, do, o, L, "at_bwd")
    dqkv, dqg, dkg = _qk_prep_bwd(qkv, dq, dk, dv, w["qg"], w["kg"], cos, sin, L, "at_prep_bwd")
    dwqkv = _matmul_tn(dqkv, h1[None], QKV_NB, "at_qkv_dw", blocked=False)
    sent = send(2, dict(w_qkv=dwqkv, w_o=dwo))
    dx2, dgm1 = _nt_normbwd(dqkv[None], w2["w_qkv"], x2, gm[1], dx3, "at_qkv_bwd", after=sent, transposed=True)
    dx1, da0, dob0, dgl0 = _mlp_bwd_dx(x1, dx2, a0, gl[0], w1["w_up0"], w1["w_down0"], "mlp0_bwd_dx")
    jobs, sink = early_adam((dob0,))
    dwu0, dwd0, early = _mlp_bwd_dw(hm0, da0, a0, dob0, fb, "mlp0_bwd_dw", jobs)
    sink(early)
    sent = send(1, dict(w_up0=dwu0, w_down0=dwd0))
    dyg, dx1b = _matmul_nt(dx1, w1["w_out"], "rg_out_bwd", F32, after=sent)
    dwout = _matmul_tn(yg, dx1b[None], QKV_NB, "rg_out_dw", blocked=False)
    dz, dwa, dwx, dpv = _rg_bwd(z, hf, hb, dyg, w0["pvec"], w["wa"], w["wx"], L, "rg_bwd")
    sent = send(4, dict(w_out=dwout, pvec=dpv, wa=dwa, wx=dwx))
    dwin = _matmul_tn(h0, dz, nb_in, "rg_in_dw", blocked=True, after=sent)
    sent = send(0, dict(w_in=dwin))
    dx0, dgm0 = _nt_normbwd(dz, w0["w_in"], x, gm[0], dx1, "rg_in_bwd", after=sent)
    send(-1, dict(g_mix=[dgm0, dgm1], g_mlp=[dgl0, dgl1], g_fin=dgf, conv_b=dpv[PV_CONV_B:PV_CONV_B + 1], qg=dqg, kg=dkg, loss=loss))
    return _deinterleave(dx0.reshape(Bl, L, D))


MESH = pl.DeviceIdType.MESH
ANY = pl.BlockSpec(memory_space=pl.ANY)
N_PEERS = N_DEV - 1


def _my_place():
    return lax.axis_index("x"), lax.axis_index("y"), lax.axis_index("c")


def _flat(px, py, pc):
    return 4 * px + 2 * py + pc


def _all_gather(shards, name):
    n = len(shards)

    def body(*refs):
        ins, outs = refs[:n], refs[n:2 * n]
        send_sems, recv_sems, local_sems = refs[2 * n:]
        x, y, c = _my_place()
        me, sibling = (x, y, c), (x, y, 1 - c)
        chips = [(1 - x, y), (x, 1 - y), (1 - x, 1 - y)]

        def copy(a, k, block, to, src=None):
            dst = outs[a].at[_flat(*block)]
            return pltpu.make_async_remote_copy(
                src_ref=dst if src is None else src, dst_ref=dst,
                send_sem=send_sems.at[a, k], recv_sem=recv_sems.at[a, k],
                device_id=to, device_id_type=MESH)

        mine = [pltpu.make_async_copy(ins[a], outs[a].at[_flat(*me)], local_sems.at[a]) for a in range(n)]
        for cp in mine:
            cp.start()
        first = []
        for a in range(n):
            first.append(copy(a, 0, me, sibling, src=ins[a]))
            first += [copy(a, 1 + j, me, (*chip, c), src=ins[a]) for j, chip in enumerate(chips)]
        for cp in first:
            cp.start()
        passed = []
        for j, chip in enumerate(chips):
            for a in range(n):
                copy(a, 1 + j, (*chip, c), me).wait_recv()
                fwd = copy(a, 4 + j, (*chip, c), sibling)
                fwd.start()
                passed.append(fwd)
        for a in range(n):
            copy(a, 0, sibling, me).wait_recv()
            for j, chip in enumerate(chips):
                copy(a, 4 + j, (*chip, 1 - c), me).wait_recv()
        for cp in first + passed:
            cp.wait_send()
        for cp in mine:
            cp.wait()

    return pl.pallas_call(
        body, name=name,
        in_specs=[ANY] * n, out_specs=[ANY] * n,
        out_shape=[S((N_DEV,) + s.shape, s.dtype) for s in shards],
        scratch_shapes=[pltpu.SemaphoreType.DMA((n, N_PEERS)), pltpu.SemaphoreType.DMA((n, N_PEERS)),
                        pltpu.SemaphoreType.DMA((n,))],
    )(*shards)


HBM = pl.BlockSpec(memory_space=pltpu.HBM)
SEM = pl.BlockSpec(memory_space=pltpu.SEMAPHORE)
SIDE_EFFECT = pltpu.SideEffectType.DATAFLOW_SIDE_EFFECTING
SEMS_PER_GROUP = 3


NEAR_PEERS = (1, 2, 4, 6)
FAR_CHIPS = (2, 4, 6)


def _exchange_copies(srcs, lands, sems, mode):
    send_sems, recv_sems, local_sems = sems
    scatter = mode == "scatter"
    x, y, c = _my_place()
    me = _flat(x, y, c)
    remote, local = [], []
    for a in range(len(srcs)):
        for r in (NEAR_PEERS if mode == "near" else range(1, N_DEV)):
            peer = (1 - x if r & 4 else x, 1 - y if r & 2 else y, 1 - c if r & 1 else c)
            remote.append(pltpu.make_async_remote_copy(
                src_ref=srcs[a].at[_flat(*peer)] if scatter else srcs[a], dst_ref=lands[a].at[me],
                send_sem=send_sems.at[a * N_PEERS + r - 1], recv_sem=recv_sems.at[a * N_PEERS + r - 1],
                device_id=peer, device_id_type=MESH))
        local.append(pltpu.make_async_copy(srcs[a].at[me] if scatter else srcs[a], lands[a].at[me], local_sems.at[a]))
    return remote, local


def _exchange_start(groups, modes, name):
    sizes = [len(g) for g in groups]
    srcs = [pltpu.with_memory_space_constraint(a, pltpu.HBM) for g in groups for a in g]
    n = len(srcs)
    scatter_of = [m == "scatter" for g, m in zip(groups, modes) for _ in g]
    lands = [pltpu.with_memory_space_constraint(lax.empty(a.shape if sc else (N_DEV,) + a.shape, a.dtype), pltpu.HBM)
             for a, sc in zip(srcs, scatter_of)]
    n_sem = SEMS_PER_GROUP * len(groups)

    def body(*refs):
        src_refs, land_refs, sem_refs, token = refs[:n], refs[n:2 * n], refs[2 * n:2 * n + n_sem], refs[-1]
        off = 0
        for gi, k in enumerate(sizes):
            remote, local = _exchange_copies(src_refs[off:off + k], land_refs[off:off + k],
                                             sem_refs[SEMS_PER_GROUP * gi:SEMS_PER_GROUP * (gi + 1)], modes[gi])
            for cp in local + remote:
                cp.start()
            off += k
        token[...] = jnp.zeros_like(token)

    sem_shapes = []
    for k in sizes:
        sem_shapes += [pltpu.SemaphoreType.DMA((k * N_PEERS,)), pltpu.SemaphoreType.DMA((k * N_PEERS,)),
                       pltpu.SemaphoreType.DMA((k,))]
    outs = pl.pallas_call(
        body, name=name,
        out_shape=sem_shapes + [pltpu.HBM(a.shape, a.dtype) for a in srcs + lands] + [S((8, 128), F32)],
        in_specs=[HBM] * (2 * n),
        out_specs=[SEM] * n_sem + [HBM] * (2 * n) + [pl.BlockSpec(memory_space=pltpu.VMEM)],
        input_output_aliases={i: n_sem + i for i in range(2 * n)},
        compiler_params=pltpu.CompilerParams(has_side_effects=SIDE_EFFECT),
    )(*srcs, *lands)
    sems, thru, token = outs[:n_sem], outs[n_sem:n_sem + 2 * n], outs[-1]
    per_group, off = [], 0
    for gi, k in enumerate(sizes):
        per_group.append((sems[SEMS_PER_GROUP * gi:SEMS_PER_GROUP * (gi + 1)], thru[off:off + k], thru[n + off:n + off + k]))
        off += k
    return per_group, token


def _exchange_wait(group, after, mode, name):
    sems, srcs, lands = group
    k = len(srcs)

    def body(*refs):
        remote, local = _exchange_copies(refs[:k], refs[k:2 * k], refs[2 * k:2 * k + SEMS_PER_GROUP], mode)
        for cp in remote:
            cp.wait_send()
            cp.wait_recv()
        for cp in local:
            cp.wait()

    outs = pl.pallas_call(
        body, name=name,
        out_shape=[pltpu.HBM(a.shape, a.dtype) for a in list(srcs) + list(lands)],
        in_specs=[HBM] * (2 * k) + [SEM] * SEMS_PER_GROUP + [ANY] * len(after),
        out_specs=[HBM] * (2 * k),
        input_output_aliases={i: i for i in range(2 * k)},
        compiler_params=pltpu.CompilerParams(has_side_effects=SIDE_EFFECT),
    )(*srcs, *lands, *sems, *after)
    return outs[k:]


def _forward_copies(lands, sems):
    send_sems, recv_sems = sems
    x, y, c = _my_place()
    mine, theirs = [], []
    for a in range(len(lands)):
        for k, r in enumerate(FAR_CHIPS):
            px, py = (1 - x if r & 4 else x), (1 - y if r & 2 else y)
            for out, core in ((mine, c), (theirs, 1 - c)):
                blk = lands[a].at[_flat(px, py, core)]
                out.append(pltpu.make_async_remote_copy(
                    src_ref=blk, dst_ref=blk, send_sem=send_sems.at[a * len(FAR_CHIPS) + k],
                    recv_sem=recv_sems.at[a * len(FAR_CHIPS) + k], device_id=(x, y, 1 - c), device_id_type=MESH))
    return mine, theirs


def _forward_start(groups, name):
    sizes = [len(g) for g in groups]
    lands = [a for g in groups for a in g]
    n = len(lands)
    n_sem = 2 * len(groups)

    def body(*refs):
        land_refs, sem_refs, token = refs[:n], refs[n:n + n_sem], refs[-1]
        off = 0
        for gi, k in enumerate(sizes):
            mine, _ = _forward_copies(land_refs[off:off + k], sem_refs[2 * gi:2 * gi + 2])
            for cp in mine:
                cp.start()
            off += k
        token[...] = jnp.zeros_like(token)

    sem_shapes = []
    for k in sizes:
        sem_shapes += [pltpu.SemaphoreType.DMA((k * len(FAR_CHIPS),))] * 2
    outs = pl.pallas_call(
        body, name=name,
        out_shape=sem_shapes + [pltpu.HBM(a.shape, a.dtype) for a in lands] + [S((8, 128), F32)],
        in_specs=[HBM] * n,
        out_specs=[SEM] * n_sem + [HBM] * n + [pl.BlockSpec(memory_space=pltpu.VMEM)],
        input_output_aliases={i: n_sem + i for i in range(n)},
        compiler_params=pltpu.CompilerParams(has_side_effects=SIDE_EFFECT),
    )(*lands)
    per_group, off = [], 0
    for gi, k in enumerate(sizes):
        per_group.append((outs[2 * gi:2 * gi + 2], outs[n_sem + off:n_sem + off + k]))
        off += k
    return per_group


def _forward_wait(group, after, name):
    sems, lands = group
    k = len(lands)

    def body(*refs):
        mine, theirs = _forward_copies(refs[:k], refs[k:k + 2])
        for cp in mine:
            cp.wait_send()
        for cp in theirs:
            cp.wait_recv()

    return pl.pallas_call(
        body, name=name,
        out_shape=[pltpu.HBM(a.shape, a.dtype) for a in lands],
        in_specs=[HBM] * k + [SEM] * 2 + [ANY] * len(after),
        out_specs=[HBM] * k,
        input_output_aliases={i: i for i in range(k)},
        compiler_params=pltpu.CompilerParams(has_side_effects=SIDE_EFFECT),
    )(*lands, *sems, *after)


def _row_tile(rows, cols):
    want = max(16, (128 * 1024) // cols)
    if rows <= want:
        return rows
    t = want - want % 16
    while rows % t:
        t -= 16
    return t


def _sum_parts(parts, name, after=()):
    P, R, C = parts.shape
    tr = _row_tile(R, C)

    def body(p_ref, *rest):
        o_ref = rest[-1]
        g = p_ref[0].astype(F32)
        for i in range(1, P):
            g = g + p_ref[i].astype(F32)
        o_ref[...] = g

    return pl.pallas_call(
        body, name=name, grid=(R // tr,),
        in_specs=[pl.BlockSpec((P, tr, C), lambda i: (0, i, 0))] + [ANY] * len(after),
        out_specs=pl.BlockSpec((tr, C), lambda i: (i, 0)),
        out_shape=S((R, C), F32),
        compiler_params=_cp("parallel"),
    )(parts, *after)


def _adamw(parts, w, m, v, name, after=()):
    P, R, C = parts.shape
    tr = _row_tile(R, C)
    c1 = 1.0 - ADAM_B1 ** ADAM_STEP
    c2 = 1.0 - ADAM_B2 ** ADAM_STEP

    def body(p_ref, w_ref, m_ref, v_ref, *rest):
        g_ref, d_ref, mo_ref, vo_ref = rest[len(after):]
        g = p_ref[0].astype(F32)
        for i in range(1, P):
            g = g + p_ref[i].astype(F32)
        mn = ADAM_B1 * m_ref[...] + (1.0 - ADAM_B1) * g
        vn = ADAM_B2 * v_ref[...] + (1.0 - ADAM_B2) * (g * g)
        g_ref[...] = g
        mo_ref[...] = mn
        vo_ref[...] = vn
        d_ref[...] = (-ADAM_LR) * ((mn / c1) / (jnp.sqrt(vn / c2) + ADAM_EPS) + ADAM_WD * w_ref[...])

    blk = pl.BlockSpec((tr, C), lambda i: (i, 0))
    return pl.pallas_call(
        body, name=name, grid=(R // tr,),
        in_specs=[pl.BlockSpec((P, tr, C), lambda i: (0, i, 0)), blk, blk, blk] + [ANY] * len(after),
        out_specs=[blk, blk, blk, blk],
        out_shape=[S((R, C), F32)] * 4,
        compiler_params=_cp("parallel"),
    )(parts, w, m, v, *after)


def _adamw_layer(parts, w3, m3, v3, layer, prev, name, after=()):
    P, R, C = parts.shape
    NL = w3.shape[0]
    tr = _row_tile(R, C)
    c1 = 1.0 - ADAM_B1 ** ADAM_STEP
    c2 = 1.0 - ADAM_B2 ** ADAM_STEP
    n_prev = 0 if prev is None else len(prev)

    def body(p_ref, w_ref, m_ref, v_ref, *rest):
        g_ref, d_ref, mo_ref, vo_ref = rest[n_prev + len(after):]
        g = p_ref[0].astype(F32)
        for i in range(1, P):
            g = g + p_ref[i].astype(F32)
        mn = ADAM_B1 * m_ref[...] + (1.0 - ADAM_B1) * g
        vn = ADAM_B2 * v_ref[...] + (1.0 - ADAM_B2) * (g * g)
        g_ref[...] = g
        mo_ref[...] = mn
        vo_ref[...] = vn
        d_ref[...] = (-ADAM_LR) * ((mn / c1) / (jnp.sqrt(vn / c2) + ADAM_EPS) + ADAM_WD * w_ref[...])

    blk = pl.BlockSpec((None, tr, C), lambda i: (layer, i, 0))
    return pl.pallas_call(
        body, name=name, grid=(R // tr,),
        in_specs=[pl.BlockSpec((P, tr, C), lambda i: (0, i, 0)), blk, blk, blk] + [ANY] * (n_prev + len(after)),
        out_specs=[blk, blk, blk, blk],
        out_shape=[S((NL, R, C), F32)] * 4,
        input_output_aliases={4 + k: k for k in range(n_prev)},
        compiler_params=_cp("parallel"),
    )(parts, w3, m3, v3, *(prev or ()), *after)


VMEM_WHOLE = pl.BlockSpec(memory_space=pltpu.VMEM)
LANES = 128


def _pack_vectors(vectors, starts, rows, name):
    def body(*refs):
        o_ref = refs[-1]
        o_ref[...] = jnp.zeros_like(o_ref)
        for v_ref, r0 in zip(refs[:-1], starts):
            for j in range(v_ref.shape[1] // LANES):
                o_ref[r0 + j:r0 + j + 1, :] = v_ref[:, j * LANES:(j + 1) * LANES]

    return pl.pallas_call(body, name=name, in_specs=[VMEM_WHOLE] * len(vectors), out_specs=VMEM_WHOLE,
                          out_shape=S((rows, LANES), F32))(*vectors)


def _adamw_vectors(g_pack, params, keep_rows, name, after=()):
    n = len(params)
    P = g_pack.shape[0]
    c1 = 1.0 - ADAM_B1 ** ADAM_STEP
    c2 = 1.0 - ADAM_B2 ** ADAM_STEP

    def body(g_ref, *refs):
        ins, outs = refs[:3 * n], refs[3 * n + len(after):]
        gs = g_ref[0]
        for i in range(1, P):
            gs = gs + g_ref[i]
        for pi, (_, _, _, slots) in enumerate(params):
            w_ref, m_ref, v_ref = ins[3 * pi:3 * pi + 3]
            g_out, d_out, m_out, v_out = outs[4 * pi:4 * pi + 4]
            for idx, row in slots:
                g = gs[row:row + 1, :]
                mn = ADAM_B1 * m_ref[idx] + (1.0 - ADAM_B1) * g
                vn = ADAM_B2 * v_ref[idx] + (1.0 - ADAM_B2) * (g * g)
                g_out[idx] = g
                m_out[idx] = mn
                v_out[idx] = vn
                d_out[idx] = (-ADAM_LR) * ((mn / c1) / (jnp.sqrt(vn / c2) + ADAM_EPS) + ADAM_WD * w_ref[idx])
        outs[-1][...] = jnp.concatenate([gs[r:r + 1, :] for r in keep_rows], axis=0)

    flat = [a for w, m, v, _ in params for a in (w, m, v)]
    out_shape = [S(w.shape, F32) for w, _, _, _ in params for _ in range(4)] + [S((len(keep_rows), LANES), F32)]
    outs = pl.pallas_call(
        body, name=name,
        in_specs=[VMEM_WHOLE] * (1 + len(flat)) + [ANY] * len(after),
        out_specs=[VMEM_WHOLE] * len(out_shape), out_shape=out_shape,
    )(g_pack, *flat, *after)
    return [outs[4 * i:4 * i + 4] for i in range(n)], outs[-1]


def _adamw_nd(parts, w, m, v, name, after=()):
    shp = w.shape
    C = shp[-1]
    outs = _adamw(parts.reshape(parts.shape[0], -1, C), w.reshape(-1, C), m.reshape(-1, C), v.reshape(-1, C), name, after)
    return [o.reshape(shp) for o in outs]


TILE_ROWS = 8


REP_SMALL_ROWS = 128
REP_GRAD_STARTS = (0, 8, 16, 24, 32, 40, 48, 56, 64)
REP_SMALL_STARTS = (0, 16, 32, 40, 48, 56)
REP_LOSS_ROW = 64


def _small_pack(cw, ba, bx, lam):
    pad8 = lambda a: jnp.pad(a, ((0, TILE_ROWS - a.shape[0]), (0, 0)))
    return jnp.concatenate([pad8(cw[0, :, 0, :]), pad8(ba[0]), pad8(bx[0]), pad8(lam[0]),
                            jnp.zeros((PV_ROWS - PV_CONV_B, LRU_BW), F32)], axis=0)


def kernel(x, norm_mix_g, norm_mlp_g, rg_w_in, rg_conv_w, rg_conv_b, rg_w_a, rg_b_a, rg_w_x, rg_b_x, rg_lam, rg_w_out, at_w_qkv, at_q_g, at_k_g, at_w_o, mlp_w_up, mlp_w_down, final_g, loss_target, m_norm_mix_g, m_norm_mlp_g, m_rg_w_in, m_rg_conv_w, m_rg_conv_b, m_rg_w_a, m_rg_b_a, m_rg_w_x, m_rg_b_x, m_rg_lam, m_rg_w_out, m_at_w_qkv, m_at_q_g, m_at_k_g, m_at_w_o, m_mlp_w_up, m_mlp_w_down, m_final_g, v_norm_mix_g, v_norm_mlp_g, v_rg_w_in, v_rg_conv_w, v_rg_conv_b, v_rg_w_a, v_rg_b_a, v_rg_w_x, v_rg_b_x, v_rg_lam, v_rg_w_out, v_at_w_qkv, v_at_q_g, v_at_k_g, v_at_w_o, v_mlp_w_up, v_mlp_w_down, v_final_g):
    D = x.shape[-1]
    bf = lambda a: a.astype(BF16)

    sp_w = _small_pack(rg_conv_w, rg_b_a, rg_b_x, rg_lam)
    started, _ = _exchange_start(
        [[bf(rg_w_in[0]), sp_w], [bf(rg_w_out[0])], [bf(mlp_w_up[0]), bf(mlp_w_down[0])],
         [bf(at_w_qkv[0]).T, bf(at_w_o[0])], [bf(mlp_w_up[1]), bf(mlp_w_down[1])]],
        ["near", "near", "near", "gather", "gather"], "gather_start")
    gathers = dict(zip((0, 1, 4, 2, 3), started))
    forwards = {}

    def fetch(stage, after):
        after = tuple(after)
        if stage == 0:
            got = _exchange_wait(gathers[0], after, "near", "gather_wait0")
            g_in, g_sp = _forward_wait(_forward_start([got], "forward_start0")[0], (), "forward_wait0")
            pvec = g_sp.transpose(1, 0, 2).reshape(PV_ROWS, D)
            pvec = jnp.concatenate([pvec[:PV_CONV_B], jnp.broadcast_to(rg_conv_b, (PV_ROWS - PV_CONV_B, D))], axis=0)
            return dict(w_in=g_in, pvec=pvec)
        if stage == 1:
            near = [_exchange_wait(gathers[s], after, "near", "gather_wait%d" % s) for s in (1, 4)]
            f_out, forwards[4] = _forward_start(near, "forward_start1")
            g_out, = _forward_wait(f_out, (), "forward_wait1")
            return dict(w_out=g_out.reshape(D, D))
        if stage == 4:
            g_up0, g_dn0 = _forward_wait(forwards[4], after, "forward_wait4")
            return dict(w_up0=g_up0, w_down0=g_dn0.reshape(-1, D))
        got = _exchange_wait(gathers[stage], after, "gather", "gather_wait%d" % stage)
        if stage == 2:
            return dict(w_qkv=got[0].reshape(-1, QKV_NB, D), w_o=got[1].reshape(D, D))
        return dict(w_up1=got[0], w_down1=got[1].reshape(-1, D))

    scatters = {}

    def send(stage, g):
        if stage == 3:
            arrs = [g["w_up1"], g["w_down1"].reshape(N_DEV, -1, D)]
        elif stage == 2:
            arrs = [g["w_qkv"].reshape(N_DEV, -1, D), g["w_o"].reshape(N_DEV, -1, D)]
        elif stage == 1:
            arrs = [g["w_up0"], g["w_down0"].reshape(N_DEV, -1, D)]
        elif stage == 4:
            arrs = [g["w_out"].reshape(N_DEV, -1, D), g["pvec"].reshape(PV_ROWS, N_DEV, LRU_BW).transpose(1, 0, 2),
                    bf(g["wa"]).reshape(N_DEV, -1, LANES), bf(g["wx"]).reshape(N_DEV, -1, LANES)]
        elif stage == 0:
            arrs = [g["w_in"]]
        else:
            small = _pack_vectors(g["g_mix"] + g["g_mlp"] + [g["g_fin"], g["conv_b"], g["qg"], g["kg"], g["loss"]],
                                  REP_GRAD_STARTS, REP_SMALL_ROWS, "pack_rep_small")
            arrs = [small.reshape(N_DEV, -1, LANES)]
        (group,), token = _exchange_start([arrs], ["scatter"], "scatter_start%d" % (stage % 6))
        scatters[stage] = (group, token)
        return (token,)

    w = dict(g_mix=norm_mix_g, g_mlp=norm_mlp_g, g_fin=final_g[None], qg=at_q_g, kg=at_k_g,
             wa=bf(rg_w_a[0]), wx=bf(rg_w_x[0]))
    early = {}

    def early_adam(after):
        r_up1, r_dn1 = _exchange_wait(scatters[3][0], tuple(after), "scatter", "scatter_wait3")
        r_qkv, r_o = _exchange_wait(scatters[2][0], (r_up1,), "scatter", "scatter_wait2")
        jobs = [(r_up1, mlp_w_up, m_mlp_w_up, v_mlp_w_up, 1), (r_dn1, mlp_w_down, m_mlp_w_down, v_mlp_w_down, 1),
                (r_o, at_w_o, m_at_w_o, v_at_w_o, 0)]
        return jobs, lambda outs: early.update(up=outs[0], dn=outs[1], o=outs[2], r_qkv=r_qkv)

    grad_x = _local_step(x, loss_target, w, fetch, send, early_adam)

    res = {}
    up, dn = early["up"], early["dn"]
    r_out, r_sp, r_wa, r_wx = _exchange_wait(scatters[4][0], (scatters[-1][1],), "scatter", "scatter_wait4")
    wa_part = _sum_parts(r_wa, "reduce_w_a")
    wx_part = _sum_parts(r_wx, "reduce_w_x", (wa_part,))
    (rep_gather,), rep_token = _exchange_start([[wa_part, wx_part]], ["gather"], "rep_gather_start")
    tr = lambda a: a[0].T
    qkv_t = _adamw_nd(early["r_qkv"], tr(at_w_qkv), tr(m_at_w_qkv), tr(v_at_w_qkv), "adam_at_w_qkv", (rep_token,))
    res["at_w_qkv"] = [o.T[None] for o in qkv_t]
    res["at_w_o"] = early["o"]
    r_up0, r_dn0 = _exchange_wait(scatters[1][0], (qkv_t[1],), "scatter", "scatter_wait1")
    res["mlp_w_up"] = _adamw_layer(r_up0, mlp_w_up, m_mlp_w_up, v_mlp_w_up, 0, up, "adam_mlp_w_up0")
    res["mlp_w_down"] = _adamw_layer(r_dn0, mlp_w_down, m_mlp_w_down, v_mlp_w_down, 0, dn, "adam_mlp_w_down0",
                                     after=(res["mlp_w_up"][1],))
    r_in, = _exchange_wait(scatters[0][0], (res["mlp_w_down"][1],), "scatter", "scatter_wait0")
    res["rg_w_in"] = _adamw_nd(r_in[:, None], rg_w_in, m_rg_w_in, v_rg_w_in, "adam_rg_w_in")
    res["rg_w_out"] = _adamw_nd(r_out[:, None], rg_w_out, m_rg_w_out, v_rg_w_out, "adam_rg_w_out", (res["rg_w_in"][1],))
    whole, lane = slice(None), slice(0, 1)
    two_rows = lambda r0: [((0, slice(d, d + 1), whole), r0 + d) for d in range(2)]
    (res["rg_conv_w"], res["rg_b_a"], res["rg_b_x"], res["rg_lam"]), _ = _adamw_vectors(
        r_sp, [(rg_conv_w, m_rg_conv_w, v_rg_conv_w, [((0, t, lane, whole), PV_CONV_W + t) for t in range(CONV_W)]),
               (rg_b_a, m_rg_b_a, v_rg_b_a, two_rows(PV_B_A)), (rg_b_x, m_rg_b_x, v_rg_b_x, two_rows(PV_B_X)),
               (rg_lam, m_rg_lam, v_rg_lam, two_rows(PV_LAM))], [0], "adam_small", (res["rg_w_out"][1],))

    r_small, = _exchange_wait(scatters[-1][0], (res["rg_lam"][1],), "scatter", "scatter_wait5")
    small_sum, = _all_gather([_sum_parts(r_small, "reduce_rep_small")], "gather_replicated")
    wa_sum, wx_sum = _exchange_wait(rep_gather, (small_sum,), "gather", "rep_gather_wait")
    rows = lambda a: a.reshape(-1, LANES)
    wa_res = _adamw(wa_sum.reshape(1, -1, LANES), rows(rg_w_a), rows(m_rg_w_a), rows(v_rg_w_a), "adam_rg_w_a")
    wx_res = _adamw(wx_sum.reshape(1, -1, LANES), rows(rg_w_x), rows(m_rg_w_x), rows(v_rg_w_x), "adam_rg_w_x", (wa_res[1],))
    res["rg_w_a"] = [o.reshape(rg_w_a.shape) for o in wa_res]
    res["rg_w_x"] = [o.reshape(rg_w_x.shape) for o in wx_res]

    def vec_slots(a, r0):
        per = a.shape[1] // LANES
        return [((slice(l, l + 1), slice(LANES * j, LANES * (j + 1))), r0 + l * per + j)
                for l in range(a.shape[0]) for j in range(per)]

    fin = [final_g[None], m_final_g[None], v_final_g[None]]
    vecs = [(norm_mix_g, m_norm_mix_g, v_norm_mix_g), (norm_mlp_g, m_norm_mlp_g, v_norm_mlp_g), fin,
            (rg_conv_b, m_rg_conv_b, v_rg_conv_b), (at_q_g, m_at_q_g, v_at_q_g), (at_k_g, m_at_k_g, v_at_k_g)]
    outs, kept = _adamw_vectors(
        small_sum.reshape(1, -1, LANES),
        [(wv, mv, vv, vec_slots(wv, r0)) for (wv, mv, vv), r0 in zip(vecs, REP_SMALL_STARTS)], [REP_LOSS_ROW],
        "adam_rep_small", (wx_res[1],))
    for nm, o in zip(["norm_mix_g", "norm_mlp_g", "final_g", "rg_conv_b", "at_q_g", "at_k_g"], outs):
        res[nm] = [a[0] for a in o] if nm == "final_g" else o
    loss = kept[0, 0]

    order = ["norm_mix_g", "norm_mlp_g", "rg_w_in", "rg_conv_w", "rg_conv_b", "rg_w_a", "rg_b_a", "rg_w_x", "rg_b_x",
             "rg_lam", "rg_w_out", "at_w_qkv", "at_q_g", "at_k_g", "at_w_o", "mlp_w_up", "mlp_w_down", "final_g"]
    return (loss, grad_x, *[res[nm][k] for k in range(4) for nm in order])
```
